```python
import math
import jax, jax.numpy as jnp
from jax import lax
import numpy as np

D_MODEL = 1024
BATCH = 8
SEQ = 2048
DEPTH = 4

SSM_WIDTH = 512
SSM_GROUP = 16
SSM_GROUPS = SSM_WIDTH // SSM_GROUP
SSM_STATE = 64
DT_MIN = 1e-3
DT_MAX = 1e-1
HEAD_DIM = 128
HEADS_PER_GROUP = 4
DILATION_PATTERNS = ((128, 1), (512, 4), (2048, 16))
N_ATTN_GROUPS = len(DILATION_PATTERNS)
ATTN_HEADS = N_ATTN_GROUPS * HEADS_PER_GROUP
ATTN_QKV_WIDTH = ATTN_HEADS * HEAD_DIM
ATTN_WIDTH = HEADS_PER_GROUP * HEAD_DIM
IN_SPLITS = (SSM_WIDTH, SSM_WIDTH, ATTN_QKV_WIDTH, ATTN_QKV_WIDTH, ATTN_QKV_WIDTH,
             ATTN_WIDTH, D_MODEL, D_MODEL)
IN_COLS = sum(IN_SPLITS)
SPLIT_POINTS = tuple(int(s) for s in np.cumsum(IN_SPLITS)[:-1])
RMS_EPS = 1e-6

kernel_name = "hybrid_s5_dilated_attn_gated_block"


def rmsnorm(x, g):
    xf = x.astype(jnp.float32)
    inv = lax.rsqrt(jnp.mean(xf * xf, axis=-1, keepdims=True) + RMS_EPS)
    return (xf * inv * g.astype(jnp.float32)).astype(x.dtype)


def _ssm_combine(left, right):
    a_l, b_l = left
    a_r, b_r = right
    return a_r * a_l, a_r * b_l + b_r


def s5_branch(u, lam_re, lam_im, log_dt, b_re, b_im, c_re, c_im, d_skip, w_glu, b_glu):
    bsz, L, _ = u.shape
    uf = u.astype(jnp.float32)
    ug = uf.reshape(bsz, L, SSM_GROUPS, SSM_GROUP)
    lam = lax.complex(jnp.minimum(lam_re.astype(jnp.float32), -1e-4), lam_im.astype(jnp.float32))
    dt = jnp.exp(log_dt.astype(jnp.float32))[:, None]
    lam_bar = jnp.exp(lam * dt)
    b = lax.complex(b_re.astype(jnp.float32), b_im.astype(jnp.float32))
    b_bar = ((lam_bar - 1.0) / lam)[..., None] * b
    c = lax.complex(c_re.astype(jnp.float32), c_im.astype(jnp.float32))
    drive = jnp.einsum('gpc,blgc->blgp', b_bar, ug)
    decay = jnp.broadcast_to(lam_bar, drive.shape)
    _, states = lax.associative_scan(_ssm_combine, (decay, drive), axis=1)
    y = jnp.einsum('gcp,blgp->blgc', c, states).real.reshape(bsz, L, SSM_WIDTH)
    y = y + d_skip.astype(jnp.float32) * uf
    y = jax.nn.gelu(y)
    y = y * jax.nn.sigmoid(y @ w_glu.astype(jnp.float32) + b_glu.astype(jnp.float32))
    return y.astype(u.dtype)


def dilated_group_attention(q, k, v, window, dilation):
    bsz, L, hg, hd = q.shape
    span = window // dilation
    n = L // dilation
    nb = -(-n // span)
    pad = nb * span - n

    def to_sub(t):
        return t.reshape(bsz, n, dilation, hg, hd).transpose(0, 2, 3, 1, 4)

    qb = jnp.pad(to_sub(q), ((0, 0),) * 3 + ((0, pad), (0, 0))).reshape(bsz, dilation, hg, nb, span, hd)

    def kv_blocks(t):
        tp = jnp.pad(to_sub(t), ((0, 0),) * 3 + ((span, pad), (0, 0))).reshape(bsz, dilation, hg, nb + 1, span, hd)
        return jnp.concatenate([tp[:, :, :, :-1], tp[:, :, :, 1:]], axis=4)

    kb, vb = kv_blocks(k), kv_blocks(v)
    scores = jnp.einsum('brhnqd,brhnkd->brhnqk', qb, kb).astype(jnp.float32) * (hd ** -0.5)
    qi = jnp.arange(span)[:, None]
    ki = jnp.arange(2 * span)[None, :]
    blk = jnp.arange(nb)[:, None, None]
    dist = span + qi - ki
    valid = (dist >= 0) & (dist <= span) & (blk * span + ki - span >= 0)
    scores = jnp.where(valid, scores, -jnp.inf)
    lse = jax.nn.logsumexp(scores, axis=-1)
    p = jnp.exp(scores - lse[..., None])
    out = jnp.einsum('brhnqk,brhnkd->brhnqd', p.astype(v.dtype), vb)

    def from_sub(t):
        rest = t.shape[5:]
        t = t.reshape(bsz, dilation, hg, nb * span, *rest)[:, :, :, :n]
        t = jnp.moveaxis(t, 3, 1)
        return t.reshape(bsz, L, hg, *rest)

    return from_sub(out), from_sub(lse)


def dilated_attention_branch(q, k, v):
    bsz, L, _ = q.shape
    shp = (bsz, L, N_ATTN_GROUPS, HEADS_PER_GROUP, HEAD_DIM)
    q, k, v = q.reshape(shp), k.reshape(shp), v.reshape(shp)
    outs, lses = [], []
    for gi, (window, dilation) in enumerate(DILATION_PATTERNS):
        o, s = dilated_group_attention(q[:, :, gi], k[:, :, gi], v[:, :, gi], window, dilation)
        outs.append(o)
        lses.append(s)
    outs = jnp.stack(outs, axis=0)
    alpha = jax.nn.softmax(jnp.stack(lses, axis=0), axis=0)
    y = jnp.sum(alpha[..., None] * outs.astype(jnp.float32), axis=0)
    return y.reshape(bsz, L, ATTN_WIDTH).astype(q.dtype)


def _fwd_setup_inputs(seed: int = 0) -> dict:
    key = jax.random.key(seed)
    ks = jax.random.split(key, 20)
    f32 = jnp.float32
    nrm = lambda k, shape, s: jax.random.normal(k, shape, f32) * s
    x = jax.random.normal(ks[0], (BATCH, SEQ, D_MODEL), f32)
    pre_norm_g = 1.0 + nrm(ks[1], (DEPTH, D_MODEL), 0.02)
    w_in = nrm(ks[2], (DEPTH, D_MODEL, IN_COLS), D_MODEL ** -0.5)
    lambda_re = -0.5 + nrm(ks[3], (DEPTH, SSM_GROUPS, SSM_STATE), 0.01)
    lambda_im = (math.pi * jnp.arange(SSM_STATE, dtype=f32))[None, None, :] + nrm(ks[4], (DEPTH, SSM_GROUPS, SSM_STATE), 0.01)
    log_dt = jax.random.uniform(ks[5], (DEPTH, SSM_GROUPS), f32, math.log(DT_MIN), math.log(DT_MAX))
    b_scale = (2.0 * SSM_GROUP) ** -0.5
    b_re = nrm(ks[6], (DEPTH, SSM_GROUPS, SSM_STATE, SSM_GROUP), b_scale)
    b_im = nrm(ks[7], (DEPTH, SSM_GROUPS, SSM_STATE, SSM_GROUP), b_scale)
    c_scale = (2.0 * SSM_STATE) ** -0.5
    c_re = nrm(ks[8], (DEPTH, SSM_GROUPS, SSM_GROUP, SSM_STATE), c_scale)
    c_im = nrm(ks[9], (DEPTH, SSM_GROUPS, SSM_GROUP, SSM_STATE), c_scale)
    d_skip = nrm(ks[10], (DEPTH, SSM_WIDTH), 1.0)
    w_glu = nrm(ks[11], (DEPTH, SSM_WIDTH, SSM_WIDTH), SSM_WIDTH ** -0.5)
    b_glu = nrm(ks[12], (DEPTH, SSM_WIDTH), 0.01)
    w_branch_s = nrm(ks[13], (DEPTH, SSM_WIDTH, D_MODEL), SSM_WIDTH ** -0.5)
    w_branch_a = nrm(ks[14], (DEPTH, ATTN_WIDTH, D_MODEL), ATTN_WIDTH ** -0.5)
    w_out = nrm(ks[15], (DEPTH, D_MODEL, D_MODEL), D_MODEL ** -0.5)
    post_norm_g = 1.0 + nrm(ks[16], (DEPTH, D_MODEL), 0.02)
    return {"x": x, "pre_norm_g": pre_norm_g, "w_in": w_in, "lambda_re": lambda_re,
            "lambda_im": lambda_im, "log_dt": log_dt, "b_re": b_re, "b_im": b_im,
            "c_re": c_re, "c_im": c_im, "d_skip": d_skip, "w_glu": w_glu, "b_glu": b_glu,
            "w_branch_s": w_branch_s, "w_branch_a": w_branch_a, "w_out": w_out,
            "post_norm_g": post_norm_g}


def _fwd_reference(x, pre_norm_g, w_in, lambda_re, lambda_im, log_dt, b_re, b_im, c_re, c_im,
              d_skip, w_glu, b_glu, w_branch_s, w_branch_a, w_out, post_norm_g):
    for l in range(DEPTH):
        h = rmsnorm(x, pre_norm_g[l])
        proj = h @ w_in[l]
        u_s, z_s, q, k, v, z_a, g_s, g_a = jnp.split(proj, SPLIT_POINTS, axis=-1)
        y_s = s5_branch(u_s, lambda_re[l], lambda_im[l], log_dt[l], b_re[l], b_im[l],
                        c_re[l], c_im[l], d_skip[l], w_glu[l], b_glu[l]) * jax.nn.silu(z_s)
        y_a = dilated_attention_branch(q, k, v) * jax.nn.silu(z_a)
        merged = (jax.nn.sigmoid(g_s) * (y_s @ w_branch_s[l])
                  + jax.nn.sigmoid(g_a) * (y_a @ w_branch_a[l]))
        out = merged @ w_out[l]
        x = x + rmsnorm(out, post_norm_g[l]).astype(x.dtype)
    return x


import jax as _jax
import jax.numpy as _jnp

TWIN_FORMAT = 'train_step'
FWD_PARAMS = ['x', 'pre_norm_g', 'w_in', 'lambda_re', 'lambda_im', 'log_dt', 'b_re', 'b_im', 'c_re', 'c_im', 'd_skip', 'w_glu', 'b_glu', 'w_branch_s', 'w_branch_a', 'w_out', 'post_norm_g']
TWIN_WEIGHTS = ['pre_norm_g', 'w_in', 'lambda_re', 'lambda_im', 'log_dt', 'b_re', 'b_im', 'c_re', 'c_im', 'd_skip', 'w_glu', 'b_glu', 'w_branch_s', 'w_branch_a', 'w_out', 'post_norm_g']
TWIN_DIFF_INPUT = 'x'
TWIN_INPUTS = ['x', 'pre_norm_g', 'w_in', 'lambda_re', 'lambda_im', 'log_dt', 'b_re', 'b_im', 'c_re', 'c_im', 'd_skip', 'w_glu', 'b_glu', 'w_branch_s', 'w_branch_a', 'w_out', 'post_norm_g', 'loss_target', 'm_pre_norm_g', 'm_w_in', 'm_lambda_re', 'm_lambda_im', 'm_log_dt', 'm_b_re', 'm_b_im', 'm_c_re', 'm_c_im', 'm_d_skip', 'm_w_glu', 'm_b_glu', 'm_w_branch_s', 'm_w_branch_a', 'm_w_out', 'm_post_norm_g', 'v_pre_norm_g', 'v_w_in', 'v_lambda_re', 'v_lambda_im', 'v_log_dt', 'v_b_re', 'v_b_im', 'v_c_re', 'v_c_im', 'v_d_skip', 'v_w_glu', 'v_b_glu', 'v_w_branch_s', 'v_w_branch_a', 'v_w_out', 'v_post_norm_g']
TWIN_OUTPUTS = ['loss', 'grad_x', 'grad_pre_norm_g', 'grad_w_in', 'grad_lambda_re', 'grad_lambda_im', 'grad_log_dt', 'grad_b_re', 'grad_b_im', 'grad_c_re', 'grad_c_im', 'grad_d_skip', 'grad_w_glu', 'grad_b_glu', 'grad_w_branch_s', 'grad_w_branch_a', 'grad_w_out', 'grad_post_norm_g', 'delta_pre_norm_g', 'delta_w_in', 'delta_lambda_re', 'delta_lambda_im', 'delta_log_dt', 'delta_b_re', 'delta_b_im', 'delta_c_re', 'delta_c_im', 'delta_d_skip', 'delta_w_glu', 'delta_b_glu', 'delta_w_branch_s', 'delta_w_branch_a', 'delta_w_out', 'delta_post_norm_g', 'new_m_pre_norm_g', 'new_m_w_in', 'new_m_lambda_re', 'new_m_lambda_im', 'new_m_log_dt', 'new_m_b_re', 'new_m_b_im', 'new_m_c_re', 'new_m_c_im', 'new_m_d_skip', 'new_m_w_glu', 'new_m_b_glu', 'new_m_w_branch_s', 'new_m_w_branch_a', 'new_m_w_out', 'new_m_post_norm_g', 'new_v_pre_norm_g', 'new_v_w_in', 'new_v_lambda_re', 'new_v_lambda_im', 'new_v_log_dt', 'new_v_b_re', 'new_v_b_im', 'new_v_c_re', 'new_v_c_im', 'new_v_d_skip', 'new_v_w_glu', 'new_v_b_glu', 'new_v_w_branch_s', 'new_v_w_branch_a', 'new_v_w_out', 'new_v_post_norm_g']
TWIN_LEAF_KINDS = {'loss': 'loss', 'grad_x': 'grad_x', 'grad_pre_norm_g': 'grad_w', 'grad_w_in': 'grad_w', 'grad_lambda_re': 'grad_w', 'grad_lambda_im': 'grad_w', 'grad_log_dt': 'grad_w', 'grad_b_re': 'grad_w', 'grad_b_im': 'grad_w', 'grad_c_re': 'grad_w', 'grad_c_im': 'grad_w', 'grad_d_skip': 'grad_w', 'grad_w_glu': 'grad_w', 'grad_b_glu': 'grad_w', 'grad_w_branch_s': 'grad_w', 'grad_w_branch_a': 'grad_w', 'grad_w_out': 'grad_w', 'grad_post_norm_g': 'grad_w', 'delta_pre_norm_g': 'delta_w', 'delta_w_in': 'delta_w', 'delta_lambda_re': 'delta_w', 'delta_lambda_im': 'delta_w', 'delta_log_dt': 'delta_w', 'delta_b_re': 'delta_w', 'delta_b_im': 'delta_w', 'delta_c_re': 'delta_w', 'delta_c_im': 'delta_w', 'delta_d_skip': 'delta_w', 'delta_w_glu': 'delta_w', 'delta_b_glu': 'delta_w', 'delta_w_branch_s': 'delta_w', 'delta_w_branch_a': 'delta_w', 'delta_w_out': 'delta_w', 'delta_post_norm_g': 'delta_w', 'new_m_pre_norm_g': 'new_m', 'new_m_w_in': 'new_m', 'new_m_lambda_re': 'new_m', 'new_m_lambda_im': 'new_m', 'new_m_log_dt': 'new_m', 'new_m_b_re': 'new_m', 'new_m_b_im': 'new_m', 'new_m_c_re': 'new_m', 'new_m_c_im': 'new_m', 'new_m_d_skip': 'new_m', 'new_m_w_glu': 'new_m', 'new_m_b_glu': 'new_m', 'new_m_w_branch_s': 'new_m', 'new_m_w_branch_a': 'new_m', 'new_m_w_out': 'new_m', 'new_m_post_norm_g': 'new_m', 'new_v_pre_norm_g': 'new_v', 'new_v_w_in': 'new_v', 'new_v_lambda_re': 'new_v', 'new_v_lambda_im': 'new_v', 'new_v_log_dt': 'new_v', 'new_v_b_re': 'new_v', 'new_v_b_im': 'new_v', 'new_v_c_re': 'new_v', 'new_v_c_im': 'new_v', 'new_v_d_skip': 'new_v', 'new_v_w_glu': 'new_v', 'new_v_b_glu': 'new_v', 'new_v_w_branch_s': 'new_v', 'new_v_w_branch_a': 'new_v', 'new_v_w_out': 'new_v', 'new_v_post_norm_g': 'new_v'}


def _forward(args):
    return _fwd_reference(*[args[k] for k in FWD_PARAMS])


def _output_shape():
    out = _jax.eval_shape(lambda: _forward(_fwd_setup_inputs(0)))
    return out.shape, out.dtype

N_MICROBATCH = 1
ADAM_LR = 0.001
ADAM_B1 = 0.9
ADAM_B2 = 0.999
ADAM_EPS = 1e-08
ADAM_WD = 0.01
ADAM_STEP = 10
PER_EXAMPLE_BATCH_AXIS = {'x': 0, 'loss_target': 0}
SHARED_INPUTS = []
_WEIGHT_DTYPES = {'pre_norm_g': _jnp.float32, 'w_in': _jnp.float32, 'lambda_re': _jnp.float32, 'lambda_im': _jnp.float32, 'log_dt': _jnp.float32, 'b_re': _jnp.float32, 'b_im': _jnp.float32, 'c_re': _jnp.float32, 'c_im': _jnp.float32, 'd_skip': _jnp.float32, 'w_glu': _jnp.float32, 'b_glu': _jnp.float32, 'w_branch_s': _jnp.float32, 'w_branch_a': _jnp.float32, 'w_out': _jnp.float32, 'post_norm_g': _jnp.float32}
MOMENT_SCALE = {'pre_norm_g': 9.267974e-01, 'w_in': 3.264321e-01, 'lambda_re': 4.226707e-02, 'lambda_im': 4.635904e-02, 'log_dt': 3.343302e+01, 'b_re': 2.822857e-02, 'b_im': 2.798125e-02, 'c_re': 5.558426e-02, 'c_im': 5.775638e-02, 'd_skip': 1.315681e+00, 'w_glu': 2.438073e-01, 'b_glu': 5.380552e-01, 'w_branch_s': 9.579136e-01, 'w_branch_a': 2.785224e-01, 'w_out': 9.979738e-01, 'post_norm_g': 1.603588e+01}


def _to_microbatches(a, axis):
    t = _jnp.moveaxis(a, axis, 0)
    t = t.reshape((N_MICROBATCH, t.shape[0] // N_MICROBATCH) + t.shape[1:])
    return _jnp.moveaxis(t, 1, axis + 1)


def setup_inputs(seed: int = 0) -> dict:
    inp = _fwd_setup_inputs(seed)
    key = _jax.random.fold_in(_jax.random.key(seed), 7919)
    shape, _ = _output_shape()
    out = dict(inp)
    out["loss_target"] = _jax.random.normal(_jax.random.fold_in(key, 0), shape, _jnp.float32)
    for i, name in enumerate(TWIN_WEIGHTS):
        w = inp[name].astype(_jnp.float32)
        if MOMENT_SCALE is None:
            s = _jnp.sqrt(_jnp.mean(_jnp.square(w)) + 1e-30)
        else:
            s = MOMENT_SCALE[name]
        km, kv = _jax.random.split(_jax.random.fold_in(key, i + 1))
        out[name] = w
        out["m_" + name] = s * _jax.random.normal(km, w.shape, _jnp.float32)
        out["v_" + name] = (s * s) * _jax.random.uniform(kv, w.shape, _jnp.float32, 0.5, 1.5)
    if N_MICROBATCH > 1:
        for name, axis in PER_EXAMPLE_BATCH_AXIS.items():
            out[name] = _to_microbatches(out[name], axis)
    return {'x': out['x'], 'pre_norm_g': out['pre_norm_g'], 'w_in': out['w_in'], 'lambda_re': out['lambda_re'], 'lambda_im': out['lambda_im'], 'log_dt': out['log_dt'], 'b_re': out['b_re'], 'b_im': out['b_im'], 'c_re': out['c_re'], 'c_im': out['c_im'], 'd_skip': out['d_skip'], 'w_glu': out['w_glu'], 'b_glu': out['b_glu'], 'w_branch_s': out['w_branch_s'], 'w_branch_a': out['w_branch_a'], 'w_out': out['w_out'], 'post_norm_g': out['post_norm_g'], 'loss_target': out['loss_target'], 'm_pre_norm_g': out['m_pre_norm_g'], 'm_w_in': out['m_w_in'], 'm_lambda_re': out['m_lambda_re'], 'm_lambda_im': out['m_lambda_im'], 'm_log_dt': out['m_log_dt'], 'm_b_re': out['m_b_re'], 'm_b_im': out['m_b_im'], 'm_c_re': out['m_c_re'], 'm_c_im': out['m_c_im'], 'm_d_skip': out['m_d_skip'], 'm_w_glu': out['m_w_glu'], 'm_b_glu': out['m_b_glu'], 'm_w_branch_s': out['m_w_branch_s'], 'm_w_branch_a': out['m_w_branch_a'], 'm_w_out': out['m_w_out'], 'm_post_norm_g': out['m_post_norm_g'], 'v_pre_norm_g': out['v_pre_norm_g'], 'v_w_in': out['v_w_in'], 'v_lambda_re': out['v_lambda_re'], 'v_lambda_im': out['v_lambda_im'], 'v_log_dt': out['v_log_dt'], 'v_b_re': out['v_b_re'], 'v_b_im': out['v_b_im'], 'v_c_re': out['v_c_re'], 'v_c_im': out['v_c_im'], 'v_d_skip': out['v_d_skip'], 'v_w_glu': out['v_w_glu'], 'v_b_glu': out['v_b_glu'], 'v_w_branch_s': out['v_w_branch_s'], 'v_w_branch_a': out['v_w_branch_a'], 'v_w_out': out['v_w_out'], 'v_post_norm_g': out['v_post_norm_g']}


def _loss(weights, diff, rest, loss_target):
    with _jax.named_scope("forward"):
        args = {**rest, TWIN_DIFF_INPUT: diff, **{k: w.astype(_WEIGHT_DTYPES[k]) for k, w in weights.items()}}
        y = _forward(args)
    with _jax.named_scope("loss_head"):
        err = _jnp.square(y.astype(_jnp.float32) - loss_target)
        return 0.5 * _jnp.sum(_jnp.mean(err, axis=-1)) if err.ndim else 0.5 * err


def _adamw(w, g, m, v):
    m = ADAM_B1 * m + (1.0 - ADAM_B1) * g
    v = ADAM_B2 * v + (1.0 - ADAM_B2) * _jnp.square(g)
    m_hat = m / (1.0 - ADAM_B1 ** ADAM_STEP)
    v_hat = v / (1.0 - ADAM_B2 ** ADAM_STEP)
    delta = -ADAM_LR * (m_hat / (_jnp.sqrt(v_hat) + ADAM_EPS) + ADAM_WD * w)
    return delta, m, v


def reference(x, pre_norm_g, w_in, lambda_re, lambda_im, log_dt, b_re, b_im, c_re, c_im, d_skip, w_glu, b_glu, w_branch_s, w_branch_a, w_out, post_norm_g, loss_target, m_pre_norm_g, m_w_in, m_lambda_re, m_lambda_im, m_log_dt, m_b_re, m_b_im, m_c_re, m_c_im, m_d_skip, m_w_glu, m_b_glu, m_w_branch_s, m_w_branch_a, m_w_out, m_post_norm_g, v_pre_norm_g, v_w_in, v_lambda_re, v_lambda_im, v_log_dt, v_b_re, v_b_im, v_c_re, v_c_im, v_d_skip, v_w_glu, v_b_glu, v_w_branch_s, v_w_branch_a, v_w_out, v_post_norm_g):
    given = dict(x=x, pre_norm_g=pre_norm_g, w_in=w_in, lambda_re=lambda_re, lambda_im=lambda_im, log_dt=log_dt, b_re=b_re, b_im=b_im, c_re=c_re, c_im=c_im, d_skip=d_skip, w_glu=w_glu, b_glu=b_glu, w_branch_s=w_branch_s, w_branch_a=w_branch_a, w_out=w_out, post_norm_g=post_norm_g, loss_target=loss_target, m_pre_norm_g=m_pre_norm_g, m_w_in=m_w_in, m_lambda_re=m_lambda_re, m_lambda_im=m_lambda_im, m_log_dt=m_log_dt, m_b_re=m_b_re, m_b_im=m_b_im, m_c_re=m_c_re, m_c_im=m_c_im, m_d_skip=m_d_skip, m_w_glu=m_w_glu, m_b_glu=m_b_glu, m_w_branch_s=m_w_branch_s, m_w_branch_a=m_w_branch_a, m_w_out=m_w_out, m_post_norm_g=m_post_norm_g, v_pre_norm_g=v_pre_norm_g, v_w_in=v_w_in, v_lambda_re=v_lambda_re, v_lambda_im=v_lambda_im, v_log_dt=v_log_dt, v_b_re=v_b_re, v_b_im=v_b_im, v_c_re=v_c_re, v_c_im=v_c_im, v_d_skip=v_d_skip, v_w_glu=v_w_glu, v_b_glu=v_b_glu, v_w_branch_s=v_w_branch_s, v_w_branch_a=v_w_branch_a, v_w_out=v_w_out, v_post_norm_g=v_post_norm_g)
    weights = {n: given[n] for n in TWIN_WEIGHTS}
    shared = {n: given[n] for n in SHARED_INPUTS}
    per_example = {n: given[n] for n in ['x']}
    grad_fn = _jax.value_and_grad(_loss, argnums=(0, 1))

    def one_microbatch(ex, loss_target):
        ex = dict(ex)
        diff = ex.pop(TWIN_DIFF_INPUT)
        return grad_fn(weights, diff, {**shared, **ex}, loss_target)

    if N_MICROBATCH == 1:
        loss, (grad_w, grad_x) = one_microbatch(per_example, given["loss_target"])
    else:
        def body(carry, xs):
            loss_sum, grad_sum = carry
            l_k, (gw_k, gx_k) = one_microbatch(xs[0], xs[1])
            with _jax.named_scope("update"):
                return (loss_sum + l_k, _jax.tree.map(_jnp.add, grad_sum, gw_k)), gx_k

        init = (_jnp.zeros((), _jnp.float32), _jax.tree.map(_jnp.zeros_like, weights))
        (loss, grad_w), grad_x = _jax.lax.scan(body, init, (per_example, given["loss_target"]))
    with _jax.named_scope("update"):
        delta_w, new_m, new_v = {}, {}, {}
        for n in TWIN_WEIGHTS:
            delta_w[n], new_m[n], new_v[n] = _adamw(weights[n], grad_w[n], given["m_" + n], given["v_" + n])
    return (loss, grad_x, *[grad_w[n] for n in TWIN_WEIGHTS], *[delta_w[n] for n in TWIN_WEIGHTS],
            *[new_m[n] for n in TWIN_WEIGHTS], *[new_v[n] for n in TWIN_WEIGHTS])
```

```python
import functools
import math

import jax
import jax.numpy as jnp
from jax import lax
from jax.experimental import pallas as pl
from jax.experimental.pallas import tpu as pltpu

F32 = jnp.float32
BF16 = jnp.bfloat16

NDEV = 8
DEPTH = 4
SEQ = 2048
DM = 1024
NCOL = 8192
SW = 512
NGRP = 32
GCH = 16
NST = 64
NS = NGRP * NST
HD = 128
AW = 512
DILATIONS = (1, 4, 16)
ABLK = 128
RMS_EPS = 1e-6
LR, B1, B2, ADAM_EPS, WD, STEP = 0.001, 0.9, 0.999, 1e-08, 0.01, 10

CB_U, CB_ZS, CB_Q, CB_K, CB_V, CB_ZA = 0, 1, 2, 5, 8, 11
CB_GS, CB_GA = 6, 7

VMEM_LIMIT = 56 * 2 ** 20


def _params(*sem):
    return pltpu.CompilerParams(dimension_semantics=sem, vmem_limit_bytes=VMEM_LIMIT)


def _ew(name, fn, rows, br, row_ins, bc_ins, row_outs, red_outs=()):
    n_in = len(row_ins) + len(bc_ins)
    n_ro = len(row_outs)
    steps = rows // br
    assert steps * br == rows

    def body(*refs):
        vals = fn(*[r[...] for r in refs[:n_in]])
        outs = refs[n_in:]
        for r, v in zip(outs[:n_ro], vals[:n_ro]):
            r[...] = v.astype(r.dtype)
        if red_outs:
            @pl.when(pl.program_id(0) == 0)
            def _():
                for r in outs[n_ro:]:
                    r[...] = jnp.zeros(r.shape, r.dtype)
            for r, v in zip(outs[n_ro:], vals[n_ro:]):
                r[...] += v

    in_specs = []
    for (_, w, cb, rb) in row_ins:
        in_specs.append(pl.BlockSpec((br, w), functools.partial(lambda i, cb, rb: (rb + i, cb), cb=cb, rb=rb)))
    for a in bc_ins:
        in_specs.append(pl.BlockSpec(a.shape, functools.partial(lambda i, nd: (0,) * nd, nd=a.ndim)))
    out_specs = [pl.BlockSpec((br, w), lambda i: (i, 0)) for (w, _) in row_outs]
    out_specs += [pl.BlockSpec((1, w), lambda i: (0, 0)) for w in red_outs]
    out_shape = [jax.ShapeDtypeStruct((rows, w), dt) for (w, dt) in row_outs]
    out_shape += [jax.ShapeDtypeStruct((1, w), F32) for w in red_outs]
    return pl.pallas_call(
        body, name=name, grid=(steps,), in_specs=in_specs, out_specs=out_specs, out_shape=out_shape,
        compiler_params=_params("arbitrary"),
    )(*[a for (a, _, _, _) in row_ins], *bc_ins)


def _ri(a, w=None, cb=0, rb=0):
    return (a, a.shape[1] if w is None else w, cb, rb)


_DIMS = {"nn": ((1,), (0,)), "nt": ((1,), (1,)), "tn": ((0,), (0,))}


def _mm(name, a, b, mode, M, N, K, bm, bn, bk, out_dtype, a_spec=None, b_spec=None, o_spec=None, out_shape=None):
    nk = K // bk
    assert M % bm == 0 and N % bn == 0 and nk * bk == K

    def body(a_ref, b_ref, o_ref, acc_ref):
        k = pl.program_id(2)
        part = lax.dot_general(a_ref[...].astype(BF16), b_ref[...].astype(BF16), (_DIMS[mode], ((), ())),
                               preferred_element_type=F32)

        @pl.when(k == 0)
        def _():
            acc_ref[...] = part

        @pl.when(k > 0)
        def _():
            acc_ref[...] += part

        @pl.when(k == nk - 1)
        def _():
            o_ref[...] = acc_ref[...].astype(o_ref.dtype)

    if a_spec is None:
        a_spec = (pl.BlockSpec((bk, bm), lambda i, j, k: (k, i)) if mode == "tn"
                  else pl.BlockSpec((bm, bk), lambda i, j, k: (i, k)))
    if b_spec is None:
        b_spec = (pl.BlockSpec((bn, bk), lambda i, j, k: (j, k)) if mode == "nt"
                  else pl.BlockSpec((bk, bn), lambda i, j, k: (k, j)))
    if o_spec is None:
        o_spec = pl.BlockSpec((bm, bn), lambda i, j, k: (i, j))
    if out_shape is None:
        out_shape = (M, N)
    return pl.pallas_call(
        body, name=name, grid=(M // bm, N // bn, nk), in_specs=[a_spec, b_spec], out_specs=o_spec,
        out_shape=jax.ShapeDtypeStruct(out_shape, out_dtype),
        scratch_shapes=[pltpu.VMEM((bm, bn), F32)],
        compiler_params=_params("parallel", "parallel", "arbitrary"),
    )(a, b)


SCAN_LANES = 512
SCAN_CHUNKS = 8


def _to_chunked(a):
    return a.reshape(SCAN_CHUNKS, SEQ // SCAN_CHUNKS, -1).transpose(1, 0, 2).reshape(SEQ, -1)


def _from_chunked(a):
    return a.reshape(SEQ // SCAN_CHUNKS, SCAN_CHUNKS, -1).transpose(1, 0, 2).reshape(SEQ, -1)


def _scan(name, d, lam_r, lam_i, reverse):
    T = SEQ // SCAN_CHUNKS
    bl = SCAN_LANES
    nblk = NS // bl
    assert T == 2 ** 8

    def body(dr_ref, di_ref, ar_ref, ai_ref, sr_ref, si_ref):
        ar = jnp.broadcast_to(ar_ref[...], (SCAN_CHUNKS, bl))
        ai = jnp.broadcast_to(ai_ref[...], (SCAN_CHUNKS, bl))
        zero = jnp.zeros((SCAN_CHUNKS, bl), F32)

        def tile(j):
            return pl.ds(pl.multiple_of(j * SCAN_CHUNKS, SCAN_CHUNKS), SCAN_CHUNKS)

        def step(jj, carry):
            sr, si = carry
            j = T - 1 - jj if reverse else jj
            nr = ar * sr - ai * si + dr_ref[tile(j), :]
            ni = ar * si + ai * sr + di_ref[tile(j), :]
            sr_ref[tile(j), :] = nr
            si_ref[tile(j), :] = ni
            return nr, ni

        er, ei = lax.fori_loop(0, T, step, (zero, zero), unroll=4)

        pr, pi = ar[0:1], ai[0:1]
        for _ in range(8):
            pr, pi = pr * pr - pi * pi, 2.0 * pr * pi
        rows = lax.broadcasted_iota(jnp.int32, (SCAN_CHUNKS, bl), 0)
        cr, ci = zero, zero
        xr = jnp.zeros((1, bl), F32)
        xi = jnp.zeros((1, bl), F32)
        order = range(SCAN_CHUNKS - 2, -1, -1) if reverse else range(1, SCAN_CHUNKS)
        for c in order:
            src = c + 1 if reverse else c - 1
            nxr = pr * xr - pi * xi + er[src:src + 1]
            nxi = pr * xi + pi * xr + ei[src:src + 1]
            xr, xi = nxr, nxi
            cr = jnp.where(rows == c, xr, cr)
            ci = jnp.where(rows == c, xi, ci)

        def fix(jj, pw):
            pwr, pwi = pw
            j = T - 1 - jj if reverse else jj
            sr_ref[tile(j), :] = sr_ref[tile(j), :] + (pwr * cr - pwi * ci)
            si_ref[tile(j), :] = si_ref[tile(j), :] + (pwr * ci + pwi * cr)
            return pwr * ar - pwi * ai, pwr * ai + pwi * ar

        lax.fori_loop(0, T, fix, (ar, ai), unroll=4)

    return pl.pallas_call(
        body, name=name, grid=(nblk,),
        in_specs=[pl.BlockSpec((SEQ, bl), lambda i: (0, i)),
                  pl.BlockSpec((SEQ, bl), lambda i: (0, nblk + i)),
                  pl.BlockSpec((1, bl), lambda i: (0, i)),
                  pl.BlockSpec((1, bl), lambda i: (0, i))],
        out_specs=[pl.BlockSpec((SEQ, bl), lambda i: (0, i)),
                   pl.BlockSpec((SEQ, bl), lambda i: (0, i))],
        out_shape=[jax.ShapeDtypeStruct((SEQ, NS), F32), jax.ShapeDtypeStruct((SEQ, NS), F32)],
        compiler_params=_params("arbitrary"),
    )(d, d, lam_r, lam_i)


def _dlam(name, a_r, a_i, s_r, s_i):
    bl = 256

    def prev(s_ref):
        last = pltpu.roll(s_ref[SEQ - SCAN_CHUNKS:SEQ, :], 1, 0)
        first = jnp.where(lax.broadcasted_iota(jnp.int32, (SCAN_CHUNKS, bl), 0) > 0, last, 0.0)
        return jnp.concatenate([first, s_ref[0:SEQ - SCAN_CHUNKS, :]], axis=0)

    def body(ar_ref, ai_ref, sr_ref, si_ref, or_ref, oi_ref):
        spr, spi = prev(sr_ref), prev(si_ref)
        a_r_, a_i_ = ar_ref[...], ai_ref[...]
        or_ref[...] = jnp.sum(a_r_ * spr + a_i_ * spi, axis=0, keepdims=True)
        oi_ref[...] = jnp.sum(a_i_ * spr - a_r_ * spi, axis=0, keepdims=True)

    spec = pl.BlockSpec((SEQ, bl), lambda i: (0, i))
    ospec = pl.BlockSpec((1, bl), lambda i: (0, i))
    return pl.pallas_call(
        body, name=name, grid=(NS // bl,), in_specs=[spec] * 4, out_specs=[ospec, ospec],
        out_shape=[jax.ShapeDtypeStruct((1, NS), F32)] * 2,
        compiler_params=_params("arbitrary"),
    )(a_r, a_i, s_r, s_i)


def _scores(qb, kb, prev):
    s = lax.dot_general(qb, kb, (((1,), (1,)), ((), ())), preferred_element_type=F32) * (HD ** -0.5)
    row = lax.broadcasted_iota(jnp.int32, (ABLK, ABLK), 0)
    col = lax.broadcasted_iota(jnp.int32, (ABLK, ABLK), 1)
    return jnp.where((col >= row) if prev else (col <= row), s, -1e30)


def _rows_of(b):
    if isinstance(b, int):
        return pl.ds(b * ABLK, ABLK)
    return pl.ds(pl.multiple_of(b * ABLK, ABLK), ABLK)


def _attn_fwd(name, proj, gi):
    dil = DILATIONS[gi]
    n = SEQ // dil
    nb = n // ABLK
    pv = proj.reshape(n, dil * NCOL)

    def body(q_ref, k_ref, v_ref, o_ref, l_ref):
        def blk(ref, b):
            return ref[_rows_of(b), :]

        def one(b, first):
            qb = blk(q_ref, b).astype(BF16)
            s_c = _scores(qb, blk(k_ref, b).astype(BF16), False)
            m = jnp.max(s_c, axis=-1, keepdims=True)
            if not first:
                s_p = _scores(qb, blk(k_ref, b - 1).astype(BF16), True)
                m = jnp.maximum(m, jnp.max(s_p, axis=-1, keepdims=True))
            p_c = jnp.exp(s_c - m)
            den = jnp.sum(p_c, axis=-1, keepdims=True)
            acc = jnp.dot(p_c.astype(BF16), blk(v_ref, b).astype(BF16), preferred_element_type=F32)
            if not first:
                p_p = jnp.exp(s_p - m)
                den = den + jnp.sum(p_p, axis=-1, keepdims=True)
                acc = acc + jnp.dot(p_p.astype(BF16), blk(v_ref, b - 1).astype(BF16), preferred_element_type=F32)
            rows = _rows_of(b)
            o_ref[rows, :] = acc / den
            l_ref[rows, :] = jnp.broadcast_to(m + jnp.log(den), (ABLK, HD))

        one(0, True)
        if nb > 1:
            def loop(b, c):
                one(b, False)
                return c
            lax.fori_loop(1, nb, loop, 0)

    def spec(cb):
        return pl.BlockSpec((n, HD), functools.partial(lambda r, j, cb: (0, r * (NCOL // HD) + cb * 4 + gi * 4 + j), cb=cb))

    ospec = pl.BlockSpec((n, HD), lambda r, j: (0, r * 4 + j))
    o, lse = pl.pallas_call(
        body, name=name, grid=(dil, 4), in_specs=[spec(CB_Q), spec(CB_K), spec(CB_V)], out_specs=[ospec, ospec],
        out_shape=[jax.ShapeDtypeStruct((n, dil * AW), F32)] * 2,
        compiler_params=_params("parallel", "parallel"),
    )(pv, pv, pv)
    return o.reshape(SEQ, AW), lse.reshape(SEQ, AW)


def _attn_bwd(name, proj, dy, lse, dsum, gi):
    dil = DILATIONS[gi]
    n = SEQ // dil
    nb = n // ABLK
    pv = proj.reshape(n, dil * NCOL)
    dyv, lv, dv_ = (t.reshape(n, dil * AW) for t in (dy, lse, dsum))

    def body(q_ref, k_ref, v_ref, dy_ref, l_ref, d_ref, dq_ref, dk_ref, dv_ref, dk_acc, dv_acc):
        dk_acc[...] = jnp.zeros(dk_acc.shape, F32)
        dv_acc[...] = jnp.zeros(dv_acc.shape, F32)

        rows_of = _rows_of

        def tn(a, b_):
            return lax.dot_general(a, b_, (((0,), (0,)), ((), ())), preferred_element_type=F32)

        def nt(a, b_):
            return lax.dot_general(a, b_, (((1,), (1,)), ((), ())), preferred_element_type=F32)

        def side(b, kb_idx, prev, qb, dyb, lb, db):
            kb = k_ref[rows_of(kb_idx), :].astype(BF16)
            vb = v_ref[rows_of(kb_idx), :].astype(BF16)
            p = jnp.exp(_scores(qb, kb, prev) - lb)
            ds = p * (nt(dyb, vb) - db) * (HD ** -0.5)
            dsb = ds.astype(BF16)
            dk_acc[rows_of(kb_idx), :] += tn(dsb, qb)
            dv_acc[rows_of(kb_idx), :] += tn(p.astype(BF16), dyb)
            return jnp.dot(dsb, kb, preferred_element_type=F32)

        def one(b, first):
            qb = q_ref[rows_of(b), :].astype(BF16)
            dyb = dy_ref[rows_of(b), :].astype(BF16)
            lb = l_ref[rows_of(b), :][:, 0:1]
            db = d_ref[rows_of(b), :][:, 0:1]
            dq = side(b, b, False, qb, dyb, lb, db)
            if not first:
                dq = dq + side(b, b - 1, True, qb, dyb, lb, db)
            dq_ref[rows_of(b), :] = dq.astype(dq_ref.dtype)

        one(0, True)
        if nb > 1:
            def loop(b, c):
                one(b, False)
                return c
            lax.fori_loop(1, nb, loop, 0)
        dk_ref[...] = dk_acc[...].astype(dk_ref.dtype)
        dv_ref[...] = dv_acc[...].astype(dv_ref.dtype)

    def spec(cb):
        return pl.BlockSpec((n, HD), functools.partial(lambda r, j, cb: (0, r * (NCOL // HD) + cb * 4 + gi * 4 + j), cb=cb))

    ospec = pl.BlockSpec((n, HD), lambda r, j: (0, r * 4 + j))
    outs = pl.pallas_call(
        body, name=name, grid=(dil, 4),
        in_specs=[spec(CB_Q), spec(CB_K), spec(CB_V), ospec, ospec, ospec], out_specs=[ospec] * 3,
        out_shape=[jax.ShapeDtypeStruct((n, dil * AW), BF16)] * 3,
        scratch_shapes=[pltpu.VMEM((n, HD), F32), pltpu.VMEM((n, HD), F32)],
        compiler_params=_params("parallel", "parallel"),
    )(pv, pv, pv, dyv, lv, dv_)
    return [t.reshape(SEQ, AW) for t in outs]


def _rms(x, g):
    return x * lax.rsqrt(jnp.mean(x * x, axis=-1, keepdims=True) + RMS_EPS) * g


def _sig(x):
    return 1.0 / (1.0 + jnp.exp(-x))


def _silu(x):
    return x * _sig(x)


def _gelu(x):
    return 0.5 * x * (1.0 + jnp.tanh(math.sqrt(2.0 / math.pi) * (x + 0.044715 * (x * x * x))))


def _y1_fn(y0p, u, dskip):
    return _gelu(y0p + dskip * u)


def _ys_fn(y1, t, z, bglu):
    return y1 * _sig(t + bglu) * _silu(z)


def _merge_fn(ms, ma, gs, ga):
    return _sig(gs) * ms + _sig(ga) * ma


def _colsum(v):
    return jnp.sum(v, axis=0, keepdims=True)


def _head_sums(v):
    parts = [jnp.broadcast_to(jnp.sum(v[:, j * HD:(j + 1) * HD], axis=-1, keepdims=True), (v.shape[0], HD))
             for j in range(AW // HD)]
    return jnp.concatenate(parts, axis=-1)


def _lam_fn(lre, lim, ldt):
    a = jnp.minimum(lre, -1e-4)
    dt = jnp.exp(ldt)
    mag = jnp.exp(a * dt)
    ar = mag * jnp.cos(lim * dt)
    ai = mag * jnp.sin(lim * dt)
    den = a * a + lim * lim
    cr = ((ar - 1.0) * a + ai * lim) / den
    ci = (ai * a - (ar - 1.0) * lim) / den
    return ar, ai, cr, ci


def _bbar_fn(cr, ci, bre, bim):
    return cr * bre - ci * bim, cr * bim + ci * bre


def _blockdiag(t):
    g, a, b = t.shape
    eye = jnp.eye(g, dtype=bool)[:, None, :, None]
    return jnp.where(eye, t[:, :, None, :], jnp.zeros((), t.dtype)).reshape(g * a, g * b)


def _diagblocks(m, a, b):
    m4 = m.reshape(NGRP, a, NGRP, b)
    idx = jnp.arange(NGRP)
    return m4[idx, :, idx, :]


def _layer_fwd(l, x, w, sp):
    tag = f"l{l}_"
    g1 = sp["pre_norm_g"].reshape(1, DM)
    (h,) = _ew(tag + "rms1", lambda x_, g: (_rms(x_, g),), SEQ, 256, [_ri(x)], [g1], [(DM, BF16)])
    win = w["w_in"]
    proj = _mm(tag + "proj", h, win, "nn", SEQ, NCOL, DM, 512, 1024, 1024, F32,
               b_spec=pl.BlockSpec((None, None, 1024, 1024), lambda i, j, k: (j, l, 0, 0)))

    ar, ai, cr, ci = _ew(tag + "lam", _lam_fn, NGRP, NGRP,
                         [_ri(sp["lambda_re"]), _ri(sp["lambda_im"]), _ri(sp["log_dt"].reshape(NGRP, 1))], [],
                         [(NST, F32)] * 4)
    bre = sp["b_re"].reshape(NS, GCH)
    bim = sp["b_im"].reshape(NS, GCH)
    bbr, bbi = _ew(tag + "bbar", _bbar_fn, NS, NS, [_ri(cr.reshape(NS, 1)), _ri(ci.reshape(NS, 1)), _ri(bre), _ri(bim)],
                   [], [(GCH, F32)] * 2)
    wd = jnp.concatenate([_blockdiag(bbr.reshape(NGRP, NST, GCH).transpose(0, 2, 1)),
                          _blockdiag(bbi.reshape(NGRP, NST, GCH).transpose(0, 2, 1))], axis=1).astype(BF16)
    cm = jnp.concatenate([_blockdiag(sp["c_re"].transpose(0, 2, 1)),
                          -_blockdiag(sp["c_im"].transpose(0, 2, 1))], axis=0).astype(BF16)
    uz = _to_chunked(proj[:, :2 * SW])
    drive = _mm(tag + "drive", uz, wd, "nn", SEQ, 2 * NS, SW, 512, 1024, 512, F32,
                a_spec=pl.BlockSpec((512, SW), lambda i, j, k: (i, 0)))
    lam_r, lam_i = ar.reshape(1, NS), ai.reshape(1, NS)
    s_r, s_i = _scan(tag + "scan", drive, lam_r, lam_i, False)
    y0p = _mm(tag + "readout_r", s_r, cm, "nn", SEQ, SW, NS, 512, 512, 1024, F32)
    y0p_i = _mm(tag + "readout_i", s_i, cm, "nn", SEQ, SW, NS, 512, 512, 1024, F32,
                b_spec=pl.BlockSpec((1024, 512), lambda i, j, k: (NS // 1024 + k, j)))
    dskip = sp["d_skip"].reshape(1, SW)
    (y1,) = _ew(tag + "y1", lambda a, b_, u, d: (_y1_fn(a + b_, u, d),), SEQ, 256,
                [_ri(y0p), _ri(y0p_i), _ri(uz, SW, 0)], [dskip], [(SW, F32)])
    t = _mm(tag + "glu", y1, w["w_glu"][l], "nn", SEQ, SW, SW, 512, 512, 512, F32)
    bglu = sp["b_glu"].reshape(1, SW)
    (ys_c,) = _ew(tag + "ys", lambda y1_, t_, z, b_: (_ys_fn(y1_, t_, z, b_),), SEQ, 256,
                  [_ri(y1), _ri(t), _ri(uz, SW, 1)], [bglu], [(SW, BF16)])
    ys = _from_chunked(ys_c)

    outs, lses = zip(*[_attn_fwd(tag + f"attn{gi}", proj, gi) for gi in range(3)])

    def comb(o0, o1, o2, l0, l1, l2, z):
        m = jnp.maximum(jnp.maximum(l0, l1), l2)
        e0, e1, e2 = jnp.exp(l0 - m), jnp.exp(l1 - m), jnp.exp(l2 - m)
        tot = e0 + e1 + e2
        ypre = (e0 * o0 + e1 * o1 + e2 * o2) / tot
        return ypre * _silu(z), ypre, m + jnp.log(tot)

    ya, ypre, lse = _ew(tag + "comb", comb, SEQ, 256, [_ri(o) for o in outs] + [_ri(s) for s in lses]
                        + [_ri(proj, AW, CB_ZA)], [], [(AW, BF16), (AW, F32), (AW, F32)])

    ms = _mm(tag + "branch_s", ys, w["w_branch_s"][l], "nn", SEQ, DM, SW, 512, 1024, 512, F32)
    ma = _mm(tag + "branch_a", ya, w["w_branch_a"][l], "nn", SEQ, DM, AW, 512, 1024, 512, F32)
    (merged,) = _ew(tag + "merge", lambda a, b_, c, d: (_merge_fn(a, b_, c, d),), SEQ, 256,
                    [_ri(ms), _ri(ma), _ri(proj, DM, CB_GS), _ri(proj, DM, CB_GA)], [], [(DM, BF16)])
    out = _mm(tag + "out", merged, w["w_out"][l], "nn", SEQ, DM, DM, 512, 1024, 1024, F32)
    g2 = sp["post_norm_g"].reshape(1, DM)
    (x_new,) = _ew(tag + "post", lambda x_, o, g: (x_ + _rms(o, g),), SEQ, 256, [_ri(x), _ri(out)], [g2], [(DM, F32)])
    res = dict(x=x, h=h, proj=proj, uz=uz, ar=ar, ai=ai, cr=cr, ci=ci, wd=wd, cm=cm, s_r=s_r, s_i=s_i, y0p=y0p, y0p_i=y0p_i,
               y1=y1, t=t, ys=ys, ya=ya, ypre=ypre, lse=lse, ms=ms, ma=ma, merged=merged, out=out)
    return x_new, res


def _layer_bwd(l, dxn, r, w, sp):
    tag = f"l{l}b_"
    proj = r["proj"]
    g1 = sp["pre_norm_g"].reshape(1, DM)
    g2 = sp["post_norm_g"].reshape(1, DM)
    dskip = sp["d_skip"].reshape(1, SW)
    bglu = sp["b_glu"].reshape(1, SW)

    def post_b(d, o, g):
        _, vjp = jax.vjp(_rms, o, g)
        do, dg = vjp(d)
        return do, dg

    d_out, dg2 = _ew(tag + "post", post_b, SEQ, 256, [_ri(dxn), _ri(r["out"])], [g2], [(DM, BF16)], [DM])
    dw_out = _mm(tag + "dw_out", r["merged"], d_out, "tn", DM, DM, SEQ, 512, 1024, 1024, BF16)
    dmerged = _mm(tag + "dmerged", d_out, w["w_out"][l], "nt", SEQ, DM, DM, 512, 1024, 1024, F32)

    def merge_b(d, ms, ma, gs, ga):
        _, vjp = jax.vjp(_merge_fn, ms, ma, gs, ga)
        return vjp(d)

    dms, dma, dgs, dga = _ew(tag + "merge", merge_b, SEQ, 256,
                             [_ri(dmerged), _ri(r["ms"]), _ri(r["ma"]), _ri(proj, DM, CB_GS), _ri(proj, DM, CB_GA)],
                             [], [(DM, BF16)] * 4)
    dw_bs = _mm(tag + "dw_bs", r["ys"], dms, "tn", SW, DM, SEQ, 512, 1024, 1024, BF16)
    dw_ba = _mm(tag + "dw_ba", r["ya"], dma, "tn", AW, DM, SEQ, 512, 1024, 1024, BF16)
    dys = _mm(tag + "dys", dms, w["w_branch_s"][l], "nt", SEQ, SW, DM, 512, 512, 1024, F32)
    dya = _mm(tag + "dya", dma, w["w_branch_a"][l], "nt", SEQ, AW, DM, 512, 512, 1024, F32)

    def comb_b(d, ypre, z):
        _, vjp = jax.vjp(lambda y, z_: y * _silu(z_), ypre, z)
        dyp, dz = vjp(d)
        return dyp, dz, _head_sums(dyp * ypre)

    dyp, dza, dsum = _ew(tag + "comb", comb_b, SEQ, 256, [_ri(dya), _ri(r["ypre"]), _ri(proj, AW, CB_ZA)], [],
                         [(AW, F32), (AW, BF16), (AW, F32)])
    dqkv = [_attn_bwd(tag + f"attn{gi}", proj, dyp, r["lse"], dsum, gi) for gi in range(3)]

    def ys_b(d, y1, t, z, b_):
        _, vjp = jax.vjp(_ys_fn, y1, t, z, b_)
        dy1, dt, dz, _ = vjp(d)
        return dy1, dt, dz, _colsum(dt)

    uz = r["uz"]
    dy1a, dt, dzs_c, dbglu = _ew(tag + "ys", ys_b, SEQ, 256,
                                 [_ri(_to_chunked(dys)), _ri(r["y1"]), _ri(r["t"]), _ri(uz, SW, 1)],
                                 [bglu], [(SW, F32), (SW, BF16), (SW, BF16)], [SW])
    dw_glu = _mm(tag + "dw_glu", r["y1"], dt, "tn", SW, SW, SEQ, 512, 512, 1024, BF16)
    dy1b = _mm(tag + "dy1b", dt, w["w_glu"][l], "nt", SEQ, SW, SW, 512, 512, 512, F32)

    def y1_b(da, db, y0p, y0p_i, u, d_):
        _, vjp = jax.vjp(_y1_fn, y0p + y0p_i, u, d_)
        dy0, du, dd = vjp(da + db)
        return dy0, du, dd

    dy0, du_skip, ddskip = _ew(tag + "y1", y1_b, SEQ, 256,
                               [_ri(dy1a), _ri(dy1b), _ri(r["y0p"]), _ri(r["y0p_i"]), _ri(uz, SW, 0)], [dskip],
                               [(SW, BF16), (SW, F32)], [SW])
    g_s = _mm(tag + "dstate", dy0, r["cm"], "nt", SEQ, 2 * NS, SW, 512, 1024, 512, F32)
    dcm_r = _mm(tag + "dcm_r", r["s_r"], dy0, "tn", NS, SW, SEQ, 1024, 512, 1024, F32)
    dcm_i = _mm(tag + "dcm_i", r["s_i"], dy0, "tn", NS, SW, SEQ, 1024, 512, 1024, F32)
    lam_r, lam_i = r["ar"].reshape(1, NS), r["ai"].reshape(1, NS)
    a_r, a_i = _scan(tag + "scan", g_s, lam_r, -lam_i, True)
    dlr, dli = _dlam(tag + "dlam", a_r, a_i, r["s_r"], r["s_i"])
    u_spec = pl.BlockSpec((1024, SW), lambda i, j, k: (k, 0))
    dwd_r = _mm(tag + "dwd_r", uz, a_r, "tn", SW, NS, SEQ, 512, 1024, 1024, F32, a_spec=u_spec)
    dwd_i = _mm(tag + "dwd_i", uz, a_i, "tn", SW, NS, SEQ, 512, 1024, 1024, F32, a_spec=u_spec)
    du_r = _mm(tag + "du_r", a_r, r["wd"], "nt", SEQ, SW, NS, 512, 512, 1024, F32)
    du_i = _mm(tag + "du_i", a_i, r["wd"], "nt", SEQ, SW, NS, 512, 512, 1024, F32,
               b_spec=pl.BlockSpec((512, 1024), lambda i, j, k: (j, NS // 1024 + k)))
    (du_c,) = _ew(tag + "du", lambda a, b_, c: (a + b_ + c,), SEQ, 256, [_ri(du_r), _ri(du_i), _ri(du_skip)], [],
                  [(SW, BF16)])
    du, dzs = _from_chunked(du_c), _from_chunked(dzs_c)

    dbbr = _diagblocks(dwd_r, GCH, NST).transpose(0, 2, 1).reshape(NS, GCH)
    dbbi = _diagblocks(dwd_i, GCH, NST).transpose(0, 2, 1).reshape(NS, GCH)
    bre = sp["b_re"].reshape(NS, GCH)
    bim = sp["b_im"].reshape(NS, GCH)

    def bbar_b(cr, ci, br_, bi_, dr, di):
        _, vjp = jax.vjp(_bbar_fn, cr, ci, br_, bi_)
        return vjp((dr, di))

    dcr, dci, dbre, dbim = _ew(tag + "bbar", bbar_b, NS, NS,
                               [_ri(r["cr"].reshape(NS, 1)), _ri(r["ci"].reshape(NS, 1)), _ri(bre), _ri(bim),
                                _ri(dbbr), _ri(dbbi)], [], [(1, F32), (1, F32), (GCH, F32), (GCH, F32)])

    def lam_b(lre, lim, ldt, dar, dai, dcr_, dci_):
        _, vjp = jax.vjp(_lam_fn, lre, lim, ldt)
        return vjp((dar, dai, dcr_, dci_))

    dlre, dlim, dldt = _ew(tag + "lam", lam_b, NGRP, NGRP,
                           [_ri(sp["lambda_re"]), _ri(sp["lambda_im"]), _ri(sp["log_dt"].reshape(NGRP, 1)),
                            _ri(dlr.reshape(NGRP, NST)), _ri(dli.reshape(NGRP, NST)),
                            _ri(dcr.reshape(NGRP, NST)), _ri(dci.reshape(NGRP, NST))], [],
                           [(NST, F32), (NST, F32), (1, F32)])
    dc_re = _diagblocks(dcm_r, NST, GCH).transpose(0, 2, 1)
    dc_im = -_diagblocks(dcm_i, NST, GCH).transpose(0, 2, 1)

    dq, dk, dv = ([d[i] for d in dqkv] for i in range(3))
    dproj = jnp.concatenate([du, dzs, *dq, *dk, *dv, dza, dgs, dga], axis=1)
    dw_in = _mm(tag + "dw_in", r["h"], dproj, "tn", DM, NCOL, SEQ, 1024, 1024, 1024, BF16,
                o_spec=pl.BlockSpec((None, 1024, 1024), lambda i, j, k: (j, 0, 0)), out_shape=(NDEV, DM, DM))
    dh = _mm(tag + "dh", dproj, w["w_in"], "nt", SEQ, DM, NCOL, 512, 1024, 1024, F32,
             b_spec=pl.BlockSpec((None, None, 1024, 1024), lambda i, j, k: (k, l, 0, 0)))

    def pre_b(d, dh_, x_, g):
        _, vjp = jax.vjp(_rms, x_, g)
        dx_, dg = vjp(dh_)
        return d + dx_, dg

    dx, dg1 = _ew(tag + "pre", pre_b, SEQ, 256, [_ri(dxn), _ri(dh), _ri(r["x"])], [g1], [(DM, F32)], [DM])

    big = dict(w_in=dw_in, w_glu=dw_glu.reshape(NDEV, SW // NDEV, SW),
               w_branch_s=dw_bs.reshape(SW, NDEV, DM // NDEV).transpose(1, 0, 2),
               w_branch_a=dw_ba.reshape(AW, NDEV, DM // NDEV).transpose(1, 0, 2),
               w_out=dw_out.reshape(NDEV, DM // NDEV, DM))
    small = dict(pre_norm_g=dg1.reshape(DM), lambda_re=dlre, lambda_im=dlim, log_dt=dldt.reshape(NGRP),
                 b_re=dbre.reshape(NGRP, NST, GCH), b_im=dbim.reshape(NGRP, NST, GCH), c_re=dc_re, c_im=dc_im,
                 d_skip=ddskip.reshape(SW), b_glu=dbglu.reshape(SW), post_norm_g=dg2.reshape(DM))
    return dx, big, small


def _exchange(name, arrs, gather):
    n = len(arrs)

    def body(*refs):
        srcs, dsts = refs[:n], refs[n:2 * n]
        send_sems, recv_sems, local_sems = refs[2 * n:]
        x, y, c = lax.axis_index("x"), lax.axis_index("y"), lax.axis_index("c")
        me = 4 * x + 2 * y + c
        copies = []
        for i in range(n):
            mine = srcs[i] if gather[i] else srcs[i].at[me]
            loc = pltpu.make_async_copy(mine, dsts[i].at[me], local_sems.at[i])
            loc.start()
            copies.append(loc)
            for k in range(1, NDEV):
                peer = (x ^ (k >> 2), y ^ ((k >> 1) & 1), c ^ (k & 1))
                src = srcs[i] if gather[i] else srcs[i].at[me ^ k]
                cp = pltpu.make_async_remote_copy(src_ref=src, dst_ref=dsts[i].at[me], send_sem=send_sems.at[i, k],
                                                  recv_sem=recv_sems.at[i, k], device_id=peer,
                                                  device_id_type=pl.DeviceIdType.MESH)
                cp.start()
                copies.append(cp)
        for cp in copies:
            cp.wait()

    out_shape = [jax.ShapeDtypeStruct(((NDEV,) + a.shape) if g else a.shape, a.dtype) for a, g in zip(arrs, gather)]
    any_spec = pl.BlockSpec(memory_space=pl.ANY)
    return pl.pallas_call(
        body, name=name, in_specs=[any_spec] * n, out_specs=[any_spec] * n, out_shape=out_shape,
        scratch_shapes=[pltpu.SemaphoreType.DMA((n, NDEV)), pltpu.SemaphoreType.DMA((n, NDEV)),
                        pltpu.SemaphoreType.DMA((n,))],
    )(*arrs)


def _adamw(name, parts, w, m, v, br):
    rows, cols = w.shape

    def fn(*a):
        g = a[0].astype(F32)
        for p in a[1:NDEV]:
            g = g + p.astype(F32)
        w_, m_, v_ = a[NDEV:]
        m2 = B1 * m_ + (1.0 - B1) * g
        v2 = B2 * v_ + (1.0 - B2) * (g * g)
        m_hat = m2 / (1.0 - B1 ** STEP)
        v_hat = v2 / (1.0 - B2 ** STEP)
        delta = -LR * (m_hat / (jnp.sqrt(v_hat) + ADAM_EPS) + WD * w_)
        return g, delta, m2, v2

    ins = [_ri(parts, cols, 0, d * (rows // br)) for d in range(NDEV)] + [_ri(w), _ri(m), _ri(v)]
    return _ew(name, fn, rows, br, ins, [], [(cols, F32)] * 4)


SMALL = ("pre_norm_g", "lambda_re", "lambda_im", "log_dt", "b_re", "b_im", "c_re", "c_im", "d_skip", "b_glu",
         "post_norm_g")
BIG = ("w_in", "w_glu", "w_branch_s", "w_branch_a", "w_out")
WEIGHTS = ("pre_norm_g", "w_in", "lambda_re", "lambda_im", "log_dt", "b_re", "b_im", "c_re", "c_im", "d_skip",
           "w_glu", "b_glu", "w_branch_s", "w_branch_a", "w_out", "post_norm_g")
PACK_COLS = 1024
PACK_BR = 136


def _pack(d):
    flat = jnp.concatenate([d[k].astype(F32).reshape(-1) for k in SMALL])
    rows = -(-flat.shape[0] // (PACK_COLS * PACK_BR)) * PACK_BR
    return jnp.pad(flat, (0, rows * PACK_COLS - flat.shape[0])).reshape(rows, PACK_COLS)


def _unpack(p, like):
    flat = p.reshape(-1)
    out, off = {}, 0
    for k in SMALL:
        n = like[k].size
        out[k] = flat[off:off + n].reshape(like[k].shape)
        off += n
    return out


def _local_step(x, target, w, small):
    res = []
    for l in range(DEPTH):
        x, r = _layer_fwd(l, x, w, {k: small[k][l] for k in SMALL})
        res.append(r)

    def loss_fn(y, t):
        e = y - t
        return e * (1.0 / DM), jnp.sum(_colsum(0.5 * e * e * (1.0 / DM)), axis=1, keepdims=True)

    dx, loss = _ew("loss", loss_fn, SEQ, 256, [_ri(x), _ri(target)], [], [(DM, F32)], [1])
    bigs, smalls = [], []
    for l in reversed(range(DEPTH)):
        dx, big, sm = _layer_bwd(l, dx, res[l], w, {k: small[k][l] for k in SMALL})
        bigs.append(big)
        smalls.append(sm)
    return loss.reshape(()), dx, bigs[::-1], smalls[::-1]


def _full_weights(gathered):
    g = gathered
    return dict(
        w_in=g["w_in"],
        w_glu=g["w_glu"].transpose(1, 0, 2, 3).reshape(DEPTH, SW, SW),
        w_branch_s=g["w_branch_s"].transpose(1, 2, 0, 3).reshape(DEPTH, SW, DM),
        w_branch_a=g["w_branch_a"].transpose(1, 2, 0, 3).reshape(DEPTH, AW, DM),
        w_out=g["w_out"].transpose(1, 0, 2, 3).reshape(DEPTH, DM, DM),
    )


def kernel(x, pre_norm_g, w_in, lambda_re, lambda_im, log_dt, b_re, b_im, c_re, c_im, d_skip, w_glu, b_glu, w_branch_s, w_branch_a, w_out, post_norm_g, loss_target, m_pre_norm_g, m_w_in, m_lambda_re, m_lambda_im, m_log_dt, m_b_re, m_b_im, m_c_re, m_c_im, m_d_skip, m_w_glu, m_b_glu, m_w_branch_s, m_w_branch_a, m_w_out, m_post_norm_g, v_pre_norm_g, v_w_in, v_lambda_re, v_lambda_im, v_log_dt, v_b_re, v_b_im, v_c_re, v_c_im, v_d_skip, v_w_glu, v_b_glu, v_w_branch_s, v_w_branch_a, v_w_out, v_post_norm_g):
    wts = dict(pre_norm_g=pre_norm_g, w_in=w_in, lambda_re=lambda_re, lambda_im=lambda_im, log_dt=log_dt, b_re=b_re,
               b_im=b_im, c_re=c_re, c_im=c_im, d_skip=d_skip, w_glu=w_glu, b_glu=b_glu, w_branch_s=w_branch_s,
               w_branch_a=w_branch_a, w_out=w_out, post_norm_g=post_norm_g)
    mom = dict(pre_norm_g=m_pre_norm_g, w_in=m_w_in, lambda_re=m_lambda_re, lambda_im=m_lambda_im, log_dt=m_log_dt,
               b_re=m_b_re, b_im=m_b_im, c_re=m_c_re, c_im=m_c_im, d_skip=m_d_skip, w_glu=m_w_glu, b_glu=m_b_glu,
               w_branch_s=m_w_branch_s, w_branch_a=m_w_branch_a, w_out=m_w_out, post_norm_g=m_post_norm_g)
    var = dict(pre_norm_g=v_pre_norm_g, w_in=v_w_in, lambda_re=v_lambda_re, lambda_im=v_lambda_im, log_dt=v_log_dt,
               b_re=v_b_re, b_im=v_b_im, c_re=v_c_re, c_im=v_c_im, d_skip=v_d_skip, w_glu=v_w_glu, b_glu=v_b_glu,
               w_branch_s=v_w_branch_s, w_branch_a=v_w_branch_a, w_out=v_w_out, post_norm_g=v_post_norm_g)

    gathered = _exchange("gather_weights", [wts[k].astype(BF16) for k in BIG], [True] * len(BIG))
    full = _full_weights(dict(zip(BIG, gathered)))

    loss, dx, bigs, smalls = _local_step(x[0], loss_target[0], full, wts)
    loss = lax.psum(loss, ("x", "y", "c"))

    send = [jnp.stack([b[k] for b in bigs], axis=1) for k in BIG]
    packed = _pack({k: jnp.stack([s[k] for s in smalls]) for k in SMALL})
    recv = _exchange("exchange_grads", send + [packed], [False] * len(BIG) + [True])

    grads, delta, new_m, new_v = {}, {}, {}, {}
    for k, parts in zip(BIG, recv[:len(BIG)]):
        shape = wts[k].shape
        cols = shape[-1]
        rows = wts[k].size // cols
        br = min(rows, 256)
        outs = _adamw("adamw_" + k, parts.reshape(NDEV * rows, cols), wts[k].reshape(rows, cols),
                      mom[k].reshape(rows, cols), var[k].reshape(rows, cols), br)
        grads[k], delta[k], new_m[k], new_v[k] = (o.reshape(shape) for o in outs)
    pw, pm, pv = (_pack({k: d[k] for k in SMALL}) for d in (wts, mom, var))
    rows = pw.shape[0]
    outs = _adamw("adamw_small", recv[-1].reshape(NDEV * rows, PACK_COLS), pw, pm, pv, PACK_BR)
    for d, o in zip((grads, delta, new_m, new_v), outs):
        d.update(_unpack(o, wts))

    return (loss, dx[None], *[grads[k] for k in WEIGHTS], *[delta[k] for k in WEIGHTS],
            *[new_m[k] for k in WEIGHTS], *[new_v[k] for k in WEIGHTS])
```

```python
import functools
import math

import jax
import jax.numpy as jnp
from jax import lax
from jax.experimental import pallas as pl
from jax.experimental.pallas import tpu as pltpu

F32 = jnp.float32
BF16 = jnp.bfloat16

NDEV = 8
DEPTH = 4
SEQ = 2048
DM = 1024
NCOL = 8192
SW = 512
NGRP = 32
GCH = 16
NST = 64
NS = NGRP * NST
HD = 128
AW = 512
DILATIONS = (1, 4, 16)
ABLK = 128
RMS_EPS = 1e-6
LR, B1, B2, ADAM_EPS, WD, STEP = 0.001, 0.9, 0.999, 1e-08, 0.01, 10

CB_U, CB_ZS, CB_Q, CB_K, CB_V, CB_ZA = 0, 1, 2, 5, 8, 11
CB_GS, CB_GA = 6, 7

VMEM_LIMIT = 56 * 2 ** 20


def _row_order(j):
    return jnp.where(j < CB_Q, 3, jnp.where(j < CB_ZA, (j - CB_Q) % 3, 0))


def _params(*sem):
    return pltpu.CompilerParams(dimension_semantics=sem, vmem_limit_bytes=VMEM_LIMIT)


def _ew(name, fn, rows, br, row_ins, bc_ins, row_outs, red_outs=()):
    n_in = len(row_ins) + len(bc_ins)
    n_ro = len(row_outs)
    steps = rows // br
    assert steps * br == rows

    def body(*refs):
        vals = fn(*[r[...] for r in refs[:n_in]])
        outs = refs[n_in:]
        for r, v in zip(outs[:n_ro], vals[:n_ro]):
            r[...] = v.astype(r.dtype)
        if red_outs:
            @pl.when(pl.program_id(0) == 0)
            def _():
                for r in outs[n_ro:]:
                    r[...] = jnp.zeros(r.shape, r.dtype)
            for r, v in zip(outs[n_ro:], vals[n_ro:]):
                r[...] += v

    in_specs = []
    for (_, w, cb, rb) in row_ins:
        in_specs.append(pl.BlockSpec((br, w), functools.partial(lambda i, cb, rb: (rb + i, cb), cb=cb, rb=rb)))
    for a in bc_ins:
        in_specs.append(pl.BlockSpec(a.shape, functools.partial(lambda i, nd: (0,) * nd, nd=a.ndim)))
    out_specs = [pl.BlockSpec((br, w), lambda i: (i, 0)) for (w, _) in row_outs]
    out_specs += [pl.BlockSpec((1, w), lambda i: (0, 0)) for w in red_outs]
    out_shape = [jax.ShapeDtypeStruct((rows, w), dt) for (w, dt) in row_outs]
    out_shape += [jax.ShapeDtypeStruct((1, w), F32) for w in red_outs]
    return pl.pallas_call(
        body, name=name, grid=(steps,), in_specs=in_specs, out_specs=out_specs, out_shape=out_shape,
        compiler_params=_params("arbitrary"),
    )(*[a for (a, _, _, _) in row_ins], *bc_ins)


def _ri(a, w=None, cb=0, rb=0):
    return (a, a.shape[1] if w is None else w, cb, rb)


_DIMS = {"nn": ((1,), (0,)), "nt": ((1,), (1,)), "tn": ((0,), (0,))}


def _mm(name, a, b, mode, M, N, K, bm, bn, bk, out_dtype, a_spec=None, b_spec=None, o_spec=None, out_shape=None):
    nk = K // bk
    assert M % bm == 0 and N % bn == 0 and nk * bk == K

    def body(a_ref, b_ref, o_ref, acc_ref):
        k = pl.program_id(2)
        part = lax.dot_general(a_ref[...].astype(BF16), b_ref[...].astype(BF16), (_DIMS[mode], ((), ())),
                               preferred_element_type=F32)

        @pl.when(k == 0)
        def _():
            acc_ref[...] = part

        @pl.when(k > 0)
        def _():
            acc_ref[...] += part

        @pl.when(k == nk - 1)
        def _():
            o_ref[...] = acc_ref[...].astype(o_ref.dtype)

    if a_spec is None:
        a_spec = (pl.BlockSpec((bk, bm), lambda i, j, k: (k, i)) if mode == "tn"
                  else pl.BlockSpec((bm, bk), lambda i, j, k: (i, k)))
    if b_spec is None:
        b_spec = (pl.BlockSpec((bn, bk), lambda i, j, k: (j, k)) if mode == "nt"
                  else pl.BlockSpec((bk, bn), lambda i, j, k: (k, j)))
    if o_spec is None:
        o_spec = pl.BlockSpec((bm, bn), lambda i, j, k: (i, j))
    if out_shape is None:
        out_shape = (M, N)
    return pl.pallas_call(
        body, name=name, grid=(M // bm, N // bn, nk), in_specs=[a_spec, b_spec], out_specs=o_spec,
        out_shape=jax.ShapeDtypeStruct(out_shape, out_dtype),
        scratch_shapes=[pltpu.VMEM((bm, bn), F32)],
        compiler_params=_params("parallel", "parallel", "arbitrary"),
    )(a, b)


SCAN_LANES = 512
SCAN_CHUNKS = 8


def _to_chunked(a):
    return a.reshape(SCAN_CHUNKS, SEQ // SCAN_CHUNKS, -1).transpose(1, 0, 2).reshape(SEQ, -1)


def _from_chunked(a):
    return a.reshape(SEQ // SCAN_CHUNKS, SCAN_CHUNKS, -1).transpose(1, 0, 2).reshape(SEQ, -1)


def _scan(name, d, lam_r, lam_i, reverse):
    T = SEQ // SCAN_CHUNKS
    bl = SCAN_LANES
    nblk = NS // bl
    assert T == 2 ** 8

    def body(dr_ref, di_ref, ar_ref, ai_ref, sr_ref, si_ref):
        ar = jnp.broadcast_to(ar_ref[...], (SCAN_CHUNKS, bl))
        ai = jnp.broadcast_to(ai_ref[...], (SCAN_CHUNKS, bl))
        zero = jnp.zeros((SCAN_CHUNKS, bl), F32)

        def tile(j):
            return pl.ds(pl.multiple_of(j * SCAN_CHUNKS, SCAN_CHUNKS), SCAN_CHUNKS)

        def step(jj, carry):
            sr, si = carry
            j = T - 1 - jj if reverse else jj
            nr = ar * sr - ai * si + dr_ref[tile(j), :]
            ni = ar * si + ai * sr + di_ref[tile(j), :]
            sr_ref[tile(j), :] = nr
            si_ref[tile(j), :] = ni
            return nr, ni

        er, ei = lax.fori_loop(0, T, step, (zero, zero), unroll=4)

        pr, pi = ar[0:1], ai[0:1]
        for _ in range(8):
            pr, pi = pr * pr - pi * pi, 2.0 * pr * pi
        rows = lax.broadcasted_iota(jnp.int32, (SCAN_CHUNKS, bl), 0)
        cr, ci = zero, zero
        xr = jnp.zeros((1, bl), F32)
        xi = jnp.zeros((1, bl), F32)
        order = range(SCAN_CHUNKS - 2, -1, -1) if reverse else range(1, SCAN_CHUNKS)
        for c in order:
            src = c + 1 if reverse else c - 1
            nxr = pr * xr - pi * xi + er[src:src + 1]
            nxi = pr * xi + pi * xr + ei[src:src + 1]
            xr, xi = nxr, nxi
            cr = jnp.where(rows == c, xr, cr)
            ci = jnp.where(rows == c, xi, ci)

        def fix(jj, pw):
            pwr, pwi = pw
            j = T - 1 - jj if reverse else jj
            sr_ref[tile(j), :] = sr_ref[tile(j), :] + (pwr * cr - pwi * ci)
            si_ref[tile(j), :] = si_ref[tile(j), :] + (pwr * ci + pwi * cr)
            return pwr * ar - pwi * ai, pwr * ai + pwi * ar

        lax.fori_loop(0, T, fix, (ar, ai), unroll=4)

    return pl.pallas_call(
        body, name=name, grid=(nblk,),
        in_specs=[pl.BlockSpec((SEQ, bl), lambda i: (0, i)),
                  pl.BlockSpec((SEQ, bl), lambda i: (0, nblk + i)),
                  pl.BlockSpec((1, bl), lambda i: (0, i)),
                  pl.BlockSpec((1, bl), lambda i: (0, i))],
        out_specs=[pl.BlockSpec((SEQ, bl), lambda i: (0, i)),
                   pl.BlockSpec((SEQ, bl), lambda i: (0, i))],
        out_shape=[jax.ShapeDtypeStruct((SEQ, NS), F32), jax.ShapeDtypeStruct((SEQ, NS), F32)],
        compiler_params=_params("arbitrary"),
    )(d, d, lam_r, lam_i)


def _dlam(name, a_r, a_i, s_r, s_i):
    bl = 256

    def prev(s_ref):
        last = pltpu.roll(s_ref[SEQ - SCAN_CHUNKS:SEQ, :], 1, 0)
        first = jnp.where(lax.broadcasted_iota(jnp.int32, (SCAN_CHUNKS, bl), 0) > 0, last, 0.0)
        return jnp.concatenate([first, s_ref[0:SEQ - SCAN_CHUNKS, :]], axis=0)

    def body(ar_ref, ai_ref, sr_ref, si_ref, or_ref, oi_ref):
        spr, spi = prev(sr_ref), prev(si_ref)
        a_r_, a_i_ = ar_ref[...], ai_ref[...]
        or_ref[...] = jnp.sum(a_r_ * spr + a_i_ * spi, axis=0, keepdims=True)
        oi_ref[...] = jnp.sum(a_i_ * spr - a_r_ * spi, axis=0, keepdims=True)

    spec = pl.BlockSpec((SEQ, bl), lambda i: (0, i))
    ospec = pl.BlockSpec((1, bl), lambda i: (0, i))
    return pl.pallas_call(
        body, name=name, grid=(NS // bl,), in_specs=[spec] * 4, out_specs=[ospec, ospec],
        out_shape=[jax.ShapeDtypeStruct((1, NS), F32)] * 2,
        compiler_params=_params("arbitrary"),
    )(a_r, a_i, s_r, s_i)


def _to_residue(a, dil):
    return a.reshape(SEQ // dil, dil, -1).transpose(1, 0, 2).reshape(SEQ, -1)


def _from_residue(a, dil):
    return a.reshape(dil, SEQ // dil, -1).transpose(1, 0, 2).reshape(SEQ, -1)


def _scores(qb, kb, prev):
    s = lax.dot_general(qb, kb, (((1,), (1,)), ((), ())), preferred_element_type=F32) * (HD ** -0.5)
    row = lax.broadcasted_iota(jnp.int32, (ABLK, ABLK), 0)
    col = lax.broadcasted_iota(jnp.int32, (ABLK, ABLK), 1)
    return jnp.where((col >= row) if prev else (col <= row), s, -1e30)


def _rows_of(b):
    if isinstance(b, int):
        return pl.ds(b * ABLK, ABLK)
    return pl.ds(pl.multiple_of(b * ABLK, ABLK), ABLK)


def _attn_fwd(name, proj, gi):
    dil = DILATIONS[gi]
    n = SEQ // dil
    nb = n // ABLK

    def body(q_ref, k_ref, v_ref, o_ref, l_ref):
        def blk(ref, b):
            return ref[_rows_of(b), :]

        def one(b, first):
            qb = blk(q_ref, b).astype(BF16)
            s_c = _scores(qb, blk(k_ref, b).astype(BF16), False)
            m = jnp.max(s_c, axis=-1, keepdims=True)
            if not first:
                s_p = _scores(qb, blk(k_ref, b - 1).astype(BF16), True)
                m = jnp.maximum(m, jnp.max(s_p, axis=-1, keepdims=True))
            p_c = jnp.exp(s_c - m)
            den = jnp.sum(p_c, axis=-1, keepdims=True)
            acc = jnp.dot(p_c.astype(BF16), blk(v_ref, b).astype(BF16), preferred_element_type=F32)
            if not first:
                p_p = jnp.exp(s_p - m)
                den = den + jnp.sum(p_p, axis=-1, keepdims=True)
                acc = acc + jnp.dot(p_p.astype(BF16), blk(v_ref, b - 1).astype(BF16), preferred_element_type=F32)
            rows = _rows_of(b)
            o_ref[rows, :] = acc / den
            l_ref[rows, :] = jnp.broadcast_to(m + jnp.log(den), (ABLK, HD))

        one(0, True)
        if nb > 1:
            def loop(b, c):
                one(b, False)
                return c
            lax.fori_loop(1, nb, loop, 0)

    def spec(cb):
        return pl.BlockSpec((n, HD), functools.partial(lambda r, j, cb: (r, cb * 4 + gi * 4 + j), cb=cb))

    ospec = pl.BlockSpec((n, HD), lambda r, j: (r, j))
    return pl.pallas_call(
        body, name=name, grid=(dil, 4), in_specs=[spec(CB_Q), spec(CB_K), spec(CB_V)], out_specs=[ospec, ospec],
        out_shape=[jax.ShapeDtypeStruct((SEQ, AW), F32)] * 2,
        compiler_params=_params("parallel", "parallel"),
    )(proj, proj, proj)


def _attn_bwd(name, proj, dy, lse, dsum, gi):
    dil = DILATIONS[gi]
    n = SEQ // dil
    nb = n // ABLK

    def body(q_ref, k_ref, v_ref, dy_ref, l_ref, d_ref, dq_ref, dk_ref, dv_ref, dk_acc, dv_acc):
        dk_acc[...] = jnp.zeros(dk_acc.shape, F32)
        dv_acc[...] = jnp.zeros(dv_acc.shape, F32)

        rows_of = _rows_of

        def tn(a, b_):
            return lax.dot_general(a, b_, (((0,), (0,)), ((), ())), preferred_element_type=F32)

        def nt(a, b_):
            return lax.dot_general(a, b_, (((1,), (1,)), ((), ())), preferred_element_type=F32)

        def side(b, kb_idx, prev, qb, dyb, lb, db):
            kb = k_ref[rows_of(kb_idx), :].astype(BF16)
            vb = v_ref[rows_of(kb_idx), :].astype(BF16)
            p = jnp.exp(_scores(qb, kb, prev) - lb)
            ds = p * (nt(dyb, vb) - db) * (HD ** -0.5)
            dsb = ds.astype(BF16)
            dk_acc[rows_of(kb_idx), :] += tn(dsb, qb)
            dv_acc[rows_of(kb_idx), :] += tn(p.astype(BF16), dyb)
            return jnp.dot(dsb, kb, preferred_element_type=F32)

        def one(b, first):
            qb = q_ref[rows_of(b), :].astype(BF16)
            dyb = dy_ref[rows_of(b), :].astype(BF16)
            lb = l_ref[rows_of(b), :][:, 0:1]
            db = d_ref[rows_of(b), :][:, 0:1]
            dq = side(b, b, False, qb, dyb, lb, db)
            if not first:
                dq = dq + side(b, b - 1, True, qb, dyb, lb, db)
            dq_ref[rows_of(b), :] = dq.astype(dq_ref.dtype)

        one(0, True)
        if nb > 1:
            def loop(b, c):
                one(b, False)
                return c
            lax.fori_loop(1, nb, loop, 0)
        dk_ref[...] = dk_acc[...].astype(dk_ref.dtype)
        dv_ref[...] = dv_acc[...].astype(dv_ref.dtype)

    def spec(cb):
        return pl.BlockSpec((n, HD), functools.partial(lambda r, j, cb: (r, cb * 4 + gi * 4 + j), cb=cb))

    ospec = pl.BlockSpec((n, HD), lambda r, j: (r, j))
    return pl.pallas_call(
        body, name=name, grid=(dil, 4),
        in_specs=[spec(CB_Q), spec(CB_K), spec(CB_V), ospec, ospec, ospec], out_specs=[ospec] * 3,
        out_shape=[jax.ShapeDtypeStruct((SEQ, AW), BF16)] * 3,
        scratch_shapes=[pltpu.VMEM((n, HD), F32), pltpu.VMEM((n, HD), F32)],
        compiler_params=_params("parallel", "parallel"),
    )(proj, proj, proj, dy, lse, dsum)


def _rms(x, g):
    return x * lax.rsqrt(jnp.mean(x * x, axis=-1, keepdims=True) + RMS_EPS) * g


def _sig(x):
    return 1.0 / (1.0 + jnp.exp(-x))


def _silu(x):
    return x * _sig(x)


def _gelu(x):
    return 0.5 * x * (1.0 + jnp.tanh(math.sqrt(2.0 / math.pi) * (x + 0.044715 * (x * x * x))))


def _y1_fn(y0p, u, dskip):
    return _gelu(y0p + dskip * u)


def _ys_fn(y1, t, z, bglu):
    return y1 * _sig(t + bglu) * _silu(z)


def _merge_fn(ms, ma, gs, ga):
    return _sig(gs) * ms + _sig(ga) * ma


def _colsum(v):
    return jnp.sum(v, axis=0, keepdims=True)


def _head_sums(v):
    parts = [jnp.broadcast_to(jnp.sum(v[:, j * HD:(j + 1) * HD], axis=-1, keepdims=True), (v.shape[0], HD))
             for j in range(AW // HD)]
    return jnp.concatenate(parts, axis=-1)


def _lam_fn(lre, lim, ldt):
    a = jnp.minimum(lre, -1e-4)
    dt = jnp.exp(ldt)
    mag = jnp.exp(a * dt)
    ar = mag * jnp.cos(lim * dt)
    ai = mag * jnp.sin(lim * dt)
    den = a * a + lim * lim
    cr = ((ar - 1.0) * a + ai * lim) / den
    ci = (ai * a - (ar - 1.0) * lim) / den
    return ar, ai, cr, ci


def _bbar_fn(cr, ci, bre, bim):
    return cr * bre - ci * bim, cr * bim + ci * bre


def _blockdiag(t):
    g, a, b = t.shape
    eye = jnp.eye(g, dtype=bool)[:, None, :, None]
    return jnp.where(eye, t[:, :, None, :], jnp.zeros((), t.dtype)).reshape(g * a, g * b)


def _diagblocks(m, a, b):
    m4 = m.reshape(NGRP, a, NGRP, b)
    idx = jnp.arange(NGRP)
    return m4[idx, :, idx, :]


def _layer_fwd(l, x, w, sp):
    tag = f"l{l}_"
    g1 = sp["pre_norm_g"].reshape(1, DM)
    (h,) = _ew(tag + "rms1", lambda x_, g: (_rms(x_, g),), SEQ, 256, [_ri(x)], [g1], [(DM, BF16)])
    hv = jnp.stack([h, _to_residue(h, DILATIONS[1]), _to_residue(h, DILATIONS[2]), _to_chunked(h)])
    win = w["w_in"]
    proj = _mm(tag + "proj", hv, win, "nn", SEQ, NCOL, DM, 512, 512, 1024, F32,
               a_spec=pl.BlockSpec((None, 512, 1024), lambda i, j, k: (_row_order(j), i, 0)),
               b_spec=pl.BlockSpec((None, None, 1024, 512), lambda i, j, k: (j // 2, l, 0, j % 2)))

    ar, ai, cr, ci = _ew(tag + "lam", _lam_fn, NGRP, NGRP,
                         [_ri(sp["lambda_re"]), _ri(sp["lambda_im"]), _ri(sp["log_dt"].reshape(NGRP, 1))], [],
                         [(NST, F32)] * 4)
    bre = sp["b_re"].reshape(NS, GCH)
    bim = sp["b_im"].reshape(NS, GCH)
    bbr, bbi = _ew(tag + "bbar", _bbar_fn, NS, NS, [_ri(cr.reshape(NS, 1)), _ri(ci.reshape(NS, 1)), _ri(bre), _ri(bim)],
                   [], [(GCH, F32)] * 2)
    wd = jnp.concatenate([_blockdiag(bbr.reshape(NGRP, NST, GCH).transpose(0, 2, 1)),
                          _blockdiag(bbi.reshape(NGRP, NST, GCH).transpose(0, 2, 1))], axis=1).astype(BF16)
    cm = jnp.concatenate([_blockdiag(sp["c_re"].transpose(0, 2, 1)),
                          -_blockdiag(sp["c_im"].transpose(0, 2, 1))], axis=0).astype(BF16)
    drive = _mm(tag + "drive", proj, wd, "nn", SEQ, 2 * NS, SW, 512, 1024, 512, F32,
                a_spec=pl.BlockSpec((512, SW), lambda i, j, k: (i, CB_U)))
    lam_r, lam_i = ar.reshape(1, NS), ai.reshape(1, NS)
    s_r, s_i = _scan(tag + "scan", drive, lam_r, lam_i, False)
    y0p = _mm(tag + "readout_r", s_r, cm, "nn", SEQ, SW, NS, 512, 512, 1024, F32)
    y0p_i = _mm(tag + "readout_i", s_i, cm, "nn", SEQ, SW, NS, 512, 512, 1024, F32,
                b_spec=pl.BlockSpec((1024, 512), lambda i, j, k: (NS // 1024 + k, j)))
    dskip = sp["d_skip"].reshape(1, SW)
    (y1,) = _ew(tag + "y1", lambda a, b_, u, d: (_y1_fn(a + b_, u, d),), SEQ, 256,
                [_ri(y0p), _ri(y0p_i), _ri(proj, SW, CB_U)], [dskip], [(SW, F32)])
    t = _mm(tag + "glu", y1, w["w_glu"][l], "nn", SEQ, SW, SW, 512, 512, 512, F32)
    bglu = sp["b_glu"].reshape(1, SW)
    (ys_c,) = _ew(tag + "ys", lambda y1_, t_, z, b_: (_ys_fn(y1_, t_, z, b_),), SEQ, 256,
                  [_ri(y1), _ri(t), _ri(proj, SW, CB_ZS)], [bglu], [(SW, BF16)])
    ys = _from_chunked(ys_c)

    outs, lses = zip(*[[_from_residue(t_, DILATIONS[gi]) for t_ in _attn_fwd(tag + f"attn{gi}", proj, gi)]
                       for gi in range(3)])

    def comb(o0, o1, o2, l0, l1, l2, z):
        m = jnp.maximum(jnp.maximum(l0, l1), l2)
        e0, e1, e2 = jnp.exp(l0 - m), jnp.exp(l1 - m), jnp.exp(l2 - m)
        tot = e0 + e1 + e2
        ypre = (e0 * o0 + e1 * o1 + e2 * o2) / tot
        return ypre * _silu(z), ypre, m + jnp.log(tot)

    ya, ypre, lse = _ew(tag + "comb", comb, SEQ, 256, [_ri(o) for o in outs] + [_ri(s) for s in lses]
                        + [_ri(proj, AW, CB_ZA)], [], [(AW, BF16), (AW, F32), (AW, F32)])

    ms = _mm(tag + "branch_s", ys, w["w_branch_s"][l], "nn", SEQ, DM, SW, 512, 1024, 512, F32)
    ma = _mm(tag + "branch_a", ya, w["w_branch_a"][l], "nn", SEQ, DM, AW, 512, 1024, 512, F32)
    (merged,) = _ew(tag + "merge", lambda a, b_, c, d: (_merge_fn(a, b_, c, d),), SEQ, 256,
                    [_ri(ms), _ri(ma), _ri(proj, DM, CB_GS), _ri(proj, DM, CB_GA)], [], [(DM, BF16)])
    out = _mm(tag + "out", merged, w["w_out"][l], "nn", SEQ, DM, DM, 512, 1024, 1024, F32)
    g2 = sp["post_norm_g"].reshape(1, DM)
    (x_new,) = _ew(tag + "post", lambda x_, o, g: (x_ + _rms(o, g),), SEQ, 256, [_ri(x), _ri(out)], [g2], [(DM, F32)])
    res = dict(x=x, hv=hv, proj=proj, ar=ar, ai=ai, cr=cr, ci=ci, wd=wd, cm=cm, s_r=s_r, s_i=s_i, y0p=y0p, y0p_i=y0p_i,
               y1=y1, t=t, ys=ys, ya=ya, ypre=ypre, lse=lse, ms=ms, ma=ma, merged=merged, out=out)
    return x_new, res


def _layer_bwd(l, dxn, r, w, sp):
    tag = f"l{l}b_"
    proj = r["proj"]
    g1 = sp["pre_norm_g"].reshape(1, DM)
    g2 = sp["post_norm_g"].reshape(1, DM)
    dskip = sp["d_skip"].reshape(1, SW)
    bglu = sp["b_glu"].reshape(1, SW)

    def post_b(d, o, g):
        _, vjp = jax.vjp(_rms, o, g)
        do, dg = vjp(d)
        return do, dg

    d_out, dg2 = _ew(tag + "post", post_b, SEQ, 256, [_ri(dxn), _ri(r["out"])], [g2], [(DM, BF16)], [DM])
    dw_out = _mm(tag + "dw_out", r["merged"], d_out, "tn", DM, DM, SEQ, 512, 1024, 1024, BF16)
    dmerged = _mm(tag + "dmerged", d_out, w["w_out"][l], "nt", SEQ, DM, DM, 512, 1024, 1024, F32)

    def merge_b(d, ms, ma, gs, ga):
        _, vjp = jax.vjp(_merge_fn, ms, ma, gs, ga)
        return vjp(d)

    dms, dma, dgs, dga = _ew(tag + "merge", merge_b, SEQ, 256,
                             [_ri(dmerged), _ri(r["ms"]), _ri(r["ma"]), _ri(proj, DM, CB_GS), _ri(proj, DM, CB_GA)],
                             [], [(DM, BF16)] * 4)
    dw_bs = _mm(tag + "dw_bs", r["ys"], dms, "tn", SW, DM, SEQ, 512, 1024, 1024, BF16)
    dw_ba = _mm(tag + "dw_ba", r["ya"], dma, "tn", AW, DM, SEQ, 512, 1024, 1024, BF16)
    dys = _mm(tag + "dys", dms, w["w_branch_s"][l], "nt", SEQ, SW, DM, 512, 512, 1024, F32)
    dya = _mm(tag + "dya", dma, w["w_branch_a"][l], "nt", SEQ, AW, DM, 512, 512, 1024, F32)

    def comb_b(d, ypre, z):
        _, vjp = jax.vjp(lambda y, z_: y * _silu(z_), ypre, z)
        dyp, dz = vjp(d)
        return dyp, dz, _head_sums(dyp * ypre)

    dyp, dza, dsum = _ew(tag + "comb", comb_b, SEQ, 256, [_ri(dya), _ri(r["ypre"]), _ri(proj, AW, CB_ZA)], [],
                         [(AW, F32), (AW, BF16), (AW, F32)])
    dqkv = [_attn_bwd(tag + f"attn{gi}", proj, *[_to_residue(t_, DILATIONS[gi]) for t_ in (dyp, r["lse"], dsum)], gi)
            for gi in range(3)]

    def ys_b(d, y1, t, z, b_):
        _, vjp = jax.vjp(_ys_fn, y1, t, z, b_)
        dy1, dt, dz, _ = vjp(d)
        return dy1, dt, dz, _colsum(dt)

    dy1a, dt, dzs, dbglu = _ew(tag + "ys", ys_b, SEQ, 256,
                               [_ri(_to_chunked(dys)), _ri(r["y1"]), _ri(r["t"]), _ri(proj, SW, CB_ZS)],
                               [bglu], [(SW, F32), (SW, BF16), (SW, BF16)], [SW])
    dw_glu = _mm(tag + "dw_glu", r["y1"], dt, "tn", SW, SW, SEQ, 512, 512, 1024, BF16)
    dy1b = _mm(tag + "dy1b", dt, w["w_glu"][l], "nt", SEQ, SW, SW, 512, 512, 512, F32)

    def y1_b(da, db, y0p, y0p_i, u, d_):
        _, vjp = jax.vjp(_y1_fn, y0p + y0p_i, u, d_)
        dy0, du, dd = vjp(da + db)
        return dy0, du, dd

    dy0, du_skip, ddskip = _ew(tag + "y1", y1_b, SEQ, 256,
                               [_ri(dy1a), _ri(dy1b), _ri(r["y0p"]), _ri(r["y0p_i"]), _ri(proj, SW, CB_U)], [dskip],
                               [(SW, BF16), (SW, F32)], [SW])
    g_s = _mm(tag + "dstate", dy0, r["cm"], "nt", SEQ, 2 * NS, SW, 512, 1024, 512, F32)
    dcm_r = _mm(tag + "dcm_r", r["s_r"], dy0, "tn", NS, SW, SEQ, 1024, 512, 1024, F32)
    dcm_i = _mm(tag + "dcm_i", r["s_i"], dy0, "tn", NS, SW, SEQ, 1024, 512, 1024, F32)
    lam_r, lam_i = r["ar"].reshape(1, NS), r["ai"].reshape(1, NS)
    a_r, a_i = _scan(tag + "scan", g_s, lam_r, -lam_i, True)
    dlr, dli = _dlam(tag + "dlam", a_r, a_i, r["s_r"], r["s_i"])
    u_spec = pl.BlockSpec((1024, SW), lambda i, j, k: (k, CB_U))
    dwd_r = _mm(tag + "dwd_r", proj, a_r, "tn", SW, NS, SEQ, 512, 1024, 1024, F32, a_spec=u_spec)
    dwd_i = _mm(tag + "dwd_i", proj, a_i, "tn", SW, NS, SEQ, 512, 1024, 1024, F32, a_spec=u_spec)
    du_r = _mm(tag + "du_r", a_r, r["wd"], "nt", SEQ, SW, NS, 512, 512, 1024, F32)
    du_i = _mm(tag + "du_i", a_i, r["wd"], "nt", SEQ, SW, NS, 512, 512, 1024, F32,
               b_spec=pl.BlockSpec((512, 1024), lambda i, j, k: (j, NS // 1024 + k)))
    (du,) = _ew(tag + "du", lambda a, b_, c: (a + b_ + c,), SEQ, 256, [_ri(du_r), _ri(du_i), _ri(du_skip)], [],
                [(SW, BF16)])

    dbbr = _diagblocks(dwd_r, GCH, NST).transpose(0, 2, 1).reshape(NS, GCH)
    dbbi = _diagblocks(dwd_i, GCH, NST).transpose(0, 2, 1).reshape(NS, GCH)
    bre = sp["b_re"].reshape(NS, GCH)
    bim = sp["b_im"].reshape(NS, GCH)

    def bbar_b(cr, ci, br_, bi_, dr, di):
        _, vjp = jax.vjp(_bbar_fn, cr, ci, br_, bi_)
        return vjp((dr, di))

    dcr, dci, dbre, dbim = _ew(tag + "bbar", bbar_b, NS, NS,
                               [_ri(r["cr"].reshape(NS, 1)), _ri(r["ci"].reshape(NS, 1)), _ri(bre), _ri(bim),
                                _ri(dbbr), _ri(dbbi)], [], [(1, F32), (1, F32), (GCH, F32), (GCH, F32)])

    def lam_b(lre, lim, ldt, dar, dai, dcr_, dci_):
        _, vjp = jax.vjp(_lam_fn, lre, lim, ldt)
        return vjp((dar, dai, dcr_, dci_))

    dlre, dlim, dldt = _ew(tag + "lam", lam_b, NGRP, NGRP,
                           [_ri(sp["lambda_re"]), _ri(sp["lambda_im"]), _ri(sp["log_dt"].reshape(NGRP, 1)),
                            _ri(dlr.reshape(NGRP, NST)), _ri(dli.reshape(NGRP, NST)),
                            _ri(dcr.reshape(NGRP, NST)), _ri(dci.reshape(NGRP, NST))], [],
                           [(NST, F32), (NST, F32), (1, F32)])
    dc_re = _diagblocks(dcm_r, NST, GCH).transpose(0, 2, 1)
    dc_im = -_diagblocks(dcm_i, NST, GCH).transpose(0, 2, 1)

    dq, dk, dv = ([d[i] for d in dqkv] for i in range(3))
    dproj = jnp.concatenate([du, dzs, *dq, *dk, *dv, dza, dgs, dga], axis=1)
    dw_in = _mm(tag + "dw_in", r["hv"], dproj, "tn", DM, NCOL, SEQ, 1024, 512, 1024, BF16,
                a_spec=pl.BlockSpec((None, 1024, 1024), lambda i, j, k: (_row_order(j), k, 0)),
                o_spec=pl.BlockSpec((None, 1024, 512), lambda i, j, k: (j // 2, 0, j % 2)), out_shape=(NDEV, DM, DM))

    blocks = (lambda k: jnp.where(k < 3, CB_Q + 3 * k, CB_ZA - 3 + k), lambda k: CB_Q + 1 + 3 * k,
              lambda k: CB_Q + 2 + 3 * k, lambda k: k)
    counts = (8, 3, 3, 2)
    dh = []
    for o in range(4):
        cb = blocks[o]
        dh.append(_mm(tag + f"dh{o}", dproj, w["w_in"], "nt", SEQ, DM, counts[o] * 512, 512, 1024, 512, F32,
                      a_spec=pl.BlockSpec((512, 512), functools.partial(lambda i, j, k, cb: (i, cb(k)), cb=cb)),
                      b_spec=pl.BlockSpec((None, None, 1024, 512),
                                          functools.partial(lambda i, j, k, cb: (cb(k) // 2, l, 0, cb(k) % 2), cb=cb))))
    dh = [dh[0], _from_residue(dh[1], DILATIONS[1]), _from_residue(dh[2], DILATIONS[2]), _from_chunked(dh[3])]

    def pre_b(d, dh0, dh1, dh2, dh3, x_, g):
        _, vjp = jax.vjp(_rms, x_, g)
        dx_, dg = vjp((dh0 + dh1) + (dh2 + dh3))
        return d + dx_, dg

    dx, dg1 = _ew(tag + "pre", pre_b, SEQ, 256, [_ri(dxn)] + [_ri(t_) for t_ in dh] + [_ri(r["x"])], [g1],
                  [(DM, F32)], [DM])

    big = dict(w_in=dw_in, w_glu=dw_glu.reshape(NDEV, SW // NDEV, SW),
               w_branch_s=dw_bs.reshape(SW, NDEV, DM // NDEV).transpose(1, 0, 2),
               w_branch_a=dw_ba.reshape(AW, NDEV, DM // NDEV).transpose(1, 0, 2),
               w_out=dw_out.reshape(NDEV, DM // NDEV, DM))
    small = dict(pre_norm_g=dg1.reshape(DM), lambda_re=dlre, lambda_im=dlim, log_dt=dldt.reshape(NGRP),
                 b_re=dbre.reshape(NGRP, NST, GCH), b_im=dbim.reshape(NGRP, NST, GCH), c_re=dc_re, c_im=dc_im,
                 d_skip=ddskip.reshape(SW), b_glu=dbglu.reshape(SW), post_norm_g=dg2.reshape(DM))
    return dx, big, small


def _exchange(name, arrs, gather):
    n = len(arrs)

    def body(*refs):
        srcs, dsts = refs[:n], refs[n:2 * n]
        send_sems, recv_sems, local_sems = refs[2 * n:]
        x, y, c = lax.axis_index("x"), lax.axis_index("y"), lax.axis_index("c")
        me = 4 * x + 2 * y + c
        copies = []
        for i in range(n):
            mine = srcs[i] if gather[i] else srcs[i].at[me]
            loc = pltpu.make_async_copy(mine, dsts[i].at[me], local_sems.at[i])
            loc.start()
            copies.append(loc)
            for k in range(1, NDEV):
                peer = (x ^ (k >> 2), y ^ ((k >> 1) & 1), c ^ (k & 1))
                src = srcs[i] if gather[i] else srcs[i].at[me ^ k]
                cp = pltpu.make_async_remote_copy(src_ref=src, dst_ref=dsts[i].at[me], send_sem=send_sems.at[i, k],
                                                  recv_sem=recv_sems.at[i, k], device_id=peer,
                                                  device_id_type=pl.DeviceIdType.MESH)
                cp.start()
                copies.append(cp)
        for cp in copies:
            cp.wait()

    out_shape = [jax.ShapeDtypeStruct(((NDEV,) + a.shape) if g else a.shape, a.dtype) for a, g in zip(arrs, gather)]
    any_spec = pl.BlockSpec(memory_space=pl.ANY)
    return pl.pallas_call(
        body, name=name, in_specs=[any_spec] * n, out_specs=[any_spec] * n, out_shape=out_shape,
        scratch_shapes=[pltpu.SemaphoreType.DMA((n, NDEV)), pltpu.SemaphoreType.DMA((n, NDEV)),
                        pltpu.SemaphoreType.DMA((n,))],
    )(*arrs)


def _adamw(name, parts, w, m, v, br):
    rows, cols = w.shape

    def fn(*a):
        g = a[0].astype(F32)
        for p in a[1:NDEV]:
            g = g + p.astype(F32)
        w_, m_, v_ = a[NDEV:]
        m2 = B1 * m_ + (1.0 - B1) * g
        v2 = B2 * v_ + (1.0 - B2) * (g * g)
        m_hat = m2 / (1.0 - B1 ** STEP)
        v_hat = v2 / (1.0 - B2 ** STEP)
        delta = -LR * (m_hat / (jnp.sqrt(v_hat) + ADAM_EPS) + WD * w_)
        return g, delta, m2, v2

    ins = [_ri(parts, cols, 0, d * (rows // br)) for d in range(NDEV)] + [_ri(w), _ri(m), _ri(v)]
    return _ew(name, fn, rows, br, ins, [], [(cols, F32)] * 4)


SMALL = ("pre_norm_g", "lambda_re", "lambda_im", "log_dt", "b_re", "b_im", "c_re", "c_im", "d_skip", "b_glu",
         "post_norm_g")
BIG = ("w_in", "w_glu", "w_branch_s", "w_branch_a", "w_out")
WEIGHTS = ("pre_norm_g", "w_in", "lambda_re", "lambda_im", "log_dt", "b_re", "b_im", "c_re", "c_im", "d_skip",
           "w_glu", "b_glu", "w_branch_s", "w_branch_a", "w_out", "post_norm_g")
PACK_COLS = 1024
PACK_BR = 136


def _pack(d):
    flat = jnp.concatenate([d[k].astype(F32).reshape(-1) for k in SMALL])
    rows = -(-flat.shape[0] // (PACK_COLS * PACK_BR)) * PACK_BR
    return jnp.pad(flat, (0, rows * PACK_COLS - flat.shape[0])).reshape(rows, PACK_COLS)


def _unpack(p, like):
    flat = p.reshape(-1)
    out, off = {}, 0
    for k in SMALL:
        n = like[k].size
        out[k] = flat[off:off + n].reshape(like[k].shape)
        off += n
    return out


def _local_step(x, target, w, small):
    res = []
    for l in range(DEPTH):
        x, r = _layer_fwd(l, x, w, {k: small[k][l] for k in SMALL})
        res.append(r)

    def loss_fn(y, t):
        e = y - t
        return e * (1.0 / DM), jnp.sum(_colsum(0.5 * e * e * (1.0 / DM)), axis=1, keepdims=True)

    dx, loss = _ew("loss", loss_fn, SEQ, 256, [_ri(x), _ri(target)], [], [(DM, F32)], [1])
    bigs, smalls = [], []
    for l in reversed(range(DEPTH)):
        dx, big, sm = _layer_bwd(l, dx, res[l], w, {k: small[k][l] for k in SMALL})
        bigs.append(big)
        smalls.append(sm)
    return loss.reshape(()), dx, bigs[::-1], smalls[::-1]


def _full_weights(gathered):
    g = gathered
    return dict(
        w_in=g["w_in"],
        w_glu=g["w_glu"].transpose(1, 0, 2, 3).reshape(DEPTH, SW, SW),
        w_branch_s=g["w_branch_s"].transpose(1, 2, 0, 3).reshape(DEPTH, SW, DM),
        w_branch_a=g["w_branch_a"].transpose(1, 2, 0, 3).reshape(DEPTH, AW, DM),
        w_out=g["w_out"].transpose(1, 0, 2, 3).reshape(DEPTH, DM, DM),
    )


def kernel(x, pre_norm_g, w_in, lambda_re, lambda_im, log_dt, b_re, b_im, c_re, c_im, d_skip, w_glu, b_glu, w_branch_s, w_branch_a, w_out, post_norm_g, loss_target, m_pre_norm_g, m_w_in, m_lambda_re, m_lambda_im, m_log_dt, m_b_re, m_b_im, m_c_re, m_c_im, m_d_skip, m_w_glu, m_b_glu, m_w_branch_s, m_w_branch_a, m_w_out, m_post_norm_g, v_pre_norm_g, v_w_in, v_lambda_re, v_lambda_im, v_log_dt, v_b_re, v_b_im, v_c_re, v_c_im, v_d_skip, v_w_glu, v_b_glu, v_w_branch_s, v_w_branch_a, v_w_out, v_post_norm_g):
    wts = dict(pre_norm_g=pre_norm_g, w_in=w_in, lambda_re=lambda_re, lambda_im=lambda_im, log_dt=log_dt, b_re=b_re,
               b_im=b_im, c_re=c_re, c_im=c_im, d_skip=d_skip, w_glu=w_glu, b_glu=b_glu, w_branch_s=w_branch_s,
               w_branch_a=w_branch_a, w_out=w_out, post_norm_g=post_norm_g)
    mom = dict(pre_norm_g=m_pre_norm_g, w_in=m_w_in, lambda_re=m_lambda_re, lambda_im=m_lambda_im, log_dt=m_log_dt,
               b_re=m_b_re, b_im=m_b_im, c_re=m_c_re, c_im=m_c_im, d_skip=m_d_skip, w_glu=m_w_glu, b_glu=m_b_glu,
               w_branch_s=m_w_branch_s, w_branch_a=m_w_branch_a, w_out=m_w_out, post_norm_g=m_post_norm_g)
    var = dict(pre_norm_g=v_pre_norm_g, w_in=v_w_in, lambda_re=v_lambda_re, lambda_im=v_lambda_im, log_dt=v_log_dt,
               b_re=v_b_re, b_im=v_b_im, c_re=v_c_re, c_im=v_c_im, d_skip=v_d_skip, w_glu=v_w_glu, b_glu=v_b_glu,
               w_branch_s=v_w_branch_s, w_branch_a=v_w_branch_a, w_out=v_w_out, post_norm_g=v_post_norm_g)

    gathered = _exchange("gather_weights", [wts[k].astype(BF16) for k in BIG], [True] * len(BIG))
    full = _full_weights(dict(zip(BIG, gathered)))

    loss, dx, bigs, smalls = _local_step(x[0], loss_target[0], full, wts)
    loss = lax.psum(loss, ("x", "y", "c"))

    send = [jnp.stack([b[k] for b in bigs], axis=1) for k in BIG]
    packed = _pack({k: jnp.stack([s[k] for s in smalls]) for k in SMALL})
    recv = _exchange("exchange_grads", send + [packed], [False] * len(BIG) + [True])

    grads, delta, new_m, new_v = {}, {}, {}, {}
    for k, parts in zip(BIG, recv[:len(BIG)]):
        shape = wts[k].shape
        cols = shape[-1]
        rows = wts[k].size // cols
        br = min(rows, 256)
        outs = _adamw("adamw_" + k, parts.reshape(NDEV * rows, cols), wts[k].reshape(rows, cols),
                      mom[k].reshape(rows, cols), var[k].reshape(rows, cols), br)
        grads[k], delta[k], new_m[k], new_v[k] = (o.reshape(shape) for o in outs)
    pw, pm, pv = (_pack({k: d[k] for k in SMALL}) for d in (wts, mom, var))
    rows = pw.shape[0]
    outs = _adamw("adamw_small", recv[-1].reshape(NDEV * rows, PACK_COLS), pw, pm, pv, PACK_BR)
    for d, o in zip((grads, delta, new_m, new_v), outs):
        d.update(_unpack(o, wts))

    return (loss, dx[None], *[grads[k] for k in WEIGHTS], *[delta[k] for k in WEIGHTS],
            *[new_m[k] for k in WEIGHTS], *[new_v[k] for k in WEIGHTS])
```

```python
import functools
import math

import jax
import jax.numpy as jnp
from jax import lax
from jax.experimental import pallas as pl
from jax.experimental.pallas import tpu as pltpu

F32 = jnp.float32
BF16 = jnp.bfloat16

NDEV = 8
DEPTH = 4
SEQ = 2048
DM = 1024
NCOL = 8192
SW = 512
NGRP = 32
GCH = 16
NST = 64
NS = NGRP * NST
HD = 128
AW = 512
DILATIONS = (1, 4, 16)
ABLK = 128
RMS_EPS = 1e-6
LR, B1, B2, ADAM_EPS, WD, STEP = 0.001, 0.9, 0.999, 1e-08, 0.01, 10

CB_U, CB_ZS, CB_Q, CB_K, CB_V, CB_ZA = 0, 1, 2, 5, 8, 11
CB_GS, CB_GA = 6, 7

VMEM_LIMIT = 56 * 2 ** 20


def _row_order(j):
    return jnp.where(j < CB_Q, 3, jnp.where(j < CB_ZA, (j - CB_Q) % 3, 0))


def _params(*sem):
    return pltpu.CompilerParams(dimension_semantics=sem, vmem_limit_bytes=VMEM_LIMIT)


def _ew(name, fn, rows, br, row_ins, bc_ins, row_outs, red_outs=()):
    n_in = len(row_ins) + len(bc_ins)
    n_ro = len(row_outs)
    steps = rows // br
    assert steps * br == rows

    def body(*refs):
        vals = fn(*[r[...] for r in refs[:n_in]])
        outs = refs[n_in:]
        for r, v in zip(outs[:n_ro], vals[:n_ro]):
            r[...] = v.astype(r.dtype)
        if red_outs:
            @pl.when(pl.program_id(0) == 0)
            def _():
                for r in outs[n_ro:]:
                    r[...] = jnp.zeros(r.shape, r.dtype)
            for r, v in zip(outs[n_ro:], vals[n_ro:]):
                r[...] += v

    in_specs = []
    for (_, w, cb, rb) in row_ins:
        in_specs.append(pl.BlockSpec((br, w), functools.partial(lambda i, cb, rb: (rb + i, cb), cb=cb, rb=rb)))
    for a in bc_ins:
        in_specs.append(pl.BlockSpec(a.shape, functools.partial(lambda i, nd: (0,) * nd, nd=a.ndim)))
    out_specs = [pl.BlockSpec((br, w), lambda i: (i, 0)) for (w, _) in row_outs]
    out_specs += [pl.BlockSpec((1, w), lambda i: (0, 0)) for w in red_outs]
    out_shape = [jax.ShapeDtypeStruct((rows, w), dt) for (w, dt) in row_outs]
    out_shape += [jax.ShapeDtypeStruct((1, w), F32) for w in red_outs]
    return pl.pallas_call(
        body, name=name, grid=(steps,), in_specs=in_specs, out_specs=out_specs, out_shape=out_shape,
        compiler_params=_params("arbitrary"),
    )(*[a for (a, _, _, _) in row_ins], *bc_ins)


def _ri(a, w=None, cb=0, rb=0):
    return (a, a.shape[1] if w is None else w, cb, rb)


_DIMS = {"nn": ((1,), (0,)), "nt": ((1,), (1,)), "tn": ((0,), (0,))}


def _mm(name, a, b, mode, M, N, K, bm, bn, bk, out_dtype, a_spec=None, b_spec=None, o_spec=None, out_shape=None):
    nk = K // bk
    assert M % bm == 0 and N % bn == 0 and nk * bk == K

    own_acc = nk > 1 and out_dtype != F32

    def body(a_ref, b_ref, o_ref, *scratch):
        part = lax.dot_general(a_ref[...].astype(BF16), b_ref[...].astype(BF16), (_DIMS[mode], ((), ())),
                               preferred_element_type=F32)
        if nk == 1:
            o_ref[...] = part.astype(o_ref.dtype)
            return
        k = pl.program_id(2)
        acc_ref = scratch[0] if own_acc else o_ref

        @pl.when(k == 0)
        def _():
            acc_ref[...] = part

        @pl.when(k > 0)
        def _():
            acc_ref[...] += part

        if own_acc:
            @pl.when(k == nk - 1)
            def _():
                o_ref[...] = acc_ref[...].astype(o_ref.dtype)

    if a_spec is None:
        a_spec = (pl.BlockSpec((bk, bm), lambda i, j, k: (k, i)) if mode == "tn"
                  else pl.BlockSpec((bm, bk), lambda i, j, k: (i, k)))
    if b_spec is None:
        b_spec = (pl.BlockSpec((bn, bk), lambda i, j, k: (j, k)) if mode == "nt"
                  else pl.BlockSpec((bk, bn), lambda i, j, k: (k, j)))
    if o_spec is None:
        o_spec = pl.BlockSpec((bm, bn), lambda i, j, k: (i, j))
    if out_shape is None:
        out_shape = (M, N)
    return pl.pallas_call(
        body, name=name, grid=(M // bm, N // bn, nk), in_specs=[a_spec, b_spec], out_specs=o_spec,
        out_shape=jax.ShapeDtypeStruct(out_shape, out_dtype),
        scratch_shapes=[pltpu.VMEM((bm, bn), F32)] if own_acc else [],
        compiler_params=_params("parallel", "parallel", "arbitrary"),
    )(a, b)


SCAN_LANES = 512
SCAN_CHUNKS = 8


def _to_chunked(a):
    return a.reshape(SCAN_CHUNKS, SEQ // SCAN_CHUNKS, -1).transpose(1, 0, 2).reshape(SEQ, -1)


def _from_chunked(a):
    return a.reshape(SEQ // SCAN_CHUNKS, SCAN_CHUNKS, -1).transpose(1, 0, 2).reshape(SEQ, -1)


def _scan(name, d, lam_r, lam_i, reverse):
    T = SEQ // SCAN_CHUNKS
    bl = SCAN_LANES
    nblk = NS // bl
    assert T == 2 ** 8

    def body(dr_ref, di_ref, ar_ref, ai_ref, sr_ref, si_ref):
        ar = jnp.broadcast_to(ar_ref[...], (SCAN_CHUNKS, bl))
        ai = jnp.broadcast_to(ai_ref[...], (SCAN_CHUNKS, bl))
        zero = jnp.zeros((SCAN_CHUNKS, bl), F32)

        def tile(j):
            return pl.ds(pl.multiple_of(j * SCAN_CHUNKS, SCAN_CHUNKS), SCAN_CHUNKS)

        def step(jj, carry):
            sr, si = carry
            j = T - 1 - jj if reverse else jj
            nr = ar * sr - ai * si + dr_ref[tile(j), :]
            ni = ar * si + ai * sr + di_ref[tile(j), :]
            sr_ref[tile(j), :] = nr
            si_ref[tile(j), :] = ni
            return nr, ni

        er, ei = lax.fori_loop(0, T, step, (zero, zero), unroll=4)

        pr, pi = ar[0:1], ai[0:1]
        for _ in range(8):
            pr, pi = pr * pr - pi * pi, 2.0 * pr * pi
        rows = lax.broadcasted_iota(jnp.int32, (SCAN_CHUNKS, bl), 0)
        cr, ci = zero, zero
        xr = jnp.zeros((1, bl), F32)
        xi = jnp.zeros((1, bl), F32)
        order = range(SCAN_CHUNKS - 2, -1, -1) if reverse else range(1, SCAN_CHUNKS)
        for c in order:
            src = c + 1 if reverse else c - 1
            nxr = pr * xr - pi * xi + er[src:src + 1]
            nxi = pr * xi + pi * xr + ei[src:src + 1]
            xr, xi = nxr, nxi
            cr = jnp.where(rows == c, xr, cr)
            ci = jnp.where(rows == c, xi, ci)

        def fix(jj, pw):
            pwr, pwi = pw
            j = T - 1 - jj if reverse else jj
            sr_ref[tile(j), :] = sr_ref[tile(j), :] + (pwr * cr - pwi * ci)
            si_ref[tile(j), :] = si_ref[tile(j), :] + (pwr * ci + pwi * cr)
            return pwr * ar - pwi * ai, pwr * ai + pwi * ar

        lax.fori_loop(0, T, fix, (ar, ai), unroll=4)

    return pl.pallas_call(
        body, name=name, grid=(nblk,),
        in_specs=[pl.BlockSpec((SEQ, bl), lambda i: (0, i)),
                  pl.BlockSpec((SEQ, bl), lambda i: (0, nblk + i)),
                  pl.BlockSpec((1, bl), lambda i: (0, i)),
                  pl.BlockSpec((1, bl), lambda i: (0, i))],
        out_specs=[pl.BlockSpec((SEQ, bl), lambda i: (0, i)),
                   pl.BlockSpec((SEQ, bl), lambda i: (0, i))],
        out_shape=[jax.ShapeDtypeStruct((SEQ, NS), F32), jax.ShapeDtypeStruct((SEQ, NS), F32)],
        compiler_params=_params("arbitrary"),
    )(d, d, lam_r, lam_i)


def _dlam(name, a_r, a_i, s_r, s_i):
    bl = 256

    def prev(s_ref):
        last = pltpu.roll(s_ref[SEQ - SCAN_CHUNKS:SEQ, :], 1, 0)
        first = jnp.where(lax.broadcasted_iota(jnp.int32, (SCAN_CHUNKS, bl), 0) > 0, last, 0.0)
        return jnp.concatenate([first, s_ref[0:SEQ - SCAN_CHUNKS, :]], axis=0)

    def body(ar_ref, ai_ref, sr_ref, si_ref, or_ref, oi_ref):
        spr, spi = prev(sr_ref), prev(si_ref)
        a_r_, a_i_ = ar_ref[...], ai_ref[...]
        or_ref[...] = jnp.sum(a_r_ * spr + a_i_ * spi, axis=0, keepdims=True)
        oi_ref[...] = jnp.sum(a_i_ * spr - a_r_ * spi, axis=0, keepdims=True)

    spec = pl.BlockSpec((SEQ, bl), lambda i: (0, i))
    ospec = pl.BlockSpec((1, bl), lambda i: (0, i))
    return pl.pallas_call(
        body, name=name, grid=(NS // bl,), in_specs=[spec] * 4, out_specs=[ospec, ospec],
        out_shape=[jax.ShapeDtypeStruct((1, NS), F32)] * 2,
        compiler_params=_params("arbitrary"),
    )(a_r, a_i, s_r, s_i)


def _to_residue(a, dil):
    return a.reshape(SEQ // dil, dil, -1).transpose(1, 0, 2).reshape(SEQ, -1)


def _from_residue(a, dil):
    return a.reshape(dil, SEQ // dil, -1).transpose(1, 0, 2).reshape(SEQ, -1)


def _scores(qb, kb, prev):
    s = lax.dot_general(qb, kb, (((1,), (1,)), ((), ())), preferred_element_type=F32) * (HD ** -0.5)
    row = lax.broadcasted_iota(jnp.int32, (ABLK, ABLK), 0)
    col = lax.broadcasted_iota(jnp.int32, (ABLK, ABLK), 1)
    return jnp.where((col >= row) if prev else (col <= row), s, -1e30)


def _rows_of(b):
    if isinstance(b, int):
        return pl.ds(b * ABLK, ABLK)
    return pl.ds(pl.multiple_of(b * ABLK, ABLK), ABLK)


def _attn_fwd(name, proj, gi):
    dil = DILATIONS[gi]
    n = SEQ // dil
    nb = n // ABLK

    def body(q_ref, k_ref, v_ref, o_ref, l_ref):
        def blk(ref, b):
            return ref[_rows_of(b), :]

        def one(b, first):
            qb = blk(q_ref, b).astype(BF16)
            s_c = _scores(qb, blk(k_ref, b).astype(BF16), False)
            m = jnp.max(s_c, axis=-1, keepdims=True)
            if not first:
                s_p = _scores(qb, blk(k_ref, b - 1).astype(BF16), True)
                m = jnp.maximum(m, jnp.max(s_p, axis=-1, keepdims=True))
            p_c = jnp.exp(s_c - m)
            den = jnp.sum(p_c, axis=-1, keepdims=True)
            acc = jnp.dot(p_c.astype(BF16), blk(v_ref, b).astype(BF16), preferred_element_type=F32)
            if not first:
                p_p = jnp.exp(s_p - m)
                den = den + jnp.sum(p_p, axis=-1, keepdims=True)
                acc = acc + jnp.dot(p_p.astype(BF16), blk(v_ref, b - 1).astype(BF16), preferred_element_type=F32)
            rows = _rows_of(b)
            o_ref[rows, :] = acc / den
            l_ref[rows, :] = jnp.broadcast_to(m + jnp.log(den), (ABLK, HD))

        one(0, True)
        if nb > 1:
            def loop(b, c):
                one(b, False)
                return c
            lax.fori_loop(1, nb, loop, 0)

    def spec(cb):
        return pl.BlockSpec((n, HD), functools.partial(lambda r, j, cb: (r, cb * 4 + gi * 4 + j), cb=cb))

    ospec = pl.BlockSpec((n, HD), lambda r, j: (r, j))
    return pl.pallas_call(
        body, name=name, grid=(dil, 4), in_specs=[spec(CB_Q), spec(CB_K), spec(CB_V)], out_specs=[ospec, ospec],
        out_shape=[jax.ShapeDtypeStruct((SEQ, AW), F32)] * 2,
        compiler_params=_params("parallel", "parallel"),
    )(proj, proj, proj)


def _attn_bwd(name, proj, dy, lse, dsum, gi):
    dil = DILATIONS[gi]
    n = SEQ // dil
    nb = n // ABLK

    def body(q_ref, k_ref, v_ref, dy_ref, l_ref, d_ref, dq_ref, dk_ref, dv_ref, dk_acc, dv_acc):
        dk_acc[...] = jnp.zeros(dk_acc.shape, F32)
        dv_acc[...] = jnp.zeros(dv_acc.shape, F32)

        rows_of = _rows_of

        def tn(a, b_):
            return lax.dot_general(a, b_, (((0,), (0,)), ((), ())), preferred_element_type=F32)

        def nt(a, b_):
            return lax.dot_general(a, b_, (((1,), (1,)), ((), ())), preferred_element_type=F32)

        def side(b, kb_idx, prev, qb, dyb, lb, db):
            kb = k_ref[rows_of(kb_idx), :].astype(BF16)
            vb = v_ref[rows_of(kb_idx), :].astype(BF16)
            p = jnp.exp(_scores(qb, kb, prev) - lb)
            ds = p * (nt(dyb, vb) - db) * (HD ** -0.5)
            dsb = ds.astype(BF16)
            dk_acc[rows_of(kb_idx), :] += tn(dsb, qb)
            dv_acc[rows_of(kb_idx), :] += tn(p.astype(BF16), dyb)
            return jnp.dot(dsb, kb, preferred_element_type=F32)

        def one(b, first):
            qb = q_ref[rows_of(b), :].astype(BF16)
            dyb = dy_ref[rows_of(b), :].astype(BF16)
            lb = l_ref[rows_of(b), :][:, 0:1]
            db = d_ref[rows_of(b), :][:, 0:1]
            dq = side(b, b, False, qb, dyb, lb, db)
            if not first:
                dq = dq + side(b, b - 1, True, qb, dyb, lb, db)
            dq_ref[rows_of(b), :] = dq.astype(dq_ref.dtype)

        one(0, True)
        if nb > 1:
            def loop(b, c):
                one(b, False)
                return c
            lax.fori_loop(1, nb, loop, 0)
        dk_ref[...] = dk_acc[...].astype(dk_ref.dtype)
        dv_ref[...] = dv_acc[...].astype(dv_ref.dtype)

    def spec(cb):
        return pl.BlockSpec((n, HD), functools.partial(lambda r, j, cb: (r, cb * 4 + gi * 4 + j), cb=cb))

    ospec = pl.BlockSpec((n, HD), lambda r, j: (r, j))
    return pl.pallas_call(
        body, name=name, grid=(dil, 4),
        in_specs=[spec(CB_Q), spec(CB_K), spec(CB_V), ospec, ospec, ospec], out_specs=[ospec] * 3,
        out_shape=[jax.ShapeDtypeStruct((SEQ, AW), BF16)] * 3,
        scratch_shapes=[pltpu.VMEM((n, HD), F32), pltpu.VMEM((n, HD), F32)],
        compiler_params=_params("parallel", "parallel"),
    )(proj, proj, proj, dy, lse, dsum)


def _rms(x, g):
    return x * lax.rsqrt(jnp.mean(x * x, axis=-1, keepdims=True) + RMS_EPS) * g


def _sig(x):
    return 1.0 / (1.0 + jnp.exp(-x))


def _silu(x):
    return x * _sig(x)


def _gelu(x):
    return 0.5 * x * (1.0 + jnp.tanh(math.sqrt(2.0 / math.pi) * (x + 0.044715 * (x * x * x))))


def _y1_fn(y0p, u, dskip):
    return _gelu(y0p + dskip * u)


def _ys_fn(y1, t, z, bglu):
    return y1 * _sig(t + bglu) * _silu(z)


def _merge_fn(ms, ma, gs, ga):
    return _sig(gs) * ms + _sig(ga) * ma


def _colsum(v):
    return jnp.sum(v, axis=0, keepdims=True)


def _head_sums(v):
    parts = [jnp.broadcast_to(jnp.sum(v[:, j * HD:(j + 1) * HD], axis=-1, keepdims=True), (v.shape[0], HD))
             for j in range(AW // HD)]
    return jnp.concatenate(parts, axis=-1)


def _lam_fn(lre, lim, ldt):
    a = jnp.minimum(lre, -1e-4)
    dt = jnp.exp(ldt)
    mag = jnp.exp(a * dt)
    ar = mag * jnp.cos(lim * dt)
    ai = mag * jnp.sin(lim * dt)
    den = a * a + lim * lim
    cr = ((ar - 1.0) * a + ai * lim) / den
    ci = (ai * a - (ar - 1.0) * lim) / den
    return ar, ai, cr, ci


def _bbar_fn(cr, ci, bre, bim):
    return cr * bre - ci * bim, cr * bim + ci * bre


def _blockdiag(t):
    g, a, b = t.shape
    eye = jnp.eye(g, dtype=bool)[:, None, :, None]
    return jnp.where(eye, t[:, :, None, :], jnp.zeros((), t.dtype)).reshape(g * a, g * b)


def _diagblocks(m, a, b):
    m4 = m.reshape(NGRP, a, NGRP, b)
    idx = jnp.arange(NGRP)
    return m4[idx, :, idx, :]


def _layer_fwd(l, x, w, sp):
    tag = f"l{l}_"
    g1 = sp["pre_norm_g"].reshape(1, DM)
    (h,) = _ew(tag + "rms1", lambda x_, g: (_rms(x_, g),), SEQ, 256, [_ri(x)], [g1], [(DM, BF16)])
    hv = jnp.stack([h, _to_residue(h, DILATIONS[1]), _to_residue(h, DILATIONS[2]), _to_chunked(h)])
    win = w["w_in"]
    proj = _mm(tag + "proj", hv, win, "nn", SEQ, NCOL, DM, 1024, 512, 1024, F32,
               a_spec=pl.BlockSpec((None, 1024, 1024), lambda i, j, k: (_row_order(j), i, 0)),
               b_spec=pl.BlockSpec((None, 1024, 512), lambda i, j, k: (j // 2, 0, j % 2)))

    ar, ai, cr, ci = _ew(tag + "lam", _lam_fn, NGRP, NGRP,
                         [_ri(sp["lambda_re"]), _ri(sp["lambda_im"]), _ri(sp["log_dt"].reshape(NGRP, 1))], [],
                         [(NST, F32)] * 4)
    bre = sp["b_re"].reshape(NS, GCH)
    bim = sp["b_im"].reshape(NS, GCH)
    bbr, bbi = _ew(tag + "bbar", _bbar_fn, NS, NS, [_ri(cr.reshape(NS, 1)), _ri(ci.reshape(NS, 1)), _ri(bre), _ri(bim)],
                   [], [(GCH, F32)] * 2)
    wd = jnp.concatenate([_blockdiag(bbr.reshape(NGRP, NST, GCH).transpose(0, 2, 1)),
                          _blockdiag(bbi.reshape(NGRP, NST, GCH).transpose(0, 2, 1))], axis=1).astype(BF16)
    cm = jnp.concatenate([_blockdiag(sp["c_re"].transpose(0, 2, 1)),
                          -_blockdiag(sp["c_im"].transpose(0, 2, 1))], axis=0).astype(BF16)
    drive = _mm(tag + "drive", proj, wd, "nn", SEQ, 2 * NS, SW, 512, 1024, 512, F32,
                a_spec=pl.BlockSpec((512, SW), lambda i, j, k: (i, CB_U)))
    lam_r, lam_i = ar.reshape(1, NS), ai.reshape(1, NS)
    s_r, s_i = _scan(tag + "scan", drive, lam_r, lam_i, False)
    y0p = _mm(tag + "readout_r", s_r, cm, "nn", SEQ, SW, NS, 512, 512, 1024, F32)
    y0p_i = _mm(tag + "readout_i", s_i, cm, "nn", SEQ, SW, NS, 512, 512, 1024, F32,
                b_spec=pl.BlockSpec((1024, 512), lambda i, j, k: (NS // 1024 + k, j)))
    dskip = sp["d_skip"].reshape(1, SW)
    (y1,) = _ew(tag + "y1", lambda a, b_, u, d: (_y1_fn(a + b_, u, d),), SEQ, 256,
                [_ri(y0p), _ri(y0p_i), _ri(proj, SW, CB_U)], [dskip], [(SW, F32)])
    t = _mm(tag + "glu", y1, w["w_glu"], "nn", SEQ, SW, SW, 1024, 512, 512, F32)
    bglu = sp["b_glu"].reshape(1, SW)
    (ys_c,) = _ew(tag + "ys", lambda y1_, t_, z, b_: (_ys_fn(y1_, t_, z, b_),), SEQ, 256,
                  [_ri(y1), _ri(t), _ri(proj, SW, CB_ZS)], [bglu], [(SW, BF16)])
    ys = _from_chunked(ys_c)

    outs, lses = zip(*[[_from_residue(t_, DILATIONS[gi]) for t_ in _attn_fwd(tag + f"attn{gi}", proj, gi)]
                       for gi in range(3)])

    def comb(o0, o1, o2, l0, l1, l2, z):
        m = jnp.maximum(jnp.maximum(l0, l1), l2)
        e0, e1, e2 = jnp.exp(l0 - m), jnp.exp(l1 - m), jnp.exp(l2 - m)
        tot = e0 + e1 + e2
        ypre = (e0 * o0 + e1 * o1 + e2 * o2) / tot
        return ypre * _silu(z), ypre, m + jnp.log(tot)

    ya, ypre, lse = _ew(tag + "comb", comb, SEQ, 256, [_ri(o) for o in outs] + [_ri(s) for s in lses]
                        + [_ri(proj, AW, CB_ZA)], [], [(AW, BF16), (AW, F32), (AW, F32)])

    ms = _mm(tag + "branch_s", ys, w["w_branch_s"], "nn", SEQ, DM, SW, 1024, 1024, 512, F32)
    ma = _mm(tag + "branch_a", ya, w["w_branch_a"], "nn", SEQ, DM, AW, 1024, 1024, 512, F32)
    (merged,) = _ew(tag + "merge", lambda a, b_, c, d: (_merge_fn(a, b_, c, d),), SEQ, 256,
                    [_ri(ms), _ri(ma), _ri(proj, DM, CB_GS), _ri(proj, DM, CB_GA)], [], [(DM, BF16)])
    out = _mm(tag + "out", merged, w["w_out"], "nn", SEQ, DM, DM, 1024, 1024, 1024, F32)
    g2 = sp["post_norm_g"].reshape(1, DM)
    (x_new,) = _ew(tag + "post", lambda x_, o, g: (x_ + _rms(o, g),), SEQ, 256, [_ri(x), _ri(out)], [g2], [(DM, F32)])
    res = dict(x=x, hv=hv, proj=proj, ar=ar, ai=ai, cr=cr, ci=ci, wd=wd, cm=cm, s_r=s_r, s_i=s_i, y0p=y0p, y0p_i=y0p_i,
               y1=y1, t=t, ys=ys, ya=ya, ypre=ypre, lse=lse, ms=ms, ma=ma, merged=merged, out=out)
    return x_new, res


def _layer_bwd(l, dxn, r, w, sp):
    tag = f"l{l}b_"
    proj = r["proj"]
    g1 = sp["pre_norm_g"].reshape(1, DM)
    g2 = sp["post_norm_g"].reshape(1, DM)
    dskip = sp["d_skip"].reshape(1, SW)
    bglu = sp["b_glu"].reshape(1, SW)

    def post_b(d, o, g):
        _, vjp = jax.vjp(_rms, o, g)
        do, dg = vjp(d)
        return do, dg

    d_out, dg2 = _ew(tag + "post", post_b, SEQ, 256, [_ri(dxn), _ri(r["out"])], [g2], [(DM, BF16)], [DM])
    dw_out = _mm(tag + "dw_out", r["merged"], d_out, "tn", DM, DM, SEQ, 1024, 1024, SEQ, BF16)
    dmerged = _mm(tag + "dmerged", d_out, w["w_out"], "nt", SEQ, DM, DM, 1024, 1024, 1024, F32)

    def merge_b(d, ms, ma, gs, ga):
        _, vjp = jax.vjp(_merge_fn, ms, ma, gs, ga)
        return vjp(d)

    dms, dma, dgs, dga = _ew(tag + "merge", merge_b, SEQ, 256,
                             [_ri(dmerged), _ri(r["ms"]), _ri(r["ma"]), _ri(proj, DM, CB_GS), _ri(proj, DM, CB_GA)],
                             [], [(DM, BF16)] * 4)
    dw_bs = _mm(tag + "dw_bs", r["ys"], dms, "tn", SW, DM, SEQ, 512, 1024, SEQ, BF16)
    dw_ba = _mm(tag + "dw_ba", r["ya"], dma, "tn", AW, DM, SEQ, 512, 1024, SEQ, BF16)
    dys = _mm(tag + "dys", dms, w["w_branch_s"], "nt", SEQ, SW, DM, 1024, 512, 1024, F32)
    dya = _mm(tag + "dya", dma, w["w_branch_a"], "nt", SEQ, AW, DM, 1024, 512, 1024, F32)

    def comb_b(d, ypre, z):
        _, vjp = jax.vjp(lambda y, z_: y * _silu(z_), ypre, z)
        dyp, dz = vjp(d)
        return dyp, dz, _head_sums(dyp * ypre)

    dyp, dza, dsum = _ew(tag + "comb", comb_b, SEQ, 256, [_ri(dya), _ri(r["ypre"]), _ri(proj, AW, CB_ZA)], [],
                         [(AW, F32), (AW, BF16), (AW, F32)])
    dqkv = [_attn_bwd(tag + f"attn{gi}", proj, *[_to_residue(t_, DILATIONS[gi]) for t_ in (dyp, r["lse"], dsum)], gi)
            for gi in range(3)]

    def ys_b(d, y1, t, z, b_):
        _, vjp = jax.vjp(_ys_fn, y1, t, z, b_)
        dy1, dt, dz, _ = vjp(d)
        return dy1, dt, dz, _colsum(dt)

    dy1a, dt, dzs, dbglu = _ew(tag + "ys", ys_b, SEQ, 256,
                               [_ri(_to_chunked(dys)), _ri(r["y1"]), _ri(r["t"]), _ri(proj, SW, CB_ZS)],
                               [bglu], [(SW, F32), (SW, BF16), (SW, BF16)], [SW])
    dw_glu = _mm(tag + "dw_glu", r["y1"], dt, "tn", SW, SW, SEQ, 512, 512, SEQ, BF16)
    dy1b = _mm(tag + "dy1b", dt, w["w_glu"], "nt", SEQ, SW, SW, 1024, 512, 512, F32)

    def y1_b(da, db, y0p, y0p_i, u, d_):
        _, vjp = jax.vjp(_y1_fn, y0p + y0p_i, u, d_)
        dy0, du, dd = vjp(da + db)
        return dy0, du, dd

    dy0, du_skip, ddskip = _ew(tag + "y1", y1_b, SEQ, 256,
                               [_ri(dy1a), _ri(dy1b), _ri(r["y0p"]), _ri(r["y0p_i"]), _ri(proj, SW, CB_U)], [dskip],
                               [(SW, BF16), (SW, F32)], [SW])
    g_s = _mm(tag + "dstate", dy0, r["cm"], "nt", SEQ, 2 * NS, SW, 512, 1024, 512, F32)
    dcm_r = _mm(tag + "dcm_r", r["s_r"], dy0, "tn", NS, SW, SEQ, 1024, 512, SEQ, F32)
    dcm_i = _mm(tag + "dcm_i", r["s_i"], dy0, "tn", NS, SW, SEQ, 1024, 512, SEQ, F32)
    lam_r, lam_i = r["ar"].reshape(1, NS), r["ai"].reshape(1, NS)
    a_r, a_i = _scan(tag + "scan", g_s, lam_r, -lam_i, True)
    dlr, dli = _dlam(tag + "dlam", a_r, a_i, r["s_r"], r["s_i"])
    u_spec = pl.BlockSpec((SEQ, SW), lambda i, j, k: (0, CB_U))
    dwd_r = _mm(tag + "dwd_r", proj, a_r, "tn", SW, NS, SEQ, 512, 1024, SEQ, F32, a_spec=u_spec)
    dwd_i = _mm(tag + "dwd_i", proj, a_i, "tn", SW, NS, SEQ, 512, 1024, SEQ, F32, a_spec=u_spec)
    du_r = _mm(tag + "du_r", a_r, r["wd"], "nt", SEQ, SW, NS, 512, 512, 1024, F32)
    du_i = _mm(tag + "du_i", a_i, r["wd"], "nt", SEQ, SW, NS, 512, 512, 1024, F32,
               b_spec=pl.BlockSpec((512, 1024), lambda i, j, k: (j, NS // 1024 + k)))
    (du,) = _ew(tag + "du", lambda a, b_, c: (a + b_ + c,), SEQ, 256, [_ri(du_r), _ri(du_i), _ri(du_skip)], [],
                [(SW, BF16)])

    dbbr = _diagblocks(dwd_r, GCH, NST).transpose(0, 2, 1).reshape(NS, GCH)
    dbbi = _diagblocks(dwd_i, GCH, NST).transpose(0, 2, 1).reshape(NS, GCH)
    bre = sp["b_re"].reshape(NS, GCH)
    bim = sp["b_im"].reshape(NS, GCH)

    def bbar_b(cr, ci, br_, bi_, dr, di):
        _, vjp = jax.vjp(_bbar_fn, cr, ci, br_, bi_)
        return vjp((dr, di))

    dcr, dci, dbre, dbim = _ew(tag + "bbar", bbar_b, NS, NS,
                               [_ri(r["cr"].reshape(NS, 1)), _ri(r["ci"].reshape(NS, 1)), _ri(bre), _ri(bim),
                                _ri(dbbr), _ri(dbbi)], [], [(1, F32), (1, F32), (GCH, F32), (GCH, F32)])

    def lam_b(lre, lim, ldt, dar, dai, dcr_, dci_):
        _, vjp = jax.vjp(_lam_fn, lre, lim, ldt)
        return vjp((dar, dai, dcr_, dci_))

    dlre, dlim, dldt = _ew(tag + "lam", lam_b, NGRP, NGRP,
                           [_ri(sp["lambda_re"]), _ri(sp["lambda_im"]), _ri(sp["log_dt"].reshape(NGRP, 1)),
                            _ri(dlr.reshape(NGRP, NST)), _ri(dli.reshape(NGRP, NST)),
                            _ri(dcr.reshape(NGRP, NST)), _ri(dci.reshape(NGRP, NST))], [],
                           [(NST, F32), (NST, F32), (1, F32)])
    dc_re = _diagblocks(dcm_r, NST, GCH).transpose(0, 2, 1)
    dc_im = -_diagblocks(dcm_i, NST, GCH).transpose(0, 2, 1)

    dq, dk, dv = ([d[i] for d in dqkv] for i in range(3))
    dproj = jnp.concatenate([du, dzs, *dq, *dk, *dv, dza, dgs, dga], axis=1)
    dw_in = _mm(tag + "dw_in", r["hv"], dproj, "tn", DM, NCOL, SEQ, 1024, 512, SEQ, BF16,
                a_spec=pl.BlockSpec((None, SEQ, 1024), lambda i, j, k: (_row_order(j), 0, 0)),
                o_spec=pl.BlockSpec((None, 1024, 512), lambda i, j, k: (j // 2, 0, j % 2)), out_shape=(NDEV, DM, DM))

    blocks = (lambda k: jnp.where(k < 3, CB_Q + 3 * k, CB_ZA - 3 + k), lambda k: CB_Q + 1 + 3 * k,
              lambda k: CB_Q + 2 + 3 * k, lambda k: k)
    counts = (8, 3, 3, 2)
    dh = []
    for o in range(4):
        cb = blocks[o]
        dh.append(_mm(tag + f"dh{o}", dproj, w["w_in"], "nt", SEQ, DM, counts[o] * 512, 1024, 1024, 512, F32,
                      a_spec=pl.BlockSpec((1024, 512), functools.partial(lambda i, j, k, cb: (i, cb(k)), cb=cb)),
                      b_spec=pl.BlockSpec((None, 1024, 512),
                                          functools.partial(lambda i, j, k, cb: (cb(k) // 2, 0, cb(k) % 2), cb=cb))))
    dh = [dh[0], _from_residue(dh[1], DILATIONS[1]), _from_residue(dh[2], DILATIONS[2]), _from_chunked(dh[3])]

    def pre_b(d, dh0, dh1, dh2, dh3, x_, g):
        _, vjp = jax.vjp(_rms, x_, g)
        dx_, dg = vjp((dh0 + dh1) + (dh2 + dh3))
        return d + dx_, dg

    dx, dg1 = _ew(tag + "pre", pre_b, SEQ, 256, [_ri(dxn)] + [_ri(t_) for t_ in dh] + [_ri(r["x"])], [g1],
                  [(DM, F32)], [DM])

    big = dict(w_in=dw_in, w_glu=dw_glu.reshape(NDEV, SW // NDEV, SW),
               w_branch_s=dw_bs.reshape(SW, NDEV, DM // NDEV).transpose(1, 0, 2),
               w_branch_a=dw_ba.reshape(AW, NDEV, DM // NDEV).transpose(1, 0, 2),
               w_out=dw_out.reshape(NDEV, DM // NDEV, DM))
    small = dict(pre_norm_g=dg1.reshape(DM), lambda_re=dlre, lambda_im=dlim, log_dt=dldt.reshape(NGRP),
                 b_re=dbre.reshape(NGRP, NST, GCH), b_im=dbim.reshape(NGRP, NST, GCH), c_re=dc_re, c_im=dc_im,
                 d_skip=ddskip.reshape(SW), b_glu=dbglu.reshape(SW), post_norm_g=dg2.reshape(DM))
    return dx, big, small


_HBM = pl.BlockSpec(memory_space=pltpu.HBM)
_SEM = pl.BlockSpec(memory_space=pltpu.SEMAPHORE)
_EFFECT = pltpu.SideEffectType.DATAFLOW_SIDE_EFFECTING


def _remote_copies(srcs, dsts, send_sems, recv_sems, gather):
    x, y, c = lax.axis_index("x"), lax.axis_index("y"), lax.axis_index("c")
    me = 4 * x + 2 * y + c
    copies = []
    for i in range(len(srcs)):
        for k in range(1, NDEV):
            peer = (x ^ (k >> 2), y ^ ((k >> 1) & 1), c ^ (k & 1))
            src = srcs[i] if gather[i] else srcs[i].at[me ^ k]
            copies.append(pltpu.make_async_remote_copy(
                src_ref=src, dst_ref=dsts[i].at[me], send_sem=send_sems[i], recv_sem=recv_sems[i],
                device_id=peer, device_id_type=pl.DeviceIdType.MESH))
    return copies


def _all_seven(dst, send_sem, recv_sem):
    seven = dst.at[pl.ds(0, NDEV - 1)]
    me = (lax.axis_index("x"), lax.axis_index("y"), lax.axis_index("c"))
    return pltpu.make_async_remote_copy(src_ref=seven, dst_ref=seven, send_sem=send_sem, recv_sem=recv_sem,
                                        device_id=me, device_id_type=pl.DeviceIdType.MESH)


def _own_slabs(name, arrs, gather):
    n = len(arrs)

    def body(*refs):
        srcs, dsts, sems = refs[:n], refs[n:2 * n], refs[2 * n]
        me = 4 * lax.axis_index("x") + 2 * lax.axis_index("y") + lax.axis_index("c")
        copies = [pltpu.make_async_copy(srcs[i] if gather[i] else srcs[i].at[me], dsts[i].at[me], sems.at[i])
                  for i in range(n)]
        for cp in copies:
            cp.start()
        for cp in copies:
            cp.wait()

    out_shape = [jax.ShapeDtypeStruct(((NDEV,) + a.shape) if g else a.shape, a.dtype) for a, g in zip(arrs, gather)]
    any_spec = pl.BlockSpec(memory_space=pl.ANY)
    return pl.pallas_call(body, name=name, in_specs=[any_spec] * n, out_specs=[any_spec] * n, out_shape=out_shape,
                          scratch_shapes=[pltpu.SemaphoreType.DMA((n,))])(*arrs)


def _exchange_start(name, arrs, gather):
    n = len(arrs)
    lands = _own_slabs(name + "_own", arrs, gather)

    def body(*refs):
        srcs, dsts = refs[:n], refs[n:2 * n]
        send_sems, recv_sems = refs[2 * n:3 * n], refs[3 * n:4 * n]
        token = refs[-1]
        for cp in _remote_copies(srcs, dsts, send_sems, recv_sems, gather):
            cp.start()
        token[...] = jnp.zeros(token.shape, token.dtype)

    thru = [pltpu.HBM(a.shape, a.dtype) for a in list(arrs) + list(lands)]
    outs = pl.pallas_call(
        body, name=name,
        out_shape=(*[pltpu.SemaphoreType.DMA(())] * (2 * n), *thru, jax.ShapeDtypeStruct((8, 128), F32)),
        in_specs=[_HBM] * (2 * n),
        out_specs=(*[_SEM] * (2 * n), *[_HBM] * (2 * n), pl.BlockSpec(memory_space=pltpu.VMEM)),
        input_output_aliases={i: 2 * n + i for i in range(2 * n)},
        compiler_params=pltpu.CompilerParams(has_side_effects=_EFFECT),
    )(*[pltpu.with_memory_space_constraint(a, pltpu.HBM) for a in list(arrs) + list(lands)])
    return dict(send=outs[:n], recv=outs[n:2 * n], srcs=outs[2 * n:3 * n], lands=outs[3 * n:4 * n], token=outs[-1],
                gather=gather)


def _exchange_wait(name, started, after):
    n = len(started["srcs"])

    def body(*refs):
        dsts = refs[n:2 * n]
        send_sems, recv_sems = refs[2 * n:3 * n], refs[3 * n:4 * n]
        for i in range(n):
            cp = _all_seven(dsts[i], send_sems[i], recv_sems[i])
            cp.wait_send()
            cp.wait_recv()

    bufs = list(started["srcs"]) + list(started["lands"])
    outs = pl.pallas_call(
        body, name=name, out_shape=tuple(pltpu.HBM(a.shape, a.dtype) for a in bufs),
        in_specs=[_HBM] * (2 * n) + [_SEM] * (2 * n) + [pl.BlockSpec(memory_space=pl.ANY)], out_specs=(_HBM,) * (2 * n),
        input_output_aliases={i: i for i in range(2 * n)},
        compiler_params=pltpu.CompilerParams(has_side_effects=_EFFECT),
    )(*bufs, *started["send"], *started["recv"], after)
    return outs[n:]


def _adamw(name, parts, w, m, v, br):
    rows, cols = w.shape

    def fn(*a):
        g = a[0].astype(F32)
        for p in a[1:NDEV]:
            g = g + p.astype(F32)
        w_, m_, v_ = a[NDEV:]
        m2 = B1 * m_ + (1.0 - B1) * g
        v2 = B2 * v_ + (1.0 - B2) * (g * g)
        m_hat = m2 / (1.0 - B1 ** STEP)
        v_hat = v2 / (1.0 - B2 ** STEP)
        delta = -LR * (m_hat / (jnp.sqrt(v_hat) + ADAM_EPS) + WD * w_)
        return g, delta, m2, v2

    ins = [_ri(parts, cols, 0, d * (rows // br)) for d in range(NDEV)] + [_ri(w), _ri(m), _ri(v)]
    return _ew(name, fn, rows, br, ins, [], [(cols, F32)] * 4)


SMALL = ("pre_norm_g", "lambda_re", "lambda_im", "log_dt", "b_re", "b_im", "c_re", "c_im", "d_skip", "b_glu",
         "post_norm_g")
BIG = ("w_in", "w_glu", "w_branch_s", "w_branch_a", "w_out")
WEIGHTS = ("pre_norm_g", "w_in", "lambda_re", "lambda_im", "log_dt", "b_re", "b_im", "c_re", "c_im", "d_skip",
           "w_glu", "b_glu", "w_branch_s", "w_branch_a", "w_out", "post_norm_g")
PACK_COLS = 1024
PACK_BR = 136


def _pack_layer(d):
    flat = jnp.concatenate([d[k].astype(F32).reshape(-1) for k in SMALL])
    assert flat.shape[0] <= PACK_BR * PACK_COLS
    return jnp.pad(flat, (0, PACK_BR * PACK_COLS - flat.shape[0])).reshape(PACK_BR, PACK_COLS)


def _pack(d):
    return jnp.concatenate([_pack_layer({k: d[k][l] for k in SMALL}) for l in range(DEPTH)], axis=0)


def _unpack(p, like):
    flat = p.reshape(DEPTH, PACK_BR * PACK_COLS)
    out, off = {}, 0
    for k in SMALL:
        n = like[k].size // DEPTH
        out[k] = flat[:, off:off + n].reshape(like[k].shape)
        off += n
    return out


def _local_step(x, target, small, weights_of, grads_done):
    res, ws = [], []
    for l in range(DEPTH):
        w_l, tok = weights_of(l, x)
        sp = {k: small[k][l] for k in SMALL}
        if tok is not None:
            sp["pre_norm_g"] = sp["pre_norm_g"] + tok[0, 0]
        x, r = _layer_fwd(l, x, w_l, sp)
        res.append(r)
        ws.append(w_l)

    def loss_fn(y, t):
        e = y - t
        return e * (1.0 / DM), jnp.sum(_colsum(0.5 * e * e * (1.0 / DM)), axis=1, keepdims=True)

    dx, loss = _ew("loss", loss_fn, SEQ, 256, [_ri(x), _ri(target)], [], [(DM, F32)], [1])
    tok = None
    for l in reversed(range(DEPTH)):
        sp = {k: small[k][l] for k in SMALL}
        if tok is not None:
            sp["post_norm_g"] = sp["post_norm_g"] + tok[0, 0]
        dx, big, sm = _layer_bwd(l, dx, res[l], ws[l], sp)
        tok = grads_done(l, big, sm)
    return loss.reshape(()), dx


def _full_weights(gathered):
    g = gathered
    return dict(
        w_in=g["w_in"],
        w_glu=g["w_glu"].reshape(SW, SW),
        w_branch_s=g["w_branch_s"].transpose(1, 0, 2).reshape(SW, DM),
        w_branch_a=g["w_branch_a"].transpose(1, 0, 2).reshape(AW, DM),
        w_out=g["w_out"].reshape(DM, DM),
    )


def kernel(x, pre_norm_g, w_in, lambda_re, lambda_im, log_dt, b_re, b_im, c_re, c_im, d_skip, w_glu, b_glu, w_branch_s, w_branch_a, w_out, post_norm_g, loss_target, m_pre_norm_g, m_w_in, m_lambda_re, m_lambda_im, m_log_dt, m_b_re, m_b_im, m_c_re, m_c_im, m_d_skip, m_w_glu, m_b_glu, m_w_branch_s, m_w_branch_a, m_w_out, m_post_norm_g, v_pre_norm_g, v_w_in, v_lambda_re, v_lambda_im, v_log_dt, v_b_re, v_b_im, v_c_re, v_c_im, v_d_skip, v_w_glu, v_b_glu, v_w_branch_s, v_w_branch_a, v_w_out, v_post_norm_g):
    wts = dict(pre_norm_g=pre_norm_g, w_in=w_in, lambda_re=lambda_re, lambda_im=lambda_im, log_dt=log_dt, b_re=b_re,
               b_im=b_im, c_re=c_re, c_im=c_im, d_skip=d_skip, w_glu=w_glu, b_glu=b_glu, w_branch_s=w_branch_s,
               w_branch_a=w_branch_a, w_out=w_out, post_norm_g=post_norm_g)
    mom = dict(pre_norm_g=m_pre_norm_g, w_in=m_w_in, lambda_re=m_lambda_re, lambda_im=m_lambda_im, log_dt=m_log_dt,
               b_re=m_b_re, b_im=m_b_im, c_re=m_c_re, c_im=m_c_im, d_skip=m_d_skip, w_glu=m_w_glu, b_glu=m_b_glu,
               w_branch_s=m_w_branch_s, w_branch_a=m_w_branch_a, w_out=m_w_out, post_norm_g=m_post_norm_g)
    var = dict(pre_norm_g=v_pre_norm_g, w_in=v_w_in, lambda_re=v_lambda_re, lambda_im=v_lambda_im, log_dt=v_log_dt,
               b_re=v_b_re, b_im=v_b_im, c_re=v_c_re, c_im=v_c_im, d_skip=v_d_skip, w_glu=v_w_glu, b_glu=v_b_glu,
               w_branch_s=v_w_branch_s, w_branch_a=v_w_branch_a, w_out=v_w_out, post_norm_g=v_post_norm_g)

    def gather_start(l):
        return _exchange_start(f"gather_start{l}", [wts[k][l].astype(BF16) for k in BIG], [True] * len(BIG))

    gathering = {0: gather_start(0)}
    sending = {}

    def weights_of(l, x_l):
        gathered = _exchange_wait(f"gather_wait{l}", gathering[l], x_l)
        tok = None
        if l + 1 < DEPTH:
            gathering[l + 1] = gather_start(l + 1)
            tok = gathering[l + 1]["token"]
        return _full_weights(dict(zip(BIG, gathered))), tok

    def grads_done(l, big, sm):
        sending[l] = _exchange_start(f"grads_start{l}", [big[k] for k in BIG] + [_pack_layer(sm)],
                                     [False] * len(BIG) + [True])
        return sending[l]["token"]

    loss, dx = _local_step(x[0], loss_target[0], wts, weights_of, grads_done)
    loss = lax.psum(loss, ("x", "y", "c"))
    recv_l = [_exchange_wait(f"grads_wait{l}", sending[l], dx) for l in range(DEPTH)]
    recv = [jnp.stack([r[i] for r in recv_l], axis=1) for i in range(len(BIG))]
    recv.append(jnp.concatenate([r[len(BIG)] for r in recv_l], axis=1))

    grads, delta, new_m, new_v = {}, {}, {}, {}
    for k, parts in zip(BIG, recv[:len(BIG)]):
        shape = wts[k].shape
        cols = shape[-1]
        rows = wts[k].size // cols
        br = min(rows, 256)
        outs = _adamw("adamw_" + k, parts.reshape(NDEV * rows, cols), wts[k].reshape(rows, cols),
                      mom[k].reshape(rows, cols), var[k].reshape(rows, cols), br)
        grads[k], delta[k], new_m[k], new_v[k] = (o.reshape(shape) for o in outs)
    pw, pm, pv = (_pack({k: d[k] for k in SMALL}) for d in (wts, mom, var))
    rows = pw.shape[0]
    outs = _adamw("adamw_small", recv[-1].reshape(NDEV * rows, PACK_COLS), pw, pm, pv, PACK_BR)
    for d, o in zip((grads, delta, new_m, new_v), outs):
        d.update(_unpack(o, wts))

    return (loss, dx[None], *[grads[k] for k in WEIGHTS], *[delta[k] for k in WEIGHTS],
            *[new_m[k] for k in WEIGHTS], *[new_v[k] for k in WEIGHTS])
```

```python
import functools
import math

import jax
import jax.numpy as jnp
from jax import lax
from jax.experimental import pallas as pl
from jax.experimental.pallas import tpu as pltpu

F32 = jnp.float32
BF16 = jnp.bfloat16

NDEV = 8
DEPTH = 4
SEQ = 2048
DM = 1024
NCOL = 8192
SW = 512
NGRP = 32
GCH = 16
NST = 64
NS = NGRP * NST
HD = 128
AW = 512
DILATIONS = (1, 4, 16)
ABLK = 128
RMS_EPS = 1e-6
LR, B1, B2, ADAM_EPS, WD, STEP = 0.001, 0.9, 0.999, 1e-08, 0.01, 10

CB_U, CB_ZS, CB_Q, CB_K, CB_V, CB_ZA = 0, 1, 2, 5, 8, 11
CB_GS, CB_GA = 6, 7

VMEM_LIMIT = 56 * 2 ** 20


def _row_order(j):
    return jnp.where(j < CB_Q, 3, jnp.where(j < CB_ZA, (j - CB_Q) % 3, 0))


def _params(*sem):
    return pltpu.CompilerParams(dimension_semantics=sem, vmem_limit_bytes=VMEM_LIMIT)


def _ew(name, fn, rows, br, row_ins, bc_ins, row_outs, red_outs=()):
    n_in = len(row_ins) + len(bc_ins)
    n_ro = len(row_outs)
    steps = rows // br
    assert steps * br == rows

    def body(*refs):
        vals = fn(*[r[...] for r in refs[:n_in]])
        outs = refs[n_in:]
        for r, v in zip(outs[:n_ro], vals[:n_ro]):
            r[...] = v.astype(r.dtype)
        if red_outs:
            @pl.when(pl.program_id(0) == 0)
            def _():
                for r in outs[n_ro:]:
                    r[...] = jnp.zeros(r.shape, r.dtype)
            for r, v in zip(outs[n_ro:], vals[n_ro:]):
                r[...] += v

    in_specs = []
    for (_, w, cb, rb) in row_ins:
        in_specs.append(pl.BlockSpec((br, w), functools.partial(lambda i, cb, rb: (rb + i, cb), cb=cb, rb=rb)))
    for a in bc_ins:
        in_specs.append(pl.BlockSpec(a.shape, functools.partial(lambda i, nd: (0,) * nd, nd=a.ndim)))
    out_specs = [pl.BlockSpec((br, w), lambda i: (i, 0)) for (w, _) in row_outs]
    out_specs += [pl.BlockSpec((1, w), lambda i: (0, 0)) for w in red_outs]
    out_shape = [jax.ShapeDtypeStruct((rows, w), dt) for (w, dt) in row_outs]
    out_shape += [jax.ShapeDtypeStruct((1, w), F32) for w in red_outs]
    return pl.pallas_call(
        body, name=name, grid=(steps,), in_specs=in_specs, out_specs=out_specs, out_shape=out_shape,
        compiler_params=_params("arbitrary"),
    )(*[a for (a, _, _, _) in row_ins], *bc_ins)


def _ri(a, w=None, cb=0, rb=0):
    return (a, a.shape[1] if w is None else w, cb, rb)


_DIMS = {"nn": ((1,), (0,)), "nt": ((1,), (1,)), "tn": ((0,), (0,))}


def _mm(name, a, b, mode, M, N, K, bm, bn, bk, out_dtype, a_spec=None, b_spec=None, o_spec=None, out_shape=None):
    nk = K // bk
    assert M % bm == 0 and N % bn == 0 and nk * bk == K

    own_acc = nk > 1 and out_dtype != F32

    def body(a_ref, b_ref, o_ref, *scratch):
        part = lax.dot_general(a_ref[...].astype(BF16), b_ref[...].astype(BF16), (_DIMS[mode], ((), ())),
                               preferred_element_type=F32)
        if nk == 1:
            o_ref[...] = part.astype(o_ref.dtype)
            return
        k = pl.program_id(2)
        acc_ref = scratch[0] if own_acc else o_ref

        @pl.when(k == 0)
        def _():
            acc_ref[...] = part

        @pl.when(k > 0)
        def _():
            acc_ref[...] += part

        if own_acc:
            @pl.when(k == nk - 1)
            def _():
                o_ref[...] = acc_ref[...].astype(o_ref.dtype)

    if a_spec is None:
        a_spec = (pl.BlockSpec((bk, bm), lambda i, j, k: (k, i)) if mode == "tn"
                  else pl.BlockSpec((bm, bk), lambda i, j, k: (i, k)))
    if b_spec is None:
        b_spec = (pl.BlockSpec((bn, bk), lambda i, j, k: (j, k)) if mode == "nt"
                  else pl.BlockSpec((bk, bn), lambda i, j, k: (k, j)))
    if o_spec is None:
        o_spec = pl.BlockSpec((bm, bn), lambda i, j, k: (i, j))
    if out_shape is None:
        out_shape = (M, N)
    return pl.pallas_call(
        body, name=name, grid=(M // bm, N // bn, nk), in_specs=[a_spec, b_spec], out_specs=o_spec,
        out_shape=jax.ShapeDtypeStruct(out_shape, out_dtype),
        scratch_shapes=[pltpu.VMEM((bm, bn), F32)] if own_acc else [],
        compiler_params=_params("parallel", "parallel", "arbitrary"),
    )(a, b)


SCAN_LANES = 512
SCAN_CHUNKS = 8


def _to_chunked(a):
    return a.reshape(SCAN_CHUNKS, SEQ // SCAN_CHUNKS, -1).transpose(1, 0, 2).reshape(SEQ, -1)


def _from_chunked(a):
    return a.reshape(SEQ // SCAN_CHUNKS, SCAN_CHUNKS, -1).transpose(1, 0, 2).reshape(SEQ, -1)


def _scan(name, d, lam_r, lam_i, reverse):
    T = SEQ // SCAN_CHUNKS
    bl = SCAN_LANES
    nblk = NS // bl
    assert T == 2 ** 8

    def body(dr_ref, di_ref, ar_ref, ai_ref, sr_ref, si_ref):
        ar = jnp.broadcast_to(ar_ref[...], (SCAN_CHUNKS, bl))
        ai = jnp.broadcast_to(ai_ref[...], (SCAN_CHUNKS, bl))
        zero = jnp.zeros((SCAN_CHUNKS, bl), F32)

        def tile(j):
            return pl.ds(pl.multiple_of(j * SCAN_CHUNKS, SCAN_CHUNKS), SCAN_CHUNKS)

        def step(jj, carry):
            sr, si = carry
            j = T - 1 - jj if reverse else jj
            nr = ar * sr - ai * si + dr_ref[tile(j), :]
            ni = ar * si + ai * sr + di_ref[tile(j), :]
            sr_ref[tile(j), :] = nr
            si_ref[tile(j), :] = ni
            return nr, ni

        er, ei = lax.fori_loop(0, T, step, (zero, zero), unroll=4)

        pr, pi = ar[0:1], ai[0:1]
        for _ in range(8):
            pr, pi = pr * pr - pi * pi, 2.0 * pr * pi
        rows = lax.broadcasted_iota(jnp.int32, (SCAN_CHUNKS, bl), 0)
        cr, ci = zero, zero
        xr = jnp.zeros((1, bl), F32)
        xi = jnp.zeros((1, bl), F32)
        order = range(SCAN_CHUNKS - 2, -1, -1) if reverse else range(1, SCAN_CHUNKS)
        for c in order:
            src = c + 1 if reverse else c - 1
            nxr = pr * xr - pi * xi + er[src:src + 1]
            nxi = pr * xi + pi * xr + ei[src:src + 1]
            xr, xi = nxr, nxi
            cr = jnp.where(rows == c, xr, cr)
            ci = jnp.where(rows == c, xi, ci)

        def fix(jj, pw):
            pwr, pwi = pw
            j = T - 1 - jj if reverse else jj
            sr_ref[tile(j), :] = sr_ref[tile(j), :] + (pwr * cr - pwi * ci)
            si_ref[tile(j), :] = si_ref[tile(j), :] + (pwr * ci + pwi * cr)
            return pwr * ar - pwi * ai, pwr * ai + pwi * ar

        lax.fori_loop(0, T, fix, (ar, ai), unroll=4)

    return pl.pallas_call(
        body, name=name, grid=(nblk,),
        in_specs=[pl.BlockSpec((SEQ, bl), lambda i: (0, i)),
                  pl.BlockSpec((SEQ, bl), lambda i: (0, nblk + i)),
                  pl.BlockSpec((1, bl), lambda i: (0, i)),
                  pl.BlockSpec((1, bl), lambda i: (0, i))],
        out_specs=[pl.BlockSpec((SEQ, bl), lambda i: (0, i)),
                   pl.BlockSpec((SEQ, bl), lambda i: (0, i))],
        out_shape=[jax.ShapeDtypeStruct((SEQ, NS), F32), jax.ShapeDtypeStruct((SEQ, NS), F32)],
        compiler_params=_params("arbitrary"),
    )(d, d, lam_r, lam_i)


def _dlam(name, a_r, a_i, s_r, s_i):
    bl = 256

    def prev(s_ref):
        last = pltpu.roll(s_ref[SEQ - SCAN_CHUNKS:SEQ, :], 1, 0)
        first = jnp.where(lax.broadcasted_iota(jnp.int32, (SCAN_CHUNKS, bl), 0) > 0, last, 0.0)
        return jnp.concatenate([first, s_ref[0:SEQ - SCAN_CHUNKS, :]], axis=0)

    def body(ar_ref, ai_ref, sr_ref, si_ref, or_ref, oi_ref):
        spr, spi = prev(sr_ref), prev(si_ref)
        a_r_, a_i_ = ar_ref[...], ai_ref[...]
        or_ref[...] = jnp.sum(a_r_ * spr + a_i_ * spi, axis=0, keepdims=True)
        oi_ref[...] = jnp.sum(a_i_ * spr - a_r_ * spi, axis=0, keepdims=True)

    spec = pl.BlockSpec((SEQ, bl), lambda i: (0, i))
    ospec = pl.BlockSpec((1, bl), lambda i: (0, i))
    return pl.pallas_call(
        body, name=name, grid=(NS // bl,), in_specs=[spec] * 4, out_specs=[ospec, ospec],
        out_shape=[jax.ShapeDtypeStruct((1, NS), F32)] * 2,
        compiler_params=_params("arbitrary"),
    )(a_r, a_i, s_r, s_i)


def _to_residue(a, dil):
    return a.reshape(SEQ // dil, dil, -1).transpose(1, 0, 2).reshape(SEQ, -1)


def _from_residue(a, dil):
    return a.reshape(dil, SEQ // dil, -1).transpose(1, 0, 2).reshape(SEQ, -1)


def _scores(qb, kb, prev):
    s = lax.dot_general(qb, kb, (((1,), (1,)), ((), ())), preferred_element_type=F32) * (HD ** -0.5)
    row = lax.broadcasted_iota(jnp.int32, (ABLK, ABLK), 0)
    col = lax.broadcasted_iota(jnp.int32, (ABLK, ABLK), 1)
    return jnp.where((col >= row) if prev else (col <= row), s, -1e30)


def _rows_of(b):
    if isinstance(b, int):
        return pl.ds(b * ABLK, ABLK)
    return pl.ds(pl.multiple_of(b * ABLK, ABLK), ABLK)


def _attn_fwd(name, proj, gi):
    dil = DILATIONS[gi]
    n = SEQ // dil
    nb = n // ABLK

    def body(q_ref, k_ref, v_ref, o_ref, l_ref):
        def blk(ref, b):
            return ref[_rows_of(b), :]

        def one(b, first):
            qb = blk(q_ref, b).astype(BF16)
            s_c = _scores(qb, blk(k_ref, b).astype(BF16), False)
            m = jnp.max(s_c, axis=-1, keepdims=True)
            if not first:
                s_p = _scores(qb, blk(k_ref, b - 1).astype(BF16), True)
                m = jnp.maximum(m, jnp.max(s_p, axis=-1, keepdims=True))
            p_c = jnp.exp(s_c - m)
            den = jnp.sum(p_c, axis=-1, keepdims=True)
            acc = jnp.dot(p_c.astype(BF16), blk(v_ref, b).astype(BF16), preferred_element_type=F32)
            if not first:
                p_p = jnp.exp(s_p - m)
                den = den + jnp.sum(p_p, axis=-1, keepdims=True)
                acc = acc + jnp.dot(p_p.astype(BF16), blk(v_ref, b - 1).astype(BF16), preferred_element_type=F32)
            rows = _rows_of(b)
            o_ref[rows, :] = acc / den
            l_ref[rows, :] = jnp.broadcast_to(m + jnp.log(den), (ABLK, HD))

        one(0, True)
        if nb > 1:
            def loop(b, c):
                one(b, False)
                return c
            lax.fori_loop(1, nb, loop, 0)

    def spec(cb):
        return pl.BlockSpec((n, HD), functools.partial(lambda r, j, cb: (r, cb * 4 + gi * 4 + j), cb=cb))

    ospec = pl.BlockSpec((n, HD), lambda r, j: (r, j))
    return pl.pallas_call(
        body, name=name, grid=(dil, 4), in_specs=[spec(CB_Q), spec(CB_K), spec(CB_V)], out_specs=[ospec, ospec],
        out_shape=[jax.ShapeDtypeStruct((SEQ, AW), F32)] * 2,
        compiler_params=_params("parallel", "parallel"),
    )(proj, proj, proj)


def _attn_bwd(name, proj, dy, lse, dsum, gi):
    dil = DILATIONS[gi]
    n = SEQ // dil
    nb = n // ABLK

    def body(q_ref, k_ref, v_ref, dy_ref, l_ref, d_ref, dq_ref, dk_ref, dv_ref, dk_acc, dv_acc):
        dk_acc[...] = jnp.zeros(dk_acc.shape, F32)
        dv_acc[...] = jnp.zeros(dv_acc.shape, F32)

        rows_of = _rows_of

        def tn(a, b_):
            return lax.dot_general(a, b_, (((0,), (0,)), ((), ())), preferred_element_type=F32)

        def nt(a, b_):
            return lax.dot_general(a, b_, (((1,), (1,)), ((), ())), preferred_element_type=F32)

        def side(b, kb_idx, prev, qb, dyb, lb, db):
            kb = k_ref[rows_of(kb_idx), :].astype(BF16)
            vb = v_ref[rows_of(kb_idx), :].astype(BF16)
            p = jnp.exp(_scores(qb, kb, prev) - lb)
            ds = p * (nt(dyb, vb) - db) * (HD ** -0.5)
            dsb = ds.astype(BF16)
            dk_acc[rows_of(kb_idx), :] += tn(dsb, qb)
            dv_acc[rows_of(kb_idx), :] += tn(p.astype(BF16), dyb)
            return jnp.dot(dsb, kb, preferred_element_type=F32)

        def one(b, first):
            qb = q_ref[rows_of(b), :].astype(BF16)
            dyb = dy_ref[rows_of(b), :].astype(BF16)
            lb = l_ref[rows_of(b), :][:, 0:1]
            db = d_ref[rows_of(b), :][:, 0:1]
            dq = side(b, b, False, qb, dyb, lb, db)
            if not first:
                dq = dq + side(b, b - 1, True, qb, dyb, lb, db)
            dq_ref[rows_of(b), :] = dq.astype(dq_ref.dtype)

        one(0, True)
        if nb > 1:
            def loop(b, c):
                one(b, False)
                return c
            lax.fori_loop(1, nb, loop, 0)
        dk_ref[...] = dk_acc[...].astype(dk_ref.dtype)
        dv_ref[...] = dv_acc[...].astype(dv_ref.dtype)

    def spec(cb):
        return pl.BlockSpec((n, HD), functools.partial(lambda r, j, cb: (r, cb * 4 + gi * 4 + j), cb=cb))

    ospec = pl.BlockSpec((n, HD), lambda r, j: (r, j))
    return pl.pallas_call(
        body, name=name, grid=(dil, 4),
        in_specs=[spec(CB_Q), spec(CB_K), spec(CB_V), ospec, ospec, ospec], out_specs=[ospec] * 3,
        out_shape=[jax.ShapeDtypeStruct((SEQ, AW), BF16)] * 3,
        scratch_shapes=[pltpu.VMEM((n, HD), F32), pltpu.VMEM((n, HD), F32)],
        compiler_params=_params("parallel", "parallel"),
    )(proj, proj, proj, dy, lse, dsum)


def _rms(x, g):
    return x * lax.rsqrt(jnp.mean(x * x, axis=-1, keepdims=True) + RMS_EPS) * g


def _sig(x):
    return 1.0 / (1.0 + jnp.exp(-x))


def _silu(x):
    return x * _sig(x)


def _gelu(x):
    return 0.5 * x * (1.0 + jnp.tanh(math.sqrt(2.0 / math.pi) * (x + 0.044715 * (x * x * x))))


def _y1_fn(y0p, u, dskip):
    return _gelu(y0p + dskip * u)


def _ys_fn(y1, t, z, bglu):
    return y1 * _sig(t + bglu) * _silu(z)


def _merge_fn(ms, ma, gs, ga):
    return _sig(gs) * ms + _sig(ga) * ma


def _colsum(v):
    return jnp.sum(v, axis=0, keepdims=True)


def _head_sums(v):
    parts = [jnp.broadcast_to(jnp.sum(v[:, j * HD:(j + 1) * HD], axis=-1, keepdims=True), (v.shape[0], HD))
             for j in range(AW // HD)]
    return jnp.concatenate(parts, axis=-1)


def _lam_fn(lre, lim, ldt):
    a = jnp.minimum(lre, -1e-4)
    dt = jnp.exp(ldt)
    mag = jnp.exp(a * dt)
    ar = mag * jnp.cos(lim * dt)
    ai = mag * jnp.sin(lim * dt)
    den = a * a + lim * lim
    cr = ((ar - 1.0) * a + ai * lim) / den
    ci = (ai * a - (ar - 1.0) * lim) / den
    return ar, ai, cr, ci


def _bbar_fn(cr, ci, bre, bim):
    return cr * bre - ci * bim, cr * bim + ci * bre


def _blockdiag(t):
    g, a, b = t.shape
    eye = jnp.eye(g, dtype=bool)[:, None, :, None]
    return jnp.where(eye, t[:, :, None, :], jnp.zeros((), t.dtype)).reshape(g * a, g * b)


def _diagblocks(m, a, b):
    m4 = m.reshape(NGRP, a, NGRP, b)
    idx = jnp.arange(NGRP)
    return m4[idx, :, idx, :]


def _layer_fwd(l, x, w, sp):
    tag = f"l{l}_"
    g1 = sp["pre_norm_g"].reshape(1, DM)
    (h,) = _ew(tag + "rms1", lambda x_, g: (_rms(x_, g),), SEQ, 256, [_ri(x)], [g1], [(DM, BF16)])
    hv = jnp.stack([h, _to_residue(h, DILATIONS[1]), _to_residue(h, DILATIONS[2]), _to_chunked(h)])
    win = w["w_in"]
    proj = _mm(tag + "proj", hv, win, "nn", SEQ, NCOL, DM, 1024, 512, 1024, F32,
               a_spec=pl.BlockSpec((None, 1024, 1024), lambda i, j, k: (_row_order(j), i, 0)),
               b_spec=pl.BlockSpec((None, 1024, 512), lambda i, j, k: (j // 2, 0, j % 2)))

    ar, ai, cr, ci = _ew(tag + "lam", _lam_fn, NGRP, NGRP,
                         [_ri(sp["lambda_re"]), _ri(sp["lambda_im"]), _ri(sp["log_dt"].reshape(NGRP, 1))], [],
                         [(NST, F32)] * 4)
    bre = sp["b_re"].reshape(NS, GCH)
    bim = sp["b_im"].reshape(NS, GCH)
    bbr, bbi = _ew(tag + "bbar", _bbar_fn, NS, NS, [_ri(cr.reshape(NS, 1)), _ri(ci.reshape(NS, 1)), _ri(bre), _ri(bim)],
                   [], [(GCH, F32)] * 2)
    wd = jnp.concatenate([_blockdiag(bbr.reshape(NGRP, NST, GCH).transpose(0, 2, 1)),
                          _blockdiag(bbi.reshape(NGRP, NST, GCH).transpose(0, 2, 1))], axis=1).astype(BF16)
    cm = jnp.concatenate([_blockdiag(sp["c_re"].transpose(0, 2, 1)),
                          -_blockdiag(sp["c_im"].transpose(0, 2, 1))], axis=0).astype(BF16)
    drive = _mm(tag + "drive", proj, wd, "nn", SEQ, 2 * NS, SW, 512, 1024, 512, F32,
                a_spec=pl.BlockSpec((512, SW), lambda i, j, k: (i, CB_U)))
    lam_r, lam_i = ar.reshape(1, NS), ai.reshape(1, NS)
    s_r, s_i = _scan(tag + "scan", drive, lam_r, lam_i, False)
    y0p = _mm(tag + "readout_r", s_r, cm, "nn", SEQ, SW, NS, 512, 512, 1024, F32)
    y0p_i = _mm(tag + "readout_i", s_i, cm, "nn", SEQ, SW, NS, 512, 512, 1024, F32,
                b_spec=pl.BlockSpec((1024, 512), lambda i, j, k: (NS // 1024 + k, j)))
    dskip = sp["d_skip"].reshape(1, SW)
    (y1,) = _ew(tag + "y1", lambda a, b_, u, d: (_y1_fn(a + b_, u, d),), SEQ, 256,
                [_ri(y0p), _ri(y0p_i), _ri(proj, SW, CB_U)], [dskip], [(SW, F32)])
    t = _mm(tag + "glu", y1, w["w_glu"], "nn", SEQ, SW, SW, 1024, 512, 512, F32)
    bglu = sp["b_glu"].reshape(1, SW)
    (ys_c,) = _ew(tag + "ys", lambda y1_, t_, z, b_: (_ys_fn(y1_, t_, z, b_),), SEQ, 256,
                  [_ri(y1), _ri(t), _ri(proj, SW, CB_ZS)], [bglu], [(SW, BF16)])
    ys = _from_chunked(ys_c)

    outs, lses = zip(*[[_from_residue(t_, DILATIONS[gi]) for t_ in _attn_fwd(tag + f"attn{gi}", proj, gi)]
                       for gi in range(3)])

    def comb(o0, o1, o2, l0, l1, l2, z):
        m = jnp.maximum(jnp.maximum(l0, l1), l2)
        e0, e1, e2 = jnp.exp(l0 - m), jnp.exp(l1 - m), jnp.exp(l2 - m)
        tot = e0 + e1 + e2
        ypre = (e0 * o0 + e1 * o1 + e2 * o2) / tot
        return ypre * _silu(z), ypre, m + jnp.log(tot)

    ya, ypre, lse = _ew(tag + "comb", comb, SEQ, 256, [_ri(o) for o in outs] + [_ri(s) for s in lses]
                        + [_ri(proj, AW, CB_ZA)], [], [(AW, BF16), (AW, F32), (AW, F32)])

    ms = _mm(tag + "branch_s", ys, w["w_branch_s"], "nn", SEQ, DM, SW, 1024, 1024, 512, F32)
    ma = _mm(tag + "branch_a", ya, w["w_branch_a"], "nn", SEQ, DM, AW, 1024, 1024, 512, F32)
    (merged,) = _ew(tag + "merge", lambda a, b_, c, d: (_merge_fn(a, b_, c, d),), SEQ, 256,
                    [_ri(ms), _ri(ma), _ri(proj, DM, CB_GS), _ri(proj, DM, CB_GA)], [], [(DM, BF16)])
    out = _mm(tag + "out", merged, w["w_out"], "nn", SEQ, DM, DM, 1024, 1024, 1024, F32)
    g2 = sp["post_norm_g"].reshape(1, DM)
    (x_new,) = _ew(tag + "post", lambda x_, o, g: (x_ + _rms(o, g),), SEQ, 256, [_ri(x), _ri(out)], [g2], [(DM, F32)])
    res = dict(x=x, hv=hv, proj=proj, ar=ar, ai=ai, cr=cr, ci=ci, wd=wd, cm=cm, s_r=s_r, s_i=s_i, y0p=y0p, y0p_i=y0p_i,
               y1=y1, t=t, ys=ys, ya=ya, ypre=ypre, lse=lse, ms=ms, ma=ma, merged=merged, out=out)
    return x_new, res


def _layer_bwd(l, dxn, r, w, sp):
    tag = f"l{l}b_"
    proj = r["proj"]
    g1 = sp["pre_norm_g"].reshape(1, DM)
    g2 = sp["post_norm_g"].reshape(1, DM)
    dskip = sp["d_skip"].reshape(1, SW)
    bglu = sp["b_glu"].reshape(1, SW)

    def post_b(d, o, g):
        _, vjp = jax.vjp(_rms, o, g)
        do, dg = vjp(d)
        return do, dg

    d_out, dg2 = _ew(tag + "post", post_b, SEQ, 256, [_ri(dxn), _ri(r["out"])], [g2], [(DM, BF16)], [DM])
    dw_out = _mm(tag + "dw_out", r["merged"], d_out, "tn", DM, DM, SEQ, 1024, 1024, SEQ, BF16)
    dmerged = _mm(tag + "dmerged", d_out, w["w_out"], "nt", SEQ, DM, DM, 1024, 1024, 1024, F32)

    def merge_b(d, ms, ma, gs, ga):
        _, vjp = jax.vjp(_merge_fn, ms, ma, gs, ga)
        return vjp(d)

    dms, dma, dgs, dga = _ew(tag + "merge", merge_b, SEQ, 256,
                             [_ri(dmerged), _ri(r["ms"]), _ri(r["ma"]), _ri(proj, DM, CB_GS), _ri(proj, DM, CB_GA)],
                             [], [(DM, BF16)] * 4)
    dw_bs = _mm(tag + "dw_bs", r["ys"], dms, "tn", SW, DM, SEQ, 512, 1024, SEQ, BF16)
    dw_ba = _mm(tag + "dw_ba", r["ya"], dma, "tn", AW, DM, SEQ, 512, 1024, SEQ, BF16)
    dys = _mm(tag + "dys", dms, w["w_branch_s"], "nt", SEQ, SW, DM, 1024, 512, 1024, F32)
    dya = _mm(tag + "dya", dma, w["w_branch_a"], "nt", SEQ, AW, DM, 1024, 512, 1024, F32)

    def comb_b(d, ypre, z):
        _, vjp = jax.vjp(lambda y, z_: y * _silu(z_), ypre, z)
        dyp, dz = vjp(d)
        return dyp, dz, _head_sums(dyp * ypre)

    dyp, dza, dsum = _ew(tag + "comb", comb_b, SEQ, 256, [_ri(dya), _ri(r["ypre"]), _ri(proj, AW, CB_ZA)], [],
                         [(AW, F32), (AW, BF16), (AW, F32)])
    dqkv = [_attn_bwd(tag + f"attn{gi}", proj, *[_to_residue(t_, DILATIONS[gi]) for t_ in (dyp, r["lse"], dsum)], gi)
            for gi in range(3)]

    def ys_b(d, y1, t, z, b_):
        _, vjp = jax.vjp(_ys_fn, y1, t, z, b_)
        dy1, dt, dz, _ = vjp(d)
        return dy1, dt, dz, _colsum(dt)

    dy1a, dt, dzs, dbglu = _ew(tag + "ys", ys_b, SEQ, 256,
                               [_ri(_to_chunked(dys)), _ri(r["y1"]), _ri(r["t"]), _ri(proj, SW, CB_ZS)],
                               [bglu], [(SW, F32), (SW, BF16), (SW, BF16)], [SW])
    dw_glu = _mm(tag + "dw_glu", r["y1"], dt, "tn", SW, SW, SEQ, 512, 512, SEQ, BF16)
    dy1b = _mm(tag + "dy1b", dt, w["w_glu"], "nt", SEQ, SW, SW, 1024, 512, 512, F32)

    def y1_b(da, db, y0p, y0p_i, u, d_):
        _, vjp = jax.vjp(_y1_fn, y0p + y0p_i, u, d_)
        dy0, du, dd = vjp(da + db)
        return dy0, du, dd

    dy0, du_skip, ddskip = _ew(tag + "y1", y1_b, SEQ, 256,
                               [_ri(dy1a), _ri(dy1b), _ri(r["y0p"]), _ri(r["y0p_i"]), _ri(proj, SW, CB_U)], [dskip],
                               [(SW, BF16), (SW, F32)], [SW])
    g_s = _mm(tag + "dstate", dy0, r["cm"], "nt", SEQ, 2 * NS, SW, 512, 1024, 512, F32)
    dcm_r = _mm(tag + "dcm_r", r["s_r"], dy0, "tn", NS, SW, SEQ, 1024, 512, SEQ, F32)
    dcm_i = _mm(tag + "dcm_i", r["s_i"], dy0, "tn", NS, SW, SEQ, 1024, 512, SEQ, F32)
    lam_r, lam_i = r["ar"].reshape(1, NS), r["ai"].reshape(1, NS)
    a_r, a_i = _scan(tag + "scan", g_s, lam_r, -lam_i, True)
    dlr, dli = _dlam(tag + "dlam", a_r, a_i, r["s_r"], r["s_i"])
    u_spec = pl.BlockSpec((SEQ, SW), lambda i, j, k: (0, CB_U))
    dwd_r = _mm(tag + "dwd_r", proj, a_r, "tn", SW, NS, SEQ, 512, 1024, SEQ, F32, a_spec=u_spec)
    dwd_i = _mm(tag + "dwd_i", proj, a_i, "tn", SW, NS, SEQ, 512, 1024, SEQ, F32, a_spec=u_spec)
    du_r = _mm(tag + "du_r", a_r, r["wd"], "nt", SEQ, SW, NS, 512, 512, 1024, F32)
    du_i = _mm(tag + "du_i", a_i, r["wd"], "nt", SEQ, SW, NS, 512, 512, 1024, F32,
               b_spec=pl.BlockSpec((512, 1024), lambda i, j, k: (j, NS // 1024 + k)))
    (du,) = _ew(tag + "du", lambda a, b_, c: (a + b_ + c,), SEQ, 256, [_ri(du_r), _ri(du_i), _ri(du_skip)], [],
                [(SW, BF16)])

    dbbr = _diagblocks(dwd_r, GCH, NST).transpose(0, 2, 1).reshape(NS, GCH)
    dbbi = _diagblocks(dwd_i, GCH, NST).transpose(0, 2, 1).reshape(NS, GCH)
    bre = sp["b_re"].reshape(NS, GCH)
    bim = sp["b_im"].reshape(NS, GCH)

    def bbar_b(cr, ci, br_, bi_, dr, di):
        _, vjp = jax.vjp(_bbar_fn, cr, ci, br_, bi_)
        return vjp((dr, di))

    dcr, dci, dbre, dbim = _ew(tag + "bbar", bbar_b, NS, NS,
                               [_ri(r["cr"].reshape(NS, 1)), _ri(r["ci"].reshape(NS, 1)), _ri(bre), _ri(bim),
                                _ri(dbbr), _ri(dbbi)], [], [(1, F32), (1, F32), (GCH, F32), (GCH, F32)])

    def lam_b(lre, lim, ldt, dar, dai, dcr_, dci_):
        _, vjp = jax.vjp(_lam_fn, lre, lim, ldt)
        return vjp((dar, dai, dcr_, dci_))

    dlre, dlim, dldt = _ew(tag + "lam", lam_b, NGRP, NGRP,
                           [_ri(sp["lambda_re"]), _ri(sp["lambda_im"]), _ri(sp["log_dt"].reshape(NGRP, 1)),
                            _ri(dlr.reshape(NGRP, NST)), _ri(dli.reshape(NGRP, NST)),
                            _ri(dcr.reshape(NGRP, NST)), _ri(dci.reshape(NGRP, NST))], [],
                           [(NST, F32), (NST, F32), (1, F32)])
    dc_re = _diagblocks(dcm_r, NST, GCH).transpose(0, 2, 1)
    dc_im = -_diagblocks(dcm_i, NST, GCH).transpose(0, 2, 1)

    dq, dk, dv = ([d[i] for d in dqkv] for i in range(3))
    dproj = jnp.concatenate([du, dzs, *dq, *dk, *dv, dza, dgs, dga], axis=1)
    dw_in = _mm(tag + "dw_in", r["hv"], dproj, "tn", DM, NCOL, SEQ, 1024, 512, SEQ, BF16,
                a_spec=pl.BlockSpec((None, SEQ, 1024), lambda i, j, k: (_row_order(j), 0, 0)),
                o_spec=pl.BlockSpec((None, 1024, 512), lambda i, j, k: (j // 2, 0, j % 2)), out_shape=(NDEV, DM, DM))

    blocks = (lambda k: jnp.where(k < 3, CB_Q + 3 * k, CB_ZA - 3 + k), lambda k: CB_Q + 1 + 3 * k,
              lambda k: CB_Q + 2 + 3 * k, lambda k: k)
    counts = (8, 3, 3, 2)
    dh = []
    for o in range(4):
        cb = blocks[o]
        dh.append(_mm(tag + f"dh{o}", dproj, w["w_in"], "nt", SEQ, DM, counts[o] * 512, 1024, 1024, 512, F32,
                      a_spec=pl.BlockSpec((1024, 512), functools.partial(lambda i, j, k, cb: (i, cb(k)), cb=cb)),
                      b_spec=pl.BlockSpec((None, 1024, 512),
                                          functools.partial(lambda i, j, k, cb: (cb(k) // 2, 0, cb(k) % 2), cb=cb))))
    dh = [dh[0], _from_residue(dh[1], DILATIONS[1]), _from_residue(dh[2], DILATIONS[2]), _from_chunked(dh[3])]

    def pre_b(d, dh0, dh1, dh2, dh3, x_, g):
        _, vjp = jax.vjp(_rms, x_, g)
        dx_, dg = vjp((dh0 + dh1) + (dh2 + dh3))
        return d + dx_, dg

    dx, dg1 = _ew(tag + "pre", pre_b, SEQ, 256, [_ri(dxn)] + [_ri(t_) for t_ in dh] + [_ri(r["x"])], [g1],
                  [(DM, F32)], [DM])

    big = dict(w_in=dw_in, w_glu=dw_glu.reshape(NDEV, SW // NDEV, SW),
               w_branch_s=dw_bs.reshape(SW, NDEV, DM // NDEV).transpose(1, 0, 2),
               w_branch_a=dw_ba.reshape(AW, NDEV, DM // NDEV).transpose(1, 0, 2),
               w_out=dw_out.reshape(NDEV, DM // NDEV, DM))
    small = dict(pre_norm_g=dg1.reshape(DM), lambda_re=dlre, lambda_im=dlim, log_dt=dldt.reshape(NGRP),
                 b_re=dbre.reshape(NGRP, NST, GCH), b_im=dbim.reshape(NGRP, NST, GCH), c_re=dc_re, c_im=dc_im,
                 d_skip=ddskip.reshape(SW), b_glu=dbglu.reshape(SW), post_norm_g=dg2.reshape(DM))
    return dx, big, small


_HBM = pl.BlockSpec(memory_space=pltpu.HBM)
_SEM = pl.BlockSpec(memory_space=pltpu.SEMAPHORE)
_EFFECT = pltpu.SideEffectType.DATAFLOW_SIDE_EFFECTING


def _remote_copies(srcs, dsts, send_sems, recv_sems, gather):
    x, y, c = lax.axis_index("x"), lax.axis_index("y"), lax.axis_index("c")
    me = 4 * x + 2 * y + c
    copies = []
    for i in range(len(srcs)):
        for k in range(1, NDEV):
            peer = (x ^ (k >> 2), y ^ ((k >> 1) & 1), c ^ (k & 1))
            src = srcs[i] if gather[i] else srcs[i].at[me ^ k]
            copies.append(pltpu.make_async_remote_copy(
                src_ref=src, dst_ref=dsts[i].at[me], send_sem=send_sems[i], recv_sem=recv_sems[i],
                device_id=peer, device_id_type=pl.DeviceIdType.MESH))
    return copies


def _all_seven(dst, send_sem, recv_sem):
    seven = dst.at[pl.ds(0, NDEV - 1)]
    me = (lax.axis_index("x"), lax.axis_index("y"), lax.axis_index("c"))
    return pltpu.make_async_remote_copy(src_ref=seven, dst_ref=seven, send_sem=send_sem, recv_sem=recv_sem,
                                        device_id=me, device_id_type=pl.DeviceIdType.MESH)


def _own_slabs(name, arrs, gather, after):
    n = len(arrs)
    me = (4 * lax.axis_index("x") + 2 * lax.axis_index("y") + lax.axis_index("c")).astype(jnp.int32).reshape(1)

    def body(me_ref, *refs):
        for src, dst in zip(refs[:n], refs[n + 1:]):
            dst[...] = src[...]

    def zeros(k):
        return (0,) * k

    in_specs, out_specs, out_shape = [], [], []
    for a, g in zip(arrs, gather):
        slab = a.shape if g else a.shape[1:]
        nd = len(slab)
        if g:
            in_specs.append(pl.BlockSpec(slab, functools.partial(lambda i, me_ref, nd: zeros(nd), nd=nd)))
        else:
            in_specs.append(pl.BlockSpec((None,) + slab, functools.partial(lambda i, me_ref, nd: (me_ref[0],) + zeros(nd), nd=nd)))
        out_specs.append(pl.BlockSpec((None,) + slab, functools.partial(lambda i, me_ref, nd: (me_ref[0],) + zeros(nd), nd=nd)))
        out_shape.append(jax.ShapeDtypeStruct((NDEV,) + slab, a.dtype))
    in_specs.append(pl.BlockSpec(memory_space=pl.ANY))
    return pl.pallas_call(
        body, name=name, out_shape=out_shape,
        grid_spec=pltpu.PrefetchScalarGridSpec(num_scalar_prefetch=1, grid=(1,), in_specs=in_specs, out_specs=out_specs),
        compiler_params=_params("arbitrary"),
    )(me, *arrs, after)


def _exchange_start(name, arrs, gather, after):
    n = len(arrs)
    lands = _own_slabs(name + "_own", arrs, gather, after)

    def body(*refs):
        srcs, dsts = refs[:n], refs[n:2 * n]
        send_sems, recv_sems = refs[2 * n:3 * n], refs[3 * n:4 * n]
        token = refs[-1]
        for cp in _remote_copies(srcs, dsts, send_sems, recv_sems, gather):
            cp.start()
        token[...] = jnp.zeros(token.shape, token.dtype)

    thru = [pltpu.HBM(a.shape, a.dtype) for a in list(arrs) + list(lands)]
    outs = pl.pallas_call(
        body, name=name,
        out_shape=(*[pltpu.SemaphoreType.DMA(())] * (2 * n), *thru, jax.ShapeDtypeStruct((8, 128), F32)),
        in_specs=[_HBM] * (2 * n),
        out_specs=(*[_SEM] * (2 * n), *[_HBM] * (2 * n), pl.BlockSpec(memory_space=pltpu.VMEM)),
        input_output_aliases={i: 2 * n + i for i in range(2 * n)},
        compiler_params=pltpu.CompilerParams(has_side_effects=_EFFECT),
    )(*[pltpu.with_memory_space_constraint(a, pltpu.HBM) for a in list(arrs) + list(lands)])
    return dict(send=outs[:n], recv=outs[n:2 * n], srcs=outs[2 * n:3 * n], lands=outs[3 * n:4 * n], token=outs[-1],
                gather=gather)


def _exchange_wait(name, started, after):
    n = len(started["srcs"])

    def body(*refs):
        dsts = refs[n:2 * n]
        send_sems, recv_sems = refs[2 * n:3 * n], refs[3 * n:4 * n]
        for i in range(n):
            cp = _all_seven(dsts[i], send_sems[i], recv_sems[i])
            cp.wait_send()
            cp.wait_recv()

    bufs = list(started["srcs"]) + list(started["lands"])
    outs = pl.pallas_call(
        body, name=name, out_shape=tuple(pltpu.HBM(a.shape, a.dtype) for a in bufs),
        in_specs=[_HBM] * (2 * n) + [_SEM] * (2 * n) + [pl.BlockSpec(memory_space=pl.ANY)], out_specs=(_HBM,) * (2 * n),
        input_output_aliases={i: i for i in range(2 * n)},
        compiler_params=pltpu.CompilerParams(has_side_effects=_EFFECT),
    )(*bufs, *started["send"], *started["recv"], after)
    return outs[n:]


def _adamw(name, parts, w, m, v, br):
    rows, cols = w.shape

    def fn(*a):
        g = a[0].astype(F32)
        for p in a[1:NDEV]:
            g = g + p.astype(F32)
        w_, m_, v_ = a[NDEV:]
        m2 = B1 * m_ + (1.0 - B1) * g
        v2 = B2 * v_ + (1.0 - B2) * (g * g)
        m_hat = m2 / (1.0 - B1 ** STEP)
        v_hat = v2 / (1.0 - B2 ** STEP)
        delta = -LR * (m_hat / (jnp.sqrt(v_hat) + ADAM_EPS) + WD * w_)
        return g, delta, m2, v2

    ins = [_ri(parts, cols, 0, d * (rows // br)) for d in range(NDEV)] + [_ri(w), _ri(m), _ri(v)]
    return _ew(name, fn, rows, br, ins, [], [(cols, F32)] * 4)


SMALL = ("pre_norm_g", "lambda_re", "lambda_im", "log_dt", "b_re", "b_im", "c_re", "c_im", "d_skip", "b_glu",
         "post_norm_g")
BIG = ("w_in", "w_glu", "w_branch_s", "w_branch_a", "w_out")
WEIGHTS = ("pre_norm_g", "w_in", "lambda_re", "lambda_im", "log_dt", "b_re", "b_im", "c_re", "c_im", "d_skip",
           "w_glu", "b_glu", "w_branch_s", "w_branch_a", "w_out", "post_norm_g")
PACK_COLS = 1024
PACK_BR = 136


def _pack_layer(d):
    flat = jnp.concatenate([d[k].astype(F32).reshape(-1) for k in SMALL])
    assert flat.shape[0] <= PACK_BR * PACK_COLS
    return jnp.pad(flat, (0, PACK_BR * PACK_COLS - flat.shape[0])).reshape(PACK_BR, PACK_COLS)


def _pack(d):
    return jnp.concatenate([_pack_layer({k: d[k][l] for k in SMALL}) for l in range(DEPTH)], axis=0)


def _unpack(p, like):
    flat = p.reshape(DEPTH, PACK_BR * PACK_COLS)
    out, off = {}, 0
    for k in SMALL:
        n = like[k].size // DEPTH
        out[k] = flat[:, off:off + n].reshape(like[k].shape)
        off += n
    return out


def _local_step(x, target, small, weights_of, grads_done):
    res, ws = [], []
    for l in range(DEPTH):
        w_l, tok = weights_of(l, x)
        sp = {k: small[k][l] for k in SMALL}
        if tok is not None:
            sp["pre_norm_g"] = sp["pre_norm_g"] + tok[0, 0]
        x, r = _layer_fwd(l, x, w_l, sp)
        res.append(r)
        ws.append(w_l)

    def loss_fn(y, t):
        e = y - t
        return e * (1.0 / DM), jnp.sum(_colsum(0.5 * e * e * (1.0 / DM)), axis=1, keepdims=True)

    dx, loss = _ew("loss", loss_fn, SEQ, 256, [_ri(x), _ri(target)], [], [(DM, F32)], [1])
    tok = None
    for l in reversed(range(DEPTH)):
        sp = {k: small[k][l] for k in SMALL}
        if tok is not None:
            sp["post_norm_g"] = sp["post_norm_g"] + tok[0, 0]
        dx, big, sm = _layer_bwd(l, dx, res[l], ws[l], sp)
        tok = grads_done(l, big, sm)
    return loss.reshape(()), dx


def _full_weights(gathered):
    g = gathered
    return dict(
        w_in=g["w_in"],
        w_glu=g["w_glu"].reshape(SW, SW),
        w_branch_s=g["w_branch_s"].transpose(1, 0, 2).reshape(SW, DM),
        w_branch_a=g["w_branch_a"].transpose(1, 0, 2).reshape(AW, DM),
        w_out=g["w_out"].reshape(DM, DM),
    )


def kernel(x, pre_norm_g, w_in, lambda_re, lambda_im, log_dt, b_re, b_im, c_re, c_im, d_skip, w_glu, b_glu, w_branch_s, w_branch_a, w_out, post_norm_g, loss_target, m_pre_norm_g, m_w_in, m_lambda_re, m_lambda_im, m_log_dt, m_b_re, m_b_im, m_c_re, m_c_im, m_d_skip, m_w_glu, m_b_glu, m_w_branch_s, m_w_branch_a, m_w_out, m_post_norm_g, v_pre_norm_g, v_w_in, v_lambda_re, v_lambda_im, v_log_dt, v_b_re, v_b_im, v_c_re, v_c_im, v_d_skip, v_w_glu, v_b_glu, v_w_branch_s, v_w_branch_a, v_w_out, v_post_norm_g):
    wts = dict(pre_norm_g=pre_norm_g, w_in=w_in, lambda_re=lambda_re, lambda_im=lambda_im, log_dt=log_dt, b_re=b_re,
               b_im=b_im, c_re=c_re, c_im=c_im, d_skip=d_skip, w_glu=w_glu, b_glu=b_glu, w_branch_s=w_branch_s,
               w_branch_a=w_branch_a, w_out=w_out, post_norm_g=post_norm_g)
    mom = dict(pre_norm_g=m_pre_norm_g, w_in=m_w_in, lambda_re=m_lambda_re, lambda_im=m_lambda_im, log_dt=m_log_dt,
               b_re=m_b_re, b_im=m_b_im, c_re=m_c_re, c_im=m_c_im, d_skip=m_d_skip, w_glu=m_w_glu, b_glu=m_b_glu,
               w_branch_s=m_w_branch_s, w_branch_a=m_w_branch_a, w_out=m_w_out, post_norm_g=m_post_norm_g)
    var = dict(pre_norm_g=v_pre_norm_g, w_in=v_w_in, lambda_re=v_lambda_re, lambda_im=v_lambda_im, log_dt=v_log_dt,
               b_re=v_b_re, b_im=v_b_im, c_re=v_c_re, c_im=v_c_im, d_skip=v_d_skip, w_glu=v_w_glu, b_glu=v_b_glu,
               w_branch_s=v_w_branch_s, w_branch_a=v_w_branch_a, w_out=v_w_out, post_norm_g=v_post_norm_g)

    def gather_start(l, after):
        return _exchange_start(f"gather_start{l}", [wts[k][l].astype(BF16) for k in BIG], [True] * len(BIG), after)

    gathering = {0: gather_start(0, x)}
    sending = {}

    def weights_of(l, x_l):
        gathered = _exchange_wait(f"gather_wait{l}", gathering[l], x_l)
        tok = None
        if l + 1 < DEPTH:
            gathering[l + 1] = gather_start(l + 1, gathered[0])
            tok = gathering[l + 1]["token"]
        return _full_weights(dict(zip(BIG, gathered))), tok

    def grads_done(l, big, sm):
        sending[l] = _exchange_start(f"grads_start{l}", [big[k] for k in BIG] + [_pack_layer(sm)],
                                     [False] * len(BIG) + [True], big["w_in"])
        return sending[l]["token"]

    loss, dx = _local_step(x[0], loss_target[0], wts, weights_of, grads_done)
    loss = lax.psum(loss, ("x", "y", "c"))
    recv_l = [_exchange_wait(f"grads_wait{l}", sending[l], dx) for l in range(DEPTH)]
    recv = [jnp.stack([r[i] for r in recv_l], axis=1) for i in range(len(BIG))]
    recv.append(jnp.concatenate([r[len(BIG)] for r in recv_l], axis=1))

    grads, delta, new_m, new_v = {}, {}, {}, {}
    for k, parts in zip(BIG, recv[:len(BIG)]):
        shape = wts[k].shape
        cols = shape[-1]
        rows = wts[k].size // cols
        br = min(rows, 256)
        outs = _adamw("adamw_" + k, parts.reshape(NDEV * rows, cols), wts[k].reshape(rows, cols),
                      mom[k].reshape(rows, cols), var[k].reshape(rows, cols), br)
        grads[k], delta[k], new_m[k], new_v[k] = (o.reshape(shape) for o in outs)
    pw, pm, pv = (_pack({k: d[k] for k in SMALL}) for d in (wts, mom, var))
    rows = pw.shape[0]
    outs = _adamw("adamw_small", recv[-1].reshape(NDEV * rows, PACK_COLS), pw, pm, pv, PACK_BR)
    for d, o in zip((grads, delta, new_m, new_v), outs):
        d.update(_unpack(o, wts))

    return (loss, dx[None], *[grads[k] for k in WEIGHTS], *[delta[k] for k in WEIGHTS],
            *[new_m[k] for k in WEIGHTS], *[new_v[k] for k in WEIGHTS])
```

```python
import functools
import math

import jax
import jax.numpy as jnp
from jax import lax
from jax.experimental import pallas as pl
from jax.experimental.pallas import tpu as pltpu

F32 = jnp.float32
BF16 = jnp.bfloat16

NDEV = 8
DEPTH = 4
SEQ = 2048
DM = 1024
NCOL = 8192
SW = 512
NGRP = 32
GCH = 16
NST = 64
NS = NGRP * NST
HD = 128
AW = 512
DILATIONS = (1, 4, 16)
ABLK = 128
RMS_EPS = 1e-6
LR, B1, B2, ADAM_EPS, WD, STEP = 0.001, 0.9, 0.999, 1e-08, 0.01, 10

CB_U, CB_ZS, CB_Q, CB_K, CB_V, CB_ZA = 0, 1, 2, 5, 8, 11
CB_GS, CB_GA = 6, 7

VMEM_LIMIT = 56 * 2 ** 20


def _row_order(j):
    return jnp.where(j < CB_Q, 3, jnp.where(j < CB_ZA, (j - CB_Q) % 3, 0))


def _params(*sem):
    return pltpu.CompilerParams(dimension_semantics=sem, vmem_limit_bytes=VMEM_LIMIT)


def _ew(name, fn, rows, br, row_ins, bc_ins, row_outs, red_outs=()):
    n_in = len(row_ins) + len(bc_ins)
    n_ro = len(row_outs)
    steps = rows // br
    assert steps * br == rows

    def body(*refs):
        vals = fn(*[r[...] for r in refs[:n_in]])
        outs = refs[n_in:]
        for r, v in zip(outs[:n_ro], vals[:n_ro]):
            r[...] = v.astype(r.dtype)
        if red_outs:
            @pl.when(pl.program_id(0) == 0)
            def _():
                for r in outs[n_ro:]:
                    r[...] = jnp.zeros(r.shape, r.dtype)
            for r, v in zip(outs[n_ro:], vals[n_ro:]):
                r[...] += v

    in_specs = []
    for (_, w, cb, rb) in row_ins:
        in_specs.append(pl.BlockSpec((br, w), functools.partial(lambda i, cb, rb: (rb + i, cb), cb=cb, rb=rb)))
    for a in bc_ins:
        in_specs.append(pl.BlockSpec(a.shape, functools.partial(lambda i, nd: (0,) * nd, nd=a.ndim)))
    out_specs = [pl.BlockSpec((br, w), lambda i: (i, 0)) for (w, _) in row_outs]
    out_specs += [pl.BlockSpec((1, w), lambda i: (0, 0)) for w in red_outs]
    out_shape = [jax.ShapeDtypeStruct((rows, w), dt) for (w, dt) in row_outs]
    out_shape += [jax.ShapeDtypeStruct((1, w), F32) for w in red_outs]
    return pl.pallas_call(
        body, name=name, grid=(steps,), in_specs=in_specs, out_specs=out_specs, out_shape=out_shape,
        compiler_params=_params("arbitrary"),
    )(*[a for (a, _, _, _) in row_ins], *bc_ins)


def _ri(a, w=None, cb=0, rb=0):
    return (a, a.shape[1] if w is None else w, cb, rb)


_DIMS = {"nn": ((1,), (0,)), "nt": ((1,), (1,)), "tn": ((0,), (0,))}


def _mm(name, a, b, mode, M, N, K, bm, bn, bk, out_dtype, a_spec=None, b_spec=None, o_spec=None, out_shape=None):
    nk = K // bk
    assert M % bm == 0 and N % bn == 0 and nk * bk == K

    own_acc = nk > 1 and out_dtype != F32

    def body(a_ref, b_ref, o_ref, *scratch):
        part = lax.dot_general(a_ref[...].astype(BF16), b_ref[...].astype(BF16), (_DIMS[mode], ((), ())),
                               preferred_element_type=F32)
        if nk == 1:
            o_ref[...] = part.astype(o_ref.dtype)
            return
        k = pl.program_id(2)
        acc_ref = scratch[0] if own_acc else o_ref

        @pl.when(k == 0)
        def _():
            acc_ref[...] = part

        @pl.when(k > 0)
        def _():
            acc_ref[...] += part

        if own_acc:
            @pl.when(k == nk - 1)
            def _():
                o_ref[...] = acc_ref[...].astype(o_ref.dtype)

    if a_spec is None:
        a_spec = (pl.BlockSpec((bk, bm), lambda i, j, k: (k, i)) if mode == "tn"
                  else pl.BlockSpec((bm, bk), lambda i, j, k: (i, k)))
    if b_spec is None:
        b_spec = (pl.BlockSpec((bn, bk), lambda i, j, k: (j, k)) if mode == "nt"
                  else pl.BlockSpec((bk, bn), lambda i, j, k: (k, j)))
    if o_spec is None:
        o_spec = pl.BlockSpec((bm, bn), lambda i, j, k: (i, j))
    if out_shape is None:
        out_shape = (M, N)
    return pl.pallas_call(
        body, name=name, grid=(M // bm, N // bn, nk), in_specs=[a_spec, b_spec], out_specs=o_spec,
        out_shape=jax.ShapeDtypeStruct(out_shape, out_dtype),
        scratch_shapes=[pltpu.VMEM((bm, bn), F32)] if own_acc else [],
        compiler_params=_params("parallel", "parallel", "arbitrary"),
    )(a, b)


SCAN_LANES = 512
SCAN_CHUNKS = 8


def _to_chunked(a):
    return a.reshape(SCAN_CHUNKS, SEQ // SCAN_CHUNKS, -1).transpose(1, 0, 2).reshape(SEQ, -1)


def _from_chunked(a):
    return a.reshape(SEQ // SCAN_CHUNKS, SCAN_CHUNKS, -1).transpose(1, 0, 2).reshape(SEQ, -1)


def _scan(name, d, lam_r, lam_i, reverse):
    T = SEQ // SCAN_CHUNKS
    bl = SCAN_LANES
    nblk = NS // bl
    assert T == 2 ** 8

    def body(dr_ref, di_ref, ar_ref, ai_ref, sr_ref, si_ref):
        ar = jnp.broadcast_to(ar_ref[...], (SCAN_CHUNKS, bl))
        ai = jnp.broadcast_to(ai_ref[...], (SCAN_CHUNKS, bl))
        zero = jnp.zeros((SCAN_CHUNKS, bl), F32)

        def tile(j):
            return pl.ds(pl.multiple_of(j * SCAN_CHUNKS, SCAN_CHUNKS), SCAN_CHUNKS)

        def step(jj, carry):
            sr, si = carry
            j = T - 1 - jj if reverse else jj
            nr = ar * sr - ai * si + dr_ref[tile(j), :]
            ni = ar * si + ai * sr + di_ref[tile(j), :]
            sr_ref[tile(j), :] = nr
            si_ref[tile(j), :] = ni
            return nr, ni

        er, ei = lax.fori_loop(0, T, step, (zero, zero), unroll=4)

        pr, pi = ar[0:1], ai[0:1]
        for _ in range(8):
            pr, pi = pr * pr - pi * pi, 2.0 * pr * pi
        rows = lax.broadcasted_iota(jnp.int32, (SCAN_CHUNKS, bl), 0)
        cr, ci = zero, zero
        xr = jnp.zeros((1, bl), F32)
        xi = jnp.zeros((1, bl), F32)
        order = range(SCAN_CHUNKS - 2, -1, -1) if reverse else range(1, SCAN_CHUNKS)
        for c in order:
            src = c + 1 if reverse else c - 1
            nxr = pr * xr - pi * xi + er[src:src + 1]
            nxi = pr * xi + pi * xr + ei[src:src + 1]
            xr, xi = nxr, nxi
            cr = jnp.where(rows == c, xr, cr)
            ci = jnp.where(rows == c, xi, ci)

        def fix(jj, pw):
            pwr, pwi = pw
            j = T - 1 - jj if reverse else jj
            sr_ref[tile(j), :] = sr_ref[tile(j), :] + (pwr * cr - pwi * ci)
            si_ref[tile(j), :] = si_ref[tile(j), :] + (pwr * ci + pwi * cr)
            return pwr * ar - pwi * ai, pwr * ai + pwi * ar

        lax.fori_loop(0, T, fix, (ar, ai), unroll=4)

    return pl.pallas_call(
        body, name=name, grid=(nblk,),
        in_specs=[pl.BlockSpec((SEQ, bl), lambda i: (0, i)),
                  pl.BlockSpec((SEQ, bl), lambda i: (0, nblk + i)),
                  pl.BlockSpec((1, bl), lambda i: (0, i)),
                  pl.BlockSpec((1, bl), lambda i: (0, i))],
        out_specs=[pl.BlockSpec((SEQ, bl), lambda i: (0, i)),
                   pl.BlockSpec((SEQ, bl), lambda i: (0, i))],
        out_shape=[jax.ShapeDtypeStruct((SEQ, NS), F32), jax.ShapeDtypeStruct((SEQ, NS), F32)],
        compiler_params=_params("arbitrary"),
    )(d, d, lam_r, lam_i)


def _dlam(name, a_r, a_i, s_r, s_i):
    bl = 256

    def prev(s_ref):
        last = pltpu.roll(s_ref[SEQ - SCAN_CHUNKS:SEQ, :], 1, 0)
        first = jnp.where(lax.broadcasted_iota(jnp.int32, (SCAN_CHUNKS, bl), 0) > 0, last, 0.0)
        return jnp.concatenate([first, s_ref[0:SEQ - SCAN_CHUNKS, :]], axis=0)

    def body(ar_ref, ai_ref, sr_ref, si_ref, or_ref, oi_ref):
        spr, spi = prev(sr_ref), prev(si_ref)
        a_r_, a_i_ = ar_ref[...], ai_ref[...]
        or_ref[...] = jnp.sum(a_r_ * spr + a_i_ * spi, axis=0, keepdims=True)
        oi_ref[...] = jnp.sum(a_i_ * spr - a_r_ * spi, axis=0, keepdims=True)

    spec = pl.BlockSpec((SEQ, bl), lambda i: (0, i))
    ospec = pl.BlockSpec((1, bl), lambda i: (0, i))
    return pl.pallas_call(
        body, name=name, grid=(NS // bl,), in_specs=[spec] * 4, out_specs=[ospec, ospec],
        out_shape=[jax.ShapeDtypeStruct((1, NS), F32)] * 2,
        compiler_params=_params("arbitrary"),
    )(a_r, a_i, s_r, s_i)


def _to_residue(a, dil):
    return a.reshape(SEQ // dil, dil, -1).transpose(1, 0, 2).reshape(SEQ, -1)


def _from_residue(a, dil):
    return a.reshape(dil, SEQ // dil, -1).transpose(1, 0, 2).reshape(SEQ, -1)


def _scores(qb, kb, prev):
    s = lax.dot_general(qb, kb, (((1,), (1,)), ((), ())), preferred_element_type=F32) * (HD ** -0.5)
    row = lax.broadcasted_iota(jnp.int32, (ABLK, ABLK), 0)
    col = lax.broadcasted_iota(jnp.int32, (ABLK, ABLK), 1)
    return jnp.where((col >= row) if prev else (col <= row), s, -1e30)


def _rows_of(b):
    if isinstance(b, int):
        return pl.ds(b * ABLK, ABLK)
    return pl.ds(pl.multiple_of(b * ABLK, ABLK), ABLK)


def _attn_fwd(name, proj, gi):
    dil = DILATIONS[gi]
    n = SEQ // dil
    nb = n // ABLK

    def body(q_ref, k_ref, v_ref, o_ref, l_ref):
        def blk(ref, b):
            return ref[_rows_of(b), :]

        def one(b, first):
            qb = blk(q_ref, b).astype(BF16)
            s_c = _scores(qb, blk(k_ref, b).astype(BF16), False)
            m = jnp.max(s_c, axis=-1, keepdims=True)
            if not first:
                s_p = _scores(qb, blk(k_ref, b - 1).astype(BF16), True)
                m = jnp.maximum(m, jnp.max(s_p, axis=-1, keepdims=True))
            p_c = jnp.exp(s_c - m)
            den = jnp.sum(p_c, axis=-1, keepdims=True)
            acc = jnp.dot(p_c.astype(BF16), blk(v_ref, b).astype(BF16), preferred_element_type=F32)
            if not first:
                p_p = jnp.exp(s_p - m)
                den = den + jnp.sum(p_p, axis=-1, keepdims=True)
                acc = acc + jnp.dot(p_p.astype(BF16), blk(v_ref, b - 1).astype(BF16), preferred_element_type=F32)
            rows = _rows_of(b)
            o_ref[rows, :] = acc / den
            l_ref[rows, :] = jnp.broadcast_to(m + jnp.log(den), (ABLK, HD))

        one(0, True)
        if nb > 1:
            def loop(b, c):
                one(b, False)
                return c
            lax.fori_loop(1, nb, loop, 0)

    def spec(cb):
        return pl.BlockSpec((n, HD), functools.partial(lambda r, j, cb: (r, cb * 4 + gi * 4 + j), cb=cb))

    ospec = pl.BlockSpec((n, HD), lambda r, j: (r, j))
    return pl.pallas_call(
        body, name=name, grid=(dil, 4), in_specs=[spec(CB_Q), spec(CB_K), spec(CB_V)], out_specs=[ospec, ospec],
        out_shape=[jax.ShapeDtypeStruct((SEQ, AW), F32)] * 2,
        compiler_params=_params("parallel", "parallel"),
    )(proj, proj, proj)


def _attn_bwd(name, proj, dy, lse, dsum, gi):
    dil = DILATIONS[gi]
    n = SEQ // dil
    nb = n // ABLK

    def body(q_ref, k_ref, v_ref, dy_ref, l_ref, d_ref, dq_ref, dk_ref, dv_ref, dk_acc, dv_acc):
        dk_acc[...] = jnp.zeros(dk_acc.shape, F32)
        dv_acc[...] = jnp.zeros(dv_acc.shape, F32)

        rows_of = _rows_of

        def tn(a, b_):
            return lax.dot_general(a, b_, (((0,), (0,)), ((), ())), preferred_element_type=F32)

        def nt(a, b_):
            return lax.dot_general(a, b_, (((1,), (1,)), ((), ())), preferred_element_type=F32)

        def side(b, kb_idx, prev, qb, dyb, lb, db):
            kb = k_ref[rows_of(kb_idx), :].astype(BF16)
            vb = v_ref[rows_of(kb_idx), :].astype(BF16)
            p = jnp.exp(_scores(qb, kb, prev) - lb)
            ds = p * (nt(dyb, vb) - db) * (HD ** -0.5)
            dsb = ds.astype(BF16)
            dk_acc[rows_of(kb_idx), :] += tn(dsb, qb)
            dv_acc[rows_of(kb_idx), :] += tn(p.astype(BF16), dyb)
            return jnp.dot(dsb, kb, preferred_element_type=F32)

        def one(b, first):
            qb = q_ref[rows_of(b), :].astype(BF16)
            dyb = dy_ref[rows_of(b), :].astype(BF16)
            lb = l_ref[rows_of(b), :][:, 0:1]
            db = d_ref[rows_of(b), :][:, 0:1]
            dq = side(b, b, False, qb, dyb, lb, db)
            if not first:
                dq = dq + side(b, b - 1, True, qb, dyb, lb, db)
            dq_ref[rows_of(b), :] = dq.astype(dq_ref.dtype)

        one(0, True)
        if nb > 1:
            def loop(b, c):
                one(b, False)
                return c
            lax.fori_loop(1, nb, loop, 0)
        dk_ref[...] = dk_acc[...].astype(dk_ref.dtype)
        dv_ref[...] = dv_acc[...].astype(dv_ref.dtype)

    def spec(cb):
        return pl.BlockSpec((n, HD), functools.partial(lambda r, j, cb: (r, cb * 4 + gi * 4 + j), cb=cb))

    ospec = pl.BlockSpec((n, HD), lambda r, j: (r, j))
    return pl.pallas_call(
        body, name=name, grid=(dil, 4),
        in_specs=[spec(CB_Q), spec(CB_K), spec(CB_V), ospec, ospec, ospec], out_specs=[ospec] * 3,
        out_shape=[jax.ShapeDtypeStruct((SEQ, AW), BF16)] * 3,
        scratch_shapes=[pltpu.VMEM((n, HD), F32), pltpu.VMEM((n, HD), F32)],
        compiler_params=_params("parallel", "parallel"),
    )(proj, proj, proj, dy, lse, dsum)


def _rms(x, g):
    return x * lax.rsqrt(jnp.mean(x * x, axis=-1, keepdims=True) + RMS_EPS) * g


def _sig(x):
    return 1.0 / (1.0 + jnp.exp(-x))


def _silu(x):
    return x * _sig(x)


def _gelu(x):
    return 0.5 * x * (1.0 + jnp.tanh(math.sqrt(2.0 / math.pi) * (x + 0.044715 * (x * x * x))))


def _y1_fn(y0p, u, dskip):
    return _gelu(y0p + dskip * u)


def _ys_fn(y1, t, z, bglu):
    return y1 * _sig(t + bglu) * _silu(z)


def _merge_fn(ms, ma, gs, ga):
    return _sig(gs) * ms + _sig(ga) * ma


def _colsum(v):
    return jnp.sum(v, axis=0, keepdims=True)


def _head_sums(v):
    parts = [jnp.broadcast_to(jnp.sum(v[:, j * HD:(j + 1) * HD], axis=-1, keepdims=True), (v.shape[0], HD))
             for j in range(AW // HD)]
    return jnp.concatenate(parts, axis=-1)


def _lam_fn(lre, lim, ldt):
    a = jnp.minimum(lre, -1e-4)
    dt = jnp.exp(ldt)
    mag = jnp.exp(a * dt)
    ar = mag * jnp.cos(lim * dt)
    ai = mag * jnp.sin(lim * dt)
    den = a * a + lim * lim
    cr = ((ar - 1.0) * a + ai * lim) / den
    ci = (ai * a - (ar - 1.0) * lim) / den
    return ar, ai, cr, ci


def _bbar_fn(cr, ci, bre, bim):
    return cr * bre - ci * bim, cr * bim + ci * bre


def _same_group(rows, a, cols, b):
    r = lax.broadcasted_iota(jnp.int32, (rows, cols), 0) >> (a.bit_length() - 1)
    c = lax.broadcasted_iota(jnp.int32, (rows, cols), 1) >> (b.bit_length() - 1)
    return r == c


def _expand(name, blocks, signs, a, b, dtype):
    rows, cols = NGRP * a, NGRP * b
    assert a & (a - 1) == 0 and b & (b - 1) == 0

    def body(*refs):
        o_ref = refs[-1]
        tile = (lax.broadcasted_iota(jnp.int32, (b, cols), 1) & (b - 1)
                == lax.broadcasted_iota(jnp.int32, (b, cols), 0)).astype(F32)
        keep = _same_group(rows, a, cols, b)
        for i, (ref, sign) in enumerate(zip(refs[:-1], signs)):
            spread = jnp.dot(ref[...], tile, preferred_element_type=F32, precision=lax.Precision.HIGHEST)
            o_ref[i * rows:(i + 1) * rows, :] = jnp.where(keep, sign * spread, 0.0).astype(o_ref.dtype)

    return pl.pallas_call(body, name=name, out_shape=jax.ShapeDtypeStruct((len(blocks) * rows, cols), dtype),
                          compiler_params=pltpu.CompilerParams(vmem_limit_bytes=VMEM_LIMIT))(*blocks)


def _extract(name, m, a, b):
    rows, cols = NGRP * a, NGRP * b
    assert m.shape == (rows, cols) and a & (a - 1) == 0 and b & (b - 1) == 0

    def body(m_ref, o_ref):
        tile = (lax.broadcasted_iota(jnp.int32, (cols, b), 0) & (b - 1)
                == lax.broadcasted_iota(jnp.int32, (cols, b), 1)).astype(F32)
        kept = jnp.where(_same_group(rows, a, cols, b), m_ref[...], 0.0)
        o_ref[...] = jnp.dot(kept, tile, preferred_element_type=F32, precision=lax.Precision.HIGHEST)

    return pl.pallas_call(body, name=name, out_shape=jax.ShapeDtypeStruct((rows, b), F32),
                          compiler_params=pltpu.CompilerParams(vmem_limit_bytes=VMEM_LIMIT))(m)


def _layer_fwd(l, x, w, sp):
    tag = f"l{l}_"
    g1 = sp["pre_norm_g"].reshape(1, DM)
    (h,) = _ew(tag + "rms1", lambda x_, g: (_rms(x_, g),), SEQ, 256, [_ri(x)], [g1], [(DM, BF16)])
    hv = jnp.stack([h, _to_residue(h, DILATIONS[1]), _to_residue(h, DILATIONS[2]), _to_chunked(h)])
    win = w["w_in"]
    proj = _mm(tag + "proj", hv, win, "nn", SEQ, NCOL, DM, 1024, 512, 1024, F32,
               a_spec=pl.BlockSpec((None, 1024, 1024), lambda i, j, k: (_row_order(j), i, 0)),
               b_spec=pl.BlockSpec((None, 1024, 512), lambda i, j, k: (j // 2, 0, j % 2)))

    ar, ai, cr, ci = _ew(tag + "lam", _lam_fn, NGRP, NGRP,
                         [_ri(sp["lambda_re"]), _ri(sp["lambda_im"]), _ri(sp["log_dt"].reshape(NGRP, 1))], [],
                         [(NST, F32)] * 4)
    bre = sp["b_re"].reshape(NS, GCH)
    bim = sp["b_im"].reshape(NS, GCH)
    bbr, bbi = _ew(tag + "bbar", _bbar_fn, NS, NS, [_ri(cr.reshape(NS, 1)), _ri(ci.reshape(NS, 1)), _ri(bre), _ri(bim)],
                   [], [(GCH, F32)] * 2)
    wdt = _expand(tag + "wdt", [bbr, bbi], [1.0, 1.0], NST, GCH, BF16)
    cmt = _expand(tag + "cmt", [sp["c_re"].reshape(SW, NST), sp["c_im"].reshape(SW, NST)], [1.0, -1.0], GCH, NST, BF16)
    drive = _mm(tag + "drive", proj, wdt, "nt", SEQ, 2 * NS, SW, 1024, 1024, 512, F32,
                a_spec=pl.BlockSpec((1024, SW), lambda i, j, k: (i, CB_U)))
    lam_r, lam_i = ar.reshape(1, NS), ai.reshape(1, NS)
    s_r, s_i = _scan(tag + "scan", drive, lam_r, lam_i, False)
    y0p = _mm(tag + "readout_r", s_r, cmt, "nt", SEQ, SW, NS, 1024, 512, 1024, F32)
    y0p_i = _mm(tag + "readout_i", s_i, cmt, "nt", SEQ, SW, NS, 1024, 512, 1024, F32,
                b_spec=pl.BlockSpec((512, 1024), lambda i, j, k: (1, k)))
    dskip = sp["d_skip"].reshape(1, SW)
    (y1,) = _ew(tag + "y1", lambda a, b_, u, d: (_y1_fn(a + b_, u, d),), SEQ, 256,
                [_ri(y0p), _ri(y0p_i), _ri(proj, SW, CB_U)], [dskip], [(SW, F32)])
    t = _mm(tag + "glu", y1, w["w_glu"], "nn", SEQ, SW, SW, 1024, 512, 512, F32)
    bglu = sp["b_glu"].reshape(1, SW)
    (ys_c,) = _ew(tag + "ys", lambda y1_, t_, z, b_: (_ys_fn(y1_, t_, z, b_),), SEQ, 256,
                  [_ri(y1), _ri(t), _ri(proj, SW, CB_ZS)], [bglu], [(SW, BF16)])
    ys = _from_chunked(ys_c)

    outs, lses = zip(*[[_from_residue(t_, DILATIONS[gi]) for t_ in _attn_fwd(tag + f"attn{gi}", proj, gi)]
                       for gi in range(3)])

    def comb(o0, o1, o2, l0, l1, l2, z):
        m = jnp.maximum(jnp.maximum(l0, l1), l2)
        e0, e1, e2 = jnp.exp(l0 - m), jnp.exp(l1 - m), jnp.exp(l2 - m)
        tot = e0 + e1 + e2
        ypre = (e0 * o0 + e1 * o1 + e2 * o2) / tot
        return ypre * _silu(z), ypre, m + jnp.log(tot)

    ya, ypre, lse = _ew(tag + "comb", comb, SEQ, 256, [_ri(o) for o in outs] + [_ri(s) for s in lses]
                        + [_ri(proj, AW, CB_ZA)], [], [(AW, BF16), (AW, F32), (AW, F32)])

    ms = _mm(tag + "branch_s", ys, w["w_branch_s"], "nn", SEQ, DM, SW, 1024, 1024, 512, F32)
    ma = _mm(tag + "branch_a", ya, w["w_branch_a"], "nn", SEQ, DM, AW, 1024, 1024, 512, F32)
    (merged,) = _ew(tag + "merge", lambda a, b_, c, d: (_merge_fn(a, b_, c, d),), SEQ, 256,
                    [_ri(ms), _ri(ma), _ri(proj, DM, CB_GS), _ri(proj, DM, CB_GA)], [], [(DM, BF16)])
    out = _mm(tag + "out", merged, w["w_out"], "nn", SEQ, DM, DM, 1024, 1024, 1024, F32)
    g2 = sp["post_norm_g"].reshape(1, DM)
    (x_new,) = _ew(tag + "post", lambda x_, o, g: (x_ + _rms(o, g),), SEQ, 256, [_ri(x), _ri(out)], [g2], [(DM, F32)])
    res = dict(x=x, hv=hv, proj=proj, ar=ar, ai=ai, cr=cr, ci=ci, wdt=wdt, cmt=cmt, s_r=s_r, s_i=s_i, y0p=y0p, y0p_i=y0p_i,
               y1=y1, t=t, ys=ys, ya=ya, ypre=ypre, lse=lse, ms=ms, ma=ma, merged=merged, out=out)
    return x_new, res


def _layer_bwd(l, dxn, r, w, sp):
    tag = f"l{l}b_"
    proj = r["proj"]
    g1 = sp["pre_norm_g"].reshape(1, DM)
    g2 = sp["post_norm_g"].reshape(1, DM)
    dskip = sp["d_skip"].reshape(1, SW)
    bglu = sp["b_glu"].reshape(1, SW)

    def post_b(d, o, g):
        _, vjp = jax.vjp(_rms, o, g)
        do, dg = vjp(d)
        return do, dg

    d_out, dg2 = _ew(tag + "post", post_b, SEQ, 256, [_ri(dxn), _ri(r["out"])], [g2], [(DM, BF16)], [DM])
    dw_out = _mm(tag + "dw_out", r["merged"], d_out, "tn", DM, DM, SEQ, 1024, 1024, SEQ, BF16)
    dmerged = _mm(tag + "dmerged", d_out, w["w_out"], "nt", SEQ, DM, DM, 1024, 1024, 1024, F32)

    def merge_b(d, ms, ma, gs, ga):
        _, vjp = jax.vjp(_merge_fn, ms, ma, gs, ga)
        return vjp(d)

    dms, dma, dgs, dga = _ew(tag + "merge", merge_b, SEQ, 256,
                             [_ri(dmerged), _ri(r["ms"]), _ri(r["ma"]), _ri(proj, DM, CB_GS), _ri(proj, DM, CB_GA)],
                             [], [(DM, BF16)] * 4)
    dw_bs = _mm(tag + "dw_bs", r["ys"], dms, "tn", SW, DM, SEQ, 512, 1024, SEQ, BF16)
    dw_ba = _mm(tag + "dw_ba", r["ya"], dma, "tn", AW, DM, SEQ, 512, 1024, SEQ, BF16)
    dys = _mm(tag + "dys", dms, w["w_branch_s"], "nt", SEQ, SW, DM, 1024, 512, 1024, F32)
    dya = _mm(tag + "dya", dma, w["w_branch_a"], "nt", SEQ, AW, DM, 1024, 512, 1024, F32)

    def comb_b(d, ypre, z):
        _, vjp = jax.vjp(lambda y, z_: y * _silu(z_), ypre, z)
        dyp, dz = vjp(d)
        return dyp, dz, _head_sums(dyp * ypre)

    dyp, dza, dsum = _ew(tag + "comb", comb_b, SEQ, 256, [_ri(dya), _ri(r["ypre"]), _ri(proj, AW, CB_ZA)], [],
                         [(AW, F32), (AW, BF16), (AW, F32)])
    dqkv = [_attn_bwd(tag + f"attn{gi}", proj, *[_to_residue(t_, DILATIONS[gi]) for t_ in (dyp, r["lse"], dsum)], gi)
            for gi in range(3)]

    def ys_b(d, y1, t, z, b_):
        _, vjp = jax.vjp(_ys_fn, y1, t, z, b_)
        dy1, dt, dz, _ = vjp(d)
        return dy1, dt, dz, _colsum(dt)

    dy1a, dt, dzs, dbglu = _ew(tag + "ys", ys_b, SEQ, 256,
                               [_ri(_to_chunked(dys)), _ri(r["y1"]), _ri(r["t"]), _ri(proj, SW, CB_ZS)],
                               [bglu], [(SW, F32), (SW, BF16), (SW, BF16)], [SW])
    dw_glu = _mm(tag + "dw_glu", r["y1"], dt, "tn", SW, SW, SEQ, 512, 512, SEQ, BF16)
    dy1b = _mm(tag + "dy1b", dt, w["w_glu"], "nt", SEQ, SW, SW, 1024, 512, 512, F32)

    def y1_b(da, db, y0p, y0p_i, u, d_):
        _, vjp = jax.vjp(_y1_fn, y0p + y0p_i, u, d_)
        dy0, du, dd = vjp(da + db)
        return dy0, du, dd

    dy0, du_skip, ddskip = _ew(tag + "y1", y1_b, SEQ, 256,
                               [_ri(dy1a), _ri(dy1b), _ri(r["y0p"]), _ri(r["y0p_i"]), _ri(proj, SW, CB_U)], [dskip],
                               [(SW, BF16), (SW, F32)], [SW])
    g_s = _mm(tag + "dstate", dy0, r["cmt"], "nn", SEQ, 2 * NS, SW, 1024, 1024, 512, F32,
              b_spec=pl.BlockSpec((512, 1024), lambda i, j, k: (j // 2, j % 2)))
    dcmt_r = _mm(tag + "dcmt_r", dy0, r["s_r"], "tn", SW, NS, SEQ, 512, 1024, SEQ, F32)
    dcmt_i = _mm(tag + "dcmt_i", dy0, r["s_i"], "tn", SW, NS, SEQ, 512, 1024, SEQ, F32)
    lam_r, lam_i = r["ar"].reshape(1, NS), r["ai"].reshape(1, NS)
    a_r, a_i = _scan(tag + "scan", g_s, lam_r, -lam_i, True)
    dlr, dli = _dlam(tag + "dlam", a_r, a_i, r["s_r"], r["s_i"])
    u_spec = pl.BlockSpec((SEQ, SW), lambda i, j, k: (0, CB_U))
    dwdt_r = _mm(tag + "dwdt_r", a_r, proj, "tn", NS, SW, SEQ, 1024, 512, SEQ, F32, b_spec=u_spec)
    dwdt_i = _mm(tag + "dwdt_i", a_i, proj, "tn", NS, SW, SEQ, 1024, 512, SEQ, F32, b_spec=u_spec)
    du_r = _mm(tag + "du_r", a_r, r["wdt"], "nn", SEQ, SW, NS, 1024, 512, 1024, F32)
    du_i = _mm(tag + "du_i", a_i, r["wdt"], "nn", SEQ, SW, NS, 1024, 512, 1024, F32,
               b_spec=pl.BlockSpec((1024, 512), lambda i, j, k: (NS // 1024 + k, j)))
    (du,) = _ew(tag + "du", lambda a, b_, c: (a + b_ + c,), SEQ, 256, [_ri(du_r), _ri(du_i), _ri(du_skip)], [],
                [(SW, BF16)])

    dbbr = _extract(tag + "dbbr", dwdt_r, NST, GCH)
    dbbi = _extract(tag + "dbbi", dwdt_i, NST, GCH)
    bre = sp["b_re"].reshape(NS, GCH)
    bim = sp["b_im"].reshape(NS, GCH)

    def bbar_b(cr, ci, br_, bi_, dr, di):
        _, vjp = jax.vjp(_bbar_fn, cr, ci, br_, bi_)
        return vjp((dr, di))

    dcr, dci, dbre, dbim = _ew(tag + "bbar", bbar_b, NS, NS,
                               [_ri(r["cr"].reshape(NS, 1)), _ri(r["ci"].reshape(NS, 1)), _ri(bre), _ri(bim),
                                _ri(dbbr), _ri(dbbi)], [], [(1, F32), (1, F32), (GCH, F32), (GCH, F32)])

    def lam_b(lre, lim, ldt, dar, dai, dcr_, dci_):
        _, vjp = jax.vjp(_lam_fn, lre, lim, ldt)
        return vjp((dar, dai, dcr_, dci_))

    dlre, dlim, dldt = _ew(tag + "lam", lam_b, NGRP, NGRP,
                           [_ri(sp["lambda_re"]), _ri(sp["lambda_im"]), _ri(sp["log_dt"].reshape(NGRP, 1)),
                            _ri(dlr.reshape(NGRP, NST)), _ri(dli.reshape(NGRP, NST)),
                            _ri(dcr.reshape(NGRP, NST)), _ri(dci.reshape(NGRP, NST))], [],
                           [(NST, F32), (NST, F32), (1, F32)])
    dc_re = _extract(tag + "dc_re", dcmt_r, GCH, NST).reshape(NGRP, GCH, NST)
    dc_im = -_extract(tag + "dc_im", dcmt_i, GCH, NST).reshape(NGRP, GCH, NST)

    dq, dk, dv = ([d[i] for d in dqkv] for i in range(3))
    dproj = jnp.concatenate([du, dzs, *dq, *dk, *dv, dza, dgs, dga], axis=1)
    dw_in = _mm(tag + "dw_in", r["hv"], dproj, "tn", DM, NCOL, SEQ, 1024, 512, SEQ, BF16,
                a_spec=pl.BlockSpec((None, SEQ, 1024), lambda i, j, k: (_row_order(j), 0, 0)),
                o_spec=pl.BlockSpec((None, 1024, 512), lambda i, j, k: (j // 2, 0, j % 2)), out_shape=(NDEV, DM, DM))

    blocks = (lambda k: jnp.where(k < 3, CB_Q + 3 * k, CB_ZA - 3 + k), lambda k: CB_Q + 1 + 3 * k,
              lambda k: CB_Q + 2 + 3 * k, lambda k: k)
    counts = (8, 3, 3, 2)
    dh = []
    for o in range(4):
        cb = blocks[o]
        dh.append(_mm(tag + f"dh{o}", dproj, w["w_in"], "nt", SEQ, DM, counts[o] * 512, 1024, 1024, 512, F32,
                      a_spec=pl.BlockSpec((1024, 512), functools.partial(lambda i, j, k, cb: (i, cb(k)), cb=cb)),
                      b_spec=pl.BlockSpec((None, 1024, 512),
                                          functools.partial(lambda i, j, k, cb: (cb(k) // 2, 0, cb(k) % 2), cb=cb))))
    dh = [dh[0], _from_residue(dh[1], DILATIONS[1]), _from_residue(dh[2], DILATIONS[2]), _from_chunked(dh[3])]

    def pre_b(d, dh0, dh1, dh2, dh3, x_, g):
        _, vjp = jax.vjp(_rms, x_, g)
        dx_, dg = vjp((dh0 + dh1) + (dh2 + dh3))
        return d + dx_, dg

    dx, dg1 = _ew(tag + "pre", pre_b, SEQ, 256, [_ri(dxn)] + [_ri(t_) for t_ in dh] + [_ri(r["x"])], [g1],
                  [(DM, F32)], [DM])

    big = dict(w_in=dw_in, w_glu=dw_glu.reshape(NDEV, SW // NDEV, SW),
               w_branch_s=dw_bs.reshape(SW, NDEV, DM // NDEV).transpose(1, 0, 2),
               w_branch_a=dw_ba.reshape(AW, NDEV, DM // NDEV).transpose(1, 0, 2),
               w_out=dw_out.reshape(NDEV, DM // NDEV, DM))
    small = dict(pre_norm_g=dg1.reshape(DM), lambda_re=dlre, lambda_im=dlim, log_dt=dldt.reshape(NGRP),
                 b_re=dbre.reshape(NGRP, NST, GCH), b_im=dbim.reshape(NGRP, NST, GCH), c_re=dc_re, c_im=dc_im,
                 d_skip=ddskip.reshape(SW), b_glu=dbglu.reshape(SW), post_norm_g=dg2.reshape(DM))
    return dx, big, small


_HBM = pl.BlockSpec(memory_space=pltpu.HBM)
_SEM = pl.BlockSpec(memory_space=pltpu.SEMAPHORE)
_EFFECT = pltpu.SideEffectType.DATAFLOW_SIDE_EFFECTING


def _remote_copies(srcs, dsts, send_sems, recv_sems, gather):
    x, y, c = lax.axis_index("x"), lax.axis_index("y"), lax.axis_index("c")
    me = 4 * x + 2 * y + c
    copies = []
    for i in range(len(srcs)):
        for k in range(1, NDEV):
            peer = (x ^ (k >> 2), y ^ ((k >> 1) & 1), c ^ (k & 1))
            src = srcs[i] if gather[i] else srcs[i].at[me ^ k]
            copies.append(pltpu.make_async_remote_copy(
                src_ref=src, dst_ref=dsts[i].at[me], send_sem=send_sems[i], recv_sem=recv_sems[i],
                device_id=peer, device_id_type=pl.DeviceIdType.MESH))
    return copies


def _all_seven(dst, send_sem, recv_sem):
    seven = dst.at[pl.ds(0, NDEV - 1)]
    me = (lax.axis_index("x"), lax.axis_index("y"), lax.axis_index("c"))
    return pltpu.make_async_remote_copy(src_ref=seven, dst_ref=seven, send_sem=send_sem, recv_sem=recv_sem,
                                        device_id=me, device_id_type=pl.DeviceIdType.MESH)


def _own_slabs(name, arrs, gather, after):
    n = len(arrs)
    me = (4 * lax.axis_index("x") + 2 * lax.axis_index("y") + lax.axis_index("c")).astype(jnp.int32).reshape(1)

    def body(me_ref, *refs):
        for src, dst in zip(refs[:n], refs[n + 1:]):
            dst[...] = src[...]

    def zeros(k):
        return (0,) * k

    in_specs, out_specs, out_shape = [], [], []
    for a, g in zip(arrs, gather):
        slab = a.shape if g else a.shape[1:]
        nd = len(slab)
        if g:
            in_specs.append(pl.BlockSpec(slab, functools.partial(lambda i, me_ref, nd: zeros(nd), nd=nd)))
        else:
            in_specs.append(pl.BlockSpec((None,) + slab, functools.partial(lambda i, me_ref, nd: (me_ref[0],) + zeros(nd), nd=nd)))
        out_specs.append(pl.BlockSpec((None,) + slab, functools.partial(lambda i, me_ref, nd: (me_ref[0],) + zeros(nd), nd=nd)))
        out_shape.append(jax.ShapeDtypeStruct((NDEV,) + slab, a.dtype))
    in_specs.append(pl.BlockSpec(memory_space=pl.ANY))
    return pl.pallas_call(
        body, name=name, out_shape=out_shape,
        grid_spec=pltpu.PrefetchScalarGridSpec(num_scalar_prefetch=1, grid=(1,), in_specs=in_specs, out_specs=out_specs),
        compiler_params=_params("arbitrary"),
    )(me, *arrs, after)


def _exchange_start(name, arrs, gather, after):
    n = len(arrs)
    lands = _own_slabs(name + "_own", arrs, gather, after)

    def body(*refs):
        srcs, dsts = refs[:n], refs[n:2 * n]
        send_sems, recv_sems = refs[2 * n:3 * n], refs[3 * n:4 * n]
        token = refs[-1]
        for cp in _remote_copies(srcs, dsts, send_sems, recv_sems, gather):
            cp.start()
        token[...] = jnp.zeros(token.shape, token.dtype)

    thru = [pltpu.HBM(a.shape, a.dtype) for a in list(arrs) + list(lands)]
    outs = pl.pallas_call(
        body, name=name,
        out_shape=(*[pltpu.SemaphoreType.DMA(())] * (2 * n), *thru, jax.ShapeDtypeStruct((8, 128), F32)),
        in_specs=[_HBM] * (2 * n),
        out_specs=(*[_SEM] * (2 * n), *[_HBM] * (2 * n), pl.BlockSpec(memory_space=pltpu.VMEM)),
        input_output_aliases={i: 2 * n + i for i in range(2 * n)},
        compiler_params=pltpu.CompilerParams(has_side_effects=_EFFECT),
    )(*[pltpu.with_memory_space_constraint(a, pltpu.HBM) for a in list(arrs) + list(lands)])
    return dict(send=outs[:n], recv=outs[n:2 * n], srcs=outs[2 * n:3 * n], lands=outs[3 * n:4 * n], token=outs[-1],
                gather=gather)


def _exchange_wait(name, started, after):
    n = len(started["srcs"])

    def body(*refs):
        dsts = refs[n:2 * n]
        send_sems, recv_sems = refs[2 * n:3 * n], refs[3 * n:4 * n]
        for i in range(n):
            cp = _all_seven(dsts[i], send_sems[i], recv_sems[i])
            cp.wait_send()
            cp.wait_recv()

    bufs = list(started["srcs"]) + list(started["lands"])
    outs = pl.pallas_call(
        body, name=name, out_shape=tuple(pltpu.HBM(a.shape, a.dtype) for a in bufs),
        in_specs=[_HBM] * (2 * n) + [_SEM] * (2 * n) + [pl.BlockSpec(memory_space=pl.ANY)], out_specs=(_HBM,) * (2 * n),
        input_output_aliases={i: i for i in range(2 * n)},
        compiler_params=pltpu.CompilerParams(has_side_effects=_EFFECT),
    )(*bufs, *started["send"], *started["recv"], after)
    return outs[n:]


def _adamw(name, parts, w, m, v, br):
    rows, cols = w.shape

    def fn(*a):
        g = a[0].astype(F32)
        for p in a[1:NDEV]:
            g = g + p.astype(F32)
        w_, m_, v_ = a[NDEV:]
        m2 = B1 * m_ + (1.0 - B1) * g
        v2 = B2 * v_ + (1.0 - B2) * (g * g)
        m_hat = m2 / (1.0 - B1 ** STEP)
        v_hat = v2 / (1.0 - B2 ** STEP)
        delta = -LR * (m_hat / (jnp.sqrt(v_hat) + ADAM_EPS) + WD * w_)
        return g, delta, m2, v2

    ins = [_ri(parts, cols, 0, d * (rows // br)) for d in range(NDEV)] + [_ri(w), _ri(m), _ri(v)]
    return _ew(name, fn, rows, br, ins, [], [(cols, F32)] * 4)


SMALL = ("pre_norm_g", "lambda_re", "lambda_im", "log_dt", "b_re", "b_im", "c_re", "c_im", "d_skip", "b_glu",
         "post_norm_g")
BIG = ("w_in", "w_glu", "w_branch_s", "w_branch_a", "w_out")
WEIGHTS = ("pre_norm_g", "w_in", "lambda_re", "lambda_im", "log_dt", "b_re", "b_im", "c_re", "c_im", "d_skip",
           "w_glu", "b_glu", "w_branch_s", "w_branch_a", "w_out", "post_norm_g")
PACK_COLS = 1024
PACK_BR = 136


def _pack_pieces(d):
    pieces = [d[k].astype(F32).reshape(-1) for k in SMALL]
    used = sum(p.shape[0] for p in pieces)
    assert used <= PACK_BR * PACK_COLS
    return pieces + [jnp.zeros((PACK_BR * PACK_COLS - used,), F32)]


def _pack_layer(d):
    return jnp.concatenate(_pack_pieces(d)).reshape(PACK_BR, PACK_COLS)


def _pack(d):
    pieces = [p for l in range(DEPTH) for p in _pack_pieces({k: d[k][l] for k in SMALL})]
    return jnp.concatenate(pieces).reshape(DEPTH * PACK_BR, PACK_COLS)


def _unpack(p, like):
    flat = p.reshape(DEPTH, PACK_BR * PACK_COLS)
    out, off = {}, 0
    for k in SMALL:
        n = like[k].size // DEPTH
        out[k] = flat[:, off:off + n].reshape(like[k].shape)
        off += n
    return out


def _local_step(x, target, small, weights_of, grads_done):
    res, ws = [], []
    for l in range(DEPTH):
        w_l, tok = weights_of(l, x)
        sp = {k: small[k][l] for k in SMALL}
        if tok is not None:
            sp["pre_norm_g"] = sp["pre_norm_g"] + tok[0, 0]
        x, r = _layer_fwd(l, x, w_l, sp)
        res.append(r)
        ws.append(w_l)

    def loss_fn(y, t):
        e = y - t
        return e * (1.0 / DM), jnp.sum(_colsum(0.5 * e * e * (1.0 / DM)), axis=1, keepdims=True)

    dx, loss = _ew("loss", loss_fn, SEQ, 256, [_ri(x), _ri(target)], [], [(DM, F32)], [1])
    tok = None
    for l in reversed(range(DEPTH)):
        sp = {k: small[k][l] for k in SMALL}
        if tok is not None:
            sp["post_norm_g"] = sp["post_norm_g"] + tok[0, 0]
        dx, big, sm = _layer_bwd(l, dx, res[l], ws[l], sp)
        tok = grads_done(l, big, sm)
    return loss.reshape(()), dx


def _full_weights(gathered):
    g = gathered
    return dict(
        w_in=g["w_in"],
        w_glu=g["w_glu"].reshape(SW, SW),
        w_branch_s=g["w_branch_s"].transpose(1, 0, 2).reshape(SW, DM),
        w_branch_a=g["w_branch_a"].transpose(1, 0, 2).reshape(AW, DM),
        w_out=g["w_out"].reshape(DM, DM),
    )


def kernel(x, pre_norm_g, w_in, lambda_re, lambda_im, log_dt, b_re, b_im, c_re, c_im, d_skip, w_glu, b_glu, w_branch_s, w_branch_a, w_out, post_norm_g, loss_target, m_pre_norm_g, m_w_in, m_lambda_re, m_lambda_im, m_log_dt, m_b_re, m_b_im, m_c_re, m_c_im, m_d_skip, m_w_glu, m_b_glu, m_w_branch_s, m_w_branch_a, m_w_out, m_post_norm_g, v_pre_norm_g, v_w_in, v_lambda_re, v_lambda_im, v_log_dt, v_b_re, v_b_im, v_c_re, v_c_im, v_d_skip, v_w_glu, v_b_glu, v_w_branch_s, v_w_branch_a, v_w_out, v_post_norm_g):
    wts = dict(pre_norm_g=pre_norm_g, w_in=w_in, lambda_re=lambda_re, lambda_im=lambda_im, log_dt=log_dt, b_re=b_re,
               b_im=b_im, c_re=c_re, c_im=c_im, d_skip=d_skip, w_glu=w_glu, b_glu=b_glu, w_branch_s=w_branch_s,
               w_branch_a=w_branch_a, w_out=w_out, post_norm_g=post_norm_g)
    mom = dict(pre_norm_g=m_pre_norm_g, w_in=m_w_in, lambda_re=m_lambda_re, lambda_im=m_lambda_im, log_dt=m_log_dt,
               b_re=m_b_re, b_im=m_b_im, c_re=m_c_re, c_im=m_c_im, d_skip=m_d_skip, w_glu=m_w_glu, b_glu=m_b_glu,
               w_branch_s=m_w_branch_s, w_branch_a=m_w_branch_a, w_out=m_w_out, post_norm_g=m_post_norm_g)
    var = dict(pre_norm_g=v_pre_norm_g, w_in=v_w_in, lambda_re=v_lambda_re, lambda_im=v_lambda_im, log_dt=v_log_dt,
               b_re=v_b_re, b_im=v_b_im, c_re=v_c_re, c_im=v_c_im, d_skip=v_d_skip, w_glu=v_w_glu, b_glu=v_b_glu,
               w_branch_s=v_w_branch_s, w_branch_a=v_w_branch_a, w_out=v_w_out, post_norm_g=v_post_norm_g)

    def gather_start(l, after):
        return _exchange_start(f"gather_start{l}", [wts[k][l].astype(BF16) for k in BIG], [True] * len(BIG), after)

    gathering = {0: gather_start(0, x)}
    sending = {}

    def weights_of(l, x_l):
        gathered = _exchange_wait(f"gather_wait{l}", gathering[l], x_l)
        tok = None
        if l + 1 < DEPTH:
            gathering[l + 1] = gather_start(l + 1, gathered[0])
            tok = gathering[l + 1]["token"]
        return _full_weights(dict(zip(BIG, gathered))), tok

    def grads_done(l, big, sm):
        sending[l] = _exchange_start(f"grads_start{l}", [big[k] for k in BIG] + [_pack_layer(sm)],
                                     [False] * len(BIG) + [True], big["w_in"])
        return sending[l]["token"]

    loss, dx = _local_step(x[0], loss_target[0], wts, weights_of, grads_done)
    loss = lax.psum(loss, ("x", "y", "c"))
    recv_l = [_exchange_wait(f"grads_wait{l}", sending[l], dx) for l in range(DEPTH)]
    recv = [jnp.stack([r[i] for r in recv_l], axis=1) for i in range(len(BIG))]
    recv.append(jnp.concatenate([r[len(BIG)] for r in recv_l], axis=1))

    grads, delta, new_m, new_v = {}, {}, {}, {}
    for k, parts in zip(BIG, recv[:len(BIG)]):
        shape = wts[k].shape
        cols = shape[-1]
        rows = wts[k].size // cols
        br = min(rows, 256)
        outs = _adamw("adamw_" + k, parts.reshape(NDEV * rows, cols), wts[k].reshape(rows, cols),
                      mom[k].reshape(rows, cols), var[k].reshape(rows, cols), br)
        grads[k], delta[k], new_m[k], new_v[k] = (o.reshape(shape) for o in outs)
    pw, pm, pv = (_pack({k: d[k] for k in SMALL}) for d in (wts, mom, var))
    rows = pw.shape[0]
    outs = _adamw("adamw_small", recv[-1].reshape(NDEV * rows, PACK_COLS), pw, pm, pv, PACK_BR)
    for d, o in zip((grads, delta, new_m, new_v), outs):
        d.update(_unpack(o, wts))

    return (loss, dx[None], *[grads[k] for k in WEIGHTS], *[delta[k] for k in WEIGHTS],
            *[new_m[k] for k in WEIGHTS], *[new_v[k] for k in WEIGHTS])
```

```python
import functools
import math

import jax
import jax.numpy as jnp
from jax import lax
from jax.experimental import pallas as pl
from jax.experimental.pallas import tpu as pltpu

F32 = jnp.float32
BF16 = jnp.bfloat16

NDEV = 8
DEPTH = 4
SEQ = 2048
DM = 1024
NCOL = 8192
SW = 512
NGRP = 32
GCH = 16
NST = 64
NS = NGRP * NST
HD = 128
AW = 512
DILATIONS = (1, 4, 16)
ABLK = 128
RMS_EPS = 1e-6
LR, B1, B2, ADAM_EPS, WD, STEP = 0.001, 0.9, 0.999, 1e-08, 0.01, 10

CB_U, CB_ZS, CB_Q, CB_K, CB_V, CB_ZA = 0, 1, 2, 5, 8, 11
CB_GS, CB_GA = 6, 7

VMEM_LIMIT = 56 * 2 ** 20


def _row_order(j):
    return jnp.where(j < CB_Q, 1, 0)


def _params(*sem):
    return pltpu.CompilerParams(dimension_semantics=sem, vmem_limit_bytes=VMEM_LIMIT)


def _ew(name, fn, rows, br, row_ins, bc_ins, row_outs, red_outs=()):
    n_in = len(row_ins) + len(bc_ins)
    n_ro = len(row_outs)
    steps = rows // br
    assert steps * br == rows

    def body(*refs):
        vals = fn(*[r[...] for r in refs[:n_in]])
        outs = refs[n_in:]
        for r, v in zip(outs[:n_ro], vals[:n_ro]):
            r[...] = v.astype(r.dtype)
        if red_outs:
            @pl.when(pl.program_id(0) == 0)
            def _():
                for r in outs[n_ro:]:
                    r[...] = jnp.zeros(r.shape, r.dtype)
            for r, v in zip(outs[n_ro:], vals[n_ro:]):
                r[...] += v

    in_specs = []
    for (_, w, cb, rb) in row_ins:
        in_specs.append(pl.BlockSpec((br, w), functools.partial(lambda i, cb, rb: (rb + i, cb), cb=cb, rb=rb)))
    for a in bc_ins:
        in_specs.append(pl.BlockSpec(a.shape, functools.partial(lambda i, nd: (0,) * nd, nd=a.ndim)))
    out_specs = [pl.BlockSpec((br, w), lambda i: (i, 0)) for (w, _) in row_outs]
    out_specs += [pl.BlockSpec((1, w), lambda i: (0, 0)) for w in red_outs]
    out_shape = [jax.ShapeDtypeStruct((rows, w), dt) for (w, dt) in row_outs]
    out_shape += [jax.ShapeDtypeStruct((1, w), F32) for w in red_outs]
    return pl.pallas_call(
        body, name=name, grid=(steps,), in_specs=in_specs, out_specs=out_specs, out_shape=out_shape,
        compiler_params=_params("arbitrary"),
    )(*[a for (a, _, _, _) in row_ins], *bc_ins)


def _ri(a, w=None, cb=0, rb=0):
    return (a, a.shape[1] if w is None else w, cb, rb)


_DIMS = {"nn": ((1,), (0,)), "nt": ((1,), (1,)), "tn": ((0,), (0,))}


def _mm(name, a, b, mode, M, N, K, bm, bn, bk, out_dtype, a_spec=None, b_spec=None, o_spec=None, out_shape=None):
    nk = K // bk
    assert M % bm == 0 and N % bn == 0 and nk * bk == K

    own_acc = nk > 1 and out_dtype != F32

    def body(a_ref, b_ref, o_ref, *scratch):
        part = lax.dot_general(a_ref[...].astype(BF16), b_ref[...].astype(BF16), (_DIMS[mode], ((), ())),
                               preferred_element_type=F32)
        if nk == 1:
            o_ref[...] = part.astype(o_ref.dtype)
            return
        k = pl.program_id(2)
        acc_ref = scratch[0] if own_acc else o_ref

        @pl.when(k == 0)
        def _():
            acc_ref[...] = part

        @pl.when(k > 0)
        def _():
            acc_ref[...] += part

        if own_acc:
            @pl.when(k == nk - 1)
            def _():
                o_ref[...] = acc_ref[...].astype(o_ref.dtype)

    if a_spec is None:
        a_spec = (pl.BlockSpec((bk, bm), lambda i, j, k: (k, i)) if mode == "tn"
                  else pl.BlockSpec((bm, bk), lambda i, j, k: (i, k)))
    if b_spec is None:
        b_spec = (pl.BlockSpec((bn, bk), lambda i, j, k: (j, k)) if mode == "nt"
                  else pl.BlockSpec((bk, bn), lambda i, j, k: (k, j)))
    if o_spec is None:
        o_spec = pl.BlockSpec((bm, bn), lambda i, j, k: (i, j))
    if out_shape is None:
        out_shape = (M, N)
    return pl.pallas_call(
        body, name=name, grid=(M // bm, N // bn, nk), in_specs=[a_spec, b_spec], out_specs=o_spec,
        out_shape=jax.ShapeDtypeStruct(out_shape, out_dtype),
        scratch_shapes=[pltpu.VMEM((bm, bn), F32)] if own_acc else [],
        compiler_params=_params("parallel", "parallel", "arbitrary"),
    )(a, b)


SCAN_LANES = 512
SCAN_CHUNKS = 8


def _to_chunked(a):
    return a.reshape(SCAN_CHUNKS, SEQ // SCAN_CHUNKS, -1).transpose(1, 0, 2).reshape(SEQ, -1)


def _from_chunked(a):
    return a.reshape(SEQ // SCAN_CHUNKS, SCAN_CHUNKS, -1).transpose(1, 0, 2).reshape(SEQ, -1)


def _scan(name, d, lam_r, lam_i, reverse):
    T = SEQ // SCAN_CHUNKS
    bl = SCAN_LANES
    nblk = NS // bl
    assert T == 2 ** 8

    def body(dr_ref, di_ref, ar_ref, ai_ref, sr_ref, si_ref):
        ar = jnp.broadcast_to(ar_ref[...], (SCAN_CHUNKS, bl))
        ai = jnp.broadcast_to(ai_ref[...], (SCAN_CHUNKS, bl))
        zero = jnp.zeros((SCAN_CHUNKS, bl), F32)

        def tile(j):
            return pl.ds(pl.multiple_of(j * SCAN_CHUNKS, SCAN_CHUNKS), SCAN_CHUNKS)

        def step(jj, carry):
            sr, si = carry
            j = T - 1 - jj if reverse else jj
            nr = ar * sr - ai * si + dr_ref[tile(j), :]
            ni = ar * si + ai * sr + di_ref[tile(j), :]
            sr_ref[tile(j), :] = nr
            si_ref[tile(j), :] = ni
            return nr, ni

        er, ei = lax.fori_loop(0, T, step, (zero, zero), unroll=4)

        pr, pi = ar[0:1], ai[0:1]
        for _ in range(8):
            pr, pi = pr * pr - pi * pi, 2.0 * pr * pi
        rows = lax.broadcasted_iota(jnp.int32, (SCAN_CHUNKS, bl), 0)
        cr, ci = zero, zero
        xr = jnp.zeros((1, bl), F32)
        xi = jnp.zeros((1, bl), F32)
        order = range(SCAN_CHUNKS - 2, -1, -1) if reverse else range(1, SCAN_CHUNKS)
        for c in order:
            src = c + 1 if reverse else c - 1
            nxr = pr * xr - pi * xi + er[src:src + 1]
            nxi = pr * xi + pi * xr + ei[src:src + 1]
            xr, xi = nxr, nxi
            cr = jnp.where(rows == c, xr, cr)
            ci = jnp.where(rows == c, xi, ci)

        def fix(jj, pw):
            pwr, pwi = pw
            j = T - 1 - jj if reverse else jj
            sr_ref[tile(j), :] = sr_ref[tile(j), :] + (pwr * cr - pwi * ci)
            si_ref[tile(j), :] = si_ref[tile(j), :] + (pwr * ci + pwi * cr)
            return pwr * ar - pwi * ai, pwr * ai + pwi * ar

        lax.fori_loop(0, T, fix, (ar, ai), unroll=4)

    return pl.pallas_call(
        body, name=name, grid=(nblk,),
        in_specs=[pl.BlockSpec((SEQ, bl), lambda i: (0, i)),
                  pl.BlockSpec((SEQ, bl), lambda i: (0, nblk + i)),
                  pl.BlockSpec((1, bl), lambda i: (0, i)),
                  pl.BlockSpec((1, bl), lambda i: (0, i))],
        out_specs=[pl.BlockSpec((SEQ, bl), lambda i: (0, i)),
                   pl.BlockSpec((SEQ, bl), lambda i: (0, i))],
        out_shape=[jax.ShapeDtypeStruct((SEQ, NS), F32), jax.ShapeDtypeStruct((SEQ, NS), F32)],
        compiler_params=_params("arbitrary"),
    )(d, d, lam_r, lam_i)


def _dlam(name, a_r, a_i, s_r, s_i):
    bl = 256

    def prev(s_ref):
        last = pltpu.roll(s_ref[SEQ - SCAN_CHUNKS:SEQ, :], 1, 0)
        first = jnp.where(lax.broadcasted_iota(jnp.int32, (SCAN_CHUNKS, bl), 0) > 0, last, 0.0)
        return jnp.concatenate([first, s_ref[0:SEQ - SCAN_CHUNKS, :]], axis=0)

    def body(ar_ref, ai_ref, sr_ref, si_ref, or_ref, oi_ref):
        spr, spi = prev(sr_ref), prev(si_ref)
        a_r_, a_i_ = ar_ref[...], ai_ref[...]
        or_ref[...] = jnp.sum(a_r_ * spr + a_i_ * spi, axis=0, keepdims=True)
        oi_ref[...] = jnp.sum(a_i_ * spr - a_r_ * spi, axis=0, keepdims=True)

    spec = pl.BlockSpec((SEQ, bl), lambda i: (0, i))
    ospec = pl.BlockSpec((1, bl), lambda i: (0, i))
    return pl.pallas_call(
        body, name=name, grid=(NS // bl,), in_specs=[spec] * 4, out_specs=[ospec, ospec],
        out_shape=[jax.ShapeDtypeStruct((1, NS), F32)] * 2,
        compiler_params=_params("arbitrary"),
    )(a_r, a_i, s_r, s_i)


def _scores(qb, kb, prev):
    s = lax.dot_general(qb, kb, (((1,), (1,)), ((), ())), preferred_element_type=F32) * (HD ** -0.5)
    row = lax.broadcasted_iota(jnp.int32, (ABLK, ABLK), 0)
    col = lax.broadcasted_iota(jnp.int32, (ABLK, ABLK), 1)
    return jnp.where((col >= row) if prev else (col <= row), s, -1e30)


def _block_rows(dil, r, b):
    if dil == 1:
        return pl.ds(pl.multiple_of(b * ABLK, ABLK), ABLK)
    return pl.ds(r + dil * ABLK * b, ABLK, stride=dil)


def _group_blocks(dil):
    nb = SEQ // dil // ABLK
    shift = nb.bit_length() - 1
    return nb, (lambda idx: (idx >> shift, idx & (nb - 1)))


def _qkv_specs(j_of):
    return [pl.BlockSpec((SEQ, HD), functools.partial(lambda j, c: (0, c + j_of(j)), c=(cb + g) * 4))
            for g in range(3) for cb in (CB_Q, CB_K, CB_V)]


def _attention_fwd(name, proj):
    def body(*refs):
        qkv, z_ref = refs[:9], refs[9]
        y_ref, ya_ref, l_ref, den_ref = refs[10:]
        for g, dil in enumerate(DILATIONS):
            q_ref, k_ref, v_ref = qkv[3 * g:3 * g + 3]
            nb, where = _group_blocks(dil)

            def step(idx, c, g=g, dil=dil, nb=nb, where=where, q_ref=q_ref, k_ref=k_ref, v_ref=v_ref):
                r, b = where(idx)
                rows = _block_rows(dil, r, b)
                qb = q_ref[rows, :].astype(BF16)
                s_c = _scores(qb, k_ref[rows, :].astype(BF16), False)
                m = jnp.max(s_c, axis=-1, keepdims=True)
                if nb > 1:
                    prev = _block_rows(dil, r, jnp.maximum(b - 1, 0))
                    s_p = jnp.where(b > 0, _scores(qb, k_ref[prev, :].astype(BF16), True), -1e30)
                    m = jnp.maximum(m, jnp.max(s_p, axis=-1, keepdims=True))
                if g > 0:
                    m_old = l_ref[rows, :][:, 0:1]
                    m = jnp.maximum(m, m_old)
                p_c = jnp.exp(s_c - m)
                den = jnp.sum(p_c, axis=-1, keepdims=True)
                acc = jnp.dot(p_c.astype(BF16), v_ref[rows, :].astype(BF16), preferred_element_type=F32)
                if nb > 1:
                    p_p = jnp.exp(s_p - m)
                    den = den + jnp.sum(p_p, axis=-1, keepdims=True)
                    acc = acc + jnp.dot(p_p.astype(BF16), v_ref[prev, :].astype(BF16), preferred_element_type=F32)
                if g > 0:
                    alpha = jnp.exp(m_old - m)
                    den = den + alpha * den_ref[rows, :][:, 0:1]
                    acc = acc + alpha * y_ref[rows, :]
                y_ref[rows, :] = acc
                l_ref[rows, :] = jnp.broadcast_to(m, (ABLK, HD))
                den_ref[rows, :] = jnp.broadcast_to(den, (ABLK, HD))
                return c

            lax.fori_loop(0, SEQ // ABLK, step, 0)
        den = den_ref[...]
        y = y_ref[...] / den
        y_ref[...] = y
        ya_ref[...] = (y * _silu(z_ref[...])).astype(ya_ref.dtype)
        l_ref[...] = l_ref[...] + jnp.log(den)

    ospec = pl.BlockSpec((SEQ, HD), lambda j: (0, j))
    return pl.pallas_call(
        body, name=name, grid=(AW // HD,),
        in_specs=_qkv_specs(lambda j: j) + [pl.BlockSpec((SEQ, HD), lambda j: (0, CB_ZA * 4 + j))],
        out_specs=[ospec, ospec, ospec],
        out_shape=[jax.ShapeDtypeStruct((SEQ, AW), F32), jax.ShapeDtypeStruct((SEQ, AW), BF16),
                   jax.ShapeDtypeStruct((SEQ, AW), F32)],
        scratch_shapes=[pltpu.VMEM((SEQ, HD), F32)],
        compiler_params=_params("parallel"),
    )(*([proj] * 10))


def _attention_bwd(name, proj, dya, y, lse):
    def tn(a, b_):
        return lax.dot_general(a, b_, (((0,), (0,)), ((), ())), preferred_element_type=F32)

    def nt(a, b_):
        return lax.dot_general(a, b_, (((1,), (1,)), ((), ())), preferred_element_type=F32)

    def body(*refs):
        qkv, z_ref, dya_ref, y_ref, l_ref = refs[:9], refs[9], refs[10], refs[11], refs[12]
        outs, dza_ref = refs[13:22], refs[22]
        dy_s, dsum_s, dq_s, dk_s, dv_s = refs[23:]
        _, vjp = jax.vjp(lambda y_, z_: y_ * _silu(z_), y_ref[...], z_ref[...])
        dy, dz = vjp(dya_ref[...])
        dza_ref[...] = dz.astype(dza_ref.dtype)
        dy_s[...] = dy
        dsum_s[...] = jnp.broadcast_to(jnp.sum(dy * y_ref[...], axis=-1, keepdims=True), (SEQ, HD))
        for g, dil in enumerate(DILATIONS):
            q_ref, k_ref, v_ref = qkv[3 * g:3 * g + 3]
            nb, where = _group_blocks(dil)
            dk_s[...] = jnp.zeros(dk_s.shape, F32)
            dv_s[...] = jnp.zeros(dv_s.shape, F32)

            def step(idx, c, dil=dil, nb=nb, where=where, q_ref=q_ref, k_ref=k_ref, v_ref=v_ref):
                r, b = where(idx)
                rows = _block_rows(dil, r, b)
                qb = q_ref[rows, :].astype(BF16)
                dyb = dy_s[rows, :].astype(BF16)
                lb = l_ref[rows, :][:, 0:1]
                db = dsum_s[rows, :][:, 0:1]

                def side(krows, prev):
                    kb = k_ref[krows, :].astype(BF16)
                    vb = v_ref[krows, :].astype(BF16)
                    s = _scores(qb, kb, prev)
                    if prev:
                        s = jnp.where(b > 0, s, -1e30)
                    p = jnp.exp(s - lb)
                    ds = (p * (nt(dyb, vb) - db) * (HD ** -0.5)).astype(BF16)
                    dk_s[krows, :] = dk_s[krows, :] + tn(ds, qb)
                    dv_s[krows, :] = dv_s[krows, :] + tn(p.astype(BF16), dyb)
                    return jnp.dot(ds, kb, preferred_element_type=F32)

                dq = side(rows, False)
                if nb > 1:
                    dq = dq + side(_block_rows(dil, r, jnp.maximum(b - 1, 0)), True)
                dq_s[rows, :] = dq
                return c

            lax.fori_loop(0, SEQ // ABLK, step, 0)
            for o_ref, s_ref in zip(outs[3 * g:3 * g + 3], (dq_s, dk_s, dv_s)):
                o_ref[...] = s_ref[...].astype(o_ref.dtype)

    ospec = pl.BlockSpec((SEQ, HD), lambda j: (0, j))
    outs = pl.pallas_call(
        body, name=name, grid=(AW // HD,),
        in_specs=_qkv_specs(lambda j: j) + [pl.BlockSpec((SEQ, HD), lambda j: (0, CB_ZA * 4 + j))] + [ospec] * 3,
        out_specs=[ospec] * 10, out_shape=[jax.ShapeDtypeStruct((SEQ, AW), BF16)] * 10,
        scratch_shapes=[pltpu.VMEM((SEQ, HD), F32)] * 5,
        compiler_params=_params("parallel"),
    )(*([proj] * 10), dya, y, lse)
    return outs[:9], outs[9]


def _rms(x, g):
    return x * lax.rsqrt(jnp.mean(x * x, axis=-1, keepdims=True) + RMS_EPS) * g


def _sig(x):
    return 1.0 / (1.0 + jnp.exp(-x))


def _silu(x):
    return x * _sig(x)


def _gelu(x):
    return 0.5 * x * (1.0 + jnp.tanh(math.sqrt(2.0 / math.pi) * (x + 0.044715 * (x * x * x))))


def _y1_fn(y0p, u, dskip):
    return _gelu(y0p + dskip * u)


def _ys_fn(y1, t, z, bglu):
    return y1 * _sig(t + bglu) * _silu(z)


def _merge_fn(ms, ma, gs, ga):
    return _sig(gs) * ms + _sig(ga) * ma


def _colsum(v):
    return jnp.sum(v, axis=0, keepdims=True)


def _lam_fn(lre, lim, ldt):
    a = jnp.minimum(lre, -1e-4)
    dt = jnp.exp(ldt)
    mag = jnp.exp(a * dt)
    ar = mag * jnp.cos(lim * dt)
    ai = mag * jnp.sin(lim * dt)
    den = a * a + lim * lim
    cr = ((ar - 1.0) * a + ai * lim) / den
    ci = (ai * a - (ar - 1.0) * lim) / den
    return ar, ai, cr, ci


def _bbar_fn(cr, ci, bre, bim):
    return cr * bre - ci * bim, cr * bim + ci * bre


def _same_group(rows, a, cols, b):
    r = lax.broadcasted_iota(jnp.int32, (rows, cols), 0) >> (a.bit_length() - 1)
    c = lax.broadcasted_iota(jnp.int32, (rows, cols), 1) >> (b.bit_length() - 1)
    return r == c


def _expand(name, blocks, signs, a, b, dtype):
    rows, cols = NGRP * a, NGRP * b
    assert a & (a - 1) == 0 and b & (b - 1) == 0

    def body(*refs):
        o_ref = refs[-1]
        tile = (lax.broadcasted_iota(jnp.int32, (b, cols), 1) & (b - 1)
                == lax.broadcasted_iota(jnp.int32, (b, cols), 0)).astype(F32)
        keep = _same_group(rows, a, cols, b)
        for i, (ref, sign) in enumerate(zip(refs[:-1], signs)):
            spread = jnp.dot(ref[...], tile, preferred_element_type=F32, precision=lax.Precision.HIGHEST)
            o_ref[i * rows:(i + 1) * rows, :] = jnp.where(keep, sign * spread, 0.0).astype(o_ref.dtype)

    return pl.pallas_call(body, name=name, out_shape=jax.ShapeDtypeStruct((len(blocks) * rows, cols), dtype),
                          compiler_params=pltpu.CompilerParams(vmem_limit_bytes=VMEM_LIMIT))(*blocks)


def _extract(name, m, a, b):
    rows, cols = NGRP * a, NGRP * b
    assert m.shape == (rows, cols) and a & (a - 1) == 0 and b & (b - 1) == 0

    def body(m_ref, o_ref):
        tile = (lax.broadcasted_iota(jnp.int32, (cols, b), 0) & (b - 1)
                == lax.broadcasted_iota(jnp.int32, (cols, b), 1)).astype(F32)
        kept = jnp.where(_same_group(rows, a, cols, b), m_ref[...], 0.0)
        o_ref[...] = jnp.dot(kept, tile, preferred_element_type=F32, precision=lax.Precision.HIGHEST)

    return pl.pallas_call(body, name=name, out_shape=jax.ShapeDtypeStruct((rows, b), F32),
                          compiler_params=pltpu.CompilerParams(vmem_limit_bytes=VMEM_LIMIT))(m)


def _layer_fwd(l, x, w, sp):
    tag = f"l{l}_"
    g1 = sp["pre_norm_g"].reshape(1, DM)
    (h,) = _ew(tag + "rms1", lambda x_, g: (_rms(x_, g),), SEQ, 256, [_ri(x)], [g1], [(DM, BF16)])
    hv = jnp.stack([h, _to_chunked(h)])
    win = w["w_in"]
    proj = _mm(tag + "proj", hv, win, "nn", SEQ, NCOL, DM, 1024, 512, 1024, F32,
               a_spec=pl.BlockSpec((None, 1024, 1024), lambda i, j, k: (_row_order(j), i, 0)),
               b_spec=pl.BlockSpec((None, 1024, 512), lambda i, j, k: (j // 2, 0, j % 2)))

    ar, ai, cr, ci = _ew(tag + "lam", _lam_fn, NGRP, NGRP,
                         [_ri(sp["lambda_re"]), _ri(sp["lambda_im"]), _ri(sp["log_dt"].reshape(NGRP, 1))], [],
                         [(NST, F32)] * 4)
    bre = sp["b_re"].reshape(NS, GCH)
    bim = sp["b_im"].reshape(NS, GCH)
    bbr, bbi = _ew(tag + "bbar", _bbar_fn, NS, NS, [_ri(cr.reshape(NS, 1)), _ri(ci.reshape(NS, 1)), _ri(bre), _ri(bim)],
                   [], [(GCH, F32)] * 2)
    wdt = _expand(tag + "wdt", [bbr, bbi], [1.0, 1.0], NST, GCH, BF16)
    cmt = _expand(tag + "cmt", [sp["c_re"].reshape(SW, NST), sp["c_im"].reshape(SW, NST)], [1.0, -1.0], GCH, NST, BF16)
    drive = _mm(tag + "drive", proj, wdt, "nt", SEQ, 2 * NS, SW, 1024, 1024, 512, F32,
                a_spec=pl.BlockSpec((1024, SW), lambda i, j, k: (i, CB_U)))
    lam_r, lam_i = ar.reshape(1, NS), ai.reshape(1, NS)
    s_r, s_i = _scan(tag + "scan", drive, lam_r, lam_i, False)
    y0p = _mm(tag + "readout_r", s_r, cmt, "nt", SEQ, SW, NS, 1024, 512, 1024, F32)
    y0p_i = _mm(tag + "readout_i", s_i, cmt, "nt", SEQ, SW, NS, 1024, 512, 1024, F32,
                b_spec=pl.BlockSpec((512, 1024), lambda i, j, k: (1, k)))
    dskip = sp["d_skip"].reshape(1, SW)
    (y1,) = _ew(tag + "y1", lambda a, b_, u, d: (_y1_fn(a + b_, u, d),), SEQ, 256,
                [_ri(y0p), _ri(y0p_i), _ri(proj, SW, CB_U)], [dskip], [(SW, F32)])
    t = _mm(tag + "glu", y1, w["w_glu"], "nn", SEQ, SW, SW, 1024, 512, 512, F32)
    bglu = sp["b_glu"].reshape(1, SW)
    (ys_c,) = _ew(tag + "ys", lambda y1_, t_, z, b_: (_ys_fn(y1_, t_, z, b_),), SEQ, 256,
                  [_ri(y1), _ri(t), _ri(proj, SW, CB_ZS)], [bglu], [(SW, BF16)])
    ys = _from_chunked(ys_c)

    ypre, ya, lse = _attention_fwd(tag + "attn", proj)

    ms = _mm(tag + "branch_s", ys, w["w_branch_s"], "nn", SEQ, DM, SW, 1024, 1024, 512, F32)
    ma = _mm(tag + "branch_a", ya, w["w_branch_a"], "nn", SEQ, DM, AW, 1024, 1024, 512, F32)
    (merged,) = _ew(tag + "merge", lambda a, b_, c, d: (_merge_fn(a, b_, c, d),), SEQ, 256,
                    [_ri(ms), _ri(ma), _ri(proj, DM, CB_GS), _ri(proj, DM, CB_GA)], [], [(DM, BF16)])
    out = _mm(tag + "out", merged, w["w_out"], "nn", SEQ, DM, DM, 1024, 1024, 1024, F32)
    g2 = sp["post_norm_g"].reshape(1, DM)
    (x_new,) = _ew(tag + "post", lambda x_, o, g: (x_ + _rms(o, g),), SEQ, 256, [_ri(x), _ri(out)], [g2], [(DM, F32)])
    res = dict(x=x, hv=hv, proj=proj, ar=ar, ai=ai, cr=cr, ci=ci, wdt=wdt, cmt=cmt, s_r=s_r, s_i=s_i, y0p=y0p, y0p_i=y0p_i,
               y1=y1, t=t, ys=ys, ya=ya, ypre=ypre, lse=lse, ms=ms, ma=ma, merged=merged, out=out)
    return x_new, res


def _layer_bwd(l, dxn, r, w, sp):
    tag = f"l{l}b_"
    proj = r["proj"]
    g1 = sp["pre_norm_g"].reshape(1, DM)
    g2 = sp["post_norm_g"].reshape(1, DM)
    dskip = sp["d_skip"].reshape(1, SW)
    bglu = sp["b_glu"].reshape(1, SW)

    def post_b(d, o, g):
        _, vjp = jax.vjp(_rms, o, g)
        do, dg = vjp(d)
        return do, dg

    d_out, dg2 = _ew(tag + "post", post_b, SEQ, 256, [_ri(dxn), _ri(r["out"])], [g2], [(DM, BF16)], [DM])
    dw_out = _mm(tag + "dw_out", r["merged"], d_out, "tn", DM, DM, SEQ, 1024, 1024, SEQ, BF16)
    dmerged = _mm(tag + "dmerged", d_out, w["w_out"], "nt", SEQ, DM, DM, 1024, 1024, 1024, F32)

    def merge_b(d, ms, ma, gs, ga):
        _, vjp = jax.vjp(_merge_fn, ms, ma, gs, ga)
        return vjp(d)

    dms, dma, dgs, dga = _ew(tag + "merge", merge_b, SEQ, 256,
                             [_ri(dmerged), _ri(r["ms"]), _ri(r["ma"]), _ri(proj, DM, CB_GS), _ri(proj, DM, CB_GA)],
                             [], [(DM, BF16)] * 4)
    dw_bs = _mm(tag + "dw_bs", r["ys"], dms, "tn", SW, DM, SEQ, 512, 1024, SEQ, BF16)
    dw_ba = _mm(tag + "dw_ba", r["ya"], dma, "tn", AW, DM, SEQ, 512, 1024, SEQ, BF16)
    dys = _mm(tag + "dys", dms, w["w_branch_s"], "nt", SEQ, SW, DM, 1024, 512, 1024, F32)
    dya = _mm(tag + "dya", dma, w["w_branch_a"], "nt", SEQ, AW, DM, 1024, 512, 1024, F32)

    dqkv, dza = _attention_bwd(tag + "attn", proj, dya, r["ypre"], r["lse"])

    def ys_b(d, y1, t, z, b_):
        _, vjp = jax.vjp(_ys_fn, y1, t, z, b_)
        dy1, dt, dz, _ = vjp(d)
        return dy1, dt, dz, _colsum(dt)

    dy1a, dt, dzs, dbglu = _ew(tag + "ys", ys_b, SEQ, 256,
                               [_ri(_to_chunked(dys)), _ri(r["y1"]), _ri(r["t"]), _ri(proj, SW, CB_ZS)],
                               [bglu], [(SW, F32), (SW, BF16), (SW, BF16)], [SW])
    dw_glu = _mm(tag + "dw_glu", r["y1"], dt, "tn", SW, SW, SEQ, 512, 512, SEQ, BF16)
    dy1b = _mm(tag + "dy1b", dt, w["w_glu"], "nt", SEQ, SW, SW, 1024, 512, 512, F32)

    def y1_b(da, db, y0p, y0p_i, u, d_):
        _, vjp = jax.vjp(_y1_fn, y0p + y0p_i, u, d_)
        dy0, du, dd = vjp(da + db)
        return dy0, du, dd

    dy0, du_skip, ddskip = _ew(tag + "y1", y1_b, SEQ, 256,
                               [_ri(dy1a), _ri(dy1b), _ri(r["y0p"]), _ri(r["y0p_i"]), _ri(proj, SW, CB_U)], [dskip],
                               [(SW, BF16), (SW, F32)], [SW])
    g_s = _mm(tag + "dstate", dy0, r["cmt"], "nn", SEQ, 2 * NS, SW, 1024, 1024, 512, F32,
              b_spec=pl.BlockSpec((512, 1024), lambda i, j, k: (j // 2, j % 2)))
    dcmt_r = _mm(tag + "dcmt_r", dy0, r["s_r"], "tn", SW, NS, SEQ, 512, 1024, SEQ, F32)
    dcmt_i = _mm(tag + "dcmt_i", dy0, r["s_i"], "tn", SW, NS, SEQ, 512, 1024, SEQ, F32)
    lam_r, lam_i = r["ar"].reshape(1, NS), r["ai"].reshape(1, NS)
    a_r, a_i = _scan(tag + "scan", g_s, lam_r, -lam_i, True)
    dlr, dli = _dlam(tag + "dlam", a_r, a_i, r["s_r"], r["s_i"])
    u_spec = pl.BlockSpec((SEQ, SW), lambda i, j, k: (0, CB_U))
    dwdt_r = _mm(tag + "dwdt_r", a_r, proj, "tn", NS, SW, SEQ, 1024, 512, SEQ, F32, b_spec=u_spec)
    dwdt_i = _mm(tag + "dwdt_i", a_i, proj, "tn", NS, SW, SEQ, 1024, 512, SEQ, F32, b_spec=u_spec)
    du_r = _mm(tag + "du_r", a_r, r["wdt"], "nn", SEQ, SW, NS, 1024, 512, 1024, F32)
    du_i = _mm(tag + "du_i", a_i, r["wdt"], "nn", SEQ, SW, NS, 1024, 512, 1024, F32,
               b_spec=pl.BlockSpec((1024, 512), lambda i, j, k: (NS // 1024 + k, j)))
    (du,) = _ew(tag + "du", lambda a, b_, c: (a + b_ + c,), SEQ, 256, [_ri(du_r), _ri(du_i), _ri(du_skip)], [],
                [(SW, BF16)])

    dbbr = _extract(tag + "dbbr", dwdt_r, NST, GCH)
    dbbi = _extract(tag + "dbbi", dwdt_i, NST, GCH)
    bre = sp["b_re"].reshape(NS, GCH)
    bim = sp["b_im"].reshape(NS, GCH)

    def bbar_b(cr, ci, br_, bi_, dr, di):
        _, vjp = jax.vjp(_bbar_fn, cr, ci, br_, bi_)
        return vjp((dr, di))

    dcr, dci, dbre, dbim = _ew(tag + "bbar", bbar_b, NS, NS,
                               [_ri(r["cr"].reshape(NS, 1)), _ri(r["ci"].reshape(NS, 1)), _ri(bre), _ri(bim),
                                _ri(dbbr), _ri(dbbi)], [], [(1, F32), (1, F32), (GCH, F32), (GCH, F32)])

    def lam_b(lre, lim, ldt, dar, dai, dcr_, dci_):
        _, vjp = jax.vjp(_lam_fn, lre, lim, ldt)
        return vjp((dar, dai, dcr_, dci_))

    dlre, dlim, dldt = _ew(tag + "lam", lam_b, NGRP, NGRP,
                           [_ri(sp["lambda_re"]), _ri(sp["lambda_im"]), _ri(sp["log_dt"].reshape(NGRP, 1)),
                            _ri(dlr.reshape(NGRP, NST)), _ri(dli.reshape(NGRP, NST)),
                            _ri(dcr.reshape(NGRP, NST)), _ri(dci.reshape(NGRP, NST))], [],
                           [(NST, F32), (NST, F32), (1, F32)])
    dc_re = _extract(tag + "dc_re", dcmt_r, GCH, NST).reshape(NGRP, GCH, NST)
    dc_im = -_extract(tag + "dc_im", dcmt_i, GCH, NST).reshape(NGRP, GCH, NST)

    dq, dk, dv = ([dqkv[3 * g + i] for g in range(3)] for i in range(3))
    dproj = jnp.concatenate([du, dzs, *dq, *dk, *dv, dza, dgs, dga], axis=1)
    dw_in = _mm(tag + "dw_in", r["hv"], dproj, "tn", DM, NCOL, SEQ, 1024, 512, SEQ, BF16,
                a_spec=pl.BlockSpec((None, SEQ, 1024), lambda i, j, k: (_row_order(j), 0, 0)),
                o_spec=pl.BlockSpec((None, 1024, 512), lambda i, j, k: (j // 2, 0, j % 2)), out_shape=(NDEV, DM, DM))

    blocks = (lambda k: CB_Q + k, lambda k: k)
    counts = (NCOL // 512 - CB_Q, CB_Q)
    dh = []
    for o in range(2):
        cb = blocks[o]
        dh.append(_mm(tag + f"dh{o}", dproj, w["w_in"], "nt", SEQ, DM, counts[o] * 512, 1024, 1024, 512, F32,
                      a_spec=pl.BlockSpec((1024, 512), functools.partial(lambda i, j, k, cb: (i, cb(k)), cb=cb)),
                      b_spec=pl.BlockSpec((None, 1024, 512),
                                          functools.partial(lambda i, j, k, cb: (cb(k) // 2, 0, cb(k) % 2), cb=cb))))
    dh = [dh[0], _from_chunked(dh[1])]

    def pre_b(d, dh0, dh1, x_, g):
        _, vjp = jax.vjp(_rms, x_, g)
        dx_, dg = vjp(dh0 + dh1)
        return d + dx_, dg

    dx, dg1 = _ew(tag + "pre", pre_b, SEQ, 256, [_ri(dxn)] + [_ri(t_) for t_ in dh] + [_ri(r["x"])], [g1],
                  [(DM, F32)], [DM])

    big = dict(w_in=dw_in, w_glu=dw_glu.reshape(NDEV, SW // NDEV, SW),
               w_branch_s=dw_bs.reshape(SW, NDEV, DM // NDEV).transpose(1, 0, 2),
               w_branch_a=dw_ba.reshape(AW, NDEV, DM // NDEV).transpose(1, 0, 2),
               w_out=dw_out.reshape(NDEV, DM // NDEV, DM))
    small = dict(pre_norm_g=dg1.reshape(DM), lambda_re=dlre, lambda_im=dlim, log_dt=dldt.reshape(NGRP),
                 b_re=dbre.reshape(NGRP, NST, GCH), b_im=dbim.reshape(NGRP, NST, GCH), c_re=dc_re, c_im=dc_im,
                 d_skip=ddskip.reshape(SW), b_glu=dbglu.reshape(SW), post_norm_g=dg2.reshape(DM))
    return dx, big, small


_HBM = pl.BlockSpec(memory_space=pltpu.HBM)
_SEM = pl.BlockSpec(memory_space=pltpu.SEMAPHORE)
_EFFECT = pltpu.SideEffectType.DATAFLOW_SIDE_EFFECTING


def _remote_copies(srcs, dsts, send_sems, recv_sems, gather):
    x, y, c = lax.axis_index("x"), lax.axis_index("y"), lax.axis_index("c")
    me = 4 * x + 2 * y + c
    copies = []
    for i in range(len(srcs)):
        for k in range(1, NDEV):
            peer = (x ^ (k >> 2), y ^ ((k >> 1) & 1), c ^ (k & 1))
            src = srcs[i] if gather[i] else srcs[i].at[me ^ k]
            copies.append(pltpu.make_async_remote_copy(
                src_ref=src, dst_ref=dsts[i].at[me], send_sem=send_sems[i], recv_sem=recv_sems[i],
                device_id=peer, device_id_type=pl.DeviceIdType.MESH))
    return copies


def _all_seven(dst, send_sem, recv_sem):
    seven = dst.at[pl.ds(0, NDEV - 1)]
    me = (lax.axis_index("x"), lax.axis_index("y"), lax.axis_index("c"))
    return pltpu.make_async_remote_copy(src_ref=seven, dst_ref=seven, send_sem=send_sem, recv_sem=recv_sem,
                                        device_id=me, device_id_type=pl.DeviceIdType.MESH)


def _own_slabs(name, arrs, gather, after):
    n = len(arrs)
    me = (4 * lax.axis_index("x") + 2 * lax.axis_index("y") + lax.axis_index("c")).astype(jnp.int32).reshape(1)

    def body(me_ref, *refs):
        for src, dst in zip(refs[:n], refs[n + 1:]):
            dst[...] = src[...]

    def zeros(k):
        return (0,) * k

    in_specs, out_specs, out_shape = [], [], []
    for a, g in zip(arrs, gather):
        slab = a.shape if g else a.shape[1:]
        nd = len(slab)
        if g:
            in_specs.append(pl.BlockSpec(slab, functools.partial(lambda i, me_ref, nd: zeros(nd), nd=nd)))
        else:
            in_specs.append(pl.BlockSpec((None,) + slab, functools.partial(lambda i, me_ref, nd: (me_ref[0],) + zeros(nd), nd=nd)))
        out_specs.append(pl.BlockSpec((None,) + slab, functools.partial(lambda i, me_ref, nd: (me_ref[0],) + zeros(nd), nd=nd)))
        out_shape.append(jax.ShapeDtypeStruct((NDEV,) + slab, a.dtype))
    in_specs.append(pl.BlockSpec(memory_space=pl.ANY))
    return pl.pallas_call(
        body, name=name, out_shape=out_shape,
        grid_spec=pltpu.PrefetchScalarGridSpec(num_scalar_prefetch=1, grid=(1,), in_specs=in_specs, out_specs=out_specs),
        compiler_params=_params("arbitrary"),
    )(me, *arrs, after)


def _exchange_start(name, arrs, gather, after):
    n = len(arrs)
    lands = _own_slabs(name + "_own", arrs, gather, after)

    def body(*refs):
        srcs, dsts = refs[:n], refs[n:2 * n]
        send_sems, recv_sems = refs[2 * n:3 * n], refs[3 * n:4 * n]
        token = refs[-1]
        for cp in _remote_copies(srcs, dsts, send_sems, recv_sems, gather):
            cp.start()
        token[...] = jnp.zeros(token.shape, token.dtype)

    thru = [pltpu.HBM(a.shape, a.dtype) for a in list(arrs) + list(lands)]
    outs = pl.pallas_call(
        body, name=name,
        out_shape=(*[pltpu.SemaphoreType.DMA(())] * (2 * n), *thru, jax.ShapeDtypeStruct((8, 128), F32)),
        in_specs=[_HBM] * (2 * n),
        out_specs=(*[_SEM] * (2 * n), *[_HBM] * (2 * n), pl.BlockSpec(memory_space=pltpu.VMEM)),
        input_output_aliases={i: 2 * n + i for i in range(2 * n)},
        compiler_params=pltpu.CompilerParams(has_side_effects=_EFFECT),
    )(*[pltpu.with_memory_space_constraint(a, pltpu.HBM) for a in list(arrs) + list(lands)])
    return dict(send=outs[:n], recv=outs[n:2 * n], srcs=outs[2 * n:3 * n], lands=outs[3 * n:4 * n], token=outs[-1],
                gather=gather)


def _exchange_wait(name, started, after):
    n = len(started["srcs"])

    def body(*refs):
        dsts = refs[n:2 * n]
        send_sems, recv_sems = refs[2 * n:3 * n], refs[3 * n:4 * n]
        for i in range(n):
            cp = _all_seven(dsts[i], send_sems[i], recv_sems[i])
            cp.wait_send()
            cp.wait_recv()

    bufs = list(started["srcs"]) + list(started["lands"])
    outs = pl.pallas_call(
        body, name=name, out_shape=tuple(pltpu.HBM(a.shape, a.dtype) for a in bufs),
        in_specs=[_HBM] * (2 * n) + [_SEM] * (2 * n) + [pl.BlockSpec(memory_space=pl.ANY)], out_specs=(_HBM,) * (2 * n),
        input_output_aliases={i: i for i in range(2 * n)},
        compiler_params=pltpu.CompilerParams(has_side_effects=_EFFECT),
    )(*bufs, *started["send"], *started["recv"], after)
    return outs[n:]


def _adamw(name, parts, w, m, v, br):
    rows, cols = w.shape

    def fn(*a):
        g = a[0].astype(F32)
        for p in a[1:NDEV]:
            g = g + p.astype(F32)
        w_, m_, v_ = a[NDEV:]
        m2 = B1 * m_ + (1.0 - B1) * g
        v2 = B2 * v_ + (1.0 - B2) * (g * g)
        m_hat = m2 / (1.0 - B1 ** STEP)
        v_hat = v2 / (1.0 - B2 ** STEP)
        delta = -LR * (m_hat / (jnp.sqrt(v_hat) + ADAM_EPS) + WD * w_)
        return g, delta, m2, v2

    ins = [_ri(parts, cols, 0, d * (rows // br)) for d in range(NDEV)] + [_ri(w), _ri(m), _ri(v)]
    return _ew(name, fn, rows, br, ins, [], [(cols, F32)] * 4)


SMALL = ("pre_norm_g", "lambda_re", "lambda_im", "log_dt", "b_re", "b_im", "c_re", "c_im", "d_skip", "b_glu",
         "post_norm_g")
BIG = ("w_in", "w_glu", "w_branch_s", "w_branch_a", "w_out")
WEIGHTS = ("pre_norm_g", "w_in", "lambda_re", "lambda_im", "log_dt", "b_re", "b_im", "c_re", "c_im", "d_skip",
           "w_glu", "b_glu", "w_branch_s", "w_branch_a", "w_out", "post_norm_g")
PACK_COLS = 1024
PACK_BR = 136


def _pack_pieces(d):
    pieces = [d[k].astype(F32).reshape(-1) for k in SMALL]
    used = sum(p.shape[0] for p in pieces)
    assert used <= PACK_BR * PACK_COLS
    return pieces + [jnp.zeros((PACK_BR * PACK_COLS - used,), F32)]


def _pack_layer(d):
    return jnp.concatenate(_pack_pieces(d)).reshape(PACK_BR, PACK_COLS)


def _pack(d):
    pieces = [p for l in range(DEPTH) for p in _pack_pieces({k: d[k][l] for k in SMALL})]
    return jnp.concatenate(pieces).reshape(DEPTH * PACK_BR, PACK_COLS)


def _unpack(p, like):
    flat = p.reshape(DEPTH, PACK_BR * PACK_COLS)
    out, off = {}, 0
    for k in SMALL:
        n = like[k].size // DEPTH
        out[k] = flat[:, off:off + n].reshape(like[k].shape)
        off += n
    return out


def _local_step(x, target, small, weights_of, grads_done):
    res, ws = [], []
    for l in range(DEPTH):
        w_l, tok = weights_of(l, x)
        sp = {k: small[k][l] for k in SMALL}
        if tok is not None:
            sp["pre_norm_g"] = sp["pre_norm_g"] + tok[0, 0]
        x, r = _layer_fwd(l, x, w_l, sp)
        res.append(r)
        ws.append(w_l)

    def loss_fn(y, t):
        e = y - t
        return e * (1.0 / DM), jnp.sum(_colsum(0.5 * e * e * (1.0 / DM)), axis=1, keepdims=True)

    dx, loss = _ew("loss", loss_fn, SEQ, 256, [_ri(x), _ri(target)], [], [(DM, F32)], [1])
    tok = None
    for l in reversed(range(DEPTH)):
        sp = {k: small[k][l] for k in SMALL}
        if tok is not None:
            sp["post_norm_g"] = sp["post_norm_g"] + tok[0, 0]
        dx, big, sm = _layer_bwd(l, dx, res[l], ws[l], sp)
        tok = grads_done(l, big, sm)
    return loss.reshape(()), dx


def _full_weights(gathered):
    g = gathered
    return dict(
        w_in=g["w_in"],
        w_glu=g["w_glu"].reshape(SW, SW),
        w_branch_s=g["w_branch_s"].transpose(1, 0, 2).reshape(SW, DM),
        w_branch_a=g["w_branch_a"].transpose(1, 0, 2).reshape(AW, DM),
        w_out=g["w_out"].reshape(DM, DM),
    )


def kernel(x, pre_norm_g, w_in, lambda_re, lambda_im, log_dt, b_re, b_im, c_re, c_im, d_skip, w_glu, b_glu, w_branch_s, w_branch_a, w_out, post_norm_g, loss_target, m_pre_norm_g, m_w_in, m_lambda_re, m_lambda_im, m_log_dt, m_b_re, m_b_im, m_c_re, m_c_im, m_d_skip, m_w_glu, m_b_glu, m_w_branch_s, m_w_branch_a, m_w_out, m_post_norm_g, v_pre_norm_g, v_w_in, v_lambda_re, v_lambda_im, v_log_dt, v_b_re, v_b_im, v_c_re, v_c_im, v_d_skip, v_w_glu, v_b_glu, v_w_branch_s, v_w_branch_a, v_w_out, v_post_norm_g):
    wts = dict(pre_norm_g=pre_norm_g, w_in=w_in, lambda_re=lambda_re, lambda_im=lambda_im, log_dt=log_dt, b_re=b_re,
               b_im=b_im, c_re=c_re, c_im=c_im, d_skip=d_skip, w_glu=w_glu, b_glu=b_glu, w_branch_s=w_branch_s,
               w_branch_a=w_branch_a, w_out=w_out, post_norm_g=post_norm_g)
    mom = dict(pre_norm_g=m_pre_norm_g, w_in=m_w_in, lambda_re=m_lambda_re, lambda_im=m_lambda_im, log_dt=m_log_dt,
               b_re=m_b_re, b_im=m_b_im, c_re=m_c_re, c_im=m_c_im, d_skip=m_d_skip, w_glu=m_w_glu, b_glu=m_b_glu,
               w_branch_s=m_w_branch_s, w_branch_a=m_w_branch_a, w_out=m_w_out, post_norm_g=m_post_norm_g)
    var = dict(pre_norm_g=v_pre_norm_g, w_in=v_w_in, lambda_re=v_lambda_re, lambda_im=v_lambda_im, log_dt=v_log_dt,
               b_re=v_b_re, b_im=v_b_im, c_re=v_c_re, c_im=v_c_im, d_skip=v_d_skip, w_glu=v_w_glu, b_glu=v_b_glu,
               w_branch_s=v_w_branch_s, w_branch_a=v_w_branch_a, w_out=v_w_out, post_norm_g=v_post_norm_g)

    def gather_start(l, after):
        return _exchange_start(f"gather_start{l}", [wts[k][l].astype(BF16) for k in BIG], [True] * len(BIG), after)

    gathering = {0: gather_start(0, x)}
    sending = {}

    def weights_of(l, x_l):
        gathered = _exchange_wait(f"gather_wait{l}", gathering[l], x_l)
        tok = None
        if l + 1 < DEPTH:
            gathering[l + 1] = gather_start(l + 1, gathered[0])
            tok = gathering[l + 1]["token"]
        return _full_weights(dict(zip(BIG, gathered))), tok

    def grads_done(l, big, sm):
        sending[l] = _exchange_start(f"grads_start{l}", [big[k] for k in BIG] + [_pack_layer(sm)],
                                     [False] * len(BIG) + [True], big["w_in"])
        return sending[l]["token"]

    loss, dx = _local_step(x[0], loss_target[0], wts, weights_of, grads_done)
    loss = lax.psum(loss, ("x", "y", "c"))
    recv_l = [_exchange_wait(f"grads_wait{l}", sending[l], dx) for l in range(DEPTH)]
    recv = [jnp.stack([r[i] for r in recv_l], axis=1) for i in range(len(BIG))]
    recv.append(jnp.concatenate([r[len(BIG)] for r in recv_l], axis=1))

    grads, delta, new_m, new_v = {}, {}, {}, {}
    for k, parts in zip(BIG, recv[:len(BIG)]):
        shape = wts[k].shape
        cols = shape[-1]
        rows = wts[k].size // cols
        br = min(rows, 256)
        outs = _adamw("adamw_" + k, parts.reshape(NDEV * rows, cols), wts[k].reshape(rows, cols),
                      mom[k].reshape(rows, cols), var[k].reshape(rows, cols), br)
        grads[k], delta[k], new_m[k], new_v[k] = (o.reshape(shape) for o in outs)
    pw, pm, pv = (_pack({k: d[k] for k in SMALL}) for d in (wts, mom, var))
    rows = pw.shape[0]
    outs = _adamw("adamw_small", recv[-1].reshape(NDEV * rows, PACK_COLS), pw, pm, pv, PACK_BR)
    for d, o in zip((grads, delta, new_m, new_v), outs):
        d.update(_unpack(o, wts))

    return (loss, dx[None], *[grads[k] for k in WEIGHTS], *[delta[k] for k in WEIGHTS],
            *[new_m[k] for k in WEIGHTS], *[new_v[k] for k in WEIGHTS])
```

```python
import functools
import math

import jax
import jax.numpy as jnp
from jax import lax
from jax.experimental import pallas as pl
from jax.experimental.pallas import tpu as pltpu

F32 = jnp.float32
BF16 = jnp.bfloat16

NDEV = 8
DEPTH = 4
SEQ = 2048
DM = 1024
NCOL = 8192
SW = 512
NGRP = 32
GCH = 16
NST = 64
NS = NGRP * NST
HD = 128
AW = 512
DILATIONS = (1, 4, 16)
ABLK = 128
RMS_EPS = 1e-6
LR, B1, B2, ADAM_EPS, WD, STEP = 0.001, 0.9, 0.999, 1e-08, 0.01, 10

CB_U, CB_ZS, CB_Q, CB_K, CB_V, CB_ZA = 0, 1, 2, 5, 8, 11
CB_GS, CB_GA = 6, 7

VMEM_LIMIT = 56 * 2 ** 20


def _row_order(j):
    return jnp.where(j < CB_Q, 1, 0)


def _params(*sem):
    return pltpu.CompilerParams(dimension_semantics=sem, vmem_limit_bytes=VMEM_LIMIT)


def _ew(name, fn, rows, br, row_ins, bc_ins, row_outs, red_outs=()):
    n_in = len(row_ins) + len(bc_ins)
    n_ro = len(row_outs)
    steps = rows // br
    assert steps * br == rows

    def body(*refs):
        vals = fn(*[r[...] for r in refs[:n_in]])
        outs = refs[n_in:]
        for r, v in zip(outs[:n_ro], vals[:n_ro]):
            r[...] = v.astype(r.dtype)
        if red_outs:
            @pl.when(pl.program_id(0) == 0)
            def _():
                for r in outs[n_ro:]:
                    r[...] = jnp.zeros(r.shape, r.dtype)
            for r, v in zip(outs[n_ro:], vals[n_ro:]):
                r[...] += v

    in_specs = []
    for (_, w, cb, rb) in row_ins:
        in_specs.append(pl.BlockSpec((br, w), functools.partial(lambda i, cb, rb: (rb + i, cb), cb=cb, rb=rb)))
    for a in bc_ins:
        in_specs.append(pl.BlockSpec(a.shape, functools.partial(lambda i, nd: (0,) * nd, nd=a.ndim)))
    out_specs = [pl.BlockSpec((br, w), lambda i: (i, 0)) for (w, _) in row_outs]
    out_specs += [pl.BlockSpec((1, w), lambda i: (0, 0)) for w in red_outs]
    out_shape = [jax.ShapeDtypeStruct((rows, w), dt) for (w, dt) in row_outs]
    out_shape += [jax.ShapeDtypeStruct((1, w), F32) for w in red_outs]
    return pl.pallas_call(
        body, name=name, grid=(steps,), in_specs=in_specs, out_specs=out_specs, out_shape=out_shape,
        compiler_params=_params("arbitrary"),
    )(*[a for (a, _, _, _) in row_ins], *bc_ins)


def _ri(a, w=None, cb=0, rb=0):
    return (a, a.shape[1] if w is None else w, cb, rb)


_DIMS = {"nn": ((1,), (0,)), "nt": ((1,), (1,)), "tn": ((0,), (0,))}


def _mm(name, a, b, mode, M, N, K, bm, bn, bk, out_dtype, a_spec=None, b_spec=None, o_spec=None, out_shape=None):
    nk = K // bk
    assert M % bm == 0 and N % bn == 0 and nk * bk == K

    own_acc = nk > 1 and out_dtype != F32

    def body(a_ref, b_ref, o_ref, *scratch):
        part = lax.dot_general(a_ref[...].astype(BF16), b_ref[...].astype(BF16), (_DIMS[mode], ((), ())),
                               preferred_element_type=F32)
        if nk == 1:
            o_ref[...] = part.astype(o_ref.dtype)
            return
        k = pl.program_id(2)
        acc_ref = scratch[0] if own_acc else o_ref

        @pl.when(k == 0)
        def _():
            acc_ref[...] = part

        @pl.when(k > 0)
        def _():
            acc_ref[...] += part

        if own_acc:
            @pl.when(k == nk - 1)
            def _():
                o_ref[...] = acc_ref[...].astype(o_ref.dtype)

    if a_spec is None:
        a_spec = (pl.BlockSpec((bk, bm), lambda i, j, k: (k, i)) if mode == "tn"
                  else pl.BlockSpec((bm, bk), lambda i, j, k: (i, k)))
    if b_spec is None:
        b_spec = (pl.BlockSpec((bn, bk), lambda i, j, k: (j, k)) if mode == "nt"
                  else pl.BlockSpec((bk, bn), lambda i, j, k: (k, j)))
    if o_spec is None:
        o_spec = pl.BlockSpec((bm, bn), lambda i, j, k: (i, j))
    if out_shape is None:
        out_shape = (M, N)
    return pl.pallas_call(
        body, name=name, grid=(M // bm, N // bn, nk), in_specs=[a_spec, b_spec], out_specs=o_spec,
        out_shape=jax.ShapeDtypeStruct(out_shape, out_dtype),
        scratch_shapes=[pltpu.VMEM((bm, bn), F32)] if own_acc else [],
        compiler_params=_params("parallel", "parallel", "arbitrary"),
    )(a, b)


SCAN_LANES = 512
SCAN_CHUNKS = 8


def _to_chunked(a):
    return a.reshape(SCAN_CHUNKS, SEQ // SCAN_CHUNKS, -1).transpose(1, 0, 2).reshape(SEQ, -1)


def _from_chunked(a):
    return a.reshape(SEQ // SCAN_CHUNKS, SCAN_CHUNKS, -1).transpose(1, 0, 2).reshape(SEQ, -1)


def _scan(name, d, lam_r, lam_i, reverse):
    T = SEQ // SCAN_CHUNKS
    bl = SCAN_LANES
    nblk = NS // bl
    assert T == 2 ** 8

    def body(dr_ref, di_ref, ar_ref, ai_ref, s_ref):
        sr_ref, si_ref = s_ref.at[0], s_ref.at[1]
        ar = jnp.broadcast_to(ar_ref[...], (SCAN_CHUNKS, bl))
        ai = jnp.broadcast_to(ai_ref[...], (SCAN_CHUNKS, bl))
        zero = jnp.zeros((SCAN_CHUNKS, bl), F32)

        def tile(j):
            return pl.ds(pl.multiple_of(j * SCAN_CHUNKS, SCAN_CHUNKS), SCAN_CHUNKS)

        def step(jj, carry):
            sr, si = carry
            j = T - 1 - jj if reverse else jj
            nr = ar * sr - ai * si + dr_ref[tile(j), :]
            ni = ar * si + ai * sr + di_ref[tile(j), :]
            sr_ref[tile(j), :] = nr
            si_ref[tile(j), :] = ni
            return nr, ni

        er, ei = lax.fori_loop(0, T, step, (zero, zero), unroll=4)

        pr, pi = ar[0:1], ai[0:1]
        for _ in range(8):
            pr, pi = pr * pr - pi * pi, 2.0 * pr * pi
        rows = lax.broadcasted_iota(jnp.int32, (SCAN_CHUNKS, bl), 0)
        cr, ci = zero, zero
        xr = jnp.zeros((1, bl), F32)
        xi = jnp.zeros((1, bl), F32)
        order = range(SCAN_CHUNKS - 2, -1, -1) if reverse else range(1, SCAN_CHUNKS)
        for c in order:
            src = c + 1 if reverse else c - 1
            nxr = pr * xr - pi * xi + er[src:src + 1]
            nxi = pr * xi + pi * xr + ei[src:src + 1]
            xr, xi = nxr, nxi
            cr = jnp.where(rows == c, xr, cr)
            ci = jnp.where(rows == c, xi, ci)

        def fix(jj, pw):
            pwr, pwi = pw
            j = T - 1 - jj if reverse else jj
            sr_ref[tile(j), :] = sr_ref[tile(j), :] + (pwr * cr - pwi * ci)
            si_ref[tile(j), :] = si_ref[tile(j), :] + (pwr * ci + pwi * cr)
            return pwr * ar - pwi * ai, pwr * ai + pwi * ar

        lax.fori_loop(0, T, fix, (ar, ai), unroll=4)

    return pl.pallas_call(
        body, name=name, grid=(nblk,),
        in_specs=[pl.BlockSpec((SEQ, bl), lambda i: (0, i)),
                  pl.BlockSpec((SEQ, bl), lambda i: (0, nblk + i)),
                  pl.BlockSpec((1, bl), lambda i: (0, i)),
                  pl.BlockSpec((1, bl), lambda i: (0, i))],
        out_specs=pl.BlockSpec((2, SEQ, bl), lambda i: (0, 0, i)),
        out_shape=jax.ShapeDtypeStruct((2, SEQ, NS), F32),
        compiler_params=_params("arbitrary"),
    )(d, d, lam_r, lam_i)


def _plane_spec(bm, bk):
    per = NS // bk
    return pl.BlockSpec((None, bm, bk), lambda i, j, k: (k // per, i, k % per))


def _dlam(name, a, s):
    bl = 256

    def prev(s_ref):
        last = pltpu.roll(s_ref[SEQ - SCAN_CHUNKS:SEQ, :], 1, 0)
        first = jnp.where(lax.broadcasted_iota(jnp.int32, (SCAN_CHUNKS, bl), 0) > 0, last, 0.0)
        return jnp.concatenate([first, s_ref[0:SEQ - SCAN_CHUNKS, :]], axis=0)

    def body(ar_ref, ai_ref, sr_ref, si_ref, or_ref, oi_ref):
        spr, spi = prev(sr_ref), prev(si_ref)
        a_r_, a_i_ = ar_ref[...], ai_ref[...]
        or_ref[...] = jnp.sum(a_r_ * spr + a_i_ * spi, axis=0, keepdims=True)
        oi_ref[...] = jnp.sum(a_i_ * spr - a_r_ * spi, axis=0, keepdims=True)

    re_spec = pl.BlockSpec((None, SEQ, bl), lambda i: (0, 0, i))
    im_spec = pl.BlockSpec((None, SEQ, bl), lambda i: (1, 0, i))
    ospec = pl.BlockSpec((1, bl), lambda i: (0, i))
    return pl.pallas_call(
        body, name=name, grid=(NS // bl,), in_specs=[re_spec, im_spec] * 2, out_specs=[ospec, ospec],
        out_shape=[jax.ShapeDtypeStruct((1, NS), F32)] * 2,
        compiler_params=_params("arbitrary"),
    )(a, a, s, s)


def _scores(qb, kb, prev):
    s = lax.dot_general(qb, kb, (((1,), (1,)), ((), ())), preferred_element_type=F32) * (HD ** -0.5)
    row = lax.broadcasted_iota(jnp.int32, (ABLK, ABLK), 0)
    col = lax.broadcasted_iota(jnp.int32, (ABLK, ABLK), 1)
    return jnp.where((col >= row) if prev else (col <= row), s, -1e30)


def _block_rows(dil, r, b):
    if dil == 1:
        return pl.ds(pl.multiple_of(b * ABLK, ABLK), ABLK)
    return pl.ds(r + dil * ABLK * b, ABLK, stride=dil)


def _group_blocks(dil):
    nb = SEQ // dil // ABLK
    shift = nb.bit_length() - 1
    return nb, (lambda idx: (idx >> shift, idx & (nb - 1)))


def _qkv_specs(j_of):
    return [pl.BlockSpec((SEQ, HD), functools.partial(lambda j, c: (0, c + j_of(j)), c=(cb + g) * 4))
            for g in range(3) for cb in (CB_Q, CB_K, CB_V)]


def _attention_fwd(name, proj):
    def body(*refs):
        qkv, z_ref = refs[:9], refs[9]
        y_ref, ya_ref, l_ref, den_ref = refs[10:]
        for g, dil in enumerate(DILATIONS):
            q_ref, k_ref, v_ref = qkv[3 * g:3 * g + 3]
            nb, where = _group_blocks(dil)

            def step(idx, c, g=g, dil=dil, nb=nb, where=where, q_ref=q_ref, k_ref=k_ref, v_ref=v_ref):
                r, b = where(idx)
                rows = _block_rows(dil, r, b)
                qb = q_ref[rows, :].astype(BF16)
                s_c = _scores(qb, k_ref[rows, :].astype(BF16), False)
                m = jnp.max(s_c, axis=-1, keepdims=True)
                if nb > 1:
                    prev = _block_rows(dil, r, jnp.maximum(b - 1, 0))
                    s_p = jnp.where(b > 0, _scores(qb, k_ref[prev, :].astype(BF16), True), -1e30)
                    m = jnp.maximum(m, jnp.max(s_p, axis=-1, keepdims=True))
                if g > 0:
                    m_old = l_ref[rows, :][:, 0:1]
                    m = jnp.maximum(m, m_old)
                p_c = jnp.exp(s_c - m)
                den = jnp.sum(p_c, axis=-1, keepdims=True)
                acc = jnp.dot(p_c.astype(BF16), v_ref[rows, :].astype(BF16), preferred_element_type=F32)
                if nb > 1:
                    p_p = jnp.exp(s_p - m)
                    den = den + jnp.sum(p_p, axis=-1, keepdims=True)
                    acc = acc + jnp.dot(p_p.astype(BF16), v_ref[prev, :].astype(BF16), preferred_element_type=F32)
                if g > 0:
                    alpha = jnp.exp(m_old - m)
                    den = den + alpha * den_ref[rows, :][:, 0:1]
                    acc = acc + alpha * y_ref[rows, :]
                y_ref[rows, :] = acc
                l_ref[rows, :] = jnp.broadcast_to(m, (ABLK, HD))
                den_ref[rows, :] = jnp.broadcast_to(den, (ABLK, HD))
                return c

            lax.fori_loop(0, SEQ // ABLK, step, 0, unroll=2)
        den = den_ref[...]
        y = y_ref[...] / den
        y_ref[...] = y
        ya_ref[...] = (y * _silu(z_ref[...])).astype(ya_ref.dtype)
        l_ref[...] = l_ref[...] + jnp.log(den)

    ospec = pl.BlockSpec((SEQ, HD), lambda j: (0, j))
    return pl.pallas_call(
        body, name=name, grid=(AW // HD,),
        in_specs=_qkv_specs(lambda j: j) + [pl.BlockSpec((SEQ, HD), lambda j: (0, CB_ZA * 4 + j))],
        out_specs=[ospec, ospec, ospec],
        out_shape=[jax.ShapeDtypeStruct((SEQ, AW), F32), jax.ShapeDtypeStruct((SEQ, AW), BF16),
                   jax.ShapeDtypeStruct((SEQ, AW), F32)],
        scratch_shapes=[pltpu.VMEM((SEQ, HD), F32)],
        compiler_params=_params("parallel"),
    )(*([proj] * 10))


def _attention_bwd(name, proj, dya, y, lse):
    def tn(a, b_):
        return lax.dot_general(a, b_, (((0,), (0,)), ((), ())), preferred_element_type=F32)

    def nt(a, b_):
        return lax.dot_general(a, b_, (((1,), (1,)), ((), ())), preferred_element_type=F32)

    def body(*refs):
        qkv, z_ref, dya_ref, y_ref, l_ref = refs[:9], refs[9], refs[10], refs[11], refs[12]
        outs, dza_ref = refs[13:22], refs[22]
        dy_s, dsum_s, dq_s, dk_s, dv_s = refs[23:]
        _, vjp = jax.vjp(lambda y_, z_: y_ * _silu(z_), y_ref[...], z_ref[...])
        dy, dz = vjp(dya_ref[...])
        dza_ref[...] = dz.astype(dza_ref.dtype)
        dy_s[...] = dy
        dsum_s[...] = jnp.broadcast_to(jnp.sum(dy * y_ref[...], axis=-1, keepdims=True), (SEQ, HD))
        for g, dil in enumerate(DILATIONS):
            q_ref, k_ref, v_ref = qkv[3 * g:3 * g + 3]
            nb, where = _group_blocks(dil)
            dk_s[...] = jnp.zeros(dk_s.shape, F32)
            dv_s[...] = jnp.zeros(dv_s.shape, F32)

            def step(idx, c, dil=dil, nb=nb, where=where, q_ref=q_ref, k_ref=k_ref, v_ref=v_ref):
                r, b = where(idx)
                rows = _block_rows(dil, r, b)
                qb = q_ref[rows, :].astype(BF16)
                dyb = dy_s[rows, :].astype(BF16)
                lb = l_ref[rows, :][:, 0:1]
                db = dsum_s[rows, :][:, 0:1]

                def side(krows, prev):
                    kb = k_ref[krows, :].astype(BF16)
                    vb = v_ref[krows, :].astype(BF16)
                    s = _scores(qb, kb, prev)
                    if prev:
                        s = jnp.where(b > 0, s, -1e30)
                    p = jnp.exp(s - lb)
                    ds = (p * (nt(dyb, vb) - db) * (HD ** -0.5)).astype(BF16)
                    dk_s[krows, :] = dk_s[krows, :] + tn(ds, qb)
                    dv_s[krows, :] = dv_s[krows, :] + tn(p.astype(BF16), dyb)
                    return jnp.dot(ds, kb, preferred_element_type=F32)

                dq = side(rows, False)
                if nb > 1:
                    dq = dq + side(_block_rows(dil, r, jnp.maximum(b - 1, 0)), True)
                dq_s[rows, :] = dq
                return c

            lax.fori_loop(0, SEQ // ABLK, step, 0, unroll=2)
            for o_ref, s_ref in zip(outs[3 * g:3 * g + 3], (dq_s, dk_s, dv_s)):
                o_ref[...] = s_ref[...].astype(o_ref.dtype)

    ospec = pl.BlockSpec((SEQ, HD), lambda j: (0, j))
    outs = pl.pallas_call(
        body, name=name, grid=(AW // HD,),
        in_specs=_qkv_specs(lambda j: j) + [pl.BlockSpec((SEQ, HD), lambda j: (0, CB_ZA * 4 + j))] + [ospec] * 3,
        out_specs=[ospec] * 10, out_shape=[jax.ShapeDtypeStruct((SEQ, AW), BF16)] * 10,
        scratch_shapes=[pltpu.VMEM((SEQ, HD), F32)] * 5,
        compiler_params=_params("parallel"),
    )(*([proj] * 10), dya, y, lse)
    return outs[:9], outs[9]


def _rms(x, g):
    return x * lax.rsqrt(jnp.mean(x * x, axis=-1, keepdims=True) + RMS_EPS) * g


def _sig(x):
    return 1.0 / (1.0 + jnp.exp(-x))


def _silu(x):
    return x * _sig(x)


def _gelu(x):
    return 0.5 * x * (1.0 + jnp.tanh(math.sqrt(2.0 / math.pi) * (x + 0.044715 * (x * x * x))))


def _y1_fn(y0p, u, dskip):
    return _gelu(y0p + dskip * u)


def _ys_fn(y1, t, z, bglu):
    return y1 * _sig(t + bglu) * _silu(z)


def _merge_fn(ms, ma, gs, ga):
    return _sig(gs) * ms + _sig(ga) * ma


def _colsum(v):
    return jnp.sum(v, axis=0, keepdims=True)


def _lam_fn(lre, lim, ldt):
    a = jnp.minimum(lre, -1e-4)
    dt = jnp.exp(ldt)
    mag = jnp.exp(a * dt)
    ar = mag * jnp.cos(lim * dt)
    ai = mag * jnp.sin(lim * dt)
    den = a * a + lim * lim
    cr = ((ar - 1.0) * a + ai * lim) / den
    ci = (ai * a - (ar - 1.0) * lim) / den
    return ar, ai, cr, ci


def _bbar_fn(cr, ci, bre, bim):
    return cr * bre - ci * bim, cr * bim + ci * bre


def _same_group(rows, a, cols, b):
    r = lax.broadcasted_iota(jnp.int32, (rows, cols), 0) >> (a.bit_length() - 1)
    c = lax.broadcasted_iota(jnp.int32, (rows, cols), 1) >> (b.bit_length() - 1)
    return r == c


def _expand(name, blocks, signs, a, b, dtype):
    rows, cols = NGRP * a, NGRP * b
    assert a & (a - 1) == 0 and b & (b - 1) == 0

    def body(*refs):
        o_ref = refs[-1]
        tile = (lax.broadcasted_iota(jnp.int32, (b, cols), 1) & (b - 1)
                == lax.broadcasted_iota(jnp.int32, (b, cols), 0)).astype(F32)
        keep = _same_group(rows, a, cols, b)
        for i, (ref, sign) in enumerate(zip(refs[:-1], signs)):
            spread = jnp.dot(ref[...], tile, preferred_element_type=F32, precision=lax.Precision.HIGHEST)
            o_ref[i * rows:(i + 1) * rows, :] = jnp.where(keep, sign * spread, 0.0).astype(o_ref.dtype)

    return pl.pallas_call(body, name=name, out_shape=jax.ShapeDtypeStruct((len(blocks) * rows, cols), dtype),
                          compiler_params=pltpu.CompilerParams(vmem_limit_bytes=VMEM_LIMIT))(*blocks)


def _extract(name, m, a, b, at=(0, 0)):
    rows, cols = NGRP * a, NGRP * b
    assert a & (a - 1) == 0 and b & (b - 1) == 0

    def body(m_ref, o_ref):
        tile = (lax.broadcasted_iota(jnp.int32, (cols, b), 0) & (b - 1)
                == lax.broadcasted_iota(jnp.int32, (cols, b), 1)).astype(F32)
        kept = jnp.where(_same_group(rows, a, cols, b), m_ref[...], 0.0)
        o_ref[...] = jnp.dot(kept, tile, preferred_element_type=F32, precision=lax.Precision.HIGHEST)

    return pl.pallas_call(body, name=name, grid=(1,), in_specs=[pl.BlockSpec((rows, cols), lambda i: at)],
                          out_specs=pl.BlockSpec((rows, b), lambda i: (0, 0)),
                          out_shape=jax.ShapeDtypeStruct((rows, b), F32), compiler_params=_params("arbitrary"))(m)


def _layer_fwd(l, x, w, sp):
    tag = f"l{l}_"
    g1 = sp["pre_norm_g"].reshape(1, DM)
    (h,) = _ew(tag + "rms1", lambda x_, g: (_rms(x_, g),), SEQ, 256, [_ri(x)], [g1], [(DM, BF16)])
    hv = jnp.stack([h, _to_chunked(h)])
    win = w["w_in"]
    proj = _mm(tag + "proj", hv, win, "nn", SEQ, NCOL, DM, SEQ, 512, 1024, F32,
               a_spec=pl.BlockSpec((None, SEQ, 1024), lambda i, j, k: (_row_order(j), 0, 0)),
               b_spec=pl.BlockSpec((None, 1024, 512), lambda i, j, k: (j // 2, 0, j % 2)))

    ar, ai, cr, ci = _ew(tag + "lam", _lam_fn, NGRP, NGRP,
                         [_ri(sp["lambda_re"]), _ri(sp["lambda_im"]), _ri(sp["log_dt"].reshape(NGRP, 1))], [],
                         [(NST, F32)] * 4)
    bre = sp["b_re"].reshape(NS, GCH)
    bim = sp["b_im"].reshape(NS, GCH)
    bbr, bbi = _ew(tag + "bbar", _bbar_fn, NS, NS, [_ri(cr.reshape(NS, 1)), _ri(ci.reshape(NS, 1)), _ri(bre), _ri(bim)],
                   [], [(GCH, F32)] * 2)
    wdt = _expand(tag + "wdt", [bbr, bbi], [1.0, 1.0], NST, GCH, BF16)
    cmt = _expand(tag + "cmt", [sp["c_re"].reshape(SW, NST), sp["c_im"].reshape(SW, NST)], [1.0, -1.0], GCH, NST, BF16)
    drive = _mm(tag + "drive", proj, wdt, "nt", SEQ, 2 * NS, SW, 1024, 1024, 512, F32,
                a_spec=pl.BlockSpec((1024, SW), lambda i, j, k: (i, CB_U)))
    lam_r, lam_i = ar.reshape(1, NS), ai.reshape(1, NS)
    s = _scan(tag + "scan", drive, lam_r, lam_i, False)
    y0p = _mm(tag + "readout", s, cmt, "nt", SEQ, SW, 2 * NS, 1024, 512, 1024, F32, a_spec=_plane_spec(1024, 1024),
              b_spec=pl.BlockSpec((512, 1024), lambda i, j, k: (k // 2, k % 2)))
    dskip = sp["d_skip"].reshape(1, SW)
    (y1,) = _ew(tag + "y1", lambda a, u, d: (_y1_fn(a, u, d),), SEQ, 256,
                [_ri(y0p), _ri(proj, SW, CB_U)], [dskip], [(SW, F32)])
    t = _mm(tag + "glu", y1, w["w_glu"], "nn", SEQ, SW, SW, 1024, 512, 512, F32)
    bglu = sp["b_glu"].reshape(1, SW)
    (ys_c,) = _ew(tag + "ys", lambda y1_, t_, z, b_: (_ys_fn(y1_, t_, z, b_),), SEQ, 256,
                  [_ri(y1), _ri(t), _ri(proj, SW, CB_ZS)], [bglu], [(SW, BF16)])
    ys = _from_chunked(ys_c)

    ypre, ya, lse = _attention_fwd(tag + "attn", proj)

    ms = _mm(tag + "branch_s", ys, w["w_branch_s"], "nn", SEQ, DM, SW, 1024, 1024, 512, F32)
    ma = _mm(tag + "branch_a", ya, w["w_branch_a"], "nn", SEQ, DM, AW, 1024, 1024, 512, F32)
    (merged,) = _ew(tag + "merge", lambda a, b_, c, d: (_merge_fn(a, b_, c, d),), SEQ, 256,
                    [_ri(ms), _ri(ma), _ri(proj, DM, CB_GS), _ri(proj, DM, CB_GA)], [], [(DM, BF16)])
    out = _mm(tag + "out", merged, w["w_out"], "nn", SEQ, DM, DM, 1024, 1024, 1024, F32)
    g2 = sp["post_norm_g"].reshape(1, DM)
    (x_new,) = _ew(tag + "post", lambda x_, o, g: (x_ + _rms(o, g),), SEQ, 256, [_ri(x), _ri(out)], [g2], [(DM, F32)])
    res = dict(x=x, hv=hv, proj=proj, ar=ar, ai=ai, cr=cr, ci=ci, wdt=wdt, cmt=cmt, s=s, y0p=y0p,
               y1=y1, t=t, ys=ys, ya=ya, ypre=ypre, lse=lse, ms=ms, ma=ma, merged=merged, out=out)
    return x_new, res


def _layer_bwd(l, dxn, r, w, sp):
    tag = f"l{l}b_"
    proj = r["proj"]
    g1 = sp["pre_norm_g"].reshape(1, DM)
    g2 = sp["post_norm_g"].reshape(1, DM)
    dskip = sp["d_skip"].reshape(1, SW)
    bglu = sp["b_glu"].reshape(1, SW)

    def post_b(d, o, g):
        _, vjp = jax.vjp(_rms, o, g)
        do, dg = vjp(d)
        return do, dg

    d_out, dg2 = _ew(tag + "post", post_b, SEQ, 256, [_ri(dxn), _ri(r["out"])], [g2], [(DM, BF16)], [DM])
    dw_out = _mm(tag + "dw_out", r["merged"], d_out, "tn", DM, DM, SEQ, 1024, 1024, SEQ, BF16)
    dmerged = _mm(tag + "dmerged", d_out, w["w_out"], "nt", SEQ, DM, DM, 1024, 1024, 1024, F32)

    def merge_b(d, ms, ma, gs, ga):
        _, vjp = jax.vjp(_merge_fn, ms, ma, gs, ga)
        return vjp(d)

    dms, dma, dgs, dga = _ew(tag + "merge", merge_b, SEQ, 256,
                             [_ri(dmerged), _ri(r["ms"]), _ri(r["ma"]), _ri(proj, DM, CB_GS), _ri(proj, DM, CB_GA)],
                             [], [(DM, BF16)] * 4)
    dw_bs = _mm(tag + "dw_bs", r["ys"], dms, "tn", SW, DM, SEQ, 512, 1024, SEQ, BF16)
    dw_ba = _mm(tag + "dw_ba", r["ya"], dma, "tn", AW, DM, SEQ, 512, 1024, SEQ, BF16)
    dys = _mm(tag + "dys", dms, w["w_branch_s"], "nt", SEQ, SW, DM, 1024, 512, 1024, F32)
    dya = _mm(tag + "dya", dma, w["w_branch_a"], "nt", SEQ, AW, DM, 1024, 512, 1024, F32)

    dqkv, dza = _attention_bwd(tag + "attn", proj, dya, r["ypre"], r["lse"])

    def ys_b(d, y1, t, z, b_):
        _, vjp = jax.vjp(_ys_fn, y1, t, z, b_)
        dy1, dt, dz, _ = vjp(d)
        return dy1, dt, dz, _colsum(dt)

    dy1a, dt, dzs, dbglu = _ew(tag + "ys", ys_b, SEQ, 256,
                               [_ri(_to_chunked(dys)), _ri(r["y1"]), _ri(r["t"]), _ri(proj, SW, CB_ZS)],
                               [bglu], [(SW, F32), (SW, BF16), (SW, BF16)], [SW])
    dw_glu = _mm(tag + "dw_glu", r["y1"], dt, "tn", SW, SW, SEQ, 512, 512, SEQ, BF16)
    dy1b = _mm(tag + "dy1b", dt, w["w_glu"], "nt", SEQ, SW, SW, 1024, 512, 512, F32)

    def y1_b(da, db, y0p, u, d_):
        _, vjp = jax.vjp(_y1_fn, y0p, u, d_)
        dy0, du, dd = vjp(da + db)
        return dy0, du, dd

    dy0, du_skip, ddskip = _ew(tag + "y1", y1_b, SEQ, 256,
                               [_ri(dy1a), _ri(dy1b), _ri(r["y0p"]), _ri(proj, SW, CB_U)], [dskip],
                               [(SW, BF16), (SW, F32)], [SW])
    g_s = _mm(tag + "dstate", dy0, r["cmt"], "nn", SEQ, 2 * NS, SW, 1024, 1024, 512, F32,
              b_spec=pl.BlockSpec((512, 1024), lambda i, j, k: (j // 2, j % 2)))
    dcmt = _mm(tag + "dcmt", dy0, r["s"], "tn", SW, 2 * NS, SEQ, 512, 1024, SEQ, F32,
               b_spec=pl.BlockSpec((None, SEQ, 1024), lambda i, j, k: (j // 2, 0, j % 2)))
    lam_r, lam_i = r["ar"].reshape(1, NS), r["ai"].reshape(1, NS)
    adj = _scan(tag + "scan", g_s, lam_r, -lam_i, True)
    dlr, dli = _dlam(tag + "dlam", adj, r["s"])
    dwdt = _mm(tag + "dwdt", adj, proj, "tn", 2 * NS, SW, SEQ, 1024, 512, SEQ, F32,
               a_spec=pl.BlockSpec((None, SEQ, 1024), lambda i, j, k: (i // 2, 0, i % 2)),
               b_spec=pl.BlockSpec((SEQ, SW), lambda i, j, k: (0, CB_U)))
    du_s = _mm(tag + "du_s", adj, r["wdt"], "nn", SEQ, SW, 2 * NS, 1024, 512, 1024, F32, a_spec=_plane_spec(1024, 1024))
    (du,) = _ew(tag + "du", lambda a, c: (a + c,), SEQ, 256, [_ri(du_s), _ri(du_skip)], [], [(SW, BF16)])

    dbbr = _extract(tag + "dbbr", dwdt, NST, GCH, (0, 0))
    dbbi = _extract(tag + "dbbi", dwdt, NST, GCH, (1, 0))
    bre = sp["b_re"].reshape(NS, GCH)
    bim = sp["b_im"].reshape(NS, GCH)

    def bbar_b(cr, ci, br_, bi_, dr, di):
        _, vjp = jax.vjp(_bbar_fn, cr, ci, br_, bi_)
        return vjp((dr, di))

    dcr, dci, dbre, dbim = _ew(tag + "bbar", bbar_b, NS, NS,
                               [_ri(r["cr"].reshape(NS, 1)), _ri(r["ci"].reshape(NS, 1)), _ri(bre), _ri(bim),
                                _ri(dbbr), _ri(dbbi)], [], [(1, F32), (1, F32), (GCH, F32), (GCH, F32)])

    def lam_b(lre, lim, ldt, dar, dai, dcr_, dci_):
        _, vjp = jax.vjp(_lam_fn, lre, lim, ldt)
        return vjp((dar, dai, dcr_, dci_))

    dlre, dlim, dldt = _ew(tag + "lam", lam_b, NGRP, NGRP,
                           [_ri(sp["lambda_re"]), _ri(sp["lambda_im"]), _ri(sp["log_dt"].reshape(NGRP, 1)),
                            _ri(dlr.reshape(NGRP, NST)), _ri(dli.reshape(NGRP, NST)),
                            _ri(dcr.reshape(NGRP, NST)), _ri(dci.reshape(NGRP, NST))], [],
                           [(NST, F32), (NST, F32), (1, F32)])
    dc_re = _extract(tag + "dc_re", dcmt, GCH, NST, (0, 0)).reshape(NGRP, GCH, NST)
    dc_im = -_extract(tag + "dc_im", dcmt, GCH, NST, (0, 1)).reshape(NGRP, GCH, NST)

    dq, dk, dv = ([dqkv[3 * g + i] for g in range(3)] for i in range(3))
    dproj = jnp.concatenate([du, dzs, *dq, *dk, *dv, dza, dgs, dga], axis=1)
    dw_in = _mm(tag + "dw_in", r["hv"], dproj, "tn", DM, NCOL, SEQ, 1024, 512, SEQ, BF16,
                a_spec=pl.BlockSpec((None, SEQ, 1024), lambda i, j, k: (_row_order(j), 0, 0)),
                o_spec=pl.BlockSpec((None, 1024, 512), lambda i, j, k: (j // 2, 0, j % 2)), out_shape=(NDEV, DM, DM))

    dh_time = _mm(tag + "dh_time", dproj, w["w_in"], "nt", SEQ, DM, NCOL - DM, SEQ, 1024, 1024, F32,
                  a_spec=pl.BlockSpec((SEQ, 1024), lambda i, j, k: (0, 1 + k)),
                  b_spec=pl.BlockSpec((None, 1024, 1024), lambda i, j, k: (1 + k, 0, 0)))
    dh_chunked = _mm(tag + "dh_chunked", dproj, w["w_in"], "nt", SEQ, DM, DM, SEQ, 1024, 1024, F32,
                     a_spec=pl.BlockSpec((SEQ, 1024), lambda i, j, k: (0, 0)),
                     b_spec=pl.BlockSpec((None, 1024, 1024), lambda i, j, k: (0, 0, 0)))
    dh = [dh_time, _from_chunked(dh_chunked)]

    def pre_b(d, dh0, dh1, x_, g):
        _, vjp = jax.vjp(_rms, x_, g)
        dx_, dg = vjp(dh0 + dh1)
        return d + dx_, dg

    dx, dg1 = _ew(tag + "pre", pre_b, SEQ, 256, [_ri(dxn)] + [_ri(t_) for t_ in dh] + [_ri(r["x"])], [g1],
                  [(DM, F32)], [DM])

    big = dict(w_in=dw_in, w_glu=dw_glu.reshape(NDEV, SW // NDEV, SW),
               w_branch_s=dw_bs.reshape(SW, NDEV, DM // NDEV).transpose(1, 0, 2),
               w_branch_a=dw_ba.reshape(AW, NDEV, DM // NDEV).transpose(1, 0, 2),
               w_out=dw_out.reshape(NDEV, DM // NDEV, DM))
    small = dict(pre_norm_g=dg1.reshape(DM), lambda_re=dlre, lambda_im=dlim, log_dt=dldt.reshape(NGRP),
                 b_re=dbre.reshape(NGRP, NST, GCH), b_im=dbim.reshape(NGRP, NST, GCH), c_re=dc_re, c_im=dc_im,
                 d_skip=ddskip.reshape(SW), b_glu=dbglu.reshape(SW), post_norm_g=dg2.reshape(DM))
    return dx, big, small


_HBM = pl.BlockSpec(memory_space=pltpu.HBM)
_SEM = pl.BlockSpec(memory_space=pltpu.SEMAPHORE)
_EFFECT = pltpu.SideEffectType.DATAFLOW_SIDE_EFFECTING


def _remote_copies(srcs, dsts, send_sems, recv_sems, gather):
    x, y, c = lax.axis_index("x"), lax.axis_index("y"), lax.axis_index("c")
    me = 4 * x + 2 * y + c
    copies = []
    for i in range(len(srcs)):
        for k in range(1, NDEV):
            peer = (x ^ (k >> 2), y ^ ((k >> 1) & 1), c ^ (k & 1))
            src = srcs[i] if gather[i] else srcs[i].at[me ^ k]
            copies.append(pltpu.make_async_remote_copy(
                src_ref=src, dst_ref=dsts[i].at[me], send_sem=send_sems[i], recv_sem=recv_sems[i],
                device_id=peer, device_id_type=pl.DeviceIdType.MESH))
    return copies


def _all_seven(dst, send_sem, recv_sem):
    seven = dst.at[pl.ds(0, NDEV - 1)]
    me = (lax.axis_index("x"), lax.axis_index("y"), lax.axis_index("c"))
    return pltpu.make_async_remote_copy(src_ref=seven, dst_ref=seven, send_sem=send_sem, recv_sem=recv_sem,
                                        device_id=me, device_id_type=pl.DeviceIdType.MESH)


def _own_slabs(name, arrs, gather, after):
    n = len(arrs)
    me = (4 * lax.axis_index("x") + 2 * lax.axis_index("y") + lax.axis_index("c")).astype(jnp.int32).reshape(1)

    def body(me_ref, *refs):
        for src, dst in zip(refs[:n], refs[n + 1:]):
            dst[...] = src[...]

    def zeros(k):
        return (0,) * k

    in_specs, out_specs, out_shape = [], [], []
    for a, g in zip(arrs, gather):
        slab = a.shape if g else a.shape[1:]
        nd = len(slab)
        if g:
            in_specs.append(pl.BlockSpec(slab, functools.partial(lambda i, me_ref, nd: zeros(nd), nd=nd)))
        else:
            in_specs.append(pl.BlockSpec((None,) + slab, functools.partial(lambda i, me_ref, nd: (me_ref[0],) + zeros(nd), nd=nd)))
        out_specs.append(pl.BlockSpec((None,) + slab, functools.partial(lambda i, me_ref, nd: (me_ref[0],) + zeros(nd), nd=nd)))
        out_shape.append(jax.ShapeDtypeStruct((NDEV,) + slab, a.dtype))
    in_specs.append(pl.BlockSpec(memory_space=pl.ANY))
    return pl.pallas_call(
        body, name=name, out_shape=out_shape,
        grid_spec=pltpu.PrefetchScalarGridSpec(num_scalar_prefetch=1, grid=(1,), in_specs=in_specs, out_specs=out_specs),
        compiler_params=_params("arbitrary"),
    )(me, *arrs, after)


def _exchange_start(name, arrs, gather, after):
    n = len(arrs)
    lands = _own_slabs(name + "_own", arrs, gather, after)

    def body(*refs):
        srcs, dsts = refs[:n], refs[n:2 * n]
        send_sems, recv_sems = refs[2 * n:3 * n], refs[3 * n:4 * n]
        token = refs[-1]
        for cp in _remote_copies(srcs, dsts, send_sems, recv_sems, gather):
            cp.start()
        token[...] = jnp.zeros(token.shape, token.dtype)

    thru = [pltpu.HBM(a.shape, a.dtype) for a in list(arrs) + list(lands)]
    outs = pl.pallas_call(
        body, name=name,
        out_shape=(*[pltpu.SemaphoreType.DMA(())] * (2 * n), *thru, jax.ShapeDtypeStruct((8, 128), F32)),
        in_specs=[_HBM] * (2 * n),
        out_specs=(*[_SEM] * (2 * n), *[_HBM] * (2 * n), pl.BlockSpec(memory_space=pltpu.VMEM)),
        input_output_aliases={i: 2 * n + i for i in range(2 * n)},
        compiler_params=pltpu.CompilerParams(has_side_effects=_EFFECT),
    )(*[pltpu.with_memory_space_constraint(a, pltpu.HBM) for a in list(arrs) + list(lands)])
    return dict(send=outs[:n], recv=outs[n:2 * n], srcs=outs[2 * n:3 * n], lands=outs[3 * n:4 * n], token=outs[-1],
                gather=gather)


def _exchange_wait(name, started, after):
    n = len(started["srcs"])

    def body(*refs):
        dsts = refs[n:2 * n]
        send_sems, recv_sems = refs[2 * n:3 * n], refs[3 * n:4 * n]
        for i in range(n):
            cp = _all_seven(dsts[i], send_sems[i], recv_sems[i])
            cp.wait_send()
            cp.wait_recv()

    bufs = list(started["srcs"]) + list(started["lands"])
    outs = pl.pallas_call(
        body, name=name, out_shape=tuple(pltpu.HBM(a.shape, a.dtype) for a in bufs),
        in_specs=[_HBM] * (2 * n) + [_SEM] * (2 * n) + [pl.BlockSpec(memory_space=pl.ANY)], out_specs=(_HBM,) * (2 * n),
        input_output_aliases={i: i for i in range(2 * n)},
        compiler_params=pltpu.CompilerParams(has_side_effects=_EFFECT),
    )(*bufs, *started["send"], *started["recv"], after)
    return outs[n:]


def _sum_in_order(parts):
    g = parts[0].astype(F32)
    for p in parts[1:]:
        g = g + p.astype(F32)
    return g


def _adamw(name, parts, nparts, w, m, v, br):
    rows, cols = w.shape

    def fn(*a):
        g = _sum_in_order(a[:nparts])
        w_, m_, v_ = a[nparts:]
        m2 = B1 * m_ + (1.0 - B1) * g
        v2 = B2 * v_ + (1.0 - B2) * (g * g)
        m_hat = m2 / (1.0 - B1 ** STEP)
        v_hat = v2 / (1.0 - B2 ** STEP)
        delta = -LR * (m_hat / (jnp.sqrt(v_hat) + ADAM_EPS) + WD * w_)
        return g, delta, m2, v2

    ins = [_ri(parts, cols, 0, d * (rows // br)) for d in range(nparts)] + [_ri(w), _ri(m), _ri(v)]
    return _ew(name, fn, rows, br, ins, [], [(cols, F32)] * 4)


SMALL = ("pre_norm_g", "lambda_re", "lambda_im", "log_dt", "b_re", "b_im", "c_re", "c_im", "d_skip", "b_glu",
         "post_norm_g")
BIG = ("w_in", "w_glu", "w_branch_s", "w_branch_a", "w_out")
WEIGHTS = ("pre_norm_g", "w_in", "lambda_re", "lambda_im", "log_dt", "b_re", "b_im", "c_re", "c_im", "d_skip",
           "w_glu", "b_glu", "w_branch_s", "w_branch_a", "w_out", "post_norm_g")
PACK_COLS = 1024
PACK_BR = 136


def _pack_layer(d):
    pieces = [d[k].astype(F32).reshape(-1) for k in SMALL]
    used = sum(p.shape[0] for p in pieces)
    assert used <= PACK_BR * PACK_COLS
    return jnp.concatenate(pieces + [jnp.zeros((PACK_BR * PACK_COLS - used,), F32)]).reshape(PACK_BR, PACK_COLS)


def _unpack(p, like):
    flat = p.reshape(DEPTH, PACK_BR * PACK_COLS)
    out, off = {}, 0
    for k in SMALL:
        n = like[k].size // DEPTH
        out[k] = flat[:, off:off + n].reshape(like[k].shape)
        off += n
    return out


def _local_step(x, target, small, weights_of, grads_done):
    res, ws = [], []
    for l in range(DEPTH):
        w_l, tok = weights_of(l, x)
        sp = {k: small[k][l] for k in SMALL}
        if tok is not None:
            sp["pre_norm_g"] = sp["pre_norm_g"] + tok[0, 0]
        x, r = _layer_fwd(l, x, w_l, sp)
        res.append(r)
        ws.append(w_l)

    def loss_fn(y, t):
        e = y - t
        return e * (1.0 / DM), jnp.sum(_colsum(0.5 * e * e * (1.0 / DM)), axis=1, keepdims=True)

    dx, loss = _ew("loss", loss_fn, SEQ, 256, [_ri(x), _ri(target)], [], [(DM, F32)], [1])
    tok = None
    for l in reversed(range(DEPTH)):
        sp = {k: small[k][l] for k in SMALL}
        if tok is not None:
            sp["post_norm_g"] = sp["post_norm_g"] + tok[0, 0]
        dx, big, sm = _layer_bwd(l, dx, res[l], ws[l], sp)
        tok = grads_done(l, big, sm)
    return loss.reshape(()), dx


def _full_weights(gathered):
    g = gathered
    return dict(
        w_in=g["w_in"],
        w_glu=g["w_glu"].reshape(SW, SW),
        w_branch_s=g["w_branch_s"].transpose(1, 0, 2).reshape(SW, DM),
        w_branch_a=g["w_branch_a"].transpose(1, 0, 2).reshape(AW, DM),
        w_out=g["w_out"].reshape(DM, DM),
    )


def kernel(x, pre_norm_g, w_in, lambda_re, lambda_im, log_dt, b_re, b_im, c_re, c_im, d_skip, w_glu, b_glu, w_branch_s, w_branch_a, w_out, post_norm_g, loss_target, m_pre_norm_g, m_w_in, m_lambda_re, m_lambda_im, m_log_dt, m_b_re, m_b_im, m_c_re, m_c_im, m_d_skip, m_w_glu, m_b_glu, m_w_branch_s, m_w_branch_a, m_w_out, m_post_norm_g, v_pre_norm_g, v_w_in, v_lambda_re, v_lambda_im, v_log_dt, v_b_re, v_b_im, v_c_re, v_c_im, v_d_skip, v_w_glu, v_b_glu, v_w_branch_s, v_w_branch_a, v_w_out, v_post_norm_g):
    wts = dict(pre_norm_g=pre_norm_g, w_in=w_in, lambda_re=lambda_re, lambda_im=lambda_im, log_dt=log_dt, b_re=b_re,
               b_im=b_im, c_re=c_re, c_im=c_im, d_skip=d_skip, w_glu=w_glu, b_glu=b_glu, w_branch_s=w_branch_s,
               w_branch_a=w_branch_a, w_out=w_out, post_norm_g=post_norm_g)
    mom = dict(pre_norm_g=m_pre_norm_g, w_in=m_w_in, lambda_re=m_lambda_re, lambda_im=m_lambda_im, log_dt=m_log_dt,
               b_re=m_b_re, b_im=m_b_im, c_re=m_c_re, c_im=m_c_im, d_skip=m_d_skip, w_glu=m_w_glu, b_glu=m_b_glu,
               w_branch_s=m_w_branch_s, w_branch_a=m_w_branch_a, w_out=m_w_out, post_norm_g=m_post_norm_g)
    var = dict(pre_norm_g=v_pre_norm_g, w_in=v_w_in, lambda_re=v_lambda_re, lambda_im=v_lambda_im, log_dt=v_log_dt,
               b_re=v_b_re, b_im=v_b_im, c_re=v_c_re, c_im=v_c_im, d_skip=v_d_skip, w_glu=v_w_glu, b_glu=v_b_glu,
               w_branch_s=v_w_branch_s, w_branch_a=v_w_branch_a, w_out=v_w_out, post_norm_g=v_post_norm_g)

    def gather_start(l, after):
        return _exchange_start(f"gather_start{l}", [wts[k][l].astype(BF16) for k in BIG], [True] * len(BIG), after)

    gathering = {0: gather_start(0, x)}
    sending = {}

    def weights_of(l, x_l):
        gathered = _exchange_wait(f"gather_wait{l}", gathering[l], x_l)
        tok = None
        if l + 1 < DEPTH:
            gathering[l + 1] = gather_start(l + 1, gathered[0])
            tok = gathering[l + 1]["token"]
        return _full_weights(dict(zip(BIG, gathered))), tok

    def grads_done(l, big, sm):
        sending[l] = _exchange_start(f"grads_start{l}", [big[k] for k in BIG] + [_pack_layer(sm)],
                                     [False] * len(BIG) + [True], big["w_in"])
        return sending[l]["token"]

    loss, dx = _local_step(x[0], loss_target[0], wts, weights_of, grads_done)
    loss = lax.psum(loss, ("x", "y", "c"))
    recv_l = [_exchange_wait(f"grads_wait{l}", sending[l], dx) for l in range(DEPTH)]
    recv = [jnp.stack([r[i] for r in recv_l], axis=1) for i in range(len(BIG))]
    recv.append(jnp.concatenate([r[len(BIG)] for r in recv_l], axis=1))

    grads, delta, new_m, new_v = {}, {}, {}, {}

    def update(k, parts, nparts):
        shape = wts[k].shape
        cols = shape[-1]
        rows = wts[k].size // cols
        br = min(rows, 1024 if cols <= 128 else 256)
        outs = _adamw("adamw_" + k, parts.reshape(nparts * rows, cols), nparts, wts[k].reshape(rows, cols),
                      mom[k].reshape(rows, cols), var[k].reshape(rows, cols), br)
        grads[k], delta[k], new_m[k], new_v[k] = (o.reshape(shape) for o in outs)

    for k, parts in zip(BIG, recv[:len(BIG)]):
        update(k, parts, NDEV)
    rows = DEPTH * PACK_BR
    (g_small,) = _ew("grads_small", lambda *p: (_sum_in_order(p),), rows, PACK_BR,
                     [_ri(recv[-1].reshape(NDEV * rows, PACK_COLS), PACK_COLS, 0, d * DEPTH) for d in range(NDEV)], [],
                     [(PACK_COLS, F32)])
    for k, g in _unpack(g_small, wts).items():
        update(k, g, 1)

    return (loss, dx[None], *[grads[k] for k in WEIGHTS], *[delta[k] for k in WEIGHTS],
            *[new_m[k] for k in WEIGHTS], *[new_v[k] for k in WEIGHTS])
```

```python
import functools
import math

import jax
import jax.numpy as jnp
from jax import lax
from jax.experimental import pallas as pl
from jax.experimental.pallas import tpu as pltpu

F32 = jnp.float32
BF16 = jnp.bfloat16

NDEV = 8
DEPTH = 4
SEQ = 2048
DM = 1024
NCOL = 8192
SW = 512
NGRP = 32
GCH = 16
NST = 64
NS = NGRP * NST
HD = 128
AW = 512
DILATIONS = (1, 4, 16)
ABLK = 128
RMS_EPS = 1e-6
LR, B1, B2, ADAM_EPS, WD, STEP = 0.001, 0.9, 0.999, 1e-08, 0.01, 10

CB_U, CB_ZS, CB_Q, CB_K, CB_V, CB_ZA = 0, 1, 2, 5, 8, 11
CB_GS, CB_GA = 6, 7

VMEM_LIMIT = 56 * 2 ** 20


def _row_order(j):
    return jnp.where(j < CB_Q, 1, 0)


def _params(*sem):
    return pltpu.CompilerParams(dimension_semantics=sem, vmem_limit_bytes=VMEM_LIMIT)


def _ew(name, fn, rows, br, row_ins, bc_ins, row_outs, red_outs=()):
    n_in = len(row_ins) + len(bc_ins)
    n_ro = len(row_outs)
    steps = rows // br
    assert steps * br == rows

    def body(*refs):
        vals = fn(*[r[...] for r in refs[:n_in]])
        outs = refs[n_in:]
        for r, v in zip(outs[:n_ro], vals[:n_ro]):
            r[...] = v.astype(r.dtype)
        if red_outs:
            @pl.when(pl.program_id(0) == 0)
            def _():
                for r in outs[n_ro:]:
                    r[...] = jnp.zeros(r.shape, r.dtype)
            for r, v in zip(outs[n_ro:], vals[n_ro:]):
                r[...] += v

    in_specs = []
    for (_, w, cb, rb) in row_ins:
        in_specs.append(pl.BlockSpec((br, w), functools.partial(lambda i, cb, rb: (rb + i, cb), cb=cb, rb=rb)))
    for a in bc_ins:
        in_specs.append(pl.BlockSpec(a.shape, functools.partial(lambda i, nd: (0,) * nd, nd=a.ndim)))
    out_specs = [pl.BlockSpec((br, w), lambda i: (i, 0)) for (w, _) in row_outs]
    out_specs += [pl.BlockSpec((1, w), lambda i: (0, 0)) for w in red_outs]
    out_shape = [jax.ShapeDtypeStruct((rows, w), dt) for (w, dt) in row_outs]
    out_shape += [jax.ShapeDtypeStruct((1, w), F32) for w in red_outs]
    return pl.pallas_call(
        body, name=name, grid=(steps,), in_specs=in_specs, out_specs=out_specs, out_shape=out_shape,
        compiler_params=_params("arbitrary"),
    )(*[a for (a, _, _, _) in row_ins], *bc_ins)


def _ri(a, w=None, cb=0, rb=0):
    return (a, a.shape[1] if w is None else w, cb, rb)


_DIMS = {"nn": ((1,), (0,)), "nt": ((1,), (1,)), "tn": ((0,), (0,))}


def _mm(name, a, b, mode, M, N, K, bm, bn, bk, out_dtype, a_spec=None, b_spec=None, o_spec=None, out_shape=None):
    nk = K // bk
    assert M % bm == 0 and N % bn == 0 and nk * bk == K

    own_acc = nk > 1 and out_dtype != F32

    def body(a_ref, b_ref, o_ref, *scratch):
        part = lax.dot_general(a_ref[...].astype(BF16), b_ref[...].astype(BF16), (_DIMS[mode], ((), ())),
                               preferred_element_type=F32)
        if nk == 1:
            o_ref[...] = part.astype(o_ref.dtype)
            return
        k = pl.program_id(2)
        acc_ref = scratch[0] if own_acc else o_ref

        @pl.when(k == 0)
        def _():
            acc_ref[...] = part

        @pl.when(k > 0)
        def _():
            acc_ref[...] += part

        if own_acc:
            @pl.when(k == nk - 1)
            def _():
                o_ref[...] = acc_ref[...].astype(o_ref.dtype)

    if a_spec is None:
        a_spec = (pl.BlockSpec((bk, bm), lambda i, j, k: (k, i)) if mode == "tn"
                  else pl.BlockSpec((bm, bk), lambda i, j, k: (i, k)))
    if b_spec is None:
        b_spec = (pl.BlockSpec((bn, bk), lambda i, j, k: (j, k)) if mode == "nt"
                  else pl.BlockSpec((bk, bn), lambda i, j, k: (k, j)))
    if o_spec is None:
        o_spec = pl.BlockSpec((bm, bn), lambda i, j, k: (i, j))
    if out_shape is None:
        out_shape = (M, N)
    return pl.pallas_call(
        body, name=name, grid=(M // bm, N // bn, nk), in_specs=[a_spec, b_spec], out_specs=o_spec,
        out_shape=jax.ShapeDtypeStruct(out_shape, out_dtype),
        scratch_shapes=[pltpu.VMEM((bm, bn), F32)] if own_acc else [],
        compiler_params=_params("parallel", "parallel", "arbitrary"),
    )(a, b)


SCAN_LANES = 512
SCAN_CHUNKS = 8


def _to_chunked(a):
    return a.reshape(SCAN_CHUNKS, SEQ // SCAN_CHUNKS, -1).transpose(1, 0, 2).reshape(SEQ, -1)


def _from_chunked(a):
    return a.reshape(SEQ // SCAN_CHUNKS, SCAN_CHUNKS, -1).transpose(1, 0, 2).reshape(SEQ, -1)


def _scan_block(dr_ref, di_ref, sr_ref, si_ref, lam_r, lam_i, reverse):
    T = SEQ // SCAN_CHUNKS
    bl = lam_r.shape[1]
    assert T == 2 ** 8
    ar = jnp.broadcast_to(lam_r, (SCAN_CHUNKS, bl))
    ai = jnp.broadcast_to(lam_i, (SCAN_CHUNKS, bl))
    zero = jnp.zeros((SCAN_CHUNKS, bl), F32)

    def tile(j):
        return pl.ds(pl.multiple_of(j * SCAN_CHUNKS, SCAN_CHUNKS), SCAN_CHUNKS)

    def step(jj, carry):
        sr, si = carry
        j = T - 1 - jj if reverse else jj
        nr = ar * sr - ai * si + dr_ref[tile(j), :]
        ni = ar * si + ai * sr + di_ref[tile(j), :]
        sr_ref[tile(j), :] = nr
        si_ref[tile(j), :] = ni
        return nr, ni

    er, ei = lax.fori_loop(0, T, step, (zero, zero), unroll=4)

    pr, pi = ar[0:1], ai[0:1]
    for _ in range(8):
        pr, pi = pr * pr - pi * pi, 2.0 * pr * pi
    rows = lax.broadcasted_iota(jnp.int32, (SCAN_CHUNKS, bl), 0)
    cr, ci = zero, zero
    xr = jnp.zeros((1, bl), F32)
    xi = jnp.zeros((1, bl), F32)
    order = range(SCAN_CHUNKS - 2, -1, -1) if reverse else range(1, SCAN_CHUNKS)
    for c in order:
        src = c + 1 if reverse else c - 1
        nxr = pr * xr - pi * xi + er[src:src + 1]
        nxi = pr * xi + pi * xr + ei[src:src + 1]
        xr, xi = nxr, nxi
        cr = jnp.where(rows == c, xr, cr)
        ci = jnp.where(rows == c, xi, ci)

    def fix(jj, pw):
        pwr, pwi = pw
        j = T - 1 - jj if reverse else jj
        sr_ref[tile(j), :] = sr_ref[tile(j), :] + (pwr * cr - pwi * ci)
        si_ref[tile(j), :] = si_ref[tile(j), :] + (pwr * ci + pwi * cr)
        return pwr * ar - pwi * ai, pwr * ai + pwi * ar

    lax.fori_loop(0, T, fix, (ar, ai), unroll=4)


def _s5_forward(name, proj, wdt, cmt, lam_r, lam_i):
    bl = SCAN_LANES
    nblk = NS // bl

    def nt(a, b_):
        return lax.dot_general(a, b_, (((1,), (1,)), ((), ())), preferred_element_type=F32)

    def body(u_ref, wr_ref, wi_ref, cr_ref, ci_ref, ar_ref, ai_ref, s_ref, y_ref):
        sr_ref, si_ref = s_ref.at[0], s_ref.at[1]
        u = u_ref[...].astype(BF16)
        sr_ref[...] = nt(u, wr_ref[...])
        si_ref[...] = nt(u, wi_ref[...])
        _scan_block(sr_ref, si_ref, sr_ref, si_ref, ar_ref[...], ai_ref[...], False)
        part = nt(sr_ref[...].astype(BF16), cr_ref[...]) + nt(si_ref[...].astype(BF16), ci_ref[...])

        @pl.when(pl.program_id(0) == 0)
        def _():
            y_ref[...] = part

        @pl.when(pl.program_id(0) > 0)
        def _():
            y_ref[...] += part

    return pl.pallas_call(
        body, name=name, grid=(nblk,),
        in_specs=[pl.BlockSpec((SEQ, SW), lambda i: (0, CB_U)),
                  pl.BlockSpec((bl, SW), lambda i: (i, 0)), pl.BlockSpec((bl, SW), lambda i: (nblk + i, 0)),
                  pl.BlockSpec((SW, bl), lambda i: (0, i)), pl.BlockSpec((SW, bl), lambda i: (1, i)),
                  pl.BlockSpec((1, bl), lambda i: (0, i)), pl.BlockSpec((1, bl), lambda i: (0, i))],
        out_specs=[pl.BlockSpec((2, SEQ, bl), lambda i: (0, 0, i)), pl.BlockSpec((SEQ, SW), lambda i: (0, 0))],
        out_shape=[jax.ShapeDtypeStruct((2, SEQ, NS), F32), jax.ShapeDtypeStruct((SEQ, SW), F32)],
        compiler_params=_params("arbitrary"),
    )(proj, wdt, wdt, cmt, cmt, lam_r, lam_i)


def _scan(name, d, lam_r, lam_i, reverse):
    bl = SCAN_LANES
    nblk = NS // bl

    def body(dr_ref, di_ref, ar_ref, ai_ref, s_ref):
        _scan_block(dr_ref, di_ref, s_ref.at[0], s_ref.at[1], ar_ref[...], ai_ref[...], reverse)

    return pl.pallas_call(
        body, name=name, grid=(nblk,),
        in_specs=[pl.BlockSpec((SEQ, bl), lambda i: (0, i)),
                  pl.BlockSpec((SEQ, bl), lambda i: (0, nblk + i)),
                  pl.BlockSpec((1, bl), lambda i: (0, i)),
                  pl.BlockSpec((1, bl), lambda i: (0, i))],
        out_specs=pl.BlockSpec((2, SEQ, bl), lambda i: (0, 0, i)),
        out_shape=jax.ShapeDtypeStruct((2, SEQ, NS), F32),
        compiler_params=_params("arbitrary"),
    )(d, d, lam_r, lam_i)


def _plane_spec(bm, bk):
    per = NS // bk
    return pl.BlockSpec((None, bm, bk), lambda i, j, k: (k // per, i, k % per))


def _dlam(name, a, s):
    bl = 256

    def prev(s_ref):
        last = pltpu.roll(s_ref[SEQ - SCAN_CHUNKS:SEQ, :], 1, 0)
        first = jnp.where(lax.broadcasted_iota(jnp.int32, (SCAN_CHUNKS, bl), 0) > 0, last, 0.0)
        return jnp.concatenate([first, s_ref[0:SEQ - SCAN_CHUNKS, :]], axis=0)

    def body(ar_ref, ai_ref, sr_ref, si_ref, or_ref, oi_ref):
        spr, spi = prev(sr_ref), prev(si_ref)
        a_r_, a_i_ = ar_ref[...], ai_ref[...]
        or_ref[...] = jnp.sum(a_r_ * spr + a_i_ * spi, axis=0, keepdims=True)
        oi_ref[...] = jnp.sum(a_i_ * spr - a_r_ * spi, axis=0, keepdims=True)

    re_spec = pl.BlockSpec((None, SEQ, bl), lambda i: (0, 0, i))
    im_spec = pl.BlockSpec((None, SEQ, bl), lambda i: (1, 0, i))
    ospec = pl.BlockSpec((1, bl), lambda i: (0, i))
    return pl.pallas_call(
        body, name=name, grid=(NS // bl,), in_specs=[re_spec, im_spec] * 2, out_specs=[ospec, ospec],
        out_shape=[jax.ShapeDtypeStruct((1, NS), F32)] * 2,
        compiler_params=_params("arbitrary"),
    )(a, a, s, s)


def _scores(qb, kb, prev):
    s = lax.dot_general(qb, kb, (((1,), (1,)), ((), ())), preferred_element_type=F32) * (HD ** -0.5)
    row = lax.broadcasted_iota(jnp.int32, (ABLK, ABLK), 0)
    col = lax.broadcasted_iota(jnp.int32, (ABLK, ABLK), 1)
    return jnp.where((col >= row) if prev else (col <= row), s, -1e30)


def _block_rows(dil, r, b):
    if dil == 1:
        return pl.ds(pl.multiple_of(b * ABLK, ABLK), ABLK)
    return pl.ds(r + dil * ABLK * b, ABLK, stride=dil)


def _group_blocks(dil):
    nb = SEQ // dil // ABLK
    shift = nb.bit_length() - 1
    return nb, (lambda idx: (idx >> shift, idx & (nb - 1)))


def _qkv_specs(j_of):
    return [pl.BlockSpec((SEQ, HD), functools.partial(lambda j, c: (0, c + j_of(j)), c=(cb + g) * 4))
            for g in range(3) for cb in (CB_Q, CB_K, CB_V)]


def _attention_fwd(name, proj):
    half = HD // 2

    def body(*refs):
        qkv, z_ref = refs[:9], refs[9]
        y_ref, ya_ref, l_ref = refs[10:13]
        accs, stats = refs[13:16], refs[16:19]
        low = lax.broadcasted_iota(jnp.int32, (ABLK, HD), 1) < half
        for g, dil in enumerate(DILATIONS):
            q_ref, k_ref, v_ref = qkv[3 * g:3 * g + 3]
            nb, where = _group_blocks(dil)

            def step(idx, c, g=g, dil=dil, nb=nb, where=where, q_ref=q_ref, k_ref=k_ref, v_ref=v_ref):
                r, b = where(idx)
                rows = _block_rows(dil, r, b)
                qb = q_ref[rows, :].astype(BF16)
                s_c = _scores(qb, k_ref[rows, :].astype(BF16), False)
                m = jnp.max(s_c, axis=-1, keepdims=True)
                if nb > 1:
                    prev = _block_rows(dil, r, jnp.maximum(b - 1, 0))
                    s_p = jnp.where(b > 0, _scores(qb, k_ref[prev, :].astype(BF16), True), -1e30)
                    m = jnp.maximum(m, jnp.max(s_p, axis=-1, keepdims=True))
                p_c = jnp.exp(s_c - m)
                den = jnp.sum(p_c, axis=-1, keepdims=True)
                acc = jnp.dot(p_c.astype(BF16), v_ref[rows, :].astype(BF16), preferred_element_type=F32)
                if nb > 1:
                    p_p = jnp.exp(s_p - m)
                    den = den + jnp.sum(p_p, axis=-1, keepdims=True)
                    acc = acc + jnp.dot(p_p.astype(BF16), v_ref[prev, :].astype(BF16), preferred_element_type=F32)
                accs[g][rows, :] = acc
                stats[g][rows, :] = jnp.where(low, m, den)
                return c

            lax.fori_loop(0, SEQ // ABLK, step, 0, unroll=2)
        ms = [s[...][:, 0:1] for s in stats]
        top = jnp.maximum(jnp.maximum(ms[0], ms[1]), ms[2])
        den = jnp.zeros((SEQ, 1), F32)
        y = jnp.zeros((SEQ, HD), F32)
        for g in range(3):
            wgt = jnp.exp(ms[g] - top)
            den = den + wgt * stats[g][...][:, half:half + 1]
            y = y + wgt * accs[g][...]
        y = y / den
        y_ref[...] = y
        ya_ref[...] = (y * _silu(z_ref[...])).astype(ya_ref.dtype)
        l_ref[...] = jnp.broadcast_to(top + jnp.log(den), (SEQ, HD))

    ospec = pl.BlockSpec((SEQ, HD), lambda j: (0, j))
    return pl.pallas_call(
        body, name=name, grid=(AW // HD,),
        in_specs=_qkv_specs(lambda j: j) + [pl.BlockSpec((SEQ, HD), lambda j: (0, CB_ZA * 4 + j))],
        out_specs=[ospec, ospec, ospec],
        out_shape=[jax.ShapeDtypeStruct((SEQ, AW), F32), jax.ShapeDtypeStruct((SEQ, AW), BF16),
                   jax.ShapeDtypeStruct((SEQ, AW), F32)],
        scratch_shapes=[pltpu.VMEM((SEQ, HD), F32)] * 6,
        compiler_params=_params("parallel"),
    )(*([proj] * 10))


def _attention_bwd(name, proj, dya, y, lse):
    def tn(a, b_):
        return lax.dot_general(a, b_, (((0,), (0,)), ((), ())), preferred_element_type=F32)

    def nt(a, b_):
        return lax.dot_general(a, b_, (((1,), (1,)), ((), ())), preferred_element_type=F32)

    def body(*refs):
        qkv, z_ref, dya_ref, y_ref, l_ref = refs[:9], refs[9], refs[10], refs[11], refs[12]
        outs, dza_ref = refs[13:22], refs[22]
        dy_s, dsum_s, dq_s, dk_own, dv_own, dk_prev, dv_prev = refs[23:]
        _, vjp = jax.vjp(lambda y_, z_: y_ * _silu(z_), y_ref[...], z_ref[...])
        dy, dz = vjp(dya_ref[...])
        dza_ref[...] = dz.astype(dza_ref.dtype)
        dy_s[...] = dy
        dsum_s[...] = jnp.broadcast_to(jnp.sum(dy * y_ref[...], axis=-1, keepdims=True), (SEQ, HD))
        for g, dil in enumerate(DILATIONS):
            q_ref, k_ref, v_ref = qkv[3 * g:3 * g + 3]
            nb, where = _group_blocks(dil)
            if nb > 1:
                dk_prev[...] = jnp.zeros(dk_prev.shape, F32)
                dv_prev[...] = jnp.zeros(dv_prev.shape, F32)

            def step(idx, c, dil=dil, nb=nb, where=where, q_ref=q_ref, k_ref=k_ref, v_ref=v_ref):
                r, b = where(idx)
                rows = _block_rows(dil, r, b)
                qb = q_ref[rows, :].astype(BF16)
                dyb = dy_s[rows, :].astype(BF16)
                lb = l_ref[rows, :][:, 0:1]
                db = dsum_s[rows, :][:, 0:1]

                def side(krows, prev, dk_ref, dv_ref):
                    kb = k_ref[krows, :].astype(BF16)
                    vb = v_ref[krows, :].astype(BF16)
                    s = _scores(qb, kb, prev)
                    if prev:
                        s = jnp.where(b > 0, s, -1e30)
                    p = jnp.exp(s - lb)
                    ds = (p * (nt(dyb, vb) - db) * (HD ** -0.5)).astype(BF16)
                    dk_ref[krows, :] = tn(ds, qb)
                    dv_ref[krows, :] = tn(p.astype(BF16), dyb)
                    return jnp.dot(ds, kb, preferred_element_type=F32)

                dq = side(rows, False, dk_own, dv_own)
                if nb > 1:
                    dq = dq + side(_block_rows(dil, r, jnp.maximum(b - 1, 0)), True, dk_prev, dv_prev)
                dq_s[rows, :] = dq
                return c

            lax.fori_loop(0, SEQ // ABLK, step, 0, unroll=2)
            dq_ref, dk_ref, dv_ref = outs[3 * g:3 * g + 3]
            dq_ref[...] = dq_s[...].astype(dq_ref.dtype)
            if nb > 1:
                dk_ref[...] = (dk_own[...] + dk_prev[...]).astype(dk_ref.dtype)
                dv_ref[...] = (dv_own[...] + dv_prev[...]).astype(dv_ref.dtype)
            else:
                dk_ref[...] = dk_own[...].astype(dk_ref.dtype)
                dv_ref[...] = dv_own[...].astype(dv_ref.dtype)

    ospec = pl.BlockSpec((SEQ, HD), lambda j: (0, j))
    outs = pl.pallas_call(
        body, name=name, grid=(AW // HD,),
        in_specs=_qkv_specs(lambda j: j) + [pl.BlockSpec((SEQ, HD), lambda j: (0, CB_ZA * 4 + j))] + [ospec] * 3,
        out_specs=[ospec] * 10, out_shape=[jax.ShapeDtypeStruct((SEQ, AW), BF16)] * 10,
        scratch_shapes=[pltpu.VMEM((SEQ, HD), F32)] * 7,
        compiler_params=_params("parallel"),
    )(*([proj] * 10), dya, y, lse)
    return outs[:9], outs[9]


def _rms(x, g):
    return x * lax.rsqrt(jnp.mean(x * x, axis=-1, keepdims=True) + RMS_EPS) * g


def _sig(x):
    return 1.0 / (1.0 + jnp.exp(-x))


def _silu(x):
    return x * _sig(x)


def _gelu(x):
    return 0.5 * x * (1.0 + jnp.tanh(math.sqrt(2.0 / math.pi) * (x + 0.044715 * (x * x * x))))


def _y1_fn(y0p, u, dskip):
    return _gelu(y0p + dskip * u)


def _ys_fn(y1, t, z, bglu):
    return y1 * _sig(t + bglu) * _silu(z)


def _merge_fn(ms, ma, gs, ga):
    return _sig(gs) * ms + _sig(ga) * ma


def _colsum(v):
    return jnp.sum(v, axis=0, keepdims=True)


def _lam_fn(lre, lim, ldt):
    a = jnp.minimum(lre, -1e-4)
    dt = jnp.exp(ldt)
    mag = jnp.exp(a * dt)
    ar = mag * jnp.cos(lim * dt)
    ai = mag * jnp.sin(lim * dt)
    den = a * a + lim * lim
    cr = ((ar - 1.0) * a + ai * lim) / den
    ci = (ai * a - (ar - 1.0) * lim) / den
    return ar, ai, cr, ci


def _bbar_fn(cr, ci, bre, bim):
    return cr * bre - ci * bim, cr * bim + ci * bre


def _same_group(rows, a, cols, b):
    r = lax.broadcasted_iota(jnp.int32, (rows, cols), 0) >> (a.bit_length() - 1)
    c = lax.broadcasted_iota(jnp.int32, (rows, cols), 1) >> (b.bit_length() - 1)
    return r == c


def _expand(name, blocks, signs, a, b, dtype):
    rows, cols = NGRP * a, NGRP * b
    assert a & (a - 1) == 0 and b & (b - 1) == 0

    def body(*refs):
        o_ref = refs[-1]
        tile = (lax.broadcasted_iota(jnp.int32, (b, cols), 1) & (b - 1)
                == lax.broadcasted_iota(jnp.int32, (b, cols), 0)).astype(F32)
        keep = _same_group(rows, a, cols, b)
        for i, (ref, sign) in enumerate(zip(refs[:-1], signs)):
            spread = jnp.dot(ref[...], tile, preferred_element_type=F32, precision=lax.Precision.HIGHEST)
            o_ref[i * rows:(i + 1) * rows, :] = jnp.where(keep, sign * spread, 0.0).astype(o_ref.dtype)

    return pl.pallas_call(body, name=name, out_shape=jax.ShapeDtypeStruct((len(blocks) * rows, cols), dtype),
                          compiler_params=pltpu.CompilerParams(vmem_limit_bytes=VMEM_LIMIT))(*blocks)


def _extract(name, m, a, b, at=(0, 0)):
    rows, cols = NGRP * a, NGRP * b
    assert a & (a - 1) == 0 and b & (b - 1) == 0

    def body(m_ref, o_ref):
        tile = (lax.broadcasted_iota(jnp.int32, (cols, b), 0) & (b - 1)
                == lax.broadcasted_iota(jnp.int32, (cols, b), 1)).astype(F32)
        kept = jnp.where(_same_group(rows, a, cols, b), m_ref[...], 0.0)
        o_ref[...] = jnp.dot(kept, tile, preferred_element_type=F32, precision=lax.Precision.HIGHEST)

    return pl.pallas_call(body, name=name, grid=(1,), in_specs=[pl.BlockSpec((rows, cols), lambda i: at)],
                          out_specs=pl.BlockSpec((rows, b), lambda i: (0, 0)),
                          out_shape=jax.ShapeDtypeStruct((rows, b), F32), compiler_params=_params("arbitrary"))(m)


def _layer_fwd(l, x, w, sp):
    tag = f"l{l}_"
    g1 = sp["pre_norm_g"].reshape(1, DM)
    (h,) = _ew(tag + "rms1", lambda x_, g: (_rms(x_, g),), SEQ, 256, [_ri(x)], [g1], [(DM, BF16)])
    hv = jnp.stack([h, _to_chunked(h)])
    win = w["w_in"]
    proj = _mm(tag + "proj", hv, win, "nn", SEQ, NCOL, DM, SEQ, 512, 1024, F32,
               a_spec=pl.BlockSpec((None, SEQ, 1024), lambda i, j, k: (_row_order(j), 0, 0)),
               b_spec=pl.BlockSpec((None, 1024, 512), lambda i, j, k: (j // 2, 0, j % 2)))

    ar, ai, cr, ci = _ew(tag + "lam", _lam_fn, NGRP, NGRP,
                         [_ri(sp["lambda_re"]), _ri(sp["lambda_im"]), _ri(sp["log_dt"].reshape(NGRP, 1))], [],
                         [(NST, F32)] * 4)
    bre = sp["b_re"].reshape(NS, GCH)
    bim = sp["b_im"].reshape(NS, GCH)
    bbr, bbi = _ew(tag + "bbar", _bbar_fn, NS, NS, [_ri(cr.reshape(NS, 1)), _ri(ci.reshape(NS, 1)), _ri(bre), _ri(bim)],
                   [], [(GCH, F32)] * 2)
    wdt = _expand(tag + "wdt", [bbr, bbi], [1.0, 1.0], NST, GCH, BF16)
    cmt = _expand(tag + "cmt", [sp["c_re"].reshape(SW, NST), sp["c_im"].reshape(SW, NST)], [1.0, -1.0], GCH, NST, BF16)
    s, y0p = _s5_forward(tag + "s5", proj, wdt, cmt, ar.reshape(1, NS), ai.reshape(1, NS))
    dskip = sp["d_skip"].reshape(1, SW)
    (y1,) = _ew(tag + "y1", lambda a, u, d: (_y1_fn(a, u, d),), SEQ, 256,
                [_ri(y0p), _ri(proj, SW, CB_U)], [dskip], [(SW, F32)])
    t = _mm(tag + "glu", y1, w["w_glu"], "nn", SEQ, SW, SW, 1024, 512, 512, F32)
    bglu = sp["b_glu"].reshape(1, SW)
    (ys_c,) = _ew(tag + "ys", lambda y1_, t_, z, b_: (_ys_fn(y1_, t_, z, b_),), SEQ, 256,
                  [_ri(y1), _ri(t), _ri(proj, SW, CB_ZS)], [bglu], [(SW, BF16)])
    ys = _from_chunked(ys_c)

    ypre, ya, lse = _attention_fwd(tag + "attn", proj)

    ms = _mm(tag + "branch_s", ys, w["w_branch_s"], "nn", SEQ, DM, SW, 1024, 1024, 512, F32)
    ma = _mm(tag + "branch_a", ya, w["w_branch_a"], "nn", SEQ, DM, AW, 1024, 1024, 512, F32)
    (merged,) = _ew(tag + "merge", lambda a, b_, c, d: (_merge_fn(a, b_, c, d),), SEQ, 256,
                    [_ri(ms), _ri(ma), _ri(proj, DM, CB_GS), _ri(proj, DM, CB_GA)], [], [(DM, BF16)])
    out = _mm(tag + "out", merged, w["w_out"], "nn", SEQ, DM, DM, 1024, 1024, 1024, F32)
    g2 = sp["post_norm_g"].reshape(1, DM)
    (x_new,) = _ew(tag + "post", lambda x_, o, g: (x_ + _rms(o, g),), SEQ, 256, [_ri(x), _ri(out)], [g2], [(DM, F32)])
    res = dict(x=x, hv=hv, proj=proj, ar=ar, ai=ai, cr=cr, ci=ci, wdt=wdt, cmt=cmt, s=s, y0p=y0p,
               y1=y1, t=t, ys=ys, ya=ya, ypre=ypre, lse=lse, ms=ms, ma=ma, merged=merged, out=out)
    return x_new, res


def _layer_bwd(l, dxn, r, w, sp):
    tag = f"l{l}b_"
    proj = r["proj"]
    g1 = sp["pre_norm_g"].reshape(1, DM)
    g2 = sp["post_norm_g"].reshape(1, DM)
    dskip = sp["d_skip"].reshape(1, SW)
    bglu = sp["b_glu"].reshape(1, SW)

    def post_b(d, o, g):
        _, vjp = jax.vjp(_rms, o, g)
        do, dg = vjp(d)
        return do, dg

    d_out, dg2 = _ew(tag + "post", post_b, SEQ, 256, [_ri(dxn), _ri(r["out"])], [g2], [(DM, BF16)], [DM])
    dw_out = _mm(tag + "dw_out", r["merged"], d_out, "tn", DM, DM, SEQ, 1024, 1024, SEQ, BF16)
    dmerged = _mm(tag + "dmerged", d_out, w["w_out"], "nt", SEQ, DM, DM, 1024, 1024, 1024, F32)

    def merge_b(d, ms, ma, gs, ga):
        _, vjp = jax.vjp(_merge_fn, ms, ma, gs, ga)
        return vjp(d)

    dms, dma, dgs, dga = _ew(tag + "merge", merge_b, SEQ, 256,
                             [_ri(dmerged), _ri(r["ms"]), _ri(r["ma"]), _ri(proj, DM, CB_GS), _ri(proj, DM, CB_GA)],
                             [], [(DM, BF16)] * 4)
    dw_bs = _mm(tag + "dw_bs", r["ys"], dms, "tn", SW, DM, SEQ, 512, 1024, SEQ, BF16)
    dw_ba = _mm(tag + "dw_ba", r["ya"], dma, "tn", AW, DM, SEQ, 512, 1024, SEQ, BF16)
    dys = _mm(tag + "dys", dms, w["w_branch_s"], "nt", SEQ, SW, DM, 1024, 512, 1024, F32)
    dya = _mm(tag + "dya", dma, w["w_branch_a"], "nt", SEQ, AW, DM, 1024, 512, 1024, F32)

    dqkv, dza = _attention_bwd(tag + "attn", proj, dya, r["ypre"], r["lse"])

    def ys_b(d, y1, t, z, b_):
        _, vjp = jax.vjp(_ys_fn, y1, t, z, b_)
        dy1, dt, dz, _ = vjp(d)
        return dy1, dt, dz, _colsum(dt)

    dy1a, dt, dzs, dbglu = _ew(tag + "ys", ys_b, SEQ, 256,
                               [_ri(_to_chunked(dys)), _ri(r["y1"]), _ri(r["t"]), _ri(proj, SW, CB_ZS)],
                               [bglu], [(SW, F32), (SW, BF16), (SW, BF16)], [SW])
    dw_glu = _mm(tag + "dw_glu", r["y1"], dt, "tn", SW, SW, SEQ, 512, 512, SEQ, BF16)
    dy1b = _mm(tag + "dy1b", dt, w["w_glu"], "nt", SEQ, SW, SW, 1024, 512, 512, F32)

    def y1_b(da, db, y0p, u, d_):
        _, vjp = jax.vjp(_y1_fn, y0p, u, d_)
        dy0, du, dd = vjp(da + db)
        return dy0, du, dd

    dy0, du_skip, ddskip = _ew(tag + "y1", y1_b, SEQ, 256,
                               [_ri(dy1a), _ri(dy1b), _ri(r["y0p"]), _ri(proj, SW, CB_U)], [dskip],
                               [(SW, BF16), (SW, F32)], [SW])
    g_s = _mm(tag + "dstate", dy0, r["cmt"], "nn", SEQ, 2 * NS, SW, 1024, 1024, 512, F32,
              b_spec=pl.BlockSpec((512, 1024), lambda i, j, k: (j // 2, j % 2)))
    dcmt = _mm(tag + "dcmt", dy0, r["s"], "tn", SW, 2 * NS, SEQ, 512, 1024, SEQ, F32,
               b_spec=pl.BlockSpec((None, SEQ, 1024), lambda i, j, k: (j // 2, 0, j % 2)))
    lam_r, lam_i = r["ar"].reshape(1, NS), r["ai"].reshape(1, NS)
    adj = _scan(tag + "scan", g_s, lam_r, -lam_i, True)
    dlr, dli = _dlam(tag + "dlam", adj, r["s"])
    dwdt = _mm(tag + "dwdt", adj, proj, "tn", 2 * NS, SW, SEQ, 1024, 512, SEQ, F32,
               a_spec=pl.BlockSpec((None, SEQ, 1024), lambda i, j, k: (i // 2, 0, i % 2)),
               b_spec=pl.BlockSpec((SEQ, SW), lambda i, j, k: (0, CB_U)))
    du_s = _mm(tag + "du_s", adj, r["wdt"], "nn", SEQ, SW, 2 * NS, 1024, 512, 1024, F32, a_spec=_plane_spec(1024, 1024))
    (du,) = _ew(tag + "du", lambda a, c: (a + c,), SEQ, 256, [_ri(du_s), _ri(du_skip)], [], [(SW, BF16)])

    dbbr = _extract(tag + "dbbr", dwdt, NST, GCH, (0, 0))
    dbbi = _extract(tag + "dbbi", dwdt, NST, GCH, (1, 0))
    bre = sp["b_re"].reshape(NS, GCH)
    bim = sp["b_im"].reshape(NS, GCH)

    def bbar_b(cr, ci, br_, bi_, dr, di):
        _, vjp = jax.vjp(_bbar_fn, cr, ci, br_, bi_)
        return vjp((dr, di))

    dcr, dci, dbre, dbim = _ew(tag + "bbar", bbar_b, NS, NS,
                               [_ri(r["cr"].reshape(NS, 1)), _ri(r["ci"].reshape(NS, 1)), _ri(bre), _ri(bim),
                                _ri(dbbr), _ri(dbbi)], [], [(1, F32), (1, F32), (GCH, F32), (GCH, F32)])

    def lam_b(lre, lim, ldt, dar, dai, dcr_, dci_):
        _, vjp = jax.vjp(_lam_fn, lre, lim, ldt)
        return vjp((dar, dai, dcr_, dci_))

    dlre, dlim, dldt = _ew(tag + "lam", lam_b, NGRP, NGRP,
                           [_ri(sp["lambda_re"]), _ri(sp["lambda_im"]), _ri(sp["log_dt"].reshape(NGRP, 1)),
                            _ri(dlr.reshape(NGRP, NST)), _ri(dli.reshape(NGRP, NST)),
                            _ri(dcr.reshape(NGRP, NST)), _ri(dci.reshape(NGRP, NST))], [],
                           [(NST, F32), (NST, F32), (1, F32)])
    dc_re = _extract(tag + "dc_re", dcmt, GCH, NST, (0, 0)).reshape(NGRP, GCH, NST)
    dc_im = -_extract(tag + "dc_im", dcmt, GCH, NST, (0, 1)).reshape(NGRP, GCH, NST)

    dq, dk, dv = ([dqkv[3 * g + i] for g in range(3)] for i in range(3))
    dproj = jnp.concatenate([du, dzs, *dq, *dk, *dv, dza, dgs, dga], axis=1)
    dw_in = _mm(tag + "dw_in", r["hv"], dproj, "tn", DM, NCOL, SEQ, 1024, 512, SEQ, BF16,
                a_spec=pl.BlockSpec((None, SEQ, 1024), lambda i, j, k: (_row_order(j), 0, 0)),
                o_spec=pl.BlockSpec((None, 1024, 512), lambda i, j, k: (j // 2, 0, j % 2)), out_shape=(NDEV, DM, DM))

    dh_time = _mm(tag + "dh_time", dproj, w["w_in"], "nt", SEQ, DM, NCOL - DM, SEQ, 1024, 1024, F32,
                  a_spec=pl.BlockSpec((SEQ, 1024), lambda i, j, k: (0, 1 + k)),
                  b_spec=pl.BlockSpec((None, 1024, 1024), lambda i, j, k: (1 + k, 0, 0)))
    dh_chunked = _mm(tag + "dh_chunked", dproj, w["w_in"], "nt", SEQ, DM, DM, SEQ, 1024, 1024, F32,
                     a_spec=pl.BlockSpec((SEQ, 1024), lambda i, j, k: (0, 0)),
                     b_spec=pl.BlockSpec((None, 1024, 1024), lambda i, j, k: (0, 0, 0)))
    dh = [dh_time, _from_chunked(dh_chunked)]

    def pre_b(d, dh0, dh1, x_, g):
        _, vjp = jax.vjp(_rms, x_, g)
        dx_, dg = vjp(dh0 + dh1)
        return d + dx_, dg

    dx, dg1 = _ew(tag + "pre", pre_b, SEQ, 256, [_ri(dxn)] + [_ri(t_) for t_ in dh] + [_ri(r["x"])], [g1],
                  [(DM, F32)], [DM])

    big = dict(w_in=dw_in, w_glu=dw_glu.reshape(NDEV, SW // NDEV, SW),
               w_branch_s=dw_bs.reshape(SW, NDEV, DM // NDEV).transpose(1, 0, 2),
               w_branch_a=dw_ba.reshape(AW, NDEV, DM // NDEV).transpose(1, 0, 2),
               w_out=dw_out.reshape(NDEV, DM // NDEV, DM))
    small = dict(pre_norm_g=dg1.reshape(DM), lambda_re=dlre, lambda_im=dlim, log_dt=dldt.reshape(NGRP),
                 b_re=dbre.reshape(NGRP, NST, GCH), b_im=dbim.reshape(NGRP, NST, GCH), c_re=dc_re, c_im=dc_im,
                 d_skip=ddskip.reshape(SW), b_glu=dbglu.reshape(SW), post_norm_g=dg2.reshape(DM))
    return dx, big, small


_HBM = pl.BlockSpec(memory_space=pltpu.HBM)
_SEM = pl.BlockSpec(memory_space=pltpu.SEMAPHORE)
_EFFECT = pltpu.SideEffectType.DATAFLOW_SIDE_EFFECTING


def _remote_copies(srcs, dsts, send_sems, recv_sems, gather):
    x, y, c = lax.axis_index("x"), lax.axis_index("y"), lax.axis_index("c")
    me = 4 * x + 2 * y + c
    copies = []
    for i in range(len(srcs)):
        for k in range(1, NDEV):
            peer = (x ^ (k >> 2), y ^ ((k >> 1) & 1), c ^ (k & 1))
            src = srcs[i] if gather[i] else srcs[i].at[me ^ k]
            copies.append(pltpu.make_async_remote_copy(
                src_ref=src, dst_ref=dsts[i].at[me], send_sem=send_sems[i], recv_sem=recv_sems[i],
                device_id=peer, device_id_type=pl.DeviceIdType.MESH))
    return copies


def _all_seven(dst, send_sem, recv_sem):
    seven = dst.at[pl.ds(0, NDEV - 1)]
    me = (lax.axis_index("x"), lax.axis_index("y"), lax.axis_index("c"))
    return pltpu.make_async_remote_copy(src_ref=seven, dst_ref=seven, send_sem=send_sem, recv_sem=recv_sem,
                                        device_id=me, device_id_type=pl.DeviceIdType.MESH)


def _own_slabs(name, arrs, gather, after):
    n = len(arrs)
    me = (4 * lax.axis_index("x") + 2 * lax.axis_index("y") + lax.axis_index("c")).astype(jnp.int32).reshape(1)

    def body(me_ref, *refs):
        for src, dst in zip(refs[:n], refs[n + 1:]):
            dst[...] = src[...]

    def zeros(k):
        return (0,) * k

    in_specs, out_specs, out_shape = [], [], []
    for a, g in zip(arrs, gather):
        slab = a.shape if g else a.shape[1:]
        nd = len(slab)
        if g:
            in_specs.append(pl.BlockSpec(slab, functools.partial(lambda i, me_ref, nd: zeros(nd), nd=nd)))
        else:
            in_specs.append(pl.BlockSpec((None,) + slab, functools.partial(lambda i, me_ref, nd: (me_ref[0],) + zeros(nd), nd=nd)))
        out_specs.append(pl.BlockSpec((None,) + slab, functools.partial(lambda i, me_ref, nd: (me_ref[0],) + zeros(nd), nd=nd)))
        out_shape.append(jax.ShapeDtypeStruct((NDEV,) + slab, a.dtype))
    in_specs.append(pl.BlockSpec(memory_space=pl.ANY))
    return pl.pallas_call(
        body, name=name, out_shape=out_shape,
        grid_spec=pltpu.PrefetchScalarGridSpec(num_scalar_prefetch=1, grid=(1,), in_specs=in_specs, out_specs=out_specs),
        compiler_params=_params("arbitrary"),
    )(me, *arrs, after)


def _exchange_start(name, arrs, gather, after):
    n = len(arrs)
    lands = _own_slabs(name + "_own", arrs, gather, after)

    def body(*refs):
        srcs, dsts = refs[:n], refs[n:2 * n]
        send_sems, recv_sems = refs[2 * n:3 * n], refs[3 * n:4 * n]
        token = refs[-1]
        for cp in _remote_copies(srcs, dsts, send_sems, recv_sems, gather):
            cp.start()
        token[...] = jnp.zeros(token.shape, token.dtype)

    thru = [pltpu.HBM(a.shape, a.dtype) for a in list(arrs) + list(lands)]
    outs = pl.pallas_call(
        body, name=name,
        out_shape=(*[pltpu.SemaphoreType.DMA(())] * (2 * n), *thru, jax.ShapeDtypeStruct((8, 128), F32)),
        in_specs=[_HBM] * (2 * n),
        out_specs=(*[_SEM] * (2 * n), *[_HBM] * (2 * n), pl.BlockSpec(memory_space=pltpu.VMEM)),
        input_output_aliases={i: 2 * n + i for i in range(2 * n)},
        compiler_params=pltpu.CompilerParams(has_side_effects=_EFFECT),
    )(*[pltpu.with_memory_space_constraint(a, pltpu.HBM) for a in list(arrs) + list(lands)])
    return dict(send=outs[:n], recv=outs[n:2 * n], srcs=outs[2 * n:3 * n], lands=outs[3 * n:4 * n], token=outs[-1],
                gather=gather)


def _exchange_wait(name, started, after):
    n = len(started["srcs"])

    def body(*refs):
        dsts = refs[n:2 * n]
        send_sems, recv_sems = refs[2 * n:3 * n], refs[3 * n:4 * n]
        for i in range(n):
            cp = _all_seven(dsts[i], send_sems[i], recv_sems[i])
            cp.wait_send()
            cp.wait_recv()

    bufs = list(started["srcs"]) + list(started["lands"])
    outs = pl.pallas_call(
        body, name=name, out_shape=tuple(pltpu.HBM(a.shape, a.dtype) for a in bufs),
        in_specs=[_HBM] * (2 * n) + [_SEM] * (2 * n) + [pl.BlockSpec(memory_space=pl.ANY)], out_specs=(_HBM,) * (2 * n),
        input_output_aliases={i: i for i in range(2 * n)},
        compiler_params=pltpu.CompilerParams(has_side_effects=_EFFECT),
    )(*bufs, *started["send"], *started["recv"], after)
    return outs[n:]


def _sum_in_order(parts):
    g = parts[0].astype(F32)
    for p in parts[1:]:
        g = g + p.astype(F32)
    return g


def _adamw(name, parts, nparts, w, m, v, br):
    rows, cols = w.shape

    def fn(*a):
        g = _sum_in_order(a[:nparts])
        w_, m_, v_ = a[nparts:]
        m2 = B1 * m_ + (1.0 - B1) * g
        v2 = B2 * v_ + (1.0 - B2) * (g * g)
        m_hat = m2 / (1.0 - B1 ** STEP)
        v_hat = v2 / (1.0 - B2 ** STEP)
        delta = -LR * (m_hat / (jnp.sqrt(v_hat) + ADAM_EPS) + WD * w_)
        return g, delta, m2, v2

    ins = [_ri(parts, cols, 0, d * (rows // br)) for d in range(nparts)] + [_ri(w), _ri(m), _ri(v)]
    return _ew(name, fn, rows, br, ins, [], [(cols, F32)] * 4)


SMALL = ("pre_norm_g", "lambda_re", "lambda_im", "log_dt", "b_re", "b_im", "c_re", "c_im", "d_skip", "b_glu",
         "post_norm_g")
BIG = ("w_in", "w_glu", "w_branch_s", "w_branch_a", "w_out")
WEIGHTS = ("pre_norm_g", "w_in", "lambda_re", "lambda_im", "log_dt", "b_re", "b_im", "c_re", "c_im", "d_skip",
           "w_glu", "b_glu", "w_branch_s", "w_branch_a", "w_out", "post_norm_g")
PACK_COLS = 1024
PACK_BR = 136


def _pack_layer(d):
    pieces = [d[k].astype(F32).reshape(-1) for k in SMALL]
    used = sum(p.shape[0] for p in pieces)
    assert used <= PACK_BR * PACK_COLS
    return jnp.concatenate(pieces + [jnp.zeros((PACK_BR * PACK_COLS - used,), F32)]).reshape(PACK_BR, PACK_COLS)


def _unpack(p, like):
    flat = p.reshape(DEPTH, PACK_BR * PACK_COLS)
    out, off = {}, 0
    for k in SMALL:
        n = like[k].size // DEPTH
        out[k] = flat[:, off:off + n].reshape(like[k].shape)
        off += n
    return out


def _local_step(x, target, small, weights_of, grads_done):
    res, ws = [], []
    for l in range(DEPTH):
        w_l, tok = weights_of(l, x)
        sp = {k: small[k][l] for k in SMALL}
        if tok is not None:
            sp["pre_norm_g"] = sp["pre_norm_g"] + tok[0, 0]
        x, r = _layer_fwd(l, x, w_l, sp)
        res.append(r)
        ws.append(w_l)

    def loss_fn(y, t):
        e = y - t
        return e * (1.0 / DM), jnp.sum(_colsum(0.5 * e * e * (1.0 / DM)), axis=1, keepdims=True)

    dx, loss = _ew("loss", loss_fn, SEQ, 256, [_ri(x), _ri(target)], [], [(DM, F32)], [1])
    tok = None
    for l in reversed(range(DEPTH)):
        sp = {k: small[k][l] for k in SMALL}
        if tok is not None:
            sp["post_norm_g"] = sp["post_norm_g"] + tok[0, 0]
        dx, big, sm = _layer_bwd(l, dx, res[l], ws[l], sp)
        tok = grads_done(l, big, sm)
    return loss.reshape(()), dx


def _full_weights(gathered):
    g = gathered
    return dict(
        w_in=g["w_in"],
        w_glu=g["w_glu"].reshape(SW, SW),
        w_branch_s=g["w_branch_s"].transpose(1, 0, 2).reshape(SW, DM),
        w_branch_a=g["w_branch_a"].transpose(1, 0, 2).reshape(AW, DM),
        w_out=g["w_out"].reshape(DM, DM),
    )


def kernel(x, pre_norm_g, w_in, lambda_re, lambda_im, log_dt, b_re, b_im, c_re, c_im, d_skip, w_glu, b_glu, w_branch_s, w_branch_a, w_out, post_norm_g, loss_target, m_pre_norm_g, m_w_in, m_lambda_re, m_lambda_im, m_log_dt, m_b_re, m_b_im, m_c_re, m_c_im, m_d_skip, m_w_glu, m_b_glu, m_w_branch_s, m_w_branch_a, m_w_out, m_post_norm_g, v_pre_norm_g, v_w_in, v_lambda_re, v_lambda_im, v_log_dt, v_b_re, v_b_im, v_c_re, v_c_im, v_d_skip, v_w_glu, v_b_glu, v_w_branch_s, v_w_branch_a, v_w_out, v_post_norm_g):
    wts = dict(pre_norm_g=pre_norm_g, w_in=w_in, lambda_re=lambda_re, lambda_im=lambda_im, log_dt=log_dt, b_re=b_re,
               b_im=b_im, c_re=c_re, c_im=c_im, d_skip=d_skip, w_glu=w_glu, b_glu=b_glu, w_branch_s=w_branch_s,
               w_branch_a=w_branch_a, w_out=w_out, post_norm_g=post_norm_g)
    mom = dict(pre_norm_g=m_pre_norm_g, w_in=m_w_in, lambda_re=m_lambda_re, lambda_im=m_lambda_im, log_dt=m_log_dt,
               b_re=m_b_re, b_im=m_b_im, c_re=m_c_re, c_im=m_c_im, d_skip=m_d_skip, w_glu=m_w_glu, b_glu=m_b_glu,
               w_branch_s=m_w_branch_s, w_branch_a=m_w_branch_a, w_out=m_w_out, post_norm_g=m_post_norm_g)
    var = dict(pre_norm_g=v_pre_norm_g, w_in=v_w_in, lambda_re=v_lambda_re, lambda_im=v_lambda_im, log_dt=v_log_dt,
               b_re=v_b_re, b_im=v_b_im, c_re=v_c_re, c_im=v_c_im, d_skip=v_d_skip, w_glu=v_w_glu, b_glu=v_b_glu,
               w_branch_s=v_w_branch_s, w_branch_a=v_w_branch_a, w_out=v_w_out, post_norm_g=v_post_norm_g)

    def gather_start(l, after):
        return _exchange_start(f"gather_start{l}", [wts[k][l].astype(BF16) for k in BIG], [True] * len(BIG), after)

    gathering = {0: gather_start(0, x)}
    sending = {}

    def weights_of(l, x_l):
        gathered = _exchange_wait(f"gather_wait{l}", gathering[l], x_l)
        tok = None
        if l + 1 < DEPTH:
            gathering[l + 1] = gather_start(l + 1, gathered[0])
            tok = gathering[l + 1]["token"]
        return _full_weights(dict(zip(BIG, gathered))), tok

    def grads_done(l, big, sm):
        sending[l] = _exchange_start(f"grads_start{l}", [big[k] for k in BIG] + [_pack_layer(sm)],
                                     [False] * len(BIG) + [True], big["w_in"])
        return sending[l]["token"]

    loss, dx = _local_step(x[0], loss_target[0], wts, weights_of, grads_done)
    loss = lax.psum(loss, ("x", "y", "c"))
    recv_l = [_exchange_wait(f"grads_wait{l}", sending[l], dx) for l in range(DEPTH)]
    recv = [jnp.stack([r[i] for r in recv_l], axis=1) for i in range(len(BIG))]
    recv.append(jnp.concatenate([r[len(BIG)] for r in recv_l], axis=1))

    grads, delta, new_m, new_v = {}, {}, {}, {}

    def update(k, parts, nparts):
        shape = wts[k].shape
        cols = shape[-1]
        rows = wts[k].size // cols
        br = min(rows, 1024 if cols <= 128 else 256)
        outs = _adamw("adamw_" + k, parts.reshape(nparts * rows, cols), nparts, wts[k].reshape(rows, cols),
                      mom[k].reshape(rows, cols), var[k].reshape(rows, cols), br)
        grads[k], delta[k], new_m[k], new_v[k] = (o.reshape(shape) for o in outs)

    for k, parts in zip(BIG, recv[:len(BIG)]):
        update(k, parts, NDEV)
    rows = DEPTH * PACK_BR
    (g_small,) = _ew("grads_small", lambda *p: (_sum_in_order(p),), rows, PACK_BR,
                     [_ri(recv[-1].reshape(NDEV * rows, PACK_COLS), PACK_COLS, 0, d * DEPTH) for d in range(NDEV)], [],
                     [(PACK_COLS, F32)])
    for k, g in _unpack(g_small, wts).items():
        update(k, g, 1)

    return (loss, dx[None], *[grads[k] for k in WEIGHTS], *[delta[k] for k in WEIGHTS],
            *[new_m[k] for k in WEIGHTS], *[new_v[k] for k in WEIGHTS])
```

```python
import functools
import math

import jax
import jax.numpy as jnp
from jax import lax
from jax.experimental import pallas as pl
from jax.experimental.pallas import tpu as pltpu

F32 = jnp.float32
BF16 = jnp.bfloat16

NDEV = 8
DEPTH = 4
SEQ = 2048
DM = 1024
NCOL = 8192
SW = 512
NGRP = 32
GCH = 16
NST = 64
NS = NGRP * NST
HD = 128
AW = 512
DILATIONS = (1, 4, 16)
ABLK = 128
ATTN_PAIR = 2
RMS_EPS = 1e-6
LR, B1, B2, ADAM_EPS, WD, STEP = 0.001, 0.9, 0.999, 1e-08, 0.01, 10

CB_U, CB_ZS, CB_Q, CB_K, CB_V, CB_ZA = 0, 1, 2, 5, 8, 11
CB_GS, CB_GA = 6, 7

VMEM_LIMIT = 56 * 2 ** 20


def _row_order(j):
    return jnp.where(j < CB_Q, 1, 0)


def _params(*sem):
    return pltpu.CompilerParams(dimension_semantics=sem, vmem_limit_bytes=VMEM_LIMIT)


def _ew(name, fn, rows, br, row_ins, bc_ins, row_outs, red_outs=()):
    n_in = len(row_ins) + len(bc_ins)
    n_ro = len(row_outs)
    steps = rows // br
    assert steps * br == rows

    def body(*refs):
        vals = fn(*[r[...] for r in refs[:n_in]])
        outs = refs[n_in:]
        for r, v in zip(outs[:n_ro], vals[:n_ro]):
            r[...] = v.astype(r.dtype)
        if red_outs:
            @pl.when(pl.program_id(0) == 0)
            def _():
                for r in outs[n_ro:]:
                    r[...] = jnp.zeros(r.shape, r.dtype)
            for r, v in zip(outs[n_ro:], vals[n_ro:]):
                r[...] += v

    in_specs = []
    for (_, w, cb, rb) in row_ins:
        in_specs.append(pl.BlockSpec((br, w), functools.partial(lambda i, cb, rb: (rb + i, cb), cb=cb, rb=rb)))
    for a in bc_ins:
        in_specs.append(pl.BlockSpec(a.shape, functools.partial(lambda i, nd: (0,) * nd, nd=a.ndim)))
    out_specs = [pl.BlockSpec((br, w), lambda i: (i, 0)) for (w, _) in row_outs]
    out_specs += [pl.BlockSpec((1, w), lambda i: (0, 0)) for w in red_outs]
    out_shape = [jax.ShapeDtypeStruct((rows, w), dt) for (w, dt) in row_outs]
    out_shape += [jax.ShapeDtypeStruct((1, w), F32) for w in red_outs]
    return pl.pallas_call(
        body, name=name, grid=(steps,), in_specs=in_specs, out_specs=out_specs, out_shape=out_shape,
        compiler_params=_params("arbitrary"),
    )(*[a for (a, _, _, _) in row_ins], *bc_ins)


def _ri(a, w=None, cb=0, rb=0):
    return (a, a.shape[1] if w is None else w, cb, rb)


_DIMS = {"nn": ((1,), (0,)), "nt": ((1,), (1,)), "tn": ((0,), (0,))}


def _mm(name, a, b, mode, M, N, K, bm, bn, bk, out_dtype, a_spec=None, b_spec=None, o_spec=None, out_shape=None):
    nk = K // bk
    assert M % bm == 0 and N % bn == 0 and nk * bk == K

    own_acc = nk > 1 and out_dtype != F32

    def body(a_ref, b_ref, o_ref, *scratch):
        part = lax.dot_general(a_ref[...].astype(BF16), b_ref[...].astype(BF16), (_DIMS[mode], ((), ())),
                               preferred_element_type=F32)
        if nk == 1:
            o_ref[...] = part.astype(o_ref.dtype)
            return
        k = pl.program_id(2)
        acc_ref = scratch[0] if own_acc else o_ref

        @pl.when(k == 0)
        def _():
            acc_ref[...] = part

        @pl.when(k > 0)
        def _():
            acc_ref[...] += part

        if own_acc:
            @pl.when(k == nk - 1)
            def _():
                o_ref[...] = acc_ref[...].astype(o_ref.dtype)

    if a_spec is None:
        a_spec = (pl.BlockSpec((bk, bm), lambda i, j, k: (k, i)) if mode == "tn"
                  else pl.BlockSpec((bm, bk), lambda i, j, k: (i, k)))
    if b_spec is None:
        b_spec = (pl.BlockSpec((bn, bk), lambda i, j, k: (j, k)) if mode == "nt"
                  else pl.BlockSpec((bk, bn), lambda i, j, k: (k, j)))
    if o_spec is None:
        o_spec = pl.BlockSpec((bm, bn), lambda i, j, k: (i, j))
    if out_shape is None:
        out_shape = (M, N)
    return pl.pallas_call(
        body, name=name, grid=(M // bm, N // bn, nk), in_specs=[a_spec, b_spec], out_specs=o_spec,
        out_shape=jax.ShapeDtypeStruct(out_shape, out_dtype),
        scratch_shapes=[pltpu.VMEM((bm, bn), F32)] if own_acc else [],
        compiler_params=_params("parallel", "parallel", "arbitrary"),
    )(a, b)


SCAN_LANES = 512
SCAN_CHUNKS = 8


def _to_chunked(a):
    return a.reshape(SCAN_CHUNKS, SEQ // SCAN_CHUNKS, -1).transpose(1, 0, 2).reshape(SEQ, -1)


def _from_chunked(a):
    return a.reshape(SEQ // SCAN_CHUNKS, SCAN_CHUNKS, -1).transpose(1, 0, 2).reshape(SEQ, -1)


def _scan_block(dr_ref, di_ref, sr_ref, si_ref, lam_r, lam_i, reverse):
    T = SEQ // SCAN_CHUNKS
    bl = lam_r.shape[1]
    assert T == 2 ** 8
    ar = jnp.broadcast_to(lam_r, (SCAN_CHUNKS, bl))
    ai = jnp.broadcast_to(lam_i, (SCAN_CHUNKS, bl))
    zero = jnp.zeros((SCAN_CHUNKS, bl), F32)

    def tile(j):
        return pl.ds(pl.multiple_of(j * SCAN_CHUNKS, SCAN_CHUNKS), SCAN_CHUNKS)

    def step(jj, carry):
        sr, si = carry
        j = T - 1 - jj if reverse else jj
        nr = ar * sr - ai * si + dr_ref[tile(j), :]
        ni = ar * si + ai * sr + di_ref[tile(j), :]
        sr_ref[tile(j), :] = nr
        si_ref[tile(j), :] = ni
        return nr, ni

    er, ei = lax.fori_loop(0, T, step, (zero, zero), unroll=4)

    pr, pi = ar[0:1], ai[0:1]
    for _ in range(8):
        pr, pi = pr * pr - pi * pi, 2.0 * pr * pi
    rows = lax.broadcasted_iota(jnp.int32, (SCAN_CHUNKS, bl), 0)
    cr, ci = zero, zero
    xr = jnp.zeros((1, bl), F32)
    xi = jnp.zeros((1, bl), F32)
    order = range(SCAN_CHUNKS - 2, -1, -1) if reverse else range(1, SCAN_CHUNKS)
    for c in order:
        src = c + 1 if reverse else c - 1
        nxr = pr * xr - pi * xi + er[src:src + 1]
        nxi = pr * xi + pi * xr + ei[src:src + 1]
        xr, xi = nxr, nxi
        cr = jnp.where(rows == c, xr, cr)
        ci = jnp.where(rows == c, xi, ci)

    def fix(jj, pw):
        pwr, pwi = pw
        j = T - 1 - jj if reverse else jj
        sr_ref[tile(j), :] = sr_ref[tile(j), :] + (pwr * cr - pwi * ci)
        si_ref[tile(j), :] = si_ref[tile(j), :] + (pwr * ci + pwi * cr)
        return pwr * ar - pwi * ai, pwr * ai + pwi * ar

    lax.fori_loop(0, T, fix, (ar, ai), unroll=4)


def _s5_forward(name, proj, wdt, cmt, lam_r, lam_i):
    bl = SCAN_LANES
    nblk = NS // bl

    def nt(a, b_):
        return lax.dot_general(a, b_, (((1,), (1,)), ((), ())), preferred_element_type=F32)

    def body(u_ref, wr_ref, wi_ref, cr_ref, ci_ref, ar_ref, ai_ref, s_ref, y_ref):
        sr_ref, si_ref = s_ref.at[0], s_ref.at[1]
        u = u_ref[...].astype(BF16)
        sr_ref[...] = nt(u, wr_ref[...])
        si_ref[...] = nt(u, wi_ref[...])
        _scan_block(sr_ref, si_ref, sr_ref, si_ref, ar_ref[...], ai_ref[...], False)
        part = nt(sr_ref[...].astype(BF16), cr_ref[...]) + nt(si_ref[...].astype(BF16), ci_ref[...])

        @pl.when(pl.program_id(0) == 0)
        def _():
            y_ref[...] = part

        @pl.when(pl.program_id(0) > 0)
        def _():
            y_ref[...] += part

    return pl.pallas_call(
        body, name=name, grid=(nblk,),
        in_specs=[pl.BlockSpec((SEQ, SW), lambda i: (0, CB_U)),
                  pl.BlockSpec((bl, SW), lambda i: (i, 0)), pl.BlockSpec((bl, SW), lambda i: (nblk + i, 0)),
                  pl.BlockSpec((SW, bl), lambda i: (0, i)), pl.BlockSpec((SW, bl), lambda i: (1, i)),
                  pl.BlockSpec((1, bl), lambda i: (0, i)), pl.BlockSpec((1, bl), lambda i: (0, i))],
        out_specs=[pl.BlockSpec((2, SEQ, bl), lambda i: (0, 0, i)), pl.BlockSpec((SEQ, SW), lambda i: (0, 0))],
        out_shape=[jax.ShapeDtypeStruct((2, SEQ, NS), F32), jax.ShapeDtypeStruct((SEQ, SW), F32)],
        compiler_params=_params("arbitrary"),
    )(proj, wdt, wdt, cmt, cmt, lam_r, lam_i)


def _s5_adjoint(name, dy0, cmt, lam_r, lam_i):
    bl = SCAN_LANES
    nblk = NS // bl

    def body(dy_ref, cr_ref, ci_ref, ar_ref, ai_ref, s_ref):
        sr_ref, si_ref = s_ref.at[0], s_ref.at[1]
        dy = dy_ref[...]
        sr_ref[...] = jnp.dot(dy, cr_ref[...], preferred_element_type=F32)
        si_ref[...] = jnp.dot(dy, ci_ref[...], preferred_element_type=F32)
        _scan_block(sr_ref, si_ref, sr_ref, si_ref, ar_ref[...], -ai_ref[...], True)

    return pl.pallas_call(
        body, name=name, grid=(nblk,),
        in_specs=[pl.BlockSpec((SEQ, SW), lambda i: (0, 0)),
                  pl.BlockSpec((SW, bl), lambda i: (0, i)), pl.BlockSpec((SW, bl), lambda i: (1, i)),
                  pl.BlockSpec((1, bl), lambda i: (0, i)), pl.BlockSpec((1, bl), lambda i: (0, i))],
        out_specs=pl.BlockSpec((2, SEQ, bl), lambda i: (0, 0, i)),
        out_shape=jax.ShapeDtypeStruct((2, SEQ, NS), F32),
        compiler_params=_params("arbitrary"),
    )(dy0, cmt, cmt, lam_r, lam_i)


def _plane_spec(bm, bk):
    per = NS // bk
    return pl.BlockSpec((None, bm, bk), lambda i, j, k: (k // per, i, k % per))


def _dlam(name, a, s):
    bl = 256

    def prev(s_ref):
        last = pltpu.roll(s_ref[SEQ - SCAN_CHUNKS:SEQ, :], 1, 0)
        first = jnp.where(lax.broadcasted_iota(jnp.int32, (SCAN_CHUNKS, bl), 0) > 0, last, 0.0)
        return jnp.concatenate([first, s_ref[0:SEQ - SCAN_CHUNKS, :]], axis=0)

    def body(ar_ref, ai_ref, sr_ref, si_ref, or_ref, oi_ref):
        spr, spi = prev(sr_ref), prev(si_ref)
        a_r_, a_i_ = ar_ref[...], ai_ref[...]
        or_ref[...] = jnp.sum(a_r_ * spr + a_i_ * spi, axis=0, keepdims=True)
        oi_ref[...] = jnp.sum(a_i_ * spr - a_r_ * spi, axis=0, keepdims=True)

    re_spec = pl.BlockSpec((None, SEQ, bl), lambda i: (0, 0, i))
    im_spec = pl.BlockSpec((None, SEQ, bl), lambda i: (1, 0, i))
    ospec = pl.BlockSpec((1, bl), lambda i: (0, i))
    return pl.pallas_call(
        body, name=name, grid=(NS // bl,), in_specs=[re_spec, im_spec] * 2, out_specs=[ospec, ospec],
        out_shape=[jax.ShapeDtypeStruct((1, NS), F32)] * 2,
        compiler_params=_params("arbitrary"),
    )(a, a, s, s)


def _scores(qb, kb, prev):
    s = lax.dot_general(qb, kb, (((1,), (1,)), ((), ())), preferred_element_type=F32) * (HD ** -0.5)
    row = lax.broadcasted_iota(jnp.int32, (ABLK, ABLK), 0)
    col = lax.broadcasted_iota(jnp.int32, (ABLK, ABLK), 1)
    return jnp.where((col >= row) if prev else (col <= row), s, -1e30)


def _block_rows(dil, r, b):
    if dil == 1:
        return pl.ds(pl.multiple_of(b * ABLK, ABLK), ABLK)
    return pl.ds(r + dil * ABLK * b, ABLK, stride=dil)


def _group_blocks(dil):
    nb = SEQ // dil // ABLK
    shift = nb.bit_length() - 1
    return nb, (lambda idx: (idx >> shift, idx & (nb - 1)))


def _qkv_specs(j_of):
    return [pl.BlockSpec((SEQ, HD), functools.partial(lambda j, c: (0, c + j_of(j)), c=(cb + g) * 4))
            for g in range(3) for cb in (CB_Q, CB_K, CB_V)]


def _attention_fwd(name, proj):
    def body(*refs):
        qkv, z_ref = refs[:9], refs[9]
        y_ref, ya_ref, l_ref = refs[10:13]
        accs, maxs, dens = refs[13:16], refs[16:19], refs[19:22]
        for g, dil in enumerate(DILATIONS):
            q_ref, k_ref, v_ref = qkv[3 * g:3 * g + 3]
            nb, where = _group_blocks(dil)

            def step(t, c, g=g, dil=dil, nb=nb, where=where, q_ref=q_ref, k_ref=k_ref, v_ref=v_ref):
                two = range(ATTN_PAIR)
                rb = [where(t + i * (SEQ // ABLK // ATTN_PAIR)) for i in two]
                rows = [_block_rows(dil, r, b) for r, b in rb]
                qb = [q_ref[rows[i], :].astype(BF16) for i in two]
                s_c = [_scores(qb[i], k_ref[rows[i], :].astype(BF16), False) for i in two]
                if nb > 1:
                    prev = [_block_rows(dil, r, jnp.maximum(b - 1, 0)) for r, b in rb]
                    s_p = [jnp.where(rb[i][1] > 0, _scores(qb[i], k_ref[prev[i], :].astype(BF16), True), -1e30)
                           for i in two]
                m = [jnp.max(s_c[i], axis=-1, keepdims=True) for i in two]
                if nb > 1:
                    m = [jnp.maximum(m[i], jnp.max(s_p[i], axis=-1, keepdims=True)) for i in two]
                p_c = [jnp.exp(s_c[i] - m[i]) for i in two]
                den = [jnp.sum(p_c[i], axis=-1, keepdims=True) for i in two]
                acc = [jnp.dot(p_c[i].astype(BF16), v_ref[rows[i], :].astype(BF16), preferred_element_type=F32)
                       for i in two]
                if nb > 1:
                    p_p = [jnp.exp(s_p[i] - m[i]) for i in two]
                    den = [den[i] + jnp.sum(p_p[i], axis=-1, keepdims=True) for i in two]
                    acc = [acc[i] + jnp.dot(p_p[i].astype(BF16), v_ref[prev[i], :].astype(BF16),
                                            preferred_element_type=F32) for i in two]
                for i in two:
                    accs[g][rows[i], :] = acc[i]
                    maxs[g][rows[i], :] = jnp.broadcast_to(m[i], (ABLK, HD))
                    dens[g][rows[i], :] = jnp.broadcast_to(den[i], (ABLK, HD))
                return c

            lax.fori_loop(0, SEQ // ABLK // ATTN_PAIR, step, 0)
        top = jnp.maximum(jnp.maximum(maxs[0][...], maxs[1][...]), maxs[2][...])
        den = jnp.zeros((SEQ, HD), F32)
        y = jnp.zeros((SEQ, HD), F32)
        for g in range(3):
            wgt = jnp.exp(maxs[g][...] - top)
            den = den + wgt * dens[g][...]
            y = y + wgt * accs[g][...]
        y = y / den
        y_ref[...] = y
        ya_ref[...] = (y * _silu(z_ref[...])).astype(ya_ref.dtype)
        l_ref[...] = top + jnp.log(den)

    ospec = pl.BlockSpec((SEQ, HD), lambda j: (0, j))
    return pl.pallas_call(
        body, name=name, grid=(AW // HD,),
        in_specs=_qkv_specs(lambda j: j) + [pl.BlockSpec((SEQ, HD), lambda j: (0, CB_ZA * 4 + j))],
        out_specs=[ospec, ospec, ospec],
        out_shape=[jax.ShapeDtypeStruct((SEQ, AW), F32), jax.ShapeDtypeStruct((SEQ, AW), BF16),
                   jax.ShapeDtypeStruct((SEQ, AW), F32)],
        scratch_shapes=[pltpu.VMEM((SEQ, HD), F32)] * 9,
        compiler_params=_params("parallel"),
    )(*([proj] * 10))


def _attention_bwd(name, proj, dya, y, lse):
    def tn(a, b_):
        return lax.dot_general(a, b_, (((0,), (0,)), ((), ())), preferred_element_type=F32)

    def nt(a, b_):
        return lax.dot_general(a, b_, (((1,), (1,)), ((), ())), preferred_element_type=F32)

    def body(*refs):
        qkv, z_ref, dya_ref, y_ref, l_ref = refs[:9], refs[9], refs[10], refs[11], refs[12]
        outs, dza_ref = refs[13:22], refs[22]
        dy_s, dsum_s, dq_s, dk_own, dv_own, dk_prev, dv_prev = refs[23:]
        _, vjp = jax.vjp(lambda y_, z_: y_ * _silu(z_), y_ref[...], z_ref[...])
        dy, dz = vjp(dya_ref[...])
        dza_ref[...] = dz.astype(dza_ref.dtype)
        dy_s[...] = dy
        dsum_s[...] = jnp.broadcast_to(jnp.sum(dy * y_ref[...], axis=-1, keepdims=True), (SEQ, HD))
        for g, dil in enumerate(DILATIONS):
            q_ref, k_ref, v_ref = qkv[3 * g:3 * g + 3]
            nb, where = _group_blocks(dil)
            if nb > 1:
                dk_prev[...] = jnp.zeros(dk_prev.shape, F32)
                dv_prev[...] = jnp.zeros(dv_prev.shape, F32)

            def step(t, c, dil=dil, nb=nb, where=where, q_ref=q_ref, k_ref=k_ref, v_ref=v_ref):
                rb = [where(t + i * (SEQ // ABLK // ATTN_PAIR)) for i in range(ATTN_PAIR)]
                sides = []
                for r, b in rb:
                    rows = _block_rows(dil, r, b)
                    own = dict(b=b, qrows=rows, krows=rows, prev=False, dk=dk_own, dv=dv_own,
                               q=q_ref[rows, :].astype(BF16), dy=dy_s[rows, :].astype(BF16))
                    sides.append(own)
                    if nb > 1:
                        sides.append(dict(own, krows=_block_rows(dil, r, jnp.maximum(b - 1, 0)), prev=True,
                                          dk=dk_prev, dv=dv_prev))
                for s_ in sides:
                    s_["k"] = k_ref[s_["krows"], :].astype(BF16)
                    s_["v"] = v_ref[s_["krows"], :].astype(BF16)
                for s_ in sides:
                    sc = _scores(s_["q"], s_["k"], s_["prev"])
                    s_["s"] = jnp.where(s_["b"] > 0, sc, -1e30) if s_["prev"] else sc
                    s_["dp"] = nt(s_["dy"], s_["v"])
                for s_ in sides:
                    p = jnp.exp(s_["s"] - l_ref[s_["qrows"], :])
                    s_["p"] = p.astype(BF16)
                    s_["ds"] = (p * (s_["dp"] - dsum_s[s_["qrows"], :]) * (HD ** -0.5)).astype(BF16)
                for s_ in sides:
                    s_["dk"][s_["krows"], :] = tn(s_["ds"], s_["q"])
                    s_["dv"][s_["krows"], :] = tn(s_["p"], s_["dy"])
                    s_["dq"] = jnp.dot(s_["ds"], s_["k"], preferred_element_type=F32)
                per = len(sides) // ATTN_PAIR
                for i in range(ATTN_PAIR):
                    dq = sides[i * per]["dq"]
                    if per > 1:
                        dq = dq + sides[i * per + 1]["dq"]
                    dq_s[sides[i * per]["qrows"], :] = dq
                return c

            lax.fori_loop(0, SEQ // ABLK // ATTN_PAIR, step, 0)
            dq_ref, dk_ref, dv_ref = outs[3 * g:3 * g + 3]
            dq_ref[...] = dq_s[...].astype(dq_ref.dtype)
            if nb > 1:
                dk_ref[...] = (dk_own[...] + dk_prev[...]).astype(dk_ref.dtype)
                dv_ref[...] = (dv_own[...] + dv_prev[...]).astype(dv_ref.dtype)
            else:
                dk_ref[...] = dk_own[...].astype(dk_ref.dtype)
                dv_ref[...] = dv_own[...].astype(dv_ref.dtype)

    ospec = pl.BlockSpec((SEQ, HD), lambda j: (0, j))
    outs = pl.pallas_call(
        body, name=name, grid=(AW // HD,),
        in_specs=_qkv_specs(lambda j: j) + [pl.BlockSpec((SEQ, HD), lambda j: (0, CB_ZA * 4 + j))] + [ospec] * 3,
        out_specs=[ospec] * 10, out_shape=[jax.ShapeDtypeStruct((SEQ, AW), BF16)] * 10,
        scratch_shapes=[pltpu.VMEM((SEQ, HD), F32)] * 7,
        compiler_params=_params("parallel"),
    )(*([proj] * 10), dya, y, lse)
    return outs[:9], outs[9]


def _rms(x, g):
    return x * lax.rsqrt(jnp.mean(x * x, axis=-1, keepdims=True) + RMS_EPS) * g


def _sig(x):
    return 1.0 / (1.0 + jnp.exp(-x))


def _silu(x):
    return x * _sig(x)


def _gelu(x):
    return 0.5 * x * (1.0 + jnp.tanh(math.sqrt(2.0 / math.pi) * (x + 0.044715 * (x * x * x))))


def _y1_fn(y0p, u, dskip):
    return _gelu(y0p + dskip * u)


def _ys_fn(y1, t, z, bglu):
    return y1 * _sig(t + bglu) * _silu(z)


def _merge_fn(ms, ma, gs, ga):
    return _sig(gs) * ms + _sig(ga) * ma


def _colsum(v):
    return jnp.sum(v, axis=0, keepdims=True)


def _lam_fn(lre, lim, ldt):
    a = jnp.minimum(lre, -1e-4)
    dt = jnp.exp(ldt)
    mag = jnp.exp(a * dt)
    ar = mag * jnp.cos(lim * dt)
    ai = mag * jnp.sin(lim * dt)
    den = a * a + lim * lim
    cr = ((ar - 1.0) * a + ai * lim) / den
    ci = (ai * a - (ar - 1.0) * lim) / den
    return ar, ai, cr, ci


def _bbar_fn(cr, ci, bre, bim):
    return cr * bre - ci * bim, cr * bim + ci * bre


def _same_group(rows, a, cols, b):
    r = lax.broadcasted_iota(jnp.int32, (rows, cols), 0) >> (a.bit_length() - 1)
    c = lax.broadcasted_iota(jnp.int32, (rows, cols), 1) >> (b.bit_length() - 1)
    return r == c


def _expand(name, blocks, signs, a, b, dtype):
    rows, cols = NGRP * a, NGRP * b
    assert a & (a - 1) == 0 and b & (b - 1) == 0

    def body(*refs):
        o_ref = refs[-1]
        tile = (lax.broadcasted_iota(jnp.int32, (b, cols), 1) & (b - 1)
                == lax.broadcasted_iota(jnp.int32, (b, cols), 0)).astype(F32)
        keep = _same_group(rows, a, cols, b)
        for i, (ref, sign) in enumerate(zip(refs[:-1], signs)):
            spread = jnp.dot(ref[...], tile, preferred_element_type=F32, precision=lax.Precision.HIGHEST)
            o_ref[i * rows:(i + 1) * rows, :] = jnp.where(keep, sign * spread, 0.0).astype(o_ref.dtype)

    return pl.pallas_call(body, name=name, out_shape=jax.ShapeDtypeStruct((len(blocks) * rows, cols), dtype),
                          compiler_params=pltpu.CompilerParams(vmem_limit_bytes=VMEM_LIMIT))(*blocks)


def _extract(name, m, a, b, at=(0, 0)):
    rows, cols = NGRP * a, NGRP * b
    assert a & (a - 1) == 0 and b & (b - 1) == 0

    def body(m_ref, o_ref):
        tile = (lax.broadcasted_iota(jnp.int32, (cols, b), 0) & (b - 1)
                == lax.broadcasted_iota(jnp.int32, (cols, b), 1)).astype(F32)
        kept = jnp.where(_same_group(rows, a, cols, b), m_ref[...], 0.0)
        o_ref[...] = jnp.dot(kept, tile, preferred_element_type=F32, precision=lax.Precision.HIGHEST)

    return pl.pallas_call(body, name=name, grid=(1,), in_specs=[pl.BlockSpec((rows, cols), lambda i: at)],
                          out_specs=pl.BlockSpec((rows, b), lambda i: (0, 0)),
                          out_shape=jax.ShapeDtypeStruct((rows, b), F32), compiler_params=_params("arbitrary"))(m)


def _layer_fwd(l, x, w, sp):
    tag = f"l{l}_"
    g1 = sp["pre_norm_g"].reshape(1, DM)
    (h,) = _ew(tag + "rms1", lambda x_, g: (_rms(x_, g),), SEQ, 256, [_ri(x)], [g1], [(DM, BF16)])
    hv = jnp.stack([h, _to_chunked(h)])
    win = w["w_in"]
    proj = _mm(tag + "proj", hv, win, "nn", SEQ, NCOL, DM, SEQ, 512, 1024, F32,
               a_spec=pl.BlockSpec((None, SEQ, 1024), lambda i, j, k: (_row_order(j), 0, 0)),
               b_spec=pl.BlockSpec((None, 1024, 512), lambda i, j, k: (j // 2, 0, j % 2)))

    ar, ai, cr, ci = _ew(tag + "lam", _lam_fn, NGRP, NGRP,
                         [_ri(sp["lambda_re"]), _ri(sp["lambda_im"]), _ri(sp["log_dt"].reshape(NGRP, 1))], [],
                         [(NST, F32)] * 4)
    bre = sp["b_re"].reshape(NS, GCH)
    bim = sp["b_im"].reshape(NS, GCH)
    bbr, bbi = _ew(tag + "bbar", _bbar_fn, NS, NS, [_ri(cr.reshape(NS, 1)), _ri(ci.reshape(NS, 1)), _ri(bre), _ri(bim)],
                   [], [(GCH, F32)] * 2)
    wdt = _expand(tag + "wdt", [bbr, bbi], [1.0, 1.0], NST, GCH, BF16)
    cmt = _expand(tag + "cmt", [sp["c_re"].reshape(SW, NST), sp["c_im"].reshape(SW, NST)], [1.0, -1.0], GCH, NST, BF16)
    s, y0p = _s5_forward(tag + "s5", proj, wdt, cmt, ar.reshape(1, NS), ai.reshape(1, NS))
    dskip = sp["d_skip"].reshape(1, SW)
    (y1,) = _ew(tag + "y1", lambda a, u, d: (_y1_fn(a, u, d),), SEQ, 256,
                [_ri(y0p), _ri(proj, SW, CB_U)], [dskip], [(SW, F32)])
    t = _mm(tag + "glu", y1, w["w_glu"], "nn", SEQ, SW, SW, 1024, 512, 512, F32)
    bglu = sp["b_glu"].reshape(1, SW)
    (ys_c,) = _ew(tag + "ys", lambda y1_, t_, z, b_: (_ys_fn(y1_, t_, z, b_),), SEQ, 256,
                  [_ri(y1), _ri(t), _ri(proj, SW, CB_ZS)], [bglu], [(SW, BF16)])
    ys = _from_chunked(ys_c)

    ypre, ya, lse = _attention_fwd(tag + "attn", proj)

    ms = _mm(tag + "branch_s", ys, w["w_branch_s"], "nn", SEQ, DM, SW, 1024, 1024, 512, F32)
    ma = _mm(tag + "branch_a", ya, w["w_branch_a"], "nn", SEQ, DM, AW, 1024, 1024, 512, F32)
    (merged,) = _ew(tag + "merge", lambda a, b_, c, d: (_merge_fn(a, b_, c, d),), SEQ, 256,
                    [_ri(ms), _ri(ma), _ri(proj, DM, CB_GS), _ri(proj, DM, CB_GA)], [], [(DM, BF16)])
    out = _mm(tag + "out", merged, w["w_out"], "nn", SEQ, DM, DM, 1024, 1024, 1024, F32)
    g2 = sp["post_norm_g"].reshape(1, DM)
    (x_new,) = _ew(tag + "post", lambda x_, o, g: (x_ + _rms(o, g),), SEQ, 256, [_ri(x), _ri(out)], [g2], [(DM, F32)])
    res = dict(x=x, hv=hv, proj=proj, ar=ar, ai=ai, cr=cr, ci=ci, wdt=wdt, cmt=cmt, s=s, y0p=y0p,
               y1=y1, t=t, ys=ys, ya=ya, ypre=ypre, lse=lse, ms=ms, ma=ma, merged=merged, out=out)
    return x_new, res


def _layer_bwd(l, dxn, r, w, sp):
    tag = f"l{l}b_"
    proj = r["proj"]
    g1 = sp["pre_norm_g"].reshape(1, DM)
    g2 = sp["post_norm_g"].reshape(1, DM)
    dskip = sp["d_skip"].reshape(1, SW)
    bglu = sp["b_glu"].reshape(1, SW)

    def post_b(d, o, g):
        _, vjp = jax.vjp(_rms, o, g)
        do, dg = vjp(d)
        return do, dg

    d_out, dg2 = _ew(tag + "post", post_b, SEQ, 256, [_ri(dxn), _ri(r["out"])], [g2], [(DM, BF16)], [DM])
    dw_out = _mm(tag + "dw_out", r["merged"], d_out, "tn", DM, DM, SEQ, 1024, 1024, SEQ, BF16)
    dmerged = _mm(tag + "dmerged", d_out, w["w_out"], "nt", SEQ, DM, DM, 1024, 1024, 1024, F32)

    def merge_b(d, ms, ma, gs, ga):
        _, vjp = jax.vjp(_merge_fn, ms, ma, gs, ga)
        return vjp(d)

    dms, dma, dgs, dga = _ew(tag + "merge", merge_b, SEQ, 256,
                             [_ri(dmerged), _ri(r["ms"]), _ri(r["ma"]), _ri(proj, DM, CB_GS), _ri(proj, DM, CB_GA)],
                             [], [(DM, BF16)] * 4)
    dw_bs = _mm(tag + "dw_bs", r["ys"], dms, "tn", SW, DM, SEQ, 512, 1024, SEQ, BF16)
    dw_ba = _mm(tag + "dw_ba", r["ya"], dma, "tn", AW, DM, SEQ, 512, 1024, SEQ, BF16)
    dys = _mm(tag + "dys", dms, w["w_branch_s"], "nt", SEQ, SW, DM, 1024, 512, 1024, F32)
    dya = _mm(tag + "dya", dma, w["w_branch_a"], "nt", SEQ, AW, DM, 1024, 512, 1024, F32)

    dqkv, dza = _attention_bwd(tag + "attn", proj, dya, r["ypre"], r["lse"])

    def ys_b(d, y1, t, z, b_):
        _, vjp = jax.vjp(_ys_fn, y1, t, z, b_)
        dy1, dt, dz, _ = vjp(d)
        return dy1, dt, dz, _colsum(dt)

    dy1a, dt, dzs, dbglu = _ew(tag + "ys", ys_b, SEQ, 256,
                               [_ri(_to_chunked(dys)), _ri(r["y1"]), _ri(r["t"]), _ri(proj, SW, CB_ZS)],
                               [bglu], [(SW, F32), (SW, BF16), (SW, BF16)], [SW])
    dw_glu = _mm(tag + "dw_glu", r["y1"], dt, "tn", SW, SW, SEQ, 512, 512, SEQ, BF16)
    dy1b = _mm(tag + "dy1b", dt, w["w_glu"], "nt", SEQ, SW, SW, 1024, 512, 512, F32)

    def y1_b(da, db, y0p, u, d_):
        _, vjp = jax.vjp(_y1_fn, y0p, u, d_)
        dy0, du, dd = vjp(da + db)
        return dy0, du, dd

    dy0, du_skip, ddskip = _ew(tag + "y1", y1_b, SEQ, 256,
                               [_ri(dy1a), _ri(dy1b), _ri(r["y0p"]), _ri(proj, SW, CB_U)], [dskip],
                               [(SW, BF16), (SW, F32)], [SW])
    dcmt = _mm(tag + "dcmt", dy0, r["s"], "tn", SW, 2 * NS, SEQ, 512, 1024, SEQ, F32,
               b_spec=pl.BlockSpec((None, SEQ, 1024), lambda i, j, k: (j // 2, 0, j % 2)))
    adj = _s5_adjoint(tag + "adjoint", dy0, r["cmt"], r["ar"].reshape(1, NS), r["ai"].reshape(1, NS))
    dlr, dli = _dlam(tag + "dlam", adj, r["s"])
    dwdt = _mm(tag + "dwdt", adj, proj, "tn", 2 * NS, SW, SEQ, 1024, 512, SEQ, F32,
               a_spec=pl.BlockSpec((None, SEQ, 1024), lambda i, j, k: (i // 2, 0, i % 2)),
               b_spec=pl.BlockSpec((SEQ, SW), lambda i, j, k: (0, CB_U)))
    du_s = _mm(tag + "du_s", adj, r["wdt"], "nn", SEQ, SW, 2 * NS, 1024, 512, 1024, F32, a_spec=_plane_spec(1024, 1024))
    (du,) = _ew(tag + "du", lambda a, c: (a + c,), SEQ, 256, [_ri(du_s), _ri(du_skip)], [], [(SW, BF16)])

    dbbr = _extract(tag + "dbbr", dwdt, NST, GCH, (0, 0))
    dbbi = _extract(tag + "dbbi", dwdt, NST, GCH, (1, 0))
    bre = sp["b_re"].reshape(NS, GCH)
    bim = sp["b_im"].reshape(NS, GCH)

    def bbar_b(cr, ci, br_, bi_, dr, di):
        _, vjp = jax.vjp(_bbar_fn, cr, ci, br_, bi_)
        return vjp((dr, di))

    dcr, dci, dbre, dbim = _ew(tag + "bbar", bbar_b, NS, NS,
                               [_ri(r["cr"].reshape(NS, 1)), _ri(r["ci"].reshape(NS, 1)), _ri(bre), _ri(bim),
                                _ri(dbbr), _ri(dbbi)], [], [(1, F32), (1, F32), (GCH, F32), (GCH, F32)])

    def lam_b(lre, lim, ldt, dar, dai, dcr_, dci_):
        _, vjp = jax.vjp(_lam_fn, lre, lim, ldt)
        return vjp((dar, dai, dcr_, dci_))

    dlre, dlim, dldt = _ew(tag + "lam", lam_b, NGRP, NGRP,
                           [_ri(sp["lambda_re"]), _ri(sp["lambda_im"]), _ri(sp["log_dt"].reshape(NGRP, 1)),
                            _ri(dlr.reshape(NGRP, NST)), _ri(dli.reshape(NGRP, NST)),
                            _ri(dcr.reshape(NGRP, NST)), _ri(dci.reshape(NGRP, NST))], [],
                           [(NST, F32), (NST, F32), (1, F32)])
    dc_re = _extract(tag + "dc_re", dcmt, GCH, NST, (0, 0)).reshape(NGRP, GCH, NST)
    dc_im = -_extract(tag + "dc_im", dcmt, GCH, NST, (0, 1)).reshape(NGRP, GCH, NST)

    dq, dk, dv = ([dqkv[3 * g + i] for g in range(3)] for i in range(3))
    dproj = jnp.concatenate([du, dzs, *dq, *dk, *dv, dza, dgs, dga], axis=1)
    dw_in = _mm(tag + "dw_in", r["hv"], dproj, "tn", DM, NCOL, SEQ, 1024, 512, SEQ, BF16,
                a_spec=pl.BlockSpec((None, SEQ, 1024), lambda i, j, k: (_row_order(j), 0, 0)),
                o_spec=pl.BlockSpec((None, 1024, 512), lambda i, j, k: (j // 2, 0, j % 2)), out_shape=(NDEV, DM, DM))

    dh_time = _mm(tag + "dh_time", dproj, w["w_in"], "nt", SEQ, DM, NCOL - DM, SEQ, 1024, 1024, F32,
                  a_spec=pl.BlockSpec((SEQ, 1024), lambda i, j, k: (0, 1 + k)),
                  b_spec=pl.BlockSpec((None, 1024, 1024), lambda i, j, k: (1 + k, 0, 0)))
    dh_chunked = _mm(tag + "dh_chunked", dproj, w["w_in"], "nt", SEQ, DM, DM, SEQ, 1024, 1024, F32,
                     a_spec=pl.BlockSpec((SEQ, 1024), lambda i, j, k: (0, 0)),
                     b_spec=pl.BlockSpec((None, 1024, 1024), lambda i, j, k: (0, 0, 0)))
    dh = [dh_time, _from_chunked(dh_chunked)]

    def pre_b(d, dh0, dh1, x_, g):
        _, vjp = jax.vjp(_rms, x_, g)
        dx_, dg = vjp(dh0 + dh1)
        return d + dx_, dg

    dx, dg1 = _ew(tag + "pre", pre_b, SEQ, 256, [_ri(dxn)] + [_ri(t_) for t_ in dh] + [_ri(r["x"])], [g1],
                  [(DM, F32)], [DM])

    big = dict(w_in=dw_in, w_glu=dw_glu.reshape(NDEV, SW // NDEV, SW),
               w_branch_s=dw_bs.reshape(SW, NDEV, DM // NDEV).transpose(1, 0, 2),
               w_branch_a=dw_ba.reshape(AW, NDEV, DM // NDEV).transpose(1, 0, 2),
               w_out=dw_out.reshape(NDEV, DM // NDEV, DM))
    small = dict(pre_norm_g=dg1.reshape(DM), lambda_re=dlre, lambda_im=dlim, log_dt=dldt.reshape(NGRP),
                 b_re=dbre.reshape(NGRP, NST, GCH), b_im=dbim.reshape(NGRP, NST, GCH), c_re=dc_re, c_im=dc_im,
                 d_skip=ddskip.reshape(SW), b_glu=dbglu.reshape(SW), post_norm_g=dg2.reshape(DM))
    return dx, big, small


_HBM = pl.BlockSpec(memory_space=pltpu.HBM)
_SEM = pl.BlockSpec(memory_space=pltpu.SEMAPHORE)
_EFFECT = pltpu.SideEffectType.DATAFLOW_SIDE_EFFECTING


def _remote_copies(srcs, dsts, send_sems, recv_sems, gather):
    x, y, c = lax.axis_index("x"), lax.axis_index("y"), lax.axis_index("c")
    me = 4 * x + 2 * y + c
    copies = []
    for i in range(len(srcs)):
        for k in range(1, NDEV):
            peer = (x ^ (k >> 2), y ^ ((k >> 1) & 1), c ^ (k & 1))
            src = srcs[i] if gather[i] else srcs[i].at[me ^ k]
            copies.append(pltpu.make_async_remote_copy(
                src_ref=src, dst_ref=dsts[i].at[me], send_sem=send_sems[i], recv_sem=recv_sems[i],
                device_id=peer, device_id_type=pl.DeviceIdType.MESH))
    return copies


def _all_seven(dst, send_sem, recv_sem):
    seven = dst.at[pl.ds(0, NDEV - 1)]
    me = (lax.axis_index("x"), lax.axis_index("y"), lax.axis_index("c"))
    return pltpu.make_async_remote_copy(src_ref=seven, dst_ref=seven, send_sem=send_sem, recv_sem=recv_sem,
                                        device_id=me, device_id_type=pl.DeviceIdType.MESH)


def _own_slabs(name, arrs, gather, after):
    n = len(arrs)
    me = (4 * lax.axis_index("x") + 2 * lax.axis_index("y") + lax.axis_index("c")).astype(jnp.int32).reshape(1)

    def body(me_ref, *refs):
        for src, dst in zip(refs[:n], refs[n + 1:]):
            dst[...] = src[...]

    def zeros(k):
        return (0,) * k

    in_specs, out_specs, out_shape = [], [], []
    for a, g in zip(arrs, gather):
        slab = a.shape if g else a.shape[1:]
        nd = len(slab)
        if g:
            in_specs.append(pl.BlockSpec(slab, functools.partial(lambda i, me_ref, nd: zeros(nd), nd=nd)))
        else:
            in_specs.append(pl.BlockSpec((None,) + slab, functools.partial(lambda i, me_ref, nd: (me_ref[0],) + zeros(nd), nd=nd)))
        out_specs.append(pl.BlockSpec((None,) + slab, functools.partial(lambda i, me_ref, nd: (me_ref[0],) + zeros(nd), nd=nd)))
        out_shape.append(jax.ShapeDtypeStruct((NDEV,) + slab, a.dtype))
    in_specs.append(pl.BlockSpec(memory_space=pl.ANY))
    return pl.pallas_call(
        body, name=name, out_shape=out_shape,
        grid_spec=pltpu.PrefetchScalarGridSpec(num_scalar_prefetch=1, grid=(1,), in_specs=in_specs, out_specs=out_specs),
        compiler_params=_params("arbitrary"),
    )(me, *arrs, after)


def _exchange_start(name, arrs, gather, after):
    n = len(arrs)
    lands = _own_slabs(name + "_own", arrs, gather, after)

    def body(*refs):
        srcs, dsts = refs[:n], refs[n:2 * n]
        send_sems, recv_sems = refs[2 * n:3 * n], refs[3 * n:4 * n]
        token = refs[-1]
        for cp in _remote_copies(srcs, dsts, send_sems, recv_sems, gather):
            cp.start()
        token[...] = jnp.zeros(token.shape, token.dtype)

    thru = [pltpu.HBM(a.shape, a.dtype) for a in list(arrs) + list(lands)]
    outs = pl.pallas_call(
        body, name=name,
        out_shape=(*[pltpu.SemaphoreType.DMA(())] * (2 * n), *thru, jax.ShapeDtypeStruct((8, 128), F32)),
        in_specs=[_HBM] * (2 * n),
        out_specs=(*[_SEM] * (2 * n), *[_HBM] * (2 * n), pl.BlockSpec(memory_space=pltpu.VMEM)),
        input_output_aliases={i: 2 * n + i for i in range(2 * n)},
        compiler_params=pltpu.CompilerParams(has_side_effects=_EFFECT),
    )(*[pltpu.with_memory_space_constraint(a, pltpu.HBM) for a in list(arrs) + list(lands)])
    return dict(send=outs[:n], recv=outs[n:2 * n], srcs=outs[2 * n:3 * n], lands=outs[3 * n:4 * n], token=outs[-1],
                gather=gather)


def _exchange_wait(name, started, after):
    n = len(started["srcs"])

    def body(*refs):
        dsts = refs[n:2 * n]
        send_sems, recv_sems = refs[2 * n:3 * n], refs[3 * n:4 * n]
        for i in range(n):
            cp = _all_seven(dsts[i], send_sems[i], recv_sems[i])
            cp.wait_send()
            cp.wait_recv()

    bufs = list(started["srcs"]) + list(started["lands"])
    outs = pl.pallas_call(
        body, name=name, out_shape=tuple(pltpu.HBM(a.shape, a.dtype) for a in bufs),
        in_specs=[_HBM] * (2 * n) + [_SEM] * (2 * n) + [pl.BlockSpec(memory_space=pl.ANY)], out_specs=(_HBM,) * (2 * n),
        input_output_aliases={i: i for i in range(2 * n)},
        compiler_params=pltpu.CompilerParams(has_side_effects=_EFFECT),
    )(*bufs, *started["send"], *started["recv"], after)
    return outs[n:]


def _sum_in_order(parts):
    g = parts[0].astype(F32)
    for p in parts[1:]:
        g = g + p.astype(F32)
    return g


def _adamw(name, parts, nparts, w, m, v, br):
    rows, cols = w.shape

    def fn(*a):
        g = _sum_in_order(a[:nparts])
        w_, m_, v_ = a[nparts:]
        m2 = B1 * m_ + (1.0 - B1) * g
        v2 = B2 * v_ + (1.0 - B2) * (g * g)
        m_hat = m2 / (1.0 - B1 ** STEP)
        v_hat = v2 / (1.0 - B2 ** STEP)
        delta = -LR * (m_hat / (jnp.sqrt(v_hat) + ADAM_EPS) + WD * w_)
        return g, delta, m2, v2

    ins = [_ri(parts, cols, 0, d * (rows // br)) for d in range(nparts)] + [_ri(w), _ri(m), _ri(v)]
    return _ew(name, fn, rows, br, ins, [], [(cols, F32)] * 4)


SMALL = ("pre_norm_g", "lambda_re", "lambda_im", "log_dt", "b_re", "b_im", "c_re", "c_im", "d_skip", "b_glu",
         "post_norm_g")
BIG = ("w_in", "w_glu", "w_branch_s", "w_branch_a", "w_out")
WEIGHTS = ("pre_norm_g", "w_in", "lambda_re", "lambda_im", "log_dt", "b_re", "b_im", "c_re", "c_im", "d_skip",
           "w_glu", "b_glu", "w_branch_s", "w_branch_a", "w_out", "post_norm_g")
PACK_COLS = 1024
PACK_BR = 136


def _pack_layer(d):
    pieces = [d[k].astype(F32).reshape(-1) for k in SMALL]
    used = sum(p.shape[0] for p in pieces)
    assert used <= PACK_BR * PACK_COLS
    return jnp.concatenate(pieces + [jnp.zeros((PACK_BR * PACK_COLS - used,), F32)]).reshape(PACK_BR, PACK_COLS)


def _unpack(p, like):
    flat = p.reshape(DEPTH, PACK_BR * PACK_COLS)
    out, off = {}, 0
    for k in SMALL:
        n = like[k].size // DEPTH
        out[k] = flat[:, off:off + n].reshape(like[k].shape)
        off += n
    return out


def _local_step(x, target, small, weights_of, grads_done):
    res, ws = [], []
    for l in range(DEPTH):
        w_l, tok = weights_of(l, x)
        sp = {k: small[k][l] for k in SMALL}
        if tok is not None:
            sp["pre_norm_g"] = sp["pre_norm_g"] + tok[0, 0]
        x, r = _layer_fwd(l, x, w_l, sp)
        res.append(r)
        ws.append(w_l)

    def loss_fn(y, t):
        e = y - t
        return e * (1.0 / DM), jnp.sum(_colsum(0.5 * e * e * (1.0 / DM)), axis=1, keepdims=True)

    dx, loss = _ew("loss", loss_fn, SEQ, 256, [_ri(x), _ri(target)], [], [(DM, F32)], [1])
    tok = None
    for l in reversed(range(DEPTH)):
        sp = {k: small[k][l] for k in SMALL}
        if tok is not None:
            sp["post_norm_g"] = sp["post_norm_g"] + tok[0, 0]
        dx, big, sm = _layer_bwd(l, dx, res[l], ws[l], sp)
        tok = grads_done(l, big, sm)
    return loss.reshape(()), dx


def _full_weights(gathered):
    g = gathered
    return dict(
        w_in=g["w_in"],
        w_glu=g["w_glu"].reshape(SW, SW),
        w_branch_s=g["w_branch_s"].transpose(1, 0, 2).reshape(SW, DM),
        w_branch_a=g["w_branch_a"].transpose(1, 0, 2).reshape(AW, DM),
        w_out=g["w_out"].reshape(DM, DM),
    )


def kernel(x, pre_norm_g, w_in, lambda_re, lambda_im, log_dt, b_re, b_im, c_re, c_im, d_skip, w_glu, b_glu, w_branch_s, w_branch_a, w_out, post_norm_g, loss_target, m_pre_norm_g, m_w_in, m_lambda_re, m_lambda_im, m_log_dt, m_b_re, m_b_im, m_c_re, m_c_im, m_d_skip, m_w_glu, m_b_glu, m_w_branch_s, m_w_branch_a, m_w_out, m_post_norm_g, v_pre_norm_g, v_w_in, v_lambda_re, v_lambda_im, v_log_dt, v_b_re, v_b_im, v_c_re, v_c_im, v_d_skip, v_w_glu, v_b_glu, v_w_branch_s, v_w_branch_a, v_w_out, v_post_norm_g):
    wts = dict(pre_norm_g=pre_norm_g, w_in=w_in, lambda_re=lambda_re, lambda_im=lambda_im, log_dt=log_dt, b_re=b_re,
               b_im=b_im, c_re=c_re, c_im=c_im, d_skip=d_skip, w_glu=w_glu, b_glu=b_glu, w_branch_s=w_branch_s,
               w_branch_a=w_branch_a, w_out=w_out, post_norm_g=post_norm_g)
    mom = dict(pre_norm_g=m_pre_norm_g, w_in=m_w_in, lambda_re=m_lambda_re, lambda_im=m_lambda_im, log_dt=m_log_dt,
               b_re=m_b_re, b_im=m_b_im, c_re=m_c_re, c_im=m_c_im, d_skip=m_d_skip, w_glu=m_w_glu, b_glu=m_b_glu,
               w_branch_s=m_w_branch_s, w_branch_a=m_w_branch_a, w_out=m_w_out, post_norm_g=m_post_norm_g)
    var = dict(pre_norm_g=v_pre_norm_g, w_in=v_w_in, lambda_re=v_lambda_re, lambda_im=v_lambda_im, log_dt=v_log_dt,
               b_re=v_b_re, b_im=v_b_im, c_re=v_c_re, c_im=v_c_im, d_skip=v_d_skip, w_glu=v_w_glu, b_glu=v_b_glu,
               w_branch_s=v_w_branch_s, w_branch_a=v_w_branch_a, w_out=v_w_out, post_norm_g=v_post_norm_g)

    def gather_start(l, after):
        return _exchange_start(f"gather_start{l}", [wts[k][l].astype(BF16) for k in BIG], [True] * len(BIG), after)

    gathering = {0: gather_start(0, x)}
    sending = {}

    def weights_of(l, x_l):
        gathered = _exchange_wait(f"gather_wait{l}", gathering[l], x_l)
        tok = None
        if l + 1 < DEPTH:
            gathering[l + 1] = gather_start(l + 1, gathered[0])
            tok = gathering[l + 1]["token"]
        return _full_weights(dict(zip(BIG, gathered))), tok

    def grads_done(l, big, sm):
        sending[l] = _exchange_start(f"grads_start{l}", [big[k] for k in BIG] + [_pack_layer(sm)],
                                     [False] * len(BIG) + [True], big["w_in"])
        return sending[l]["token"]

    loss, dx = _local_step(x[0], loss_target[0], wts, weights_of, grads_done)
    loss = lax.psum(loss, ("x", "y", "c"))
    recv_l = [_exchange_wait(f"grads_wait{l}", sending[l], dx) for l in range(DEPTH)]
    recv = [jnp.stack([r[i] for r in recv_l], axis=1) for i in range(len(BIG))]
    recv.append(jnp.concatenate([r[len(BIG)] for r in recv_l], axis=1))

    grads, delta, new_m, new_v = {}, {}, {}, {}

    def update(k, parts, nparts):
        shape = wts[k].shape
        cols = shape[-1]
        rows = wts[k].size // cols
        br = min(rows, 1024 if cols <= 128 else 256)
        outs = _adamw("adamw_" + k, parts.reshape(nparts * rows, cols), nparts, wts[k].reshape(rows, cols),
                      mom[k].reshape(rows, cols), var[k].reshape(rows, cols), br)
        grads[k], delta[k], new_m[k], new_v[k] = (o.reshape(shape) for o in outs)

    for k, parts in zip(BIG, recv[:len(BIG)]):
        update(k, parts, NDEV)
    rows = DEPTH * PACK_BR
    (g_small,) = _ew("grads_small", lambda *p: (_sum_in_order(p),), rows, PACK_BR,
                     [_ri(recv[-1].reshape(NDEV * rows, PACK_COLS), PACK_COLS, 0, d * DEPTH) for d in range(NDEV)], [],
                     [(PACK_COLS, F32)])
    for k, g in _unpack(g_small, wts).items():
        update(k, g, 1)

    return (loss, dx[None], *[grads[k] for k in WEIGHTS], *[delta[k] for k in WEIGHTS],
            *[new_m[k] for k in WEIGHTS], *[new_v[k] for k in WEIGHTS])
```

```python
import functools
import math

import jax
import jax.numpy as jnp
from jax import lax
from jax.experimental import pallas as pl
from jax.experimental.pallas import tpu as pltpu

F32 = jnp.float32
BF16 = jnp.bfloat16

NDEV = 8
DEPTH = 4
SEQ = 2048
DM = 1024
NCOL = 8192
SW = 512
NGRP = 32
GCH = 16
NST = 64
NS = NGRP * NST
HD = 128
AW = 512
DILATIONS = (1, 4, 16)
ABLK = 128
ATTN_PAIR = 2
RMS_EPS = 1e-6
LR, B1, B2, ADAM_EPS, WD, STEP = 0.001, 0.9, 0.999, 1e-08, 0.01, 10

CB_U, CB_ZS, CB_Q, CB_K, CB_V, CB_ZA = 0, 1, 2, 5, 8, 11
CB_GS, CB_GA = 6, 7

VMEM_LIMIT = 56 * 2 ** 20


def _row_order(j):
    return jnp.where(j < CB_Q, 1, 0)


def _params(*sem):
    return pltpu.CompilerParams(dimension_semantics=sem, vmem_limit_bytes=VMEM_LIMIT)


def _ew(name, fn, rows, br, row_ins, bc_ins, row_outs, red_outs=()):
    n_in = len(row_ins) + len(bc_ins)
    n_ro = len(row_outs)
    steps = rows // br
    assert steps * br == rows

    def body(*refs):
        vals = fn(*[r[...] for r in refs[:n_in]])
        outs = refs[n_in:]
        for r, v in zip(outs[:n_ro], vals[:n_ro]):
            r[...] = v.astype(r.dtype)
        if red_outs:
            @pl.when(pl.program_id(0) == 0)
            def _():
                for r in outs[n_ro:]:
                    r[...] = jnp.zeros(r.shape, r.dtype)
            for r, v in zip(outs[n_ro:], vals[n_ro:]):
                r[...] += v

    in_specs = []
    for (_, w, cb, rb) in row_ins:
        in_specs.append(pl.BlockSpec((br, w), functools.partial(lambda i, cb, rb: (rb + i, cb), cb=cb, rb=rb)))
    for a in bc_ins:
        in_specs.append(pl.BlockSpec(a.shape, functools.partial(lambda i, nd: (0,) * nd, nd=a.ndim)))
    out_specs = [pl.BlockSpec((br, w), lambda i: (i, 0)) for (w, _) in row_outs]
    out_specs += [pl.BlockSpec((1, w), lambda i: (0, 0)) for w in red_outs]
    out_shape = [jax.ShapeDtypeStruct((rows, w), dt) for (w, dt) in row_outs]
    out_shape += [jax.ShapeDtypeStruct((1, w), F32) for w in red_outs]
    return pl.pallas_call(
        body, name=name, grid=(steps,), in_specs=in_specs, out_specs=out_specs, out_shape=out_shape,
        compiler_params=_params("arbitrary"),
    )(*[a for (a, _, _, _) in row_ins], *bc_ins)


def _ri(a, w=None, cb=0, rb=0):
    return (a, a.shape[1] if w is None else w, cb, rb)


_DIMS = {"nn": ((1,), (0,)), "nt": ((1,), (1,)), "tn": ((0,), (0,))}


def _mm(name, a, b, mode, M, N, K, bm, bn, bk, out_dtype, a_spec=None, b_spec=None, o_spec=None, out_shape=None):
    nk = K // bk
    assert M % bm == 0 and N % bn == 0 and nk * bk == K

    own_acc = nk > 1 and out_dtype != F32

    def body(a_ref, b_ref, o_ref, *scratch):
        part = lax.dot_general(a_ref[...].astype(BF16), b_ref[...].astype(BF16), (_DIMS[mode], ((), ())),
                               preferred_element_type=F32)
        if nk == 1:
            o_ref[...] = part.astype(o_ref.dtype)
            return
        k = pl.program_id(2)
        acc_ref = scratch[0] if own_acc else o_ref

        @pl.when(k == 0)
        def _():
            acc_ref[...] = part

        @pl.when(k > 0)
        def _():
            acc_ref[...] += part

        if own_acc:
            @pl.when(k == nk - 1)
            def _():
                o_ref[...] = acc_ref[...].astype(o_ref.dtype)

    if a_spec is None:
        a_spec = (pl.BlockSpec((bk, bm), lambda i, j, k: (k, i)) if mode == "tn"
                  else pl.BlockSpec((bm, bk), lambda i, j, k: (i, k)))
    if b_spec is None:
        b_spec = (pl.BlockSpec((bn, bk), lambda i, j, k: (j, k)) if mode == "nt"
                  else pl.BlockSpec((bk, bn), lambda i, j, k: (k, j)))
    if o_spec is None:
        o_spec = pl.BlockSpec((bm, bn), lambda i, j, k: (i, j))
    if out_shape is None:
        out_shape = (M, N)
    return pl.pallas_call(
        body, name=name, grid=(M // bm, N // bn, nk), in_specs=[a_spec, b_spec], out_specs=o_spec,
        out_shape=jax.ShapeDtypeStruct(out_shape, out_dtype),
        scratch_shapes=[pltpu.VMEM((bm, bn), F32)] if own_acc else [],
        compiler_params=_params("parallel", "parallel", "arbitrary"),
    )(a, b)


SCAN_LANES = 512
SCAN_CHUNKS = 8


def _to_chunked(a):
    return a.reshape(SCAN_CHUNKS, SEQ // SCAN_CHUNKS, -1).transpose(1, 0, 2).reshape(SEQ, -1)


def _from_chunked(a):
    return a.reshape(SEQ // SCAN_CHUNKS, SCAN_CHUNKS, -1).transpose(1, 0, 2).reshape(SEQ, -1)


def _scan_block(dr_ref, di_ref, sr_ref, si_ref, lam_r, lam_i, reverse):
    T = SEQ // SCAN_CHUNKS
    bl = lam_r.shape[1]
    assert T == 2 ** 8
    ar = jnp.broadcast_to(lam_r, (SCAN_CHUNKS, bl))
    ai = jnp.broadcast_to(lam_i, (SCAN_CHUNKS, bl))
    zero = jnp.zeros((SCAN_CHUNKS, bl), F32)

    def tile(j):
        return pl.ds(pl.multiple_of(j * SCAN_CHUNKS, SCAN_CHUNKS), SCAN_CHUNKS)

    def step(jj, carry):
        sr, si = carry
        j = T - 1 - jj if reverse else jj
        nr = ar * sr - ai * si + dr_ref[tile(j), :]
        ni = ar * si + ai * sr + di_ref[tile(j), :]
        sr_ref[tile(j), :] = nr
        si_ref[tile(j), :] = ni
        return nr, ni

    er, ei = lax.fori_loop(0, T, step, (zero, zero), unroll=4)

    pr, pi = ar[0:1], ai[0:1]
    for _ in range(8):
        pr, pi = pr * pr - pi * pi, 2.0 * pr * pi
    rows = lax.broadcasted_iota(jnp.int32, (SCAN_CHUNKS, bl), 0)
    cr, ci = zero, zero
    xr = jnp.zeros((1, bl), F32)
    xi = jnp.zeros((1, bl), F32)
    order = range(SCAN_CHUNKS - 2, -1, -1) if reverse else range(1, SCAN_CHUNKS)
    for c in order:
        src = c + 1 if reverse else c - 1
        nxr = pr * xr - pi * xi + er[src:src + 1]
        nxi = pr * xi + pi * xr + ei[src:src + 1]
        xr, xi = nxr, nxi
        cr = jnp.where(rows == c, xr, cr)
        ci = jnp.where(rows == c, xi, ci)

    def fix(jj, pw):
        pwr, pwi = pw
        j = T - 1 - jj if reverse else jj
        sr_ref[tile(j), :] = sr_ref[tile(j), :] + (pwr * cr - pwi * ci)
        si_ref[tile(j), :] = si_ref[tile(j), :] + (pwr * ci + pwi * cr)
        return pwr * ar - pwi * ai, pwr * ai + pwi * ar

    lax.fori_loop(0, T, fix, (ar, ai), unroll=4)


def _s5_forward(name, proj, wdt, cmt, lam_r, lam_i):
    bl = SCAN_LANES
    nblk = NS // bl

    def nt(a, b_):
        return lax.dot_general(a, b_, (((1,), (1,)), ((), ())), preferred_element_type=F32)

    def body(u_ref, wr_ref, wi_ref, cr_ref, ci_ref, ar_ref, ai_ref, s_ref, y_ref):
        sr_ref, si_ref = s_ref.at[0], s_ref.at[1]
        u = u_ref[...].astype(BF16)
        sr_ref[...] = nt(u, wr_ref[...])
        si_ref[...] = nt(u, wi_ref[...])
        _scan_block(sr_ref, si_ref, sr_ref, si_ref, ar_ref[...], ai_ref[...], False)
        part = nt(sr_ref[...].astype(BF16), cr_ref[...]) + nt(si_ref[...].astype(BF16), ci_ref[...])

        @pl.when(pl.program_id(0) == 0)
        def _():
            y_ref[...] = part

        @pl.when(pl.program_id(0) > 0)
        def _():
            y_ref[...] += part

    return pl.pallas_call(
        body, name=name, grid=(nblk,),
        in_specs=[pl.BlockSpec((SEQ, SW), lambda i: (0, CB_U)),
                  pl.BlockSpec((bl, SW), lambda i: (i, 0)), pl.BlockSpec((bl, SW), lambda i: (nblk + i, 0)),
                  pl.BlockSpec((SW, bl), lambda i: (0, i)), pl.BlockSpec((SW, bl), lambda i: (1, i)),
                  pl.BlockSpec((1, bl), lambda i: (0, i)), pl.BlockSpec((1, bl), lambda i: (0, i))],
        out_specs=[pl.BlockSpec((2, SEQ, bl), lambda i: (0, 0, i)), pl.BlockSpec((SEQ, SW), lambda i: (0, 0))],
        out_shape=[jax.ShapeDtypeStruct((2, SEQ, NS), F32), jax.ShapeDtypeStruct((SEQ, SW), F32)],
        compiler_params=_params("arbitrary"),
    )(proj, wdt, wdt, cmt, cmt, lam_r, lam_i)


S5_BWD_LANES = 256


def _s5_backward(name, dy0, proj, s, wdt, cmt, lam_r, lam_i):
    bl = S5_BWD_LANES
    nblk = NS // bl

    def tn(a, b_):
        return lax.dot_general(a, b_, (((0,), (0,)), ((), ())), preferred_element_type=F32)

    def prev(s_ref):
        last = pltpu.roll(s_ref[SEQ - SCAN_CHUNKS:SEQ, :], 1, 0)
        first = jnp.where(lax.broadcasted_iota(jnp.int32, (SCAN_CHUNKS, bl), 0) > 0, last, 0.0)
        return jnp.concatenate([first, s_ref[0:SEQ - SCAN_CHUNKS, :]], axis=0)

    def body(dy_ref, u_ref, s_ref, cr_ref, ci_ref, wr_ref, wi_ref, lr_ref, li_ref,
             dlr_ref, dli_ref, dwdt_ref, dcmt_ref, du_ref, ar_ref, ai_ref):
        dy = dy_ref[...]
        ar_ref[...] = jnp.dot(dy, cr_ref[...], preferred_element_type=F32)
        ai_ref[...] = jnp.dot(dy, ci_ref[...], preferred_element_type=F32)
        _scan_block(ar_ref, ai_ref, ar_ref, ai_ref, lr_ref[...], -li_ref[...], True)
        a_r, a_i = ar_ref[...], ai_ref[...]
        sr_ref, si_ref = s_ref.at[0], s_ref.at[1]
        spr, spi = prev(sr_ref), prev(si_ref)
        dlr_ref[...] = jnp.sum(a_r * spr + a_i * spi, axis=0, keepdims=True)
        dli_ref[...] = jnp.sum(a_i * spr - a_r * spi, axis=0, keepdims=True)
        a_rb, a_ib = a_r.astype(BF16), a_i.astype(BF16)
        u = u_ref[...].astype(BF16)
        dwdt_ref[0] = tn(a_rb, u)
        dwdt_ref[1] = tn(a_ib, u)
        dcmt_ref[0] = tn(dy, sr_ref[...].astype(BF16))
        dcmt_ref[1] = tn(dy, si_ref[...].astype(BF16))
        part = (jnp.dot(a_rb, wr_ref[...], preferred_element_type=F32)
                + jnp.dot(a_ib, wi_ref[...], preferred_element_type=F32))

        @pl.when(pl.program_id(0) == 0)
        def _():
            du_ref[...] = part

        @pl.when(pl.program_id(0) > 0)
        def _():
            du_ref[...] += part

    lam_spec = pl.BlockSpec((1, bl), lambda i: (0, i))
    return pl.pallas_call(
        body, name=name, grid=(nblk,),
        in_specs=[pl.BlockSpec((SEQ, SW), lambda i: (0, 0)), pl.BlockSpec((SEQ, SW), lambda i: (0, CB_U)),
                  pl.BlockSpec((2, SEQ, bl), lambda i: (0, 0, i)),
                  pl.BlockSpec((SW, bl), lambda i: (0, i)), pl.BlockSpec((SW, bl), lambda i: (1, i)),
                  pl.BlockSpec((bl, SW), lambda i: (i, 0)), pl.BlockSpec((bl, SW), lambda i: (nblk + i, 0)),
                  lam_spec, lam_spec],
        out_specs=[lam_spec, lam_spec, pl.BlockSpec((2, bl, SW), lambda i: (0, i, 0)),
                   pl.BlockSpec((2, SW, bl), lambda i: (0, 0, i)), pl.BlockSpec((SEQ, SW), lambda i: (0, 0))],
        out_shape=[jax.ShapeDtypeStruct((1, NS), F32), jax.ShapeDtypeStruct((1, NS), F32),
                   jax.ShapeDtypeStruct((2, NS, SW), F32), jax.ShapeDtypeStruct((2, SW, NS), F32),
                   jax.ShapeDtypeStruct((SEQ, SW), F32)],
        scratch_shapes=[pltpu.VMEM((SEQ, bl), F32)] * 2,
        compiler_params=_params("arbitrary"),
    )(dy0, proj, s, cmt, cmt, wdt, wdt, lam_r, lam_i)


def _scores(qb, kb, prev):
    s = lax.dot_general(qb, kb, (((1,), (1,)), ((), ())), preferred_element_type=F32) * (HD ** -0.5)
    row = lax.broadcasted_iota(jnp.int32, (ABLK, ABLK), 0)
    col = lax.broadcasted_iota(jnp.int32, (ABLK, ABLK), 1)
    return jnp.where((col >= row) if prev else (col <= row), s, -1e30)


def _block_rows(dil, r, b):
    if dil == 1:
        return pl.ds(pl.multiple_of(b * ABLK, ABLK), ABLK)
    return pl.ds(r + dil * ABLK * b, ABLK, stride=dil)


def _group_blocks(dil):
    nb = SEQ // dil // ABLK
    shift = nb.bit_length() - 1
    return nb, (lambda idx: (idx >> shift, idx & (nb - 1)))


def _qkv_specs(j_of):
    return [pl.BlockSpec((SEQ, HD), functools.partial(lambda j, c: (0, c + j_of(j)), c=(cb + g) * 4))
            for g in range(3) for cb in (CB_Q, CB_K, CB_V)]


def _attention_fwd(name, proj):
    def body(*refs):
        qkv, z_ref = refs[:9], refs[9]
        y_ref, ya_ref, l_ref = refs[10:13]
        accs, maxs, dens = refs[13:16], refs[16:19], refs[19:22]
        for g, dil in enumerate(DILATIONS):
            q_ref, k_ref, v_ref = qkv[3 * g:3 * g + 3]
            nb, where = _group_blocks(dil)

            def step(t, c, g=g, dil=dil, nb=nb, where=where, q_ref=q_ref, k_ref=k_ref, v_ref=v_ref):
                two = range(ATTN_PAIR)
                rb = [where(t + i * (SEQ // ABLK // ATTN_PAIR)) for i in two]
                rows = [_block_rows(dil, r, b) for r, b in rb]
                qb = [q_ref[rows[i], :].astype(BF16) for i in two]
                s_c = [_scores(qb[i], k_ref[rows[i], :].astype(BF16), False) for i in two]
                if nb > 1:
                    prev = [_block_rows(dil, r, jnp.maximum(b - 1, 0)) for r, b in rb]
                    s_p = [jnp.where(rb[i][1] > 0, _scores(qb[i], k_ref[prev[i], :].astype(BF16), True), -1e30)
                           for i in two]
                m = [jnp.max(s_c[i], axis=-1, keepdims=True) for i in two]
                if nb > 1:
                    m = [jnp.maximum(m[i], jnp.max(s_p[i], axis=-1, keepdims=True)) for i in two]
                p_c = [jnp.exp(s_c[i] - m[i]) for i in two]
                den = [jnp.sum(p_c[i], axis=-1, keepdims=True) for i in two]
                acc = [jnp.dot(p_c[i].astype(BF16), v_ref[rows[i], :].astype(BF16), preferred_element_type=F32)
                       for i in two]
                if nb > 1:
                    p_p = [jnp.exp(s_p[i] - m[i]) for i in two]
                    den = [den[i] + jnp.sum(p_p[i], axis=-1, keepdims=True) for i in two]
                    acc = [acc[i] + jnp.dot(p_p[i].astype(BF16), v_ref[prev[i], :].astype(BF16),
                                            preferred_element_type=F32) for i in two]
                for i in two:
                    accs[g][rows[i], :] = acc[i]
                    maxs[g][rows[i], :] = jnp.broadcast_to(m[i], (ABLK, HD))
                    dens[g][rows[i], :] = jnp.broadcast_to(den[i], (ABLK, HD))
                return c

            lax.fori_loop(0, SEQ // ABLK // ATTN_PAIR, step, 0)
        top = jnp.maximum(jnp.maximum(maxs[0][...], maxs[1][...]), maxs[2][...])
        den = jnp.zeros((SEQ, HD), F32)
        y = jnp.zeros((SEQ, HD), F32)
        for g in range(3):
            wgt = jnp.exp(maxs[g][...] - top)
            den = den + wgt * dens[g][...]
            y = y + wgt * accs[g][...]
        y = y / den
        y_ref[...] = y
        ya_ref[...] = (y * _silu(z_ref[...])).astype(ya_ref.dtype)
        l_ref[...] = top + jnp.log(den)

    ospec = pl.BlockSpec((SEQ, HD), lambda j: (0, j))
    return pl.pallas_call(
        body, name=name, grid=(AW // HD,),
        in_specs=_qkv_specs(lambda j: j) + [pl.BlockSpec((SEQ, HD), lambda j: (0, CB_ZA * 4 + j))],
        out_specs=[ospec, ospec, ospec],
        out_shape=[jax.ShapeDtypeStruct((SEQ, AW), F32), jax.ShapeDtypeStruct((SEQ, AW), BF16),
                   jax.ShapeDtypeStruct((SEQ, AW), F32)],
        scratch_shapes=[pltpu.VMEM((SEQ, HD), F32)] * 9,
        compiler_params=_params("parallel"),
    )(*([proj] * 10))


def _attention_bwd(name, proj, dya, y, lse):
    def tn(a, b_):
        return lax.dot_general(a, b_, (((0,), (0,)), ((), ())), preferred_element_type=F32)

    def nt(a, b_):
        return lax.dot_general(a, b_, (((1,), (1,)), ((), ())), preferred_element_type=F32)

    def body(*refs):
        qkv, z_ref, dya_ref, y_ref, l_ref = refs[:9], refs[9], refs[10], refs[11], refs[12]
        outs, dza_ref = refs[13:22], refs[22]
        dy_s, dsum_s, dq_s, dk_own, dv_own, dk_prev, dv_prev = refs[23:]
        _, vjp = jax.vjp(lambda y_, z_: y_ * _silu(z_), y_ref[...], z_ref[...])
        dy, dz = vjp(dya_ref[...])
        dza_ref[...] = dz.astype(dza_ref.dtype)
        dy_s[...] = dy
        dsum_s[...] = jnp.broadcast_to(jnp.sum(dy * y_ref[...], axis=-1, keepdims=True), (SEQ, HD))
        for g, dil in enumerate(DILATIONS):
            q_ref, k_ref, v_ref = qkv[3 * g:3 * g + 3]
            nb, where = _group_blocks(dil)
            if nb > 1:
                dk_prev[...] = jnp.zeros(dk_prev.shape, F32)
                dv_prev[...] = jnp.zeros(dv_prev.shape, F32)

            def step(t, c, dil=dil, nb=nb, where=where, q_ref=q_ref, k_ref=k_ref, v_ref=v_ref):
                rb = [where(t + i * (SEQ // ABLK // ATTN_PAIR)) for i in range(ATTN_PAIR)]
                sides = []
                for r, b in rb:
                    rows = _block_rows(dil, r, b)
                    own = dict(b=b, qrows=rows, krows=rows, prev=False, dk=dk_own, dv=dv_own,
                               q=q_ref[rows, :].astype(BF16), dy=dy_s[rows, :].astype(BF16))
                    sides.append(own)
                    if nb > 1:
                        sides.append(dict(own, krows=_block_rows(dil, r, jnp.maximum(b - 1, 0)), prev=True,
                                          dk=dk_prev, dv=dv_prev))
                for s_ in sides:
                    s_["k"] = k_ref[s_["krows"], :].astype(BF16)
                    s_["v"] = v_ref[s_["krows"], :].astype(BF16)
                for s_ in sides:
                    sc = _scores(s_["q"], s_["k"], s_["prev"])
                    s_["s"] = jnp.where(s_["b"] > 0, sc, -1e30) if s_["prev"] else sc
                    s_["dp"] = nt(s_["dy"], s_["v"])
                for s_ in sides:
                    p = jnp.exp(s_["s"] - l_ref[s_["qrows"], :])
                    s_["p"] = p.astype(BF16)
                    s_["ds"] = (p * (s_["dp"] - dsum_s[s_["qrows"], :]) * (HD ** -0.5)).astype(BF16)
                for s_ in sides:
                    s_["dk"][s_["krows"], :] = tn(s_["ds"], s_["q"])
                    s_["dv"][s_["krows"], :] = tn(s_["p"], s_["dy"])
                    s_["dq"] = jnp.dot(s_["ds"], s_["k"], preferred_element_type=F32)
                per = len(sides) // ATTN_PAIR
                for i in range(ATTN_PAIR):
                    dq = sides[i * per]["dq"]
                    if per > 1:
                        dq = dq + sides[i * per + 1]["dq"]
                    dq_s[sides[i * per]["qrows"], :] = dq
                return c

            lax.fori_loop(0, SEQ // ABLK // ATTN_PAIR, step, 0)
            dq_ref, dk_ref, dv_ref = outs[3 * g:3 * g + 3]
            dq_ref[...] = dq_s[...].astype(dq_ref.dtype)
            if nb > 1:
                dk_ref[...] = (dk_own[...] + dk_prev[...]).astype(dk_ref.dtype)
                dv_ref[...] = (dv_own[...] + dv_prev[...]).astype(dv_ref.dtype)
            else:
                dk_ref[...] = dk_own[...].astype(dk_ref.dtype)
                dv_ref[...] = dv_own[...].astype(dv_ref.dtype)

    ospec = pl.BlockSpec((SEQ, HD), lambda j: (0, j))
    outs = pl.pallas_call(
        body, name=name, grid=(AW // HD,),
        in_specs=_qkv_specs(lambda j: j) + [pl.BlockSpec((SEQ, HD), lambda j: (0, CB_ZA * 4 + j))] + [ospec] * 3,
        out_specs=[ospec] * 10, out_shape=[jax.ShapeDtypeStruct((SEQ, AW), BF16)] * 10,
        scratch_shapes=[pltpu.VMEM((SEQ, HD), F32)] * 7,
        compiler_params=_params("parallel"),
    )(*([proj] * 10), dya, y, lse)
    return outs[:9], outs[9]


def _rms(x, g):
    return x * lax.rsqrt(jnp.mean(x * x, axis=-1, keepdims=True) + RMS_EPS) * g


def _sig(x):
    return 1.0 / (1.0 + jnp.exp(-x))


def _silu(x):
    return x * _sig(x)


def _gelu(x):
    return 0.5 * x * (1.0 + jnp.tanh(math.sqrt(2.0 / math.pi) * (x + 0.044715 * (x * x * x))))


def _y1_fn(y0p, u, dskip):
    return _gelu(y0p + dskip * u)


def _ys_fn(y1, t, z, bglu):
    return y1 * _sig(t + bglu) * _silu(z)


def _merge_fn(ms, ma, gs, ga):
    return _sig(gs) * ms + _sig(ga) * ma


def _colsum(v):
    return jnp.sum(v, axis=0, keepdims=True)


def _lam_fn(lre, lim, ldt):
    a = jnp.minimum(lre, -1e-4)
    dt = jnp.exp(ldt)
    mag = jnp.exp(a * dt)
    ar = mag * jnp.cos(lim * dt)
    ai = mag * jnp.sin(lim * dt)
    den = a * a + lim * lim
    cr = ((ar - 1.0) * a + ai * lim) / den
    ci = (ai * a - (ar - 1.0) * lim) / den
    return ar, ai, cr, ci


def _bbar_fn(cr, ci, bre, bim):
    return cr * bre - ci * bim, cr * bim + ci * bre


def _same_group(rows, a, cols, b):
    r = lax.broadcasted_iota(jnp.int32, (rows, cols), 0) >> (a.bit_length() - 1)
    c = lax.broadcasted_iota(jnp.int32, (rows, cols), 1) >> (b.bit_length() - 1)
    return r == c


def _expand(name, blocks, signs, a, b, dtype):
    rows, cols = NGRP * a, NGRP * b
    assert a & (a - 1) == 0 and b & (b - 1) == 0

    def body(*refs):
        o_ref = refs[-1]
        tile = (lax.broadcasted_iota(jnp.int32, (b, cols), 1) & (b - 1)
                == lax.broadcasted_iota(jnp.int32, (b, cols), 0)).astype(F32)
        keep = _same_group(rows, a, cols, b)
        for i, (ref, sign) in enumerate(zip(refs[:-1], signs)):
            spread = jnp.dot(ref[...], tile, preferred_element_type=F32, precision=lax.Precision.HIGHEST)
            o_ref[i * rows:(i + 1) * rows, :] = jnp.where(keep, sign * spread, 0.0).astype(o_ref.dtype)

    return pl.pallas_call(body, name=name, out_shape=jax.ShapeDtypeStruct((len(blocks) * rows, cols), dtype),
                          compiler_params=pltpu.CompilerParams(vmem_limit_bytes=VMEM_LIMIT))(*blocks)


def _extract(name, m, a, b, at=(0, 0)):
    rows, cols = NGRP * a, NGRP * b
    assert a & (a - 1) == 0 and b & (b - 1) == 0

    def body(m_ref, o_ref):
        tile = (lax.broadcasted_iota(jnp.int32, (cols, b), 0) & (b - 1)
                == lax.broadcasted_iota(jnp.int32, (cols, b), 1)).astype(F32)
        kept = jnp.where(_same_group(rows, a, cols, b), m_ref[...], 0.0)
        o_ref[...] = jnp.dot(kept, tile, preferred_element_type=F32, precision=lax.Precision.HIGHEST)

    return pl.pallas_call(body, name=name, grid=(1,), in_specs=[pl.BlockSpec((rows, cols), lambda i: at)],
                          out_specs=pl.BlockSpec((rows, b), lambda i: (0, 0)),
                          out_shape=jax.ShapeDtypeStruct((rows, b), F32), compiler_params=_params("arbitrary"))(m)


def _layer_fwd(l, x, w, sp):
    tag = f"l{l}_"
    g1 = sp["pre_norm_g"].reshape(1, DM)
    (h,) = _ew(tag + "rms1", lambda x_, g: (_rms(x_, g),), SEQ, 256, [_ri(x)], [g1], [(DM, BF16)])
    hv = jnp.stack([h, _to_chunked(h)])
    win = w["w_in"]
    proj = _mm(tag + "proj", hv, win, "nn", SEQ, NCOL, DM, SEQ, 512, 1024, F32,
               a_spec=pl.BlockSpec((None, SEQ, 1024), lambda i, j, k: (_row_order(j), 0, 0)),
               b_spec=pl.BlockSpec((None, 1024, 512), lambda i, j, k: (j // 2, 0, j % 2)))

    ar, ai, cr, ci = _ew(tag + "lam", _lam_fn, NGRP, NGRP,
                         [_ri(sp["lambda_re"]), _ri(sp["lambda_im"]), _ri(sp["log_dt"].reshape(NGRP, 1))], [],
                         [(NST, F32)] * 4)
    bre = sp["b_re"].reshape(NS, GCH)
    bim = sp["b_im"].reshape(NS, GCH)
    bbr, bbi = _ew(tag + "bbar", _bbar_fn, NS, NS, [_ri(cr.reshape(NS, 1)), _ri(ci.reshape(NS, 1)), _ri(bre), _ri(bim)],
                   [], [(GCH, F32)] * 2)
    wdt = _expand(tag + "wdt", [bbr, bbi], [1.0, 1.0], NST, GCH, BF16)
    cmt = _expand(tag + "cmt", [sp["c_re"].reshape(SW, NST), sp["c_im"].reshape(SW, NST)], [1.0, -1.0], GCH, NST, BF16)
    s, y0p = _s5_forward(tag + "s5", proj, wdt, cmt, ar.reshape(1, NS), ai.reshape(1, NS))
    dskip = sp["d_skip"].reshape(1, SW)
    (y1,) = _ew(tag + "y1", lambda a, u, d: (_y1_fn(a, u, d),), SEQ, 256,
                [_ri(y0p), _ri(proj, SW, CB_U)], [dskip], [(SW, F32)])
    t = _mm(tag + "glu", y1, w["w_glu"], "nn", SEQ, SW, SW, 1024, 512, 512, F32)
    bglu = sp["b_glu"].reshape(1, SW)
    (ys_c,) = _ew(tag + "ys", lambda y1_, t_, z, b_: (_ys_fn(y1_, t_, z, b_),), SEQ, 256,
                  [_ri(y1), _ri(t), _ri(proj, SW, CB_ZS)], [bglu], [(SW, BF16)])
    ys = _from_chunked(ys_c)

    ypre, ya, lse = _attention_fwd(tag + "attn", proj)

    ms = _mm(tag + "branch_s", ys, w["w_branch_s"], "nn", SEQ, DM, SW, 1024, 1024, 512, F32)
    ma = _mm(tag + "branch_a", ya, w["w_branch_a"], "nn", SEQ, DM, AW, 1024, 1024, 512, F32)
    (merged,) = _ew(tag + "merge", lambda a, b_, c, d: (_merge_fn(a, b_, c, d),), SEQ, 256,
                    [_ri(ms), _ri(ma), _ri(proj, DM, CB_GS), _ri(proj, DM, CB_GA)], [], [(DM, BF16)])
    out = _mm(tag + "out", merged, w["w_out"], "nn", SEQ, DM, DM, 1024, 1024, 1024, F32)
    g2 = sp["post_norm_g"].reshape(1, DM)
    (x_new,) = _ew(tag + "post", lambda x_, o, g: (x_ + _rms(o, g),), SEQ, 256, [_ri(x), _ri(out)], [g2], [(DM, F32)])
    res = dict(x=x, hv=hv, proj=proj, ar=ar, ai=ai, cr=cr, ci=ci, wdt=wdt, cmt=cmt, s=s, y0p=y0p,
               y1=y1, t=t, ys=ys, ya=ya, ypre=ypre, lse=lse, ms=ms, ma=ma, merged=merged, out=out)
    return x_new, res


def _layer_bwd(l, dxn, r, w, sp):
    tag = f"l{l}b_"
    proj = r["proj"]
    g1 = sp["pre_norm_g"].reshape(1, DM)
    g2 = sp["post_norm_g"].reshape(1, DM)
    dskip = sp["d_skip"].reshape(1, SW)
    bglu = sp["b_glu"].reshape(1, SW)

    def post_b(d, o, g):
        _, vjp = jax.vjp(_rms, o, g)
        do, dg = vjp(d)
        return do, dg

    d_out, dg2 = _ew(tag + "post", post_b, SEQ, 256, [_ri(dxn), _ri(r["out"])], [g2], [(DM, BF16)], [DM])
    dw_out = _mm(tag + "dw_out", r["merged"], d_out, "tn", DM, DM, SEQ, 1024, 1024, SEQ, BF16)
    dmerged = _mm(tag + "dmerged", d_out, w["w_out"], "nt", SEQ, DM, DM, 1024, 1024, 1024, F32)

    def merge_b(d, ms, ma, gs, ga):
        _, vjp = jax.vjp(_merge_fn, ms, ma, gs, ga)
        return vjp(d)

    dms, dma, dgs, dga = _ew(tag + "merge", merge_b, SEQ, 256,
                             [_ri(dmerged), _ri(r["ms"]), _ri(r["ma"]), _ri(proj, DM, CB_GS), _ri(proj, DM, CB_GA)],
                             [], [(DM, BF16)] * 4)
    dw_bs = _mm(tag + "dw_bs", r["ys"], dms, "tn", SW, DM, SEQ, 512, 1024, SEQ, BF16)
    dw_ba = _mm(tag + "dw_ba", r["ya"], dma, "tn", AW, DM, SEQ, 512, 1024, SEQ, BF16)
    dys = _mm(tag + "dys", dms, w["w_branch_s"], "nt", SEQ, SW, DM, 1024, 512, 1024, F32)
    dya = _mm(tag + "dya", dma, w["w_branch_a"], "nt", SEQ, AW, DM, 1024, 512, 1024, F32)

    dqkv, dza = _attention_bwd(tag + "attn", proj, dya, r["ypre"], r["lse"])

    def ys_b(d, y1, t, z, b_):
        _, vjp = jax.vjp(_ys_fn, y1, t, z, b_)
        dy1, dt, dz, _ = vjp(d)
        return dy1, dt, dz, _colsum(dt)

    dy1a, dt, dzs, dbglu = _ew(tag + "ys", ys_b, SEQ, 256,
                               [_ri(_to_chunked(dys)), _ri(r["y1"]), _ri(r["t"]), _ri(proj, SW, CB_ZS)],
                               [bglu], [(SW, F32), (SW, BF16), (SW, BF16)], [SW])
    dw_glu = _mm(tag + "dw_glu", r["y1"], dt, "tn", SW, SW, SEQ, 512, 512, SEQ, BF16)
    dy1b = _mm(tag + "dy1b", dt, w["w_glu"], "nt", SEQ, SW, SW, 1024, 512, 512, F32)

    def y1_b(da, db, y0p, u, d_):
        _, vjp = jax.vjp(_y1_fn, y0p, u, d_)
        dy0, du, dd = vjp(da + db)
        return dy0, du, dd

    dy0, du_skip, ddskip = _ew(tag + "y1", y1_b, SEQ, 256,
                               [_ri(dy1a), _ri(dy1b), _ri(r["y0p"]), _ri(proj, SW, CB_U)], [dskip],
                               [(SW, BF16), (SW, F32)], [SW])
    dlr, dli, dwdt, dcmt, du_s = _s5_backward(tag + "s5", dy0, proj, r["s"], r["wdt"], r["cmt"],
                                              r["ar"].reshape(1, NS), r["ai"].reshape(1, NS))
    (du,) = _ew(tag + "du", lambda a, c: (a + c,), SEQ, 256, [_ri(du_s), _ri(du_skip)], [], [(SW, BF16)])

    dwdt = dwdt.reshape(2 * NS, SW)
    dcmt = dcmt.reshape(2 * SW, NS)
    dbbr = _extract(tag + "dbbr", dwdt, NST, GCH, (0, 0))
    dbbi = _extract(tag + "dbbi", dwdt, NST, GCH, (1, 0))
    bre = sp["b_re"].reshape(NS, GCH)
    bim = sp["b_im"].reshape(NS, GCH)

    def bbar_b(cr, ci, br_, bi_, dr, di):
        _, vjp = jax.vjp(_bbar_fn, cr, ci, br_, bi_)
        return vjp((dr, di))

    dcr, dci, dbre, dbim = _ew(tag + "bbar", bbar_b, NS, NS,
                               [_ri(r["cr"].reshape(NS, 1)), _ri(r["ci"].reshape(NS, 1)), _ri(bre), _ri(bim),
                                _ri(dbbr), _ri(dbbi)], [], [(1, F32), (1, F32), (GCH, F32), (GCH, F32)])

    def lam_b(lre, lim, ldt, dar, dai, dcr_, dci_):
        _, vjp = jax.vjp(_lam_fn, lre, lim, ldt)
        return vjp((dar, dai, dcr_, dci_))

    dlre, dlim, dldt = _ew(tag + "lam", lam_b, NGRP, NGRP,
                           [_ri(sp["lambda_re"]), _ri(sp["lambda_im"]), _ri(sp["log_dt"].reshape(NGRP, 1)),
                            _ri(dlr.reshape(NGRP, NST)), _ri(dli.reshape(NGRP, NST)),
                            _ri(dcr.reshape(NGRP, NST)), _ri(dci.reshape(NGRP, NST))], [],
                           [(NST, F32), (NST, F32), (1, F32)])
    dc_re = _extract(tag + "dc_re", dcmt, GCH, NST, (0, 0)).reshape(NGRP, GCH, NST)
    dc_im = -_extract(tag + "dc_im", dcmt, GCH, NST, (1, 0)).reshape(NGRP, GCH, NST)

    dq, dk, dv = ([dqkv[3 * g + i] for g in range(3)] for i in range(3))
    dproj = jnp.concatenate([du, dzs, *dq, *dk, *dv, dza, dgs, dga], axis=1)
    dw_in = _mm(tag + "dw_in", r["hv"], dproj, "tn", DM, NCOL, SEQ, 1024, 512, SEQ, BF16,
                a_spec=pl.BlockSpec((None, SEQ, 1024), lambda i, j, k: (_row_order(j), 0, 0)),
                o_spec=pl.BlockSpec((None, 1024, 512), lambda i, j, k: (j // 2, 0, j % 2)), out_shape=(NDEV, DM, DM))

    dh_time = _mm(tag + "dh_time", dproj, w["w_in"], "nt", SEQ, DM, NCOL - DM, SEQ, 1024, 1024, F32,
                  a_spec=pl.BlockSpec((SEQ, 1024), lambda i, j, k: (0, 1 + k)),
                  b_spec=pl.BlockSpec((None, 1024, 1024), lambda i, j, k: (1 + k, 0, 0)))
    dh_chunked = _mm(tag + "dh_chunked", dproj, w["w_in"], "nt", SEQ, DM, DM, SEQ, 1024, 1024, F32,
                     a_spec=pl.BlockSpec((SEQ, 1024), lambda i, j, k: (0, 0)),
                     b_spec=pl.BlockSpec((None, 1024, 1024), lambda i, j, k: (0, 0, 0)))
    dh = [dh_time, _from_chunked(dh_chunked)]

    def pre_b(d, dh0, dh1, x_, g):
        _, vjp = jax.vjp(_rms, x_, g)
        dx_, dg = vjp(dh0 + dh1)
        return d + dx_, dg

    dx, dg1 = _ew(tag + "pre", pre_b, SEQ, 256, [_ri(dxn)] + [_ri(t_) for t_ in dh] + [_ri(r["x"])], [g1],
                  [(DM, F32)], [DM])

    big = dict(w_in=dw_in, w_glu=dw_glu.reshape(NDEV, SW // NDEV, SW),
               w_branch_s=dw_bs.reshape(SW, NDEV, DM // NDEV).transpose(1, 0, 2),
               w_branch_a=dw_ba.reshape(AW, NDEV, DM // NDEV).transpose(1, 0, 2),
               w_out=dw_out.reshape(NDEV, DM // NDEV, DM))
    small = dict(pre_norm_g=dg1.reshape(DM), lambda_re=dlre, lambda_im=dlim, log_dt=dldt.reshape(NGRP),
                 b_re=dbre.reshape(NGRP, NST, GCH), b_im=dbim.reshape(NGRP, NST, GCH), c_re=dc_re, c_im=dc_im,
                 d_skip=ddskip.reshape(SW), b_glu=dbglu.reshape(SW), post_norm_g=dg2.reshape(DM))
    return dx, big, small


_HBM = pl.BlockSpec(memory_space=pltpu.HBM)
_SEM = pl.BlockSpec(memory_space=pltpu.SEMAPHORE)
_EFFECT = pltpu.SideEffectType.DATAFLOW_SIDE_EFFECTING


def _remote_copies(srcs, dsts, send_sems, recv_sems, gather):
    x, y, c = lax.axis_index("x"), lax.axis_index("y"), lax.axis_index("c")
    me = 4 * x + 2 * y + c
    copies = []
    for i in range(len(srcs)):
        for k in range(1, NDEV):
            peer = (x ^ (k >> 2), y ^ ((k >> 1) & 1), c ^ (k & 1))
            src = srcs[i] if gather[i] else srcs[i].at[me ^ k]
            copies.append(pltpu.make_async_remote_copy(
                src_ref=src, dst_ref=dsts[i].at[me], send_sem=send_sems[i], recv_sem=recv_sems[i],
                device_id=peer, device_id_type=pl.DeviceIdType.MESH))
    return copies


def _all_seven(dst, send_sem, recv_sem):
    seven = dst.at[pl.ds(0, NDEV - 1)]
    me = (lax.axis_index("x"), lax.axis_index("y"), lax.axis_index("c"))
    return pltpu.make_async_remote_copy(src_ref=seven, dst_ref=seven, send_sem=send_sem, recv_sem=recv_sem,
                                        device_id=me, device_id_type=pl.DeviceIdType.MESH)


def _own_slabs(name, arrs, gather, after):
    n = len(arrs)
    me = (4 * lax.axis_index("x") + 2 * lax.axis_index("y") + lax.axis_index("c")).astype(jnp.int32).reshape(1)

    def body(me_ref, *refs):
        for src, dst in zip(refs[:n], refs[n + 1:]):
            dst[...] = src[...]

    def zeros(k):
        return (0,) * k

    in_specs, out_specs, out_shape = [], [], []
    for a, g in zip(arrs, gather):
        slab = a.shape if g else a.shape[1:]
        nd = len(slab)
        if g:
            in_specs.append(pl.BlockSpec(slab, functools.partial(lambda i, me_ref, nd: zeros(nd), nd=nd)))
        else:
            in_specs.append(pl.BlockSpec((None,) + slab, functools.partial(lambda i, me_ref, nd: (me_ref[0],) + zeros(nd), nd=nd)))
        out_specs.append(pl.BlockSpec((None,) + slab, functools.partial(lambda i, me_ref, nd: (me_ref[0],) + zeros(nd), nd=nd)))
        out_shape.append(jax.ShapeDtypeStruct((NDEV,) + slab, a.dtype))
    in_specs.append(pl.BlockSpec(memory_space=pl.ANY))
    return pl.pallas_call(
        body, name=name, out_shape=out_shape,
        grid_spec=pltpu.PrefetchScalarGridSpec(num_scalar_prefetch=1, grid=(1,), in_specs=in_specs, out_specs=out_specs),
        compiler_params=_params("arbitrary"),
    )(me, *arrs, after)


def _exchange_start(name, arrs, gather, after):
    n = len(arrs)
    lands = _own_slabs(name + "_own", arrs, gather, after)

    def body(*refs):
        srcs, dsts = refs[:n], refs[n:2 * n]
        send_sems, recv_sems = refs[2 * n:3 * n], refs[3 * n:4 * n]
        token = refs[-1]
        for cp in _remote_copies(srcs, dsts, send_sems, recv_sems, gather):
            cp.start()
        token[...] = jnp.zeros(token.shape, token.dtype)

    thru = [pltpu.HBM(a.shape, a.dtype) for a in list(arrs) + list(lands)]
    outs = pl.pallas_call(
        body, name=name,
        out_shape=(*[pltpu.SemaphoreType.DMA(())] * (2 * n), *thru, jax.ShapeDtypeStruct((8, 128), F32)),
        in_specs=[_HBM] * (2 * n),
        out_specs=(*[_SEM] * (2 * n), *[_HBM] * (2 * n), pl.BlockSpec(memory_space=pltpu.VMEM)),
        input_output_aliases={i: 2 * n + i for i in range(2 * n)},
        compiler_params=pltpu.CompilerParams(has_side_effects=_EFFECT),
    )(*[pltpu.with_memory_space_constraint(a, pltpu.HBM) for a in list(arrs) + list(lands)])
    return dict(send=outs[:n], recv=outs[n:2 * n], srcs=outs[2 * n:3 * n], lands=outs[3 * n:4 * n], token=outs[-1],
                gather=gather)


def _exchange_wait(name, started, after):
    n = len(started["srcs"])

    def body(*refs):
        dsts = refs[n:2 * n]
        send_sems, recv_sems = refs[2 * n:3 * n], refs[3 * n:4 * n]
        for i in range(n):
            cp = _all_seven(dsts[i], send_sems[i], recv_sems[i])
            cp.wait_send()
            cp.wait_recv()

    bufs = list(started["srcs"]) + list(started["lands"])
    outs = pl.pallas_call(
        body, name=name, out_shape=tuple(pltpu.HBM(a.shape, a.dtype) for a in bufs),
        in_specs=[_HBM] * (2 * n) + [_SEM] * (2 * n) + [pl.BlockSpec(memory_space=pl.ANY)], out_specs=(_HBM,) * (2 * n),
        input_output_aliases={i: i for i in range(2 * n)},
        compiler_params=pltpu.CompilerParams(has_side_effects=_EFFECT),
    )(*bufs, *started["send"], *started["recv"], after)
    return outs[n:]


def _sum_in_order(parts):
    g = parts[0].astype(F32)
    for p in parts[1:]:
        g = g + p.astype(F32)
    return g


def _adamw(name, parts, nparts, w, m, v, br):
    rows, cols = w.shape

    def fn(*a):
        g = _sum_in_order(a[:nparts])
        w_, m_, v_ = a[nparts:]
        m2 = B1 * m_ + (1.0 - B1) * g
        v2 = B2 * v_ + (1.0 - B2) * (g * g)
        m_hat = m2 / (1.0 - B1 ** STEP)
        v_hat = v2 / (1.0 - B2 ** STEP)
        delta = -LR * (m_hat / (jnp.sqrt(v_hat) + ADAM_EPS) + WD * w_)
        return g, delta, m2, v2

    ins = [_ri(parts, cols, 0, d * (rows // br)) for d in range(nparts)] + [_ri(w), _ri(m), _ri(v)]
    return _ew(name, fn, rows, br, ins, [], [(cols, F32)] * 4)


SMALL = ("pre_norm_g", "lambda_re", "lambda_im", "log_dt", "b_re", "b_im", "c_re", "c_im", "d_skip", "b_glu",
         "post_norm_g")
BIG = ("w_in", "w_glu", "w_branch_s", "w_branch_a", "w_out")
WEIGHTS = ("pre_norm_g", "w_in", "lambda_re", "lambda_im", "log_dt", "b_re", "b_im", "c_re", "c_im", "d_skip",
           "w_glu", "b_glu", "w_branch_s", "w_branch_a", "w_out", "post_norm_g")
PACK_COLS = 1024
PACK_BR = 136


def _pack_layer(d):
    pieces = [d[k].astype(F32).reshape(-1) for k in SMALL]
    used = sum(p.shape[0] for p in pieces)
    assert used <= PACK_BR * PACK_COLS
    return jnp.concatenate(pieces + [jnp.zeros((PACK_BR * PACK_COLS - used,), F32)]).reshape(PACK_BR, PACK_COLS)


def _unpack(p, like):
    flat = p.reshape(DEPTH, PACK_BR * PACK_COLS)
    out, off = {}, 0
    for k in SMALL:
        n = like[k].size // DEPTH
        out[k] = flat[:, off:off + n].reshape(like[k].shape)
        off += n
    return out


def _local_step(x, target, small, weights_of, grads_done):
    res, ws = [], []
    for l in range(DEPTH):
        w_l, tok = weights_of(l, x)
        sp = {k: small[k][l] for k in SMALL}
        if tok is not None:
            sp["pre_norm_g"] = sp["pre_norm_g"] + tok[0, 0]
        x, r = _layer_fwd(l, x, w_l, sp)
        res.append(r)
        ws.append(w_l)

    def loss_fn(y, t):
        e = y - t
        return e * (1.0 / DM), jnp.sum(_colsum(0.5 * e * e * (1.0 / DM)), axis=1, keepdims=True)

    dx, loss = _ew("loss", loss_fn, SEQ, 256, [_ri(x), _ri(target)], [], [(DM, F32)], [1])
    tok = None
    for l in reversed(range(DEPTH)):
        sp = {k: small[k][l] for k in SMALL}
        if tok is not None:
            sp["post_norm_g"] = sp["post_norm_g"] + tok[0, 0]
        dx, big, sm = _layer_bwd(l, dx, res[l], ws[l], sp)
        tok = grads_done(l, big, sm)
    return loss.reshape(()), dx


def _full_weights(gathered):
    g = gathered
    return dict(
        w_in=g["w_in"],
        w_glu=g["w_glu"].reshape(SW, SW),
        w_branch_s=g["w_branch_s"].transpose(1, 0, 2).reshape(SW, DM),
        w_branch_a=g["w_branch_a"].transpose(1, 0, 2).reshape(AW, DM),
        w_out=g["w_out"].reshape(DM, DM),
    )


def kernel(x, pre_norm_g, w_in, lambda_re, lambda_im, log_dt, b_re, b_im, c_re, c_im, d_skip, w_glu, b_glu, w_branch_s, w_branch_a, w_out, post_norm_g, loss_target, m_pre_norm_g, m_w_in, m_lambda_re, m_lambda_im, m_log_dt, m_b_re, m_b_im, m_c_re, m_c_im, m_d_skip, m_w_glu, m_b_glu, m_w_branch_s, m_w_branch_a, m_w_out, m_post_norm_g, v_pre_norm_g, v_w_in, v_lambda_re, v_lambda_im, v_log_dt, v_b_re, v_b_im, v_c_re, v_c_im, v_d_skip, v_w_glu, v_b_glu, v_w_branch_s, v_w_branch_a, v_w_out, v_post_norm_g):
    wts = dict(pre_norm_g=pre_norm_g, w_in=w_in, lambda_re=lambda_re, lambda_im=lambda_im, log_dt=log_dt, b_re=b_re,
               b_im=b_im, c_re=c_re, c_im=c_im, d_skip=d_skip, w_glu=w_glu, b_glu=b_glu, w_branch_s=w_branch_s,
               w_branch_a=w_branch_a, w_out=w_out, post_norm_g=post_norm_g)
    mom = dict(pre_norm_g=m_pre_norm_g, w_in=m_w_in, lambda_re=m_lambda_re, lambda_im=m_lambda_im, log_dt=m_log_dt,
               b_re=m_b_re, b_im=m_b_im, c_re=m_c_re, c_im=m_c_im, d_skip=m_d_skip, w_glu=m_w_glu, b_glu=m_b_glu,
               w_branch_s=m_w_branch_s, w_branch_a=m_w_branch_a, w_out=m_w_out, post_norm_g=m_post_norm_g)
    var = dict(pre_norm_g=v_pre_norm_g, w_in=v_w_in, lambda_re=v_lambda_re, lambda_im=v_lambda_im, log_dt=v_log_dt,
               b_re=v_b_re, b_im=v_b_im, c_re=v_c_re, c_im=v_c_im, d_skip=v_d_skip, w_glu=v_w_glu, b_glu=v_b_glu,
               w_branch_s=v_w_branch_s, w_branch_a=v_w_branch_a, w_out=v_w_out, post_norm_g=v_post_norm_g)

    def gather_start(l, after):
        return _exchange_start(f"gather_start{l}", [wts[k][l].astype(BF16) for k in BIG], [True] * len(BIG), after)

    gathering = {0: gather_start(0, x)}
    sending = {}

    def weights_of(l, x_l):
        gathered = _exchange_wait(f"gather_wait{l}", gathering[l], x_l)
        tok = None
        if l + 1 < DEPTH:
            gathering[l + 1] = gather_start(l + 1, gathered[0])
            tok = gathering[l + 1]["token"]
        return _full_weights(dict(zip(BIG, gathered))), tok

    def grads_done(l, big, sm):
        sending[l] = _exchange_start(f"grads_start{l}", [big[k] for k in BIG] + [_pack_layer(sm)],
                                     [False] * len(BIG) + [True], big["w_in"])
        return sending[l]["token"]

    loss, dx = _local_step(x[0], loss_target[0], wts, weights_of, grads_done)
    loss = lax.psum(loss, ("x", "y", "c"))
    recv_l = [_exchange_wait(f"grads_wait{l}", sending[l], dx) for l in range(DEPTH)]
    recv = [jnp.stack([r[i] for r in recv_l], axis=1) for i in range(len(BIG))]
    recv.append(jnp.concatenate([r[len(BIG)] for r in recv_l], axis=1))

    grads, delta, new_m, new_v = {}, {}, {}, {}

    def update(k, parts, nparts):
        shape = wts[k].shape
        cols = shape[-1]
        rows = wts[k].size // cols
        br = min(rows, 1024 if cols <= 128 else 256)
        outs = _adamw("adamw_" + k, parts.reshape(nparts * rows, cols), nparts, wts[k].reshape(rows, cols),
                      mom[k].reshape(rows, cols), var[k].reshape(rows, cols), br)
        grads[k], delta[k], new_m[k], new_v[k] = (o.reshape(shape) for o in outs)

    for k, parts in zip(BIG, recv[:len(BIG)]):
        update(k, parts, NDEV)
    rows = DEPTH * PACK_BR
    (g_small,) = _ew("grads_small", lambda *p: (_sum_in_order(p),), rows, PACK_BR,
                     [_ri(recv[-1].reshape(NDEV * rows, PACK_COLS), PACK_COLS, 0, d * DEPTH) for d in range(NDEV)], [],
                     [(PACK_COLS, F32)])
    for k, g in _unpack(g_small, wts).items():
        update(k, g, 1)

    return (loss, dx[None], *[grads[k] for k in WEIGHTS], *[delta[k] for k in WEIGHTS],
            *[new_m[k] for k in WEIGHTS], *[new_v[k] for k in WEIGHTS])
```

```python
import functools
import math

import jax
import jax.numpy as jnp
from jax import lax
from jax.experimental import pallas as pl
from jax.experimental.pallas import tpu as pltpu

F32 = jnp.float32
BF16 = jnp.bfloat16

NDEV = 8
DEPTH = 4
SEQ = 2048
DM = 1024
NCOL = 8192
SW = 512
NGRP = 32
GCH = 16
NST = 64
NS = NGRP * NST
HD = 128
AW = 512
DILATIONS = (1, 4, 16)
ABLK = 128
ATTN_PAIR = 2
RMS_EPS = 1e-6
LR, B1, B2, ADAM_EPS, WD, STEP = 0.001, 0.9, 0.999, 1e-08, 0.01, 10

CB_U, CB_ZS, CB_Q, CB_K, CB_V, CB_ZA = 0, 1, 2, 5, 8, 11
CB_GS, CB_GA = 6, 7

VMEM_LIMIT = 56 * 2 ** 20


def _row_order(j):
    return jnp.where(j < CB_Q, 1, 0)


def _params(*sem):
    return pltpu.CompilerParams(dimension_semantics=sem, vmem_limit_bytes=VMEM_LIMIT)


def _ew(name, fn, rows, br, row_ins, bc_ins, row_outs, red_outs=()):
    n_in = len(row_ins) + len(bc_ins)
    n_ro = len(row_outs)
    steps = rows // br
    assert steps * br == rows

    def body(*refs):
        vals = fn(*[r[...] for r in refs[:n_in]])
        outs = refs[n_in:]
        for r, v in zip(outs[:n_ro], vals[:n_ro]):
            r[...] = v.astype(r.dtype)
        if red_outs:
            @pl.when(pl.program_id(0) == 0)
            def _():
                for r in outs[n_ro:]:
                    r[...] = jnp.zeros(r.shape, r.dtype)
            for r, v in zip(outs[n_ro:], vals[n_ro:]):
                r[...] += v

    in_specs = []
    for (_, w, cb, rb) in row_ins:
        in_specs.append(pl.BlockSpec((br, w), functools.partial(lambda i, cb, rb: (rb + i, cb), cb=cb, rb=rb)))
    for a in bc_ins:
        in_specs.append(pl.BlockSpec(a.shape, functools.partial(lambda i, nd: (0,) * nd, nd=a.ndim)))
    out_specs = [pl.BlockSpec((br, w), lambda i: (i, 0)) for (w, _) in row_outs]
    out_specs += [pl.BlockSpec((1, w), lambda i: (0, 0)) for w in red_outs]
    out_shape = [jax.ShapeDtypeStruct((rows, w), dt) for (w, dt) in row_outs]
    out_shape += [jax.ShapeDtypeStruct((1, w), F32) for w in red_outs]
    return pl.pallas_call(
        body, name=name, grid=(steps,), in_specs=in_specs, out_specs=out_specs, out_shape=out_shape,
        compiler_params=_params("arbitrary"),
    )(*[a for (a, _, _, _) in row_ins], *bc_ins)


def _ri(a, w=None, cb=0, rb=0):
    return (a, a.shape[1] if w is None else w, cb, rb)


_DIMS = {"nn": ((1,), (0,)), "nt": ((1,), (1,)), "tn": ((0,), (0,))}


def _after(after):
    return ([pl.BlockSpec(memory_space=pl.ANY)], [after]) if after is not None else ([], [])


def _mm(name, a, b, mode, M, N, K, bm, bn, bk, out_dtype, a_spec=None, b_spec=None, o_spec=None, out_shape=None,
        after=None):
    nk = K // bk
    assert M % bm == 0 and N % bn == 0 and nk * bk == K
    after_specs, after_args = _after(after)

    own_acc = nk > 1 and out_dtype != F32

    def body(a_ref, b_ref, *rest):
        o_ref, scratch = rest[len(after_args)], rest[len(after_args) + 1:]
        part = lax.dot_general(a_ref[...].astype(BF16), b_ref[...].astype(BF16), (_DIMS[mode], ((), ())),
                               preferred_element_type=F32)
        if nk == 1:
            o_ref[...] = part.astype(o_ref.dtype)
            return
        k = pl.program_id(2)
        acc_ref = scratch[0] if own_acc else o_ref

        @pl.when(k == 0)
        def _():
            acc_ref[...] = part

        @pl.when(k > 0)
        def _():
            acc_ref[...] += part

        if own_acc:
            @pl.when(k == nk - 1)
            def _():
                o_ref[...] = acc_ref[...].astype(o_ref.dtype)

    if a_spec is None:
        a_spec = (pl.BlockSpec((bk, bm), lambda i, j, k: (k, i)) if mode == "tn"
                  else pl.BlockSpec((bm, bk), lambda i, j, k: (i, k)))
    if b_spec is None:
        b_spec = (pl.BlockSpec((bn, bk), lambda i, j, k: (j, k)) if mode == "nt"
                  else pl.BlockSpec((bk, bn), lambda i, j, k: (k, j)))
    if o_spec is None:
        o_spec = pl.BlockSpec((bm, bn), lambda i, j, k: (i, j))
    if out_shape is None:
        out_shape = (M, N)
    return pl.pallas_call(
        body, name=name, grid=(M // bm, N // bn, nk), in_specs=[a_spec, b_spec] + after_specs, out_specs=o_spec,
        out_shape=jax.ShapeDtypeStruct(out_shape, out_dtype),
        scratch_shapes=[pltpu.VMEM((bm, bn), F32)] if own_acc else [],
        compiler_params=_params("parallel", "parallel", "arbitrary"),
    )(a, b, *after_args)


SCAN_LANES = 512
SCAN_CHUNKS = 8


def _to_chunked(a):
    return a.reshape(SCAN_CHUNKS, SEQ // SCAN_CHUNKS, -1).transpose(1, 0, 2).reshape(SEQ, -1)


def _from_chunked(a):
    return a.reshape(SEQ // SCAN_CHUNKS, SCAN_CHUNKS, -1).transpose(1, 0, 2).reshape(SEQ, -1)


def _scan_block(dr_ref, di_ref, sr_ref, si_ref, lam_r, lam_i, reverse):
    T = SEQ // SCAN_CHUNKS
    bl = lam_r.shape[1]
    assert T == 2 ** 8
    ar = jnp.broadcast_to(lam_r, (SCAN_CHUNKS, bl))
    ai = jnp.broadcast_to(lam_i, (SCAN_CHUNKS, bl))
    zero = jnp.zeros((SCAN_CHUNKS, bl), F32)

    def tile(j):
        return pl.ds(pl.multiple_of(j * SCAN_CHUNKS, SCAN_CHUNKS), SCAN_CHUNKS)

    def step(jj, carry):
        sr, si = carry
        j = T - 1 - jj if reverse else jj
        nr = ar * sr - ai * si + dr_ref[tile(j), :]
        ni = ar * si + ai * sr + di_ref[tile(j), :]
        sr_ref[tile(j), :] = nr
        si_ref[tile(j), :] = ni
        return nr, ni

    er, ei = lax.fori_loop(0, T, step, (zero, zero), unroll=4)

    pr, pi = ar[0:1], ai[0:1]
    for _ in range(8):
        pr, pi = pr * pr - pi * pi, 2.0 * pr * pi
    rows = lax.broadcasted_iota(jnp.int32, (SCAN_CHUNKS, bl), 0)
    cr, ci = zero, zero
    xr = jnp.zeros((1, bl), F32)
    xi = jnp.zeros((1, bl), F32)
    order = range(SCAN_CHUNKS - 2, -1, -1) if reverse else range(1, SCAN_CHUNKS)
    for c in order:
        src = c + 1 if reverse else c - 1
        nxr = pr * xr - pi * xi + er[src:src + 1]
        nxi = pr * xi + pi * xr + ei[src:src + 1]
        xr, xi = nxr, nxi
        cr = jnp.where(rows == c, xr, cr)
        ci = jnp.where(rows == c, xi, ci)

    def fix(jj, pw):
        pwr, pwi = pw
        j = T - 1 - jj if reverse else jj
        sr_ref[tile(j), :] = sr_ref[tile(j), :] + (pwr * cr - pwi * ci)
        si_ref[tile(j), :] = si_ref[tile(j), :] + (pwr * ci + pwi * cr)
        return pwr * ar - pwi * ai, pwr * ai + pwi * ar

    lax.fori_loop(0, T, fix, (ar, ai), unroll=4)


def _s5_forward(name, proj, wdt, cmt, lam_r, lam_i):
    bl = SCAN_LANES
    nblk = NS // bl

    def nt(a, b_):
        return lax.dot_general(a, b_, (((1,), (1,)), ((), ())), preferred_element_type=F32)

    def body(u_ref, wr_ref, wi_ref, cr_ref, ci_ref, ar_ref, ai_ref, s_ref, y_ref):
        sr_ref, si_ref = s_ref.at[0], s_ref.at[1]
        u = u_ref[...].astype(BF16)
        sr_ref[...] = nt(u, wr_ref[...])
        si_ref[...] = nt(u, wi_ref[...])
        _scan_block(sr_ref, si_ref, sr_ref, si_ref, ar_ref[...], ai_ref[...], False)
        part = nt(sr_ref[...].astype(BF16), cr_ref[...]) + nt(si_ref[...].astype(BF16), ci_ref[...])

        @pl.when(pl.program_id(0) == 0)
        def _():
            y_ref[...] = part

        @pl.when(pl.program_id(0) > 0)
        def _():
            y_ref[...] += part

    return pl.pallas_call(
        body, name=name, grid=(nblk,),
        in_specs=[pl.BlockSpec((SEQ, SW), lambda i: (0, CB_U)),
                  pl.BlockSpec((bl, SW), lambda i: (i, 0)), pl.BlockSpec((bl, SW), lambda i: (nblk + i, 0)),
                  pl.BlockSpec((SW, bl), lambda i: (0, i)), pl.BlockSpec((SW, bl), lambda i: (1, i)),
                  pl.BlockSpec((1, bl), lambda i: (0, i)), pl.BlockSpec((1, bl), lambda i: (0, i))],
        out_specs=[pl.BlockSpec((2, SEQ, bl), lambda i: (0, 0, i)), pl.BlockSpec((SEQ, SW), lambda i: (0, 0))],
        out_shape=[jax.ShapeDtypeStruct((2, SEQ, NS), F32), jax.ShapeDtypeStruct((SEQ, SW), F32)],
        compiler_params=_params("arbitrary"),
    )(proj, wdt, wdt, cmt, cmt, lam_r, lam_i)


S5_BWD_LANES = 256


def _s5_backward(name, dy0, proj, s, wdt, cmt, lam_r, lam_i):
    bl = S5_BWD_LANES
    nblk = NS // bl

    def tn(a, b_):
        return lax.dot_general(a, b_, (((0,), (0,)), ((), ())), preferred_element_type=F32)

    def prev(s_ref):
        last = pltpu.roll(s_ref[SEQ - SCAN_CHUNKS:SEQ, :], 1, 0)
        first = jnp.where(lax.broadcasted_iota(jnp.int32, (SCAN_CHUNKS, bl), 0) > 0, last, 0.0)
        return jnp.concatenate([first, s_ref[0:SEQ - SCAN_CHUNKS, :]], axis=0)

    def body(dy_ref, u_ref, s_ref, cr_ref, ci_ref, wr_ref, wi_ref, lr_ref, li_ref,
             dlr_ref, dli_ref, dwdt_ref, dcmt_ref, du_ref, ar_ref, ai_ref):
        dy = dy_ref[...]
        ar_ref[...] = jnp.dot(dy, cr_ref[...], preferred_element_type=F32)
        ai_ref[...] = jnp.dot(dy, ci_ref[...], preferred_element_type=F32)
        _scan_block(ar_ref, ai_ref, ar_ref, ai_ref, lr_ref[...], -li_ref[...], True)
        a_r, a_i = ar_ref[...], ai_ref[...]
        sr_ref, si_ref = s_ref.at[0], s_ref.at[1]
        spr, spi = prev(sr_ref), prev(si_ref)
        dlr_ref[...] = jnp.sum(a_r * spr + a_i * spi, axis=0, keepdims=True)
        dli_ref[...] = jnp.sum(a_i * spr - a_r * spi, axis=0, keepdims=True)
        a_rb, a_ib = a_r.astype(BF16), a_i.astype(BF16)
        u = u_ref[...].astype(BF16)
        dwdt_ref[0] = tn(a_rb, u)
        dwdt_ref[1] = tn(a_ib, u)
        dcmt_ref[0] = tn(dy, sr_ref[...].astype(BF16))
        dcmt_ref[1] = tn(dy, si_ref[...].astype(BF16))
        part = (jnp.dot(a_rb, wr_ref[...], preferred_element_type=F32)
                + jnp.dot(a_ib, wi_ref[...], preferred_element_type=F32))

        @pl.when(pl.program_id(0) == 0)
        def _():
            du_ref[...] = part

        @pl.when(pl.program_id(0) > 0)
        def _():
            du_ref[...] += part

    lam_spec = pl.BlockSpec((1, bl), lambda i: (0, i))
    return pl.pallas_call(
        body, name=name, grid=(nblk,),
        in_specs=[pl.BlockSpec((SEQ, SW), lambda i: (0, 0)), pl.BlockSpec((SEQ, SW), lambda i: (0, CB_U)),
                  pl.BlockSpec((2, SEQ, bl), lambda i: (0, 0, i)),
                  pl.BlockSpec((SW, bl), lambda i: (0, i)), pl.BlockSpec((SW, bl), lambda i: (1, i)),
                  pl.BlockSpec((bl, SW), lambda i: (i, 0)), pl.BlockSpec((bl, SW), lambda i: (nblk + i, 0)),
                  lam_spec, lam_spec],
        out_specs=[lam_spec, lam_spec, pl.BlockSpec((2, bl, SW), lambda i: (0, i, 0)),
                   pl.BlockSpec((2, SW, bl), lambda i: (0, 0, i)), pl.BlockSpec((SEQ, SW), lambda i: (0, 0))],
        out_shape=[jax.ShapeDtypeStruct((1, NS), F32), jax.ShapeDtypeStruct((1, NS), F32),
                   jax.ShapeDtypeStruct((2, NS, SW), F32), jax.ShapeDtypeStruct((2, SW, NS), F32),
                   jax.ShapeDtypeStruct((SEQ, SW), F32)],
        scratch_shapes=[pltpu.VMEM((SEQ, bl), F32)] * 2,
        compiler_params=_params("arbitrary"),
    )(dy0, proj, s, cmt, cmt, wdt, wdt, lam_r, lam_i)


def _scores(qb, kb, prev):
    s = lax.dot_general(qb, kb, (((1,), (1,)), ((), ())), preferred_element_type=F32) * (HD ** -0.5)
    row = lax.broadcasted_iota(jnp.int32, (ABLK, ABLK), 0)
    col = lax.broadcasted_iota(jnp.int32, (ABLK, ABLK), 1)
    return jnp.where((col >= row) if prev else (col <= row), s, -1e30)


def _block_rows(dil, r, b):
    if dil == 1:
        return pl.ds(pl.multiple_of(b * ABLK, ABLK), ABLK)
    return pl.ds(r + dil * ABLK * b, ABLK, stride=dil)


def _group_blocks(dil):
    nb = SEQ // dil // ABLK
    shift = nb.bit_length() - 1
    return nb, (lambda idx: (idx >> shift, idx & (nb - 1)))


def _qkv_specs(j_of):
    return [pl.BlockSpec((SEQ, HD), functools.partial(lambda j, c: (0, c + j_of(j)), c=(cb + g) * 4))
            for g in range(3) for cb in (CB_Q, CB_K, CB_V)]


def _attention_fwd(name, proj):
    def body(*refs):
        qkv, z_ref = refs[:9], refs[9]
        y_ref, ya_ref, l_ref = refs[10:13]
        accs, maxs, dens = refs[13:16], refs[16:19], refs[19:22]
        for g, dil in enumerate(DILATIONS):
            q_ref, k_ref, v_ref = qkv[3 * g:3 * g + 3]
            nb, where = _group_blocks(dil)

            def step(t, c, g=g, dil=dil, nb=nb, where=where, q_ref=q_ref, k_ref=k_ref, v_ref=v_ref):
                two = range(ATTN_PAIR)
                rb = [where(t + i * (SEQ // ABLK // ATTN_PAIR)) for i in two]
                rows = [_block_rows(dil, r, b) for r, b in rb]
                qb = [q_ref[rows[i], :].astype(BF16) for i in two]
                s_c = [_scores(qb[i], k_ref[rows[i], :].astype(BF16), False) for i in two]
                if nb > 1:
                    prev = [_block_rows(dil, r, jnp.maximum(b - 1, 0)) for r, b in rb]
                    s_p = [jnp.where(rb[i][1] > 0, _scores(qb[i], k_ref[prev[i], :].astype(BF16), True), -1e30)
                           for i in two]
                m = [jnp.max(s_c[i], axis=-1, keepdims=True) for i in two]
                if nb > 1:
                    m = [jnp.maximum(m[i], jnp.max(s_p[i], axis=-1, keepdims=True)) for i in two]
                p_c = [jnp.exp(s_c[i] - m[i]) for i in two]
                den = [jnp.sum(p_c[i], axis=-1, keepdims=True) for i in two]
                acc = [jnp.dot(p_c[i].astype(BF16), v_ref[rows[i], :].astype(BF16), preferred_element_type=F32)
                       for i in two]
                if nb > 1:
                    p_p = [jnp.exp(s_p[i] - m[i]) for i in two]
                    den = [den[i] + jnp.sum(p_p[i], axis=-1, keepdims=True) for i in two]
                    acc = [acc[i] + jnp.dot(p_p[i].astype(BF16), v_ref[prev[i], :].astype(BF16),
                                            preferred_element_type=F32) for i in two]
                for i in two:
                    accs[g][rows[i], :] = acc[i]
                    maxs[g][rows[i], :] = jnp.broadcast_to(m[i], (ABLK, HD))
                    dens[g][rows[i], :] = jnp.broadcast_to(den[i], (ABLK, HD))
                return c

            lax.fori_loop(0, SEQ // ABLK // ATTN_PAIR, step, 0)
        top = jnp.maximum(jnp.maximum(maxs[0][...], maxs[1][...]), maxs[2][...])
        den = jnp.zeros((SEQ, HD), F32)
        y = jnp.zeros((SEQ, HD), F32)
        for g in range(3):
            wgt = jnp.exp(maxs[g][...] - top)
            den = den + wgt * dens[g][...]
            y = y + wgt * accs[g][...]
        y = y / den
        y_ref[...] = y
        ya_ref[...] = (y * _silu(z_ref[...])).astype(ya_ref.dtype)
        l_ref[...] = top + jnp.log(den)

    ospec = pl.BlockSpec((SEQ, HD), lambda j: (0, j))
    return pl.pallas_call(
        body, name=name, grid=(AW // HD,),
        in_specs=_qkv_specs(lambda j: j) + [pl.BlockSpec((SEQ, HD), lambda j: (0, CB_ZA * 4 + j))],
        out_specs=[ospec, ospec, ospec],
        out_shape=[jax.ShapeDtypeStruct((SEQ, AW), F32), jax.ShapeDtypeStruct((SEQ, AW), BF16),
                   jax.ShapeDtypeStruct((SEQ, AW), F32)],
        scratch_shapes=[pltpu.VMEM((SEQ, HD), F32)] * 9,
        compiler_params=_params("parallel"),
    )(*([proj] * 10))


def _attention_bwd(name, proj, dya, y, lse):
    def tn(a, b_):
        return lax.dot_general(a, b_, (((0,), (0,)), ((), ())), preferred_element_type=F32)

    def nt(a, b_):
        return lax.dot_general(a, b_, (((1,), (1,)), ((), ())), preferred_element_type=F32)

    def body(*refs):
        qkv, z_ref, dya_ref, y_ref, l_ref = refs[:9], refs[9], refs[10], refs[11], refs[12]
        outs, dza_ref = refs[13:22], refs[22]
        dy_s, dsum_s, dq_s, dk_own, dv_own, dk_prev, dv_prev = refs[23:]
        _, vjp = jax.vjp(lambda y_, z_: y_ * _silu(z_), y_ref[...], z_ref[...])
        dy, dz = vjp(dya_ref[...])
        dza_ref[...] = dz.astype(dza_ref.dtype)
        dy_s[...] = dy
        dsum_s[...] = jnp.broadcast_to(jnp.sum(dy * y_ref[...], axis=-1, keepdims=True), (SEQ, HD))
        for g, dil in enumerate(DILATIONS):
            q_ref, k_ref, v_ref = qkv[3 * g:3 * g + 3]
            nb, where = _group_blocks(dil)
            if nb > 1:
                dk_prev[...] = jnp.zeros(dk_prev.shape, F32)
                dv_prev[...] = jnp.zeros(dv_prev.shape, F32)

            def step(t, c, dil=dil, nb=nb, where=where, q_ref=q_ref, k_ref=k_ref, v_ref=v_ref):
                rb = [where(t + i * (SEQ // ABLK // ATTN_PAIR)) for i in range(ATTN_PAIR)]
                sides = []
                for r, b in rb:
                    rows = _block_rows(dil, r, b)
                    own = dict(b=b, qrows=rows, krows=rows, prev=False, dk=dk_own, dv=dv_own,
                               q=q_ref[rows, :].astype(BF16), dy=dy_s[rows, :].astype(BF16))
                    sides.append(own)
                    if nb > 1:
                        sides.append(dict(own, krows=_block_rows(dil, r, jnp.maximum(b - 1, 0)), prev=True,
                                          dk=dk_prev, dv=dv_prev))
                for s_ in sides:
                    s_["k"] = k_ref[s_["krows"], :].astype(BF16)
                    s_["v"] = v_ref[s_["krows"], :].astype(BF16)
                for s_ in sides:
                    sc = _scores(s_["q"], s_["k"], s_["prev"])
                    s_["s"] = jnp.where(s_["b"] > 0, sc, -1e30) if s_["prev"] else sc
                    s_["dp"] = nt(s_["dy"], s_["v"])
                for s_ in sides:
                    p = jnp.exp(s_["s"] - l_ref[s_["qrows"], :])
                    s_["p"] = p.astype(BF16)
                    s_["ds"] = (p * (s_["dp"] - dsum_s[s_["qrows"], :]) * (HD ** -0.5)).astype(BF16)
                for s_ in sides:
                    s_["dk"][s_["krows"], :] = tn(s_["ds"], s_["q"])
                    s_["dv"][s_["krows"], :] = tn(s_["p"], s_["dy"])
                    s_["dq"] = jnp.dot(s_["ds"], s_["k"], preferred_element_type=F32)
                per = len(sides) // ATTN_PAIR
                for i in range(ATTN_PAIR):
                    dq = sides[i * per]["dq"]
                    if per > 1:
                        dq = dq + sides[i * per + 1]["dq"]
                    dq_s[sides[i * per]["qrows"], :] = dq
                return c

            lax.fori_loop(0, SEQ // ABLK // ATTN_PAIR, step, 0)
            dq_ref, dk_ref, dv_ref = outs[3 * g:3 * g + 3]
            dq_ref[...] = dq_s[...].astype(dq_ref.dtype)
            if nb > 1:
                dk_ref[...] = (dk_own[...] + dk_prev[...]).astype(dk_ref.dtype)
                dv_ref[...] = (dv_own[...] + dv_prev[...]).astype(dv_ref.dtype)
            else:
                dk_ref[...] = dk_own[...].astype(dk_ref.dtype)
                dv_ref[...] = dv_own[...].astype(dv_ref.dtype)

    ospec = pl.BlockSpec((SEQ, HD), lambda j: (0, j))
    outs = pl.pallas_call(
        body, name=name, grid=(AW // HD,),
        in_specs=_qkv_specs(lambda j: j) + [pl.BlockSpec((SEQ, HD), lambda j: (0, CB_ZA * 4 + j))] + [ospec] * 3,
        out_specs=[ospec] * 10, out_shape=[jax.ShapeDtypeStruct((SEQ, AW), BF16)] * 10,
        scratch_shapes=[pltpu.VMEM((SEQ, HD), F32)] * 7,
        compiler_params=_params("parallel"),
    )(*([proj] * 10), dya, y, lse)
    return outs[:9], outs[9]


def _rms(x, g):
    return x * lax.rsqrt(jnp.mean(x * x, axis=-1, keepdims=True) + RMS_EPS) * g


def _sig(x):
    return 1.0 / (1.0 + jnp.exp(-x))


def _silu(x):
    return x * _sig(x)


def _gelu(x):
    return 0.5 * x * (1.0 + jnp.tanh(math.sqrt(2.0 / math.pi) * (x + 0.044715 * (x * x * x))))


def _y1_fn(y0p, u, dskip):
    return _gelu(y0p + dskip * u)


def _ys_fn(y1, t, z, bglu):
    return y1 * _sig(t + bglu) * _silu(z)


def _merge_fn(ms, ma, gs, ga):
    return _sig(gs) * ms + _sig(ga) * ma


def _colsum(v):
    return jnp.sum(v, axis=0, keepdims=True)


def _lam_fn(lre, lim, ldt):
    a = jnp.minimum(lre, -1e-4)
    dt = jnp.exp(ldt)
    mag = jnp.exp(a * dt)
    ar = mag * jnp.cos(lim * dt)
    ai = mag * jnp.sin(lim * dt)
    den = a * a + lim * lim
    cr = ((ar - 1.0) * a + ai * lim) / den
    ci = (ai * a - (ar - 1.0) * lim) / den
    return ar, ai, cr, ci


def _bbar_fn(cr, ci, bre, bim):
    return cr * bre - ci * bim, cr * bim + ci * bre


def _same_group(rows, a, cols, b):
    r = lax.broadcasted_iota(jnp.int32, (rows, cols), 0) >> (a.bit_length() - 1)
    c = lax.broadcasted_iota(jnp.int32, (rows, cols), 1) >> (b.bit_length() - 1)
    return r == c


def _expand(name, blocks, signs, a, b, dtype, after=None):
    rows, cols = NGRP * a, NGRP * b
    n = len(blocks)
    assert a & (a - 1) == 0 and b & (b - 1) == 0
    after_specs, after_args = _after(after)

    def body(*refs):
        o_ref = refs[-1]
        tile = (lax.broadcasted_iota(jnp.int32, (b, cols), 1) & (b - 1)
                == lax.broadcasted_iota(jnp.int32, (b, cols), 0)).astype(F32)
        keep = _same_group(rows, a, cols, b)
        for i, (ref, sign) in enumerate(zip(refs[:n], signs)):
            spread = jnp.dot(ref[...], tile, preferred_element_type=F32, precision=lax.Precision.HIGHEST)
            o_ref[i * rows:(i + 1) * rows, :] = jnp.where(keep, sign * spread, 0.0).astype(o_ref.dtype)

    return pl.pallas_call(body, name=name, grid=(1,),
                          in_specs=[pl.BlockSpec((rows, b), lambda i: (0, 0))] * n + after_specs,
                          out_specs=pl.BlockSpec((n * rows, cols), lambda i: (0, 0)),
                          out_shape=jax.ShapeDtypeStruct((n * rows, cols), dtype),
                          compiler_params=_params("arbitrary"))(*blocks, *after_args)


def _extract(name, m, a, b, at=(0, 0), after=None):
    rows, cols = NGRP * a, NGRP * b
    assert a & (a - 1) == 0 and b & (b - 1) == 0
    after_specs, after_args = _after(after)

    def body(m_ref, *rest):
        o_ref = rest[-1]
        tile = (lax.broadcasted_iota(jnp.int32, (cols, b), 0) & (b - 1)
                == lax.broadcasted_iota(jnp.int32, (cols, b), 1)).astype(F32)
        kept = jnp.where(_same_group(rows, a, cols, b), m_ref[...], 0.0)
        o_ref[...] = jnp.dot(kept, tile, preferred_element_type=F32, precision=lax.Precision.HIGHEST)

    return pl.pallas_call(body, name=name, grid=(1,),
                          in_specs=[pl.BlockSpec((rows, cols), lambda i: at)] + after_specs,
                          out_specs=pl.BlockSpec((rows, b), lambda i: (0, 0)),
                          out_shape=jax.ShapeDtypeStruct((rows, b), F32),
                          compiler_params=_params("arbitrary"))(m, *after_args)


def _s5_prepare(l, sp, after):
    tag = f"l{l}_"
    ar, ai, cr, ci = _ew(tag + "lam", _lam_fn, NGRP, NGRP,
                         [_ri(sp["lambda_re"]), _ri(sp["lambda_im"]),
                          _ri(sp["log_dt"].reshape(NGRP, 1) + after[0, 0])], [], [(NST, F32)] * 4)
    bre = sp["b_re"].reshape(NS, GCH)
    bim = sp["b_im"].reshape(NS, GCH)
    bbr, bbi = _ew(tag + "bbar", _bbar_fn, NS, NS, [_ri(cr.reshape(NS, 1)), _ri(ci.reshape(NS, 1)), _ri(bre), _ri(bim)],
                   [], [(GCH, F32)] * 2)
    wdt = _expand(tag + "wdt", [bbr, bbi], [1.0, 1.0], NST, GCH, BF16)
    cmt = _expand(tag + "cmt", [sp["c_re"].reshape(SW, NST), sp["c_im"].reshape(SW, NST)], [1.0, -1.0], GCH, NST, BF16,
                  after=after)
    return dict(ar=ar, ai=ai, cr=cr, ci=ci, wdt=wdt, cmt=cmt)


def _layer_fwd(l, x, w, sp, prep):
    tag = f"l{l}_"
    g1 = sp["pre_norm_g"].reshape(1, DM)
    (h,) = _ew(tag + "rms1", lambda x_, g: (_rms(x_, g),), SEQ, 256, [_ri(x)], [g1], [(DM, BF16)])
    hv = jnp.stack([h, _to_chunked(h)])
    win = w["w_in"]
    proj = _mm(tag + "proj", hv, win, "nn", SEQ, NCOL, DM, SEQ, 512, 1024, F32,
               a_spec=pl.BlockSpec((None, SEQ, 1024), lambda i, j, k: (_row_order(j), 0, 0)),
               b_spec=pl.BlockSpec((None, 1024, 512), lambda i, j, k: (j // 2, 0, j % 2)))

    ar, ai, cr, ci, wdt, cmt = (prep[k] for k in ("ar", "ai", "cr", "ci", "wdt", "cmt"))
    s, y0p = _s5_forward(tag + "s5", proj, wdt, cmt, ar.reshape(1, NS), ai.reshape(1, NS))
    dskip = sp["d_skip"].reshape(1, SW)
    (y1,) = _ew(tag + "y1", lambda a, u, d: (_y1_fn(a, u, d),), SEQ, 256,
                [_ri(y0p), _ri(proj, SW, CB_U)], [dskip], [(SW, F32)])
    t = _mm(tag + "glu", y1, w["w_glu"], "nn", SEQ, SW, SW, 1024, 512, 512, F32)
    bglu = sp["b_glu"].reshape(1, SW)
    (ys_c,) = _ew(tag + "ys", lambda y1_, t_, z, b_: (_ys_fn(y1_, t_, z, b_),), SEQ, 256,
                  [_ri(y1), _ri(t), _ri(proj, SW, CB_ZS)], [bglu], [(SW, BF16)])
    ys = _from_chunked(ys_c)

    ypre, ya, lse = _attention_fwd(tag + "attn", proj)

    ms = _mm(tag + "branch_s", ys, w["w_branch_s"], "nn", SEQ, DM, SW, 1024, 1024, 512, F32)
    ma = _mm(tag + "branch_a", ya, w["w_branch_a"], "nn", SEQ, DM, AW, 1024, 1024, 512, F32)
    (merged,) = _ew(tag + "merge", lambda a, b_, c, d: (_merge_fn(a, b_, c, d),), SEQ, 256,
                    [_ri(ms), _ri(ma), _ri(proj, DM, CB_GS), _ri(proj, DM, CB_GA)], [], [(DM, BF16)])
    out = _mm(tag + "out", merged, w["w_out"], "nn", SEQ, DM, DM, 1024, 1024, 1024, F32)
    g2 = sp["post_norm_g"].reshape(1, DM)
    (x_new,) = _ew(tag + "post", lambda x_, o, g: (x_ + _rms(o, g),), SEQ, 256, [_ri(x), _ri(out)], [g2], [(DM, F32)])
    res = dict(x=x, hv=hv, proj=proj, ar=ar, ai=ai, cr=cr, ci=ci, wdt=wdt, cmt=cmt, s=s, y0p=y0p,
               y1=y1, t=t, ys=ys, ya=ya, ypre=ypre, lse=lse, ms=ms, ma=ma, merged=merged, out=out)
    return x_new, res


def _layer_bwd(l, dxn, r, w, sp, big_done):
    tag = f"l{l}b_"
    proj = r["proj"]
    g1 = sp["pre_norm_g"].reshape(1, DM)
    g2 = sp["post_norm_g"].reshape(1, DM)
    dskip = sp["d_skip"].reshape(1, SW)
    bglu = sp["b_glu"].reshape(1, SW)

    def post_b(d, o, g):
        _, vjp = jax.vjp(_rms, o, g)
        do, dg = vjp(d)
        return do, dg

    d_out, dg2 = _ew(tag + "post", post_b, SEQ, 256, [_ri(dxn), _ri(r["out"])], [g2], [(DM, BF16)], [DM])
    dw_out = _mm(tag + "dw_out", r["merged"], d_out, "tn", DM, DM, SEQ, 1024, 1024, SEQ, BF16)
    dmerged = _mm(tag + "dmerged", d_out, w["w_out"], "nt", SEQ, DM, DM, 1024, 1024, 1024, F32)

    def merge_b(d, ms, ma, gs, ga):
        _, vjp = jax.vjp(_merge_fn, ms, ma, gs, ga)
        return vjp(d)

    dms, dma, dgs, dga = _ew(tag + "merge", merge_b, SEQ, 256,
                             [_ri(dmerged), _ri(r["ms"]), _ri(r["ma"]), _ri(proj, DM, CB_GS), _ri(proj, DM, CB_GA)],
                             [], [(DM, BF16)] * 4)
    dw_bs = _mm(tag + "dw_bs", r["ys"], dms, "tn", SW, DM, SEQ, 512, 1024, SEQ, BF16)
    dw_ba = _mm(tag + "dw_ba", r["ya"], dma, "tn", AW, DM, SEQ, 512, 1024, SEQ, BF16)
    dys = _mm(tag + "dys", dms, w["w_branch_s"], "nt", SEQ, SW, DM, 1024, 512, 1024, F32)
    dya = _mm(tag + "dya", dma, w["w_branch_a"], "nt", SEQ, AW, DM, 1024, 512, 1024, F32)

    dqkv, dza = _attention_bwd(tag + "attn", proj, dya, r["ypre"], r["lse"])

    def ys_b(d, y1, t, z, b_):
        _, vjp = jax.vjp(_ys_fn, y1, t, z, b_)
        dy1, dt, dz, _ = vjp(d)
        return dy1, dt, dz, _colsum(dt)

    dy1a, dt, dzs, dbglu = _ew(tag + "ys", ys_b, SEQ, 256,
                               [_ri(_to_chunked(dys)), _ri(r["y1"]), _ri(r["t"]), _ri(proj, SW, CB_ZS)],
                               [bglu], [(SW, F32), (SW, BF16), (SW, BF16)], [SW])
    dw_glu = _mm(tag + "dw_glu", r["y1"], dt, "tn", SW, SW, SEQ, 512, 512, SEQ, BF16)
    dy1b = _mm(tag + "dy1b", dt, w["w_glu"], "nt", SEQ, SW, SW, 1024, 512, 512, F32)

    def y1_b(da, db, y0p, u, d_):
        _, vjp = jax.vjp(_y1_fn, y0p, u, d_)
        dy0, du, dd = vjp(da + db)
        return dy0, du, dd

    dy0, du_skip, ddskip = _ew(tag + "y1", y1_b, SEQ, 256,
                               [_ri(dy1a), _ri(dy1b), _ri(r["y0p"]), _ri(proj, SW, CB_U)], [dskip],
                               [(SW, BF16), (SW, F32)], [SW])
    dlr, dli, dwdt, dcmt, du_s = _s5_backward(tag + "s5", dy0, proj, r["s"], r["wdt"], r["cmt"],
                                              r["ar"].reshape(1, NS), r["ai"].reshape(1, NS))
    (du,) = _ew(tag + "du", lambda a, c: (a + c,), SEQ, 256, [_ri(du_s), _ri(du_skip)], [], [(SW, BF16)])

    dq, dk, dv = ([dqkv[3 * g + i] for g in range(3)] for i in range(3))
    dproj = jnp.concatenate([du, dzs, *dq, *dk, *dv, dza, dgs, dga], axis=1)
    dw_in = _mm(tag + "dw_in", r["hv"], dproj, "tn", DM, NCOL, SEQ, 1024, 512, SEQ, BF16,
                a_spec=pl.BlockSpec((None, SEQ, 1024), lambda i, j, k: (_row_order(j), 0, 0)),
                o_spec=pl.BlockSpec((None, 1024, 512), lambda i, j, k: (j // 2, 0, j % 2)), out_shape=(NDEV, DM, DM))
    tok = big_done(l, dict(w_in=dw_in, w_glu=dw_glu.reshape(NDEV, SW // NDEV, SW),
                           w_branch_s=dw_bs.reshape(SW, NDEV, DM // NDEV).transpose(1, 0, 2),
                           w_branch_a=dw_ba.reshape(AW, NDEV, DM // NDEV).transpose(1, 0, 2),
                           w_out=dw_out.reshape(NDEV, DM // NDEV, DM)))
    if tok is not None:
        g1 = g1 + tok[0, 0]

    dwdt = dwdt.reshape(2 * NS, SW)
    dcmt = dcmt.reshape(2 * SW, NS)
    dbbr = _extract(tag + "dbbr", dwdt, NST, GCH, (0, 0), after=tok)
    dbbi = _extract(tag + "dbbi", dwdt, NST, GCH, (1, 0), after=tok)
    bre = sp["b_re"].reshape(NS, GCH)
    bim = sp["b_im"].reshape(NS, GCH)

    def bbar_b(cr, ci, br_, bi_, dr, di):
        _, vjp = jax.vjp(_bbar_fn, cr, ci, br_, bi_)
        return vjp((dr, di))

    dcr, dci, dbre, dbim = _ew(tag + "bbar", bbar_b, NS, NS,
                               [_ri(r["cr"].reshape(NS, 1)), _ri(r["ci"].reshape(NS, 1)), _ri(bre), _ri(bim),
                                _ri(dbbr), _ri(dbbi)], [], [(1, F32), (1, F32), (GCH, F32), (GCH, F32)])

    def lam_b(lre, lim, ldt, dar, dai, dcr_, dci_):
        _, vjp = jax.vjp(_lam_fn, lre, lim, ldt)
        return vjp((dar, dai, dcr_, dci_))

    dlre, dlim, dldt = _ew(tag + "lam", lam_b, NGRP, NGRP,
                           [_ri(sp["lambda_re"]), _ri(sp["lambda_im"]), _ri(sp["log_dt"].reshape(NGRP, 1)),
                            _ri(dlr.reshape(NGRP, NST)), _ri(dli.reshape(NGRP, NST)),
                            _ri(dcr.reshape(NGRP, NST)), _ri(dci.reshape(NGRP, NST))], [],
                           [(NST, F32), (NST, F32), (1, F32)])
    dc_re = _extract(tag + "dc_re", dcmt, GCH, NST, (0, 0), after=tok).reshape(NGRP, GCH, NST)
    dc_im = -_extract(tag + "dc_im", dcmt, GCH, NST, (1, 0), after=tok).reshape(NGRP, GCH, NST)

    dh_time = _mm(tag + "dh_time", dproj, w["w_in"], "nt", SEQ, DM, NCOL - DM, SEQ, 1024, 1024, F32,
                  a_spec=pl.BlockSpec((SEQ, 1024), lambda i, j, k: (0, 1 + k)),
                  b_spec=pl.BlockSpec((None, 1024, 1024), lambda i, j, k: (1 + k, 0, 0)), after=tok)
    dh_chunked = _mm(tag + "dh_chunked", dproj, w["w_in"], "nt", SEQ, DM, DM, SEQ, 1024, 1024, F32,
                     a_spec=pl.BlockSpec((SEQ, 1024), lambda i, j, k: (0, 0)),
                     b_spec=pl.BlockSpec((None, 1024, 1024), lambda i, j, k: (0, 0, 0)), after=tok)
    dh = [dh_time, _from_chunked(dh_chunked)]

    def pre_b(d, dh0, dh1, x_, g):
        _, vjp = jax.vjp(_rms, x_, g)
        dx_, dg = vjp(dh0 + dh1)
        return d + dx_, dg

    dx, dg1 = _ew(tag + "pre", pre_b, SEQ, 256, [_ri(dxn)] + [_ri(t_) for t_ in dh] + [_ri(r["x"])], [g1],
                  [(DM, F32)], [DM])

    small = dict(pre_norm_g=dg1.reshape(DM), lambda_re=dlre, lambda_im=dlim, log_dt=dldt.reshape(NGRP),
                 b_re=dbre.reshape(NGRP, NST, GCH), b_im=dbim.reshape(NGRP, NST, GCH), c_re=dc_re, c_im=dc_im,
                 d_skip=ddskip.reshape(SW), b_glu=dbglu.reshape(SW), post_norm_g=dg2.reshape(DM))
    return dx, small


_HBM = pl.BlockSpec(memory_space=pltpu.HBM)
_SEM = pl.BlockSpec(memory_space=pltpu.SEMAPHORE)
_EFFECT = pltpu.SideEffectType.DATAFLOW_SIDE_EFFECTING


def _remote_copies(srcs, dsts, send_sems, recv_sems, gather):
    x, y, c = lax.axis_index("x"), lax.axis_index("y"), lax.axis_index("c")
    me = 4 * x + 2 * y + c
    copies = []
    for i in range(len(srcs)):
        for k in range(1, NDEV):
            peer = (x ^ (k >> 2), y ^ ((k >> 1) & 1), c ^ (k & 1))
            src = srcs[i] if gather[i] else srcs[i].at[me ^ k]
            copies.append(pltpu.make_async_remote_copy(
                src_ref=src, dst_ref=dsts[i].at[me], send_sem=send_sems[i], recv_sem=recv_sems[i],
                device_id=peer, device_id_type=pl.DeviceIdType.MESH))
    return copies


def _all_seven(dst, send_sem, recv_sem):
    seven = dst.at[pl.ds(0, NDEV - 1)]
    me = (lax.axis_index("x"), lax.axis_index("y"), lax.axis_index("c"))
    return pltpu.make_async_remote_copy(src_ref=seven, dst_ref=seven, send_sem=send_sem, recv_sem=recv_sem,
                                        device_id=me, device_id_type=pl.DeviceIdType.MESH)


def _own_slabs(name, arrs, gather, after):
    n = len(arrs)
    me = (4 * lax.axis_index("x") + 2 * lax.axis_index("y") + lax.axis_index("c")).astype(jnp.int32).reshape(1)

    def body(me_ref, *refs):
        for src, dst in zip(refs[:n], refs[n + 1:]):
            dst[...] = src[...]

    def zeros(k):
        return (0,) * k

    in_specs, out_specs, out_shape = [], [], []
    for a, g in zip(arrs, gather):
        slab = a.shape if g else a.shape[1:]
        nd = len(slab)
        if g:
            in_specs.append(pl.BlockSpec(slab, functools.partial(lambda i, me_ref, nd: zeros(nd), nd=nd)))
        else:
            in_specs.append(pl.BlockSpec((None,) + slab, functools.partial(lambda i, me_ref, nd: (me_ref[0],) + zeros(nd), nd=nd)))
        out_specs.append(pl.BlockSpec((None,) + slab, functools.partial(lambda i, me_ref, nd: (me_ref[0],) + zeros(nd), nd=nd)))
        out_shape.append(jax.ShapeDtypeStruct((NDEV,) + slab, a.dtype))
    in_specs.append(pl.BlockSpec(memory_space=pl.ANY))
    return pl.pallas_call(
        body, name=name, out_shape=out_shape,
        grid_spec=pltpu.PrefetchScalarGridSpec(num_scalar_prefetch=1, grid=(1,), in_specs=in_specs, out_specs=out_specs),
        compiler_params=_params("arbitrary"),
    )(me, *arrs, after)


def _exchange_start(name, arrs, gather, after):
    n = len(arrs)
    lands = _own_slabs(name + "_own", arrs, gather, after)

    def body(*refs):
        srcs, dsts = refs[:n], refs[n:2 * n]
        send_sems, recv_sems = refs[2 * n:3 * n], refs[3 * n:4 * n]
        token = refs[-1]
        for cp in _remote_copies(srcs, dsts, send_sems, recv_sems, gather):
            cp.start()
        token[...] = jnp.zeros(token.shape, token.dtype)

    thru = [pltpu.HBM(a.shape, a.dtype) for a in list(arrs) + list(lands)]
    outs = pl.pallas_call(
        body, name=name,
        out_shape=(*[pltpu.SemaphoreType.DMA(())] * (2 * n), *thru, jax.ShapeDtypeStruct((8, 128), F32)),
        in_specs=[_HBM] * (2 * n),
        out_specs=(*[_SEM] * (2 * n), *[_HBM] * (2 * n), pl.BlockSpec(memory_space=pltpu.VMEM)),
        input_output_aliases={i: 2 * n + i for i in range(2 * n)},
        compiler_params=pltpu.CompilerParams(has_side_effects=_EFFECT),
    )(*[pltpu.with_memory_space_constraint(a, pltpu.HBM) for a in list(arrs) + list(lands)])
    return dict(send=outs[:n], recv=outs[n:2 * n], srcs=outs[2 * n:3 * n], lands=outs[3 * n:4 * n], token=outs[-1],
                gather=gather)


def _exchange_wait(name, started, after):
    n = len(started["srcs"])
    after = list(after)

    def body(*refs):
        dsts = refs[n:2 * n]
        send_sems, recv_sems = refs[2 * n:3 * n], refs[3 * n:4 * n]
        for i in range(n):
            cp = _all_seven(dsts[i], send_sems[i], recv_sems[i])
            cp.wait_send()
            cp.wait_recv()

    bufs = list(started["srcs"]) + list(started["lands"])
    outs = pl.pallas_call(
        body, name=name, out_shape=tuple(pltpu.HBM(a.shape, a.dtype) for a in bufs),
        in_specs=[_HBM] * (2 * n) + [_SEM] * (2 * n) + [pl.BlockSpec(memory_space=pl.ANY)] * len(after),
        out_specs=(_HBM,) * (2 * n), input_output_aliases={i: i for i in range(2 * n)},
        compiler_params=pltpu.CompilerParams(has_side_effects=_EFFECT),
    )(*bufs, *started["send"], *started["recv"], *after)
    return outs[n:]


def _sum_in_order(parts):
    g = parts[0].astype(F32)
    for p in parts[1:]:
        g = g + p.astype(F32)
    return g


def _adamw(name, parts, nparts, w, m, v, br):
    rows, cols = w.shape

    def fn(*a):
        g = _sum_in_order(a[:nparts])
        w_, m_, v_ = a[nparts:]
        m2 = B1 * m_ + (1.0 - B1) * g
        v2 = B2 * v_ + (1.0 - B2) * (g * g)
        m_hat = m2 / (1.0 - B1 ** STEP)
        v_hat = v2 / (1.0 - B2 ** STEP)
        delta = -LR * (m_hat / (jnp.sqrt(v_hat) + ADAM_EPS) + WD * w_)
        return g, delta, m2, v2

    ins = [_ri(parts, cols, 0, d * (rows // br)) for d in range(nparts)] + [_ri(w), _ri(m), _ri(v)]
    return _ew(name, fn, rows, br, ins, [], [(cols, F32)] * 4)


SMALL = ("pre_norm_g", "lambda_re", "lambda_im", "log_dt", "b_re", "b_im", "c_re", "c_im", "d_skip", "b_glu",
         "post_norm_g")
BIG = ("w_in", "w_glu", "w_branch_s", "w_branch_a", "w_out")
WEIGHTS = ("pre_norm_g", "w_in", "lambda_re", "lambda_im", "log_dt", "b_re", "b_im", "c_re", "c_im", "d_skip",
           "w_glu", "b_glu", "w_branch_s", "w_branch_a", "w_out", "post_norm_g")
PACK_COLS = 1024
PACK_BR = 136


def _pack_layer(d):
    pieces = [d[k].astype(F32).reshape(-1) for k in SMALL]
    used = sum(p.shape[0] for p in pieces)
    assert used <= PACK_BR * PACK_COLS
    return jnp.concatenate(pieces + [jnp.zeros((PACK_BR * PACK_COLS - used,), F32)]).reshape(PACK_BR, PACK_COLS)


def _unpack(p, like):
    flat = p.reshape(DEPTH, PACK_BR * PACK_COLS)
    out, off = {}, 0
    for k in SMALL:
        n = like[k].size // DEPTH
        out[k] = flat[:, off:off + n].reshape(like[k].shape)
        off += n
    return out


def _local_step(x, target, small, started, weights_of, big_done, small_done):
    preps = [_s5_prepare(l, {k: small[k][l] for k in SMALL}, started) for l in range(DEPTH)]
    res, ws = [], []
    for l in range(DEPTH):
        w_l, tok = weights_of(l, [x] + ([p[k] for p in preps for k in ("wdt", "cmt")] if l == 0 else []))
        sp = {k: small[k][l] for k in SMALL}
        if tok is not None:
            sp["pre_norm_g"] = sp["pre_norm_g"] + tok[0, 0]
        x, r = _layer_fwd(l, x, w_l, sp, preps[l])
        res.append(r)
        ws.append(w_l)

    def loss_fn(y, t):
        e = y - t
        return e * (1.0 / DM), jnp.sum(_colsum(0.5 * e * e * (1.0 / DM)), axis=1, keepdims=True)

    dx, loss = _ew("loss", loss_fn, SEQ, 256, [_ri(x), _ri(target)], [], [(DM, F32)], [1])
    for l in reversed(range(DEPTH)):
        dx, sm = _layer_bwd(l, dx, res[l], ws[l], {k: small[k][l] for k in SMALL}, big_done)
        small_done(l, sm)
    return loss.reshape(()), dx


def _full_weights(gathered):
    g = gathered
    return dict(
        w_in=g["w_in"],
        w_glu=g["w_glu"].reshape(SW, SW),
        w_branch_s=g["w_branch_s"].transpose(1, 0, 2).reshape(SW, DM),
        w_branch_a=g["w_branch_a"].transpose(1, 0, 2).reshape(AW, DM),
        w_out=g["w_out"].reshape(DM, DM),
    )


def kernel(x, pre_norm_g, w_in, lambda_re, lambda_im, log_dt, b_re, b_im, c_re, c_im, d_skip, w_glu, b_glu, w_branch_s, w_branch_a, w_out, post_norm_g, loss_target, m_pre_norm_g, m_w_in, m_lambda_re, m_lambda_im, m_log_dt, m_b_re, m_b_im, m_c_re, m_c_im, m_d_skip, m_w_glu, m_b_glu, m_w_branch_s, m_w_branch_a, m_w_out, m_post_norm_g, v_pre_norm_g, v_w_in, v_lambda_re, v_lambda_im, v_log_dt, v_b_re, v_b_im, v_c_re, v_c_im, v_d_skip, v_w_glu, v_b_glu, v_w_branch_s, v_w_branch_a, v_w_out, v_post_norm_g):
    wts = dict(pre_norm_g=pre_norm_g, w_in=w_in, lambda_re=lambda_re, lambda_im=lambda_im, log_dt=log_dt, b_re=b_re,
               b_im=b_im, c_re=c_re, c_im=c_im, d_skip=d_skip, w_glu=w_glu, b_glu=b_glu, w_branch_s=w_branch_s,
               w_branch_a=w_branch_a, w_out=w_out, post_norm_g=post_norm_g)
    mom = dict(pre_norm_g=m_pre_norm_g, w_in=m_w_in, lambda_re=m_lambda_re, lambda_im=m_lambda_im, log_dt=m_log_dt,
               b_re=m_b_re, b_im=m_b_im, c_re=m_c_re, c_im=m_c_im, d_skip=m_d_skip, w_glu=m_w_glu, b_glu=m_b_glu,
               w_branch_s=m_w_branch_s, w_branch_a=m_w_branch_a, w_out=m_w_out, post_norm_g=m_post_norm_g)
    var = dict(pre_norm_g=v_pre_norm_g, w_in=v_w_in, lambda_re=v_lambda_re, lambda_im=v_lambda_im, log_dt=v_log_dt,
               b_re=v_b_re, b_im=v_b_im, c_re=v_c_re, c_im=v_c_im, d_skip=v_d_skip, w_glu=v_w_glu, b_glu=v_b_glu,
               w_branch_s=v_w_branch_s, w_branch_a=v_w_branch_a, w_out=v_w_out, post_norm_g=v_post_norm_g)

    def gather_start(l, after):
        return _exchange_start(f"gather_start{l}", [wts[k][l].astype(BF16) for k in BIG], [True] * len(BIG), after)

    gathering = {0: gather_start(0, x)}
    sending, packed = {}, {}

    def weights_of(l, after):
        gathered = _exchange_wait(f"gather_wait{l}", gathering[l], after)
        tok = None
        if l + 1 < DEPTH:
            gathering[l + 1] = gather_start(l + 1, gathered[0])
            tok = gathering[l + 1]["token"]
        return _full_weights(dict(zip(BIG, gathered))), tok

    def big_done(l, big):
        arrs, kinds = [big[k] for k in BIG], [False] * len(BIG)
        if l + 1 < DEPTH:
            arrs, kinds = arrs + [packed[l + 1]], kinds + [True]
        sending[l] = _exchange_start(f"grads_start{l}", arrs, kinds, big["w_in"])
        return sending[l]["token"]

    def small_done(l, sm):
        packed[l] = _pack_layer(sm)

    loss, dx = _local_step(x[0], loss_target[0], wts, gathering[0]["token"], weights_of, big_done, small_done)
    loss = lax.psum(loss, ("x", "y", "c"))
    last = _exchange_start("grads_start_last", [packed[0]], [True], dx)
    recv_l = [_exchange_wait(f"grads_wait{l}", sending[l], [dx]) for l in range(DEPTH)]
    recv_last = _exchange_wait("grads_wait_last", last, [dx])
    recv = [jnp.stack([r[i] for r in recv_l], axis=1) for i in range(len(BIG))]
    recv.append(jnp.concatenate([recv_last[0]] + [recv_l[l][len(BIG)] for l in range(DEPTH - 1)], axis=1))

    grads, delta, new_m, new_v = {}, {}, {}, {}

    def update(k, parts, nparts):
        shape = wts[k].shape
        cols = shape[-1]
        rows = wts[k].size // cols
        br = min(rows, 1024 if cols <= 128 else 256)
        outs = _adamw("adamw_" + k, parts.reshape(nparts * rows, cols), nparts, wts[k].reshape(rows, cols),
                      mom[k].reshape(rows, cols), var[k].reshape(rows, cols), br)
        grads[k], delta[k], new_m[k], new_v[k] = (o.reshape(shape) for o in outs)

    for k, parts in zip(BIG, recv[:len(BIG)]):
        update(k, parts, NDEV)
    rows = DEPTH * PACK_BR
    (g_small,) = _ew("grads_small", lambda *p: (_sum_in_order(p),), rows, PACK_BR,
                     [_ri(recv[-1].reshape(NDEV * rows, PACK_COLS), PACK_COLS, 0, d * DEPTH) for d in range(NDEV)], [],
                     [(PACK_COLS, F32)])
    for k, g in _unpack(g_small, wts).items():
        update(k, g, 1)

    return (loss, dx[None], *[grads[k] for k in WEIGHTS], *[delta[k] for k in WEIGHTS],
            *[new_m[k] for k in WEIGHTS], *[new_v[k] for k in WEIGHTS])
```

```python
import functools
import math

import jax
import jax.numpy as jnp
from jax import lax
from jax.experimental import pallas as pl
from jax.experimental.pallas import tpu as pltpu

F32 = jnp.float32
BF16 = jnp.bfloat16

NDEV = 8
DEPTH = 4
SEQ = 2048
DM = 1024
NCOL = 8192
SW = 512
NGRP = 32
GCH = 16
NST = 64
NS = NGRP * NST
HD = 128
AW = 512
DILATIONS = (1, 4, 16)
ABLK = 128
ATTN_PAIR = 2
RMS_EPS = 1e-6
LR, B1, B2, ADAM_EPS, WD, STEP = 0.001, 0.9, 0.999, 1e-08, 0.01, 10

CB_U, CB_ZS, CB_Q, CB_K, CB_V, CB_ZA = 0, 1, 2, 5, 8, 11
CB_GS, CB_GA = 6, 7

VMEM_LIMIT = 56 * 2 ** 20


def _row_order(j):
    return jnp.where(j < CB_Q, 1, 0)


def _params(*sem):
    return pltpu.CompilerParams(dimension_semantics=sem, vmem_limit_bytes=VMEM_LIMIT)


def _ew(name, fn, rows, br, row_ins, bc_ins, row_outs, red_outs=()):
    n_in = len(row_ins) + len(bc_ins)
    n_ro = len(row_outs)
    steps = rows // br
    assert steps * br == rows

    def body(*refs):
        vals = fn(*[r[...] for r in refs[:n_in]])
        outs = refs[n_in:]
        for r, v in zip(outs[:n_ro], vals[:n_ro]):
            r[...] = v.astype(r.dtype)
        if red_outs:
            @pl.when(pl.program_id(0) == 0)
            def _():
                for r in outs[n_ro:]:
                    r[...] = jnp.zeros(r.shape, r.dtype)
            for r, v in zip(outs[n_ro:], vals[n_ro:]):
                r[...] += v

    in_specs = []
    for (_, w, cb, rb) in row_ins:
        in_specs.append(pl.BlockSpec((br, w), functools.partial(lambda i, cb, rb: (rb + i, cb), cb=cb, rb=rb)))
    for a in bc_ins:
        in_specs.append(pl.BlockSpec(a.shape, functools.partial(lambda i, nd: (0,) * nd, nd=a.ndim)))
    out_specs = [pl.BlockSpec((br, w), lambda i: (i, 0)) for (w, _) in row_outs]
    out_specs += [pl.BlockSpec((1, w), lambda i: (0, 0)) for w in red_outs]
    out_shape = [jax.ShapeDtypeStruct((rows, w), dt) for (w, dt) in row_outs]
    out_shape += [jax.ShapeDtypeStruct((1, w), F32) for w in red_outs]
    return pl.pallas_call(
        body, name=name, grid=(steps,), in_specs=in_specs, out_specs=out_specs, out_shape=out_shape,
        compiler_params=_params("arbitrary"),
    )(*[a for (a, _, _, _) in row_ins], *bc_ins)


def _ri(a, w=None, cb=0, rb=0):
    return (a, a.shape[1] if w is None else w, cb, rb)


_DIMS = {"nn": ((1,), (0,)), "nt": ((1,), (1,)), "tn": ((0,), (0,))}


def _after(after):
    return ([pl.BlockSpec(memory_space=pl.ANY)], [after]) if after is not None else ([], [])


def _mm(name, a, b, mode, M, N, K, bm, bn, bk, out_dtype, a_spec=None, b_spec=None, o_spec=None, out_shape=None,
        after=None):
    nk = K // bk
    assert M % bm == 0 and N % bn == 0 and nk * bk == K
    after_specs, after_args = _after(after)

    own_acc = nk > 1 and out_dtype != F32

    def body(a_ref, b_ref, *rest):
        o_ref, scratch = rest[len(after_args)], rest[len(after_args) + 1:]
        part = lax.dot_general(a_ref[...].astype(BF16), b_ref[...].astype(BF16), (_DIMS[mode], ((), ())),
                               preferred_element_type=F32)
        if nk == 1:
            o_ref[...] = part.astype(o_ref.dtype)
            return
        k = pl.program_id(2)
        acc_ref = scratch[0] if own_acc else o_ref

        @pl.when(k == 0)
        def _():
            acc_ref[...] = part

        @pl.when(k > 0)
        def _():
            acc_ref[...] += part

        if own_acc:
            @pl.when(k == nk - 1)
            def _():
                o_ref[...] = acc_ref[...].astype(o_ref.dtype)

    if a_spec is None:
        a_spec = (pl.BlockSpec((bk, bm), lambda i, j, k: (k, i)) if mode == "tn"
                  else pl.BlockSpec((bm, bk), lambda i, j, k: (i, k)))
    if b_spec is None:
        b_spec = (pl.BlockSpec((bn, bk), lambda i, j, k: (j, k)) if mode == "nt"
                  else pl.BlockSpec((bk, bn), lambda i, j, k: (k, j)))
    if o_spec is None:
        o_spec = pl.BlockSpec((bm, bn), lambda i, j, k: (i, j))
    if out_shape is None:
        out_shape = (M, N)
    return pl.pallas_call(
        body, name=name, grid=(M // bm, N // bn, nk), in_specs=[a_spec, b_spec] + after_specs, out_specs=o_spec,
        out_shape=jax.ShapeDtypeStruct(out_shape, out_dtype),
        scratch_shapes=[pltpu.VMEM((bm, bn), F32)] if own_acc else [],
        compiler_params=_params("parallel", "parallel", "arbitrary"),
    )(a, b, *after_args)


SCAN_LANES = 512
SCAN_CHUNKS = 8


def _to_chunked(a):
    return a.reshape(SCAN_CHUNKS, SEQ // SCAN_CHUNKS, -1).transpose(1, 0, 2).reshape(SEQ, -1)


def _from_chunked(a):
    return a.reshape(SEQ // SCAN_CHUNKS, SCAN_CHUNKS, -1).transpose(1, 0, 2).reshape(SEQ, -1)


def _scan_block(dr_ref, di_ref, sr_ref, si_ref, lam_r, lam_i, reverse):
    T = SEQ // SCAN_CHUNKS
    bl = lam_r.shape[1]
    assert T == 2 ** 8
    ar = jnp.broadcast_to(lam_r, (SCAN_CHUNKS, bl))
    ai = jnp.broadcast_to(lam_i, (SCAN_CHUNKS, bl))
    zero = jnp.zeros((SCAN_CHUNKS, bl), F32)

    def tile(j):
        return pl.ds(pl.multiple_of(j * SCAN_CHUNKS, SCAN_CHUNKS), SCAN_CHUNKS)

    def step(jj, carry):
        sr, si = carry
        j = T - 1 - jj if reverse else jj
        nr = ar * sr - ai * si + dr_ref[tile(j), :]
        ni = ar * si + ai * sr + di_ref[tile(j), :]
        sr_ref[tile(j), :] = nr
        si_ref[tile(j), :] = ni
        return nr, ni

    er, ei = lax.fori_loop(0, T, step, (zero, zero), unroll=4)

    pr, pi = ar[0:1], ai[0:1]
    for _ in range(8):
        pr, pi = pr * pr - pi * pi, 2.0 * pr * pi
    rows = lax.broadcasted_iota(jnp.int32, (SCAN_CHUNKS, bl), 0)
    cr, ci = zero, zero
    xr = jnp.zeros((1, bl), F32)
    xi = jnp.zeros((1, bl), F32)
    order = range(SCAN_CHUNKS - 2, -1, -1) if reverse else range(1, SCAN_CHUNKS)
    for c in order:
        src = c + 1 if reverse else c - 1
        nxr = pr * xr - pi * xi + er[src:src + 1]
        nxi = pr * xi + pi * xr + ei[src:src + 1]
        xr, xi = nxr, nxi
        cr = jnp.where(rows == c, xr, cr)
        ci = jnp.where(rows == c, xi, ci)

    def fix(jj, pw):
        pwr, pwi = pw
        j = T - 1 - jj if reverse else jj
        sr_ref[tile(j), :] = sr_ref[tile(j), :] + (pwr * cr - pwi * ci)
        si_ref[tile(j), :] = si_ref[tile(j), :] + (pwr * ci + pwi * cr)
        return pwr * ar - pwi * ai, pwr * ai + pwi * ar

    lax.fori_loop(0, T, fix, (ar, ai), unroll=4)


def _s5_forward(name, proj, wdt, cmt, lam_r, lam_i):
    bl = SCAN_LANES
    nblk = NS // bl

    def nt(a, b_):
        return lax.dot_general(a, b_, (((1,), (1,)), ((), ())), preferred_element_type=F32)

    def body(u_ref, wr_ref, wi_ref, cr_ref, ci_ref, ar_ref, ai_ref, s_ref, y_ref):
        sr_ref, si_ref = s_ref.at[0], s_ref.at[1]
        u = u_ref[...].astype(BF16)
        sr_ref[...] = nt(u, wr_ref[...])
        si_ref[...] = nt(u, wi_ref[...])
        _scan_block(sr_ref, si_ref, sr_ref, si_ref, ar_ref[...], ai_ref[...], False)
        part = nt(sr_ref[...].astype(BF16), cr_ref[...]) + nt(si_ref[...].astype(BF16), ci_ref[...])

        @pl.when(pl.program_id(0) == 0)
        def _():
            y_ref[...] = part

        @pl.when(pl.program_id(0) > 0)
        def _():
            y_ref[...] += part

    return pl.pallas_call(
        body, name=name, grid=(nblk,),
        in_specs=[pl.BlockSpec((SEQ, SW), lambda i: (0, CB_U)),
                  pl.BlockSpec((bl, SW), lambda i: (i, 0)), pl.BlockSpec((bl, SW), lambda i: (nblk + i, 0)),
                  pl.BlockSpec((SW, bl), lambda i: (0, i)), pl.BlockSpec((SW, bl), lambda i: (1, i)),
                  pl.BlockSpec((1, bl), lambda i: (0, i)), pl.BlockSpec((1, bl), lambda i: (0, i))],
        out_specs=[pl.BlockSpec((2, SEQ, bl), lambda i: (0, 0, i)), pl.BlockSpec((SEQ, SW), lambda i: (0, 0))],
        out_shape=[jax.ShapeDtypeStruct((2, SEQ, NS), F32), jax.ShapeDtypeStruct((SEQ, SW), F32)],
        compiler_params=_params("arbitrary"),
    )(proj, wdt, wdt, cmt, cmt, lam_r, lam_i)


S5_BWD_LANES = 256


def _s5_backward(name, dy0, proj, s, wdt, cmt, lam_r, lam_i):
    bl = S5_BWD_LANES
    nblk = NS // bl

    def tn(a, b_):
        return lax.dot_general(a, b_, (((0,), (0,)), ((), ())), preferred_element_type=F32)

    def prev(s_ref):
        last = pltpu.roll(s_ref[SEQ - SCAN_CHUNKS:SEQ, :], 1, 0)
        first = jnp.where(lax.broadcasted_iota(jnp.int32, (SCAN_CHUNKS, bl), 0) > 0, last, 0.0)
        return jnp.concatenate([first, s_ref[0:SEQ - SCAN_CHUNKS, :]], axis=0)

    def body(dy_ref, u_ref, s_ref, cr_ref, ci_ref, wr_ref, wi_ref, lr_ref, li_ref,
             dlr_ref, dli_ref, dwdt_ref, dcmt_ref, du_ref, ar_ref, ai_ref):
        dy = dy_ref[...]
        ar_ref[...] = jnp.dot(dy, cr_ref[...], preferred_element_type=F32)
        ai_ref[...] = jnp.dot(dy, ci_ref[...], preferred_element_type=F32)
        _scan_block(ar_ref, ai_ref, ar_ref, ai_ref, lr_ref[...], -li_ref[...], True)
        a_r, a_i = ar_ref[...], ai_ref[...]
        sr_ref, si_ref = s_ref.at[0], s_ref.at[1]
        spr, spi = prev(sr_ref), prev(si_ref)
        dlr_ref[...] = jnp.sum(a_r * spr + a_i * spi, axis=0, keepdims=True)
        dli_ref[...] = jnp.sum(a_i * spr - a_r * spi, axis=0, keepdims=True)
        a_rb, a_ib = a_r.astype(BF16), a_i.astype(BF16)
        u = u_ref[...].astype(BF16)
        dwdt_ref[0] = tn(a_rb, u)
        dwdt_ref[1] = tn(a_ib, u)
        dcmt_ref[0] = tn(dy, sr_ref[...].astype(BF16))
        dcmt_ref[1] = tn(dy, si_ref[...].astype(BF16))
        part = (jnp.dot(a_rb, wr_ref[...], preferred_element_type=F32)
                + jnp.dot(a_ib, wi_ref[...], preferred_element_type=F32))

        @pl.when(pl.program_id(0) == 0)
        def _():
            du_ref[...] = part

        @pl.when(pl.program_id(0) > 0)
        def _():
            du_ref[...] += part

    lam_spec = pl.BlockSpec((1, bl), lambda i: (0, i))
    return pl.pallas_call(
        body, name=name, grid=(nblk,),
        in_specs=[pl.BlockSpec((SEQ, SW), lambda i: (0, 0)), pl.BlockSpec((SEQ, SW), lambda i: (0, CB_U)),
                  pl.BlockSpec((2, SEQ, bl), lambda i: (0, 0, i)),
                  pl.BlockSpec((SW, bl), lambda i: (0, i)), pl.BlockSpec((SW, bl), lambda i: (1, i)),
                  pl.BlockSpec((bl, SW), lambda i: (i, 0)), pl.BlockSpec((bl, SW), lambda i: (nblk + i, 0)),
                  lam_spec, lam_spec],
        out_specs=[lam_spec, lam_spec, pl.BlockSpec((2, bl, SW), lambda i: (0, i, 0)),
                   pl.BlockSpec((2, SW, bl), lambda i: (0, 0, i)), pl.BlockSpec((SEQ, SW), lambda i: (0, 0))],
        out_shape=[jax.ShapeDtypeStruct((1, NS), F32), jax.ShapeDtypeStruct((1, NS), F32),
                   jax.ShapeDtypeStruct((2, NS, SW), F32), jax.ShapeDtypeStruct((2, SW, NS), F32),
                   jax.ShapeDtypeStruct((SEQ, SW), F32)],
        scratch_shapes=[pltpu.VMEM((SEQ, bl), F32)] * 2,
        compiler_params=_params("arbitrary"),
    )(dy0, proj, s, cmt, cmt, wdt, wdt, lam_r, lam_i)


def _scores(qb, kb, prev):
    s = lax.dot_general(qb, kb, (((1,), (1,)), ((), ())), preferred_element_type=F32) * (HD ** -0.5)
    row = lax.broadcasted_iota(jnp.int32, (ABLK, ABLK), 0)
    col = lax.broadcasted_iota(jnp.int32, (ABLK, ABLK), 1)
    return jnp.where((col >= row) if prev else (col <= row), s, -1e30)


def _block_rows(dil, r, b):
    if dil == 1:
        return pl.ds(pl.multiple_of(b * ABLK, ABLK), ABLK)
    return pl.ds(r + dil * ABLK * b, ABLK, stride=dil)


def _group_blocks(dil):
    nb = SEQ // dil // ABLK
    shift = nb.bit_length() - 1
    return nb, (lambda idx: (idx >> shift, idx & (nb - 1)))


def _qkv_specs(j_of):
    return [pl.BlockSpec((SEQ, HD), functools.partial(lambda j, c: (0, c + j_of(j)), c=(cb + g) * 4))
            for g in range(3) for cb in (CB_Q, CB_K, CB_V)]


def _attention_fwd(name, proj):
    def body(*refs):
        qkv, z_ref = refs[:9], refs[9]
        y_ref, ya_ref, l_ref = refs[10:13]
        accs, maxs, dens = refs[13:16], refs[16:19], refs[19:22]
        for g, dil in enumerate(DILATIONS):
            q_ref, k_ref, v_ref = qkv[3 * g:3 * g + 3]
            nb, where = _group_blocks(dil)

            def step(t, c, g=g, dil=dil, nb=nb, where=where, q_ref=q_ref, k_ref=k_ref, v_ref=v_ref):
                two = range(ATTN_PAIR)
                rb = [where(t + i * (SEQ // ABLK // ATTN_PAIR)) for i in two]
                rows = [_block_rows(dil, r, b) for r, b in rb]
                qb = [q_ref[rows[i], :].astype(BF16) for i in two]
                s_c = [_scores(qb[i], k_ref[rows[i], :].astype(BF16), False) for i in two]
                if nb > 1:
                    prev = [_block_rows(dil, r, jnp.maximum(b - 1, 0)) for r, b in rb]
                    s_p = [jnp.where(rb[i][1] > 0, _scores(qb[i], k_ref[prev[i], :].astype(BF16), True), -1e30)
                           for i in two]
                m = [jnp.max(s_c[i], axis=-1, keepdims=True) for i in two]
                if nb > 1:
                    m = [jnp.maximum(m[i], jnp.max(s_p[i], axis=-1, keepdims=True)) for i in two]
                p_c = [jnp.exp(s_c[i] - m[i]) for i in two]
                den = [jnp.sum(p_c[i], axis=-1, keepdims=True) for i in two]
                acc = [jnp.dot(p_c[i].astype(BF16), v_ref[rows[i], :].astype(BF16), preferred_element_type=F32)
                       for i in two]
                if nb > 1:
                    p_p = [jnp.exp(s_p[i] - m[i]) for i in two]
                    den = [den[i] + jnp.sum(p_p[i], axis=-1, keepdims=True) for i in two]
                    acc = [acc[i] + jnp.dot(p_p[i].astype(BF16), v_ref[prev[i], :].astype(BF16),
                                            preferred_element_type=F32) for i in two]
                for i in two:
                    accs[g][rows[i], :] = acc[i]
                    maxs[g][rows[i], :] = jnp.broadcast_to(m[i], (ABLK, HD))
                    dens[g][rows[i], :] = jnp.broadcast_to(den[i], (ABLK, HD))
                return c

            lax.fori_loop(0, SEQ // ABLK // ATTN_PAIR, step, 0)
        top = jnp.maximum(jnp.maximum(maxs[0][...], maxs[1][...]), maxs[2][...])
        den = jnp.zeros((SEQ, HD), F32)
        y = jnp.zeros((SEQ, HD), F32)
        for g in range(3):
            wgt = jnp.exp(maxs[g][...] - top)
            den = den + wgt * dens[g][...]
            y = y + wgt * accs[g][...]
        y = y / den
        y_ref[...] = y
        ya_ref[...] = (y * _silu(z_ref[...])).astype(ya_ref.dtype)
        l_ref[...] = top + jnp.log(den)

    ospec = pl.BlockSpec((SEQ, HD), lambda j: (0, j))
    return pl.pallas_call(
        body, name=name, grid=(AW // HD,),
        in_specs=_qkv_specs(lambda j: j) + [pl.BlockSpec((SEQ, HD), lambda j: (0, CB_ZA * 4 + j))],
        out_specs=[ospec, ospec, ospec],
        out_shape=[jax.ShapeDtypeStruct((SEQ, AW), F32), jax.ShapeDtypeStruct((SEQ, AW), BF16),
                   jax.ShapeDtypeStruct((SEQ, AW), F32)],
        scratch_shapes=[pltpu.VMEM((SEQ, HD), F32)] * 9,
        compiler_params=_params("parallel"),
    )(*([proj] * 10))


def _attention_bwd(name, proj, dya, y, lse):
    def tn(a, b_):
        return lax.dot_general(a, b_, (((0,), (0,)), ((), ())), preferred_element_type=F32)

    def nt(a, b_):
        return lax.dot_general(a, b_, (((1,), (1,)), ((), ())), preferred_element_type=F32)

    def body(*refs):
        qkv, z_ref, dya_ref, y_ref, l_ref = refs[:9], refs[9], refs[10], refs[11], refs[12]
        outs, dza_ref = refs[13:22], refs[22]
        dy_s, dsum_s, dq_s, dk_own, dv_own, dk_prev, dv_prev = refs[23:]
        _, vjp = jax.vjp(lambda y_, z_: y_ * _silu(z_), y_ref[...], z_ref[...])
        dy, dz = vjp(dya_ref[...])
        dza_ref[...] = dz.astype(dza_ref.dtype)
        dy_s[...] = dy
        dsum_s[...] = jnp.broadcast_to(jnp.sum(dy * y_ref[...], axis=-1, keepdims=True), (SEQ, HD))
        for g, dil in enumerate(DILATIONS):
            q_ref, k_ref, v_ref = qkv[3 * g:3 * g + 3]
            nb, where = _group_blocks(dil)
            if nb > 1:
                dk_prev[...] = jnp.zeros(dk_prev.shape, F32)
                dv_prev[...] = jnp.zeros(dv_prev.shape, F32)

            def step(t, c, dil=dil, nb=nb, where=where, q_ref=q_ref, k_ref=k_ref, v_ref=v_ref):
                rb = [where(t + i * (SEQ // ABLK // ATTN_PAIR)) for i in range(ATTN_PAIR)]
                sides = []
                for r, b in rb:
                    rows = _block_rows(dil, r, b)
                    own = dict(b=b, qrows=rows, krows=rows, prev=False, dk=dk_own, dv=dv_own,
                               q=q_ref[rows, :].astype(BF16), dy=dy_s[rows, :].astype(BF16))
                    sides.append(own)
                    if nb > 1:
                        sides.append(dict(own, krows=_block_rows(dil, r, jnp.maximum(b - 1, 0)), prev=True,
                                          dk=dk_prev, dv=dv_prev))
                for s_ in sides:
                    s_["k"] = k_ref[s_["krows"], :].astype(BF16)
                    s_["v"] = v_ref[s_["krows"], :].astype(BF16)
                for s_ in sides:
                    sc = _scores(s_["q"], s_["k"], s_["prev"])
                    s_["s"] = jnp.where(s_["b"] > 0, sc, -1e30) if s_["prev"] else sc
                    s_["dp"] = nt(s_["dy"], s_["v"])
                for s_ in sides:
                    p = jnp.exp(s_["s"] - l_ref[s_["qrows"], :])
                    s_["p"] = p.astype(BF16)
                    s_["ds"] = (p * (s_["dp"] - dsum_s[s_["qrows"], :]) * (HD ** -0.5)).astype(BF16)
                for s_ in sides:
                    s_["dk"][s_["krows"], :] = tn(s_["ds"], s_["q"])
                    s_["dv"][s_["krows"], :] = tn(s_["p"], s_["dy"])
                    s_["dq"] = jnp.dot(s_["ds"], s_["k"], preferred_element_type=F32)
                per = len(sides) // ATTN_PAIR
                for i in range(ATTN_PAIR):
                    dq = sides[i * per]["dq"]
                    if per > 1:
                        dq = dq + sides[i * per + 1]["dq"]
                    dq_s[sides[i * per]["qrows"], :] = dq
                return c

            lax.fori_loop(0, SEQ // ABLK // ATTN_PAIR, step, 0)
            dq_ref, dk_ref, dv_ref = outs[3 * g:3 * g + 3]
            dq_ref[...] = dq_s[...].astype(dq_ref.dtype)
            if nb > 1:
                dk_ref[...] = (dk_own[...] + dk_prev[...]).astype(dk_ref.dtype)
                dv_ref[...] = (dv_own[...] + dv_prev[...]).astype(dv_ref.dtype)
            else:
                dk_ref[...] = dk_own[...].astype(dk_ref.dtype)
                dv_ref[...] = dv_own[...].astype(dv_ref.dtype)

    ospec = pl.BlockSpec((SEQ, HD), lambda j: (0, j))
    outs = pl.pallas_call(
        body, name=name, grid=(AW // HD,),
        in_specs=_qkv_specs(lambda j: j) + [pl.BlockSpec((SEQ, HD), lambda j: (0, CB_ZA * 4 + j))] + [ospec] * 3,
        out_specs=[ospec] * 10, out_shape=[jax.ShapeDtypeStruct((SEQ, AW), BF16)] * 10,
        scratch_shapes=[pltpu.VMEM((SEQ, HD), F32)] * 7,
        compiler_params=_params("parallel"),
    )(*([proj] * 10), dya, y, lse)
    return outs[:9], outs[9]


def _rms(x, g):
    return x * lax.rsqrt(jnp.mean(x * x, axis=-1, keepdims=True) + RMS_EPS) * g


def _sig(x):
    return 1.0 / (1.0 + jnp.exp(-x))


def _silu(x):
    return x * _sig(x)


def _gelu(x):
    return 0.5 * x * (1.0 + jnp.tanh(math.sqrt(2.0 / math.pi) * (x + 0.044715 * (x * x * x))))


def _y1_fn(y0p, u, dskip):
    return _gelu(y0p + dskip * u)


def _ys_fn(y1, t, z, bglu):
    return y1 * _sig(t + bglu) * _silu(z)


def _merge_fn(ms, ma, gs, ga):
    return _sig(gs) * ms + _sig(ga) * ma


def _colsum(v):
    return jnp.sum(v, axis=0, keepdims=True)


def _lam_fn(lre, lim, ldt):
    a = jnp.minimum(lre, -1e-4)
    dt = jnp.exp(ldt)
    mag = jnp.exp(a * dt)
    ar = mag * jnp.cos(lim * dt)
    ai = mag * jnp.sin(lim * dt)
    den = a * a + lim * lim
    cr = ((ar - 1.0) * a + ai * lim) / den
    ci = (ai * a - (ar - 1.0) * lim) / den
    return ar, ai, cr, ci


def _bbar_fn(cr, ci, bre, bim):
    return cr * bre - ci * bim, cr * bim + ci * bre


def _same_group(rows, a, cols, b):
    r = lax.broadcasted_iota(jnp.int32, (rows, cols), 0) >> (a.bit_length() - 1)
    c = lax.broadcasted_iota(jnp.int32, (rows, cols), 1) >> (b.bit_length() - 1)
    return r == c


def _expand(name, blocks, signs, a, b, dtype, after=None):
    rows, cols = NGRP * a, NGRP * b
    n = len(blocks)
    assert a & (a - 1) == 0 and b & (b - 1) == 0
    after_specs, after_args = _after(after)

    def body(*refs):
        o_ref = refs[-1]
        tile = (lax.broadcasted_iota(jnp.int32, (b, cols), 1) & (b - 1)
                == lax.broadcasted_iota(jnp.int32, (b, cols), 0)).astype(F32)
        keep = _same_group(rows, a, cols, b)
        for i, (ref, sign) in enumerate(zip(refs[:n], signs)):
            spread = jnp.dot(ref[...], tile, preferred_element_type=F32, precision=lax.Precision.HIGHEST)
            o_ref[i * rows:(i + 1) * rows, :] = jnp.where(keep, sign * spread, 0.0).astype(o_ref.dtype)

    return pl.pallas_call(body, name=name, grid=(1,),
                          in_specs=[pl.BlockSpec((rows, b), lambda i: (0, 0))] * n + after_specs,
                          out_specs=pl.BlockSpec((n * rows, cols), lambda i: (0, 0)),
                          out_shape=jax.ShapeDtypeStruct((n * rows, cols), dtype),
                          compiler_params=_params("arbitrary"))(*blocks, *after_args)


def _extract(name, m, a, b, at=(0, 0), after=None):
    rows, cols = NGRP * a, NGRP * b
    assert a & (a - 1) == 0 and b & (b - 1) == 0
    after_specs, after_args = _after(after)

    def body(m_ref, *rest):
        o_ref = rest[-1]
        tile = (lax.broadcasted_iota(jnp.int32, (cols, b), 0) & (b - 1)
                == lax.broadcasted_iota(jnp.int32, (cols, b), 1)).astype(F32)
        kept = jnp.where(_same_group(rows, a, cols, b), m_ref[...], 0.0)
        o_ref[...] = jnp.dot(kept, tile, preferred_element_type=F32, precision=lax.Precision.HIGHEST)

    return pl.pallas_call(body, name=name, grid=(1,),
                          in_specs=[pl.BlockSpec((rows, cols), lambda i: at)] + after_specs,
                          out_specs=pl.BlockSpec((rows, b), lambda i: (0, 0)),
                          out_shape=jax.ShapeDtypeStruct((rows, b), F32),
                          compiler_params=_params("arbitrary"))(m, *after_args)


def _s5_prepare(l, sp, after):
    tag = f"l{l}_"
    ar, ai, cr, ci = _ew(tag + "lam", _lam_fn, NGRP, NGRP,
                         [_ri(sp["lambda_re"]), _ri(sp["lambda_im"]),
                          _ri(sp["log_dt"].reshape(NGRP, 1) + after[0, 0])], [], [(NST, F32)] * 4)
    bre = sp["b_re"].reshape(NS, GCH)
    bim = sp["b_im"].reshape(NS, GCH)
    bbr, bbi = _ew(tag + "bbar", _bbar_fn, NS, NS, [_ri(cr.reshape(NS, 1)), _ri(ci.reshape(NS, 1)), _ri(bre), _ri(bim)],
                   [], [(GCH, F32)] * 2)
    wdt = _expand(tag + "wdt", [bbr, bbi], [1.0, 1.0], NST, GCH, BF16)
    cmt = _expand(tag + "cmt", [sp["c_re"].reshape(SW, NST), sp["c_im"].reshape(SW, NST)], [1.0, -1.0], GCH, NST, BF16,
                  after=after)
    return dict(ar=ar, ai=ai, cr=cr, ci=ci, wdt=wdt, cmt=cmt)


def _layer_fwd(l, x, w, sp, prep):
    tag = f"l{l}_"
    g1 = sp["pre_norm_g"].reshape(1, DM)
    (h,) = _ew(tag + "rms1", lambda x_, g: (_rms(x_, g),), SEQ, 256, [_ri(x)], [g1], [(DM, BF16)])
    hv = jnp.stack([h, _to_chunked(h)])
    win = w["w_in"]
    proj = _mm(tag + "proj", hv, win, "nn", SEQ, NCOL, DM, SEQ, 512, 1024, F32,
               a_spec=pl.BlockSpec((None, SEQ, 1024), lambda i, j, k: (_row_order(j), 0, 0)),
               b_spec=pl.BlockSpec((None, 1024, 512), lambda i, j, k: (j // 2, 0, j % 2)))

    ar, ai, cr, ci, wdt, cmt = (prep[k] for k in ("ar", "ai", "cr", "ci", "wdt", "cmt"))
    s, y0p = _s5_forward(tag + "s5", proj, wdt, cmt, ar.reshape(1, NS), ai.reshape(1, NS))
    dskip = sp["d_skip"].reshape(1, SW)
    (y1,) = _ew(tag + "y1", lambda a, u, d: (_y1_fn(a, u, d),), SEQ, 256,
                [_ri(y0p), _ri(proj, SW, CB_U)], [dskip], [(SW, F32)])
    t = _mm(tag + "glu", y1, w["w_glu"], "nn", SEQ, SW, SW, 1024, 512, 512, F32)
    bglu = sp["b_glu"].reshape(1, SW)
    (ys_c,) = _ew(tag + "ys", lambda y1_, t_, z, b_: (_ys_fn(y1_, t_, z, b_),), SEQ, 256,
                  [_ri(y1), _ri(t), _ri(proj, SW, CB_ZS)], [bglu], [(SW, BF16)])
    ys = _from_chunked(ys_c)

    ypre, ya, lse = _attention_fwd(tag + "attn", proj)

    ms = _mm(tag + "branch_s", ys, w["w_branch_s"], "nn", SEQ, DM, SW, 1024, 1024, 512, F32)
    ma = _mm(tag + "branch_a", ya, w["w_branch_a"], "nn", SEQ, DM, AW, 1024, 1024, 512, F32)
    (merged,) = _ew(tag + "merge", lambda a, b_, c, d: (_merge_fn(a, b_, c, d),), SEQ, 256,
                    [_ri(ms), _ri(ma), _ri(proj, DM, CB_GS), _ri(proj, DM, CB_GA)], [], [(DM, BF16)])
    out = _mm(tag + "out", merged, w["w_out"], "nn", SEQ, DM, DM, 1024, 1024, 1024, F32)
    g2 = sp["post_norm_g"].reshape(1, DM)
    (x_new,) = _ew(tag + "post", lambda x_, o, g: (x_ + _rms(o, g),), SEQ, 256, [_ri(x), _ri(out)], [g2], [(DM, F32)])
    res = dict(x=x, hv=hv, proj=proj, ar=ar, ai=ai, cr=cr, ci=ci, wdt=wdt, cmt=cmt, s=s, y0p=y0p,
               y1=y1, t=t, ys=ys, ya=ya, ypre=ypre, lse=lse, ms=ms, ma=ma, merged=merged, out=out)
    return x_new, res


def _layer_bwd(l, dxn, r, w, sp, big_done, after=None):
    tag = f"l{l}b_"
    proj = r["proj"]
    g1 = sp["pre_norm_g"].reshape(1, DM)
    g2 = sp["post_norm_g"].reshape(1, DM)
    dskip = sp["d_skip"].reshape(1, SW)
    bglu = sp["b_glu"].reshape(1, SW)

    def post_b(d, o, g):
        _, vjp = jax.vjp(_rms, o, g)
        do, dg = vjp(d)
        return do, dg

    d_out, dg2 = _ew(tag + "post", post_b, SEQ, 256, [_ri(dxn), _ri(r["out"])], [g2], [(DM, BF16)], [DM])
    dw_out = _mm(tag + "dw_out", r["merged"], d_out, "tn", DM, DM, SEQ, 1024, 1024, SEQ, BF16, after=after)
    dmerged = _mm(tag + "dmerged", d_out, w["w_out"], "nt", SEQ, DM, DM, 1024, 1024, 1024, F32, after=after)

    def merge_b(d, ms, ma, gs, ga):
        _, vjp = jax.vjp(_merge_fn, ms, ma, gs, ga)
        return vjp(d)

    dms, dma, dgs, dga = _ew(tag + "merge", merge_b, SEQ, 256,
                             [_ri(dmerged), _ri(r["ms"]), _ri(r["ma"]), _ri(proj, DM, CB_GS), _ri(proj, DM, CB_GA)],
                             [], [(DM, BF16)] * 4)
    dw_bs = _mm(tag + "dw_bs", r["ys"], dms, "tn", SW, DM, SEQ, 512, 1024, SEQ, BF16)
    dw_ba = _mm(tag + "dw_ba", r["ya"], dma, "tn", AW, DM, SEQ, 512, 1024, SEQ, BF16)
    dys = _mm(tag + "dys", dms, w["w_branch_s"], "nt", SEQ, SW, DM, 1024, 512, 1024, F32)
    dya = _mm(tag + "dya", dma, w["w_branch_a"], "nt", SEQ, AW, DM, 1024, 512, 1024, F32)

    dqkv, dza = _attention_bwd(tag + "attn", proj, dya, r["ypre"], r["lse"])

    def ys_b(d, y1, t, z, b_):
        _, vjp = jax.vjp(_ys_fn, y1, t, z, b_)
        dy1, dt, dz, _ = vjp(d)
        return dy1, dt, dz, _colsum(dt)

    dy1a, dt, dzs, dbglu = _ew(tag + "ys", ys_b, SEQ, 256,
                               [_ri(_to_chunked(dys)), _ri(r["y1"]), _ri(r["t"]), _ri(proj, SW, CB_ZS)],
                               [bglu], [(SW, F32), (SW, BF16), (SW, BF16)], [SW])
    dw_glu = _mm(tag + "dw_glu", r["y1"], dt, "tn", SW, SW, SEQ, 512, 512, SEQ, BF16)
    dy1b = _mm(tag + "dy1b", dt, w["w_glu"], "nt", SEQ, SW, SW, 1024, 512, 512, F32)

    def y1_b(da, db, y0p, u, d_):
        _, vjp = jax.vjp(_y1_fn, y0p, u, d_)
        dy0, du, dd = vjp(da + db)
        return dy0, du, dd

    dy0, du_skip, ddskip = _ew(tag + "y1", y1_b, SEQ, 256,
                               [_ri(dy1a), _ri(dy1b), _ri(r["y0p"]), _ri(proj, SW, CB_U)], [dskip],
                               [(SW, BF16), (SW, F32)], [SW])
    dlr, dli, dwdt, dcmt, du_s = _s5_backward(tag + "s5", dy0, proj, r["s"], r["wdt"], r["cmt"],
                                              r["ar"].reshape(1, NS), r["ai"].reshape(1, NS))
    (du,) = _ew(tag + "du", lambda a, c: (a + c,), SEQ, 256, [_ri(du_s), _ri(du_skip)], [], [(SW, BF16)])

    dq, dk, dv = ([dqkv[3 * g + i] for g in range(3)] for i in range(3))
    dproj = jnp.concatenate([du, dzs, *dq, *dk, *dv, dza, dgs, dga], axis=1)
    dw_in = _mm(tag + "dw_in", r["hv"], dproj, "tn", DM, NCOL, SEQ, 1024, 512, SEQ, BF16,
                a_spec=pl.BlockSpec((None, SEQ, 1024), lambda i, j, k: (_row_order(j), 0, 0)),
                o_spec=pl.BlockSpec((None, 1024, 512), lambda i, j, k: (j // 2, 0, j % 2)), out_shape=(NDEV, DM, DM))
    tok = big_done(l, dict(w_in=dw_in, w_glu=dw_glu.reshape(NDEV, SW // NDEV, SW),
                           w_branch_s=dw_bs.reshape(SW, NDEV, DM // NDEV).transpose(1, 0, 2),
                           w_branch_a=dw_ba.reshape(AW, NDEV, DM // NDEV).transpose(1, 0, 2),
                           w_out=dw_out.reshape(NDEV, DM // NDEV, DM)))
    if tok is not None:
        g1 = g1 + tok[0, 0]

    dwdt = dwdt.reshape(2 * NS, SW)
    dcmt = dcmt.reshape(2 * SW, NS)
    dbbr = _extract(tag + "dbbr", dwdt, NST, GCH, (0, 0), after=tok)
    dbbi = _extract(tag + "dbbi", dwdt, NST, GCH, (1, 0), after=tok)
    bre = sp["b_re"].reshape(NS, GCH)
    bim = sp["b_im"].reshape(NS, GCH)

    def bbar_b(cr, ci, br_, bi_, dr, di):
        _, vjp = jax.vjp(_bbar_fn, cr, ci, br_, bi_)
        return vjp((dr, di))

    dcr, dci, dbre, dbim = _ew(tag + "bbar", bbar_b, NS, NS,
                               [_ri(r["cr"].reshape(NS, 1)), _ri(r["ci"].reshape(NS, 1)), _ri(bre), _ri(bim),
                                _ri(dbbr), _ri(dbbi)], [], [(1, F32), (1, F32), (GCH, F32), (GCH, F32)])

    def lam_b(lre, lim, ldt, dar, dai, dcr_, dci_):
        _, vjp = jax.vjp(_lam_fn, lre, lim, ldt)
        return vjp((dar, dai, dcr_, dci_))

    dlre, dlim, dldt = _ew(tag + "lam", lam_b, NGRP, NGRP,
                           [_ri(sp["lambda_re"]), _ri(sp["lambda_im"]), _ri(sp["log_dt"].reshape(NGRP, 1)),
                            _ri(dlr.reshape(NGRP, NST)), _ri(dli.reshape(NGRP, NST)),
                            _ri(dcr.reshape(NGRP, NST)), _ri(dci.reshape(NGRP, NST))], [],
                           [(NST, F32), (NST, F32), (1, F32)])
    dc_re = _extract(tag + "dc_re", dcmt, GCH, NST, (0, 0), after=tok).reshape(NGRP, GCH, NST)
    dc_im = -_extract(tag + "dc_im", dcmt, GCH, NST, (1, 0), after=tok).reshape(NGRP, GCH, NST)

    dh_time = _mm(tag + "dh_time", dproj, w["w_in"], "nt", SEQ, DM, NCOL - DM, SEQ, 1024, 1024, F32,
                  a_spec=pl.BlockSpec((SEQ, 1024), lambda i, j, k: (0, 1 + k)),
                  b_spec=pl.BlockSpec((None, 1024, 1024), lambda i, j, k: (1 + k, 0, 0)), after=tok)
    dh_chunked = _mm(tag + "dh_chunked", dproj, w["w_in"], "nt", SEQ, DM, DM, SEQ, 1024, 1024, F32,
                     a_spec=pl.BlockSpec((SEQ, 1024), lambda i, j, k: (0, 0)),
                     b_spec=pl.BlockSpec((None, 1024, 1024), lambda i, j, k: (0, 0, 0)), after=tok)
    dh = [dh_time, _from_chunked(dh_chunked)]

    def pre_b(d, dh0, dh1, x_, g):
        _, vjp = jax.vjp(_rms, x_, g)
        dx_, dg = vjp(dh0 + dh1)
        return d + dx_, dg

    dx, dg1 = _ew(tag + "pre", pre_b, SEQ, 256, [_ri(dxn)] + [_ri(t_) for t_ in dh] + [_ri(r["x"])], [g1],
                  [(DM, F32)], [DM])

    small = dict(pre_norm_g=dg1.reshape(DM), lambda_re=dlre, lambda_im=dlim, log_dt=dldt.reshape(NGRP),
                 b_re=dbre.reshape(NGRP, NST, GCH), b_im=dbim.reshape(NGRP, NST, GCH), c_re=dc_re, c_im=dc_im,
                 d_skip=ddskip.reshape(SW), b_glu=dbglu.reshape(SW), post_norm_g=dg2.reshape(DM))
    return dx, small


_HBM = pl.BlockSpec(memory_space=pltpu.HBM)
_SEM = pl.BlockSpec(memory_space=pltpu.SEMAPHORE)
_EFFECT = pltpu.SideEffectType.DATAFLOW_SIDE_EFFECTING


def _remote_copies(srcs, dsts, send_sems, recv_sems, gather):
    x, y, c = lax.axis_index("x"), lax.axis_index("y"), lax.axis_index("c")
    me = 4 * x + 2 * y + c
    copies = []
    for i in range(len(srcs)):
        for k in range(1, NDEV):
            peer = (x ^ (k >> 2), y ^ ((k >> 1) & 1), c ^ (k & 1))
            src = srcs[i] if gather[i] else srcs[i].at[me ^ k]
            copies.append(pltpu.make_async_remote_copy(
                src_ref=src, dst_ref=dsts[i].at[me], send_sem=send_sems[i], recv_sem=recv_sems[i],
                device_id=peer, device_id_type=pl.DeviceIdType.MESH))
    return copies


def _all_seven(dst, send_sem, recv_sem):
    seven = dst.at[pl.ds(0, NDEV - 1)]
    me = (lax.axis_index("x"), lax.axis_index("y"), lax.axis_index("c"))
    return pltpu.make_async_remote_copy(src_ref=seven, dst_ref=seven, send_sem=send_sem, recv_sem=recv_sem,
                                        device_id=me, device_id_type=pl.DeviceIdType.MESH)


def _own_slabs(name, arrs, gather, after):
    n = len(arrs)
    me = (4 * lax.axis_index("x") + 2 * lax.axis_index("y") + lax.axis_index("c")).astype(jnp.int32).reshape(1)

    def body(me_ref, *refs):
        for src, dst in zip(refs[:n], refs[n + 1:]):
            dst[...] = src[...]

    def zeros(k):
        return (0,) * k

    in_specs, out_specs, out_shape = [], [], []
    for a, g in zip(arrs, gather):
        slab = a.shape if g else a.shape[1:]
        nd = len(slab)
        if g:
            in_specs.append(pl.BlockSpec(slab, functools.partial(lambda i, me_ref, nd: zeros(nd), nd=nd)))
        else:
            in_specs.append(pl.BlockSpec((None,) + slab, functools.partial(lambda i, me_ref, nd: (me_ref[0],) + zeros(nd), nd=nd)))
        out_specs.append(pl.BlockSpec((None,) + slab, functools.partial(lambda i, me_ref, nd: (me_ref[0],) + zeros(nd), nd=nd)))
        out_shape.append(jax.ShapeDtypeStruct((NDEV,) + slab, a.dtype))
    in_specs.append(pl.BlockSpec(memory_space=pl.ANY))
    return pl.pallas_call(
        body, name=name, out_shape=out_shape,
        grid_spec=pltpu.PrefetchScalarGridSpec(num_scalar_prefetch=1, grid=(1,), in_specs=in_specs, out_specs=out_specs),
        compiler_params=_params("arbitrary"),
    )(me, *arrs, after)


def _exchange_start(name, arrs, gather, after):
    n = len(arrs)
    lands = _own_slabs(name + "_own", arrs, gather, after)

    def body(*refs):
        srcs, dsts = refs[:n], refs[n:2 * n]
        send_sems, recv_sems = refs[2 * n:3 * n], refs[3 * n:4 * n]
        token = refs[-1]
        for cp in _remote_copies(srcs, dsts, send_sems, recv_sems, gather):
            cp.start()
        token[...] = jnp.zeros(token.shape, token.dtype)

    thru = [pltpu.HBM(a.shape, a.dtype) for a in list(arrs) + list(lands)]
    outs = pl.pallas_call(
        body, name=name,
        out_shape=(*[pltpu.SemaphoreType.DMA(())] * (2 * n), *thru, jax.ShapeDtypeStruct((8, 128), F32)),
        in_specs=[_HBM] * (2 * n),
        out_specs=(*[_SEM] * (2 * n), *[_HBM] * (2 * n), pl.BlockSpec(memory_space=pltpu.VMEM)),
        input_output_aliases={i: 2 * n + i for i in range(2 * n)},
        compiler_params=pltpu.CompilerParams(has_side_effects=_EFFECT),
    )(*[pltpu.with_memory_space_constraint(a, pltpu.HBM) for a in list(arrs) + list(lands)])
    return dict(send=outs[:n], recv=outs[n:2 * n], srcs=outs[2 * n:3 * n], lands=outs[3 * n:4 * n], token=outs[-1],
                gather=gather)


def _exchange_wait(name, started, after):
    n = len(started["srcs"])
    after = list(after)

    def body(*refs):
        dsts = refs[n:2 * n]
        send_sems, recv_sems = refs[2 * n:3 * n], refs[3 * n:4 * n]
        for i in range(n):
            cp = _all_seven(dsts[i], send_sems[i], recv_sems[i])
            cp.wait_send()
            cp.wait_recv()

    bufs = list(started["srcs"]) + list(started["lands"])
    outs = pl.pallas_call(
        body, name=name, out_shape=tuple(pltpu.HBM(a.shape, a.dtype) for a in bufs),
        in_specs=[_HBM] * (2 * n) + [_SEM] * (2 * n) + [pl.BlockSpec(memory_space=pl.ANY)] * len(after),
        out_specs=(_HBM,) * (2 * n), input_output_aliases={i: i for i in range(2 * n)},
        compiler_params=pltpu.CompilerParams(has_side_effects=_EFFECT),
    )(*bufs, *started["send"], *started["recv"], *after)
    return outs[n:]


def _sum_in_order(parts):
    g = parts[0].astype(F32)
    for p in parts[1:]:
        g = g + p.astype(F32)
    return g


def _adamw(name, parts, nparts, w, m, v, br):
    rows, cols = w.shape

    def fn(*a):
        g = _sum_in_order(a[:nparts])
        w_, m_, v_ = a[nparts:]
        m2 = B1 * m_ + (1.0 - B1) * g
        v2 = B2 * v_ + (1.0 - B2) * (g * g)
        m_hat = m2 / (1.0 - B1 ** STEP)
        v_hat = v2 / (1.0 - B2 ** STEP)
        delta = -LR * (m_hat / (jnp.sqrt(v_hat) + ADAM_EPS) + WD * w_)
        return g, delta, m2, v2

    ins = [_ri(parts, cols, 0, d * (rows // br)) for d in range(nparts)] + [_ri(w), _ri(m), _ri(v)]
    return _ew(name, fn, rows, br, ins, [], [(cols, F32)] * 4)


SMALL = ("pre_norm_g", "lambda_re", "lambda_im", "log_dt", "b_re", "b_im", "c_re", "c_im", "d_skip", "b_glu",
         "post_norm_g")
BIG = ("w_in", "w_glu", "w_branch_s", "w_branch_a", "w_out")
WEIGHTS = ("pre_norm_g", "w_in", "lambda_re", "lambda_im", "log_dt", "b_re", "b_im", "c_re", "c_im", "d_skip",
           "w_glu", "b_glu", "w_branch_s", "w_branch_a", "w_out", "post_norm_g")
PACK_COLS = 1024
PACK_BR = 136


def _pack_layer(d):
    pieces = [d[k].astype(F32).reshape(-1) for k in SMALL]
    used = sum(p.shape[0] for p in pieces)
    assert used <= PACK_BR * PACK_COLS
    return jnp.concatenate(pieces + [jnp.zeros((PACK_BR * PACK_COLS - used,), F32)]).reshape(PACK_BR, PACK_COLS)


def _unpack(p, like):
    flat = p.reshape(DEPTH, PACK_BR * PACK_COLS)
    out, off = {}, 0
    for k in SMALL:
        n = like[k].size // DEPTH
        out[k] = flat[:, off:off + n].reshape(like[k].shape)
        off += n
    return out


def _local_step(x, target, small, started, weights_of, big_done, small_done, total_loss):
    preps = [_s5_prepare(l, {k: small[k][l] for k in SMALL}, started) for l in range(DEPTH)]
    res, ws = [], []
    for l in range(DEPTH):
        w_l, tok = weights_of(l, [x] + ([p[k] for p in preps for k in ("wdt", "cmt")] if l == 0 else []))
        sp = {k: small[k][l] for k in SMALL}
        if tok is not None:
            sp["pre_norm_g"] = sp["pre_norm_g"] + tok[0, 0]
        x, r = _layer_fwd(l, x, w_l, sp, preps[l])
        res.append(r)
        ws.append(w_l)

    def loss_fn(y, t):
        e = y - t
        return e * (1.0 / DM), jnp.sum(_colsum(0.5 * e * e * (1.0 / DM)), axis=1, keepdims=True)

    dx, loss = _ew("loss", loss_fn, SEQ, 256, [_ri(x), _ri(target)], [], [(DM, F32)], [1])
    total = total_loss(loss.reshape(()))
    for l in reversed(range(DEPTH)):
        dx, sm = _layer_bwd(l, dx, res[l], ws[l], {k: small[k][l] for k in SMALL}, big_done,
                            after=total.reshape(1, 1) if l == DEPTH - 1 else None)
        small_done(l, sm)
    return total, dx


def _full_weights(gathered):
    g = gathered
    return dict(
        w_in=g["w_in"],
        w_glu=g["w_glu"].reshape(SW, SW),
        w_branch_s=g["w_branch_s"].transpose(1, 0, 2).reshape(SW, DM),
        w_branch_a=g["w_branch_a"].transpose(1, 0, 2).reshape(AW, DM),
        w_out=g["w_out"].reshape(DM, DM),
    )


def kernel(x, pre_norm_g, w_in, lambda_re, lambda_im, log_dt, b_re, b_im, c_re, c_im, d_skip, w_glu, b_glu, w_branch_s, w_branch_a, w_out, post_norm_g, loss_target, m_pre_norm_g, m_w_in, m_lambda_re, m_lambda_im, m_log_dt, m_b_re, m_b_im, m_c_re, m_c_im, m_d_skip, m_w_glu, m_b_glu, m_w_branch_s, m_w_branch_a, m_w_out, m_post_norm_g, v_pre_norm_g, v_w_in, v_lambda_re, v_lambda_im, v_log_dt, v_b_re, v_b_im, v_c_re, v_c_im, v_d_skip, v_w_glu, v_b_glu, v_w_branch_s, v_w_branch_a, v_w_out, v_post_norm_g):
    wts = dict(pre_norm_g=pre_norm_g, w_in=w_in, lambda_re=lambda_re, lambda_im=lambda_im, log_dt=log_dt, b_re=b_re,
               b_im=b_im, c_re=c_re, c_im=c_im, d_skip=d_skip, w_glu=w_glu, b_glu=b_glu, w_branch_s=w_branch_s,
               w_branch_a=w_branch_a, w_out=w_out, post_norm_g=post_norm_g)
    mom = dict(pre_norm_g=m_pre_norm_g, w_in=m_w_in, lambda_re=m_lambda_re, lambda_im=m_lambda_im, log_dt=m_log_dt,
               b_re=m_b_re, b_im=m_b_im, c_re=m_c_re, c_im=m_c_im, d_skip=m_d_skip, w_glu=m_w_glu, b_glu=m_b_glu,
               w_branch_s=m_w_branch_s, w_branch_a=m_w_branch_a, w_out=m_w_out, post_norm_g=m_post_norm_g)
    var = dict(pre_norm_g=v_pre_norm_g, w_in=v_w_in, lambda_re=v_lambda_re, lambda_im=v_lambda_im, log_dt=v_log_dt,
               b_re=v_b_re, b_im=v_b_im, c_re=v_c_re, c_im=v_c_im, d_skip=v_d_skip, w_glu=v_w_glu, b_glu=v_b_glu,
               w_branch_s=v_w_branch_s, w_branch_a=v_w_branch_a, w_out=v_w_out, post_norm_g=v_post_norm_g)

    def gather_start(l, after):
        return _exchange_start(f"gather_start{l}", [wts[k][l].astype(BF16) for k in BIG], [True] * len(BIG), after)

    gathering = {0: gather_start(0, x)}
    sending, packed = {}, {}

    def weights_of(l, after):
        gathered = _exchange_wait(f"gather_wait{l}", gathering[l], after)
        tok = None
        if l + 1 < DEPTH:
            gathering[l + 1] = gather_start(l + 1, gathered[0])
            tok = gathering[l + 1]["token"]
        return _full_weights(dict(zip(BIG, gathered))), tok

    def big_done(l, big):
        arrs, kinds = [big[k] for k in BIG], [False] * len(BIG)
        if l + 1 < DEPTH:
            arrs, kinds = arrs + [packed[l + 1]], kinds + [True]
        sending[l] = _exchange_start(f"grads_start{l}", arrs, kinds, big["w_in"])
        return sending[l]["token"]

    def small_done(l, sm):
        packed[l] = _pack_layer(sm)

    loss, dx = _local_step(x[0], loss_target[0], wts, gathering[0]["token"], weights_of, big_done, small_done,
                           lambda part: lax.psum(part, ("x", "y", "c")))
    last = _exchange_start("grads_start_last", [packed[0]], [True], dx)
    recv_l = [_exchange_wait(f"grads_wait{l}", sending[l], [dx]) for l in range(DEPTH)]
    recv_last = _exchange_wait("grads_wait_last", last, [dx])
    recv = [jnp.stack([r[i] for r in recv_l], axis=1) for i in range(len(BIG))]
    recv.append(jnp.concatenate([recv_last[0]] + [recv_l[l][len(BIG)] for l in range(DEPTH - 1)], axis=1))

    grads, delta, new_m, new_v = {}, {}, {}, {}

    def update(k, parts, nparts):
        shape = wts[k].shape
        cols = shape[-1]
        rows = wts[k].size // cols
        br = min(rows, 1024 if cols <= 128 else 256)
        outs = _adamw("adamw_" + k, parts.reshape(nparts * rows, cols), nparts, wts[k].reshape(rows, cols),
                      mom[k].reshape(rows, cols), var[k].reshape(rows, cols), br)
        grads[k], delta[k], new_m[k], new_v[k] = (o.reshape(shape) for o in outs)

    for k, parts in zip(BIG, recv[:len(BIG)]):
        update(k, parts, NDEV)
    rows = DEPTH * PACK_BR
    (g_small,) = _ew("grads_small", lambda *p: (_sum_in_order(p),), rows, PACK_BR,
                     [_ri(recv[-1].reshape(NDEV * rows, PACK_COLS), PACK_COLS, 0, d * DEPTH) for d in range(NDEV)], [],
                     [(PACK_COLS, F32)])
    for k, g in _unpack(g_small, wts).items():
        update(k, g, 1)

    return (loss, dx[None], *[grads[k] for k in WEIGHTS], *[delta[k] for k in WEIGHTS],
            *[new_m[k] for k in WEIGHTS], *[new_v[k] for k in WEIGHTS])
```

```python
import functools
import math

import jax
import jax.numpy as jnp
from jax import lax
from jax.experimental import pallas as pl
from jax.experimental.pallas import tpu as pltpu

F32 = jnp.float32
BF16 = jnp.bfloat16

NDEV = 8
DEPTH = 4
SEQ = 2048
DM = 1024
NCOL = 8192
SW = 512
NGRP = 32
GCH = 16
NST = 64
NS = NGRP * NST
HD = 128
AW = 512
DILATIONS = (1, 4, 16)
ABLK = 128
ATTN_PAIR = 4
RMS_EPS = 1e-6
LR, B1, B2, ADAM_EPS, WD, STEP = 0.001, 0.9, 0.999, 1e-08, 0.01, 10

CB_U, CB_ZS, CB_Q, CB_K, CB_V, CB_ZA = 0, 1, 2, 5, 8, 11
CB_GS, CB_GA = 6, 7

VMEM_LIMIT = 56 * 2 ** 20


def _row_order(j):
    return jnp.where(j < CB_Q, 1, 0)


def _params(*sem):
    return pltpu.CompilerParams(dimension_semantics=sem, vmem_limit_bytes=VMEM_LIMIT)


def _ew(name, fn, rows, br, row_ins, bc_ins, row_outs, red_outs=()):
    n_in = len(row_ins) + len(bc_ins)
    n_ro = len(row_outs)
    steps = rows // br
    assert steps * br == rows

    def body(*refs):
        vals = fn(*[r[...] for r in refs[:n_in]])
        outs = refs[n_in:]
        for r, v in zip(outs[:n_ro], vals[:n_ro]):
            r[...] = v.astype(r.dtype)
        if red_outs:
            @pl.when(pl.program_id(0) == 0)
            def _():
                for r in outs[n_ro:]:
                    r[...] = jnp.zeros(r.shape, r.dtype)
            for r, v in zip(outs[n_ro:], vals[n_ro:]):
                r[...] += v

    in_specs = []
    for (_, w, cb, rb) in row_ins:
        in_specs.append(pl.BlockSpec((br, w), functools.partial(lambda i, cb, rb: (rb + i, cb), cb=cb, rb=rb)))
    for a in bc_ins:
        in_specs.append(pl.BlockSpec(a.shape, functools.partial(lambda i, nd: (0,) * nd, nd=a.ndim)))
    out_specs = [pl.BlockSpec((br, w), lambda i: (i, 0)) for (w, _) in row_outs]
    out_specs += [pl.BlockSpec((1, w), lambda i: (0, 0)) for w in red_outs]
    out_shape = [jax.ShapeDtypeStruct((rows, w), dt) for (w, dt) in row_outs]
    out_shape += [jax.ShapeDtypeStruct((1, w), F32) for w in red_outs]
    return pl.pallas_call(
        body, name=name, grid=(steps,), in_specs=in_specs, out_specs=out_specs, out_shape=out_shape,
        compiler_params=_params("arbitrary"),
    )(*[a for (a, _, _, _) in row_ins], *bc_ins)


def _ri(a, w=None, cb=0, rb=0):
    return (a, a.shape[1] if w is None else w, cb, rb)


_DIMS = {"nn": ((1,), (0,)), "nt": ((1,), (1,)), "tn": ((0,), (0,))}


def _after(after):
    return ([pl.BlockSpec(memory_space=pl.ANY)], [after]) if after is not None else ([], [])


def _mm(name, a, b, mode, M, N, K, bm, bn, bk, out_dtype, a_spec=None, b_spec=None, o_spec=None, out_shape=None,
        after=None):
    nk = K // bk
    assert M % bm == 0 and N % bn == 0 and nk * bk == K
    after_specs, after_args = _after(after)

    own_acc = nk > 1 and out_dtype != F32

    def body(a_ref, b_ref, *rest):
        o_ref, scratch = rest[len(after_args)], rest[len(after_args) + 1:]
        part = lax.dot_general(a_ref[...].astype(BF16), b_ref[...].astype(BF16), (_DIMS[mode], ((), ())),
                               preferred_element_type=F32)
        if nk == 1:
            o_ref[...] = part.astype(o_ref.dtype)
            return
        k = pl.program_id(2)
        acc_ref = scratch[0] if own_acc else o_ref

        @pl.when(k == 0)
        def _():
            acc_ref[...] = part

        @pl.when(k > 0)
        def _():
            acc_ref[...] += part

        if own_acc:
            @pl.when(k == nk - 1)
            def _():
                o_ref[...] = acc_ref[...].astype(o_ref.dtype)

    if a_spec is None:
        a_spec = (pl.BlockSpec((bk, bm), lambda i, j, k: (k, i)) if mode == "tn"
                  else pl.BlockSpec((bm, bk), lambda i, j, k: (i, k)))
    if b_spec is None:
        b_spec = (pl.BlockSpec((bn, bk), lambda i, j, k: (j, k)) if mode == "nt"
                  else pl.BlockSpec((bk, bn), lambda i, j, k: (k, j)))
    if o_spec is None:
        o_spec = pl.BlockSpec((bm, bn), lambda i, j, k: (i, j))
    if out_shape is None:
        out_shape = (M, N)
    return pl.pallas_call(
        body, name=name, grid=(M // bm, N // bn, nk), in_specs=[a_spec, b_spec] + after_specs, out_specs=o_spec,
        out_shape=jax.ShapeDtypeStruct(out_shape, out_dtype),
        scratch_shapes=[pltpu.VMEM((bm, bn), F32)] if own_acc else [],
        compiler_params=_params("parallel", "parallel", "arbitrary"),
    )(a, b, *after_args)


SCAN_LANES = 512
SCAN_CHUNKS = 8


def _to_chunked(a):
    return a.reshape(SCAN_CHUNKS, SEQ // SCAN_CHUNKS, -1).transpose(1, 0, 2).reshape(SEQ, -1)


def _from_chunked(a):
    return a.reshape(SEQ // SCAN_CHUNKS, SCAN_CHUNKS, -1).transpose(1, 0, 2).reshape(SEQ, -1)


def _scan_block(dr_ref, di_ref, sr_ref, si_ref, lam_r, lam_i, reverse):
    T = SEQ // SCAN_CHUNKS
    bl = lam_r.shape[1]
    assert T == 2 ** 8
    ar = jnp.broadcast_to(lam_r, (SCAN_CHUNKS, bl))
    ai = jnp.broadcast_to(lam_i, (SCAN_CHUNKS, bl))
    zero = jnp.zeros((SCAN_CHUNKS, bl), F32)

    def tile(j):
        return pl.ds(pl.multiple_of(j * SCAN_CHUNKS, SCAN_CHUNKS), SCAN_CHUNKS)

    def step(jj, carry):
        sr, si = carry
        j = T - 1 - jj if reverse else jj
        nr = ar * sr - ai * si + dr_ref[tile(j), :]
        ni = ar * si + ai * sr + di_ref[tile(j), :]
        sr_ref[tile(j), :] = nr
        si_ref[tile(j), :] = ni
        return nr, ni

    er, ei = lax.fori_loop(0, T, step, (zero, zero), unroll=4)

    pr, pi = ar[0:1], ai[0:1]
    for _ in range(8):
        pr, pi = pr * pr - pi * pi, 2.0 * pr * pi
    rows = lax.broadcasted_iota(jnp.int32, (SCAN_CHUNKS, bl), 0)
    cr, ci = zero, zero
    xr = jnp.zeros((1, bl), F32)
    xi = jnp.zeros((1, bl), F32)
    order = range(SCAN_CHUNKS - 2, -1, -1) if reverse else range(1, SCAN_CHUNKS)
    for c in order:
        src = c + 1 if reverse else c - 1
        nxr = pr * xr - pi * xi + er[src:src + 1]
        nxi = pr * xi + pi * xr + ei[src:src + 1]
        xr, xi = nxr, nxi
        cr = jnp.where(rows == c, xr, cr)
        ci = jnp.where(rows == c, xi, ci)

    def fix(jj, pw):
        pwr, pwi = pw
        j = T - 1 - jj if reverse else jj
        sr_ref[tile(j), :] = sr_ref[tile(j), :] + (pwr * cr - pwi * ci)
        si_ref[tile(j), :] = si_ref[tile(j), :] + (pwr * ci + pwi * cr)
        return pwr * ar - pwi * ai, pwr * ai + pwi * ar

    lax.fori_loop(0, T, fix, (ar, ai), unroll=4)


def _s5_forward(name, proj, wdt, cmt, lam_r, lam_i):
    bl = SCAN_LANES
    nblk = NS // bl
    cw = bl * GCH // NST
    assert cw == 128 and CB_U == 0

    def nt(a, b_):
        return lax.dot_general(a, b_, (((1,), (1,)), ((), ())), preferred_element_type=F32)

    def body(u_ref, wr_ref, wi_ref, cr_ref, ci_ref, ar_ref, ai_ref, s_ref, y_ref):
        sr_ref, si_ref = s_ref.at[0], s_ref.at[1]
        u = u_ref[...].astype(BF16)
        sr_ref[...] = nt(u, wr_ref[...])
        si_ref[...] = nt(u, wi_ref[...])
        _scan_block(sr_ref, si_ref, sr_ref, si_ref, ar_ref[...], ai_ref[...], False)
        y_ref[...] = nt(sr_ref[...].astype(BF16), cr_ref[...]) + nt(si_ref[...].astype(BF16), ci_ref[...])

    return pl.pallas_call(
        body, name=name, grid=(nblk,),
        in_specs=[pl.BlockSpec((SEQ, cw), lambda i: (0, i)),
                  pl.BlockSpec((bl, cw), lambda i: (i, i)), pl.BlockSpec((bl, cw), lambda i: (nblk + i, i)),
                  pl.BlockSpec((cw, bl), lambda i: (i, i)), pl.BlockSpec((cw, bl), lambda i: (SW // cw + i, i)),
                  pl.BlockSpec((1, bl), lambda i: (0, i)), pl.BlockSpec((1, bl), lambda i: (0, i))],
        out_specs=[pl.BlockSpec((2, SEQ, bl), lambda i: (0, 0, i)), pl.BlockSpec((SEQ, cw), lambda i: (0, i))],
        out_shape=[jax.ShapeDtypeStruct((2, SEQ, NS), F32), jax.ShapeDtypeStruct((SEQ, SW), F32)],
        compiler_params=_params("arbitrary"),
    )(proj, wdt, wdt, cmt, cmt, lam_r, lam_i)


S5_BWD_LANES = 256


def _s5_backward(name, dy0, proj, s, wdt, cmt, lam_r, lam_i):
    bl = S5_BWD_LANES
    nblk = NS // bl
    cw = 128
    per = cw // (bl * GCH // NST)
    assert per >= 1 and CB_U == 0

    def tn(a, b_):
        return lax.dot_general(a, b_, (((0,), (0,)), ((), ())), preferred_element_type=F32)

    def prev(s_ref):
        last = pltpu.roll(s_ref[SEQ - SCAN_CHUNKS:SEQ, :], 1, 0)
        first = jnp.where(lax.broadcasted_iota(jnp.int32, (SCAN_CHUNKS, bl), 0) > 0, last, 0.0)
        return jnp.concatenate([first, s_ref[0:SEQ - SCAN_CHUNKS, :]], axis=0)

    def body(dy_ref, u_ref, s_ref, cr_ref, ci_ref, wr_ref, wi_ref, lr_ref, li_ref,
             dlr_ref, dli_ref, dwdt_ref, dcmt_ref, du_ref, ar_ref, ai_ref):
        dy = dy_ref[...]
        ar_ref[...] = jnp.dot(dy, cr_ref[...], preferred_element_type=F32)
        ai_ref[...] = jnp.dot(dy, ci_ref[...], preferred_element_type=F32)
        _scan_block(ar_ref, ai_ref, ar_ref, ai_ref, lr_ref[...], -li_ref[...], True)
        a_r, a_i = ar_ref[...], ai_ref[...]
        sr_ref, si_ref = s_ref.at[0], s_ref.at[1]
        spr, spi = prev(sr_ref), prev(si_ref)
        dlr_ref[...] = jnp.sum(a_r * spr + a_i * spi, axis=0, keepdims=True)
        dli_ref[...] = jnp.sum(a_i * spr - a_r * spi, axis=0, keepdims=True)
        a_rb, a_ib = a_r.astype(BF16), a_i.astype(BF16)
        u = u_ref[...].astype(BF16)
        dwdt_ref[0] = tn(a_rb, u)
        dwdt_ref[1] = tn(a_ib, u)
        dcmt_ref[0] = tn(dy, sr_ref[...].astype(BF16))
        dcmt_ref[1] = tn(dy, si_ref[...].astype(BF16))
        part = (jnp.dot(a_rb, wr_ref[...], preferred_element_type=F32)
                + jnp.dot(a_ib, wi_ref[...], preferred_element_type=F32))

        @pl.when(pl.program_id(0) % per == 0)
        def _():
            du_ref[...] = part

        @pl.when(pl.program_id(0) % per > 0)
        def _():
            du_ref[...] += part

    lam_spec = pl.BlockSpec((1, bl), lambda i: (0, i))
    return pl.pallas_call(
        body, name=name, grid=(nblk,),
        in_specs=[pl.BlockSpec((SEQ, cw), lambda i: (0, i // per)), pl.BlockSpec((SEQ, cw), lambda i: (0, i // per)),
                  pl.BlockSpec((2, SEQ, bl), lambda i: (0, 0, i)),
                  pl.BlockSpec((cw, bl), lambda i: (i // per, i)),
                  pl.BlockSpec((cw, bl), lambda i: (SW // cw + i // per, i)),
                  pl.BlockSpec((bl, cw), lambda i: (i, i // per)), pl.BlockSpec((bl, cw), lambda i: (nblk + i, i // per)),
                  lam_spec, lam_spec],
        out_specs=[lam_spec, lam_spec, pl.BlockSpec((2, bl, cw), lambda i: (0, i, i // per)),
                   pl.BlockSpec((2, cw, bl), lambda i: (0, i // per, i)),
                   pl.BlockSpec((SEQ, cw), lambda i: (0, i // per))],
        out_shape=[jax.ShapeDtypeStruct((1, NS), F32), jax.ShapeDtypeStruct((1, NS), F32),
                   jax.ShapeDtypeStruct((2, NS, SW), F32), jax.ShapeDtypeStruct((2, SW, NS), F32),
                   jax.ShapeDtypeStruct((SEQ, SW), F32)],
        scratch_shapes=[pltpu.VMEM((SEQ, bl), F32)] * 2,
        compiler_params=_params("arbitrary"),
    )(dy0, proj, s, cmt, cmt, wdt, wdt, lam_r, lam_i)


def _scores(qb, kb, prev):
    s = lax.dot_general(qb, kb, (((1,), (1,)), ((), ())), preferred_element_type=F32) * (HD ** -0.5)
    row = lax.broadcasted_iota(jnp.int32, (ABLK, ABLK), 0)
    col = lax.broadcasted_iota(jnp.int32, (ABLK, ABLK), 1)
    return jnp.where((col >= row) if prev else (col <= row), s, -1e30)


def _block_rows(dil, r, b):
    if dil == 1:
        return pl.ds(pl.multiple_of(b * ABLK, ABLK), ABLK)
    return pl.ds(r + dil * ABLK * b, ABLK, stride=dil)


def _group_blocks(dil):
    nb = SEQ // dil // ABLK
    shift = nb.bit_length() - 1
    return nb, (lambda idx: (idx >> shift, idx & (nb - 1)))


def _qkv_specs(j_of):
    return [pl.BlockSpec((SEQ, HD), functools.partial(lambda j, c: (0, c + j_of(j)), c=(cb + g) * 4))
            for g in range(3) for cb in (CB_Q, CB_K, CB_V)]


def _attention_fwd(name, proj):
    def body(*refs):
        qkv, z_ref = refs[:9], refs[9]
        y_ref, ya_ref, l_ref = refs[10:13]
        accs, maxs, dens = refs[13:16], refs[16:19], refs[19:22]
        for g, dil in enumerate(DILATIONS):
            q_ref, k_ref, v_ref = qkv[3 * g:3 * g + 3]
            nb, where = _group_blocks(dil)

            def step(t, c, g=g, dil=dil, nb=nb, where=where, q_ref=q_ref, k_ref=k_ref, v_ref=v_ref):
                two = range(ATTN_PAIR)
                rb = [where(t + i * (SEQ // ABLK // ATTN_PAIR)) for i in two]
                rows = [_block_rows(dil, r, b) for r, b in rb]
                qb = [q_ref[rows[i], :].astype(BF16) for i in two]
                s_c = [_scores(qb[i], k_ref[rows[i], :].astype(BF16), False) for i in two]
                if nb > 1:
                    prev = [_block_rows(dil, r, jnp.maximum(b - 1, 0)) for r, b in rb]
                    s_p = [jnp.where(rb[i][1] > 0, _scores(qb[i], k_ref[prev[i], :].astype(BF16), True), -1e30)
                           for i in two]
                m = [jnp.max(s_c[i], axis=-1, keepdims=True) for i in two]
                if nb > 1:
                    m = [jnp.maximum(m[i], jnp.max(s_p[i], axis=-1, keepdims=True)) for i in two]
                p_c = [jnp.exp(s_c[i] - m[i]) for i in two]
                den = [jnp.sum(p_c[i], axis=-1, keepdims=True) for i in two]
                acc = [jnp.dot(p_c[i].astype(BF16), v_ref[rows[i], :].astype(BF16), preferred_element_type=F32)
                       for i in two]
                if nb > 1:
                    p_p = [jnp.exp(s_p[i] - m[i]) for i in two]
                    den = [den[i] + jnp.sum(p_p[i], axis=-1, keepdims=True) for i in two]
                    acc = [acc[i] + jnp.dot(p_p[i].astype(BF16), v_ref[prev[i], :].astype(BF16),
                                            preferred_element_type=F32) for i in two]
                for i in two:
                    accs[g][rows[i], :] = acc[i]
                    maxs[g][rows[i], :] = jnp.broadcast_to(m[i], (ABLK, HD))
                    dens[g][rows[i], :] = jnp.broadcast_to(den[i], (ABLK, HD))
                return c

            lax.fori_loop(0, SEQ // ABLK // ATTN_PAIR, step, 0)
        top = jnp.maximum(jnp.maximum(maxs[0][...], maxs[1][...]), maxs[2][...])
        den = jnp.zeros((SEQ, HD), F32)
        y = jnp.zeros((SEQ, HD), F32)
        for g in range(3):
            wgt = jnp.exp(maxs[g][...] - top)
            den = den + wgt * dens[g][...]
            y = y + wgt * accs[g][...]
        y = y / den
        y_ref[...] = y
        ya_ref[...] = (y * _silu(z_ref[...])).astype(ya_ref.dtype)
        l_ref[...] = top + jnp.log(den)

    ospec = pl.BlockSpec((SEQ, HD), lambda j: (0, j))
    return pl.pallas_call(
        body, name=name, grid=(AW // HD,),
        in_specs=_qkv_specs(lambda j: j) + [pl.BlockSpec((SEQ, HD), lambda j: (0, CB_ZA * 4 + j))],
        out_specs=[ospec, ospec, ospec],
        out_shape=[jax.ShapeDtypeStruct((SEQ, AW), F32), jax.ShapeDtypeStruct((SEQ, AW), BF16),
                   jax.ShapeDtypeStruct((SEQ, AW), F32)],
        scratch_shapes=[pltpu.VMEM((SEQ, HD), F32)] * 9,
        compiler_params=_params("parallel"),
    )(*([proj] * 10))


def _attention_bwd(name, proj, dya, y, lse):
    def tn(a, b_):
        return lax.dot_general(a, b_, (((0,), (0,)), ((), ())), preferred_element_type=F32)

    def nt(a, b_):
        return lax.dot_general(a, b_, (((1,), (1,)), ((), ())), preferred_element_type=F32)

    def body(*refs):
        qkv, z_ref, dya_ref, y_ref, l_ref = refs[:9], refs[9], refs[10], refs[11], refs[12]
        outs, dza_ref = refs[13:22], refs[22]
        dy_s, dsum_s, dq_s, dk_own, dv_own, dk_prev, dv_prev = refs[23:]
        _, vjp = jax.vjp(lambda y_, z_: y_ * _silu(z_), y_ref[...], z_ref[...])
        dy, dz = vjp(dya_ref[...])
        dza_ref[...] = dz.astype(dza_ref.dtype)
        dy_s[...] = dy
        dsum_s[...] = jnp.broadcast_to(jnp.sum(dy * y_ref[...], axis=-1, keepdims=True), (SEQ, HD))
        for g, dil in enumerate(DILATIONS):
            q_ref, k_ref, v_ref = qkv[3 * g:3 * g + 3]
            nb, where = _group_blocks(dil)
            if nb > 1:
                dk_prev[...] = jnp.zeros(dk_prev.shape, F32)
                dv_prev[...] = jnp.zeros(dv_prev.shape, F32)

            def step(t, c, dil=dil, nb=nb, where=where, q_ref=q_ref, k_ref=k_ref, v_ref=v_ref):
                rb = [where(t + i * (SEQ // ABLK // ATTN_PAIR)) for i in range(ATTN_PAIR)]
                sides = []
                for r, b in rb:
                    rows = _block_rows(dil, r, b)
                    own = dict(b=b, qrows=rows, krows=rows, prev=False, dk=dk_own, dv=dv_own,
                               q=q_ref[rows, :].astype(BF16), dy=dy_s[rows, :].astype(BF16))
                    sides.append(own)
                    if nb > 1:
                        sides.append(dict(own, krows=_block_rows(dil, r, jnp.maximum(b - 1, 0)), prev=True,
                                          dk=dk_prev, dv=dv_prev))
                for s_ in sides:
                    s_["k"] = k_ref[s_["krows"], :].astype(BF16)
                    s_["v"] = v_ref[s_["krows"], :].astype(BF16)
                for s_ in sides:
                    sc = _scores(s_["q"], s_["k"], s_["prev"])
                    s_["s"] = jnp.where(s_["b"] > 0, sc, -1e30) if s_["prev"] else sc
                    s_["dp"] = nt(s_["dy"], s_["v"])
                for s_ in sides:
                    p = jnp.exp(s_["s"] - l_ref[s_["qrows"], :])
                    s_["p"] = p.astype(BF16)
                    s_["ds"] = (p * (s_["dp"] - dsum_s[s_["qrows"], :]) * (HD ** -0.5)).astype(BF16)
                for s_ in sides:
                    s_["dk"][s_["krows"], :] = tn(s_["ds"], s_["q"])
                    s_["dv"][s_["krows"], :] = tn(s_["p"], s_["dy"])
                    s_["dq"] = jnp.dot(s_["ds"], s_["k"], preferred_element_type=F32)
                per = len(sides) // ATTN_PAIR
                for i in range(ATTN_PAIR):
                    dq = sides[i * per]["dq"]
                    if per > 1:
                        dq = dq + sides[i * per + 1]["dq"]
                    dq_s[sides[i * per]["qrows"], :] = dq
                return c

            lax.fori_loop(0, SEQ // ABLK // ATTN_PAIR, step, 0)
            dq_ref, dk_ref, dv_ref = outs[3 * g:3 * g + 3]
            dq_ref[...] = dq_s[...].astype(dq_ref.dtype)
            if nb > 1:
                dk_ref[...] = (dk_own[...] + dk_prev[...]).astype(dk_ref.dtype)
                dv_ref[...] = (dv_own[...] + dv_prev[...]).astype(dv_ref.dtype)
            else:
                dk_ref[...] = dk_own[...].astype(dk_ref.dtype)
                dv_ref[...] = dv_own[...].astype(dv_ref.dtype)

    ospec = pl.BlockSpec((SEQ, HD), lambda j: (0, j))
    outs = pl.pallas_call(
        body, name=name, grid=(AW // HD,),
        in_specs=_qkv_specs(lambda j: j) + [pl.BlockSpec((SEQ, HD), lambda j: (0, CB_ZA * 4 + j))] + [ospec] * 3,
        out_specs=[ospec] * 10, out_shape=[jax.ShapeDtypeStruct((SEQ, AW), BF16)] * 10,
        scratch_shapes=[pltpu.VMEM((SEQ, HD), F32)] * 7,
        compiler_params=_params("parallel"),
    )(*([proj] * 10), dya, y, lse)
    return outs[:9], outs[9]


def _rms(x, g):
    return x * lax.rsqrt(jnp.mean(x * x, axis=-1, keepdims=True) + RMS_EPS) * g


def _sig(x):
    return 1.0 / (1.0 + jnp.exp(-x))


def _silu(x):
    return x * _sig(x)


def _gelu(x):
    return 0.5 * x * (1.0 + jnp.tanh(math.sqrt(2.0 / math.pi) * (x + 0.044715 * (x * x * x))))


def _y1_fn(y0p, u, dskip):
    return _gelu(y0p + dskip * u)


def _ys_fn(y1, t, z, bglu):
    return y1 * _sig(t + bglu) * _silu(z)


def _merge_fn(ms, ma, gs, ga):
    return _sig(gs) * ms + _sig(ga) * ma


def _colsum(v):
    return jnp.sum(v, axis=0, keepdims=True)


def _lam_fn(lre, lim, ldt):
    a = jnp.minimum(lre, -1e-4)
    dt = jnp.exp(ldt)
    mag = jnp.exp(a * dt)
    ar = mag * jnp.cos(lim * dt)
    ai = mag * jnp.sin(lim * dt)
    den = a * a + lim * lim
    cr = ((ar - 1.0) * a + ai * lim) / den
    ci = (ai * a - (ar - 1.0) * lim) / den
    return ar, ai, cr, ci


def _bbar_fn(cr, ci, bre, bim):
    return cr * bre - ci * bim, cr * bim + ci * bre


def _same_group(rows, a, cols, b):
    r = lax.broadcasted_iota(jnp.int32, (rows, cols), 0) >> (a.bit_length() - 1)
    c = lax.broadcasted_iota(jnp.int32, (rows, cols), 1) >> (b.bit_length() - 1)
    return r == c


def _expand(name, blocks, signs, a, b, dtype, after=None):
    rows, cols = NGRP * a, NGRP * b
    n = len(blocks)
    assert a & (a - 1) == 0 and b & (b - 1) == 0
    after_specs, after_args = _after(after)

    def body(*refs):
        o_ref = refs[-1]
        tile = (lax.broadcasted_iota(jnp.int32, (b, cols), 1) & (b - 1)
                == lax.broadcasted_iota(jnp.int32, (b, cols), 0)).astype(F32)
        keep = _same_group(rows, a, cols, b)
        for i, (ref, sign) in enumerate(zip(refs[:n], signs)):
            spread = jnp.dot(ref[...], tile, preferred_element_type=F32, precision=lax.Precision.HIGHEST)
            o_ref[i * rows:(i + 1) * rows, :] = jnp.where(keep, sign * spread, 0.0).astype(o_ref.dtype)

    return pl.pallas_call(body, name=name, grid=(1,),
                          in_specs=[pl.BlockSpec((rows, b), lambda i: (0, 0))] * n + after_specs,
                          out_specs=pl.BlockSpec((n * rows, cols), lambda i: (0, 0)),
                          out_shape=jax.ShapeDtypeStruct((n * rows, cols), dtype),
                          compiler_params=_params("arbitrary"))(*blocks, *after_args)


def _extract(name, m, a, b, at=(0, 0), after=None):
    rows, cols = NGRP * a, NGRP * b
    assert a & (a - 1) == 0 and b & (b - 1) == 0
    after_specs, after_args = _after(after)

    def body(m_ref, *rest):
        o_ref = rest[-1]
        tile = (lax.broadcasted_iota(jnp.int32, (cols, b), 0) & (b - 1)
                == lax.broadcasted_iota(jnp.int32, (cols, b), 1)).astype(F32)
        kept = jnp.where(_same_group(rows, a, cols, b), m_ref[...], 0.0)
        o_ref[...] = jnp.dot(kept, tile, preferred_element_type=F32, precision=lax.Precision.HIGHEST)

    return pl.pallas_call(body, name=name, grid=(1,),
                          in_specs=[pl.BlockSpec((rows, cols), lambda i: at)] + after_specs,
                          out_specs=pl.BlockSpec((rows, b), lambda i: (0, 0)),
                          out_shape=jax.ShapeDtypeStruct((rows, b), F32),
                          compiler_params=_params("arbitrary"))(m, *after_args)


def _s5_prepare(l, sp, after):
    tag = f"l{l}_"
    ar, ai, cr, ci = _ew(tag + "lam", _lam_fn, NGRP, NGRP,
                         [_ri(sp["lambda_re"]), _ri(sp["lambda_im"]),
                          _ri(sp["log_dt"].reshape(NGRP, 1) + after[0, 0])], [], [(NST, F32)] * 4)
    bre = sp["b_re"].reshape(NS, GCH)
    bim = sp["b_im"].reshape(NS, GCH)
    bbr, bbi = _ew(tag + "bbar", _bbar_fn, NS, NS, [_ri(cr.reshape(NS, 1)), _ri(ci.reshape(NS, 1)), _ri(bre), _ri(bim)],
                   [], [(GCH, F32)] * 2)
    wdt = _expand(tag + "wdt", [bbr, bbi], [1.0, 1.0], NST, GCH, BF16)
    cmt = _expand(tag + "cmt", [sp["c_re"].reshape(SW, NST), sp["c_im"].reshape(SW, NST)], [1.0, -1.0], GCH, NST, BF16,
                  after=after)
    return dict(ar=ar, ai=ai, cr=cr, ci=ci, wdt=wdt, cmt=cmt)


def _layer_fwd(l, x, w, sp, prep):
    tag = f"l{l}_"
    g1 = sp["pre_norm_g"].reshape(1, DM)
    (h,) = _ew(tag + "rms1", lambda x_, g: (_rms(x_, g),), SEQ, 256, [_ri(x)], [g1], [(DM, BF16)])
    hv = jnp.stack([h, _to_chunked(h)])
    win = w["w_in"]
    proj = _mm(tag + "proj", hv, win, "nn", SEQ, NCOL, DM, SEQ, 512, 1024, F32,
               a_spec=pl.BlockSpec((None, SEQ, 1024), lambda i, j, k: (_row_order(j), 0, 0)),
               b_spec=pl.BlockSpec((None, 1024, 512), lambda i, j, k: (j // 2, 0, j % 2)))

    ar, ai, cr, ci, wdt, cmt = (prep[k] for k in ("ar", "ai", "cr", "ci", "wdt", "cmt"))
    s, y0p = _s5_forward(tag + "s5", proj, wdt, cmt, ar.reshape(1, NS), ai.reshape(1, NS))
    dskip = sp["d_skip"].reshape(1, SW)
    (y1,) = _ew(tag + "y1", lambda a, u, d: (_y1_fn(a, u, d),), SEQ, 256,
                [_ri(y0p), _ri(proj, SW, CB_U)], [dskip], [(SW, F32)])
    t = _mm(tag + "glu", y1, w["w_glu"], "nn", SEQ, SW, SW, 1024, 512, 512, F32)
    bglu = sp["b_glu"].reshape(1, SW)
    (ys_c,) = _ew(tag + "ys", lambda y1_, t_, z, b_: (_ys_fn(y1_, t_, z, b_),), SEQ, 256,
                  [_ri(y1), _ri(t), _ri(proj, SW, CB_ZS)], [bglu], [(SW, BF16)])
    ys = _from_chunked(ys_c)

    ypre, ya, lse = _attention_fwd(tag + "attn", proj)

    ms = _mm(tag + "branch_s", ys, w["w_branch_s"], "nn", SEQ, DM, SW, 1024, 1024, 512, F32)
    ma = _mm(tag + "branch_a", ya, w["w_branch_a"], "nn", SEQ, DM, AW, 1024, 1024, 512, F32)
    (merged,) = _ew(tag + "merge", lambda a, b_, c, d: (_merge_fn(a, b_, c, d),), SEQ, 256,
                    [_ri(ms), _ri(ma), _ri(proj, DM, CB_GS), _ri(proj, DM, CB_GA)], [], [(DM, BF16)])
    out = _mm(tag + "out", merged, w["w_out"], "nn", SEQ, DM, DM, 1024, 1024, 1024, F32)
    g2 = sp["post_norm_g"].reshape(1, DM)
    (x_new,) = _ew(tag + "post", lambda x_, o, g: (x_ + _rms(o, g),), SEQ, 256, [_ri(x), _ri(out)], [g2], [(DM, F32)])
    res = dict(x=x, hv=hv, proj=proj, ar=ar, ai=ai, cr=cr, ci=ci, wdt=wdt, cmt=cmt, s=s, y0p=y0p,
               y1=y1, t=t, ys=ys, ya=ya, ypre=ypre, lse=lse, ms=ms, ma=ma, merged=merged, out=out)
    return x_new, res


def _layer_bwd(l, dxn, r, w, sp, big_done, after=None):
    tag = f"l{l}b_"
    proj = r["proj"]
    g1 = sp["pre_norm_g"].reshape(1, DM)
    g2 = sp["post_norm_g"].reshape(1, DM)
    dskip = sp["d_skip"].reshape(1, SW)
    bglu = sp["b_glu"].reshape(1, SW)

    def post_b(d, o, g):
        _, vjp = jax.vjp(_rms, o, g)
        do, dg = vjp(d)
        return do, dg

    d_out, dg2 = _ew(tag + "post", post_b, SEQ, 256, [_ri(dxn), _ri(r["out"])], [g2], [(DM, BF16)], [DM])
    dw_out = _mm(tag + "dw_out", r["merged"], d_out, "tn", DM, DM, SEQ, 1024, 1024, SEQ, BF16, after=after)
    dmerged = _mm(tag + "dmerged", d_out, w["w_out"], "nt", SEQ, DM, DM, 1024, 1024, 1024, F32, after=after)

    def merge_b(d, ms, ma, gs, ga):
        _, vjp = jax.vjp(_merge_fn, ms, ma, gs, ga)
        return vjp(d)

    dms, dma, dgs, dga = _ew(tag + "merge", merge_b, SEQ, 256,
                             [_ri(dmerged), _ri(r["ms"]), _ri(r["ma"]), _ri(proj, DM, CB_GS), _ri(proj, DM, CB_GA)],
                             [], [(DM, BF16)] * 4)
    dw_bs = _mm(tag + "dw_bs", r["ys"], dms, "tn", SW, DM, SEQ, 512, 1024, SEQ, BF16)
    dw_ba = _mm(tag + "dw_ba", r["ya"], dma, "tn", AW, DM, SEQ, 512, 1024, SEQ, BF16)
    dys = _mm(tag + "dys", dms, w["w_branch_s"], "nt", SEQ, SW, DM, 1024, 512, 1024, F32)
    dya = _mm(tag + "dya", dma, w["w_branch_a"], "nt", SEQ, AW, DM, 1024, 512, 1024, F32)

    dqkv, dza = _attention_bwd(tag + "attn", proj, dya, r["ypre"], r["lse"])

    def ys_b(d, y1, t, z, b_):
        _, vjp = jax.vjp(_ys_fn, y1, t, z, b_)
        dy1, dt, dz, _ = vjp(d)
        return dy1, dt, dz, _colsum(dt)

    dy1a, dt, dzs, dbglu = _ew(tag + "ys", ys_b, SEQ, 256,
                               [_ri(_to_chunked(dys)), _ri(r["y1"]), _ri(r["t"]), _ri(proj, SW, CB_ZS)],
                               [bglu], [(SW, F32), (SW, BF16), (SW, BF16)], [SW])
    dw_glu = _mm(tag + "dw_glu", r["y1"], dt, "tn", SW, SW, SEQ, 512, 512, SEQ, BF16)
    dy1b = _mm(tag + "dy1b", dt, w["w_glu"], "nt", SEQ, SW, SW, 1024, 512, 512, F32)

    def y1_b(da, db, y0p, u, d_):
        _, vjp = jax.vjp(_y1_fn, y0p, u, d_)
        dy0, du, dd = vjp(da + db)
        return dy0, du, dd

    dy0, du_skip, ddskip = _ew(tag + "y1", y1_b, SEQ, 256,
                               [_ri(dy1a), _ri(dy1b), _ri(r["y0p"]), _ri(proj, SW, CB_U)], [dskip],
                               [(SW, BF16), (SW, F32)], [SW])
    dlr, dli, dwdt, dcmt, du_s = _s5_backward(tag + "s5", dy0, proj, r["s"], r["wdt"], r["cmt"],
                                              r["ar"].reshape(1, NS), r["ai"].reshape(1, NS))
    (du,) = _ew(tag + "du", lambda a, c: (a + c,), SEQ, 256, [_ri(du_s), _ri(du_skip)], [], [(SW, BF16)])

    dq, dk, dv = ([dqkv[3 * g + i] for g in range(3)] for i in range(3))
    dproj = jnp.concatenate([du, dzs, *dq, *dk, *dv, dza, dgs, dga], axis=1)
    dw_in = _mm(tag + "dw_in", r["hv"], dproj, "tn", DM, NCOL, SEQ, 1024, 512, SEQ, BF16,
                a_spec=pl.BlockSpec((None, SEQ, 1024), lambda i, j, k: (_row_order(j), 0, 0)),
                o_spec=pl.BlockSpec((None, 1024, 512), lambda i, j, k: (j // 2, 0, j % 2)), out_shape=(NDEV, DM, DM))
    tok = big_done(l, dict(w_in=dw_in, w_glu=dw_glu.reshape(NDEV, SW // NDEV, SW),
                           w_branch_s=dw_bs.reshape(SW, NDEV, DM // NDEV).transpose(1, 0, 2),
                           w_branch_a=dw_ba.reshape(AW, NDEV, DM // NDEV).transpose(1, 0, 2),
                           w_out=dw_out.reshape(NDEV, DM // NDEV, DM)))
    if tok is not None:
        g1 = g1 + tok[0, 0]

    dwdt = dwdt.reshape(2 * NS, SW)
    dcmt = dcmt.reshape(2 * SW, NS)
    dbbr = _extract(tag + "dbbr", dwdt, NST, GCH, (0, 0), after=tok)
    dbbi = _extract(tag + "dbbi", dwdt, NST, GCH, (1, 0), after=tok)
    bre = sp["b_re"].reshape(NS, GCH)
    bim = sp["b_im"].reshape(NS, GCH)

    def bbar_b(cr, ci, br_, bi_, dr, di):
        _, vjp = jax.vjp(_bbar_fn, cr, ci, br_, bi_)
        return vjp((dr, di))

    dcr, dci, dbre, dbim = _ew(tag + "bbar", bbar_b, NS, NS,
                               [_ri(r["cr"].reshape(NS, 1)), _ri(r["ci"].reshape(NS, 1)), _ri(bre), _ri(bim),
                                _ri(dbbr), _ri(dbbi)], [], [(1, F32), (1, F32), (GCH, F32), (GCH, F32)])

    def lam_b(lre, lim, ldt, dar, dai, dcr_, dci_):
        _, vjp = jax.vjp(_lam_fn, lre, lim, ldt)
        return vjp((dar, dai, dcr_, dci_))

    dlre, dlim, dldt = _ew(tag + "lam", lam_b, NGRP, NGRP,
                           [_ri(sp["lambda_re"]), _ri(sp["lambda_im"]), _ri(sp["log_dt"].reshape(NGRP, 1)),
                            _ri(dlr.reshape(NGRP, NST)), _ri(dli.reshape(NGRP, NST)),
                            _ri(dcr.reshape(NGRP, NST)), _ri(dci.reshape(NGRP, NST))], [],
                           [(NST, F32), (NST, F32), (1, F32)])
    dc_re = _extract(tag + "dc_re", dcmt, GCH, NST, (0, 0), after=tok).reshape(NGRP, GCH, NST)
    dc_im = -_extract(tag + "dc_im", dcmt, GCH, NST, (1, 0), after=tok).reshape(NGRP, GCH, NST)

    dh_time = _mm(tag + "dh_time", dproj, w["w_in"], "nt", SEQ, DM, NCOL - DM, SEQ, 1024, 1024, F32,
                  a_spec=pl.BlockSpec((SEQ, 1024), lambda i, j, k: (0, 1 + k)),
                  b_spec=pl.BlockSpec((None, 1024, 1024), lambda i, j, k: (1 + k, 0, 0)), after=tok)
    dh_chunked = _mm(tag + "dh_chunked", dproj, w["w_in"], "nt", SEQ, DM, DM, SEQ, 1024, 1024, F32,
                     a_spec=pl.BlockSpec((SEQ, 1024), lambda i, j, k: (0, 0)),
                     b_spec=pl.BlockSpec((None, 1024, 1024), lambda i, j, k: (0, 0, 0)), after=tok)
    dh = [dh_time, _from_chunked(dh_chunked)]

    def pre_b(d, dh0, dh1, x_, g):
        _, vjp = jax.vjp(_rms, x_, g)
        dx_, dg = vjp(dh0 + dh1)
        return d + dx_, dg

    dx, dg1 = _ew(tag + "pre", pre_b, SEQ, 256, [_ri(dxn)] + [_ri(t_) for t_ in dh] + [_ri(r["x"])], [g1],
                  [(DM, F32)], [DM])

    small = dict(pre_norm_g=dg1.reshape(DM), lambda_re=dlre, lambda_im=dlim, log_dt=dldt.reshape(NGRP),
                 b_re=dbre.reshape(NGRP, NST, GCH), b_im=dbim.reshape(NGRP, NST, GCH), c_re=dc_re, c_im=dc_im,
                 d_skip=ddskip.reshape(SW), b_glu=dbglu.reshape(SW), post_norm_g=dg2.reshape(DM))
    return dx, small


_HBM = pl.BlockSpec(memory_space=pltpu.HBM)
_SEM = pl.BlockSpec(memory_space=pltpu.SEMAPHORE)
_EFFECT = pltpu.SideEffectType.DATAFLOW_SIDE_EFFECTING


def _remote_copies(srcs, dsts, send_sems, recv_sems, gather):
    x, y, c = lax.axis_index("x"), lax.axis_index("y"), lax.axis_index("c")
    me = 4 * x + 2 * y + c
    copies = []
    for i in range(len(srcs)):
        for k in range(1, NDEV):
            peer = (x ^ (k >> 2), y ^ ((k >> 1) & 1), c ^ (k & 1))
            src = srcs[i] if gather[i] else srcs[i].at[me ^ k]
            copies.append(pltpu.make_async_remote_copy(
                src_ref=src, dst_ref=dsts[i].at[me], send_sem=send_sems[i], recv_sem=recv_sems[i],
                device_id=peer, device_id_type=pl.DeviceIdType.MESH))
    return copies


def _all_seven(dst, send_sem, recv_sem):
    seven = dst.at[pl.ds(0, NDEV - 1)]
    me = (lax.axis_index("x"), lax.axis_index("y"), lax.axis_index("c"))
    return pltpu.make_async_remote_copy(src_ref=seven, dst_ref=seven, send_sem=send_sem, recv_sem=recv_sem,
                                        device_id=me, device_id_type=pl.DeviceIdType.MESH)


def _own_slabs(name, arrs, gather, after):
    n = len(arrs)
    me = (4 * lax.axis_index("x") + 2 * lax.axis_index("y") + lax.axis_index("c")).astype(jnp.int32).reshape(1)

    def body(me_ref, *refs):
        for src, dst in zip(refs[:n], refs[n + 1:]):
            dst[...] = src[...]

    def zeros(k):
        return (0,) * k

    in_specs, out_specs, out_shape = [], [], []
    for a, g in zip(arrs, gather):
        slab = a.shape if g else a.shape[1:]
        nd = len(slab)
        if g:
            in_specs.append(pl.BlockSpec(slab, functools.partial(lambda i, me_ref, nd: zeros(nd), nd=nd)))
        else:
            in_specs.append(pl.BlockSpec((None,) + slab, functools.partial(lambda i, me_ref, nd: (me_ref[0],) + zeros(nd), nd=nd)))
        out_specs.append(pl.BlockSpec((None,) + slab, functools.partial(lambda i, me_ref, nd: (me_ref[0],) + zeros(nd), nd=nd)))
        out_shape.append(jax.ShapeDtypeStruct((NDEV,) + slab, a.dtype))
    in_specs.append(pl.BlockSpec(memory_space=pl.ANY))
    return pl.pallas_call(
        body, name=name, out_shape=out_shape,
        grid_spec=pltpu.PrefetchScalarGridSpec(num_scalar_prefetch=1, grid=(1,), in_specs=in_specs, out_specs=out_specs),
        compiler_params=_params("arbitrary"),
    )(me, *arrs, after)


def _exchange_start(name, arrs, gather, after):
    n = len(arrs)
    lands = _own_slabs(name + "_own", arrs, gather, after)

    def body(*refs):
        srcs, dsts = refs[:n], refs[n:2 * n]
        send_sems, recv_sems = refs[2 * n:3 * n], refs[3 * n:4 * n]
        token = refs[-1]
        for cp in _remote_copies(srcs, dsts, send_sems, recv_sems, gather):
            cp.start()
        token[...] = jnp.zeros(token.shape, token.dtype)

    thru = [pltpu.HBM(a.shape, a.dtype) for a in list(arrs) + list(lands)]
    outs = pl.pallas_call(
        body, name=name,
        out_shape=(*[pltpu.SemaphoreType.DMA(())] * (2 * n), *thru, jax.ShapeDtypeStruct((8, 128), F32)),
        in_specs=[_HBM] * (2 * n),
        out_specs=(*[_SEM] * (2 * n), *[_HBM] * (2 * n), pl.BlockSpec(memory_space=pltpu.VMEM)),
        input_output_aliases={i: 2 * n + i for i in range(2 * n)},
        compiler_params=pltpu.CompilerParams(has_side_effects=_EFFECT),
    )(*[pltpu.with_memory_space_constraint(a, pltpu.HBM) for a in list(arrs) + list(lands)])
    return dict(send=outs[:n], recv=outs[n:2 * n], srcs=outs[2 * n:3 * n], lands=outs[3 * n:4 * n], token=outs[-1],
                gather=gather)


def _exchange_wait(name, started, after):
    n = len(started["srcs"])
    after = list(after)

    def body(*refs):
        dsts = refs[n:2 * n]
        send_sems, recv_sems = refs[2 * n:3 * n], refs[3 * n:4 * n]
        for i in range(n):
            cp = _all_seven(dsts[i], send_sems[i], recv_sems[i])
            cp.wait_send()
            cp.wait_recv()

    bufs = list(started["srcs"]) + list(started["lands"])
    outs = pl.pallas_call(
        body, name=name, out_shape=tuple(pltpu.HBM(a.shape, a.dtype) for a in bufs),
        in_specs=[_HBM] * (2 * n) + [_SEM] * (2 * n) + [pl.BlockSpec(memory_space=pl.ANY)] * len(after),
        out_specs=(_HBM,) * (2 * n), input_output_aliases={i: i for i in range(2 * n)},
        compiler_params=pltpu.CompilerParams(has_side_effects=_EFFECT),
    )(*bufs, *started["send"], *started["recv"], *after)
    return outs[n:]


def _sum_in_order(parts):
    g = parts[0].astype(F32)
    for p in parts[1:]:
        g = g + p.astype(F32)
    return g


def _adamw(name, parts, nparts, w, m, v, br):
    rows, cols = w.shape

    def fn(*a):
        g = _sum_in_order(a[:nparts])
        w_, m_, v_ = a[nparts:]
        m2 = B1 * m_ + (1.0 - B1) * g
        v2 = B2 * v_ + (1.0 - B2) * (g * g)
        m_hat = m2 / (1.0 - B1 ** STEP)
        v_hat = v2 / (1.0 - B2 ** STEP)
        delta = -LR * (m_hat / (jnp.sqrt(v_hat) + ADAM_EPS) + WD * w_)
        return g, delta, m2, v2

    ins = [_ri(parts, cols, 0, d * (rows // br)) for d in range(nparts)] + [_ri(w), _ri(m), _ri(v)]
    return _ew(name, fn, rows, br, ins, [], [(cols, F32)] * 4)


SMALL = ("pre_norm_g", "lambda_re", "lambda_im", "log_dt", "b_re", "b_im", "c_re", "c_im", "d_skip", "b_glu",
         "post_norm_g")
BIG = ("w_in", "w_glu", "w_branch_s", "w_branch_a", "w_out")
WEIGHTS = ("pre_norm_g", "w_in", "lambda_re", "lambda_im", "log_dt", "b_re", "b_im", "c_re", "c_im", "d_skip",
           "w_glu", "b_glu", "w_branch_s", "w_branch_a", "w_out", "post_norm_g")
PACK_COLS = 1024
PACK_BR = 136


def _pack_layer(d):
    pieces = [d[k].astype(F32).reshape(-1) for k in SMALL]
    used = sum(p.shape[0] for p in pieces)
    assert used <= PACK_BR * PACK_COLS
    return jnp.concatenate(pieces + [jnp.zeros((PACK_BR * PACK_COLS - used,), F32)]).reshape(PACK_BR, PACK_COLS)


def _unpack(p, like):
    flat = p.reshape(DEPTH, PACK_BR * PACK_COLS)
    out, off = {}, 0
    for k in SMALL:
        n = like[k].size // DEPTH
        out[k] = flat[:, off:off + n].reshape(like[k].shape)
        off += n
    return out


def _local_step(x, target, small, started, weights_of, big_done, small_done, total_loss):
    preps = [_s5_prepare(l, {k: small[k][l] for k in SMALL}, started) for l in range(DEPTH)]
    res, ws = [], []
    for l in range(DEPTH):
        w_l, tok = weights_of(l, [x] + ([p[k] for p in preps for k in ("wdt", "cmt")] if l == 0 else []))
        sp = {k: small[k][l] for k in SMALL}
        if tok is not None:
            sp["pre_norm_g"] = sp["pre_norm_g"] + tok[0, 0]
        x, r = _layer_fwd(l, x, w_l, sp, preps[l])
        res.append(r)
        ws.append(w_l)

    def loss_fn(y, t):
        e = y - t
        return e * (1.0 / DM), jnp.sum(_colsum(0.5 * e * e * (1.0 / DM)), axis=1, keepdims=True)

    dx, loss = _ew("loss", loss_fn, SEQ, 256, [_ri(x), _ri(target)], [], [(DM, F32)], [1])
    total = total_loss(loss.reshape(()))
    for l in reversed(range(DEPTH)):
        dx, sm = _layer_bwd(l, dx, res[l], ws[l], {k: small[k][l] for k in SMALL}, big_done,
                            after=total.reshape(1, 1) if l == DEPTH - 1 else None)
        small_done(l, sm)
    return total, dx


def _full_weights(gathered):
    g = gathered
    return dict(
        w_in=g["w_in"],
        w_glu=g["w_glu"].reshape(SW, SW),
        w_branch_s=g["w_branch_s"].transpose(1, 0, 2).reshape(SW, DM),
        w_branch_a=g["w_branch_a"].transpose(1, 0, 2).reshape(AW, DM),
        w_out=g["w_out"].reshape(DM, DM),
    )


def kernel(x, pre_norm_g, w_in, lambda_re, lambda_im, log_dt, b_re, b_im, c_re, c_im, d_skip, w_glu, b_glu, w_branch_s, w_branch_a, w_out, post_norm_g, loss_target, m_pre_norm_g, m_w_in, m_lambda_re, m_lambda_im, m_log_dt, m_b_re, m_b_im, m_c_re, m_c_im, m_d_skip, m_w_glu, m_b_glu, m_w_branch_s, m_w_branch_a, m_w_out, m_post_norm_g, v_pre_norm_g, v_w_in, v_lambda_re, v_lambda_im, v_log_dt, v_b_re, v_b_im, v_c_re, v_c_im, v_d_skip, v_w_glu, v_b_glu, v_w_branch_s, v_w_branch_a, v_w_out, v_post_norm_g):
    wts = dict(pre_norm_g=pre_norm_g, w_in=w_in, lambda_re=lambda_re, lambda_im=lambda_im, log_dt=log_dt, b_re=b_re,
               b_im=b_im, c_re=c_re, c_im=c_im, d_skip=d_skip, w_glu=w_glu, b_glu=b_glu, w_branch_s=w_branch_s,
               w_branch_a=w_branch_a, w_out=w_out, post_norm_g=post_norm_g)
    mom = dict(pre_norm_g=m_pre_norm_g, w_in=m_w_in, lambda_re=m_lambda_re, lambda_im=m_lambda_im, log_dt=m_log_dt,
               b_re=m_b_re, b_im=m_b_im, c_re=m_c_re, c_im=m_c_im, d_skip=m_d_skip, w_glu=m_w_glu, b_glu=m_b_glu,
               w_branch_s=m_w_branch_s, w_branch_a=m_w_branch_a, w_out=m_w_out, post_norm_g=m_post_norm_g)
    var = dict(pre_norm_g=v_pre_norm_g, w_in=v_w_in, lambda_re=v_lambda_re, lambda_im=v_lambda_im, log_dt=v_log_dt,
               b_re=v_b_re, b_im=v_b_im, c_re=v_c_re, c_im=v_c_im, d_skip=v_d_skip, w_glu=v_w_glu, b_glu=v_b_glu,
               w_branch_s=v_w_branch_s, w_branch_a=v_w_branch_a, w_out=v_w_out, post_norm_g=v_post_norm_g)

    def gather_start(l, after):
        return _exchange_start(f"gather_start{l}", [wts[k][l].astype(BF16) for k in BIG], [True] * len(BIG), after)

    gathering = {0: gather_start(0, x)}
    sending, packed = {}, {}

    def weights_of(l, after):
        gathered = _exchange_wait(f"gather_wait{l}", gathering[l], after)
        tok = None
        if l + 1 < DEPTH:
            gathering[l + 1] = gather_start(l + 1, gathered[0])
            tok = gathering[l + 1]["token"]
        return _full_weights(dict(zip(BIG, gathered))), tok

    def big_done(l, big):
        arrs, kinds = [big[k] for k in BIG], [False] * len(BIG)
        if l + 1 < DEPTH:
            arrs, kinds = arrs + [packed[l + 1]], kinds + [True]
        sending[l] = _exchange_start(f"grads_start{l}", arrs, kinds, big["w_in"])
        return sending[l]["token"]

    def small_done(l, sm):
        packed[l] = _pack_layer(sm)

    loss, dx = _local_step(x[0], loss_target[0], wts, gathering[0]["token"], weights_of, big_done, small_done,
                           lambda part: lax.psum(part, ("x", "y", "c")))
    last = _exchange_start("grads_start_last", [packed[0]], [True], dx)
    recv_l = [_exchange_wait(f"grads_wait{l}", sending[l], [dx]) for l in range(DEPTH)]
    recv_last = _exchange_wait("grads_wait_last", last, [dx])
    recv = [jnp.stack([r[i] for r in recv_l], axis=1) for i in range(len(BIG))]
    recv.append(jnp.concatenate([recv_last[0]] + [recv_l[l][len(BIG)] for l in range(DEPTH - 1)], axis=1))

    grads, delta, new_m, new_v = {}, {}, {}, {}

    def update(k, parts, nparts):
        shape = wts[k].shape
        cols = shape[-1]
        rows = wts[k].size // cols
        br = min(rows, 1024 if cols <= 128 else 256)
        outs = _adamw("adamw_" + k, parts.reshape(nparts * rows, cols), nparts, wts[k].reshape(rows, cols),
                      mom[k].reshape(rows, cols), var[k].reshape(rows, cols), br)
        grads[k], delta[k], new_m[k], new_v[k] = (o.reshape(shape) for o in outs)

    for k, parts in zip(BIG, recv[:len(BIG)]):
        update(k, parts, NDEV)
    rows = DEPTH * PACK_BR
    (g_small,) = _ew("grads_small", lambda *p: (_sum_in_order(p),), rows, PACK_BR,
                     [_ri(recv[-1].reshape(NDEV * rows, PACK_COLS), PACK_COLS, 0, d * DEPTH) for d in range(NDEV)], [],
                     [(PACK_COLS, F32)])
    for k, g in _unpack(g_small, wts).items():
        update(k, g, 1)

    return (loss, dx[None], *[grads[k] for k in WEIGHTS], *[delta[k] for k in WEIGHTS],
            *[new_m[k] for k in WEIGHTS], *[new_v[k] for k in WEIGHTS])
```

```python
import functools
import math

import jax
import jax.numpy as jnp
from jax import lax
from jax.experimental import pallas as pl
from jax.experimental.pallas import tpu as pltpu

F32 = jnp.float32
BF16 = jnp.bfloat16

NDEV = 8
DEPTH = 4
SEQ = 2048
DM = 1024
NCOL = 8192
SW = 512
NGRP = 32
GCH = 16
NST = 64
NS = NGRP * NST
HD = 128
AW = 512
DILATIONS = (1, 4, 16)
ABLK = 128
ATTN_PAIR = 4
RMS_EPS = 1e-6
LR, B1, B2, ADAM_EPS, WD, STEP = 0.001, 0.9, 0.999, 1e-08, 0.01, 10

CB_U, CB_ZS, CB_Q, CB_K, CB_V, CB_ZA = 0, 1, 2, 5, 8, 11
CB_GS, CB_GA = 6, 7

VMEM_LIMIT = 56 * 2 ** 20


def _row_order(j):
    return jnp.where(j < CB_Q, 1, 0)


def _params(*sem):
    return pltpu.CompilerParams(dimension_semantics=sem, vmem_limit_bytes=VMEM_LIMIT)


def _ew(name, fn, rows, br, row_ins, bc_ins, row_outs, red_outs=()):
    n_in = len(row_ins) + len(bc_ins)
    n_ro = len(row_outs)
    steps = rows // br
    assert steps * br == rows

    def body(*refs):
        vals = fn(*[r[...] for r in refs[:n_in]])
        outs = refs[n_in:]
        for r, v in zip(outs[:n_ro], vals[:n_ro]):
            r[...] = v.astype(r.dtype)
        if red_outs:
            @pl.when(pl.program_id(0) == 0)
            def _():
                for r in outs[n_ro:]:
                    r[...] = jnp.zeros(r.shape, r.dtype)
            for r, v in zip(outs[n_ro:], vals[n_ro:]):
                r[...] += v

    in_specs = []
    for (_, w, cb, rb) in row_ins:
        in_specs.append(pl.BlockSpec((br, w), functools.partial(lambda i, cb, rb: (rb + i, cb), cb=cb, rb=rb)))
    for a in bc_ins:
        in_specs.append(pl.BlockSpec(a.shape, functools.partial(lambda i, nd: (0,) * nd, nd=a.ndim)))
    out_specs = [pl.BlockSpec((br, w), lambda i: (i, 0)) for (w, _) in row_outs]
    out_specs += [pl.BlockSpec((1, w), lambda i: (0, 0)) for w in red_outs]
    out_shape = [jax.ShapeDtypeStruct((rows, w), dt) for (w, dt) in row_outs]
    out_shape += [jax.ShapeDtypeStruct((1, w), F32) for w in red_outs]
    return pl.pallas_call(
        body, name=name, grid=(steps,), in_specs=in_specs, out_specs=out_specs, out_shape=out_shape,
        compiler_params=_params("arbitrary"),
    )(*[a for (a, _, _, _) in row_ins], *bc_ins)


def _ri(a, w=None, cb=0, rb=0):
    return (a, a.shape[1] if w is None else w, cb, rb)


_DIMS = {"nn": ((1,), (0,)), "nt": ((1,), (1,)), "tn": ((0,), (0,))}


def _after(after):
    return ([pl.BlockSpec(memory_space=pl.ANY)], [after]) if after is not None else ([], [])


def _mm(name, a, b, mode, M, N, K, bm, bn, bk, out_dtype, a_spec=None, b_spec=None, o_spec=None, out_shape=None,
        after=None):
    nk = K // bk
    assert M % bm == 0 and N % bn == 0 and nk * bk == K
    after_specs, after_args = _after(after)

    own_acc = nk > 1 and out_dtype != F32

    def body(a_ref, b_ref, *rest):
        o_ref, scratch = rest[len(after_args)], rest[len(after_args) + 1:]
        part = lax.dot_general(a_ref[...].astype(BF16), b_ref[...].astype(BF16), (_DIMS[mode], ((), ())),
                               preferred_element_type=F32)
        if nk == 1:
            o_ref[...] = part.astype(o_ref.dtype)
            return
        k = pl.program_id(2)
        acc_ref = scratch[0] if own_acc else o_ref

        @pl.when(k == 0)
        def _():
            acc_ref[...] = part

        @pl.when(k > 0)
        def _():
            acc_ref[...] += part

        if own_acc:
            @pl.when(k == nk - 1)
            def _():
                o_ref[...] = acc_ref[...].astype(o_ref.dtype)

    if a_spec is None:
        a_spec = (pl.BlockSpec((bk, bm), lambda i, j, k: (k, i)) if mode == "tn"
                  else pl.BlockSpec((bm, bk), lambda i, j, k: (i, k)))
    if b_spec is None:
        b_spec = (pl.BlockSpec((bn, bk), lambda i, j, k: (j, k)) if mode == "nt"
                  else pl.BlockSpec((bk, bn), lambda i, j, k: (k, j)))
    if o_spec is None:
        o_spec = pl.BlockSpec((bm, bn), lambda i, j, k: (i, j))
    if out_shape is None:
        out_shape = (M, N)
    return pl.pallas_call(
        body, name=name, grid=(M // bm, N // bn, nk), in_specs=[a_spec, b_spec] + after_specs, out_specs=o_spec,
        out_shape=jax.ShapeDtypeStruct(out_shape, out_dtype),
        scratch_shapes=[pltpu.VMEM((bm, bn), F32)] if own_acc else [],
        compiler_params=_params("parallel", "parallel", "arbitrary"),
    )(a, b, *after_args)


SCAN_LANES = 512
SCAN_CHUNKS = 8


def _to_chunked(a):
    return a.reshape(SCAN_CHUNKS, SEQ // SCAN_CHUNKS, -1).transpose(1, 0, 2).reshape(SEQ, -1)


def _from_chunked(a):
    return a.reshape(SEQ // SCAN_CHUNKS, SCAN_CHUNKS, -1).transpose(1, 0, 2).reshape(SEQ, -1)


def _scan_block(dr_ref, di_ref, sr_ref, si_ref, lam_r, lam_i, reverse):
    T = SEQ // SCAN_CHUNKS
    bl = lam_r.shape[1]
    assert T == 2 ** 8
    ar = jnp.broadcast_to(lam_r, (SCAN_CHUNKS, bl))
    ai = jnp.broadcast_to(lam_i, (SCAN_CHUNKS, bl))
    zero = jnp.zeros((SCAN_CHUNKS, bl), F32)

    def tile(j):
        return pl.ds(pl.multiple_of(j * SCAN_CHUNKS, SCAN_CHUNKS), SCAN_CHUNKS)

    def step(jj, carry):
        sr, si = carry
        j = T - 1 - jj if reverse else jj
        nr = ar * sr - ai * si + dr_ref[tile(j), :]
        ni = ar * si + ai * sr + di_ref[tile(j), :]
        sr_ref[tile(j), :] = nr
        si_ref[tile(j), :] = ni
        return nr, ni

    er, ei = lax.fori_loop(0, T, step, (zero, zero), unroll=4)

    pr, pi = ar[0:1], ai[0:1]
    for _ in range(8):
        pr, pi = pr * pr - pi * pi, 2.0 * pr * pi
    rows = lax.broadcasted_iota(jnp.int32, (SCAN_CHUNKS, bl), 0)
    cr, ci = zero, zero
    xr = jnp.zeros((1, bl), F32)
    xi = jnp.zeros((1, bl), F32)
    order = range(SCAN_CHUNKS - 2, -1, -1) if reverse else range(1, SCAN_CHUNKS)
    for c in order:
        src = c + 1 if reverse else c - 1
        nxr = pr * xr - pi * xi + er[src:src + 1]
        nxi = pr * xi + pi * xr + ei[src:src + 1]
        xr, xi = nxr, nxi
        cr = jnp.where(rows == c, xr, cr)
        ci = jnp.where(rows == c, xi, ci)

    def fix(jj, pw):
        pwr, pwi = pw
        j = T - 1 - jj if reverse else jj
        sr_ref[tile(j), :] = sr_ref[tile(j), :] + (pwr * cr - pwi * ci)
        si_ref[tile(j), :] = si_ref[tile(j), :] + (pwr * ci + pwi * cr)
        return pwr * ar - pwi * ai, pwr * ai + pwi * ar

    lax.fori_loop(0, T, fix, (ar, ai), unroll=4)


def _s5_forward(name, proj, wdt, cmt, lam_r, lam_i):
    bl = SCAN_LANES
    nblk = NS // bl
    cw = bl * GCH // NST
    assert cw == 128 and CB_U == 0

    def nt(a, b_):
        return lax.dot_general(a, b_, (((1,), (1,)), ((), ())), preferred_element_type=F32)

    def body(u_ref, wr_ref, wi_ref, cr_ref, ci_ref, ar_ref, ai_ref, s_ref, y_ref):
        sr_ref, si_ref = s_ref.at[0], s_ref.at[1]
        u = u_ref[...].astype(BF16)
        sr_ref[...] = nt(u, wr_ref[...])
        si_ref[...] = nt(u, wi_ref[...])
        _scan_block(sr_ref, si_ref, sr_ref, si_ref, ar_ref[...], ai_ref[...], False)
        y_ref[...] = nt(sr_ref[...].astype(BF16), cr_ref[...]) + nt(si_ref[...].astype(BF16), ci_ref[...])

    return pl.pallas_call(
        body, name=name, grid=(nblk,),
        in_specs=[pl.BlockSpec((SEQ, cw), lambda i: (0, i)),
                  pl.BlockSpec((bl, cw), lambda i: (i, i)), pl.BlockSpec((bl, cw), lambda i: (nblk + i, i)),
                  pl.BlockSpec((cw, bl), lambda i: (i, i)), pl.BlockSpec((cw, bl), lambda i: (SW // cw + i, i)),
                  pl.BlockSpec((1, bl), lambda i: (0, i)), pl.BlockSpec((1, bl), lambda i: (0, i))],
        out_specs=[pl.BlockSpec((2, SEQ, bl), lambda i: (0, 0, i)), pl.BlockSpec((SEQ, cw), lambda i: (0, i))],
        out_shape=[jax.ShapeDtypeStruct((2, SEQ, NS), F32), jax.ShapeDtypeStruct((SEQ, SW), F32)],
        compiler_params=_params("arbitrary"),
    )(proj, wdt, wdt, cmt, cmt, lam_r, lam_i)


S5_BWD_LANES = 512


def _s5_backward(name, dy0, proj, s, wdt, cmt, lam_r, lam_i):
    bl = S5_BWD_LANES
    nblk = NS // bl
    cw = 128
    per = cw // (bl * GCH // NST)
    assert per >= 1 and CB_U == 0

    def tn(a, b_):
        return lax.dot_general(a, b_, (((0,), (0,)), ((), ())), preferred_element_type=F32)

    def prev(s_ref):
        last = pltpu.roll(s_ref[SEQ - SCAN_CHUNKS:SEQ, :], 1, 0)
        first = jnp.where(lax.broadcasted_iota(jnp.int32, (SCAN_CHUNKS, bl), 0) > 0, last, 0.0)
        return jnp.concatenate([first, s_ref[0:SEQ - SCAN_CHUNKS, :]], axis=0)

    def body(dy_ref, u_ref, s_ref, cr_ref, ci_ref, wr_ref, wi_ref, lr_ref, li_ref,
             dlr_ref, dli_ref, dwdt_ref, dcmt_ref, du_ref, ar_ref, ai_ref):
        dy = dy_ref[...]
        ar_ref[...] = jnp.dot(dy, cr_ref[...], preferred_element_type=F32)
        ai_ref[...] = jnp.dot(dy, ci_ref[...], preferred_element_type=F32)
        _scan_block(ar_ref, ai_ref, ar_ref, ai_ref, lr_ref[...], -li_ref[...], True)
        a_r, a_i = ar_ref[...], ai_ref[...]
        sr_ref, si_ref = s_ref.at[0], s_ref.at[1]
        spr, spi = prev(sr_ref), prev(si_ref)
        dlr_ref[...] = jnp.sum(a_r * spr + a_i * spi, axis=0, keepdims=True)
        dli_ref[...] = jnp.sum(a_i * spr - a_r * spi, axis=0, keepdims=True)
        a_rb, a_ib = a_r.astype(BF16), a_i.astype(BF16)
        u = u_ref[...].astype(BF16)
        dwdt_ref[0] = tn(a_rb, u)
        dwdt_ref[1] = tn(a_ib, u)
        dcmt_ref[0] = tn(dy, sr_ref[...].astype(BF16))
        dcmt_ref[1] = tn(dy, si_ref[...].astype(BF16))
        part = (jnp.dot(a_rb, wr_ref[...], preferred_element_type=F32)
                + jnp.dot(a_ib, wi_ref[...], preferred_element_type=F32))

        @pl.when(pl.program_id(0) % per == 0)
        def _():
            du_ref[...] = part

        @pl.when(pl.program_id(0) % per > 0)
        def _():
            du_ref[...] += part

    lam_spec = pl.BlockSpec((1, bl), lambda i: (0, i))
    return pl.pallas_call(
        body, name=name, grid=(nblk,),
        in_specs=[pl.BlockSpec((SEQ, cw), lambda i: (0, i // per)), pl.BlockSpec((SEQ, cw), lambda i: (0, i // per)),
                  pl.BlockSpec((2, SEQ, bl), lambda i: (0, 0, i)),
                  pl.BlockSpec((cw, bl), lambda i: (i // per, i)),
                  pl.BlockSpec((cw, bl), lambda i: (SW // cw + i // per, i)),
                  pl.BlockSpec((bl, cw), lambda i: (i, i // per)), pl.BlockSpec((bl, cw), lambda i: (nblk + i, i // per)),
                  lam_spec, lam_spec],
        out_specs=[lam_spec, lam_spec, pl.BlockSpec((2, bl, cw), lambda i: (0, i, i // per)),
                   pl.BlockSpec((2, cw, bl), lambda i: (0, i // per, i)),
                   pl.BlockSpec((SEQ, cw), lambda i: (0, i // per))],
        out_shape=[jax.ShapeDtypeStruct((1, NS), F32), jax.ShapeDtypeStruct((1, NS), F32),
                   jax.ShapeDtypeStruct((2, NS, SW), F32), jax.ShapeDtypeStruct((2, SW, NS), F32),
                   jax.ShapeDtypeStruct((SEQ, SW), F32)],
        scratch_shapes=[pltpu.VMEM((SEQ, bl), F32)] * 2,
        compiler_params=_params("arbitrary"),
    )(dy0, proj, s, cmt, cmt, wdt, wdt, lam_r, lam_i)


def _scores(qb, kb, prev):
    s = lax.dot_general(qb, kb, (((1,), (1,)), ((), ())), preferred_element_type=F32) * (HD ** -0.5)
    row = lax.broadcasted_iota(jnp.int32, (ABLK, ABLK), 0)
    col = lax.broadcasted_iota(jnp.int32, (ABLK, ABLK), 1)
    return jnp.where((col >= row) if prev else (col <= row), s, -1e30)


def _block_rows(dil, r, b):
    if dil == 1:
        return pl.ds(pl.multiple_of(b * ABLK, ABLK), ABLK)
    return pl.ds(r + dil * ABLK * b, ABLK, stride=dil)


def _group_blocks(dil):
    nb = SEQ // dil // ABLK
    shift = nb.bit_length() - 1
    return nb, (lambda idx: (idx >> shift, idx & (nb - 1)))


def _qkv_specs(j_of):
    return [pl.BlockSpec((SEQ, HD), functools.partial(lambda j, c: (0, c + j_of(j)), c=(cb + g) * 4))
            for g in range(3) for cb in (CB_Q, CB_K, CB_V)]


def _attention_fwd(name, proj):
    def body(*refs):
        qkv, z_ref = refs[:9], refs[9]
        y_ref, ya_ref, l_ref = refs[10:13]
        accs, maxs, dens = refs[13:16], refs[16:19], refs[19:22]
        for g, dil in enumerate(DILATIONS):
            q_ref, k_ref, v_ref = qkv[3 * g:3 * g + 3]
            nb, where = _group_blocks(dil)

            def step(t, c, g=g, dil=dil, nb=nb, where=where, q_ref=q_ref, k_ref=k_ref, v_ref=v_ref):
                two = range(ATTN_PAIR)
                rb = [where(t + i * (SEQ // ABLK // ATTN_PAIR)) for i in two]
                rows = [_block_rows(dil, r, b) for r, b in rb]
                qb = [q_ref[rows[i], :].astype(BF16) for i in two]
                s_c = [_scores(qb[i], k_ref[rows[i], :].astype(BF16), False) for i in two]
                if nb > 1:
                    prev = [_block_rows(dil, r, jnp.maximum(b - 1, 0)) for r, b in rb]
                    s_p = [jnp.where(rb[i][1] > 0, _scores(qb[i], k_ref[prev[i], :].astype(BF16), True), -1e30)
                           for i in two]
                m = [jnp.max(s_c[i], axis=-1, keepdims=True) for i in two]
                if nb > 1:
                    m = [jnp.maximum(m[i], jnp.max(s_p[i], axis=-1, keepdims=True)) for i in two]
                p_c = [jnp.exp(s_c[i] - m[i]) for i in two]
                den = [jnp.sum(p_c[i], axis=-1, keepdims=True) for i in two]
                acc = [jnp.dot(p_c[i].astype(BF16), v_ref[rows[i], :].astype(BF16), preferred_element_type=F32)
                       for i in two]
                if nb > 1:
                    p_p = [jnp.exp(s_p[i] - m[i]) for i in two]
                    den = [den[i] + jnp.sum(p_p[i], axis=-1, keepdims=True) for i in two]
                    acc = [acc[i] + jnp.dot(p_p[i].astype(BF16), v_ref[prev[i], :].astype(BF16),
                                            preferred_element_type=F32) for i in two]
                for i in two:
                    accs[g][rows[i], :] = acc[i]
                    maxs[g][rows[i], :] = jnp.broadcast_to(m[i], (ABLK, HD))
                    dens[g][rows[i], :] = jnp.broadcast_to(den[i], (ABLK, HD))
                return c

            lax.fori_loop(0, SEQ // ABLK // ATTN_PAIR, step, 0)
        top = jnp.maximum(jnp.maximum(maxs[0][...], maxs[1][...]), maxs[2][...])
        den = jnp.zeros((SEQ, HD), F32)
        y = jnp.zeros((SEQ, HD), F32)
        for g in range(3):
            wgt = jnp.exp(maxs[g][...] - top)
            den = den + wgt * dens[g][...]
            y = y + wgt * accs[g][...]
        y = y / den
        y_ref[...] = y
        ya_ref[...] = (y * _silu(z_ref[...])).astype(ya_ref.dtype)
        l_ref[...] = top + jnp.log(den)

    ospec = pl.BlockSpec((SEQ, HD), lambda j: (0, j))
    return pl.pallas_call(
        body, name=name, grid=(AW // HD,),
        in_specs=_qkv_specs(lambda j: j) + [pl.BlockSpec((SEQ, HD), lambda j: (0, CB_ZA * 4 + j))],
        out_specs=[ospec, ospec, ospec],
        out_shape=[jax.ShapeDtypeStruct((SEQ, AW), F32), jax.ShapeDtypeStruct((SEQ, AW), BF16),
                   jax.ShapeDtypeStruct((SEQ, AW), F32)],
        scratch_shapes=[pltpu.VMEM((SEQ, HD), F32)] * 9,
        compiler_params=_params("parallel"),
    )(*([proj] * 10))


def _attention_bwd(name, proj, dya, y, lse):
    def tn(a, b_):
        return lax.dot_general(a, b_, (((0,), (0,)), ((), ())), preferred_element_type=F32)

    def nt(a, b_):
        return lax.dot_general(a, b_, (((1,), (1,)), ((), ())), preferred_element_type=F32)

    def body(*refs):
        qkv, z_ref, dya_ref, y_ref, l_ref = refs[:9], refs[9], refs[10], refs[11], refs[12]
        outs, dza_ref = refs[13:22], refs[22]
        dy_s, dsum_s, dq_s, dk_own, dv_own, dk_prev, dv_prev = refs[23:]
        _, vjp = jax.vjp(lambda y_, z_: y_ * _silu(z_), y_ref[...], z_ref[...])
        dy, dz = vjp(dya_ref[...])
        dza_ref[...] = dz.astype(dza_ref.dtype)
        dy_s[...] = dy
        dsum_s[...] = jnp.broadcast_to(jnp.sum(dy * y_ref[...], axis=-1, keepdims=True), (SEQ, HD))
        for g, dil in enumerate(DILATIONS):
            q_ref, k_ref, v_ref = qkv[3 * g:3 * g + 3]
            nb, where = _group_blocks(dil)
            if nb > 1:
                dk_prev[...] = jnp.zeros(dk_prev.shape, F32)
                dv_prev[...] = jnp.zeros(dv_prev.shape, F32)

            def step(t, c, dil=dil, nb=nb, where=where, q_ref=q_ref, k_ref=k_ref, v_ref=v_ref):
                rb = [where(t + i * (SEQ // ABLK // ATTN_PAIR)) for i in range(ATTN_PAIR)]
                sides = []
                for r, b in rb:
                    rows = _block_rows(dil, r, b)
                    own = dict(b=b, qrows=rows, krows=rows, prev=False, dk=dk_own, dv=dv_own,
                               q=q_ref[rows, :].astype(BF16), dy=dy_s[rows, :].astype(BF16))
                    sides.append(own)
                    if nb > 1:
                        sides.append(dict(own, krows=_block_rows(dil, r, jnp.maximum(b - 1, 0)), prev=True,
                                          dk=dk_prev, dv=dv_prev))
                for s_ in sides:
                    s_["k"] = k_ref[s_["krows"], :].astype(BF16)
                    s_["v"] = v_ref[s_["krows"], :].astype(BF16)
                for s_ in sides:
                    sc = _scores(s_["q"], s_["k"], s_["prev"])
                    s_["s"] = jnp.where(s_["b"] > 0, sc, -1e30) if s_["prev"] else sc
                    s_["dp"] = nt(s_["dy"], s_["v"])
                for s_ in sides:
                    p = jnp.exp(s_["s"] - l_ref[s_["qrows"], :])
                    s_["p"] = p.astype(BF16)
                    s_["ds"] = (p * (s_["dp"] - dsum_s[s_["qrows"], :]) * (HD ** -0.5)).astype(BF16)
                for s_ in sides:
                    s_["dk"][s_["krows"], :] = tn(s_["ds"], s_["q"])
                    s_["dv"][s_["krows"], :] = tn(s_["p"], s_["dy"])
                    s_["dq"] = jnp.dot(s_["ds"], s_["k"], preferred_element_type=F32)
                per = len(sides) // ATTN_PAIR
                for i in range(ATTN_PAIR):
                    dq = sides[i * per]["dq"]
                    if per > 1:
                        dq = dq + sides[i * per + 1]["dq"]
                    dq_s[sides[i * per]["qrows"], :] = dq
                return c

            lax.fori_loop(0, SEQ // ABLK // ATTN_PAIR, step, 0)
            dq_ref, dk_ref, dv_ref = outs[3 * g:3 * g + 3]
            dq_ref[...] = dq_s[...].astype(dq_ref.dtype)
            if nb > 1:
                dk_ref[...] = (dk_own[...] + dk_prev[...]).astype(dk_ref.dtype)
                dv_ref[...] = (dv_own[...] + dv_prev[...]).astype(dv_ref.dtype)
            else:
                dk_ref[...] = dk_own[...].astype(dk_ref.dtype)
                dv_ref[...] = dv_own[...].astype(dv_ref.dtype)

    ospec = pl.BlockSpec((SEQ, HD), lambda j: (0, j))
    outs = pl.pallas_call(
        body, name=name, grid=(AW // HD,),
        in_specs=_qkv_specs(lambda j: j) + [pl.BlockSpec((SEQ, HD), lambda j: (0, CB_ZA * 4 + j))] + [ospec] * 3,
        out_specs=[ospec] * 10, out_shape=[jax.ShapeDtypeStruct((SEQ, AW), BF16)] * 10,
        scratch_shapes=[pltpu.VMEM((SEQ, HD), F32)] * 7,
        compiler_params=_params("parallel"),
    )(*([proj] * 10), dya, y, lse)
    return outs[:9], outs[9]


def _rms(x, g):
    return x * lax.rsqrt(jnp.mean(x * x, axis=-1, keepdims=True) + RMS_EPS) * g


def _sig(x):
    return 1.0 / (1.0 + jnp.exp(-x))


def _silu(x):
    return x * _sig(x)


def _gelu(x):
    return 0.5 * x * (1.0 + jnp.tanh(math.sqrt(2.0 / math.pi) * (x + 0.044715 * (x * x * x))))


def _y1_fn(y0p, u, dskip):
    return _gelu(y0p + dskip * u)


def _ys_fn(y1, t, z, bglu):
    return y1 * _sig(t + bglu) * _silu(z)


def _merge_fn(ms, ma, gs, ga):
    return _sig(gs) * ms + _sig(ga) * ma


def _colsum(v):
    return jnp.sum(v, axis=0, keepdims=True)


def _lam_fn(lre, lim, ldt):
    a = jnp.minimum(lre, -1e-4)
    dt = jnp.exp(ldt)
    mag = jnp.exp(a * dt)
    ar = mag * jnp.cos(lim * dt)
    ai = mag * jnp.sin(lim * dt)
    den = a * a + lim * lim
    cr = ((ar - 1.0) * a + ai * lim) / den
    ci = (ai * a - (ar - 1.0) * lim) / den
    return ar, ai, cr, ci


def _bbar_fn(cr, ci, bre, bim):
    return cr * bre - ci * bim, cr * bim + ci * bre


def _same_group(rows, a, cols, b):
    r = lax.broadcasted_iota(jnp.int32, (rows, cols), 0) >> (a.bit_length() - 1)
    c = lax.broadcasted_iota(jnp.int32, (rows, cols), 1) >> (b.bit_length() - 1)
    return r == c


def _expand(name, blocks, signs, a, b, dtype, after=None):
    rows, cols = NGRP * a, NGRP * b
    n = len(blocks)
    assert a & (a - 1) == 0 and b & (b - 1) == 0
    after_specs, after_args = _after(after)

    def body(*refs):
        o_ref = refs[-1]
        tile = (lax.broadcasted_iota(jnp.int32, (b, cols), 1) & (b - 1)
                == lax.broadcasted_iota(jnp.int32, (b, cols), 0)).astype(F32)
        keep = _same_group(rows, a, cols, b)
        for i, (ref, sign) in enumerate(zip(refs[:n], signs)):
            spread = jnp.dot(ref[...], tile, preferred_element_type=F32, precision=lax.Precision.HIGHEST)
            o_ref[i * rows:(i + 1) * rows, :] = jnp.where(keep, sign * spread, 0.0).astype(o_ref.dtype)

    return pl.pallas_call(body, name=name, grid=(1,),
                          in_specs=[pl.BlockSpec((rows, b), lambda i: (0, 0))] * n + after_specs,
                          out_specs=pl.BlockSpec((n * rows, cols), lambda i: (0, 0)),
                          out_shape=jax.ShapeDtypeStruct((n * rows, cols), dtype),
                          compiler_params=_params("arbitrary"))(*blocks, *after_args)


def _extract(name, m, a, b, at=(0, 0), after=None):
    rows, cols = NGRP * a, NGRP * b
    assert a & (a - 1) == 0 and b & (b - 1) == 0
    after_specs, after_args = _after(after)

    def body(m_ref, *rest):
        o_ref = rest[-1]
        tile = (lax.broadcasted_iota(jnp.int32, (cols, b), 0) & (b - 1)
                == lax.broadcasted_iota(jnp.int32, (cols, b), 1)).astype(F32)
        kept = jnp.where(_same_group(rows, a, cols, b), m_ref[...], 0.0)
        o_ref[...] = jnp.dot(kept, tile, preferred_element_type=F32, precision=lax.Precision.HIGHEST)

    return pl.pallas_call(body, name=name, grid=(1,),
                          in_specs=[pl.BlockSpec((rows, cols), lambda i: at)] + after_specs,
                          out_specs=pl.BlockSpec((rows, b), lambda i: (0, 0)),
                          out_shape=jax.ShapeDtypeStruct((rows, b), F32),
                          compiler_params=_params("arbitrary"))(m, *after_args)


def _s5_prepare(l, sp, after):
    tag = f"l{l}_"
    ar, ai, cr, ci = _ew(tag + "lam", _lam_fn, NGRP, NGRP,
                         [_ri(sp["lambda_re"]), _ri(sp["lambda_im"]),
                          _ri(sp["log_dt"].reshape(NGRP, 1) + after[0, 0])], [], [(NST, F32)] * 4)
    bre = sp["b_re"].reshape(NS, GCH)
    bim = sp["b_im"].reshape(NS, GCH)
    bbr, bbi = _ew(tag + "bbar", _bbar_fn, NS, NS, [_ri(cr.reshape(NS, 1)), _ri(ci.reshape(NS, 1)), _ri(bre), _ri(bim)],
                   [], [(GCH, F32)] * 2)
    wdt = _expand(tag + "wdt", [bbr, bbi], [1.0, 1.0], NST, GCH, BF16)
    cmt = _expand(tag + "cmt", [sp["c_re"].reshape(SW, NST), sp["c_im"].reshape(SW, NST)], [1.0, -1.0], GCH, NST, BF16,
                  after=after)
    return dict(ar=ar, ai=ai, cr=cr, ci=ci, wdt=wdt, cmt=cmt)


def _layer_fwd(l, x, w, sp, prep):
    tag = f"l{l}_"
    g1 = sp["pre_norm_g"].reshape(1, DM)
    (h,) = _ew(tag + "rms1", lambda x_, g: (_rms(x_, g),), SEQ, 256, [_ri(x)], [g1], [(DM, BF16)])
    hv = jnp.stack([h, _to_chunked(h)])
    win = w["w_in"]
    proj = _mm(tag + "proj", hv, win, "nn", SEQ, NCOL, DM, SEQ, 512, 1024, F32,
               a_spec=pl.BlockSpec((None, SEQ, 1024), lambda i, j, k: (_row_order(j), 0, 0)),
               b_spec=pl.BlockSpec((None, 1024, 512), lambda i, j, k: (j // 2, 0, j % 2)))

    ar, ai, cr, ci, wdt, cmt = (prep[k] for k in ("ar", "ai", "cr", "ci", "wdt", "cmt"))
    s, y0p = _s5_forward(tag + "s5", proj, wdt, cmt, ar.reshape(1, NS), ai.reshape(1, NS))
    dskip = sp["d_skip"].reshape(1, SW)
    (y1,) = _ew(tag + "y1", lambda a, u, d: (_y1_fn(a, u, d),), SEQ, 256,
                [_ri(y0p), _ri(proj, SW, CB_U)], [dskip], [(SW, F32)])
    t = _mm(tag + "glu", y1, w["w_glu"], "nn", SEQ, SW, SW, 1024, 512, 512, F32)
    bglu = sp["b_glu"].reshape(1, SW)
    (ys_c,) = _ew(tag + "ys", lambda y1_, t_, z, b_: (_ys_fn(y1_, t_, z, b_),), SEQ, 256,
                  [_ri(y1), _ri(t), _ri(proj, SW, CB_ZS)], [bglu], [(SW, BF16)])
    ys = _from_chunked(ys_c)

    ypre, ya, lse = _attention_fwd(tag + "attn", proj)

    ms = _mm(tag + "branch_s", ys, w["w_branch_s"], "nn", SEQ, DM, SW, 1024, 1024, 512, F32)
    ma = _mm(tag + "branch_a", ya, w["w_branch_a"], "nn", SEQ, DM, AW, 1024, 1024, 512, F32)
    (merged,) = _ew(tag + "merge", lambda a, b_, c, d: (_merge_fn(a, b_, c, d),), SEQ, 256,
                    [_ri(ms), _ri(ma), _ri(proj, DM, CB_GS), _ri(proj, DM, CB_GA)], [], [(DM, BF16)])
    out = _mm(tag + "out", merged, w["w_out"], "nn", SEQ, DM, DM, 1024, 1024, 1024, F32)
    g2 = sp["post_norm_g"].reshape(1, DM)
    (x_new,) = _ew(tag + "post", lambda x_, o, g: (x_ + _rms(o, g),), SEQ, 256, [_ri(x), _ri(out)], [g2], [(DM, F32)])
    res = dict(x=x, hv=hv, proj=proj, ar=ar, ai=ai, cr=cr, ci=ci, wdt=wdt, cmt=cmt, s=s, y0p=y0p,
               y1=y1, t=t, ys=ys, ya=ya, ypre=ypre, lse=lse, ms=ms, ma=ma, merged=merged, out=out)
    return x_new, res


def _layer_bwd(l, dxn, r, w, sp, big_done, after=None):
    tag = f"l{l}b_"
    proj = r["proj"]
    g1 = sp["pre_norm_g"].reshape(1, DM)
    g2 = sp["post_norm_g"].reshape(1, DM)
    dskip = sp["d_skip"].reshape(1, SW)
    bglu = sp["b_glu"].reshape(1, SW)

    def post_b(d, o, g):
        _, vjp = jax.vjp(_rms, o, g)
        do, dg = vjp(d)
        return do, dg

    d_out, dg2 = _ew(tag + "post", post_b, SEQ, 256, [_ri(dxn), _ri(r["out"])], [g2], [(DM, BF16)], [DM])
    dw_out = _mm(tag + "dw_out", r["merged"], d_out, "tn", DM, DM, SEQ, 1024, 1024, SEQ, BF16, after=after)
    dmerged = _mm(tag + "dmerged", d_out, w["w_out"], "nt", SEQ, DM, DM, 1024, 1024, 1024, F32, after=after)

    def merge_b(d, ms, ma, gs, ga):
        _, vjp = jax.vjp(_merge_fn, ms, ma, gs, ga)
        return vjp(d)

    dms, dma, dgs, dga = _ew(tag + "merge", merge_b, SEQ, 256,
                             [_ri(dmerged), _ri(r["ms"]), _ri(r["ma"]), _ri(proj, DM, CB_GS), _ri(proj, DM, CB_GA)],
                             [], [(DM, BF16)] * 4)
    dw_bs = _mm(tag + "dw_bs", r["ys"], dms, "tn", SW, DM, SEQ, 512, 1024, SEQ, BF16)
    dw_ba = _mm(tag + "dw_ba", r["ya"], dma, "tn", AW, DM, SEQ, 512, 1024, SEQ, BF16)
    dys = _mm(tag + "dys", dms, w["w_branch_s"], "nt", SEQ, SW, DM, 1024, 512, 1024, F32)
    dya = _mm(tag + "dya", dma, w["w_branch_a"], "nt", SEQ, AW, DM, 1024, 512, 1024, F32)

    dqkv, dza = _attention_bwd(tag + "attn", proj, dya, r["ypre"], r["lse"])

    def ys_b(d, y1, t, z, b_):
        _, vjp = jax.vjp(_ys_fn, y1, t, z, b_)
        dy1, dt, dz, _ = vjp(d)
        return dy1, dt, dz, _colsum(dt)

    dy1a, dt, dzs, dbglu = _ew(tag + "ys", ys_b, SEQ, 256,
                               [_ri(_to_chunked(dys)), _ri(r["y1"]), _ri(r["t"]), _ri(proj, SW, CB_ZS)],
                               [bglu], [(SW, F32), (SW, BF16), (SW, BF16)], [SW])
    dw_glu = _mm(tag + "dw_glu", r["y1"], dt, "tn", SW, SW, SEQ, 512, 512, SEQ, BF16)
    dy1b = _mm(tag + "dy1b", dt, w["w_glu"], "nt", SEQ, SW, SW, 1024, 512, 512, F32)

    def y1_b(da, db, y0p, u, d_):
        _, vjp = jax.vjp(_y1_fn, y0p, u, d_)
        dy0, du, dd = vjp(da + db)
        return dy0, du, dd

    dy0, du_skip, ddskip = _ew(tag + "y1", y1_b, SEQ, 256,
                               [_ri(dy1a), _ri(dy1b), _ri(r["y0p"]), _ri(proj, SW, CB_U)], [dskip],
                               [(SW, BF16), (SW, F32)], [SW])
    dlr, dli, dwdt, dcmt, du_s = _s5_backward(tag + "s5", dy0, proj, r["s"], r["wdt"], r["cmt"],
                                              r["ar"].reshape(1, NS), r["ai"].reshape(1, NS))
    (du,) = _ew(tag + "du", lambda a, c: (a + c,), SEQ, 256, [_ri(du_s), _ri(du_skip)], [], [(SW, BF16)])

    dq, dk, dv = ([dqkv[3 * g + i] for g in range(3)] for i in range(3))
    dproj = jnp.concatenate([du, dzs, *dq, *dk, *dv, dza, dgs, dga], axis=1)
    dw_in = _mm(tag + "dw_in", r["hv"], dproj, "tn", DM, NCOL, SEQ, 1024, 512, SEQ, BF16,
                a_spec=pl.BlockSpec((None, SEQ, 1024), lambda i, j, k: (_row_order(j), 0, 0)),
                o_spec=pl.BlockSpec((None, 1024, 512), lambda i, j, k: (j // 2, 0, j % 2)), out_shape=(NDEV, DM, DM))
    tok = big_done(l, dict(w_in=dw_in, w_glu=dw_glu.reshape(NDEV, SW // NDEV, SW),
                           w_branch_s=dw_bs.reshape(SW, NDEV, DM // NDEV).transpose(1, 0, 2),
                           w_branch_a=dw_ba.reshape(AW, NDEV, DM // NDEV).transpose(1, 0, 2),
                           w_out=dw_out.reshape(NDEV, DM // NDEV, DM)))
    if tok is not None:
        g1 = g1 + tok[0, 0]

    dwdt = dwdt.reshape(2 * NS, SW)
    dcmt = dcmt.reshape(2 * SW, NS)
    dbbr = _extract(tag + "dbbr", dwdt, NST, GCH, (0, 0), after=tok)
    dbbi = _extract(tag + "dbbi", dwdt, NST, GCH, (1, 0), after=tok)
    bre = sp["b_re"].reshape(NS, GCH)
    bim = sp["b_im"].reshape(NS, GCH)

    def bbar_b(cr, ci, br_, bi_, dr, di):
        _, vjp = jax.vjp(_bbar_fn, cr, ci, br_, bi_)
        return vjp((dr, di))

    dcr, dci, dbre, dbim = _ew(tag + "bbar", bbar_b, NS, NS,
                               [_ri(r["cr"].reshape(NS, 1)), _ri(r["ci"].reshape(NS, 1)), _ri(bre), _ri(bim),
                                _ri(dbbr), _ri(dbbi)], [], [(1, F32), (1, F32), (GCH, F32), (GCH, F32)])

    def lam_b(lre, lim, ldt, dar, dai, dcr_, dci_):
        _, vjp = jax.vjp(_lam_fn, lre, lim, ldt)
        return vjp((dar, dai, dcr_, dci_))

    dlre, dlim, dldt = _ew(tag + "lam", lam_b, NGRP, NGRP,
                           [_ri(sp["lambda_re"]), _ri(sp["lambda_im"]), _ri(sp["log_dt"].reshape(NGRP, 1)),
                            _ri(dlr.reshape(NGRP, NST)), _ri(dli.reshape(NGRP, NST)),
                            _ri(dcr.reshape(NGRP, NST)), _ri(dci.reshape(NGRP, NST))], [],
                           [(NST, F32), (NST, F32), (1, F32)])
    dc_re = _extract(tag + "dc_re", dcmt, GCH, NST, (0, 0), after=tok).reshape(NGRP, GCH, NST)
    dc_im = -_extract(tag + "dc_im", dcmt, GCH, NST, (1, 0), after=tok).reshape(NGRP, GCH, NST)

    dh_time = _mm(tag + "dh_time", dproj, w["w_in"], "nt", SEQ, DM, NCOL - DM, SEQ, 1024, 1024, F32,
                  a_spec=pl.BlockSpec((SEQ, 1024), lambda i, j, k: (0, 1 + k)),
                  b_spec=pl.BlockSpec((None, 1024, 1024), lambda i, j, k: (1 + k, 0, 0)), after=tok)
    dh_chunked = _mm(tag + "dh_chunked", dproj, w["w_in"], "nt", SEQ, DM, DM, SEQ, 1024, 1024, F32,
                     a_spec=pl.BlockSpec((SEQ, 1024), lambda i, j, k: (0, 0)),
                     b_spec=pl.BlockSpec((None, 1024, 1024), lambda i, j, k: (0, 0, 0)), after=tok)
    dh = [dh_time, _from_chunked(dh_chunked)]

    def pre_b(d, dh0, dh1, x_, g):
        _, vjp = jax.vjp(_rms, x_, g)
        dx_, dg = vjp(dh0 + dh1)
        return d + dx_, dg

    dx, dg1 = _ew(tag + "pre", pre_b, SEQ, 256, [_ri(dxn)] + [_ri(t_) for t_ in dh] + [_ri(r["x"])], [g1],
                  [(DM, F32)], [DM])

    small = dict(pre_norm_g=dg1.reshape(DM), lambda_re=dlre, lambda_im=dlim, log_dt=dldt.reshape(NGRP),
                 b_re=dbre.reshape(NGRP, NST, GCH), b_im=dbim.reshape(NGRP, NST, GCH), c_re=dc_re, c_im=dc_im,
                 d_skip=ddskip.reshape(SW), b_glu=dbglu.reshape(SW), post_norm_g=dg2.reshape(DM))
    return dx, small


_HBM = pl.BlockSpec(memory_space=pltpu.HBM)
_SEM = pl.BlockSpec(memory_space=pltpu.SEMAPHORE)
_EFFECT = pltpu.SideEffectType.DATAFLOW_SIDE_EFFECTING


def _remote_copies(srcs, dsts, send_sems, recv_sems, gather):
    x, y, c = lax.axis_index("x"), lax.axis_index("y"), lax.axis_index("c")
    me = 4 * x + 2 * y + c
    copies = []
    for i in range(len(srcs)):
        for k in range(1, NDEV):
            peer = (x ^ (k >> 2), y ^ ((k >> 1) & 1), c ^ (k & 1))
            src = srcs[i] if gather[i] else srcs[i].at[me ^ k]
            copies.append(pltpu.make_async_remote_copy(
                src_ref=src, dst_ref=dsts[i].at[me], send_sem=send_sems[i], recv_sem=recv_sems[i],
                device_id=peer, device_id_type=pl.DeviceIdType.MESH))
    return copies


def _all_seven(dst, send_sem, recv_sem):
    seven = dst.at[pl.ds(0, NDEV - 1)]
    me = (lax.axis_index("x"), lax.axis_index("y"), lax.axis_index("c"))
    return pltpu.make_async_remote_copy(src_ref=seven, dst_ref=seven, send_sem=send_sem, recv_sem=recv_sem,
                                        device_id=me, device_id_type=pl.DeviceIdType.MESH)


def _own_slabs(name, arrs, gather, after):
    n = len(arrs)
    me = (4 * lax.axis_index("x") + 2 * lax.axis_index("y") + lax.axis_index("c")).astype(jnp.int32).reshape(1)

    def body(me_ref, *refs):
        for src, dst in zip(refs[:n], refs[n + 1:]):
            dst[...] = src[...]

    def zeros(k):
        return (0,) * k

    in_specs, out_specs, out_shape = [], [], []
    for a, g in zip(arrs, gather):
        slab = a.shape if g else a.shape[1:]
        nd = len(slab)
        if g:
            in_specs.append(pl.BlockSpec(slab, functools.partial(lambda i, me_ref, nd: zeros(nd), nd=nd)))
        else:
            in_specs.append(pl.BlockSpec((None,) + slab, functools.partial(lambda i, me_ref, nd: (me_ref[0],) + zeros(nd), nd=nd)))
        out_specs.append(pl.BlockSpec((None,) + slab, functools.partial(lambda i, me_ref, nd: (me_ref[0],) + zeros(nd), nd=nd)))
        out_shape.append(jax.ShapeDtypeStruct((NDEV,) + slab, a.dtype))
    in_specs.append(pl.BlockSpec(memory_space=pl.ANY))
    return pl.pallas_call(
        body, name=name, out_shape=out_shape,
        grid_spec=pltpu.PrefetchScalarGridSpec(num_scalar_prefetch=1, grid=(1,), in_specs=in_specs, out_specs=out_specs),
        compiler_params=_params("arbitrary"),
    )(me, *arrs, after)


def _exchange_start(name, arrs, gather, after):
    n = len(arrs)
    lands = _own_slabs(name + "_own", arrs, gather, after)

    def body(*refs):
        srcs, dsts = refs[:n], refs[n:2 * n]
        send_sems, recv_sems = refs[2 * n:3 * n], refs[3 * n:4 * n]
        token = refs[-1]
        for cp in _remote_copies(srcs, dsts, send_sems, recv_sems, gather):
            cp.start()
        token[...] = jnp.zeros(token.shape, token.dtype)

    thru = [pltpu.HBM(a.shape, a.dtype) for a in list(arrs) + list(lands)]
    outs = pl.pallas_call(
        body, name=name,
        out_shape=(*[pltpu.SemaphoreType.DMA(())] * (2 * n), *thru, jax.ShapeDtypeStruct((8, 128), F32)),
        in_specs=[_HBM] * (2 * n),
        out_specs=(*[_SEM] * (2 * n), *[_HBM] * (2 * n), pl.BlockSpec(memory_space=pltpu.VMEM)),
        input_output_aliases={i: 2 * n + i for i in range(2 * n)},
        compiler_params=pltpu.CompilerParams(has_side_effects=_EFFECT),
    )(*[pltpu.with_memory_space_constraint(a, pltpu.HBM) for a in list(arrs) + list(lands)])
    return dict(send=outs[:n], recv=outs[n:2 * n], srcs=outs[2 * n:3 * n], lands=outs[3 * n:4 * n], token=outs[-1],
                gather=gather)


def _exchange_wait(name, started, after):
    n = len(started["srcs"])
    after = list(after)

    def body(*refs):
        dsts = refs[n:2 * n]
        send_sems, recv_sems = refs[2 * n:3 * n], refs[3 * n:4 * n]
        for i in range(n):
            cp = _all_seven(dsts[i], send_sems[i], recv_sems[i])
            cp.wait_send()
            cp.wait_recv()

    bufs = list(started["srcs"]) + list(started["lands"])
    outs = pl.pallas_call(
        body, name=name, out_shape=tuple(pltpu.HBM(a.shape, a.dtype) for a in bufs),
        in_specs=[_HBM] * (2 * n) + [_SEM] * (2 * n) + [pl.BlockSpec(memory_space=pl.ANY)] * len(after),
        out_specs=(_HBM,) * (2 * n), input_output_aliases={i: i for i in range(2 * n)},
        compiler_params=pltpu.CompilerParams(has_side_effects=_EFFECT),
    )(*bufs, *started["send"], *started["recv"], *after)
    return outs[n:]


def _sum_in_order(parts):
    g = parts[0].astype(F32)
    for p in parts[1:]:
        g = g + p.astype(F32)
    return g


def _adam_update(g, w_, m_, v_):
    m2 = B1 * m_ + (1.0 - B1) * g
    v2 = B2 * v_ + (1.0 - B2) * (g * g)
    m_hat = m2 / (1.0 - B1 ** STEP)
    v_hat = v2 / (1.0 - B2 ** STEP)
    delta = -LR * (m_hat / (jnp.sqrt(v_hat) + ADAM_EPS) + WD * w_)
    return g, delta, m2, v2


def _adamw(name, g, w, m, v, br):
    rows, cols = w.shape
    return _ew(name, _adam_update, rows, br, [_ri(g), _ri(w), _ri(m), _ri(v)], [], [(cols, F32)] * 4)


def _adamw_layers(name, parts, w, m, v, br):
    rows, cols = w.shape
    nb = rows // DEPTH // br
    assert nb * br * DEPTH == rows
    outs = []
    for l in range(DEPTH):
        def body(*refs):
            vals = _adam_update(_sum_in_order([r[...] for r in refs[:NDEV]]), *[r[...] for r in refs[NDEV:NDEV + 3]])
            for r, val in zip(refs[-4:], vals):
                r[...] = val

        mine = pl.BlockSpec((br, cols), functools.partial(lambda i, l: (l * nb + i, 0), l=l))
        in_specs = [pl.BlockSpec((br, cols), functools.partial(lambda i, d: (d * nb + i, 0), d=d)) for d in range(NDEV)]
        in_specs += [mine] * 3 + [pl.BlockSpec(memory_space=pl.ANY)] * len(outs)
        outs = pl.pallas_call(
            body, name=f"{name}{l}", grid=(nb,), in_specs=in_specs, out_specs=[mine] * 4,
            out_shape=[jax.ShapeDtypeStruct((rows, cols), F32)] * 4,
            input_output_aliases={NDEV + 3 + q: q for q in range(len(outs))},
            compiler_params=_params("arbitrary"),
        )(*([parts[l]] * NDEV), w, m, v, *outs)
    return outs


SMALL = ("pre_norm_g", "lambda_re", "lambda_im", "log_dt", "b_re", "b_im", "c_re", "c_im", "d_skip", "b_glu",
         "post_norm_g")
BIG = ("w_in", "w_glu", "w_branch_s", "w_branch_a", "w_out")
WEIGHTS = ("pre_norm_g", "w_in", "lambda_re", "lambda_im", "log_dt", "b_re", "b_im", "c_re", "c_im", "d_skip",
           "w_glu", "b_glu", "w_branch_s", "w_branch_a", "w_out", "post_norm_g")
PACK_COLS = 1024
PACK_BR = 136


def _pack_layer(d):
    pieces = [d[k].astype(F32).reshape(-1) for k in SMALL]
    used = sum(p.shape[0] for p in pieces)
    assert used <= PACK_BR * PACK_COLS
    return jnp.concatenate(pieces + [jnp.zeros((PACK_BR * PACK_COLS - used,), F32)]).reshape(PACK_BR, PACK_COLS)


def _unpack(p, like):
    flat = p.reshape(DEPTH, PACK_BR * PACK_COLS)
    out, off = {}, 0
    for k in SMALL:
        n = like[k].size // DEPTH
        out[k] = flat[:, off:off + n].reshape(like[k].shape)
        off += n
    return out


def _local_step(x, target, small, started, weights_of, big_done, small_done, total_loss):
    preps = [_s5_prepare(l, {k: small[k][l] for k in SMALL}, started) for l in range(DEPTH)]
    res, ws = [], []
    for l in range(DEPTH):
        w_l, tok = weights_of(l, [x] + ([p[k] for p in preps for k in ("wdt", "cmt")] if l == 0 else []))
        sp = {k: small[k][l] for k in SMALL}
        if tok is not None:
            sp["pre_norm_g"] = sp["pre_norm_g"] + tok[0, 0]
        x, r = _layer_fwd(l, x, w_l, sp, preps[l])
        res.append(r)
        ws.append(w_l)

    def loss_fn(y, t):
        e = y - t
        return e * (1.0 / DM), jnp.sum(_colsum(0.5 * e * e * (1.0 / DM)), axis=1, keepdims=True)

    dx, loss = _ew("loss", loss_fn, SEQ, 256, [_ri(x), _ri(target)], [], [(DM, F32)], [1])
    total = total_loss(loss.reshape(()))
    for l in reversed(range(DEPTH)):
        dx, sm = _layer_bwd(l, dx, res[l], ws[l], {k: small[k][l] for k in SMALL}, big_done,
                            after=total.reshape(1, 1) if l == DEPTH - 1 else None)
        small_done(l, sm)
    return total, dx


def _full_weights(gathered):
    g = gathered
    return dict(
        w_in=g["w_in"],
        w_glu=g["w_glu"].reshape(SW, SW),
        w_branch_s=g["w_branch_s"].transpose(1, 0, 2).reshape(SW, DM),
        w_branch_a=g["w_branch_a"].transpose(1, 0, 2).reshape(AW, DM),
        w_out=g["w_out"].reshape(DM, DM),
    )


def kernel(x, pre_norm_g, w_in, lambda_re, lambda_im, log_dt, b_re, b_im, c_re, c_im, d_skip, w_glu, b_glu, w_branch_s, w_branch_a, w_out, post_norm_g, loss_target, m_pre_norm_g, m_w_in, m_lambda_re, m_lambda_im, m_log_dt, m_b_re, m_b_im, m_c_re, m_c_im, m_d_skip, m_w_glu, m_b_glu, m_w_branch_s, m_w_branch_a, m_w_out, m_post_norm_g, v_pre_norm_g, v_w_in, v_lambda_re, v_lambda_im, v_log_dt, v_b_re, v_b_im, v_c_re, v_c_im, v_d_skip, v_w_glu, v_b_glu, v_w_branch_s, v_w_branch_a, v_w_out, v_post_norm_g):
    wts = dict(pre_norm_g=pre_norm_g, w_in=w_in, lambda_re=lambda_re, lambda_im=lambda_im, log_dt=log_dt, b_re=b_re,
               b_im=b_im, c_re=c_re, c_im=c_im, d_skip=d_skip, w_glu=w_glu, b_glu=b_glu, w_branch_s=w_branch_s,
               w_branch_a=w_branch_a, w_out=w_out, post_norm_g=post_norm_g)
    mom = dict(pre_norm_g=m_pre_norm_g, w_in=m_w_in, lambda_re=m_lambda_re, lambda_im=m_lambda_im, log_dt=m_log_dt,
               b_re=m_b_re, b_im=m_b_im, c_re=m_c_re, c_im=m_c_im, d_skip=m_d_skip, w_glu=m_w_glu, b_glu=m_b_glu,
               w_branch_s=m_w_branch_s, w_branch_a=m_w_branch_a, w_out=m_w_out, post_norm_g=m_post_norm_g)
    var = dict(pre_norm_g=v_pre_norm_g, w_in=v_w_in, lambda_re=v_lambda_re, lambda_im=v_lambda_im, log_dt=v_log_dt,
               b_re=v_b_re, b_im=v_b_im, c_re=v_c_re, c_im=v_c_im, d_skip=v_d_skip, w_glu=v_w_glu, b_glu=v_b_glu,
               w_branch_s=v_w_branch_s, w_branch_a=v_w_branch_a, w_out=v_w_out, post_norm_g=v_post_norm_g)

    def gather_start(l, after):
        return _exchange_start(f"gather_start{l}", [wts[k][l].astype(BF16) for k in BIG], [True] * len(BIG), after)

    gathering = {0: gather_start(0, x)}
    sending, packed = {}, {}

    def weights_of(l, after):
        gathered = _exchange_wait(f"gather_wait{l}", gathering[l], after)
        tok = None
        if l + 1 < DEPTH:
            gathering[l + 1] = gather_start(l + 1, gathered[0])
            tok = gathering[l + 1]["token"]
        return _full_weights(dict(zip(BIG, gathered))), tok

    def big_done(l, big):
        arrs, kinds = [big[k] for k in BIG], [False] * len(BIG)
        if l + 1 < DEPTH:
            arrs, kinds = arrs + [packed[l + 1]], kinds + [True]
        sending[l] = _exchange_start(f"grads_start{l}", arrs, kinds, big["w_in"])
        return sending[l]["token"]

    def small_done(l, sm):
        packed[l] = _pack_layer(sm)

    loss, dx = _local_step(x[0], loss_target[0], wts, gathering[0]["token"], weights_of, big_done, small_done,
                           lambda part: lax.psum(part, ("x", "y", "c")))
    last = _exchange_start("grads_start_last", [packed[0]], [True], dx)
    recv_l = [_exchange_wait(f"grads_wait{l}", sending[l], [dx]) for l in range(DEPTH)]
    recv_last = _exchange_wait("grads_wait_last", last, [dx])
    recv_small = jnp.concatenate([recv_last[0]] + [recv_l[l][len(BIG)] for l in range(DEPTH - 1)], axis=1)

    grads, delta, new_m, new_v = {}, {}, {}, {}

    def update(k, g_or_parts):
        shape = wts[k].shape
        cols = shape[-1]
        rows = wts[k].size // cols
        w2, m2, v2 = (t[k].reshape(rows, cols) for t in (wts, mom, var))
        if k in BIG:
            per_layer = rows // DEPTH
            outs = _adamw_layers("adamw_" + k, [p.reshape(NDEV * per_layer, cols) for p in g_or_parts], w2, m2, v2,
                                 min(per_layer, 256))
        else:
            outs = _adamw("adamw_" + k, g_or_parts.reshape(rows, cols), w2, m2, v2,
                          min(rows, 1024 if cols <= 128 else 256))
        grads[k], delta[k], new_m[k], new_v[k] = (o.reshape(shape) for o in outs)

    for i, k in enumerate(BIG):
        update(k, [r[i] for r in recv_l])
    rows = DEPTH * PACK_BR
    (g_small,) = _ew("grads_small", lambda *p: (_sum_in_order(p),), rows, PACK_BR,
                     [_ri(recv_small.reshape(NDEV * rows, PACK_COLS), PACK_COLS, 0, d * DEPTH) for d in range(NDEV)], [],
                     [(PACK_COLS, F32)])
    for k, g in _unpack(g_small, wts).items():
        update(k, g)

    return (loss, dx[None], *[grads[k] for k in WEIGHTS], *[delta[k] for k in WEIGHTS],
            *[new_m[k] for k in WEIGHTS], *[new_v[k] for k in WEIGHTS])
```

```python
import functools
import math

import jax
import jax.numpy as jnp
from jax import lax
from jax.experimental import pallas as pl
from jax.experimental.pallas import tpu as pltpu

F32 = jnp.float32
BF16 = jnp.bfloat16

NDEV = 8
DEPTH = 4
SEQ = 2048
DM = 1024
NCOL = 8192
SW = 512
NGRP = 32
GCH = 16
NST = 64
NS = NGRP * NST
HD = 128
AW = 512
DILATIONS = (1, 4, 16)
ABLK = 128
ATTN_PAIR = 4
RMS_EPS = 1e-6
LR, B1, B2, ADAM_EPS, WD, STEP = 0.001, 0.9, 0.999, 1e-08, 0.01, 10

CB_U, CB_ZS, CB_Q, CB_K, CB_V, CB_ZA = 0, 1, 2, 5, 8, 11
CB_GS, CB_GA = 6, 7

VMEM_LIMIT = 56 * 2 ** 20


def _row_order(j):
    return jnp.where(j < CB_Q, 1, 0)


def _params(*sem):
    return pltpu.CompilerParams(dimension_semantics=sem, vmem_limit_bytes=VMEM_LIMIT)


def _ew(name, fn, rows, br, row_ins, bc_ins, row_outs, red_outs=()):
    n_in = len(row_ins) + len(bc_ins)
    n_ro = len(row_outs)
    steps = rows // br
    assert steps * br == rows

    def body(*refs):
        vals = fn(*[r[...] for r in refs[:n_in]])
        outs = refs[n_in:]
        for r, v in zip(outs[:n_ro], vals[:n_ro]):
            r[...] = v.astype(r.dtype)
        if red_outs:
            @pl.when(pl.program_id(0) == 0)
            def _():
                for r in outs[n_ro:]:
                    r[...] = jnp.zeros(r.shape, r.dtype)
            for r, v in zip(outs[n_ro:], vals[n_ro:]):
                r[...] += v

    in_specs = []
    for (_, w, cb, rb) in row_ins:
        in_specs.append(pl.BlockSpec((br, w), functools.partial(lambda i, cb, rb: (rb + i, cb), cb=cb, rb=rb)))
    for a in bc_ins:
        in_specs.append(pl.BlockSpec(a.shape, functools.partial(lambda i, nd: (0,) * nd, nd=a.ndim)))
    out_specs = [pl.BlockSpec((br, w), lambda i: (i, 0)) for (w, _) in row_outs]
    out_specs += [pl.BlockSpec((1, w), lambda i: (0, 0)) for w in red_outs]
    out_shape = [jax.ShapeDtypeStruct((rows, w), dt) for (w, dt) in row_outs]
    out_shape += [jax.ShapeDtypeStruct((1, w), F32) for w in red_outs]
    return pl.pallas_call(
        body, name=name, grid=(steps,), in_specs=in_specs, out_specs=out_specs, out_shape=out_shape,
        compiler_params=_params("arbitrary"),
    )(*[a for (a, _, _, _) in row_ins], *bc_ins)


def _ri(a, w=None, cb=0, rb=0):
    return (a, a.shape[1] if w is None else w, cb, rb)


_DIMS = {"nn": ((1,), (0,)), "nt": ((1,), (1,)), "tn": ((0,), (0,))}


def _after(after):
    return ([pl.BlockSpec(memory_space=pl.ANY)], [after]) if after is not None else ([], [])


def _mm(name, a, b, mode, M, N, K, bm, bn, bk, out_dtype, a_spec=None, b_spec=None, o_spec=None, out_shape=None,
        after=None):
    nk = K // bk
    assert M % bm == 0 and N % bn == 0 and nk * bk == K
    after_specs, after_args = _after(after)

    own_acc = nk > 1 and out_dtype != F32

    def body(a_ref, b_ref, *rest):
        o_ref, scratch = rest[len(after_args)], rest[len(after_args) + 1:]
        part = lax.dot_general(a_ref[...].astype(BF16), b_ref[...].astype(BF16), (_DIMS[mode], ((), ())),
                               preferred_element_type=F32)
        if nk == 1:
            o_ref[...] = part.astype(o_ref.dtype)
            return
        k = pl.program_id(2)
        acc_ref = scratch[0] if own_acc else o_ref

        @pl.when(k == 0)
        def _():
            acc_ref[...] = part

        @pl.when(k > 0)
        def _():
            acc_ref[...] += part

        if own_acc:
            @pl.when(k == nk - 1)
            def _():
                o_ref[...] = acc_ref[...].astype(o_ref.dtype)

    if a_spec is None:
        a_spec = (pl.BlockSpec((bk, bm), lambda i, j, k: (k, i)) if mode == "tn"
                  else pl.BlockSpec((bm, bk), lambda i, j, k: (i, k)))
    if b_spec is None:
        b_spec = (pl.BlockSpec((bn, bk), lambda i, j, k: (j, k)) if mode == "nt"
                  else pl.BlockSpec((bk, bn), lambda i, j, k: (k, j)))
    if o_spec is None:
        o_spec = pl.BlockSpec((bm, bn), lambda i, j, k: (i, j))
    if out_shape is None:
        out_shape = (M, N)
    return pl.pallas_call(
        body, name=name, grid=(M // bm, N // bn, nk), in_specs=[a_spec, b_spec] + after_specs, out_specs=o_spec,
        out_shape=jax.ShapeDtypeStruct(out_shape, out_dtype),
        scratch_shapes=[pltpu.VMEM((bm, bn), F32)] if own_acc else [],
        compiler_params=_params("parallel", "parallel", "arbitrary"),
    )(a, b, *after_args)


SCAN_LANES = 512
SCAN_CHUNKS = 8


def _to_chunked(a):
    return a.reshape(SCAN_CHUNKS, SEQ // SCAN_CHUNKS, -1).transpose(1, 0, 2).reshape(SEQ, -1)


def _from_chunked(a):
    return a.reshape(SEQ // SCAN_CHUNKS, SCAN_CHUNKS, -1).transpose(1, 0, 2).reshape(SEQ, -1)


def _scan_block(dr_ref, di_ref, sr_ref, si_ref, lam_r, lam_i, reverse):
    T = SEQ // SCAN_CHUNKS
    bl = lam_r.shape[1]
    assert T == 2 ** 8
    ar = jnp.broadcast_to(lam_r, (SCAN_CHUNKS, bl))
    ai = jnp.broadcast_to(lam_i, (SCAN_CHUNKS, bl))
    zero = jnp.zeros((SCAN_CHUNKS, bl), F32)

    def tile(j):
        return pl.ds(pl.multiple_of(j * SCAN_CHUNKS, SCAN_CHUNKS), SCAN_CHUNKS)

    def step(jj, carry):
        sr, si = carry
        j = T - 1 - jj if reverse else jj
        nr = ar * sr - ai * si + dr_ref[tile(j), :]
        ni = ar * si + ai * sr + di_ref[tile(j), :]
        sr_ref[tile(j), :] = nr
        si_ref[tile(j), :] = ni
        return nr, ni

    er, ei = lax.fori_loop(0, T, step, (zero, zero), unroll=4)

    pr, pi = ar[0:1], ai[0:1]
    for _ in range(8):
        pr, pi = pr * pr - pi * pi, 2.0 * pr * pi
    rows = lax.broadcasted_iota(jnp.int32, (SCAN_CHUNKS, bl), 0)
    cr, ci = zero, zero
    xr = jnp.zeros((1, bl), F32)
    xi = jnp.zeros((1, bl), F32)
    order = range(SCAN_CHUNKS - 2, -1, -1) if reverse else range(1, SCAN_CHUNKS)
    for c in order:
        src = c + 1 if reverse else c - 1
        nxr = pr * xr - pi * xi + er[src:src + 1]
        nxi = pr * xi + pi * xr + ei[src:src + 1]
        xr, xi = nxr, nxi
        cr = jnp.where(rows == c, xr, cr)
        ci = jnp.where(rows == c, xi, ci)

    def fix(jj, pw):
        pwr, pwi = pw
        j = T - 1 - jj if reverse else jj
        sr_ref[tile(j), :] = sr_ref[tile(j), :] + (pwr * cr - pwi * ci)
        si_ref[tile(j), :] = si_ref[tile(j), :] + (pwr * ci + pwi * cr)
        return pwr * ar - pwi * ai, pwr * ai + pwi * ar

    lax.fori_loop(0, T, fix, (ar, ai), unroll=4)


def _s5_forward(name, proj, wdt, cmt, lam_r, lam_i):
    bl = SCAN_LANES
    nblk = NS // bl
    cw = bl * GCH // NST
    assert cw == 128 and CB_U == 0

    def nt(a, b_):
        return lax.dot_general(a, b_, (((1,), (1,)), ((), ())), preferred_element_type=F32)

    def body(u_ref, wr_ref, wi_ref, cr_ref, ci_ref, ar_ref, ai_ref, s_ref, y_ref):
        sr_ref, si_ref = s_ref.at[0], s_ref.at[1]
        u = u_ref[...].astype(BF16)
        sr_ref[...] = nt(u, wr_ref[...])
        si_ref[...] = nt(u, wi_ref[...])
        _scan_block(sr_ref, si_ref, sr_ref, si_ref, ar_ref[...], ai_ref[...], False)
        y_ref[...] = nt(sr_ref[...].astype(BF16), cr_ref[...]) + nt(si_ref[...].astype(BF16), ci_ref[...])

    return pl.pallas_call(
        body, name=name, grid=(nblk,),
        in_specs=[pl.BlockSpec((SEQ, cw), lambda i: (0, i)),
                  pl.BlockSpec((bl, cw), lambda i: (i, i)), pl.BlockSpec((bl, cw), lambda i: (nblk + i, i)),
                  pl.BlockSpec((cw, bl), lambda i: (i, i)), pl.BlockSpec((cw, bl), lambda i: (SW // cw + i, i)),
                  pl.BlockSpec((1, bl), lambda i: (0, i)), pl.BlockSpec((1, bl), lambda i: (0, i))],
        out_specs=[pl.BlockSpec((2, SEQ, bl), lambda i: (0, 0, i)), pl.BlockSpec((SEQ, cw), lambda i: (0, i))],
        out_shape=[jax.ShapeDtypeStruct((2, SEQ, NS), F32), jax.ShapeDtypeStruct((SEQ, SW), F32)],
        compiler_params=_params("arbitrary"),
    )(proj, wdt, wdt, cmt, cmt, lam_r, lam_i)


S5_BWD_LANES = 512


def _s5_backward(name, dy0, proj, s, wdt, cmt, lam_r, lam_i):
    bl = S5_BWD_LANES
    nblk = NS // bl
    cw = 128
    per = cw // (bl * GCH // NST)
    assert per >= 1 and CB_U == 0

    def tn(a, b_):
        return lax.dot_general(a, b_, (((0,), (0,)), ((), ())), preferred_element_type=F32)

    def prev(s_ref):
        last = pltpu.roll(s_ref[SEQ - SCAN_CHUNKS:SEQ, :], 1, 0)
        first = jnp.where(lax.broadcasted_iota(jnp.int32, (SCAN_CHUNKS, bl), 0) > 0, last, 0.0)
        return jnp.concatenate([first, s_ref[0:SEQ - SCAN_CHUNKS, :]], axis=0)

    def body(dy_ref, u_ref, s_ref, cr_ref, ci_ref, wr_ref, wi_ref, lr_ref, li_ref,
             dlr_ref, dli_ref, dwdt_ref, dcmt_ref, du_ref, ar_ref, ai_ref):
        dy = dy_ref[...]
        ar_ref[...] = jnp.dot(dy, cr_ref[...], preferred_element_type=F32)
        ai_ref[...] = jnp.dot(dy, ci_ref[...], preferred_element_type=F32)
        _scan_block(ar_ref, ai_ref, ar_ref, ai_ref, lr_ref[...], -li_ref[...], True)
        a_r, a_i = ar_ref[...], ai_ref[...]
        sr_ref, si_ref = s_ref.at[0], s_ref.at[1]
        spr, spi = prev(sr_ref), prev(si_ref)
        dlr_ref[...] = jnp.sum(a_r * spr + a_i * spi, axis=0, keepdims=True)
        dli_ref[...] = jnp.sum(a_i * spr - a_r * spi, axis=0, keepdims=True)
        a_rb, a_ib = a_r.astype(BF16), a_i.astype(BF16)
        u = u_ref[...].astype(BF16)
        dwdt_ref[0] = tn(a_rb, u)
        dwdt_ref[1] = tn(a_ib, u)
        dcmt_ref[0] = tn(dy, sr_ref[...].astype(BF16))
        dcmt_ref[1] = tn(dy, si_ref[...].astype(BF16))
        part = (jnp.dot(a_rb, wr_ref[...], preferred_element_type=F32)
                + jnp.dot(a_ib, wi_ref[...], preferred_element_type=F32))

        @pl.when(pl.program_id(0) % per == 0)
        def _():
            du_ref[...] = part

        @pl.when(pl.program_id(0) % per > 0)
        def _():
            du_ref[...] += part

    lam_spec = pl.BlockSpec((1, bl), lambda i: (0, i))
    return pl.pallas_call(
        body, name=name, grid=(nblk,),
        in_specs=[pl.BlockSpec((SEQ, cw), lambda i: (0, i // per)), pl.BlockSpec((SEQ, cw), lambda i: (0, i // per)),
                  pl.BlockSpec((2, SEQ, bl), lambda i: (0, 0, i)),
                  pl.BlockSpec((cw, bl), lambda i: (i // per, i)),
                  pl.BlockSpec((cw, bl), lambda i: (SW // cw + i // per, i)),
                  pl.BlockSpec((bl, cw), lambda i: (i, i // per)), pl.BlockSpec((bl, cw), lambda i: (nblk + i, i // per)),
                  lam_spec, lam_spec],
        out_specs=[lam_spec, lam_spec, pl.BlockSpec((2, bl, cw), lambda i: (0, i, i // per)),
                   pl.BlockSpec((2, cw, bl), lambda i: (0, i // per, i)),
                   pl.BlockSpec((SEQ, cw), lambda i: (0, i // per))],
        out_shape=[jax.ShapeDtypeStruct((1, NS), F32), jax.ShapeDtypeStruct((1, NS), F32),
                   jax.ShapeDtypeStruct((2, NS, SW), F32), jax.ShapeDtypeStruct((2, SW, NS), F32),
                   jax.ShapeDtypeStruct((SEQ, SW), F32)],
        scratch_shapes=[pltpu.VMEM((SEQ, bl), F32)] * 2,
        compiler_params=_params("arbitrary"),
    )(dy0, proj, s, cmt, cmt, wdt, wdt, lam_r, lam_i)


def _scores(qb, kb, prev):
    s = lax.dot_general(qb, kb, (((1,), (1,)), ((), ())), preferred_element_type=F32) * (HD ** -0.5)
    row = lax.broadcasted_iota(jnp.int32, (ABLK, ABLK), 0)
    col = lax.broadcasted_iota(jnp.int32, (ABLK, ABLK), 1)
    return jnp.where((col >= row) if prev else (col <= row), s, -1e30)


def _block_rows(dil, r, b):
    if dil == 1:
        return pl.ds(pl.multiple_of(b * ABLK, ABLK), ABLK)
    return pl.ds(r + dil * ABLK * b, ABLK, stride=dil)


def _group_blocks(dil):
    nb = SEQ // dil // ABLK
    shift = nb.bit_length() - 1
    return nb, (lambda idx: (idx >> shift, idx & (nb - 1)))


def _qkv_specs(j_of):
    return [pl.BlockSpec((SEQ, HD), functools.partial(lambda j, c: (0, c + j_of(j)), c=(cb + g) * 4))
            for g in range(3) for cb in (CB_Q, CB_K, CB_V)]


def _attention_fwd(name, proj):
    def body(*refs):
        qkv, z_ref = refs[:9], refs[9]
        y_ref, ya_ref, l_ref = refs[10:13]
        accs, maxs, dens = refs[13:16], refs[16:19], refs[19:22]
        for g, dil in enumerate(DILATIONS):
            q_ref, k_ref, v_ref = qkv[3 * g:3 * g + 3]
            nb, where = _group_blocks(dil)

            def step(t, c, g=g, dil=dil, nb=nb, where=where, q_ref=q_ref, k_ref=k_ref, v_ref=v_ref):
                two = range(ATTN_PAIR)
                rb = [where(t + i * (SEQ // ABLK // ATTN_PAIR)) for i in two]
                rows = [_block_rows(dil, r, b) for r, b in rb]
                qb = [q_ref[rows[i], :].astype(BF16) for i in two]
                s_c = [_scores(qb[i], k_ref[rows[i], :].astype(BF16), False) for i in two]
                if nb > 1:
                    prev = [_block_rows(dil, r, jnp.maximum(b - 1, 0)) for r, b in rb]
                    s_p = [jnp.where(rb[i][1] > 0, _scores(qb[i], k_ref[prev[i], :].astype(BF16), True), -1e30)
                           for i in two]
                m = [jnp.max(s_c[i], axis=-1, keepdims=True) for i in two]
                if nb > 1:
                    m = [jnp.maximum(m[i], jnp.max(s_p[i], axis=-1, keepdims=True)) for i in two]
                p_c = [jnp.exp(s_c[i] - m[i]) for i in two]
                den = [jnp.sum(p_c[i], axis=-1, keepdims=True) for i in two]
                acc = [jnp.dot(p_c[i].astype(BF16), v_ref[rows[i], :].astype(BF16), preferred_element_type=F32)
                       for i in two]
                if nb > 1:
                    p_p = [jnp.exp(s_p[i] - m[i]) for i in two]
                    den = [den[i] + jnp.sum(p_p[i], axis=-1, keepdims=True) for i in two]
                    acc = [acc[i] + jnp.dot(p_p[i].astype(BF16), v_ref[prev[i], :].astype(BF16),
                                            preferred_element_type=F32) for i in two]
                for i in two:
                    accs[g][rows[i], :] = acc[i]
                    maxs[g][rows[i], :] = jnp.broadcast_to(m[i], (ABLK, HD))
                    dens[g][rows[i], :] = jnp.broadcast_to(den[i], (ABLK, HD))
                return c

            lax.fori_loop(0, SEQ // ABLK // ATTN_PAIR, step, 0)
        top = jnp.maximum(jnp.maximum(maxs[0][...], maxs[1][...]), maxs[2][...])
        den = jnp.zeros((SEQ, HD), F32)
        y = jnp.zeros((SEQ, HD), F32)
        for g in range(3):
            wgt = jnp.exp(maxs[g][...] - top)
            den = den + wgt * dens[g][...]
            y = y + wgt * accs[g][...]
        y = y / den
        y_ref[...] = y
        ya_ref[...] = (y * _silu(z_ref[...])).astype(ya_ref.dtype)
        l_ref[...] = top + jnp.log(den)

    ospec = pl.BlockSpec((SEQ, HD), lambda j: (0, j))
    return pl.pallas_call(
        body, name=name, grid=(AW // HD,),
        in_specs=_qkv_specs(lambda j: j) + [pl.BlockSpec((SEQ, HD), lambda j: (0, CB_ZA * 4 + j))],
        out_specs=[ospec, ospec, ospec],
        out_shape=[jax.ShapeDtypeStruct((SEQ, AW), F32), jax.ShapeDtypeStruct((SEQ, AW), BF16),
                   jax.ShapeDtypeStruct((SEQ, AW), F32)],
        scratch_shapes=[pltpu.VMEM((SEQ, HD), F32)] * 9,
        compiler_params=_params("parallel"),
    )(*([proj] * 10))


def _attention_bwd(name, proj, dya, y, lse):
    def tn(a, b_):
        return lax.dot_general(a, b_, (((0,), (0,)), ((), ())), preferred_element_type=F32)

    def nt(a, b_):
        return lax.dot_general(a, b_, (((1,), (1,)), ((), ())), preferred_element_type=F32)

    def body(*refs):
        qkv, z_ref, dya_ref, y_ref, l_ref = refs[:9], refs[9], refs[10], refs[11], refs[12]
        outs, dza_ref = refs[13:22], refs[22]
        dy_s, dsum_s, dq_s, dk_own, dv_own, dk_prev, dv_prev = refs[23:]
        _, vjp = jax.vjp(lambda y_, z_: y_ * _silu(z_), y_ref[...], z_ref[...])
        dy, dz = vjp(dya_ref[...])
        dza_ref[...] = dz.astype(dza_ref.dtype)
        dy_s[...] = dy
        dsum_s[...] = jnp.broadcast_to(jnp.sum(dy * y_ref[...], axis=-1, keepdims=True), (SEQ, HD))
        for g, dil in enumerate(DILATIONS):
            q_ref, k_ref, v_ref = qkv[3 * g:3 * g + 3]
            nb, where = _group_blocks(dil)
            if nb > 1:
                dk_prev[...] = jnp.zeros(dk_prev.shape, F32)
                dv_prev[...] = jnp.zeros(dv_prev.shape, F32)

            def step(t, c, dil=dil, nb=nb, where=where, q_ref=q_ref, k_ref=k_ref, v_ref=v_ref):
                rb = [where(t + i * (SEQ // ABLK // ATTN_PAIR)) for i in range(ATTN_PAIR)]
                sides = []
                for r, b in rb:
                    rows = _block_rows(dil, r, b)
                    own = dict(b=b, qrows=rows, krows=rows, prev=False, dk=dk_own, dv=dv_own,
                               q=q_ref[rows, :].astype(BF16), dy=dy_s[rows, :].astype(BF16))
                    sides.append(own)
                    if nb > 1:
                        sides.append(dict(own, krows=_block_rows(dil, r, jnp.maximum(b - 1, 0)), prev=True,
                                          dk=dk_prev, dv=dv_prev))
                for s_ in sides:
                    s_["k"] = k_ref[s_["krows"], :].astype(BF16)
                    s_["v"] = v_ref[s_["krows"], :].astype(BF16)
                for s_ in sides:
                    sc = _scores(s_["q"], s_["k"], s_["prev"])
                    s_["s"] = jnp.where(s_["b"] > 0, sc, -1e30) if s_["prev"] else sc
                    s_["dp"] = nt(s_["dy"], s_["v"])
                for s_ in sides:
                    p = jnp.exp(s_["s"] - l_ref[s_["qrows"], :])
                    s_["p"] = p.astype(BF16)
                    s_["ds"] = (p * (s_["dp"] - dsum_s[s_["qrows"], :]) * (HD ** -0.5)).astype(BF16)
                for s_ in sides:
                    s_["dk"][s_["krows"], :] = tn(s_["ds"], s_["q"])
                    s_["dv"][s_["krows"], :] = tn(s_["p"], s_["dy"])
                    s_["dq"] = jnp.dot(s_["ds"], s_["k"], preferred_element_type=F32)
                per = len(sides) // ATTN_PAIR
                for i in range(ATTN_PAIR):
                    dq = sides[i * per]["dq"]
                    if per > 1:
                        dq = dq + sides[i * per + 1]["dq"]
                    dq_s[sides[i * per]["qrows"], :] = dq
                return c

            lax.fori_loop(0, SEQ // ABLK // ATTN_PAIR, step, 0)
            dq_ref, dk_ref, dv_ref = outs[3 * g:3 * g + 3]
            dq_ref[...] = dq_s[...].astype(dq_ref.dtype)
            if nb > 1:
                dk_ref[...] = (dk_own[...] + dk_prev[...]).astype(dk_ref.dtype)
                dv_ref[...] = (dv_own[...] + dv_prev[...]).astype(dv_ref.dtype)
            else:
                dk_ref[...] = dk_own[...].astype(dk_ref.dtype)
                dv_ref[...] = dv_own[...].astype(dv_ref.dtype)

    ospec = pl.BlockSpec((SEQ, HD), lambda j: (0, j))
    outs = pl.pallas_call(
        body, name=name, grid=(AW // HD,),
        in_specs=_qkv_specs(lambda j: j) + [pl.BlockSpec((SEQ, HD), lambda j: (0, CB_ZA * 4 + j))] + [ospec] * 3,
        out_specs=[ospec] * 10, out_shape=[jax.ShapeDtypeStruct((SEQ, AW), BF16)] * 10,
        scratch_shapes=[pltpu.VMEM((SEQ, HD), F32)] * 7,
        compiler_params=_params("parallel"),
    )(*([proj] * 10), dya, y, lse)
    return outs[:9], outs[9]


def _rms(x, g):
    return x * lax.rsqrt(jnp.mean(x * x, axis=-1, keepdims=True) + RMS_EPS) * g


def _sig(x):
    return 1.0 / (1.0 + jnp.exp(-x))


def _silu(x):
    return x * _sig(x)


def _gelu(x):
    return 0.5 * x * (1.0 + jnp.tanh(math.sqrt(2.0 / math.pi) * (x + 0.044715 * (x * x * x))))


def _y1_fn(y0p, u, dskip):
    return _gelu(y0p + dskip * u)


def _ys_fn(y1, t, z, bglu):
    return y1 * _sig(t + bglu) * _silu(z)


def _merge_fn(ms, ma, gs, ga):
    return _sig(gs) * ms + _sig(ga) * ma


def _colsum(v):
    return jnp.sum(v, axis=0, keepdims=True)


def _lam_fn(lre, lim, ldt):
    a = jnp.minimum(lre, -1e-4)
    dt = jnp.exp(ldt)
    mag = jnp.exp(a * dt)
    ar = mag * jnp.cos(lim * dt)
    ai = mag * jnp.sin(lim * dt)
    den = a * a + lim * lim
    cr = ((ar - 1.0) * a + ai * lim) / den
    ci = (ai * a - (ar - 1.0) * lim) / den
    return ar, ai, cr, ci


def _bbar_fn(cr, ci, bre, bim):
    return cr * bre - ci * bim, cr * bim + ci * bre


def _same_group(rows, a, cols, b):
    r = lax.broadcasted_iota(jnp.int32, (rows, cols), 0) >> (a.bit_length() - 1)
    c = lax.broadcasted_iota(jnp.int32, (rows, cols), 1) >> (b.bit_length() - 1)
    return r == c


def _expand(name, blocks, signs, a, b, dtype, after=None):
    rows, cols = NGRP * a, NGRP * b
    n = len(blocks)
    assert a & (a - 1) == 0 and b & (b - 1) == 0
    after_specs, after_args = _after(after)

    def body(*refs):
        o_ref = refs[-1]
        tile = (lax.broadcasted_iota(jnp.int32, (b, cols), 1) & (b - 1)
                == lax.broadcasted_iota(jnp.int32, (b, cols), 0)).astype(F32)
        keep = _same_group(rows, a, cols, b)
        for i, (ref, sign) in enumerate(zip(refs[:n], signs)):
            spread = jnp.dot(ref[...], tile, preferred_element_type=F32, precision=lax.Precision.HIGHEST)
            o_ref[i * rows:(i + 1) * rows, :] = jnp.where(keep, sign * spread, 0.0).astype(o_ref.dtype)

    return pl.pallas_call(body, name=name, grid=(1,),
                          in_specs=[pl.BlockSpec((rows, b), lambda i: (0, 0))] * n + after_specs,
                          out_specs=pl.BlockSpec((n * rows, cols), lambda i: (0, 0)),
                          out_shape=jax.ShapeDtypeStruct((n * rows, cols), dtype),
                          compiler_params=_params("arbitrary"))(*blocks, *after_args)


def _extract(name, m, a, b, at=(0, 0), after=None):
    rows, cols = NGRP * a, NGRP * b
    assert a & (a - 1) == 0 and b & (b - 1) == 0
    after_specs, after_args = _after(after)

    def body(m_ref, *rest):
        o_ref = rest[-1]
        tile = (lax.broadcasted_iota(jnp.int32, (cols, b), 0) & (b - 1)
                == lax.broadcasted_iota(jnp.int32, (cols, b), 1)).astype(F32)
        kept = jnp.where(_same_group(rows, a, cols, b), m_ref[...], 0.0)
        o_ref[...] = jnp.dot(kept, tile, preferred_element_type=F32, precision=lax.Precision.HIGHEST)

    return pl.pallas_call(body, name=name, grid=(1,),
                          in_specs=[pl.BlockSpec((rows, cols), lambda i: at)] + after_specs,
                          out_specs=pl.BlockSpec((rows, b), lambda i: (0, 0)),
                          out_shape=jax.ShapeDtypeStruct((rows, b), F32),
                          compiler_params=_params("arbitrary"))(m, *after_args)


def _s5_prepare(l, sp, after):
    tag = f"l{l}_"
    ar, ai, cr, ci = _ew(tag + "lam", _lam_fn, NGRP, NGRP,
                         [_ri(sp["lambda_re"]), _ri(sp["lambda_im"]),
                          _ri(sp["log_dt"].reshape(NGRP, 1) + after[0, 0])], [], [(NST, F32)] * 4)
    bre = sp["b_re"].reshape(NS, GCH)
    bim = sp["b_im"].reshape(NS, GCH)
    bbr, bbi = _ew(tag + "bbar", _bbar_fn, NS, NS, [_ri(cr.reshape(NS, 1)), _ri(ci.reshape(NS, 1)), _ri(bre), _ri(bim)],
                   [], [(GCH, F32)] * 2)
    wdt = _expand(tag + "wdt", [bbr, bbi], [1.0, 1.0], NST, GCH, BF16)
    cmt = _expand(tag + "cmt", [sp["c_re"].reshape(SW, NST), sp["c_im"].reshape(SW, NST)], [1.0, -1.0], GCH, NST, BF16,
                  after=after)
    return dict(ar=ar, ai=ai, cr=cr, ci=ci, wdt=wdt, cmt=cmt)


def _layer_fwd(l, x, w, sp, prep):
    tag = f"l{l}_"
    g1 = sp["pre_norm_g"].reshape(1, DM)
    (h,) = _ew(tag + "rms1", lambda x_, g: (_rms(x_, g),), SEQ, 256, [_ri(x)], [g1], [(DM, BF16)])
    hv = jnp.stack([h, _to_chunked(h)])
    win = w["w_in"]
    proj = _mm(tag + "proj", hv, win, "nn", SEQ, NCOL, DM, SEQ, 512, 1024, F32,
               a_spec=pl.BlockSpec((None, SEQ, 1024), lambda i, j, k: (_row_order(j), 0, 0)),
               b_spec=pl.BlockSpec((None, 1024, 512), lambda i, j, k: (j // 2, 0, j % 2)))

    ar, ai, cr, ci, wdt, cmt = (prep[k] for k in ("ar", "ai", "cr", "ci", "wdt", "cmt"))
    s, y0p = _s5_forward(tag + "s5", proj, wdt, cmt, ar.reshape(1, NS), ai.reshape(1, NS))
    dskip = sp["d_skip"].reshape(1, SW)
    (y1,) = _ew(tag + "y1", lambda a, u, d: (_y1_fn(a, u, d),), SEQ, 256,
                [_ri(y0p), _ri(proj, SW, CB_U)], [dskip], [(SW, F32)])
    t = _mm(tag + "glu", y1, w["w_glu"], "nn", SEQ, SW, SW, 1024, 512, 512, F32)
    bglu = sp["b_glu"].reshape(1, SW)
    (ys_c,) = _ew(tag + "ys", lambda y1_, t_, z, b_: (_ys_fn(y1_, t_, z, b_),), SEQ, 256,
                  [_ri(y1), _ri(t), _ri(proj, SW, CB_ZS)], [bglu], [(SW, BF16)])
    ys = _from_chunked(ys_c)

    ypre, ya, lse = _attention_fwd(tag + "attn", proj)

    ms = _mm(tag + "branch_s", ys, w["w_branch_s"], "nn", SEQ, DM, SW, 1024, 1024, 512, F32)
    ma = _mm(tag + "branch_a", ya, w["w_branch_a"], "nn", SEQ, DM, AW, 1024, 1024, 512, F32)
    (merged,) = _ew(tag + "merge", lambda a, b_, c, d: (_merge_fn(a, b_, c, d),), SEQ, 256,
                    [_ri(ms), _ri(ma), _ri(proj, DM, CB_GS), _ri(proj, DM, CB_GA)], [], [(DM, BF16)])
    out = _mm(tag + "out", merged, w["w_out"], "nn", SEQ, DM, DM, 1024, 1024, 1024, F32)
    g2 = sp["post_norm_g"].reshape(1, DM)
    (x_new,) = _ew(tag + "post", lambda x_, o, g: (x_ + _rms(o, g),), SEQ, 256, [_ri(x), _ri(out)], [g2], [(DM, F32)])
    res = dict(x=x, hv=hv, proj=proj, ar=ar, ai=ai, cr=cr, ci=ci, wdt=wdt, cmt=cmt, s=s, y0p=y0p,
               y1=y1, t=t, ys=ys, ya=ya, ypre=ypre, lse=lse, ms=ms, ma=ma, merged=merged, out=out)
    return x_new, res


def _layer_bwd(l, dxn, r, w, sp, big_done, after=None):
    tag = f"l{l}b_"
    proj = r["proj"]
    g1 = sp["pre_norm_g"].reshape(1, DM)
    g2 = sp["post_norm_g"].reshape(1, DM)
    dskip = sp["d_skip"].reshape(1, SW)
    bglu = sp["b_glu"].reshape(1, SW)

    def post_b(d, o, g):
        _, vjp = jax.vjp(_rms, o, g)
        do, dg = vjp(d)
        return do, dg

    d_out, dg2 = _ew(tag + "post", post_b, SEQ, 256, [_ri(dxn), _ri(r["out"])], [g2], [(DM, BF16)], [DM])
    dw_out = _mm(tag + "dw_out", r["merged"], d_out, "tn", DM, DM, SEQ, 1024, 1024, SEQ, BF16, after=after)
    dmerged = _mm(tag + "dmerged", d_out, w["w_out"], "nt", SEQ, DM, DM, 1024, 1024, 1024, F32, after=after)

    def merge_b(d, ms, ma, gs, ga):
        _, vjp = jax.vjp(_merge_fn, ms, ma, gs, ga)
        return vjp(d)

    dms, dma, dgs, dga = _ew(tag + "merge", merge_b, SEQ, 256,
                             [_ri(dmerged), _ri(r["ms"]), _ri(r["ma"]), _ri(proj, DM, CB_GS), _ri(proj, DM, CB_GA)],
                             [], [(DM, BF16)] * 4)
    dw_bs = _mm(tag + "dw_bs", r["ys"], dms, "tn", SW, DM, SEQ, 512, 1024, SEQ, BF16)
    dw_ba = _mm(tag + "dw_ba", r["ya"], dma, "tn", AW, DM, SEQ, 512, 1024, SEQ, BF16)
    dys = _mm(tag + "dys", dms, w["w_branch_s"], "nt", SEQ, SW, DM, 1024, 512, 1024, F32)
    dya = _mm(tag + "dya", dma, w["w_branch_a"], "nt", SEQ, AW, DM, 1024, 512, 1024, F32)

    dqkv, dza = _attention_bwd(tag + "attn", proj, dya, r["ypre"], r["lse"])

    def ys_b(d, y1, t, z, b_):
        _, vjp = jax.vjp(_ys_fn, y1, t, z, b_)
        dy1, dt, dz, _ = vjp(d)
        return dy1, dt, dz, _colsum(dt)

    dy1a, dt, dzs, dbglu = _ew(tag + "ys", ys_b, SEQ, 256,
                               [_ri(_to_chunked(dys)), _ri(r["y1"]), _ri(r["t"]), _ri(proj, SW, CB_ZS)],
                               [bglu], [(SW, F32), (SW, BF16), (SW, BF16)], [SW])
    dw_glu = _mm(tag + "dw_glu", r["y1"], dt, "tn", SW, SW, SEQ, 512, 512, SEQ, BF16)
    dy1b = _mm(tag + "dy1b", dt, w["w_glu"], "nt", SEQ, SW, SW, 1024, 512, 512, F32)

    def y1_b(da, db, y0p, u, d_):
        _, vjp = jax.vjp(_y1_fn, y0p, u, d_)
        dy0, du, dd = vjp(da + db)
        return dy0, du, dd

    dy0, du_skip, ddskip = _ew(tag + "y1", y1_b, SEQ, 256,
                               [_ri(dy1a), _ri(dy1b), _ri(r["y0p"]), _ri(proj, SW, CB_U)], [dskip],
                               [(SW, BF16), (SW, F32)], [SW])
    dlr, dli, dwdt, dcmt, du_s = _s5_backward(tag + "s5", dy0, proj, r["s"], r["wdt"], r["cmt"],
                                              r["ar"].reshape(1, NS), r["ai"].reshape(1, NS))
    (du,) = _ew(tag + "du", lambda a, c: (a + c,), SEQ, 256, [_ri(du_s), _ri(du_skip)], [], [(SW, BF16)])

    dq, dk, dv = ([dqkv[3 * g + i] for g in range(3)] for i in range(3))
    dproj = jnp.concatenate([du, dzs, *dq, *dk, *dv, dza, dgs, dga], axis=1)
    dw_in = _mm(tag + "dw_in", r["hv"], dproj, "tn", DM, NCOL, SEQ, 1024, 512, SEQ, BF16,
                a_spec=pl.BlockSpec((None, SEQ, 1024), lambda i, j, k: (_row_order(j), 0, 0)),
                o_spec=pl.BlockSpec((None, 1024, 512), lambda i, j, k: (j // 2, 0, j % 2)), out_shape=(NDEV, DM, DM))
    tok = big_done(l, dict(w_in=dw_in, w_glu=dw_glu.reshape(NDEV, SW // NDEV, SW),
                           w_branch_s=dw_bs.reshape(SW, NDEV, DM // NDEV).transpose(1, 0, 2),
                           w_branch_a=dw_ba.reshape(AW, NDEV, DM // NDEV).transpose(1, 0, 2),
                           w_out=dw_out.reshape(NDEV, DM // NDEV, DM)))
    if tok is not None:
        g1 = g1 + tok[0, 0]

    dwdt = dwdt.reshape(2 * NS, SW)
    dcmt = dcmt.reshape(2 * SW, NS)
    dbbr = _extract(tag + "dbbr", dwdt, NST, GCH, (0, 0), after=tok)
    dbbi = _extract(tag + "dbbi", dwdt, NST, GCH, (1, 0), after=tok)
    bre = sp["b_re"].reshape(NS, GCH)
    bim = sp["b_im"].reshape(NS, GCH)

    def bbar_b(cr, ci, br_, bi_, dr, di):
        _, vjp = jax.vjp(_bbar_fn, cr, ci, br_, bi_)
        return vjp((dr, di))

    dcr, dci, dbre, dbim = _ew(tag + "bbar", bbar_b, NS, NS,
                               [_ri(r["cr"].reshape(NS, 1)), _ri(r["ci"].reshape(NS, 1)), _ri(bre), _ri(bim),
                                _ri(dbbr), _ri(dbbi)], [], [(1, F32), (1, F32), (GCH, F32), (GCH, F32)])

    def lam_b(lre, lim, ldt, dar, dai, dcr_, dci_):
        _, vjp = jax.vjp(_lam_fn, lre, lim, ldt)
        return vjp((dar, dai, dcr_, dci_))

    dlre, dlim, dldt = _ew(tag + "lam", lam_b, NGRP, NGRP,
                           [_ri(sp["lambda_re"]), _ri(sp["lambda_im"]), _ri(sp["log_dt"].reshape(NGRP, 1)),
                            _ri(dlr.reshape(NGRP, NST)), _ri(dli.reshape(NGRP, NST)),
                            _ri(dcr.reshape(NGRP, NST)), _ri(dci.reshape(NGRP, NST))], [],
                           [(NST, F32), (NST, F32), (1, F32)])
    dc_re = _extract(tag + "dc_re", dcmt, GCH, NST, (0, 0), after=tok).reshape(NGRP, GCH, NST)
    dc_im = -_extract(tag + "dc_im", dcmt, GCH, NST, (1, 0), after=tok).reshape(NGRP, GCH, NST)

    dh_time = _mm(tag + "dh_time", dproj, w["w_in"], "nt", SEQ, DM, NCOL - DM, SEQ, 1024, 1024, F32,
                  a_spec=pl.BlockSpec((SEQ, 1024), lambda i, j, k: (0, 1 + k)),
                  b_spec=pl.BlockSpec((None, 1024, 1024), lambda i, j, k: (1 + k, 0, 0)), after=tok)
    dh_chunked = _mm(tag + "dh_chunked", dproj, w["w_in"], "nt", SEQ, DM, DM, SEQ, 1024, 1024, F32,
                     a_spec=pl.BlockSpec((SEQ, 1024), lambda i, j, k: (0, 0)),
                     b_spec=pl.BlockSpec((None, 1024, 1024), lambda i, j, k: (0, 0, 0)), after=tok)
    dh = [dh_time, _from_chunked(dh_chunked)]

    def pre_b(d, dh0, dh1, x_, g):
        _, vjp = jax.vjp(_rms, x_, g)
        dx_, dg = vjp(dh0 + dh1)
        return d + dx_, dg

    dx, dg1 = _ew(tag + "pre", pre_b, SEQ, 256, [_ri(dxn)] + [_ri(t_) for t_ in dh] + [_ri(r["x"])], [g1],
                  [(DM, F32)], [DM])

    small = dict(pre_norm_g=dg1.reshape(DM), lambda_re=dlre, lambda_im=dlim, log_dt=dldt.reshape(NGRP),
                 b_re=dbre.reshape(NGRP, NST, GCH), b_im=dbim.reshape(NGRP, NST, GCH), c_re=dc_re, c_im=dc_im,
                 d_skip=ddskip.reshape(SW), b_glu=dbglu.reshape(SW), post_norm_g=dg2.reshape(DM))
    return dx, small


_HBM = pl.BlockSpec(memory_space=pltpu.HBM)
_SEM = pl.BlockSpec(memory_space=pltpu.SEMAPHORE)
_EFFECT = pltpu.SideEffectType.DATAFLOW_SIDE_EFFECTING


def _remote_copies(srcs, dsts, send_sems, recv_sems, gather):
    x, y, c = lax.axis_index("x"), lax.axis_index("y"), lax.axis_index("c")
    me = 4 * x + 2 * y + c
    copies = []
    for i in range(len(srcs)):
        for k in range(1, NDEV):
            peer = (x ^ (k >> 2), y ^ ((k >> 1) & 1), c ^ (k & 1))
            src = srcs[i] if gather[i] else srcs[i].at[me ^ k]
            copies.append(pltpu.make_async_remote_copy(
                src_ref=src, dst_ref=dsts[i].at[me], send_sem=send_sems[i], recv_sem=recv_sems[i],
                device_id=peer, device_id_type=pl.DeviceIdType.MESH))
    return copies


def _all_seven(dst, send_sem, recv_sem):
    seven = dst.at[pl.ds(0, NDEV - 1)]
    me = (lax.axis_index("x"), lax.axis_index("y"), lax.axis_index("c"))
    return pltpu.make_async_remote_copy(src_ref=seven, dst_ref=seven, send_sem=send_sem, recv_sem=recv_sem,
                                        device_id=me, device_id_type=pl.DeviceIdType.MESH)


def _own_slabs(name, arrs, gather, after):
    n = len(arrs)
    me = (4 * lax.axis_index("x") + 2 * lax.axis_index("y") + lax.axis_index("c")).astype(jnp.int32).reshape(1)

    def body(me_ref, *refs):
        for src, dst in zip(refs[:n], refs[n + 1:]):
            dst[...] = src[...]

    def zeros(k):
        return (0,) * k

    in_specs, out_specs, out_shape = [], [], []
    for a, g in zip(arrs, gather):
        slab = a.shape if g else a.shape[1:]
        nd = len(slab)
        if g:
            in_specs.append(pl.BlockSpec(slab, functools.partial(lambda i, me_ref, nd: zeros(nd), nd=nd)))
        else:
            in_specs.append(pl.BlockSpec((None,) + slab, functools.partial(lambda i, me_ref, nd: (me_ref[0],) + zeros(nd), nd=nd)))
        out_specs.append(pl.BlockSpec((None,) + slab, functools.partial(lambda i, me_ref, nd: (me_ref[0],) + zeros(nd), nd=nd)))
        out_shape.append(jax.ShapeDtypeStruct((NDEV,) + slab, a.dtype))
    in_specs.append(pl.BlockSpec(memory_space=pl.ANY))
    return pl.pallas_call(
        body, name=name, out_shape=out_shape,
        grid_spec=pltpu.PrefetchScalarGridSpec(num_scalar_prefetch=1, grid=(1,), in_specs=in_specs, out_specs=out_specs),
        compiler_params=_params("arbitrary"),
    )(me, *arrs, after)


def _exchange_start(name, arrs, gather, after):
    n = len(arrs)
    lands = _own_slabs(name + "_own", arrs, gather, after)

    def body(*refs):
        srcs, dsts = refs[:n], refs[n:2 * n]
        send_sems, recv_sems = refs[2 * n:3 * n], refs[3 * n:4 * n]
        token = refs[-1]
        for cp in _remote_copies(srcs, dsts, send_sems, recv_sems, gather):
            cp.start()
        token[...] = jnp.zeros(token.shape, token.dtype)

    thru = [pltpu.HBM(a.shape, a.dtype) for a in list(arrs) + list(lands)]
    outs = pl.pallas_call(
        body, name=name,
        out_shape=(*[pltpu.SemaphoreType.DMA(())] * (2 * n), *thru, jax.ShapeDtypeStruct((8, 128), F32)),
        in_specs=[_HBM] * (2 * n),
        out_specs=(*[_SEM] * (2 * n), *[_HBM] * (2 * n), pl.BlockSpec(memory_space=pltpu.VMEM)),
        input_output_aliases={i: 2 * n + i for i in range(2 * n)},
        compiler_params=pltpu.CompilerParams(has_side_effects=_EFFECT),
    )(*[pltpu.with_memory_space_constraint(a, pltpu.HBM) for a in list(arrs) + list(lands)])
    return dict(send=outs[:n], recv=outs[n:2 * n], srcs=outs[2 * n:3 * n], lands=outs[3 * n:4 * n], token=outs[-1],
                gather=gather)


def _exchange_wait(name, started, after):
    n = len(started["srcs"])
    after = list(after)

    def body(*refs):
        dsts = refs[n:2 * n]
        send_sems, recv_sems = refs[2 * n:3 * n], refs[3 * n:4 * n]
        for i in range(n):
            cp = _all_seven(dsts[i], send_sems[i], recv_sems[i])
            cp.wait_send()
            cp.wait_recv()

    bufs = list(started["srcs"]) + list(started["lands"])
    outs = pl.pallas_call(
        body, name=name, out_shape=tuple(pltpu.HBM(a.shape, a.dtype) for a in bufs),
        in_specs=[_HBM] * (2 * n) + [_SEM] * (2 * n) + [pl.BlockSpec(memory_space=pl.ANY)] * len(after),
        out_specs=(_HBM,) * (2 * n), input_output_aliases={i: i for i in range(2 * n)},
        compiler_params=pltpu.CompilerParams(has_side_effects=_EFFECT),
    )(*bufs, *started["send"], *started["recv"], *after)
    return outs[n:]


def _sum_in_order(parts):
    g = parts[0].astype(F32)
    for p in parts[1:]:
        g = g + p.astype(F32)
    return g


def _adam_update(g, w_, m_, v_):
    m2 = B1 * m_ + (1.0 - B1) * g
    v2 = B2 * v_ + (1.0 - B2) * (g * g)
    m_hat = m2 / (1.0 - B1 ** STEP)
    v_hat = v2 / (1.0 - B2 ** STEP)
    delta = -LR * (m_hat / (jnp.sqrt(v_hat) + ADAM_EPS) + WD * w_)
    return g, delta, m2, v2


def _adamw(name, g, w, m, v, br):
    rows, cols = w.shape
    return _ew(name, _adam_update, rows, br, [_ri(g), _ri(w), _ri(m), _ri(v)], [], [(cols, F32)] * 4)


def _adamw_layer(name, l, parts, w, m, v, br, outs):
    rows, cols = w.shape
    nb = rows // DEPTH // br
    assert nb * br * DEPTH == rows

    def body(*refs):
        vals = _adam_update(_sum_in_order([r[...] for r in refs[:NDEV]]), *[r[...] for r in refs[NDEV:NDEV + 3]])
        for r, val in zip(refs[-4:], vals):
            r[...] = val

    mine = pl.BlockSpec((br, cols), lambda i: (l * nb + i, 0))
    in_specs = [pl.BlockSpec((br, cols), functools.partial(lambda i, d: (d * nb + i, 0), d=d)) for d in range(NDEV)]
    in_specs += [mine] * 3 + [pl.BlockSpec(memory_space=pl.ANY)] * len(outs)
    return pl.pallas_call(
        body, name=f"{name}{l}", grid=(nb,), in_specs=in_specs, out_specs=[mine] * 4,
        out_shape=[jax.ShapeDtypeStruct((rows, cols), F32)] * 4,
        input_output_aliases={NDEV + 3 + q: q for q in range(len(outs))},
        compiler_params=_params("arbitrary"),
    )(*([parts] * NDEV), w, m, v, *outs)


SMALL = ("pre_norm_g", "lambda_re", "lambda_im", "log_dt", "b_re", "b_im", "c_re", "c_im", "d_skip", "b_glu",
         "post_norm_g")
BIG = ("w_in", "w_glu", "w_branch_s", "w_branch_a", "w_out")
WEIGHTS = ("pre_norm_g", "w_in", "lambda_re", "lambda_im", "log_dt", "b_re", "b_im", "c_re", "c_im", "d_skip",
           "w_glu", "b_glu", "w_branch_s", "w_branch_a", "w_out", "post_norm_g")
PACK_COLS = 1024
PACK_BR = 136


def _pack_layer(d):
    pieces = [d[k].astype(F32).reshape(-1) for k in SMALL]
    used = sum(p.shape[0] for p in pieces)
    assert used <= PACK_BR * PACK_COLS
    return jnp.concatenate(pieces + [jnp.zeros((PACK_BR * PACK_COLS - used,), F32)]).reshape(PACK_BR, PACK_COLS)


def _unpack(p, like):
    flat = p.reshape(DEPTH, PACK_BR * PACK_COLS)
    out, off = {}, 0
    for k in SMALL:
        n = like[k].size // DEPTH
        out[k] = flat[:, off:off + n].reshape(like[k].shape)
        off += n
    return out


def _local_step(x, target, small, started, weights_of, big_done, small_done, total_loss):
    preps = [_s5_prepare(l, {k: small[k][l] for k in SMALL}, started) for l in range(DEPTH)]
    res, ws = [], []
    for l in range(DEPTH):
        w_l, tok = weights_of(l, [x] + ([p[k] for p in preps for k in ("wdt", "cmt")] if l == 0 else []))
        sp = {k: small[k][l] for k in SMALL}
        if tok is not None:
            sp["pre_norm_g"] = sp["pre_norm_g"] + tok[0, 0]
        x, r = _layer_fwd(l, x, w_l, sp, preps[l])
        res.append(r)
        ws.append(w_l)

    def loss_fn(y, t):
        e = y - t
        return e * (1.0 / DM), jnp.sum(_colsum(0.5 * e * e * (1.0 / DM)), axis=1, keepdims=True)

    dx, loss = _ew("loss", loss_fn, SEQ, 256, [_ri(x), _ri(target)], [], [(DM, F32)], [1])
    total = total_loss(loss.reshape(()))
    for l in reversed(range(DEPTH)):
        dx, sm = _layer_bwd(l, dx, res[l], ws[l], {k: small[k][l] for k in SMALL}, big_done,
                            after=total.reshape(1, 1) if l == DEPTH - 1 else None)
        small_done(l, sm)
    return total, dx


def _full_weights(gathered):
    g = gathered
    return dict(
        w_in=g["w_in"],
        w_glu=g["w_glu"].reshape(SW, SW),
        w_branch_s=g["w_branch_s"].transpose(1, 0, 2).reshape(SW, DM),
        w_branch_a=g["w_branch_a"].transpose(1, 0, 2).reshape(AW, DM),
        w_out=g["w_out"].reshape(DM, DM),
    )


def kernel(x, pre_norm_g, w_in, lambda_re, lambda_im, log_dt, b_re, b_im, c_re, c_im, d_skip, w_glu, b_glu, w_branch_s, w_branch_a, w_out, post_norm_g, loss_target, m_pre_norm_g, m_w_in, m_lambda_re, m_lambda_im, m_log_dt, m_b_re, m_b_im, m_c_re, m_c_im, m_d_skip, m_w_glu, m_b_glu, m_w_branch_s, m_w_branch_a, m_w_out, m_post_norm_g, v_pre_norm_g, v_w_in, v_lambda_re, v_lambda_im, v_log_dt, v_b_re, v_b_im, v_c_re, v_c_im, v_d_skip, v_w_glu, v_b_glu, v_w_branch_s, v_w_branch_a, v_w_out, v_post_norm_g):
    wts = dict(pre_norm_g=pre_norm_g, w_in=w_in, lambda_re=lambda_re, lambda_im=lambda_im, log_dt=log_dt, b_re=b_re,
               b_im=b_im, c_re=c_re, c_im=c_im, d_skip=d_skip, w_glu=w_glu, b_glu=b_glu, w_branch_s=w_branch_s,
               w_branch_a=w_branch_a, w_out=w_out, post_norm_g=post_norm_g)
    mom = dict(pre_norm_g=m_pre_norm_g, w_in=m_w_in, lambda_re=m_lambda_re, lambda_im=m_lambda_im, log_dt=m_log_dt,
               b_re=m_b_re, b_im=m_b_im, c_re=m_c_re, c_im=m_c_im, d_skip=m_d_skip, w_glu=m_w_glu, b_glu=m_b_glu,
               w_branch_s=m_w_branch_s, w_branch_a=m_w_branch_a, w_out=m_w_out, post_norm_g=m_post_norm_g)
    var = dict(pre_norm_g=v_pre_norm_g, w_in=v_w_in, lambda_re=v_lambda_re, lambda_im=v_lambda_im, log_dt=v_log_dt,
               b_re=v_b_re, b_im=v_b_im, c_re=v_c_re, c_im=v_c_im, d_skip=v_d_skip, w_glu=v_w_glu, b_glu=v_b_glu,
               w_branch_s=v_w_branch_s, w_branch_a=v_w_branch_a, w_out=v_w_out, post_norm_g=v_post_norm_g)

    def gather_start(l, after):
        return _exchange_start(f"gather_start{l}", [wts[k][l].astype(BF16) for k in BIG], [True] * len(BIG), after)

    gathering = {0: gather_start(0, x)}
    sending, packed = {}, {}

    def weights_of(l, after):
        gathered = _exchange_wait(f"gather_wait{l}", gathering[l], after)
        tok = None
        if l + 1 < DEPTH:
            gathering[l + 1] = gather_start(l + 1, gathered[0])
            tok = gathering[l + 1]["token"]
        return _full_weights(dict(zip(BIG, gathered))), tok

    def big_done(l, big):
        arrs, kinds = [big[k] for k in BIG], [False] * len(BIG)
        if l + 1 < DEPTH:
            arrs, kinds = arrs + [packed[l + 1]], kinds + [True]
        sending[l] = _exchange_start(f"grads_start{l}", arrs, kinds, big["w_in"])
        return sending[l]["token"]

    def small_done(l, sm):
        packed[l] = _pack_layer(sm)

    loss, dx = _local_step(x[0], loss_target[0], wts, gathering[0]["token"], weights_of, big_done, small_done,
                           lambda part: lax.psum(part, ("x", "y", "c")))
    last = _exchange_start("grads_start_last", [packed[0]], [True], dx)

    grads, delta, new_m, new_v = {}, {}, {}, {}

    def as_rows(k):
        cols = wts[k].shape[-1]
        rows = wts[k].size // cols
        return rows, cols, [t[k].reshape(rows, cols) for t in (wts, mom, var)]

    recv_l, outs, after = {}, {k: [] for k in BIG}, [dx]
    for l in reversed(range(DEPTH)):
        recv_l[l] = _exchange_wait(f"grads_wait{l}", sending[l], after)
        for i, k in enumerate(BIG):
            rows, cols, wmv = as_rows(k)
            per_layer = rows // DEPTH
            outs[k] = _adamw_layer("adamw_" + k, l, recv_l[l][i].reshape(NDEV * per_layer, cols), *wmv,
                                   min(per_layer, 256), outs[k])
        after = [outs[k][0] for k in BIG]
    for k in BIG:
        grads[k], delta[k], new_m[k], new_v[k] = (o.reshape(wts[k].shape) for o in outs[k])

    def update(k, g):
        rows, cols, wmv = as_rows(k)
        res = _adamw("adamw_" + k, g.reshape(rows, cols), *wmv, min(rows, 1024 if cols <= 128 else 256))
        grads[k], delta[k], new_m[k], new_v[k] = (o.reshape(wts[k].shape) for o in res)

    recv_last = _exchange_wait("grads_wait_last", last, after)
    recv_small = jnp.concatenate([recv_last[0]] + [recv_l[l][len(BIG)] for l in range(DEPTH - 1)], axis=1)
    rows = DEPTH * PACK_BR
    (g_small,) = _ew("grads_small", lambda *p: (_sum_in_order(p),), rows, PACK_BR,
                     [_ri(recv_small.reshape(NDEV * rows, PACK_COLS), PACK_COLS, 0, d * DEPTH) for d in range(NDEV)], [],
                     [(PACK_COLS, F32)])
    for k, g in _unpack(g_small, wts).items():
        update(k, g)

    return (loss, dx[None], *[grads[k] for k in WEIGHTS], *[delta[k] for k in WEIGHTS],
            *[new_m[k] for k in WEIGHTS], *[new_v[k] for k in WEIGHTS])
```

```python
import functools
import math

import jax
import jax.numpy as jnp
from jax import lax
from jax.experimental import pallas as pl
from jax.experimental.pallas import tpu as pltpu

F32 = jnp.float32
BF16 = jnp.bfloat16

NDEV = 8
DEPTH = 4
SEQ = 2048
DM = 1024
NCOL = 8192
SW = 512
NGRP = 32
GCH = 16
NST = 64
NS = NGRP * NST
HD = 128
AW = 512
DILATIONS = (1, 4, 16)
ABLK = 128
ATTN_PAIR = 4
RMS_EPS = 1e-6
LR, B1, B2, ADAM_EPS, WD, STEP = 0.001, 0.9, 0.999, 1e-08, 0.01, 10

CB_U, CB_ZS, CB_Q, CB_K, CB_V, CB_ZA = 0, 1, 2, 5, 8, 11
CB_GS, CB_GA = 6, 7

VMEM_LIMIT = 56 * 2 ** 20


def _row_order(j):
    return jnp.where(j < CB_Q, 1, 0)


def _params(*sem):
    return pltpu.CompilerParams(dimension_semantics=sem, vmem_limit_bytes=VMEM_LIMIT)


def _ew(name, fn, rows, br, row_ins, bc_ins, row_outs, red_outs=()):
    n_in = len(row_ins) + len(bc_ins)
    n_ro = len(row_outs)
    steps = rows // br
    assert steps * br == rows

    def body(*refs):
        vals = fn(*[r[...] for r in refs[:n_in]])
        outs = refs[n_in:]
        for r, v in zip(outs[:n_ro], vals[:n_ro]):
            r[...] = v.astype(r.dtype)
        if red_outs:
            @pl.when(pl.program_id(0) == 0)
            def _():
                for r in outs[n_ro:]:
                    r[...] = jnp.zeros(r.shape, r.dtype)
            for r, v in zip(outs[n_ro:], vals[n_ro:]):
                r[...] += v

    in_specs = []
    for (_, w, cb, rb) in row_ins:
        in_specs.append(pl.BlockSpec((br, w), functools.partial(lambda i, cb, rb: (rb + i, cb), cb=cb, rb=rb)))
    for a in bc_ins:
        in_specs.append(pl.BlockSpec(a.shape, functools.partial(lambda i, nd: (0,) * nd, nd=a.ndim)))
    out_specs = [pl.BlockSpec((br, w), lambda i: (i, 0)) for (w, _) in row_outs]
    out_specs += [pl.BlockSpec((1, w), lambda i: (0, 0)) for w in red_outs]
    out_shape = [jax.ShapeDtypeStruct((rows, w), dt) for (w, dt) in row_outs]
    out_shape += [jax.ShapeDtypeStruct((1, w), F32) for w in red_outs]
    return pl.pallas_call(
        body, name=name, grid=(steps,), in_specs=in_specs, out_specs=out_specs, out_shape=out_shape,
        compiler_params=_params("arbitrary"),
    )(*[a for (a, _, _, _) in row_ins], *bc_ins)


def _ri(a, w=None, cb=0, rb=0):
    return (a, a.shape[1] if w is None else w, cb, rb)


_DIMS = {"nn": ((1,), (0,)), "nt": ((1,), (1,)), "tn": ((0,), (0,))}


def _after(after):
    return ([pl.BlockSpec(memory_space=pl.ANY)], [after]) if after is not None else ([], [])


def _mm(name, a, b, mode, M, N, K, bm, bn, bk, out_dtype, a_spec=None, b_spec=None, o_spec=None, out_shape=None,
        after=None):
    nk = K // bk
    assert M % bm == 0 and N % bn == 0 and nk * bk == K
    after_specs, after_args = _after(after)

    own_acc = nk > 1 and out_dtype != F32

    def body(a_ref, b_ref, *rest):
        o_ref, scratch = rest[len(after_args)], rest[len(after_args) + 1:]
        part = lax.dot_general(a_ref[...].astype(BF16), b_ref[...].astype(BF16), (_DIMS[mode], ((), ())),
                               preferred_element_type=F32)
        if nk == 1:
            o_ref[...] = part.astype(o_ref.dtype)
            return
        k = pl.program_id(2)
        acc_ref = scratch[0] if own_acc else o_ref

        @pl.when(k == 0)
        def _():
            acc_ref[...] = part

        @pl.when(k > 0)
        def _():
            acc_ref[...] += part

        if own_acc:
            @pl.when(k == nk - 1)
            def _():
                o_ref[...] = acc_ref[...].astype(o_ref.dtype)

    if a_spec is None:
        a_spec = (pl.BlockSpec((bk, bm), lambda i, j, k: (k, i)) if mode == "tn"
                  else pl.BlockSpec((bm, bk), lambda i, j, k: (i, k)))
    if b_spec is None:
        b_spec = (pl.BlockSpec((bn, bk), lambda i, j, k: (j, k)) if mode == "nt"
                  else pl.BlockSpec((bk, bn), lambda i, j, k: (k, j)))
    if o_spec is None:
        o_spec = pl.BlockSpec((bm, bn), lambda i, j, k: (i, j))
    if out_shape is None:
        out_shape = (M, N)
    return pl.pallas_call(
        body, name=name, grid=(M // bm, N // bn, nk), in_specs=[a_spec, b_spec] + after_specs, out_specs=o_spec,
        out_shape=jax.ShapeDtypeStruct(out_shape, out_dtype),
        scratch_shapes=[pltpu.VMEM((bm, bn), F32)] if own_acc else [],
        compiler_params=_params("parallel", "parallel", "arbitrary"),
    )(a, b, *after_args)


SCAN_LANES = 512
SCAN_CHUNKS = 8


def _to_chunked(a):
    return a.reshape(SCAN_CHUNKS, SEQ // SCAN_CHUNKS, -1).transpose(1, 0, 2).reshape(SEQ, -1)


def _from_chunked(a):
    return a.reshape(SEQ // SCAN_CHUNKS, SCAN_CHUNKS, -1).transpose(1, 0, 2).reshape(SEQ, -1)


def _scan_block(dr_ref, di_ref, sr_ref, si_ref, lam_r, lam_i, reverse):
    T = SEQ // SCAN_CHUNKS
    bl = lam_r.shape[1]
    assert T == 2 ** 8
    ar = jnp.broadcast_to(lam_r, (SCAN_CHUNKS, bl))
    ai = jnp.broadcast_to(lam_i, (SCAN_CHUNKS, bl))
    zero = jnp.zeros((SCAN_CHUNKS, bl), F32)

    def tile(j):
        return pl.ds(pl.multiple_of(j * SCAN_CHUNKS, SCAN_CHUNKS), SCAN_CHUNKS)

    def step(jj, carry):
        sr, si = carry
        j = T - 1 - jj if reverse else jj
        nr = ar * sr - ai * si + dr_ref[tile(j), :]
        ni = ar * si + ai * sr + di_ref[tile(j), :]
        sr_ref[tile(j), :] = nr
        si_ref[tile(j), :] = ni
        return nr, ni

    er, ei = lax.fori_loop(0, T, step, (zero, zero), unroll=4)

    pr, pi = ar[0:1], ai[0:1]
    for _ in range(8):
        pr, pi = pr * pr - pi * pi, 2.0 * pr * pi
    rows = lax.broadcasted_iota(jnp.int32, (SCAN_CHUNKS, bl), 0)
    cr, ci = zero, zero
    xr = jnp.zeros((1, bl), F32)
    xi = jnp.zeros((1, bl), F32)
    order = range(SCAN_CHUNKS - 2, -1, -1) if reverse else range(1, SCAN_CHUNKS)
    for c in order:
        src = c + 1 if reverse else c - 1
        nxr = pr * xr - pi * xi + er[src:src + 1]
        nxi = pr * xi + pi * xr + ei[src:src + 1]
        xr, xi = nxr, nxi
        cr = jnp.where(rows == c, xr, cr)
        ci = jnp.where(rows == c, xi, ci)

    def fix(jj, pw):
        pwr, pwi = pw
        j = T - 1 - jj if reverse else jj
        sr_ref[tile(j), :] = sr_ref[tile(j), :] + (pwr * cr - pwi * ci)
        si_ref[tile(j), :] = si_ref[tile(j), :] + (pwr * ci + pwi * cr)
        return pwr * ar - pwi * ai, pwr * ai + pwi * ar

    lax.fori_loop(0, T, fix, (ar, ai), unroll=4)


def _s5_forward(name, proj, wdt, cmt, lam_r, lam_i):
    bl = SCAN_LANES
    nblk = NS // bl
    cw = bl * GCH // NST
    assert cw == 128 and CB_U == 0

    def nt(a, b_):
        return lax.dot_general(a, b_, (((1,), (1,)), ((), ())), preferred_element_type=F32)

    def body(u_ref, wr_ref, wi_ref, cr_ref, ci_ref, ar_ref, ai_ref, s_ref, y_ref):
        sr_ref, si_ref = s_ref.at[0], s_ref.at[1]
        u = u_ref[...].astype(BF16)
        sr_ref[...] = nt(u, wr_ref[...])
        si_ref[...] = nt(u, wi_ref[...])
        _scan_block(sr_ref, si_ref, sr_ref, si_ref, ar_ref[...], ai_ref[...], False)
        y_ref[...] = nt(sr_ref[...].astype(BF16), cr_ref[...]) + nt(si_ref[...].astype(BF16), ci_ref[...])

    return pl.pallas_call(
        body, name=name, grid=(nblk,),
        in_specs=[pl.BlockSpec((SEQ, cw), lambda i: (0, i)),
                  pl.BlockSpec((bl, cw), lambda i: (i, i)), pl.BlockSpec((bl, cw), lambda i: (nblk + i, i)),
                  pl.BlockSpec((cw, bl), lambda i: (i, i)), pl.BlockSpec((cw, bl), lambda i: (SW // cw + i, i)),
                  pl.BlockSpec((1, bl), lambda i: (0, i)), pl.BlockSpec((1, bl), lambda i: (0, i))],
        out_specs=[pl.BlockSpec((2, SEQ, bl), lambda i: (0, 0, i)), pl.BlockSpec((SEQ, cw), lambda i: (0, i))],
        out_shape=[jax.ShapeDtypeStruct((2, SEQ, NS), F32), jax.ShapeDtypeStruct((SEQ, SW), F32)],
        compiler_params=_params("arbitrary"),
    )(proj, wdt, wdt, cmt, cmt, lam_r, lam_i)


S5_BWD_LANES = 512


def _s5_backward(name, dy0, proj, s, wdt, cmt, lam_r, lam_i):
    bl = S5_BWD_LANES
    nblk = NS // bl
    cw = 128
    per = cw // (bl * GCH // NST)
    assert per >= 1 and CB_U == 0

    def tn(a, b_):
        return lax.dot_general(a, b_, (((0,), (0,)), ((), ())), preferred_element_type=F32)

    def prev(s_ref):
        last = pltpu.roll(s_ref[SEQ - SCAN_CHUNKS:SEQ, :], 1, 0)
        first = jnp.where(lax.broadcasted_iota(jnp.int32, (SCAN_CHUNKS, bl), 0) > 0, last, 0.0)
        return jnp.concatenate([first, s_ref[0:SEQ - SCAN_CHUNKS, :]], axis=0)

    def body(dy_ref, u_ref, s_ref, cr_ref, ci_ref, wr_ref, wi_ref, lr_ref, li_ref,
             dlr_ref, dli_ref, dwdt_ref, dcmt_ref, du_ref, ar_ref, ai_ref):
        dy = dy_ref[...]
        ar_ref[...] = jnp.dot(dy, cr_ref[...], preferred_element_type=F32)
        ai_ref[...] = jnp.dot(dy, ci_ref[...], preferred_element_type=F32)
        _scan_block(ar_ref, ai_ref, ar_ref, ai_ref, lr_ref[...], -li_ref[...], True)
        a_r, a_i = ar_ref[...], ai_ref[...]
        sr_ref, si_ref = s_ref.at[0], s_ref.at[1]
        spr, spi = prev(sr_ref), prev(si_ref)
        dlr_ref[...] = jnp.sum(a_r * spr + a_i * spi, axis=0, keepdims=True)
        dli_ref[...] = jnp.sum(a_i * spr - a_r * spi, axis=0, keepdims=True)
        a_rb, a_ib = a_r.astype(BF16), a_i.astype(BF16)
        u = u_ref[...].astype(BF16)
        dwdt_ref[0] = tn(a_rb, u)
        dwdt_ref[1] = tn(a_ib, u)
        dcmt_ref[0] = tn(dy, sr_ref[...].astype(BF16))
        dcmt_ref[1] = tn(dy, si_ref[...].astype(BF16))
        part = (jnp.dot(a_rb, wr_ref[...], preferred_element_type=F32)
                + jnp.dot(a_ib, wi_ref[...], preferred_element_type=F32))

        @pl.when(pl.program_id(0) % per == 0)
        def _():
            du_ref[...] = part

        @pl.when(pl.program_id(0) % per > 0)
        def _():
            du_ref[...] += part

    lam_spec = pl.BlockSpec((1, bl), lambda i: (0, i))
    return pl.pallas_call(
        body, name=name, grid=(nblk,),
        in_specs=[pl.BlockSpec((SEQ, cw), lambda i: (0, i // per)), pl.BlockSpec((SEQ, cw), lambda i: (0, i // per)),
                  pl.BlockSpec((2, SEQ, bl), lambda i: (0, 0, i)),
                  pl.BlockSpec((cw, bl), lambda i: (i // per, i)),
                  pl.BlockSpec((cw, bl), lambda i: (SW // cw + i // per, i)),
                  pl.BlockSpec((bl, cw), lambda i: (i, i // per)), pl.BlockSpec((bl, cw), lambda i: (nblk + i, i // per)),
                  lam_spec, lam_spec],
        out_specs=[lam_spec, lam_spec, pl.BlockSpec((2, bl, cw), lambda i: (0, i, i // per)),
                   pl.BlockSpec((2, cw, bl), lambda i: (0, i // per, i)),
                   pl.BlockSpec((SEQ, cw), lambda i: (0, i // per))],
        out_shape=[jax.ShapeDtypeStruct((1, NS), F32), jax.ShapeDtypeStruct((1, NS), F32),
                   jax.ShapeDtypeStruct((2, NS, SW), F32), jax.ShapeDtypeStruct((2, SW, NS), F32),
                   jax.ShapeDtypeStruct((SEQ, SW), F32)],
        scratch_shapes=[pltpu.VMEM((SEQ, bl), F32)] * 2,
        compiler_params=_params("arbitrary"),
    )(dy0, proj, s, cmt, cmt, wdt, wdt, lam_r, lam_i)


def _scores(qb, kb, prev):
    s = lax.dot_general(qb, kb, (((1,), (1,)), ((), ())), preferred_element_type=F32) * (HD ** -0.5)
    row = lax.broadcasted_iota(jnp.int32, (ABLK, ABLK), 0)
    col = lax.broadcasted_iota(jnp.int32, (ABLK, ABLK), 1)
    return jnp.where((col >= row) if prev else (col <= row), s, -1e30)


def _block_rows(dil, r, b):
    if dil == 1:
        return pl.ds(pl.multiple_of(b * ABLK, ABLK), ABLK)
    return pl.ds(r + dil * ABLK * b, ABLK, stride=dil)


def _group_blocks(dil):
    nb = SEQ // dil // ABLK
    shift = nb.bit_length() - 1
    return nb, (lambda idx: (idx >> shift, idx & (nb - 1)))


def _qkv_specs(j_of):
    return [pl.BlockSpec((SEQ, HD), functools.partial(lambda j, c: (0, c + j_of(j)), c=(cb + g) * 4))
            for g in range(3) for cb in (CB_Q, CB_K, CB_V)]


def _attention_fwd(name, proj):
    def body(*refs):
        qkv, z_ref = refs[:9], refs[9]
        y_ref, ya_ref, l_ref = refs[10:13]
        accs, maxs, dens = refs[13:16], refs[16:19], refs[19:22]
        for g, dil in enumerate(DILATIONS):
            q_ref, k_ref, v_ref = qkv[3 * g:3 * g + 3]
            nb, where = _group_blocks(dil)

            def step(t, c, g=g, dil=dil, nb=nb, where=where, q_ref=q_ref, k_ref=k_ref, v_ref=v_ref):
                two = range(ATTN_PAIR)
                rb = [where(t + i * (SEQ // ABLK // ATTN_PAIR)) for i in two]
                rows = [_block_rows(dil, r, b) for r, b in rb]
                qb = [q_ref[rows[i], :].astype(BF16) for i in two]
                s_c = [_scores(qb[i], k_ref[rows[i], :].astype(BF16), False) for i in two]
                if nb > 1:
                    prev = [_block_rows(dil, r, jnp.maximum(b - 1, 0)) for r, b in rb]
                    s_p = [jnp.where(rb[i][1] > 0, _scores(qb[i], k_ref[prev[i], :].astype(BF16), True), -1e30)
                           for i in two]
                m = [jnp.max(s_c[i], axis=-1, keepdims=True) for i in two]
                if nb > 1:
                    m = [jnp.maximum(m[i], jnp.max(s_p[i], axis=-1, keepdims=True)) for i in two]
                p_c = [jnp.exp(s_c[i] - m[i]) for i in two]
                den = [jnp.sum(p_c[i], axis=-1, keepdims=True) for i in two]
                acc = [jnp.dot(p_c[i].astype(BF16), v_ref[rows[i], :].astype(BF16), preferred_element_type=F32)
                       for i in two]
                if nb > 1:
                    p_p = [jnp.exp(s_p[i] - m[i]) for i in two]
                    den = [den[i] + jnp.sum(p_p[i], axis=-1, keepdims=True) for i in two]
                    acc = [acc[i] + jnp.dot(p_p[i].astype(BF16), v_ref[prev[i], :].astype(BF16),
                                            preferred_element_type=F32) for i in two]
                for i in two:
                    accs[g][rows[i], :] = acc[i]
                    maxs[g][rows[i], :] = jnp.broadcast_to(m[i], (ABLK, HD))
                    dens[g][rows[i], :] = jnp.broadcast_to(den[i], (ABLK, HD))
                return c

            lax.fori_loop(0, SEQ // ABLK // ATTN_PAIR, step, 0)
        top = jnp.maximum(jnp.maximum(maxs[0][...], maxs[1][...]), maxs[2][...])
        den = jnp.zeros((SEQ, HD), F32)
        y = jnp.zeros((SEQ, HD), F32)
        for g in range(3):
            wgt = jnp.exp(maxs[g][...] - top)
            den = den + wgt * dens[g][...]
            y = y + wgt * accs[g][...]
        y = y / den
        y_ref[...] = y
        ya_ref[...] = (y * _silu(z_ref[...])).astype(ya_ref.dtype)
        l_ref[...] = top + jnp.log(den)

    ospec = pl.BlockSpec((SEQ, HD), lambda j: (0, j))
    return pl.pallas_call(
        body, name=name, grid=(AW // HD,),
        in_specs=_qkv_specs(lambda j: j) + [pl.BlockSpec((SEQ, HD), lambda j: (0, CB_ZA * 4 + j))],
        out_specs=[ospec, ospec, ospec],
        out_shape=[jax.ShapeDtypeStruct((SEQ, AW), F32), jax.ShapeDtypeStruct((SEQ, AW), BF16),
                   jax.ShapeDtypeStruct((SEQ, AW), F32)],
        scratch_shapes=[pltpu.VMEM((SEQ, HD), F32)] * 9,
        compiler_params=_params("parallel"),
    )(*([proj] * 10))


def _attention_bwd(name, proj, dya, y, lse):
    def tn(a, b_):
        return lax.dot_general(a, b_, (((0,), (0,)), ((), ())), preferred_element_type=F32)

    def nt(a, b_):
        return lax.dot_general(a, b_, (((1,), (1,)), ((), ())), preferred_element_type=F32)

    def body(*refs):
        qkv, z_ref, dya_ref, y_ref, l_ref = refs[:9], refs[9], refs[10], refs[11], refs[12]
        outs, dza_ref = refs[13:22], refs[22]
        dy_s, dsum_s, dq_s, dk_own, dv_own, dk_prev, dv_prev = refs[23:]
        _, vjp = jax.vjp(lambda y_, z_: y_ * _silu(z_), y_ref[...], z_ref[...])
        dy, dz = vjp(dya_ref[...])
        dza_ref[...] = dz.astype(dza_ref.dtype)
        dy_s[...] = dy
        dsum_s[...] = jnp.broadcast_to(jnp.sum(dy * y_ref[...], axis=-1, keepdims=True), (SEQ, HD))
        for g, dil in enumerate(DILATIONS):
            q_ref, k_ref, v_ref = qkv[3 * g:3 * g + 3]
            nb, where = _group_blocks(dil)
            if nb > 1:
                dk_prev[...] = jnp.zeros(dk_prev.shape, F32)
                dv_prev[...] = jnp.zeros(dv_prev.shape, F32)

            def step(t, c, dil=dil, nb=nb, where=where, q_ref=q_ref, k_ref=k_ref, v_ref=v_ref):
                rb = [where(t + i * (SEQ // ABLK // ATTN_PAIR)) for i in range(ATTN_PAIR)]
                sides = []
                for r, b in rb:
                    rows = _block_rows(dil, r, b)
                    own = dict(b=b, qrows=rows, krows=rows, prev=False, dk=dk_own, dv=dv_own,
                               q=q_ref[rows, :].astype(BF16), dy=dy_s[rows, :].astype(BF16))
                    sides.append(own)
                    if nb > 1:
                        sides.append(dict(own, krows=_block_rows(dil, r, jnp.maximum(b - 1, 0)), prev=True,
                                          dk=dk_prev, dv=dv_prev))
                for s_ in sides:
                    s_["k"] = k_ref[s_["krows"], :].astype(BF16)
                    s_["v"] = v_ref[s_["krows"], :].astype(BF16)
                for s_ in sides:
                    sc = _scores(s_["q"], s_["k"], s_["prev"])
                    s_["s"] = jnp.where(s_["b"] > 0, sc, -1e30) if s_["prev"] else sc
                    s_["dp"] = nt(s_["dy"], s_["v"])
                for s_ in sides:
                    p = jnp.exp(s_["s"] - l_ref[s_["qrows"], :])
                    s_["p"] = p.astype(BF16)
                    s_["ds"] = (p * (s_["dp"] - dsum_s[s_["qrows"], :]) * (HD ** -0.5)).astype(BF16)
                for s_ in sides:
                    s_["dk"][s_["krows"], :] = tn(s_["ds"], s_["q"])
                    s_["dv"][s_["krows"], :] = tn(s_["p"], s_["dy"])
                    s_["dq"] = jnp.dot(s_["ds"], s_["k"], preferred_element_type=F32)
                per = len(sides) // ATTN_PAIR
                for i in range(ATTN_PAIR):
                    dq = sides[i * per]["dq"]
                    if per > 1:
                        dq = dq + sides[i * per + 1]["dq"]
                    dq_s[sides[i * per]["qrows"], :] = dq
                return c

            lax.fori_loop(0, SEQ // ABLK // ATTN_PAIR, step, 0)
            dq_ref, dk_ref, dv_ref = outs[3 * g:3 * g + 3]
            dq_ref[...] = dq_s[...].astype(dq_ref.dtype)
            if nb > 1:
                dk_ref[...] = (dk_own[...] + dk_prev[...]).astype(dk_ref.dtype)
                dv_ref[...] = (dv_own[...] + dv_prev[...]).astype(dv_ref.dtype)
            else:
                dk_ref[...] = dk_own[...].astype(dk_ref.dtype)
                dv_ref[...] = dv_own[...].astype(dv_ref.dtype)

    ospec = pl.BlockSpec((SEQ, HD), lambda j: (0, j))
    outs = pl.pallas_call(
        body, name=name, grid=(AW // HD,),
        in_specs=_qkv_specs(lambda j: j) + [pl.BlockSpec((SEQ, HD), lambda j: (0, CB_ZA * 4 + j))] + [ospec] * 3,
        out_specs=[ospec] * 10, out_shape=[jax.ShapeDtypeStruct((SEQ, AW), BF16)] * 10,
        scratch_shapes=[pltpu.VMEM((SEQ, HD), F32)] * 7,
        compiler_params=_params("parallel"),
    )(*([proj] * 10), dya, y, lse)
    return outs[:9], outs[9]


def _rms(x, g):
    return x * lax.rsqrt(jnp.mean(x * x, axis=-1, keepdims=True) + RMS_EPS) * g


def _sig(x):
    return 1.0 / (1.0 + jnp.exp(-x))


def _silu(x):
    return x * _sig(x)


def _gelu(x):
    return 0.5 * x * (1.0 + jnp.tanh(math.sqrt(2.0 / math.pi) * (x + 0.044715 * (x * x * x))))


def _y1_fn(y0p, u, dskip):
    return _gelu(y0p + dskip * u)


def _ys_fn(y1, t, z, bglu):
    return y1 * _sig(t + bglu) * _silu(z)


def _merge_fn(ms, ma, gs, ga):
    return _sig(gs) * ms + _sig(ga) * ma


def _colsum(v):
    return jnp.sum(v, axis=0, keepdims=True)


def _lam_fn(lre, lim, ldt):
    a = jnp.minimum(lre, -1e-4)
    dt = jnp.exp(ldt)
    mag = jnp.exp(a * dt)
    ar = mag * jnp.cos(lim * dt)
    ai = mag * jnp.sin(lim * dt)
    den = a * a + lim * lim
    cr = ((ar - 1.0) * a + ai * lim) / den
    ci = (ai * a - (ar - 1.0) * lim) / den
    return ar, ai, cr, ci


def _bbar_fn(cr, ci, bre, bim):
    return cr * bre - ci * bim, cr * bim + ci * bre


def _same_group(rows, a, cols, b):
    r = lax.broadcasted_iota(jnp.int32, (rows, cols), 0) >> (a.bit_length() - 1)
    c = lax.broadcasted_iota(jnp.int32, (rows, cols), 1) >> (b.bit_length() - 1)
    return r == c


def _expand(name, blocks, signs, a, b, dtype, after=None):
    rows, cols = NGRP * a, NGRP * b
    n = len(blocks)
    assert a & (a - 1) == 0 and b & (b - 1) == 0
    after_specs, after_args = _after(after)

    def body(*refs):
        o_ref = refs[-1]
        tile = (lax.broadcasted_iota(jnp.int32, (b, cols), 1) & (b - 1)
                == lax.broadcasted_iota(jnp.int32, (b, cols), 0)).astype(F32)
        keep = _same_group(rows, a, cols, b)
        for i, (ref, sign) in enumerate(zip(refs[:n], signs)):
            spread = jnp.dot(ref[...], tile, preferred_element_type=F32, precision=lax.Precision.HIGHEST)
            o_ref[i * rows:(i + 1) * rows, :] = jnp.where(keep, sign * spread, 0.0).astype(o_ref.dtype)

    return pl.pallas_call(body, name=name, grid=(1,),
                          in_specs=[pl.BlockSpec((rows, b), lambda i: (0, 0))] * n + after_specs,
                          out_specs=pl.BlockSpec((n * rows, cols), lambda i: (0, 0)),
                          out_shape=jax.ShapeDtypeStruct((n * rows, cols), dtype),
                          compiler_params=_params("arbitrary"))(*blocks, *after_args)


def _extract(name, m, a, b, at=(0, 0), after=None):
    rows, cols = NGRP * a, NGRP * b
    assert a & (a - 1) == 0 and b & (b - 1) == 0
    after_specs, after_args = _after(after)

    def body(m_ref, *rest):
        o_ref = rest[-1]
        tile = (lax.broadcasted_iota(jnp.int32, (cols, b), 0) & (b - 1)
                == lax.broadcasted_iota(jnp.int32, (cols, b), 1)).astype(F32)
        kept = jnp.where(_same_group(rows, a, cols, b), m_ref[...], 0.0)
        o_ref[...] = jnp.dot(kept, tile, preferred_element_type=F32, precision=lax.Precision.HIGHEST)

    return pl.pallas_call(body, name=name, grid=(1,),
                          in_specs=[pl.BlockSpec((rows, cols), lambda i: at)] + after_specs,
                          out_specs=pl.BlockSpec((rows, b), lambda i: (0, 0)),
                          out_shape=jax.ShapeDtypeStruct((rows, b), F32),
                          compiler_params=_params("arbitrary"))(m, *after_args)


def _s5_prepare(l, sp, after):
    tag = f"l{l}_"
    ar, ai, cr, ci = _ew(tag + "lam", _lam_fn, NGRP, NGRP,
                         [_ri(sp["lambda_re"]), _ri(sp["lambda_im"]),
                          _ri(sp["log_dt"].reshape(NGRP, 1) + after[0, 0])], [], [(NST, F32)] * 4)
    bre = sp["b_re"].reshape(NS, GCH)
    bim = sp["b_im"].reshape(NS, GCH)
    bbr, bbi = _ew(tag + "bbar", _bbar_fn, NS, NS, [_ri(cr.reshape(NS, 1)), _ri(ci.reshape(NS, 1)), _ri(bre), _ri(bim)],
                   [], [(GCH, F32)] * 2)
    wdt = _expand(tag + "wdt", [bbr, bbi], [1.0, 1.0], NST, GCH, BF16)
    cmt = _expand(tag + "cmt", [sp["c_re"].reshape(SW, NST), sp["c_im"].reshape(SW, NST)], [1.0, -1.0], GCH, NST, BF16,
                  after=after)
    return dict(ar=ar, ai=ai, cr=cr, ci=ci, wdt=wdt, cmt=cmt)


def _layer_head(l, x, sp):
    g1 = sp["pre_norm_g"].reshape(1, DM)
    (h,) = _ew(f"l{l}_rms1", lambda x_, g: (_rms(x_, g),), SEQ, 256, [_ri(x)], [g1], [(DM, BF16)])
    return jnp.stack([h, _to_chunked(h)])


def _layer_fwd(l, x, hv, w, sp, prep, after):
    tag = f"l{l}_"
    win = w["w_in"]
    proj = _mm(tag + "proj", hv, win, "nn", SEQ, NCOL, DM, SEQ, 512, 1024, F32,
               a_spec=pl.BlockSpec((None, SEQ, 1024), lambda i, j, k: (_row_order(j), 0, 0)),
               b_spec=pl.BlockSpec((None, 1024, 512), lambda i, j, k: (j // 2, 0, j % 2)), after=after)

    ar, ai, cr, ci, wdt, cmt = (prep[k] for k in ("ar", "ai", "cr", "ci", "wdt", "cmt"))
    s, y0p = _s5_forward(tag + "s5", proj, wdt, cmt, ar.reshape(1, NS), ai.reshape(1, NS))
    dskip = sp["d_skip"].reshape(1, SW)
    (y1,) = _ew(tag + "y1", lambda a, u, d: (_y1_fn(a, u, d),), SEQ, 256,
                [_ri(y0p), _ri(proj, SW, CB_U)], [dskip], [(SW, F32)])
    t = _mm(tag + "glu", y1, w["w_glu"], "nn", SEQ, SW, SW, 1024, 512, 512, F32)
    bglu = sp["b_glu"].reshape(1, SW)
    (ys_c,) = _ew(tag + "ys", lambda y1_, t_, z, b_: (_ys_fn(y1_, t_, z, b_),), SEQ, 256,
                  [_ri(y1), _ri(t), _ri(proj, SW, CB_ZS)], [bglu], [(SW, BF16)])
    ys = _from_chunked(ys_c)

    ypre, ya, lse = _attention_fwd(tag + "attn", proj)

    ms = _mm(tag + "branch_s", ys, w["w_branch_s"], "nn", SEQ, DM, SW, 1024, 1024, 512, F32)
    ma = _mm(tag + "branch_a", ya, w["w_branch_a"], "nn", SEQ, DM, AW, 1024, 1024, 512, F32)
    (merged,) = _ew(tag + "merge", lambda a, b_, c, d: (_merge_fn(a, b_, c, d),), SEQ, 256,
                    [_ri(ms), _ri(ma), _ri(proj, DM, CB_GS), _ri(proj, DM, CB_GA)], [], [(DM, BF16)])
    out = _mm(tag + "out", merged, w["w_out"], "nn", SEQ, DM, DM, 1024, 1024, 1024, F32)
    g2 = sp["post_norm_g"].reshape(1, DM)
    (x_new,) = _ew(tag + "post", lambda x_, o, g: (x_ + _rms(o, g),), SEQ, 256, [_ri(x), _ri(out)], [g2], [(DM, F32)])
    res = dict(x=x, hv=hv, proj=proj, ar=ar, ai=ai, cr=cr, ci=ci, wdt=wdt, cmt=cmt, s=s, y0p=y0p,
               y1=y1, t=t, ys=ys, ya=ya, ypre=ypre, lse=lse, ms=ms, ma=ma, merged=merged, out=out)
    return x_new, res


def _layer_bwd(l, dxn, r, w, sp, big_done, after=None):
    tag = f"l{l}b_"
    proj = r["proj"]
    g1 = sp["pre_norm_g"].reshape(1, DM)
    g2 = sp["post_norm_g"].reshape(1, DM)
    dskip = sp["d_skip"].reshape(1, SW)
    bglu = sp["b_glu"].reshape(1, SW)

    def post_b(d, o, g):
        _, vjp = jax.vjp(_rms, o, g)
        do, dg = vjp(d)
        return do, dg

    d_out, dg2 = _ew(tag + "post", post_b, SEQ, 256, [_ri(dxn), _ri(r["out"])], [g2], [(DM, BF16)], [DM])
    dw_out = _mm(tag + "dw_out", r["merged"], d_out, "tn", DM, DM, SEQ, 1024, 1024, SEQ, BF16, after=after)
    dmerged = _mm(tag + "dmerged", d_out, w["w_out"], "nt", SEQ, DM, DM, 1024, 1024, 1024, F32, after=after)

    def merge_b(d, ms, ma, gs, ga):
        _, vjp = jax.vjp(_merge_fn, ms, ma, gs, ga)
        return vjp(d)

    dms, dma, dgs, dga = _ew(tag + "merge", merge_b, SEQ, 256,
                             [_ri(dmerged), _ri(r["ms"]), _ri(r["ma"]), _ri(proj, DM, CB_GS), _ri(proj, DM, CB_GA)],
                             [], [(DM, BF16)] * 4)
    dw_bs = _mm(tag + "dw_bs", r["ys"], dms, "tn", SW, DM, SEQ, 512, 1024, SEQ, BF16)
    dw_ba = _mm(tag + "dw_ba", r["ya"], dma, "tn", AW, DM, SEQ, 512, 1024, SEQ, BF16)
    dys = _mm(tag + "dys", dms, w["w_branch_s"], "nt", SEQ, SW, DM, 1024, 512, 1024, F32)
    dya = _mm(tag + "dya", dma, w["w_branch_a"], "nt", SEQ, AW, DM, 1024, 512, 1024, F32)

    dqkv, dza = _attention_bwd(tag + "attn", proj, dya, r["ypre"], r["lse"])

    def ys_b(d, y1, t, z, b_):
        _, vjp = jax.vjp(_ys_fn, y1, t, z, b_)
        dy1, dt, dz, _ = vjp(d)
        return dy1, dt, dz, _colsum(dt)

    dy1a, dt, dzs, dbglu = _ew(tag + "ys", ys_b, SEQ, 256,
                               [_ri(_to_chunked(dys)), _ri(r["y1"]), _ri(r["t"]), _ri(proj, SW, CB_ZS)],
                               [bglu], [(SW, F32), (SW, BF16), (SW, BF16)], [SW])
    dw_glu = _mm(tag + "dw_glu", r["y1"], dt, "tn", SW, SW, SEQ, 512, 512, SEQ, BF16)
    dy1b = _mm(tag + "dy1b", dt, w["w_glu"], "nt", SEQ, SW, SW, 1024, 512, 512, F32)

    def y1_b(da, db, y0p, u, d_):
        _, vjp = jax.vjp(_y1_fn, y0p, u, d_)
        dy0, du, dd = vjp(da + db)
        return dy0, du, dd

    dy0, du_skip, ddskip = _ew(tag + "y1", y1_b, SEQ, 256,
                               [_ri(dy1a), _ri(dy1b), _ri(r["y0p"]), _ri(proj, SW, CB_U)], [dskip],
                               [(SW, BF16), (SW, F32)], [SW])
    dlr, dli, dwdt, dcmt, du_s = _s5_backward(tag + "s5", dy0, proj, r["s"], r["wdt"], r["cmt"],
                                              r["ar"].reshape(1, NS), r["ai"].reshape(1, NS))
    (du,) = _ew(tag + "du", lambda a, c: (a + c,), SEQ, 256, [_ri(du_s), _ri(du_skip)], [], [(SW, BF16)])

    dq, dk, dv = ([dqkv[3 * g + i] for g in range(3)] for i in range(3))
    dproj = jnp.concatenate([du, dzs, *dq, *dk, *dv, dza, dgs, dga], axis=1)
    dw_in = _mm(tag + "dw_in", r["hv"], dproj, "tn", DM, NCOL, SEQ, 1024, 512, SEQ, BF16,
                a_spec=pl.BlockSpec((None, SEQ, 1024), lambda i, j, k: (_row_order(j), 0, 0)),
                o_spec=pl.BlockSpec((None, 1024, 512), lambda i, j, k: (j // 2, 0, j % 2)), out_shape=(NDEV, DM, DM))
    tok = big_done(l, dict(w_in=dw_in, w_glu=dw_glu.reshape(NDEV, SW // NDEV, SW),
                           w_branch_s=dw_bs.reshape(SW, NDEV, DM // NDEV).transpose(1, 0, 2),
                           w_branch_a=dw_ba.reshape(AW, NDEV, DM // NDEV).transpose(1, 0, 2),
                           w_out=dw_out.reshape(NDEV, DM // NDEV, DM)))
    if tok is not None:
        g1 = g1 + tok[0, 0]

    dwdt = dwdt.reshape(2 * NS, SW)
    dcmt = dcmt.reshape(2 * SW, NS)
    dbbr = _extract(tag + "dbbr", dwdt, NST, GCH, (0, 0), after=tok)
    dbbi = _extract(tag + "dbbi", dwdt, NST, GCH, (1, 0), after=tok)
    bre = sp["b_re"].reshape(NS, GCH)
    bim = sp["b_im"].reshape(NS, GCH)

    def bbar_b(cr, ci, br_, bi_, dr, di):
        _, vjp = jax.vjp(_bbar_fn, cr, ci, br_, bi_)
        return vjp((dr, di))

    dcr, dci, dbre, dbim = _ew(tag + "bbar", bbar_b, NS, NS,
                               [_ri(r["cr"].reshape(NS, 1)), _ri(r["ci"].reshape(NS, 1)), _ri(bre), _ri(bim),
                                _ri(dbbr), _ri(dbbi)], [], [(1, F32), (1, F32), (GCH, F32), (GCH, F32)])

    def lam_b(lre, lim, ldt, dar, dai, dcr_, dci_):
        _, vjp = jax.vjp(_lam_fn, lre, lim, ldt)
        return vjp((dar, dai, dcr_, dci_))

    dlre, dlim, dldt = _ew(tag + "lam", lam_b, NGRP, NGRP,
                           [_ri(sp["lambda_re"]), _ri(sp["lambda_im"]), _ri(sp["log_dt"].reshape(NGRP, 1)),
                            _ri(dlr.reshape(NGRP, NST)), _ri(dli.reshape(NGRP, NST)),
                            _ri(dcr.reshape(NGRP, NST)), _ri(dci.reshape(NGRP, NST))], [],
                           [(NST, F32), (NST, F32), (1, F32)])
    dc_re = _extract(tag + "dc_re", dcmt, GCH, NST, (0, 0), after=tok).reshape(NGRP, GCH, NST)
    dc_im = -_extract(tag + "dc_im", dcmt, GCH, NST, (1, 0), after=tok).reshape(NGRP, GCH, NST)

    dh_time = _mm(tag + "dh_time", dproj, w["w_in"], "nt", SEQ, DM, NCOL - DM, SEQ, 1024, 1024, F32,
                  a_spec=pl.BlockSpec((SEQ, 1024), lambda i, j, k: (0, 1 + k)),
                  b_spec=pl.BlockSpec((None, 1024, 1024), lambda i, j, k: (1 + k, 0, 0)), after=tok)
    dh_chunked = _mm(tag + "dh_chunked", dproj, w["w_in"], "nt", SEQ, DM, DM, SEQ, 1024, 1024, F32,
                     a_spec=pl.BlockSpec((SEQ, 1024), lambda i, j, k: (0, 0)),
                     b_spec=pl.BlockSpec((None, 1024, 1024), lambda i, j, k: (0, 0, 0)), after=tok)
    dh = [dh_time, _from_chunked(dh_chunked)]

    def pre_b(d, dh0, dh1, x_, g):
        _, vjp = jax.vjp(_rms, x_, g)
        dx_, dg = vjp(dh0 + dh1)
        return d + dx_, dg

    dx, dg1 = _ew(tag + "pre", pre_b, SEQ, 256, [_ri(dxn)] + [_ri(t_) for t_ in dh] + [_ri(r["x"])], [g1],
                  [(DM, F32)], [DM])

    small = dict(pre_norm_g=dg1.reshape(DM), lambda_re=dlre, lambda_im=dlim, log_dt=dldt.reshape(NGRP),
                 b_re=dbre.reshape(NGRP, NST, GCH), b_im=dbim.reshape(NGRP, NST, GCH), c_re=dc_re, c_im=dc_im,
                 d_skip=ddskip.reshape(SW), b_glu=dbglu.reshape(SW), post_norm_g=dg2.reshape(DM))
    return dx, small


_HBM = pl.BlockSpec(memory_space=pltpu.HBM)
_SEM = pl.BlockSpec(memory_space=pltpu.SEMAPHORE)
_EFFECT = pltpu.SideEffectType.DATAFLOW_SIDE_EFFECTING


def _remote_copies(srcs, dsts, send_sems, recv_sems, gather):
    x, y, c = lax.axis_index("x"), lax.axis_index("y"), lax.axis_index("c")
    me = 4 * x + 2 * y + c
    copies = []
    for i in range(len(srcs)):
        for k in range(1, NDEV):
            peer = (x ^ (k >> 2), y ^ ((k >> 1) & 1), c ^ (k & 1))
            src = srcs[i] if gather[i] else srcs[i].at[me ^ k]
            copies.append(pltpu.make_async_remote_copy(
                src_ref=src, dst_ref=dsts[i].at[me], send_sem=send_sems[i], recv_sem=recv_sems[i],
                device_id=peer, device_id_type=pl.DeviceIdType.MESH))
    return copies


def _all_seven(dst, send_sem, recv_sem):
    seven = dst.at[pl.ds(0, NDEV - 1)]
    me = (lax.axis_index("x"), lax.axis_index("y"), lax.axis_index("c"))
    return pltpu.make_async_remote_copy(src_ref=seven, dst_ref=seven, send_sem=send_sem, recv_sem=recv_sem,
                                        device_id=me, device_id_type=pl.DeviceIdType.MESH)


def _own_slabs(name, arrs, gather, after):
    n = len(arrs)
    me = (4 * lax.axis_index("x") + 2 * lax.axis_index("y") + lax.axis_index("c")).astype(jnp.int32).reshape(1)

    def body(me_ref, *refs):
        for src, dst in zip(refs[:n], refs[n + 1:]):
            dst[...] = src[...]

    def zeros(k):
        return (0,) * k

    in_specs, out_specs, out_shape = [], [], []
    for a, g in zip(arrs, gather):
        slab = a.shape if g else a.shape[1:]
        nd = len(slab)
        if g:
            in_specs.append(pl.BlockSpec(slab, functools.partial(lambda i, me_ref, nd: zeros(nd), nd=nd)))
        else:
            in_specs.append(pl.BlockSpec((None,) + slab, functools.partial(lambda i, me_ref, nd: (me_ref[0],) + zeros(nd), nd=nd)))
        out_specs.append(pl.BlockSpec((None,) + slab, functools.partial(lambda i, me_ref, nd: (me_ref[0],) + zeros(nd), nd=nd)))
        out_shape.append(jax.ShapeDtypeStruct((NDEV,) + slab, a.dtype))
    in_specs.append(pl.BlockSpec(memory_space=pl.ANY))
    return pl.pallas_call(
        body, name=name, out_shape=out_shape,
        grid_spec=pltpu.PrefetchScalarGridSpec(num_scalar_prefetch=1, grid=(1,), in_specs=in_specs, out_specs=out_specs),
        compiler_params=_params("arbitrary"),
    )(me, *arrs, after)


def _exchange_start(name, arrs, gather, after):
    n = len(arrs)
    lands = _own_slabs(name + "_own", arrs, gather, after)

    def body(*refs):
        srcs, dsts = refs[:n], refs[n:2 * n]
        send_sems, recv_sems = refs[2 * n:3 * n], refs[3 * n:4 * n]
        token = refs[-1]
        for cp in _remote_copies(srcs, dsts, send_sems, recv_sems, gather):
            cp.start()
        token[...] = jnp.zeros(token.shape, token.dtype)

    thru = [pltpu.HBM(a.shape, a.dtype) for a in list(arrs) + list(lands)]
    outs = pl.pallas_call(
        body, name=name,
        out_shape=(*[pltpu.SemaphoreType.DMA(())] * (2 * n), *thru, jax.ShapeDtypeStruct((8, 128), F32)),
        in_specs=[_HBM] * (2 * n),
        out_specs=(*[_SEM] * (2 * n), *[_HBM] * (2 * n), pl.BlockSpec(memory_space=pltpu.VMEM)),
        input_output_aliases={i: 2 * n + i for i in range(2 * n)},
        compiler_params=pltpu.CompilerParams(has_side_effects=_EFFECT),
    )(*[pltpu.with_memory_space_constraint(a, pltpu.HBM) for a in list(arrs) + list(lands)])
    return dict(send=outs[:n], recv=outs[n:2 * n], srcs=outs[2 * n:3 * n], lands=outs[3 * n:4 * n], token=outs[-1],
                gather=gather)


def _exchange_wait(name, started, after):
    n = len(started["srcs"])
    after = list(after)

    def body(*refs):
        dsts = refs[n:2 * n]
        send_sems, recv_sems = refs[2 * n:3 * n], refs[3 * n:4 * n]
        for i in range(n):
            cp = _all_seven(dsts[i], send_sems[i], recv_sems[i])
            cp.wait_send()
            cp.wait_recv()

    bufs = list(started["srcs"]) + list(started["lands"])
    outs = pl.pallas_call(
        body, name=name, out_shape=tuple(pltpu.HBM(a.shape, a.dtype) for a in bufs),
        in_specs=[_HBM] * (2 * n) + [_SEM] * (2 * n) + [pl.BlockSpec(memory_space=pl.ANY)] * len(after),
        out_specs=(_HBM,) * (2 * n), input_output_aliases={i: i for i in range(2 * n)},
        compiler_params=pltpu.CompilerParams(has_side_effects=_EFFECT),
    )(*bufs, *started["send"], *started["recv"], *after)
    return outs[n:]


def _sum_in_order(parts):
    g = parts[0].astype(F32)
    for p in parts[1:]:
        g = g + p.astype(F32)
    return g


def _adam_update(g, w_, m_, v_):
    m2 = B1 * m_ + (1.0 - B1) * g
    v2 = B2 * v_ + (1.0 - B2) * (g * g)
    m_hat = m2 / (1.0 - B1 ** STEP)
    v_hat = v2 / (1.0 - B2 ** STEP)
    delta = -LR * (m_hat / (jnp.sqrt(v_hat) + ADAM_EPS) + WD * w_)
    return g, delta, m2, v2


def _adamw(name, g, w, m, v, br):
    rows, cols = w.shape
    return _ew(name, _adam_update, rows, br, [_ri(g), _ri(w), _ri(m), _ri(v)], [], [(cols, F32)] * 4)


def _adamw_layer(name, l, parts, w, m, v, br, outs):
    rows, cols = w.shape
    nb = rows // DEPTH // br
    assert nb * br * DEPTH == rows

    def body(*refs):
        vals = _adam_update(_sum_in_order([r[...] for r in refs[:NDEV]]), *[r[...] for r in refs[NDEV:NDEV + 3]])
        for r, val in zip(refs[-4:], vals):
            r[...] = val

    mine = pl.BlockSpec((br, cols), lambda i: (l * nb + i, 0))
    in_specs = [pl.BlockSpec((br, cols), functools.partial(lambda i, d: (d * nb + i, 0), d=d)) for d in range(NDEV)]
    in_specs += [mine] * 3 + [pl.BlockSpec(memory_space=pl.ANY)] * len(outs)
    return pl.pallas_call(
        body, name=f"{name}{l}", grid=(nb,), in_specs=in_specs, out_specs=[mine] * 4,
        out_shape=[jax.ShapeDtypeStruct((rows, cols), F32)] * 4,
        input_output_aliases={NDEV + 3 + q: q for q in range(len(outs))},
        compiler_params=_params("arbitrary"),
    )(*([parts] * NDEV), w, m, v, *outs)


SMALL = ("pre_norm_g", "lambda_re", "lambda_im", "log_dt", "b_re", "b_im", "c_re", "c_im", "d_skip", "b_glu",
         "post_norm_g")
BIG = ("w_in", "w_glu", "w_branch_s", "w_branch_a", "w_out")
WEIGHTS = ("pre_norm_g", "w_in", "lambda_re", "lambda_im", "log_dt", "b_re", "b_im", "c_re", "c_im", "d_skip",
           "w_glu", "b_glu", "w_branch_s", "w_branch_a", "w_out", "post_norm_g")
PACK_COLS = 1024
PACK_BR = 136


def _pack_layer(d):
    pieces = [d[k].astype(F32).reshape(-1) for k in SMALL]
    used = sum(p.shape[0] for p in pieces)
    assert used <= PACK_BR * PACK_COLS
    return jnp.concatenate(pieces + [jnp.zeros((PACK_BR * PACK_COLS - used,), F32)]).reshape(PACK_BR, PACK_COLS)


def _unpack(p, like):
    flat = p.reshape(DEPTH, PACK_BR * PACK_COLS)
    out, off = {}, 0
    for k in SMALL:
        n = like[k].size // DEPTH
        out[k] = flat[:, off:off + n].reshape(like[k].shape)
        off += n
    return out


def _local_step(x, target, small, started, weights_of, big_done, small_done, total_loss):
    preps = [_s5_prepare(l, {k: small[k][l] for k in SMALL}, started) for l in range(DEPTH)]
    res, ws = [], []
    for l in range(DEPTH):
        sp = {k: small[k][l] for k in SMALL}
        hv = _layer_head(l, x, sp)
        w_l, tok = weights_of(l, [hv] + ([p[k] for p in preps for k in ("wdt", "cmt")] if l == 0 else []))
        x, r = _layer_fwd(l, x, hv, w_l, sp, preps[l], tok)
        res.append(r)
        ws.append(w_l)

    def loss_fn(y, t):
        e = y - t
        return e * (1.0 / DM), jnp.sum(_colsum(0.5 * e * e * (1.0 / DM)), axis=1, keepdims=True)

    dx, loss = _ew("loss", loss_fn, SEQ, 256, [_ri(x), _ri(target)], [], [(DM, F32)], [1])
    total = total_loss(loss.reshape(()))
    for l in reversed(range(DEPTH)):
        dx, sm = _layer_bwd(l, dx, res[l], ws[l], {k: small[k][l] for k in SMALL}, big_done,
                            after=total.reshape(1, 1) if l == DEPTH - 1 else None)
        small_done(l, sm)
    return total, dx


def _full_weights(gathered):
    g = gathered
    return dict(
        w_in=g["w_in"],
        w_glu=g["w_glu"].reshape(SW, SW),
        w_branch_s=g["w_branch_s"].transpose(1, 0, 2).reshape(SW, DM),
        w_branch_a=g["w_branch_a"].transpose(1, 0, 2).reshape(AW, DM),
        w_out=g["w_out"].reshape(DM, DM),
    )


def kernel(x, pre_norm_g, w_in, lambda_re, lambda_im, log_dt, b_re, b_im, c_re, c_im, d_skip, w_glu, b_glu, w_branch_s, w_branch_a, w_out, post_norm_g, loss_target, m_pre_norm_g, m_w_in, m_lambda_re, m_lambda_im, m_log_dt, m_b_re, m_b_im, m_c_re, m_c_im, m_d_skip, m_w_glu, m_b_glu, m_w_branch_s, m_w_branch_a, m_w_out, m_post_norm_g, v_pre_norm_g, v_w_in, v_lambda_re, v_lambda_im, v_log_dt, v_b_re, v_b_im, v_c_re, v_c_im, v_d_skip, v_w_glu, v_b_glu, v_w_branch_s, v_w_branch_a, v_w_out, v_post_norm_g):
    wts = dict(pre_norm_g=pre_norm_g, w_in=w_in, lambda_re=lambda_re, lambda_im=lambda_im, log_dt=log_dt, b_re=b_re,
               b_im=b_im, c_re=c_re, c_im=c_im, d_skip=d_skip, w_glu=w_glu, b_glu=b_glu, w_branch_s=w_branch_s,
               w_branch_a=w_branch_a, w_out=w_out, post_norm_g=post_norm_g)
    mom = dict(pre_norm_g=m_pre_norm_g, w_in=m_w_in, lambda_re=m_lambda_re, lambda_im=m_lambda_im, log_dt=m_log_dt,
               b_re=m_b_re, b_im=m_b_im, c_re=m_c_re, c_im=m_c_im, d_skip=m_d_skip, w_glu=m_w_glu, b_glu=m_b_glu,
               w_branch_s=m_w_branch_s, w_branch_a=m_w_branch_a, w_out=m_w_out, post_norm_g=m_post_norm_g)
    var = dict(pre_norm_g=v_pre_norm_g, w_in=v_w_in, lambda_re=v_lambda_re, lambda_im=v_lambda_im, log_dt=v_log_dt,
               b_re=v_b_re, b_im=v_b_im, c_re=v_c_re, c_im=v_c_im, d_skip=v_d_skip, w_glu=v_w_glu, b_glu=v_b_glu,
               w_branch_s=v_w_branch_s, w_branch_a=v_w_branch_a, w_out=v_w_out, post_norm_g=v_post_norm_g)

    def gather_start(l, after):
        return _exchange_start(f"gather_start{l}", [wts[k][l].astype(BF16) for k in BIG], [True] * len(BIG), after)

    gathering = {0: gather_start(0, x)}
    sending, packed = {}, {}

    def weights_of(l, after):
        gathered = _exchange_wait(f"gather_wait{l}", gathering[l], after)
        tok = None
        if l + 1 < DEPTH:
            gathering[l + 1] = gather_start(l + 1, gathered[0])
            tok = gathering[l + 1]["token"]
        return _full_weights(dict(zip(BIG, gathered))), tok

    def big_done(l, big):
        arrs, kinds = [big[k] for k in BIG], [False] * len(BIG)
        if l + 1 < DEPTH:
            arrs, kinds = arrs + [packed[l + 1]], kinds + [True]
        sending[l] = _exchange_start(f"grads_start{l}", arrs, kinds, big["w_in"])
        return sending[l]["token"]

    def small_done(l, sm):
        packed[l] = _pack_layer(sm)

    loss, dx = _local_step(x[0], loss_target[0], wts, gathering[0]["token"], weights_of, big_done, small_done,
                           lambda part: lax.psum(part, ("x", "y", "c")))
    last = _exchange_start("grads_start_last", [packed[0]], [True], dx)

    grads, delta, new_m, new_v = {}, {}, {}, {}

    def as_rows(k):
        cols = wts[k].shape[-1]
        rows = wts[k].size // cols
        return rows, cols, [t[k].reshape(rows, cols) for t in (wts, mom, var)]

    recv_l, outs, after = {}, {k: [] for k in BIG}, [dx]
    for l in reversed(range(DEPTH)):
        recv_l[l] = _exchange_wait(f"grads_wait{l}", sending[l], after)
        for i, k in enumerate(BIG):
            rows, cols, wmv = as_rows(k)
            per_layer = rows // DEPTH
            outs[k] = _adamw_layer("adamw_" + k, l, recv_l[l][i].reshape(NDEV * per_layer, cols), *wmv,
                                   min(per_layer, 256), outs[k])
        after = [outs[k][0] for k in BIG]
    for k in BIG:
        grads[k], delta[k], new_m[k], new_v[k] = (o.reshape(wts[k].shape) for o in outs[k])

    def update(k, g):
        rows, cols, wmv = as_rows(k)
        res = _adamw("adamw_" + k, g.reshape(rows, cols), *wmv, min(rows, 1024 if cols <= 128 else 256))
        grads[k], delta[k], new_m[k], new_v[k] = (o.reshape(wts[k].shape) for o in res)

    recv_last = _exchange_wait("grads_wait_last", last, after)
    recv_small = jnp.concatenate([recv_last[0]] + [recv_l[l][len(BIG)] for l in range(DEPTH - 1)], axis=1)
    rows = DEPTH * PACK_BR
    (g_small,) = _ew("grads_small", lambda *p: (_sum_in_order(p),), rows, PACK_BR,
                     [_ri(recv_small.reshape(NDEV * rows, PACK_COLS), PACK_COLS, 0, d * DEPTH) for d in range(NDEV)], [],
                     [(PACK_COLS, F32)])
    for k, g in _unpack(g_small, wts).items():
        update(k, g)

    return (loss, dx[None], *[grads[k] for k in WEIGHTS], *[delta[k] for k in WEIGHTS],
            *[new_m[k] for k in WEIGHTS], *[new_v[k] for k in WEIGHTS])
```

```python
import functools
import math

import jax
import jax.numpy as jnp
from jax import lax
from jax.experimental import pallas as pl
from jax.experimental.pallas import tpu as pltpu

F32 = jnp.float32
BF16 = jnp.bfloat16

NDEV = 8
DEPTH = 4
SEQ = 2048
DM = 1024
NCOL = 8192
SW = 512
NGRP = 32
GCH = 16
NST = 64
NS = NGRP * NST
HD = 128
AW = 512
DILATIONS = (1, 4, 16)
ABLK = 128
ATTN_PAIR = 4
RMS_EPS = 1e-6
LR, B1, B2, ADAM_EPS, WD, STEP = 0.001, 0.9, 0.999, 1e-08, 0.01, 10

CB_U, CB_ZS, CB_Q, CB_K, CB_V, CB_ZA = 0, 1, 2, 5, 8, 11
CB_GS, CB_GA = 6, 7

VMEM_LIMIT = 56 * 2 ** 20


def _row_order(j):
    return jnp.where(j < CB_Q, 1, 0)


def _params(*sem):
    return pltpu.CompilerParams(dimension_semantics=sem, vmem_limit_bytes=VMEM_LIMIT)


def _ew(name, fn, rows, br, row_ins, bc_ins, row_outs, red_outs=()):
    n_in = len(row_ins) + len(bc_ins)
    n_ro = len(row_outs)
    steps = rows // br
    assert steps * br == rows

    def body(*refs):
        vals = fn(*[r[...] for r in refs[:n_in]])
        outs = refs[n_in:]
        for r, v in zip(outs[:n_ro], vals[:n_ro]):
            r[...] = v.astype(r.dtype)
        if red_outs:
            @pl.when(pl.program_id(0) == 0)
            def _():
                for r in outs[n_ro:]:
                    r[...] = jnp.zeros(r.shape, r.dtype)
            for r, v in zip(outs[n_ro:], vals[n_ro:]):
                r[...] += v

    in_specs = []
    for (_, w, cb, rb) in row_ins:
        in_specs.append(pl.BlockSpec((br, w), functools.partial(lambda i, cb, rb: (rb + i, cb), cb=cb, rb=rb)))
    for a in bc_ins:
        in_specs.append(pl.BlockSpec(a.shape, functools.partial(lambda i, nd: (0,) * nd, nd=a.ndim)))
    out_specs = [pl.BlockSpec((br, w), lambda i: (i, 0)) for (w, _) in row_outs]
    out_specs += [pl.BlockSpec((1, w), lambda i: (0, 0)) for w in red_outs]
    out_shape = [jax.ShapeDtypeStruct((rows, w), dt) for (w, dt) in row_outs]
    out_shape += [jax.ShapeDtypeStruct((1, w), F32) for w in red_outs]
    return pl.pallas_call(
        body, name=name, grid=(steps,), in_specs=in_specs, out_specs=out_specs, out_shape=out_shape,
        compiler_params=_params("arbitrary"),
    )(*[a for (a, _, _, _) in row_ins], *bc_ins)


def _ri(a, w=None, cb=0, rb=0):
    return (a, a.shape[1] if w is None else w, cb, rb)


_DIMS = {"nn": ((1,), (0,)), "nt": ((1,), (1,)), "tn": ((0,), (0,))}


def _after(after):
    return ([pl.BlockSpec(memory_space=pl.ANY)], [after]) if after is not None else ([], [])


def _mm(name, a, b, mode, M, N, K, bm, bn, bk, out_dtype, a_spec=None, b_spec=None, o_spec=None, out_shape=None,
        after=None):
    nk = K // bk
    assert M % bm == 0 and N % bn == 0 and nk * bk == K
    after_specs, after_args = _after(after)

    own_acc = nk > 1 and out_dtype != F32

    def body(a_ref, b_ref, *rest):
        o_ref, scratch = rest[len(after_args)], rest[len(after_args) + 1:]
        part = lax.dot_general(a_ref[...].astype(BF16), b_ref[...].astype(BF16), (_DIMS[mode], ((), ())),
                               preferred_element_type=F32)
        if nk == 1:
            o_ref[...] = part.astype(o_ref.dtype)
            return
        k = pl.program_id(2)
        acc_ref = scratch[0] if own_acc else o_ref

        @pl.when(k == 0)
        def _():
            acc_ref[...] = part

        @pl.when(k > 0)
        def _():
            acc_ref[...] += part

        if own_acc:
            @pl.when(k == nk - 1)
            def _():
                o_ref[...] = acc_ref[...].astype(o_ref.dtype)

    if a_spec is None:
        a_spec = (pl.BlockSpec((bk, bm), lambda i, j, k: (k, i)) if mode == "tn"
                  else pl.BlockSpec((bm, bk), lambda i, j, k: (i, k)))
    if b_spec is None:
        b_spec = (pl.BlockSpec((bn, bk), lambda i, j, k: (j, k)) if mode == "nt"
                  else pl.BlockSpec((bk, bn), lambda i, j, k: (k, j)))
    if o_spec is None:
        o_spec = pl.BlockSpec((bm, bn), lambda i, j, k: (i, j))
    if out_shape is None:
        out_shape = (M, N)
    return pl.pallas_call(
        body, name=name, grid=(M // bm, N // bn, nk), in_specs=[a_spec, b_spec] + after_specs, out_specs=o_spec,
        out_shape=jax.ShapeDtypeStruct(out_shape, out_dtype),
        scratch_shapes=[pltpu.VMEM((bm, bn), F32)] if own_acc else [],
        compiler_params=_params("parallel", "parallel", "arbitrary"),
    )(a, b, *after_args)


SCAN_LANES = 512
SCAN_CHUNKS = 8


def _to_chunked(a):
    return a.reshape(SCAN_CHUNKS, SEQ // SCAN_CHUNKS, -1).transpose(1, 0, 2).reshape(SEQ, -1)


def _from_chunked(a):
    return a.reshape(SEQ // SCAN_CHUNKS, SCAN_CHUNKS, -1).transpose(1, 0, 2).reshape(SEQ, -1)


def _scan_block(dr_ref, di_ref, sr_ref, si_ref, lam_r, lam_i, reverse):
    T = SEQ // SCAN_CHUNKS
    bl = lam_r.shape[1]
    assert T == 2 ** 8
    ar = jnp.broadcast_to(lam_r, (SCAN_CHUNKS, bl))
    ai = jnp.broadcast_to(lam_i, (SCAN_CHUNKS, bl))
    zero = jnp.zeros((SCAN_CHUNKS, bl), F32)

    def tile(j):
        return pl.ds(pl.multiple_of(j * SCAN_CHUNKS, SCAN_CHUNKS), SCAN_CHUNKS)

    def step(jj, carry):
        sr, si = carry
        j = T - 1 - jj if reverse else jj
        nr = ar * sr - ai * si + dr_ref[tile(j), :]
        ni = ar * si + ai * sr + di_ref[tile(j), :]
        sr_ref[tile(j), :] = nr
        si_ref[tile(j), :] = ni
        return nr, ni

    er, ei = lax.fori_loop(0, T, step, (zero, zero), unroll=4)

    pr, pi = ar[0:1], ai[0:1]
    for _ in range(8):
        pr, pi = pr * pr - pi * pi, 2.0 * pr * pi
    rows = lax.broadcasted_iota(jnp.int32, (SCAN_CHUNKS, bl), 0)
    cr, ci = zero, zero
    xr = jnp.zeros((1, bl), F32)
    xi = jnp.zeros((1, bl), F32)
    order = range(SCAN_CHUNKS - 2, -1, -1) if reverse else range(1, SCAN_CHUNKS)
    for c in order:
        src = c + 1 if reverse else c - 1
        nxr = pr * xr - pi * xi + er[src:src + 1]
        nxi = pr * xi + pi * xr + ei[src:src + 1]
        xr, xi = nxr, nxi
        cr = jnp.where(rows == c, xr, cr)
        ci = jnp.where(rows == c, xi, ci)

    def fix(jj, pw):
        pwr, pwi = pw
        j = T - 1 - jj if reverse else jj
        sr_ref[tile(j), :] = sr_ref[tile(j), :] + (pwr * cr - pwi * ci)
        si_ref[tile(j), :] = si_ref[tile(j), :] + (pwr * ci + pwi * cr)
        return pwr * ar - pwi * ai, pwr * ai + pwi * ar

    lax.fori_loop(0, T, fix, (ar, ai), unroll=4)


def _s5_forward(name, proj, wdt, cmt, lam_r, lam_i):
    bl = SCAN_LANES
    nblk = NS // bl
    cw = bl * GCH // NST
    assert cw == 128 and CB_U == 0

    def nt(a, b_):
        return lax.dot_general(a, b_, (((1,), (1,)), ((), ())), preferred_element_type=F32)

    def body(u_ref, wr_ref, wi_ref, cr_ref, ci_ref, ar_ref, ai_ref, s_ref, y_ref):
        sr_ref, si_ref = s_ref.at[0], s_ref.at[1]
        u = u_ref[...].astype(BF16)
        sr_ref[...] = nt(u, wr_ref[...])
        si_ref[...] = nt(u, wi_ref[...])
        _scan_block(sr_ref, si_ref, sr_ref, si_ref, ar_ref[...], ai_ref[...], False)
        y_ref[...] = nt(sr_ref[...].astype(BF16), cr_ref[...]) + nt(si_ref[...].astype(BF16), ci_ref[...])

    return pl.pallas_call(
        body, name=name, grid=(nblk,),
        in_specs=[pl.BlockSpec((SEQ, cw), lambda i: (0, i)),
                  pl.BlockSpec((bl, cw), lambda i: (i, i)), pl.BlockSpec((bl, cw), lambda i: (nblk + i, i)),
                  pl.BlockSpec((cw, bl), lambda i: (i, i)), pl.BlockSpec((cw, bl), lambda i: (SW // cw + i, i)),
                  pl.BlockSpec((1, bl), lambda i: (0, i)), pl.BlockSpec((1, bl), lambda i: (0, i))],
        out_specs=[pl.BlockSpec((2, SEQ, bl), lambda i: (0, 0, i)), pl.BlockSpec((SEQ, cw), lambda i: (0, i))],
        out_shape=[jax.ShapeDtypeStruct((2, SEQ, NS), F32), jax.ShapeDtypeStruct((SEQ, SW), F32)],
        compiler_params=_params("arbitrary"),
    )(proj, wdt, wdt, cmt, cmt, lam_r, lam_i)


S5_BWD_LANES = 512


def _s5_backward(name, dy0, proj, s, wdt, cmt, lam_r, lam_i):
    bl = S5_BWD_LANES
    nblk = NS // bl
    cw = 128
    per = cw // (bl * GCH // NST)
    assert per >= 1 and CB_U == 0

    def tn(a, b_):
        return lax.dot_general(a, b_, (((0,), (0,)), ((), ())), preferred_element_type=F32)

    def prev(s_ref):
        last = pltpu.roll(s_ref[SEQ - SCAN_CHUNKS:SEQ, :], 1, 0)
        first = jnp.where(lax.broadcasted_iota(jnp.int32, (SCAN_CHUNKS, bl), 0) > 0, last, 0.0)
        return jnp.concatenate([first, s_ref[0:SEQ - SCAN_CHUNKS, :]], axis=0)

    def body(dy_ref, u_ref, s_ref, cr_ref, ci_ref, wr_ref, wi_ref, lr_ref, li_ref,
             dlr_ref, dli_ref, dwdt_ref, dcmt_ref, du_ref, ar_ref, ai_ref):
        dy = dy_ref[...]
        ar_ref[...] = jnp.dot(dy, cr_ref[...], preferred_element_type=F32)
        ai_ref[...] = jnp.dot(dy, ci_ref[...], preferred_element_type=F32)
        _scan_block(ar_ref, ai_ref, ar_ref, ai_ref, lr_ref[...], -li_ref[...], True)
        a_r, a_i = ar_ref[...], ai_ref[...]
        sr_ref, si_ref = s_ref.at[0], s_ref.at[1]
        spr, spi = prev(sr_ref), prev(si_ref)
        dlr_ref[...] = jnp.sum(a_r * spr + a_i * spi, axis=0, keepdims=True)
        dli_ref[...] = jnp.sum(a_i * spr - a_r * spi, axis=0, keepdims=True)
        a_rb, a_ib = a_r.astype(BF16), a_i.astype(BF16)
        u = u_ref[...].astype(BF16)
        dwdt_ref[0] = tn(a_rb, u)
        dwdt_ref[1] = tn(a_ib, u)
        dcmt_ref[0] = tn(dy, sr_ref[...].astype(BF16))
        dcmt_ref[1] = tn(dy, si_ref[...].astype(BF16))
        part = (jnp.dot(a_rb, wr_ref[...], preferred_element_type=F32)
                + jnp.dot(a_ib, wi_ref[...], preferred_element_type=F32))

        @pl.when(pl.program_id(0) % per == 0)
        def _():
            du_ref[...] = part

        @pl.when(pl.program_id(0) % per > 0)
        def _():
            du_ref[...] += part

    lam_spec = pl.BlockSpec((1, bl), lambda i: (0, i))
    return pl.pallas_call(
        body, name=name, grid=(nblk,),
        in_specs=[pl.BlockSpec((SEQ, cw), lambda i: (0, i // per)), pl.BlockSpec((SEQ, cw), lambda i: (0, i // per)),
                  pl.BlockSpec((2, SEQ, bl), lambda i: (0, 0, i)),
                  pl.BlockSpec((cw, bl), lambda i: (i // per, i)),
                  pl.BlockSpec((cw, bl), lambda i: (SW // cw + i // per, i)),
                  pl.BlockSpec((bl, cw), lambda i: (i, i // per)), pl.BlockSpec((bl, cw), lambda i: (nblk + i, i // per)),
                  lam_spec, lam_spec],
        out_specs=[lam_spec, lam_spec, pl.BlockSpec((2, bl, cw), lambda i: (0, i, i // per)),
                   pl.BlockSpec((2, cw, bl), lambda i: (0, i // per, i)),
                   pl.BlockSpec((SEQ, cw), lambda i: (0, i // per))],
        out_shape=[jax.ShapeDtypeStruct((1, NS), F32), jax.ShapeDtypeStruct((1, NS), F32),
                   jax.ShapeDtypeStruct((2, NS, SW), F32), jax.ShapeDtypeStruct((2, SW, NS), F32),
                   jax.ShapeDtypeStruct((SEQ, SW), F32)],
        scratch_shapes=[pltpu.VMEM((SEQ, bl), F32)] * 2,
        compiler_params=_params("arbitrary"),
    )(dy0, proj, s, cmt, cmt, wdt, wdt, lam_r, lam_i)


def _scores(qb, kb, prev):
    s = lax.dot_general(qb, kb, (((1,), (1,)), ((), ())), preferred_element_type=F32) * (HD ** -0.5)
    row = lax.broadcasted_iota(jnp.int32, (ABLK, ABLK), 0)
    col = lax.broadcasted_iota(jnp.int32, (ABLK, ABLK), 1)
    return jnp.where((col >= row) if prev else (col <= row), s, -1e30)


def _block_rows(dil, r, b):
    if dil == 1:
        return pl.ds(pl.multiple_of(b * ABLK, ABLK), ABLK)
    return pl.ds(r + dil * ABLK * b, ABLK, stride=dil)


def _group_blocks(dil):
    nb = SEQ // dil // ABLK
    shift = nb.bit_length() - 1
    return nb, (lambda idx: (idx >> shift, idx & (nb - 1)))


def _qkv_specs(j_of):
    return [pl.BlockSpec((SEQ, HD), functools.partial(lambda j, c: (0, c + j_of(j)), c=(cb + g) * 4))
            for g in range(3) for cb in (CB_Q, CB_K, CB_V)]


def _attention_fwd(name, proj):
    def body(*refs):
        qkv, z_ref = refs[:9], refs[9]
        y_ref, ya_ref, l_ref = refs[10:13]
        accs, maxs, dens = refs[13:16], refs[16:19], refs[19:22]
        for g, dil in enumerate(DILATIONS):
            q_ref, k_ref, v_ref = qkv[3 * g:3 * g + 3]
            nb, where = _group_blocks(dil)

            def step(t, c, g=g, dil=dil, nb=nb, where=where, q_ref=q_ref, k_ref=k_ref, v_ref=v_ref):
                two = range(ATTN_PAIR)
                rb = [where(t + i * (SEQ // ABLK // ATTN_PAIR)) for i in two]
                rows = [_block_rows(dil, r, b) for r, b in rb]
                qb = [q_ref[rows[i], :].astype(BF16) for i in two]
                s_c = [_scores(qb[i], k_ref[rows[i], :].astype(BF16), False) for i in two]
                if nb > 1:
                    prev = [_block_rows(dil, r, jnp.maximum(b - 1, 0)) for r, b in rb]
                    s_p = [jnp.where(rb[i][1] > 0, _scores(qb[i], k_ref[prev[i], :].astype(BF16), True), -1e30)
                           for i in two]
                m = [jnp.max(s_c[i], axis=-1, keepdims=True) for i in two]
                if nb > 1:
                    m = [jnp.maximum(m[i], jnp.max(s_p[i], axis=-1, keepdims=True)) for i in two]
                p_c = [jnp.exp(s_c[i] - m[i]) for i in two]
                den = [jnp.sum(p_c[i], axis=-1, keepdims=True) for i in two]
                acc = [jnp.dot(p_c[i].astype(BF16), v_ref[rows[i], :].astype(BF16), preferred_element_type=F32)
                       for i in two]
                if nb > 1:
                    p_p = [jnp.exp(s_p[i] - m[i]) for i in two]
                    den = [den[i] + jnp.sum(p_p[i], axis=-1, keepdims=True) for i in two]
                    acc = [acc[i] + jnp.dot(p_p[i].astype(BF16), v_ref[prev[i], :].astype(BF16),
                                            preferred_element_type=F32) for i in two]
                for i in two:
                    accs[g][rows[i], :] = acc[i]
                    maxs[g][rows[i], :] = jnp.broadcast_to(m[i], (ABLK, HD))
                    dens[g][rows[i], :] = jnp.broadcast_to(den[i], (ABLK, HD))
                return c

            lax.fori_loop(0, SEQ // ABLK // ATTN_PAIR, step, 0)
        top = jnp.maximum(jnp.maximum(maxs[0][...], maxs[1][...]), maxs[2][...])
        den = jnp.zeros((SEQ, HD), F32)
        y = jnp.zeros((SEQ, HD), F32)
        for g in range(3):
            wgt = jnp.exp(maxs[g][...] - top)
            den = den + wgt * dens[g][...]
            y = y + wgt * accs[g][...]
        y = y / den
        y_ref[...] = y
        ya_ref[...] = (y * _silu(z_ref[...])).astype(ya_ref.dtype)
        l_ref[...] = top + jnp.log(den)

    ospec = pl.BlockSpec((SEQ, HD), lambda j: (0, j))
    return pl.pallas_call(
        body, name=name, grid=(AW // HD,),
        in_specs=_qkv_specs(lambda j: j) + [pl.BlockSpec((SEQ, HD), lambda j: (0, CB_ZA * 4 + j))],
        out_specs=[ospec, ospec, ospec],
        out_shape=[jax.ShapeDtypeStruct((SEQ, AW), F32), jax.ShapeDtypeStruct((SEQ, AW), BF16),
                   jax.ShapeDtypeStruct((SEQ, AW), F32)],
        scratch_shapes=[pltpu.VMEM((SEQ, HD), F32)] * 9,
        compiler_params=_params("parallel"),
    )(*([proj] * 10))


def _attention_bwd(name, proj, dya, y, lse):
    def tn(a, b_):
        return lax.dot_general(a, b_, (((0,), (0,)), ((), ())), preferred_element_type=F32)

    def nt(a, b_):
        return lax.dot_general(a, b_, (((1,), (1,)), ((), ())), preferred_element_type=F32)

    def body(*refs):
        qkv, z_ref, dya_ref, y_ref, l_ref = refs[:9], refs[9], refs[10], refs[11], refs[12]
        outs, dza_ref = refs[13:22], refs[22]
        dy_s, dsum_s, dq_s, dk_own, dv_own, dk_prev, dv_prev = refs[23:]
        _, vjp = jax.vjp(lambda y_, z_: y_ * _silu(z_), y_ref[...], z_ref[...])
        dy, dz = vjp(dya_ref[...])
        dza_ref[...] = dz.astype(dza_ref.dtype)
        dy_s[...] = dy
        dsum_s[...] = jnp.broadcast_to(jnp.sum(dy * y_ref[...], axis=-1, keepdims=True), (SEQ, HD))
        for g, dil in enumerate(DILATIONS):
            q_ref, k_ref, v_ref = qkv[3 * g:3 * g + 3]
            nb, where = _group_blocks(dil)
            if nb > 1:
                dk_prev[...] = jnp.zeros(dk_prev.shape, F32)
                dv_prev[...] = jnp.zeros(dv_prev.shape, F32)

            def step(t, c, dil=dil, nb=nb, where=where, q_ref=q_ref, k_ref=k_ref, v_ref=v_ref):
                rb = [where(t + i * (SEQ // ABLK // ATTN_PAIR)) for i in range(ATTN_PAIR)]
                sides = []
                for r, b in rb:
                    rows = _block_rows(dil, r, b)
                    own = dict(b=b, qrows=rows, krows=rows, prev=False, dk=dk_own, dv=dv_own,
                               q=q_ref[rows, :].astype(BF16), dy=dy_s[rows, :].astype(BF16))
                    sides.append(own)
                    if nb > 1:
                        sides.append(dict(own, krows=_block_rows(dil, r, jnp.maximum(b - 1, 0)), prev=True,
                                          dk=dk_prev, dv=dv_prev))
                for s_ in sides:
                    s_["k"] = k_ref[s_["krows"], :].astype(BF16)
                    s_["v"] = v_ref[s_["krows"], :].astype(BF16)
                for s_ in sides:
                    sc = _scores(s_["q"], s_["k"], s_["prev"])
                    s_["s"] = jnp.where(s_["b"] > 0, sc, -1e30) if s_["prev"] else sc
                    s_["dp"] = nt(s_["dy"], s_["v"])
                for s_ in sides:
                    p = jnp.exp(s_["s"] - l_ref[s_["qrows"], :])
                    s_["p"] = p.astype(BF16)
                    s_["ds"] = (p * (s_["dp"] - dsum_s[s_["qrows"], :]) * (HD ** -0.5)).astype(BF16)
                for s_ in sides:
                    s_["dk"][s_["krows"], :] = tn(s_["ds"], s_["q"])
                    s_["dv"][s_["krows"], :] = tn(s_["p"], s_["dy"])
                    s_["dq"] = jnp.dot(s_["ds"], s_["k"], preferred_element_type=F32)
                per = len(sides) // ATTN_PAIR
                for i in range(ATTN_PAIR):
                    dq = sides[i * per]["dq"]
                    if per > 1:
                        dq = dq + sides[i * per + 1]["dq"]
                    dq_s[sides[i * per]["qrows"], :] = dq
                return c

            lax.fori_loop(0, SEQ // ABLK // ATTN_PAIR, step, 0)
            dq_ref, dk_ref, dv_ref = outs[3 * g:3 * g + 3]
            dq_ref[...] = dq_s[...].astype(dq_ref.dtype)
            if nb > 1:
                dk_ref[...] = (dk_own[...] + dk_prev[...]).astype(dk_ref.dtype)
                dv_ref[...] = (dv_own[...] + dv_prev[...]).astype(dv_ref.dtype)
            else:
                dk_ref[...] = dk_own[...].astype(dk_ref.dtype)
                dv_ref[...] = dv_own[...].astype(dv_ref.dtype)

    ospec = pl.BlockSpec((SEQ, HD), lambda j: (0, j))
    outs = pl.pallas_call(
        body, name=name, grid=(AW // HD,),
        in_specs=_qkv_specs(lambda j: j) + [pl.BlockSpec((SEQ, HD), lambda j: (0, CB_ZA * 4 + j))] + [ospec] * 3,
        out_specs=[ospec] * 10, out_shape=[jax.ShapeDtypeStruct((SEQ, AW), BF16)] * 10,
        scratch_shapes=[pltpu.VMEM((SEQ, HD), F32)] * 7,
        compiler_params=_params("parallel"),
    )(*([proj] * 10), dya, y, lse)
    return outs[:9], outs[9]


def _rms(x, g):
    return x * lax.rsqrt(jnp.mean(x * x, axis=-1, keepdims=True) + RMS_EPS) * g


def _sig(x):
    return 1.0 / (1.0 + jnp.exp(-x))


def _silu(x):
    return x * _sig(x)


def _gelu(x):
    return 0.5 * x * (1.0 + jnp.tanh(math.sqrt(2.0 / math.pi) * (x + 0.044715 * (x * x * x))))


def _y1_fn(y0p, u, dskip):
    return _gelu(y0p + dskip * u)


def _ys_fn(y1, t, z, bglu):
    return y1 * _sig(t + bglu) * _silu(z)


def _merge_fn(ms, ma, gs, ga):
    return _sig(gs) * ms.astype(F32) + _sig(ga) * ma.astype(F32)


def _colsum(v):
    return jnp.sum(v, axis=0, keepdims=True)


def _lam_fn(lre, lim, ldt):
    a = jnp.minimum(lre, -1e-4)
    dt = jnp.exp(ldt)
    mag = jnp.exp(a * dt)
    ar = mag * jnp.cos(lim * dt)
    ai = mag * jnp.sin(lim * dt)
    den = a * a + lim * lim
    cr = ((ar - 1.0) * a + ai * lim) / den
    ci = (ai * a - (ar - 1.0) * lim) / den
    return ar, ai, cr, ci


def _bbar_fn(cr, ci, bre, bim):
    return cr * bre - ci * bim, cr * bim + ci * bre


def _same_group(rows, a, cols, b):
    r = lax.broadcasted_iota(jnp.int32, (rows, cols), 0) >> (a.bit_length() - 1)
    c = lax.broadcasted_iota(jnp.int32, (rows, cols), 1) >> (b.bit_length() - 1)
    return r == c


def _expand(name, blocks, signs, a, b, dtype, after=None):
    rows, cols = NGRP * a, NGRP * b
    n = len(blocks)
    assert a & (a - 1) == 0 and b & (b - 1) == 0
    after_specs, after_args = _after(after)

    def body(*refs):
        o_ref = refs[-1]
        tile = (lax.broadcasted_iota(jnp.int32, (b, cols), 1) & (b - 1)
                == lax.broadcasted_iota(jnp.int32, (b, cols), 0)).astype(F32)
        keep = _same_group(rows, a, cols, b)
        for i, (ref, sign) in enumerate(zip(refs[:n], signs)):
            spread = jnp.dot(ref[...], tile, preferred_element_type=F32, precision=lax.Precision.HIGHEST)
            o_ref[i * rows:(i + 1) * rows, :] = jnp.where(keep, sign * spread, 0.0).astype(o_ref.dtype)

    return pl.pallas_call(body, name=name, grid=(1,),
                          in_specs=[pl.BlockSpec((rows, b), lambda i: (0, 0))] * n + after_specs,
                          out_specs=pl.BlockSpec((n * rows, cols), lambda i: (0, 0)),
                          out_shape=jax.ShapeDtypeStruct((n * rows, cols), dtype),
                          compiler_params=_params("arbitrary"))(*blocks, *after_args)


def _extract(name, m, a, b, ats, after=None):
    rows, cols = NGRP * a, NGRP * b
    n = len(ats)
    assert a & (a - 1) == 0 and b & (b - 1) == 0
    after_specs, after_args = _after(after)

    def body(*refs):
        tile = (lax.broadcasted_iota(jnp.int32, (cols, b), 0) & (b - 1)
                == lax.broadcasted_iota(jnp.int32, (cols, b), 1)).astype(F32)
        keep = _same_group(rows, a, cols, b)
        for m_ref, o_ref in zip(refs[:n], refs[-n:]):
            kept = jnp.where(keep, m_ref[...], 0.0)
            o_ref[...] = jnp.dot(kept, tile, preferred_element_type=F32, precision=lax.Precision.HIGHEST)

    return pl.pallas_call(
        body, name=name, grid=(1,),
        in_specs=[pl.BlockSpec((rows, cols), functools.partial(lambda i, at: at, at=at)) for at in ats] + after_specs,
        out_specs=[pl.BlockSpec((rows, b), lambda i: (0, 0))] * n,
        out_shape=[jax.ShapeDtypeStruct((rows, b), F32)] * n,
        compiler_params=_params("arbitrary"))(*([m] * n), *after_args)


def _s5_prepare(l, sp, after):
    tag = f"l{l}_"
    ar, ai, cr, ci = _ew(tag + "lam", _lam_fn, NGRP, NGRP,
                         [_ri(sp["lambda_re"]), _ri(sp["lambda_im"]),
                          _ri(sp["log_dt"].reshape(NGRP, 1) + after[0, 0])], [], [(NST, F32)] * 4)
    bre = sp["b_re"].reshape(NS, GCH)
    bim = sp["b_im"].reshape(NS, GCH)
    bbr, bbi = _ew(tag + "bbar", _bbar_fn, NS, NS, [_ri(cr.reshape(NS, 1)), _ri(ci.reshape(NS, 1)), _ri(bre), _ri(bim)],
                   [], [(GCH, F32)] * 2)
    wdt = _expand(tag + "wdt", [bbr, bbi], [1.0, 1.0], NST, GCH, BF16)
    cmt = _expand(tag + "cmt", [sp["c_re"].reshape(SW, NST), sp["c_im"].reshape(SW, NST)], [1.0, -1.0], GCH, NST, BF16,
                  after=after)
    return dict(ar=ar, ai=ai, cr=cr, ci=ci, wdt=wdt, cmt=cmt)


def _layer_head(l, x, sp):
    g1 = sp["pre_norm_g"].reshape(1, DM)
    (h,) = _ew(f"l{l}_rms1", lambda x_, g: (_rms(x_, g),), SEQ, 256, [_ri(x)], [g1], [(DM, BF16)])
    return jnp.stack([h, _to_chunked(h)])


def _layer_fwd(l, x, hv, w, sp, prep, after):
    tag = f"l{l}_"
    win = w["w_in"]
    proj = _mm(tag + "proj", hv, win, "nn", SEQ, NCOL, DM, SEQ, 512, 1024, F32,
               a_spec=pl.BlockSpec((None, SEQ, 1024), lambda i, j, k: (_row_order(j), 0, 0)),
               b_spec=pl.BlockSpec((None, 1024, 512), lambda i, j, k: (j // 2, 0, j % 2)), after=after)

    ar, ai, cr, ci, wdt, cmt = (prep[k] for k in ("ar", "ai", "cr", "ci", "wdt", "cmt"))
    s, y0p = _s5_forward(tag + "s5", proj, wdt, cmt, ar.reshape(1, NS), ai.reshape(1, NS))
    dskip = sp["d_skip"].reshape(1, SW)
    (y1,) = _ew(tag + "y1", lambda a, u, d: (_y1_fn(a, u, d),), SEQ, 256,
                [_ri(y0p), _ri(proj, SW, CB_U)], [dskip], [(SW, F32)])
    t = _mm(tag + "glu", y1, w["w_glu"], "nn", SEQ, SW, SW, 1024, 512, 512, F32)
    bglu = sp["b_glu"].reshape(1, SW)
    (ys_c,) = _ew(tag + "ys", lambda y1_, t_, z, b_: (_ys_fn(y1_, t_, z, b_),), SEQ, 256,
                  [_ri(y1), _ri(t), _ri(proj, SW, CB_ZS)], [bglu], [(SW, BF16)])
    ys = _from_chunked(ys_c)

    ypre, ya, lse = _attention_fwd(tag + "attn", proj)

    ms = _mm(tag + "branch_s", ys, w["w_branch_s"], "nn", SEQ, DM, SW, 1024, 1024, 512, BF16)
    ma = _mm(tag + "branch_a", ya, w["w_branch_a"], "nn", SEQ, DM, AW, 1024, 1024, 512, BF16)
    (merged,) = _ew(tag + "merge", lambda a, b_, c, d: (_merge_fn(a, b_, c, d),), SEQ, 256,
                    [_ri(ms), _ri(ma), _ri(proj, DM, CB_GS), _ri(proj, DM, CB_GA)], [], [(DM, BF16)])
    out = _mm(tag + "out", merged, w["w_out"], "nn", SEQ, DM, DM, 1024, 1024, 1024, F32)
    g2 = sp["post_norm_g"].reshape(1, DM)
    (x_new,) = _ew(tag + "post", lambda x_, o, g: (x_ + _rms(o, g),), SEQ, 256, [_ri(x), _ri(out)], [g2], [(DM, F32)])
    res = dict(x=x, hv=hv, proj=proj, ar=ar, ai=ai, cr=cr, ci=ci, wdt=wdt, cmt=cmt, s=s, y0p=y0p,
               y1=y1, t=t, ys=ys, ya=ya, ypre=ypre, lse=lse, ms=ms, ma=ma, merged=merged, out=out)
    return x_new, res


def _layer_bwd(l, dxn, r, w, sp, big_done, after=None):
    tag = f"l{l}b_"
    proj = r["proj"]
    g1 = sp["pre_norm_g"].reshape(1, DM)
    g2 = sp["post_norm_g"].reshape(1, DM)
    dskip = sp["d_skip"].reshape(1, SW)
    bglu = sp["b_glu"].reshape(1, SW)

    def post_b(d, o, g):
        _, vjp = jax.vjp(_rms, o, g)
        do, dg = vjp(d)
        return do, dg

    d_out, dg2 = _ew(tag + "post", post_b, SEQ, 256, [_ri(dxn), _ri(r["out"])], [g2], [(DM, BF16)], [DM])
    dw_out = _mm(tag + "dw_out", r["merged"], d_out, "tn", DM, DM, SEQ, 1024, 1024, SEQ, BF16, after=after)
    dmerged = _mm(tag + "dmerged", d_out, w["w_out"], "nt", SEQ, DM, DM, 1024, 1024, 1024, F32, after=after)

    def merge_b(d, ms, ma, gs, ga):
        _, vjp = jax.vjp(_merge_fn, ms, ma, gs, ga)
        return vjp(d)

    dms, dma, dgs, dga = _ew(tag + "merge", merge_b, SEQ, 256,
                             [_ri(dmerged), _ri(r["ms"]), _ri(r["ma"]), _ri(proj, DM, CB_GS), _ri(proj, DM, CB_GA)],
                             [], [(DM, BF16)] * 4)
    dw_bs = _mm(tag + "dw_bs", r["ys"], dms, "tn", SW, DM, SEQ, 512, 1024, SEQ, BF16)
    dw_ba = _mm(tag + "dw_ba", r["ya"], dma, "tn", AW, DM, SEQ, 512, 1024, SEQ, BF16)
    dys = _mm(tag + "dys", dms, w["w_branch_s"], "nt", SEQ, SW, DM, 1024, 512, 1024, F32)
    dya = _mm(tag + "dya", dma, w["w_branch_a"], "nt", SEQ, AW, DM, 1024, 512, 1024, F32)

    dqkv, dza = _attention_bwd(tag + "attn", proj, dya, r["ypre"], r["lse"])

    def ys_b(d, y1, t, z, b_):
        _, vjp = jax.vjp(_ys_fn, y1, t, z, b_)
        dy1, dt, dz, _ = vjp(d)
        return dy1, dt, dz, _colsum(dt)

    dy1a, dt, dzs, dbglu = _ew(tag + "ys", ys_b, SEQ, 256,
                               [_ri(_to_chunked(dys)), _ri(r["y1"]), _ri(r["t"]), _ri(proj, SW, CB_ZS)],
                               [bglu], [(SW, F32), (SW, BF16), (SW, BF16)], [SW])
    dw_glu = _mm(tag + "dw_glu", r["y1"], dt, "tn", SW, SW, SEQ, 512, 512, SEQ, BF16)
    dy1b = _mm(tag + "dy1b", dt, w["w_glu"], "nt", SEQ, SW, SW, 1024, 512, 512, F32)

    def y1_b(da, db, y0p, u, d_):
        _, vjp = jax.vjp(_y1_fn, y0p, u, d_)
        dy0, du, dd = vjp(da + db)
        return dy0, du, dd

    dy0, du_skip, ddskip = _ew(tag + "y1", y1_b, SEQ, 256,
                               [_ri(dy1a), _ri(dy1b), _ri(r["y0p"]), _ri(proj, SW, CB_U)], [dskip],
                               [(SW, BF16), (SW, F32)], [SW])
    dlr, dli, dwdt, dcmt, du_s = _s5_backward(tag + "s5", dy0, proj, r["s"], r["wdt"], r["cmt"],
                                              r["ar"].reshape(1, NS), r["ai"].reshape(1, NS))
    (du,) = _ew(tag + "du", lambda a, c: (a + c,), SEQ, 256, [_ri(du_s), _ri(du_skip)], [], [(SW, BF16)])

    dq, dk, dv = ([dqkv[3 * g + i] for g in range(3)] for i in range(3))
    dproj = jnp.concatenate([du, dzs, *dq, *dk, *dv, dza, dgs, dga], axis=1)
    dw_in = _mm(tag + "dw_in", r["hv"], dproj, "tn", DM, NCOL, SEQ, 1024, 512, SEQ, BF16,
                a_spec=pl.BlockSpec((None, SEQ, 1024), lambda i, j, k: (_row_order(j), 0, 0)),
                o_spec=pl.BlockSpec((None, 1024, 512), lambda i, j, k: (j // 2, 0, j % 2)), out_shape=(NDEV, DM, DM))
    tok = big_done(l, dict(w_in=dw_in, w_glu=dw_glu.reshape(NDEV, SW // NDEV, SW),
                           w_branch_s=dw_bs.reshape(SW, NDEV, DM // NDEV).transpose(1, 0, 2),
                           w_branch_a=dw_ba.reshape(AW, NDEV, DM // NDEV).transpose(1, 0, 2),
                           w_out=dw_out.reshape(NDEV, DM // NDEV, DM)))
    if tok is not None:
        g1 = g1 + tok[0, 0]

    dwdt = dwdt.reshape(2 * NS, SW)
    dcmt = dcmt.reshape(2 * SW, NS)
    dbbr, dbbi = _extract(tag + "dbb", dwdt, NST, GCH, ((0, 0), (1, 0)), after=tok)
    bre = sp["b_re"].reshape(NS, GCH)
    bim = sp["b_im"].reshape(NS, GCH)

    def bbar_b(cr, ci, br_, bi_, dr, di):
        _, vjp = jax.vjp(_bbar_fn, cr, ci, br_, bi_)
        return vjp((dr, di))

    dcr, dci, dbre, dbim = _ew(tag + "bbar", bbar_b, NS, NS,
                               [_ri(r["cr"].reshape(NS, 1)), _ri(r["ci"].reshape(NS, 1)), _ri(bre), _ri(bim),
                                _ri(dbbr), _ri(dbbi)], [], [(1, F32), (1, F32), (GCH, F32), (GCH, F32)])

    def lam_b(lre, lim, ldt, dar, dai, dcr_, dci_):
        _, vjp = jax.vjp(_lam_fn, lre, lim, ldt)
        return vjp((dar, dai, dcr_, dci_))

    dlre, dlim, dldt = _ew(tag + "lam", lam_b, NGRP, NGRP,
                           [_ri(sp["lambda_re"]), _ri(sp["lambda_im"]), _ri(sp["log_dt"].reshape(NGRP, 1)),
                            _ri(dlr.reshape(NGRP, NST)), _ri(dli.reshape(NGRP, NST)),
                            _ri(dcr.reshape(NGRP, NST)), _ri(dci.reshape(NGRP, NST))], [],
                           [(NST, F32), (NST, F32), (1, F32)])
    dc_re, dc_im = _extract(tag + "dc", dcmt, GCH, NST, ((0, 0), (1, 0)), after=tok)
    dc_re, dc_im = dc_re.reshape(NGRP, GCH, NST), -dc_im.reshape(NGRP, GCH, NST)

    dh_time = _mm(tag + "dh_time", dproj, w["w_in"], "nt", SEQ, DM, NCOL - DM, SEQ, 1024, 1024, F32,
                  a_spec=pl.BlockSpec((SEQ, 1024), lambda i, j, k: (0, 1 + k)),
                  b_spec=pl.BlockSpec((None, 1024, 1024), lambda i, j, k: (1 + k, 0, 0)), after=tok)
    dh_chunked = _mm(tag + "dh_chunked", dproj, w["w_in"], "nt", SEQ, DM, DM, SEQ, 1024, 1024, F32,
                     a_spec=pl.BlockSpec((SEQ, 1024), lambda i, j, k: (0, 0)),
                     b_spec=pl.BlockSpec((None, 1024, 1024), lambda i, j, k: (0, 0, 0)), after=tok)
    dh = [dh_time, _from_chunked(dh_chunked)]

    def pre_b(d, dh0, dh1, x_, g):
        _, vjp = jax.vjp(_rms, x_, g)
        dx_, dg = vjp(dh0 + dh1)
        return d + dx_, dg

    dx, dg1 = _ew(tag + "pre", pre_b, SEQ, 256, [_ri(dxn)] + [_ri(t_) for t_ in dh] + [_ri(r["x"])], [g1],
                  [(DM, F32)], [DM])

    small = dict(pre_norm_g=dg1.reshape(DM), lambda_re=dlre, lambda_im=dlim, log_dt=dldt.reshape(NGRP),
                 b_re=dbre.reshape(NGRP, NST, GCH), b_im=dbim.reshape(NGRP, NST, GCH), c_re=dc_re, c_im=dc_im,
                 d_skip=ddskip.reshape(SW), b_glu=dbglu.reshape(SW), post_norm_g=dg2.reshape(DM))
    return dx, small


_HBM = pl.BlockSpec(memory_space=pltpu.HBM)
_SEM = pl.BlockSpec(memory_space=pltpu.SEMAPHORE)
_EFFECT = pltpu.SideEffectType.DATAFLOW_SIDE_EFFECTING


def _remote_copies(srcs, dsts, send_sems, recv_sems, gather):
    x, y, c = lax.axis_index("x"), lax.axis_index("y"), lax.axis_index("c")
    me = 4 * x + 2 * y + c
    copies = []
    for i in range(len(srcs)):
        for k in range(1, NDEV):
            peer = (x ^ (k >> 2), y ^ ((k >> 1) & 1), c ^ (k & 1))
            src = srcs[i] if gather[i] else srcs[i].at[me ^ k]
            copies.append(pltpu.make_async_remote_copy(
                src_ref=src, dst_ref=dsts[i].at[me], send_sem=send_sems[i], recv_sem=recv_sems[i],
                device_id=peer, device_id_type=pl.DeviceIdType.MESH))
    return copies


def _all_seven(dst, send_sem, recv_sem):
    seven = dst.at[pl.ds(0, NDEV - 1)]
    me = (lax.axis_index("x"), lax.axis_index("y"), lax.axis_index("c"))
    return pltpu.make_async_remote_copy(src_ref=seven, dst_ref=seven, send_sem=send_sem, recv_sem=recv_sem,
                                        device_id=me, device_id_type=pl.DeviceIdType.MESH)


def _own_slabs(name, arrs, gather, after):
    n = len(arrs)
    me = (4 * lax.axis_index("x") + 2 * lax.axis_index("y") + lax.axis_index("c")).astype(jnp.int32).reshape(1)

    def body(me_ref, *refs):
        for src, dst in zip(refs[:n], refs[n + 1:]):
            dst[...] = src[...]

    def zeros(k):
        return (0,) * k

    in_specs, out_specs, out_shape = [], [], []
    for a, g in zip(arrs, gather):
        slab = a.shape if g else a.shape[1:]
        nd = len(slab)
        if g:
            in_specs.append(pl.BlockSpec(slab, functools.partial(lambda i, me_ref, nd: zeros(nd), nd=nd)))
        else:
            in_specs.append(pl.BlockSpec((None,) + slab, functools.partial(lambda i, me_ref, nd: (me_ref[0],) + zeros(nd), nd=nd)))
        out_specs.append(pl.BlockSpec((None,) + slab, functools.partial(lambda i, me_ref, nd: (me_ref[0],) + zeros(nd), nd=nd)))
        out_shape.append(jax.ShapeDtypeStruct((NDEV,) + slab, a.dtype))
    in_specs.append(pl.BlockSpec(memory_space=pl.ANY))
    return pl.pallas_call(
        body, name=name, out_shape=out_shape,
        grid_spec=pltpu.PrefetchScalarGridSpec(num_scalar_prefetch=1, grid=(1,), in_specs=in_specs, out_specs=out_specs),
        compiler_params=_params("arbitrary"),
    )(me, *arrs, after)


def _exchange_start(name, arrs, gather, after):
    n = len(arrs)
    lands = _own_slabs(name + "_own", arrs, gather, after)

    def body(*refs):
        srcs, dsts = refs[:n], refs[n:2 * n]
        send_sems, recv_sems = refs[2 * n:3 * n], refs[3 * n:4 * n]
        token = refs[-1]
        for cp in _remote_copies(srcs, dsts, send_sems, recv_sems, gather):
            cp.start()
        token[...] = jnp.zeros(token.shape, token.dtype)

    thru = [pltpu.HBM(a.shape, a.dtype) for a in list(arrs) + list(lands)]
    outs = pl.pallas_call(
        body, name=name,
        out_shape=(*[pltpu.SemaphoreType.DMA(())] * (2 * n), *thru, jax.ShapeDtypeStruct((8, 128), F32)),
        in_specs=[_HBM] * (2 * n),
        out_specs=(*[_SEM] * (2 * n), *[_HBM] * (2 * n), pl.BlockSpec(memory_space=pltpu.VMEM)),
        input_output_aliases={i: 2 * n + i for i in range(2 * n)},
        compiler_params=pltpu.CompilerParams(has_side_effects=_EFFECT),
    )(*[pltpu.with_memory_space_constraint(a, pltpu.HBM) for a in list(arrs) + list(lands)])
    return dict(send=outs[:n], recv=outs[n:2 * n], srcs=outs[2 * n:3 * n], lands=outs[3 * n:4 * n], token=outs[-1],
                gather=gather)


def _exchange_wait(name, started, after):
    n = len(started["srcs"])
    after = list(after)

    def body(*refs):
        dsts = refs[n:2 * n]
        send_sems, recv_sems = refs[2 * n:3 * n], refs[3 * n:4 * n]
        for i in range(n):
            cp = _all_seven(dsts[i], send_sems[i], recv_sems[i])
            cp.wait_send()
            cp.wait_recv()

    bufs = list(started["srcs"]) + list(started["lands"])
    outs = pl.pallas_call(
        body, name=name, out_shape=tuple(pltpu.HBM(a.shape, a.dtype) for a in bufs),
        in_specs=[_HBM] * (2 * n) + [_SEM] * (2 * n) + [pl.BlockSpec(memory_space=pl.ANY)] * len(after),
        out_specs=(_HBM,) * (2 * n), input_output_aliases={i: i for i in range(2 * n)},
        compiler_params=pltpu.CompilerParams(has_side_effects=_EFFECT),
    )(*bufs, *started["send"], *started["recv"], *after)
    return outs[n:]


def _sum_in_order(parts):
    g = parts[0].astype(F32)
    for p in parts[1:]:
        g = g + p.astype(F32)
    return g


def _adam_update(g, w_, m_, v_):
    m2 = B1 * m_ + (1.0 - B1) * g
    v2 = B2 * v_ + (1.0 - B2) * (g * g)
    m_hat = m2 / (1.0 - B1 ** STEP)
    v_hat = v2 / (1.0 - B2 ** STEP)
    delta = -LR * (m_hat / (jnp.sqrt(v_hat) + ADAM_EPS) + WD * w_)
    return g, delta, m2, v2


def _adamw(name, g, w, m, v, br):
    rows, cols = w.shape
    return _ew(name, _adam_update, rows, br, [_ri(g), _ri(w), _ri(m), _ri(v)], [], [(cols, F32)] * 4)


def _adamw_layer(name, l, parts, w, m, v, br, outs):
    rows, cols = w.shape
    nb = rows // DEPTH // br
    assert nb * br * DEPTH == rows

    def body(*refs):
        vals = _adam_update(_sum_in_order([r[...] for r in refs[:NDEV]]), *[r[...] for r in refs[NDEV:NDEV + 3]])
        for r, val in zip(refs[-4:], vals):
            r[...] = val

    mine = pl.BlockSpec((br, cols), lambda i: (l * nb + i, 0))
    in_specs = [pl.BlockSpec((br, cols), functools.partial(lambda i, d: (d * nb + i, 0), d=d)) for d in range(NDEV)]
    in_specs += [mine] * 3 + [pl.BlockSpec(memory_space=pl.ANY)] * len(outs)
    return pl.pallas_call(
        body, name=f"{name}{l}", grid=(nb,), in_specs=in_specs, out_specs=[mine] * 4,
        out_shape=[jax.ShapeDtypeStruct((rows, cols), F32)] * 4,
        input_output_aliases={NDEV + 3 + q: q for q in range(len(outs))},
        compiler_params=_params("arbitrary"),
    )(*([parts] * NDEV), w, m, v, *outs)


SMALL = ("pre_norm_g", "lambda_re", "lambda_im", "log_dt", "b_re", "b_im", "c_re", "c_im", "d_skip", "b_glu",
         "post_norm_g")
BIG = ("w_in", "w_glu", "w_branch_s", "w_branch_a", "w_out")
WEIGHTS = ("pre_norm_g", "w_in", "lambda_re", "lambda_im", "log_dt", "b_re", "b_im", "c_re", "c_im", "d_skip",
           "w_glu", "b_glu", "w_branch_s", "w_branch_a", "w_out", "post_norm_g")
PACK_COLS = 1024
PACK_BR = 136


def _pack_layer(d):
    pieces = [d[k].astype(F32).reshape(-1) for k in SMALL]
    used = sum(p.shape[0] for p in pieces)
    assert used <= PACK_BR * PACK_COLS
    return jnp.concatenate(pieces + [jnp.zeros((PACK_BR * PACK_COLS - used,), F32)]).reshape(PACK_BR, PACK_COLS)


def _unpack(p, like):
    flat = p.reshape(DEPTH, PACK_BR * PACK_COLS)
    out, off = {}, 0
    for k in SMALL:
        n = like[k].size // DEPTH
        out[k] = flat[:, off:off + n].reshape(like[k].shape)
        off += n
    return out


def _local_step(x, target, small, started, weights_of, big_done, small_done, total_loss):
    preps = [_s5_prepare(l, {k: small[k][l] for k in SMALL}, started) for l in range(DEPTH)]
    res, ws = [], []
    for l in range(DEPTH):
        sp = {k: small[k][l] for k in SMALL}
        hv = _layer_head(l, x, sp)
        w_l, tok = weights_of(l, [hv] + ([p[k] for p in preps for k in ("wdt", "cmt")] if l == 0 else []))
        x, r = _layer_fwd(l, x, hv, w_l, sp, preps[l], tok)
        res.append(r)
        ws.append(w_l)

    def loss_fn(y, t):
        e = y - t
        return e * (1.0 / DM), jnp.sum(_colsum(0.5 * e * e * (1.0 / DM)), axis=1, keepdims=True)

    dx, loss = _ew("loss", loss_fn, SEQ, 256, [_ri(x), _ri(target)], [], [(DM, F32)], [1])
    total = total_loss(loss.reshape(()))
    for l in reversed(range(DEPTH)):
        dx, sm = _layer_bwd(l, dx, res[l], ws[l], {k: small[k][l] for k in SMALL}, big_done,
                            after=total.reshape(1, 1) if l == DEPTH - 1 else None)
        small_done(l, sm)
    return total, dx


def _full_weights(gathered):
    g = gathered
    return dict(
        w_in=g["w_in"],
        w_glu=g["w_glu"].reshape(SW, SW),
        w_branch_s=g["w_branch_s"].transpose(1, 0, 2).reshape(SW, DM),
        w_branch_a=g["w_branch_a"].transpose(1, 0, 2).reshape(AW, DM),
        w_out=g["w_out"].reshape(DM, DM),
    )


def kernel(x, pre_norm_g, w_in, lambda_re, lambda_im, log_dt, b_re, b_im, c_re, c_im, d_skip, w_glu, b_glu, w_branch_s, w_branch_a, w_out, post_norm_g, loss_target, m_pre_norm_g, m_w_in, m_lambda_re, m_lambda_im, m_log_dt, m_b_re, m_b_im, m_c_re, m_c_im, m_d_skip, m_w_glu, m_b_glu, m_w_branch_s, m_w_branch_a, m_w_out, m_post_norm_g, v_pre_norm_g, v_w_in, v_lambda_re, v_lambda_im, v_log_dt, v_b_re, v_b_im, v_c_re, v_c_im, v_d_skip, v_w_glu, v_b_glu, v_w_branch_s, v_w_branch_a, v_w_out, v_post_norm_g):
    wts = dict(pre_norm_g=pre_norm_g, w_in=w_in, lambda_re=lambda_re, lambda_im=lambda_im, log_dt=log_dt, b_re=b_re,
               b_im=b_im, c_re=c_re, c_im=c_im, d_skip=d_skip, w_glu=w_glu, b_glu=b_glu, w_branch_s=w_branch_s,
               w_branch_a=w_branch_a, w_out=w_out, post_norm_g=post_norm_g)
    mom = dict(pre_norm_g=m_pre_norm_g, w_in=m_w_in, lambda_re=m_lambda_re, lambda_im=m_lambda_im, log_dt=m_log_dt,
               b_re=m_b_re, b_im=m_b_im, c_re=m_c_re, c_im=m_c_im, d_skip=m_d_skip, w_glu=m_w_glu, b_glu=m_b_glu,
               w_branch_s=m_w_branch_s, w_branch_a=m_w_branch_a, w_out=m_w_out, post_norm_g=m_post_norm_g)
    var = dict(pre_norm_g=v_pre_norm_g, w_in=v_w_in, lambda_re=v_lambda_re, lambda_im=v_lambda_im, log_dt=v_log_dt,
               b_re=v_b_re, b_im=v_b_im, c_re=v_c_re, c_im=v_c_im, d_skip=v_d_skip, w_glu=v_w_glu, b_glu=v_b_glu,
               w_branch_s=v_w_branch_s, w_branch_a=v_w_branch_a, w_out=v_w_out, post_norm_g=v_post_norm_g)

    def gather_start(l, after):
        return _exchange_start(f"gather_start{l}", [wts[k][l].astype(BF16) for k in BIG], [True] * len(BIG), after)

    gathering = {0: gather_start(0, x)}
    sending, packed = {}, {}

    def weights_of(l, after):
        gathered = _exchange_wait(f"gather_wait{l}", gathering[l], after)
        tok = None
        if l + 1 < DEPTH:
            gathering[l + 1] = gather_start(l + 1, gathered[0])
            tok = gathering[l + 1]["token"]
        return _full_weights(dict(zip(BIG, gathered))), tok

    def big_done(l, big):
        arrs, kinds = [big[k] for k in BIG], [False] * len(BIG)
        if l + 1 < DEPTH:
            arrs, kinds = arrs + [packed[l + 1]], kinds + [True]
        sending[l] = _exchange_start(f"grads_start{l}", arrs, kinds, big["w_in"])
        return sending[l]["token"]

    def small_done(l, sm):
        packed[l] = _pack_layer(sm)

    loss, dx = _local_step(x[0], loss_target[0], wts, gathering[0]["token"], weights_of, big_done, small_done,
                           lambda part: lax.psum(part, ("x", "y", "c")))
    last = _exchange_start("grads_start_last", [packed[0]], [True], dx)

    grads, delta, new_m, new_v = {}, {}, {}, {}

    def as_rows(k):
        cols = wts[k].shape[-1]
        rows = wts[k].size // cols
        return rows, cols, [t[k].reshape(rows, cols) for t in (wts, mom, var)]

    recv_l, outs, after = {}, {k: [] for k in BIG}, [dx]
    for l in reversed(range(DEPTH)):
        recv_l[l] = _exchange_wait(f"grads_wait{l}", sending[l], after)
        for i, k in enumerate(BIG):
            rows, cols, wmv = as_rows(k)
            per_layer = rows // DEPTH
            outs[k] = _adamw_layer("adamw_" + k, l, recv_l[l][i].reshape(NDEV * per_layer, cols), *wmv,
                                   min(per_layer, 256), outs[k])
        after = [outs[k][0] for k in BIG]
    for k in BIG:
        grads[k], delta[k], new_m[k], new_v[k] = (o.reshape(wts[k].shape) for o in outs[k])

    def update(k, g):
        rows, cols, wmv = as_rows(k)
        res = _adamw("adamw_" + k, g.reshape(rows, cols), *wmv, min(rows, 1024 if cols <= 128 else 256))
        grads[k], delta[k], new_m[k], new_v[k] = (o.reshape(wts[k].shape) for o in res)

    recv_last = _exchange_wait("grads_wait_last", last, after)
    recv_small = jnp.concatenate([recv_last[0]] + [recv_l[l][len(BIG)] for l in range(DEPTH - 1)], axis=1)
    rows = DEPTH * PACK_BR
    (g_small,) = _ew("grads_small", lambda *p: (_sum_in_order(p),), rows, PACK_BR,
                     [_ri(recv_small.reshape(NDEV * rows, PACK_COLS), PACK_COLS, 0, d * DEPTH) for d in range(NDEV)], [],
                     [(PACK_COLS, F32)])
    for k, g in _unpack(g_small, wts).items():
        update(k, g)

    return (loss, dx[None], *[grads[k] for k in WEIGHTS], *[delta[k] for k in WEIGHTS],
            *[new_m[k] for k in WEIGHTS], *[new_v[k] for k in WEIGHTS])
```

```python
import functools
import math

import jax
import jax.numpy as jnp
from jax import lax
from jax.experimental import pallas as pl
from jax.experimental.pallas import tpu as pltpu

F32 = jnp.float32
BF16 = jnp.bfloat16

NDEV = 8
DEPTH = 4
SEQ = 2048
DM = 1024
NCOL = 8192
SW = 512
NGRP = 32
GCH = 16
NST = 64
NS = NGRP * NST
HD = 128
AW = 512
DILATIONS = (1, 4, 16)
ABLK = 128
ATTN_PAIR = 4
RMS_EPS = 1e-6
LR, B1, B2, ADAM_EPS, WD, STEP = 0.001, 0.9, 0.999, 1e-08, 0.01, 10

CB_U, CB_ZS, CB_Q, CB_K, CB_V, CB_ZA = 0, 1, 2, 5, 8, 11
CB_GS, CB_GA = 6, 7

VMEM_LIMIT = 56 * 2 ** 20
EW_ROWS = 512


def _row_order(j):
    return jnp.where(j < CB_Q, 1, 0)


def _params(*sem):
    return pltpu.CompilerParams(dimension_semantics=sem, vmem_limit_bytes=VMEM_LIMIT)


def _ew(name, fn, rows, br, row_ins, bc_ins, row_outs, red_outs=()):
    n_in = len(row_ins) + len(bc_ins)
    n_ro = len(row_outs)
    steps = rows // br
    assert steps * br == rows

    def body(*refs):
        vals = fn(*[r[...] for r in refs[:n_in]])
        outs = refs[n_in:]
        for r, v in zip(outs[:n_ro], vals[:n_ro]):
            r[...] = v.astype(r.dtype)
        if red_outs:
            @pl.when(pl.program_id(0) == 0)
            def _():
                for r in outs[n_ro:]:
                    r[...] = jnp.zeros(r.shape, r.dtype)
            for r, v in zip(outs[n_ro:], vals[n_ro:]):
                r[...] += v

    in_specs = []
    for (_, w, cb, rb) in row_ins:
        in_specs.append(pl.BlockSpec((br, w), functools.partial(lambda i, cb, rb: (rb + i, cb), cb=cb, rb=rb)))
    for a in bc_ins:
        in_specs.append(pl.BlockSpec(a.shape, functools.partial(lambda i, nd: (0,) * nd, nd=a.ndim)))
    out_specs = [pl.BlockSpec((br, w), lambda i: (i, 0)) for (w, _) in row_outs]
    out_specs += [pl.BlockSpec((1, w), lambda i: (0, 0)) for w in red_outs]
    out_shape = [jax.ShapeDtypeStruct((rows, w), dt) for (w, dt) in row_outs]
    out_shape += [jax.ShapeDtypeStruct((1, w), F32) for w in red_outs]
    return pl.pallas_call(
        body, name=name, grid=(steps,), in_specs=in_specs, out_specs=out_specs, out_shape=out_shape,
        compiler_params=_params("arbitrary"),
    )(*[a for (a, _, _, _) in row_ins], *bc_ins)


def _ri(a, w=None, cb=0, rb=0):
    return (a, a.shape[1] if w is None else w, cb, rb)


_DIMS = {"nn": ((1,), (0,)), "nt": ((1,), (1,)), "tn": ((0,), (0,))}


def _after(after):
    return ([pl.BlockSpec(memory_space=pl.ANY)], [after]) if after is not None else ([], [])


def _mm(name, a, b, mode, M, N, K, bm, bn, bk, out_dtype, a_spec=None, b_spec=None, o_spec=None, out_shape=None,
        after=None):
    nk = K // bk
    assert M % bm == 0 and N % bn == 0 and nk * bk == K
    after_specs, after_args = _after(after)

    own_acc = nk > 1 and out_dtype != F32

    def body(a_ref, b_ref, *rest):
        o_ref, scratch = rest[len(after_args)], rest[len(after_args) + 1:]
        part = lax.dot_general(a_ref[...].astype(BF16), b_ref[...].astype(BF16), (_DIMS[mode], ((), ())),
                               preferred_element_type=F32)
        if nk == 1:
            o_ref[...] = part.astype(o_ref.dtype)
            return
        k = pl.program_id(2)
        acc_ref = scratch[0] if own_acc else o_ref

        @pl.when(k == 0)
        def _():
            acc_ref[...] = part

        @pl.when(k > 0)
        def _():
            acc_ref[...] += part

        if own_acc:
            @pl.when(k == nk - 1)
            def _():
                o_ref[...] = acc_ref[...].astype(o_ref.dtype)

    if a_spec is None:
        a_spec = (pl.BlockSpec((bk, bm), lambda i, j, k: (k, i)) if mode == "tn"
                  else pl.BlockSpec((bm, bk), lambda i, j, k: (i, k)))
    if b_spec is None:
        b_spec = (pl.BlockSpec((bn, bk), lambda i, j, k: (j, k)) if mode == "nt"
                  else pl.BlockSpec((bk, bn), lambda i, j, k: (k, j)))
    if o_spec is None:
        o_spec = pl.BlockSpec((bm, bn), lambda i, j, k: (i, j))
    if out_shape is None:
        out_shape = (M, N)
    return pl.pallas_call(
        body, name=name, grid=(M // bm, N // bn, nk), in_specs=[a_spec, b_spec] + after_specs, out_specs=o_spec,
        out_shape=jax.ShapeDtypeStruct(out_shape, out_dtype),
        scratch_shapes=[pltpu.VMEM((bm, bn), F32)] if own_acc else [],
        compiler_params=_params("parallel", "parallel", "arbitrary"),
    )(a, b, *after_args)


SCAN_LANES = 512
SCAN_CHUNKS = 8


def _to_chunked(a):
    return a.reshape(SCAN_CHUNKS, SEQ // SCAN_CHUNKS, -1).transpose(1, 0, 2).reshape(SEQ, -1)


def _from_chunked(a):
    return a.reshape(SEQ // SCAN_CHUNKS, SCAN_CHUNKS, -1).transpose(1, 0, 2).reshape(SEQ, -1)


def _scan_block(dr_ref, di_ref, sr_ref, si_ref, lam_r, lam_i, reverse):
    T = SEQ // SCAN_CHUNKS
    bl = lam_r.shape[1]
    assert T == 2 ** 8
    ar = jnp.broadcast_to(lam_r, (SCAN_CHUNKS, bl))
    ai = jnp.broadcast_to(lam_i, (SCAN_CHUNKS, bl))
    zero = jnp.zeros((SCAN_CHUNKS, bl), F32)

    def tile(j):
        return pl.ds(pl.multiple_of(j * SCAN_CHUNKS, SCAN_CHUNKS), SCAN_CHUNKS)

    def step(jj, carry):
        sr, si = carry
        j = T - 1 - jj if reverse else jj
        nr = ar * sr - ai * si + dr_ref[tile(j), :]
        ni = ar * si + ai * sr + di_ref[tile(j), :]
        sr_ref[tile(j), :] = nr
        si_ref[tile(j), :] = ni
        return nr, ni

    er, ei = lax.fori_loop(0, T, step, (zero, zero), unroll=4)

    pr, pi = ar[0:1], ai[0:1]
    for _ in range(8):
        pr, pi = pr * pr - pi * pi, 2.0 * pr * pi
    rows = lax.broadcasted_iota(jnp.int32, (SCAN_CHUNKS, bl), 0)
    cr, ci = zero, zero
    xr = jnp.zeros((1, bl), F32)
    xi = jnp.zeros((1, bl), F32)
    order = range(SCAN_CHUNKS - 2, -1, -1) if reverse else range(1, SCAN_CHUNKS)
    for c in order:
        src = c + 1 if reverse else c - 1
        nxr = pr * xr - pi * xi + er[src:src + 1]
        nxi = pr * xi + pi * xr + ei[src:src + 1]
        xr, xi = nxr, nxi
        cr = jnp.where(rows == c, xr, cr)
        ci = jnp.where(rows == c, xi, ci)

    def fix(jj, pw):
        pwr, pwi = pw
        j = T - 1 - jj if reverse else jj
        sr_ref[tile(j), :] = sr_ref[tile(j), :] + (pwr * cr - pwi * ci)
        si_ref[tile(j), :] = si_ref[tile(j), :] + (pwr * ci + pwi * cr)
        return pwr * ar - pwi * ai, pwr * ai + pwi * ar

    lax.fori_loop(0, T, fix, (ar, ai), unroll=4)


def _s5_forward(name, proj, wdt, cmt, lam_r, lam_i):
    bl = SCAN_LANES
    nblk = NS // bl
    cw = bl * GCH // NST
    assert cw == 128 and CB_U == 0

    def nt(a, b_):
        return lax.dot_general(a, b_, (((1,), (1,)), ((), ())), preferred_element_type=F32)

    def body(u_ref, wr_ref, wi_ref, cr_ref, ci_ref, ar_ref, ai_ref, s_ref, y_ref):
        sr_ref, si_ref = s_ref.at[0], s_ref.at[1]
        u = u_ref[...].astype(BF16)
        sr_ref[...] = nt(u, wr_ref[...])
        si_ref[...] = nt(u, wi_ref[...])
        _scan_block(sr_ref, si_ref, sr_ref, si_ref, ar_ref[...], ai_ref[...], False)
        y_ref[...] = nt(sr_ref[...].astype(BF16), cr_ref[...]) + nt(si_ref[...].astype(BF16), ci_ref[...])

    return pl.pallas_call(
        body, name=name, grid=(nblk,),
        in_specs=[pl.BlockSpec((SEQ, cw), lambda i: (0, i)),
                  pl.BlockSpec((bl, cw), lambda i: (i, i)), pl.BlockSpec((bl, cw), lambda i: (nblk + i, i)),
                  pl.BlockSpec((cw, bl), lambda i: (i, i)), pl.BlockSpec((cw, bl), lambda i: (SW // cw + i, i)),
                  pl.BlockSpec((1, bl), lambda i: (0, i)), pl.BlockSpec((1, bl), lambda i: (0, i))],
        out_specs=[pl.BlockSpec((2, SEQ, bl), lambda i: (0, 0, i)), pl.BlockSpec((SEQ, cw), lambda i: (0, i))],
        out_shape=[jax.ShapeDtypeStruct((2, SEQ, NS), F32), jax.ShapeDtypeStruct((SEQ, SW), F32)],
        compiler_params=_params("arbitrary"),
    )(proj, wdt, wdt, cmt, cmt, lam_r, lam_i)


S5_BWD_LANES = 512


def _s5_backward(name, dy0, proj, s, wdt, cmt, lam_r, lam_i):
    bl = S5_BWD_LANES
    nblk = NS // bl
    cw = 128
    per = cw // (bl * GCH // NST)
    assert per >= 1 and CB_U == 0

    def tn(a, b_):
        return lax.dot_general(a, b_, (((0,), (0,)), ((), ())), preferred_element_type=F32)

    def prev(s_ref):
        last = pltpu.roll(s_ref[SEQ - SCAN_CHUNKS:SEQ, :], 1, 0)
        first = jnp.where(lax.broadcasted_iota(jnp.int32, (SCAN_CHUNKS, bl), 0) > 0, last, 0.0)
        return jnp.concatenate([first, s_ref[0:SEQ - SCAN_CHUNKS, :]], axis=0)

    def body(dy_ref, u_ref, s_ref, cr_ref, ci_ref, wr_ref, wi_ref, lr_ref, li_ref,
             dlr_ref, dli_ref, dwdt_ref, dcmt_ref, du_ref, ar_ref, ai_ref):
        dy = dy_ref[...]
        ar_ref[...] = jnp.dot(dy, cr_ref[...], preferred_element_type=F32)
        ai_ref[...] = jnp.dot(dy, ci_ref[...], preferred_element_type=F32)
        _scan_block(ar_ref, ai_ref, ar_ref, ai_ref, lr_ref[...], -li_ref[...], True)
        a_r, a_i = ar_ref[...], ai_ref[...]
        sr_ref, si_ref = s_ref.at[0], s_ref.at[1]
        spr, spi = prev(sr_ref), prev(si_ref)
        dlr_ref[...] = jnp.sum(a_r * spr + a_i * spi, axis=0, keepdims=True)
        dli_ref[...] = jnp.sum(a_i * spr - a_r * spi, axis=0, keepdims=True)
        a_rb, a_ib = a_r.astype(BF16), a_i.astype(BF16)
        u = u_ref[...].astype(BF16)
        dwdt_ref[0] = tn(a_rb, u)
        dwdt_ref[1] = tn(a_ib, u)
        dcmt_ref[0] = tn(dy, sr_ref[...].astype(BF16))
        dcmt_ref[1] = tn(dy, si_ref[...].astype(BF16))
        part = (jnp.dot(a_rb, wr_ref[...], preferred_element_type=F32)
                + jnp.dot(a_ib, wi_ref[...], preferred_element_type=F32))

        @pl.when(pl.program_id(0) % per == 0)
        def _():
            du_ref[...] = part

        @pl.when(pl.program_id(0) % per > 0)
        def _():
            du_ref[...] += part

    lam_spec = pl.BlockSpec((1, bl), lambda i: (0, i))
    return pl.pallas_call(
        body, name=name, grid=(nblk,),
        in_specs=[pl.BlockSpec((SEQ, cw), lambda i: (0, i // per)), pl.BlockSpec((SEQ, cw), lambda i: (0, i // per)),
                  pl.BlockSpec((2, SEQ, bl), lambda i: (0, 0, i)),
                  pl.BlockSpec((cw, bl), lambda i: (i // per, i)),
                  pl.BlockSpec((cw, bl), lambda i: (SW // cw + i // per, i)),
                  pl.BlockSpec((bl, cw), lambda i: (i, i // per)), pl.BlockSpec((bl, cw), lambda i: (nblk + i, i // per)),
                  lam_spec, lam_spec],
        out_specs=[lam_spec, lam_spec, pl.BlockSpec((2, bl, cw), lambda i: (0, i, i // per)),
                   pl.BlockSpec((2, cw, bl), lambda i: (0, i // per, i)),
                   pl.BlockSpec((SEQ, cw), lambda i: (0, i // per))],
        out_shape=[jax.ShapeDtypeStruct((1, NS), F32), jax.ShapeDtypeStruct((1, NS), F32),
                   jax.ShapeDtypeStruct((2, NS, SW), F32), jax.ShapeDtypeStruct((2, SW, NS), F32),
                   jax.ShapeDtypeStruct((SEQ, SW), F32)],
        scratch_shapes=[pltpu.VMEM((SEQ, bl), F32)] * 2,
        compiler_params=_params("arbitrary"),
    )(dy0, proj, s, cmt, cmt, wdt, wdt, lam_r, lam_i)


def _scores(qb, kb, prev):
    s = lax.dot_general(qb, kb, (((1,), (1,)), ((), ())), preferred_element_type=F32) * (HD ** -0.5)
    row = lax.broadcasted_iota(jnp.int32, (ABLK, ABLK), 0)
    col = lax.broadcasted_iota(jnp.int32, (ABLK, ABLK), 1)
    return jnp.where((col >= row) if prev else (col <= row), s, -1e30)


def _block_rows(dil, r, b):
    if dil == 1:
        return pl.ds(pl.multiple_of(b * ABLK, ABLK), ABLK)
    return pl.ds(r + dil * ABLK * b, ABLK, stride=dil)


def _group_blocks(dil):
    nb = SEQ // dil // ABLK
    shift = nb.bit_length() - 1
    return nb, (lambda idx: (idx >> shift, idx & (nb - 1)))


def _qkv_specs(j_of):
    return [pl.BlockSpec((SEQ, HD), functools.partial(lambda j, c: (0, c + j_of(j)), c=(cb + g) * 4))
            for g in range(3) for cb in (CB_Q, CB_K, CB_V)]


def _attention_fwd(name, proj):
    def body(*refs):
        qkv, z_ref = refs[:9], refs[9]
        y_ref, ya_ref, l_ref = refs[10:13]
        accs, maxs, dens = refs[13:16], refs[16:19], refs[19:22]
        for g, dil in enumerate(DILATIONS):
            q_ref, k_ref, v_ref = qkv[3 * g:3 * g + 3]
            nb, where = _group_blocks(dil)

            def step(t, c, g=g, dil=dil, nb=nb, where=where, q_ref=q_ref, k_ref=k_ref, v_ref=v_ref):
                two = range(ATTN_PAIR)
                rb = [where(t + i * (SEQ // ABLK // ATTN_PAIR)) for i in two]
                rows = [_block_rows(dil, r, b) for r, b in rb]
                qb = [q_ref[rows[i], :].astype(BF16) for i in two]
                s_c = [_scores(qb[i], k_ref[rows[i], :].astype(BF16), False) for i in two]
                if nb > 1:
                    prev = [_block_rows(dil, r, jnp.maximum(b - 1, 0)) for r, b in rb]
                    s_p = [jnp.where(rb[i][1] > 0, _scores(qb[i], k_ref[prev[i], :].astype(BF16), True), -1e30)
                           for i in two]
                m = [jnp.max(s_c[i], axis=-1, keepdims=True) for i in two]
                if nb > 1:
                    m = [jnp.maximum(m[i], jnp.max(s_p[i], axis=-1, keepdims=True)) for i in two]
                p_c = [jnp.exp(s_c[i] - m[i]) for i in two]
                den = [jnp.sum(p_c[i], axis=-1, keepdims=True) for i in two]
                acc = [jnp.dot(p_c[i].astype(BF16), v_ref[rows[i], :].astype(BF16), preferred_element_type=F32)
                       for i in two]
                if nb > 1:
                    p_p = [jnp.exp(s_p[i] - m[i]) for i in two]
                    den = [den[i] + jnp.sum(p_p[i], axis=-1, keepdims=True) for i in two]
                    acc = [acc[i] + jnp.dot(p_p[i].astype(BF16), v_ref[prev[i], :].astype(BF16),
                                            preferred_element_type=F32) for i in two]
                for i in two:
                    accs[g][rows[i], :] = acc[i]
                    maxs[g][rows[i], :] = jnp.broadcast_to(m[i], (ABLK, HD))
                    dens[g][rows[i], :] = jnp.broadcast_to(den[i], (ABLK, HD))
                return c

            lax.fori_loop(0, SEQ // ABLK // ATTN_PAIR, step, 0)
        top = jnp.maximum(jnp.maximum(maxs[0][...], maxs[1][...]), maxs[2][...])
        den = jnp.zeros((SEQ, HD), F32)
        y = jnp.zeros((SEQ, HD), F32)
        for g in range(3):
            wgt = jnp.exp(maxs[g][...] - top)
            den = den + wgt * dens[g][...]
            y = y + wgt * accs[g][...]
        y = y / den
        y_ref[...] = y
        ya_ref[...] = (y * _silu(z_ref[...])).astype(ya_ref.dtype)
        l_ref[...] = top + jnp.log(den)

    ospec = pl.BlockSpec((SEQ, HD), lambda j: (0, j))
    return pl.pallas_call(
        body, name=name, grid=(AW // HD,),
        in_specs=_qkv_specs(lambda j: j) + [pl.BlockSpec((SEQ, HD), lambda j: (0, CB_ZA * 4 + j))],
        out_specs=[ospec, ospec, ospec],
        out_shape=[jax.ShapeDtypeStruct((SEQ, AW), F32), jax.ShapeDtypeStruct((SEQ, AW), BF16),
                   jax.ShapeDtypeStruct((SEQ, AW), F32)],
        scratch_shapes=[pltpu.VMEM((SEQ, HD), F32)] * 9,
        compiler_params=_params("parallel"),
    )(*([proj] * 10))


def _attention_bwd(name, proj, dya, y, lse):
    def tn(a, b_):
        return lax.dot_general(a, b_, (((0,), (0,)), ((), ())), preferred_element_type=F32)

    def nt(a, b_):
        return lax.dot_general(a, b_, (((1,), (1,)), ((), ())), preferred_element_type=F32)

    def body(*refs):
        qkv, z_ref, dya_ref, y_ref, l_ref = refs[:9], refs[9], refs[10], refs[11], refs[12]
        outs, dza_ref = refs[13:22], refs[22]
        dy_s, dsum_s, dq_s, dk_own, dv_own, dk_prev, dv_prev = refs[23:]
        _, vjp = jax.vjp(lambda y_, z_: y_ * _silu(z_), y_ref[...], z_ref[...])
        dy, dz = vjp(dya_ref[...])
        dza_ref[...] = dz.astype(dza_ref.dtype)
        dy_s[...] = dy
        dsum_s[...] = jnp.broadcast_to(jnp.sum(dy * y_ref[...], axis=-1, keepdims=True), (SEQ, HD))
        for g, dil in enumerate(DILATIONS):
            q_ref, k_ref, v_ref = qkv[3 * g:3 * g + 3]
            nb, where = _group_blocks(dil)
            if nb > 1:
                dk_prev[...] = jnp.zeros(dk_prev.shape, F32)
                dv_prev[...] = jnp.zeros(dv_prev.shape, F32)

            def step(t, c, dil=dil, nb=nb, where=where, q_ref=q_ref, k_ref=k_ref, v_ref=v_ref):
                rb = [where(t + i * (SEQ // ABLK // ATTN_PAIR)) for i in range(ATTN_PAIR)]
                sides = []
                for r, b in rb:
                    rows = _block_rows(dil, r, b)
                    own = dict(b=b, qrows=rows, krows=rows, prev=False, dk=dk_own, dv=dv_own,
                               q=q_ref[rows, :].astype(BF16), dy=dy_s[rows, :].astype(BF16))
                    sides.append(own)
                    if nb > 1:
                        sides.append(dict(own, krows=_block_rows(dil, r, jnp.maximum(b - 1, 0)), prev=True,
                                          dk=dk_prev, dv=dv_prev))
                for s_ in sides:
                    s_["k"] = k_ref[s_["krows"], :].astype(BF16)
                    s_["v"] = v_ref[s_["krows"], :].astype(BF16)
                for s_ in sides:
                    sc = _scores(s_["q"], s_["k"], s_["prev"])
                    s_["s"] = jnp.where(s_["b"] > 0, sc, -1e30) if s_["prev"] else sc
                    s_["dp"] = nt(s_["dy"], s_["v"])
                for s_ in sides:
                    p = jnp.exp(s_["s"] - l_ref[s_["qrows"], :])
                    s_["p"] = p.astype(BF16)
                    s_["ds"] = (p * (s_["dp"] - dsum_s[s_["qrows"], :]) * (HD ** -0.5)).astype(BF16)
                for s_ in sides:
                    s_["dk"][s_["krows"], :] = tn(s_["ds"], s_["q"])
                    s_["dv"][s_["krows"], :] = tn(s_["p"], s_["dy"])
                    s_["dq"] = jnp.dot(s_["ds"], s_["k"], preferred_element_type=F32)
                per = len(sides) // ATTN_PAIR
                for i in range(ATTN_PAIR):
                    dq = sides[i * per]["dq"]
                    if per > 1:
                        dq = dq + sides[i * per + 1]["dq"]
                    dq_s[sides[i * per]["qrows"], :] = dq
                return c

            lax.fori_loop(0, SEQ // ABLK // ATTN_PAIR, step, 0)
            dq_ref, dk_ref, dv_ref = outs[3 * g:3 * g + 3]
            dq_ref[...] = dq_s[...].astype(dq_ref.dtype)
            if nb > 1:
                dk_ref[...] = (dk_own[...] + dk_prev[...]).astype(dk_ref.dtype)
                dv_ref[...] = (dv_own[...] + dv_prev[...]).astype(dv_ref.dtype)
            else:
                dk_ref[...] = dk_own[...].astype(dk_ref.dtype)
                dv_ref[...] = dv_own[...].astype(dv_ref.dtype)

    ospec = pl.BlockSpec((SEQ, HD), lambda j: (0, j))
    outs = pl.pallas_call(
        body, name=name, grid=(AW // HD,),
        in_specs=_qkv_specs(lambda j: j) + [pl.BlockSpec((SEQ, HD), lambda j: (0, CB_ZA * 4 + j))] + [ospec] * 3,
        out_specs=[ospec] * 10, out_shape=[jax.ShapeDtypeStruct((SEQ, AW), BF16)] * 10,
        scratch_shapes=[pltpu.VMEM((SEQ, HD), F32)] * 7,
        compiler_params=_params("parallel"),
    )(*([proj] * 10), dya, y, lse)
    return outs[:9], outs[9]


def _rms(x, g):
    return x * lax.rsqrt(jnp.mean(x * x, axis=-1, keepdims=True) + RMS_EPS) * g


def _sig(x):
    return 1.0 / (1.0 + jnp.exp(-x))


def _silu(x):
    return x * _sig(x)


def _gelu(x):
    return 0.5 * x * (1.0 + jnp.tanh(math.sqrt(2.0 / math.pi) * (x + 0.044715 * (x * x * x))))


def _y1_fn(y0p, u, dskip):
    return _gelu(y0p + dskip * u)


def _ys_fn(y1, t, z, bglu):
    return y1 * _sig(t + bglu) * _silu(z)


def _merge_fn(ms, ma, gs, ga):
    return _sig(gs) * ms.astype(F32) + _sig(ga) * ma.astype(F32)


def _colsum(v):
    return jnp.sum(v, axis=0, keepdims=True)


def _lam_fn(lre, lim, ldt):
    a = jnp.minimum(lre, -1e-4)
    dt = jnp.exp(ldt)
    mag = jnp.exp(a * dt)
    ar = mag * jnp.cos(lim * dt)
    ai = mag * jnp.sin(lim * dt)
    den = a * a + lim * lim
    cr = ((ar - 1.0) * a + ai * lim) / den
    ci = (ai * a - (ar - 1.0) * lim) / den
    return ar, ai, cr, ci


def _bbar_fn(cr, ci, bre, bim):
    return cr * bre - ci * bim, cr * bim + ci * bre


def _same_group(rows, a, cols, b):
    r = lax.broadcasted_iota(jnp.int32, (rows, cols), 0) >> (a.bit_length() - 1)
    c = lax.broadcasted_iota(jnp.int32, (rows, cols), 1) >> (b.bit_length() - 1)
    return r == c


def _expand(name, blocks, signs, a, b, dtype, after=None):
    rows, cols = NGRP * a, NGRP * b
    n = len(blocks)
    assert a & (a - 1) == 0 and b & (b - 1) == 0
    after_specs, after_args = _after(after)

    def body(*refs):
        o_ref = refs[-1]
        tile = (lax.broadcasted_iota(jnp.int32, (b, cols), 1) & (b - 1)
                == lax.broadcasted_iota(jnp.int32, (b, cols), 0)).astype(F32)
        keep = _same_group(rows, a, cols, b)
        for i, (ref, sign) in enumerate(zip(refs[:n], signs)):
            spread = jnp.dot(ref[...], tile, preferred_element_type=F32, precision=lax.Precision.HIGHEST)
            o_ref[i * rows:(i + 1) * rows, :] = jnp.where(keep, sign * spread, 0.0).astype(o_ref.dtype)

    return pl.pallas_call(body, name=name, grid=(1,),
                          in_specs=[pl.BlockSpec((rows, b), lambda i: (0, 0))] * n + after_specs,
                          out_specs=pl.BlockSpec((n * rows, cols), lambda i: (0, 0)),
                          out_shape=jax.ShapeDtypeStruct((n * rows, cols), dtype),
                          compiler_params=_params("arbitrary"))(*blocks, *after_args)


def _extract(name, m, a, b, ats, after=None):
    rows, cols = NGRP * a, NGRP * b
    n = len(ats)
    assert a & (a - 1) == 0 and b & (b - 1) == 0
    after_specs, after_args = _after(after)

    def body(*refs):
        tile = (lax.broadcasted_iota(jnp.int32, (cols, b), 0) & (b - 1)
                == lax.broadcasted_iota(jnp.int32, (cols, b), 1)).astype(F32)
        keep = _same_group(rows, a, cols, b)
        for m_ref, o_ref in zip(refs[:n], refs[-n:]):
            kept = jnp.where(keep, m_ref[...], 0.0)
            o_ref[...] = jnp.dot(kept, tile, preferred_element_type=F32, precision=lax.Precision.HIGHEST)

    return pl.pallas_call(
        body, name=name, grid=(1,),
        in_specs=[pl.BlockSpec((rows, cols), functools.partial(lambda i, at: at, at=at)) for at in ats] + after_specs,
        out_specs=[pl.BlockSpec((rows, b), lambda i: (0, 0))] * n,
        out_shape=[jax.ShapeDtypeStruct((rows, b), F32)] * n,
        compiler_params=_params("arbitrary"))(*([m] * n), *after_args)


def _s5_prepare(l, sp, after):
    tag = f"l{l}_"
    ar, ai, cr, ci = _ew(tag + "lam", _lam_fn, NGRP, NGRP,
                         [_ri(sp["lambda_re"]), _ri(sp["lambda_im"]),
                          _ri(sp["log_dt"].reshape(NGRP, 1) + after[0, 0])], [], [(NST, F32)] * 4)
    bre = sp["b_re"].reshape(NS, GCH)
    bim = sp["b_im"].reshape(NS, GCH)
    bbr, bbi = _ew(tag + "bbar", _bbar_fn, NS, NS, [_ri(cr.reshape(NS, 1)), _ri(ci.reshape(NS, 1)), _ri(bre), _ri(bim)],
                   [], [(GCH, F32)] * 2)
    wdt = _expand(tag + "wdt", [bbr, bbi], [1.0, 1.0], NST, GCH, BF16)
    cmt = _expand(tag + "cmt", [sp["c_re"].reshape(SW, NST), sp["c_im"].reshape(SW, NST)], [1.0, -1.0], GCH, NST, BF16,
                  after=after)
    return dict(ar=ar, ai=ai, cr=cr, ci=ci, wdt=wdt, cmt=cmt)


def _layer_head(l, x, sp):
    g1 = sp["pre_norm_g"].reshape(1, DM)
    (h,) = _ew(f"l{l}_rms1", lambda x_, g: (_rms(x_, g),), SEQ, EW_ROWS, [_ri(x)], [g1], [(DM, BF16)])
    return jnp.stack([h, _to_chunked(h)])


def _layer_fwd(l, x, hv, w, sp, prep, after):
    tag = f"l{l}_"
    win = w["w_in"]
    proj = _mm(tag + "proj", hv, win, "nn", SEQ, NCOL, DM, SEQ, 512, 1024, F32,
               a_spec=pl.BlockSpec((None, SEQ, 1024), lambda i, j, k: (_row_order(j), 0, 0)),
               b_spec=pl.BlockSpec((None, 1024, 512), lambda i, j, k: (j // 2, 0, j % 2)), after=after)

    ar, ai, cr, ci, wdt, cmt = (prep[k] for k in ("ar", "ai", "cr", "ci", "wdt", "cmt"))
    s, y0p = _s5_forward(tag + "s5", proj, wdt, cmt, ar.reshape(1, NS), ai.reshape(1, NS))
    dskip = sp["d_skip"].reshape(1, SW)
    (y1,) = _ew(tag + "y1", lambda a, u, d: (_y1_fn(a, u, d),), SEQ, EW_ROWS,
                [_ri(y0p), _ri(proj, SW, CB_U)], [dskip], [(SW, F32)])
    t = _mm(tag + "glu", y1, w["w_glu"], "nn", SEQ, SW, SW, 1024, 512, 512, F32)
    bglu = sp["b_glu"].reshape(1, SW)
    (ys_c,) = _ew(tag + "ys", lambda y1_, t_, z, b_: (_ys_fn(y1_, t_, z, b_),), SEQ, EW_ROWS,
                  [_ri(y1), _ri(t), _ri(proj, SW, CB_ZS)], [bglu], [(SW, BF16)])
    ys = _from_chunked(ys_c)

    ypre, ya, lse = _attention_fwd(tag + "attn", proj)

    ms = _mm(tag + "branch_s", ys, w["w_branch_s"], "nn", SEQ, DM, SW, 1024, 1024, 512, BF16)
    ma = _mm(tag + "branch_a", ya, w["w_branch_a"], "nn", SEQ, DM, AW, 1024, 1024, 512, BF16)
    (merged,) = _ew(tag + "merge", lambda a, b_, c, d: (_merge_fn(a, b_, c, d),), SEQ, EW_ROWS,
                    [_ri(ms), _ri(ma), _ri(proj, DM, CB_GS), _ri(proj, DM, CB_GA)], [], [(DM, BF16)])
    out = _mm(tag + "out", merged, w["w_out"], "nn", SEQ, DM, DM, 1024, 1024, 1024, F32)
    g2 = sp["post_norm_g"].reshape(1, DM)
    (x_new,) = _ew(tag + "post", lambda x_, o, g: (x_ + _rms(o, g),), SEQ, EW_ROWS, [_ri(x), _ri(out)], [g2], [(DM, F32)])
    res = dict(x=x, hv=hv, proj=proj, ar=ar, ai=ai, cr=cr, ci=ci, wdt=wdt, cmt=cmt, s=s, y0p=y0p,
               y1=y1, t=t, ys=ys, ya=ya, ypre=ypre, lse=lse, ms=ms, ma=ma, merged=merged, out=out)
    return x_new, res


def _layer_bwd(l, dxn, r, w, sp, big_done, after=None):
    tag = f"l{l}b_"
    proj = r["proj"]
    g1 = sp["pre_norm_g"].reshape(1, DM)
    g2 = sp["post_norm_g"].reshape(1, DM)
    dskip = sp["d_skip"].reshape(1, SW)
    bglu = sp["b_glu"].reshape(1, SW)

    def post_b(d, o, g):
        _, vjp = jax.vjp(_rms, o, g)
        do, dg = vjp(d)
        return do, dg

    d_out, dg2 = _ew(tag + "post", post_b, SEQ, EW_ROWS, [_ri(dxn), _ri(r["out"])], [g2], [(DM, BF16)], [DM])
    dw_out = _mm(tag + "dw_out", r["merged"], d_out, "tn", DM, DM, SEQ, 1024, 1024, SEQ, BF16, after=after)
    dmerged = _mm(tag + "dmerged", d_out, w["w_out"], "nt", SEQ, DM, DM, 1024, 1024, 1024, F32, after=after)

    def merge_b(d, ms, ma, gs, ga):
        _, vjp = jax.vjp(_merge_fn, ms, ma, gs, ga)
        return vjp(d)

    dms, dma, dgs, dga = _ew(tag + "merge", merge_b, SEQ, EW_ROWS,
                             [_ri(dmerged), _ri(r["ms"]), _ri(r["ma"]), _ri(proj, DM, CB_GS), _ri(proj, DM, CB_GA)],
                             [], [(DM, BF16)] * 4)
    dw_bs = _mm(tag + "dw_bs", r["ys"], dms, "tn", SW, DM, SEQ, 512, 1024, SEQ, BF16)
    dw_ba = _mm(tag + "dw_ba", r["ya"], dma, "tn", AW, DM, SEQ, 512, 1024, SEQ, BF16)
    dys = _mm(tag + "dys", dms, w["w_branch_s"], "nt", SEQ, SW, DM, 1024, 512, 1024, F32)
    dya = _mm(tag + "dya", dma, w["w_branch_a"], "nt", SEQ, AW, DM, 1024, 512, 1024, F32)

    dqkv, dza = _attention_bwd(tag + "attn", proj, dya, r["ypre"], r["lse"])

    def ys_b(d, y1, t, z, b_):
        _, vjp = jax.vjp(_ys_fn, y1, t, z, b_)
        dy1, dt, dz, _ = vjp(d)
        return dy1, dt, dz, _colsum(dt)

    dy1a, dt, dzs, dbglu = _ew(tag + "ys", ys_b, SEQ, EW_ROWS,
                               [_ri(_to_chunked(dys)), _ri(r["y1"]), _ri(r["t"]), _ri(proj, SW, CB_ZS)],
                               [bglu], [(SW, F32), (SW, BF16), (SW, BF16)], [SW])
    dw_glu = _mm(tag + "dw_glu", r["y1"], dt, "tn", SW, SW, SEQ, 512, 512, SEQ, BF16)
    dy1b = _mm(tag + "dy1b", dt, w["w_glu"], "nt", SEQ, SW, SW, 1024, 512, 512, F32)

    def y1_b(da, db, y0p, u, d_):
        _, vjp = jax.vjp(_y1_fn, y0p, u, d_)
        dy0, du, dd = vjp(da + db)
        return dy0, du, dd

    dy0, du_skip, ddskip = _ew(tag + "y1", y1_b, SEQ, EW_ROWS,
                               [_ri(dy1a), _ri(dy1b), _ri(r["y0p"]), _ri(proj, SW, CB_U)], [dskip],
                               [(SW, BF16), (SW, F32)], [SW])
    dlr, dli, dwdt, dcmt, du_s = _s5_backward(tag + "s5", dy0, proj, r["s"], r["wdt"], r["cmt"],
                                              r["ar"].reshape(1, NS), r["ai"].reshape(1, NS))
    (du,) = _ew(tag + "du", lambda a, c: (a + c,), SEQ, EW_ROWS, [_ri(du_s), _ri(du_skip)], [], [(SW, BF16)])

    dq, dk, dv = ([dqkv[3 * g + i] for g in range(3)] for i in range(3))
    dproj = jnp.concatenate([du, dzs, *dq, *dk, *dv, dza, dgs, dga], axis=1)
    dw_in = _mm(tag + "dw_in", r["hv"], dproj, "tn", DM, NCOL, SEQ, 1024, 512, SEQ, BF16,
                a_spec=pl.BlockSpec((None, SEQ, 1024), lambda i, j, k: (_row_order(j), 0, 0)),
                o_spec=pl.BlockSpec((None, 1024, 512), lambda i, j, k: (j // 2, 0, j % 2)), out_shape=(NDEV, DM, DM))
    tok = big_done(l, dict(w_in=dw_in, w_glu=dw_glu.reshape(NDEV, SW // NDEV, SW),
                           w_branch_s=dw_bs.reshape(SW, NDEV, DM // NDEV).transpose(1, 0, 2),
                           w_branch_a=dw_ba.reshape(AW, NDEV, DM // NDEV).transpose(1, 0, 2),
                           w_out=dw_out.reshape(NDEV, DM // NDEV, DM)))
    if tok is not None:
        g1 = g1 + tok[0, 0]

    dwdt = dwdt.reshape(2 * NS, SW)
    dcmt = dcmt.reshape(2 * SW, NS)
    dbbr, dbbi = _extract(tag + "dbb", dwdt, NST, GCH, ((0, 0), (1, 0)), after=tok)
    bre = sp["b_re"].reshape(NS, GCH)
    bim = sp["b_im"].reshape(NS, GCH)

    def bbar_b(cr, ci, br_, bi_, dr, di):
        _, vjp = jax.vjp(_bbar_fn, cr, ci, br_, bi_)
        return vjp((dr, di))

    dcr, dci, dbre, dbim = _ew(tag + "bbar", bbar_b, NS, NS,
                               [_ri(r["cr"].reshape(NS, 1)), _ri(r["ci"].reshape(NS, 1)), _ri(bre), _ri(bim),
                                _ri(dbbr), _ri(dbbi)], [], [(1, F32), (1, F32), (GCH, F32), (GCH, F32)])

    def lam_b(lre, lim, ldt, dar, dai, dcr_, dci_):
        _, vjp = jax.vjp(_lam_fn, lre, lim, ldt)
        return vjp((dar, dai, dcr_, dci_))

    dlre, dlim, dldt = _ew(tag + "lam", lam_b, NGRP, NGRP,
                           [_ri(sp["lambda_re"]), _ri(sp["lambda_im"]), _ri(sp["log_dt"].reshape(NGRP, 1)),
                            _ri(dlr.reshape(NGRP, NST)), _ri(dli.reshape(NGRP, NST)),
                            _ri(dcr.reshape(NGRP, NST)), _ri(dci.reshape(NGRP, NST))], [],
                           [(NST, F32), (NST, F32), (1, F32)])
    dc_re, dc_im = _extract(tag + "dc", dcmt, GCH, NST, ((0, 0), (1, 0)), after=tok)
    dc_re, dc_im = dc_re.reshape(NGRP, GCH, NST), -dc_im.reshape(NGRP, GCH, NST)

    dh_time = _mm(tag + "dh_time", dproj, w["w_in"], "nt", SEQ, DM, NCOL - DM, SEQ, 1024, 1024, F32,
                  a_spec=pl.BlockSpec((SEQ, 1024), lambda i, j, k: (0, 1 + k)),
                  b_spec=pl.BlockSpec((None, 1024, 1024), lambda i, j, k: (1 + k, 0, 0)), after=tok)
    dh_chunked = _mm(tag + "dh_chunked", dproj, w["w_in"], "nt", SEQ, DM, DM, SEQ, 1024, 1024, F32,
                     a_spec=pl.BlockSpec((SEQ, 1024), lambda i, j, k: (0, 0)),
                     b_spec=pl.BlockSpec((None, 1024, 1024), lambda i, j, k: (0, 0, 0)), after=tok)
    dh = [dh_time, _from_chunked(dh_chunked)]

    def pre_b(d, dh0, dh1, x_, g):
        _, vjp = jax.vjp(_rms, x_, g)
        dx_, dg = vjp(dh0 + dh1)
        return d + dx_, dg

    dx, dg1 = _ew(tag + "pre", pre_b, SEQ, EW_ROWS, [_ri(dxn)] + [_ri(t_) for t_ in dh] + [_ri(r["x"])], [g1],
                  [(DM, F32)], [DM])

    small = dict(pre_norm_g=dg1.reshape(DM), lambda_re=dlre, lambda_im=dlim, log_dt=dldt.reshape(NGRP),
                 b_re=dbre.reshape(NGRP, NST, GCH), b_im=dbim.reshape(NGRP, NST, GCH), c_re=dc_re, c_im=dc_im,
                 d_skip=ddskip.reshape(SW), b_glu=dbglu.reshape(SW), post_norm_g=dg2.reshape(DM))
    return dx, small


_HBM = pl.BlockSpec(memory_space=pltpu.HBM)
_SEM = pl.BlockSpec(memory_space=pltpu.SEMAPHORE)
_EFFECT = pltpu.SideEffectType.DATAFLOW_SIDE_EFFECTING


def _remote_copies(srcs, dsts, send_sems, recv_sems, gather):
    x, y, c = lax.axis_index("x"), lax.axis_index("y"), lax.axis_index("c")
    me = 4 * x + 2 * y + c
    copies = []
    for i in range(len(srcs)):
        for k in range(1, NDEV):
            peer = (x ^ (k >> 2), y ^ ((k >> 1) & 1), c ^ (k & 1))
            src = srcs[i] if gather[i] else srcs[i].at[me ^ k]
            copies.append(pltpu.make_async_remote_copy(
                src_ref=src, dst_ref=dsts[i].at[me], send_sem=send_sems[i], recv_sem=recv_sems[i],
                device_id=peer, device_id_type=pl.DeviceIdType.MESH))
    return copies


def _all_seven(dst, send_sem, recv_sem):
    seven = dst.at[pl.ds(0, NDEV - 1)]
    me = (lax.axis_index("x"), lax.axis_index("y"), lax.axis_index("c"))
    return pltpu.make_async_remote_copy(src_ref=seven, dst_ref=seven, send_sem=send_sem, recv_sem=recv_sem,
                                        device_id=me, device_id_type=pl.DeviceIdType.MESH)


def _own_slabs(name, arrs, gather, after):
    n = len(arrs)
    me = (4 * lax.axis_index("x") + 2 * lax.axis_index("y") + lax.axis_index("c")).astype(jnp.int32).reshape(1)

    def body(me_ref, *refs):
        for src, dst in zip(refs[:n], refs[n + 1:]):
            dst[...] = src[...]

    def zeros(k):
        return (0,) * k

    in_specs, out_specs, out_shape = [], [], []
    for a, g in zip(arrs, gather):
        slab = a.shape if g else a.shape[1:]
        nd = len(slab)
        if g:
            in_specs.append(pl.BlockSpec(slab, functools.partial(lambda i, me_ref, nd: zeros(nd), nd=nd)))
        else:
            in_specs.append(pl.BlockSpec((None,) + slab, functools.partial(lambda i, me_ref, nd: (me_ref[0],) + zeros(nd), nd=nd)))
        out_specs.append(pl.BlockSpec((None,) + slab, functools.partial(lambda i, me_ref, nd: (me_ref[0],) + zeros(nd), nd=nd)))
        out_shape.append(jax.ShapeDtypeStruct((NDEV,) + slab, a.dtype))
    in_specs.append(pl.BlockSpec(memory_space=pl.ANY))
    return pl.pallas_call(
        body, name=name, out_shape=out_shape,
        grid_spec=pltpu.PrefetchScalarGridSpec(num_scalar_prefetch=1, grid=(1,), in_specs=in_specs, out_specs=out_specs),
        compiler_params=_params("arbitrary"),
    )(me, *arrs, after)


def _exchange_start(name, arrs, gather, after):
    n = len(arrs)
    lands = _own_slabs(name + "_own", arrs, gather, after)

    def body(*refs):
        srcs, dsts = refs[:n], refs[n:2 * n]
        send_sems, recv_sems = refs[2 * n:3 * n], refs[3 * n:4 * n]
        token = refs[-1]
        for cp in _remote_copies(srcs, dsts, send_sems, recv_sems, gather):
            cp.start()
        token[...] = jnp.zeros(token.shape, token.dtype)

    thru = [pltpu.HBM(a.shape, a.dtype) for a in list(arrs) + list(lands)]
    outs = pl.pallas_call(
        body, name=name,
        out_shape=(*[pltpu.SemaphoreType.DMA(())] * (2 * n), *thru, jax.ShapeDtypeStruct((8, 128), F32)),
        in_specs=[_HBM] * (2 * n),
        out_specs=(*[_SEM] * (2 * n), *[_HBM] * (2 * n), pl.BlockSpec(memory_space=pltpu.VMEM)),
        input_output_aliases={i: 2 * n + i for i in range(2 * n)},
        compiler_params=pltpu.CompilerParams(has_side_effects=_EFFECT),
    )(*[pltpu.with_memory_space_constraint(a, pltpu.HBM) for a in list(arrs) + list(lands)])
    return dict(send=outs[:n], recv=outs[n:2 * n], srcs=outs[2 * n:3 * n], lands=outs[3 * n:4 * n], token=outs[-1],
                gather=gather)


def _exchange_wait(name, started, after):
    n = len(started["srcs"])
    after = list(after)

    def body(*refs):
        dsts = refs[n:2 * n]
        send_sems, recv_sems = refs[2 * n:3 * n], refs[3 * n:4 * n]
        for i in range(n):
            cp = _all_seven(dsts[i], send_sems[i], recv_sems[i])
            cp.wait_send()
            cp.wait_recv()

    bufs = list(started["srcs"]) + list(started["lands"])
    outs = pl.pallas_call(
        body, name=name, out_shape=tuple(pltpu.HBM(a.shape, a.dtype) for a in bufs),
        in_specs=[_HBM] * (2 * n) + [_SEM] * (2 * n) + [pl.BlockSpec(memory_space=pl.ANY)] * len(after),
        out_specs=(_HBM,) * (2 * n), input_output_aliases={i: i for i in range(2 * n)},
        compiler_params=pltpu.CompilerParams(has_side_effects=_EFFECT),
    )(*bufs, *started["send"], *started["recv"], *after)
    return outs[n:]


def _sum_in_order(parts):
    g = parts[0].astype(F32)
    for p in parts[1:]:
        g = g + p.astype(F32)
    return g


def _adam_update(g, w_, m_, v_):
    m2 = B1 * m_ + (1.0 - B1) * g
    v2 = B2 * v_ + (1.0 - B2) * (g * g)
    m_hat = m2 / (1.0 - B1 ** STEP)
    v_hat = v2 / (1.0 - B2 ** STEP)
    delta = -LR * (m_hat / (jnp.sqrt(v_hat) + ADAM_EPS) + WD * w_)
    return g, delta, m2, v2


def _adamw(name, g, w, m, v, br):
    rows, cols = w.shape
    return _ew(name, _adam_update, rows, br, [_ri(g), _ri(w), _ri(m), _ri(v)], [], [(cols, F32)] * 4)


def _adamw_layer(name, l, parts, w, m, v, br, outs):
    rows, cols = w.shape
    nb = rows // DEPTH // br
    assert nb * br * DEPTH == rows

    def body(*refs):
        vals = _adam_update(_sum_in_order([r[...] for r in refs[:NDEV]]), *[r[...] for r in refs[NDEV:NDEV + 3]])
        for r, val in zip(refs[-4:], vals):
            r[...] = val

    mine = pl.BlockSpec((br, cols), lambda i: (l * nb + i, 0))
    in_specs = [pl.BlockSpec((br, cols), functools.partial(lambda i, d: (d * nb + i, 0), d=d)) for d in range(NDEV)]
    in_specs += [mine] * 3 + [pl.BlockSpec(memory_space=pl.ANY)] * len(outs)
    return pl.pallas_call(
        body, name=f"{name}{l}", grid=(nb,), in_specs=in_specs, out_specs=[mine] * 4,
        out_shape=[jax.ShapeDtypeStruct((rows, cols), F32)] * 4,
        input_output_aliases={NDEV + 3 + q: q for q in range(len(outs))},
        compiler_params=_params("arbitrary"),
    )(*([parts] * NDEV), w, m, v, *outs)


SMALL = ("pre_norm_g", "lambda_re", "lambda_im", "log_dt", "b_re", "b_im", "c_re", "c_im", "d_skip", "b_glu",
         "post_norm_g")
BIG = ("w_in", "w_glu", "w_branch_s", "w_branch_a", "w_out")
WEIGHTS = ("pre_norm_g", "w_in", "lambda_re", "lambda_im", "log_dt", "b_re", "b_im", "c_re", "c_im", "d_skip",
           "w_glu", "b_glu", "w_branch_s", "w_branch_a", "w_out", "post_norm_g")
PACK_COLS = 1024
PACK_BR = 136


def _pack_layer(d):
    pieces = [d[k].astype(F32).reshape(-1) for k in SMALL]
    used = sum(p.shape[0] for p in pieces)
    assert used <= PACK_BR * PACK_COLS
    return jnp.concatenate(pieces + [jnp.zeros((PACK_BR * PACK_COLS - used,), F32)]).reshape(PACK_BR, PACK_COLS)


def _unpack(p, like):
    flat = p.reshape(DEPTH, PACK_BR * PACK_COLS)
    out, off = {}, 0
    for k in SMALL:
        n = like[k].size // DEPTH
        out[k] = flat[:, off:off + n].reshape(like[k].shape)
        off += n
    return out


def _local_step(x, target, small, started, weights_of, big_done, small_done, total_loss):
    preps = [_s5_prepare(l, {k: small[k][l] for k in SMALL}, started) for l in range(DEPTH)]
    res, ws = [], []
    for l in range(DEPTH):
        sp = {k: small[k][l] for k in SMALL}
        hv = _layer_head(l, x, sp)
        w_l, tok = weights_of(l, [hv] + ([p[k] for p in preps for k in ("wdt", "cmt")] if l == 0 else []))
        x, r = _layer_fwd(l, x, hv, w_l, sp, preps[l], tok)
        res.append(r)
        ws.append(w_l)

    def loss_fn(y, t):
        e = y - t
        return e * (1.0 / DM), jnp.sum(_colsum(0.5 * e * e * (1.0 / DM)), axis=1, keepdims=True)

    dx, loss = _ew("loss", loss_fn, SEQ, EW_ROWS, [_ri(x), _ri(target)], [], [(DM, F32)], [1])
    total = total_loss(loss.reshape(()))
    for l in reversed(range(DEPTH)):
        dx, sm = _layer_bwd(l, dx, res[l], ws[l], {k: small[k][l] for k in SMALL}, big_done,
                            after=total.reshape(1, 1) if l == DEPTH - 1 else None)
        small_done(l, sm)
    return total, dx


def _full_weights(gathered):
    g = gathered
    return dict(
        w_in=g["w_in"],
        w_glu=g["w_glu"].reshape(SW, SW),
        w_branch_s=g["w_branch_s"].transpose(1, 0, 2).reshape(SW, DM),
        w_branch_a=g["w_branch_a"].transpose(1, 0, 2).reshape(AW, DM),
        w_out=g["w_out"].reshape(DM, DM),
    )


def kernel(x, pre_norm_g, w_in, lambda_re, lambda_im, log_dt, b_re, b_im, c_re, c_im, d_skip, w_glu, b_glu, w_branch_s, w_branch_a, w_out, post_norm_g, loss_target, m_pre_norm_g, m_w_in, m_lambda_re, m_lambda_im, m_log_dt, m_b_re, m_b_im, m_c_re, m_c_im, m_d_skip, m_w_glu, m_b_glu, m_w_branch_s, m_w_branch_a, m_w_out, m_post_norm_g, v_pre_norm_g, v_w_in, v_lambda_re, v_lambda_im, v_log_dt, v_b_re, v_b_im, v_c_re, v_c_im, v_d_skip, v_w_glu, v_b_glu, v_w_branch_s, v_w_branch_a, v_w_out, v_post_norm_g):
    wts = dict(pre_norm_g=pre_norm_g, w_in=w_in, lambda_re=lambda_re, lambda_im=lambda_im, log_dt=log_dt, b_re=b_re,
               b_im=b_im, c_re=c_re, c_im=c_im, d_skip=d_skip, w_glu=w_glu, b_glu=b_glu, w_branch_s=w_branch_s,
               w_branch_a=w_branch_a, w_out=w_out, post_norm_g=post_norm_g)
    mom = dict(pre_norm_g=m_pre_norm_g, w_in=m_w_in, lambda_re=m_lambda_re, lambda_im=m_lambda_im, log_dt=m_log_dt,
               b_re=m_b_re, b_im=m_b_im, c_re=m_c_re, c_im=m_c_im, d_skip=m_d_skip, w_glu=m_w_glu, b_glu=m_b_glu,
               w_branch_s=m_w_branch_s, w_branch_a=m_w_branch_a, w_out=m_w_out, post_norm_g=m_post_norm_g)
    var = dict(pre_norm_g=v_pre_norm_g, w_in=v_w_in, lambda_re=v_lambda_re, lambda_im=v_lambda_im, log_dt=v_log_dt,
               b_re=v_b_re, b_im=v_b_im, c_re=v_c_re, c_im=v_c_im, d_skip=v_d_skip, w_glu=v_w_glu, b_glu=v_b_glu,
               w_branch_s=v_w_branch_s, w_branch_a=v_w_branch_a, w_out=v_w_out, post_norm_g=v_post_norm_g)

    def gather_start(l, after):
        return _exchange_start(f"gather_start{l}", [wts[k][l].astype(BF16) for k in BIG], [True] * len(BIG), after)

    gathering = {0: gather_start(0, x)}
    sending, packed = {}, {}

    def weights_of(l, after):
        gathered = _exchange_wait(f"gather_wait{l}", gathering[l], after)
        tok = None
        if l + 1 < DEPTH:
            gathering[l + 1] = gather_start(l + 1, gathered[0])
            tok = gathering[l + 1]["token"]
        return _full_weights(dict(zip(BIG, gathered))), tok

    def big_done(l, big):
        arrs, kinds = [big[k] for k in BIG], [False] * len(BIG)
        if l + 1 < DEPTH:
            arrs, kinds = arrs + [packed[l + 1]], kinds + [True]
        sending[l] = _exchange_start(f"grads_start{l}", arrs, kinds, big["w_in"])
        return sending[l]["token"]

    def small_done(l, sm):
        packed[l] = _pack_layer(sm)

    loss, dx = _local_step(x[0], loss_target[0], wts, gathering[0]["token"], weights_of, big_done, small_done,
                           lambda part: lax.psum(part, ("x", "y", "c")))
    last = _exchange_start("grads_start_last", [packed[0]], [True], dx)

    grads, delta, new_m, new_v = {}, {}, {}, {}

    def as_rows(k):
        cols = wts[k].shape[-1]
        rows = wts[k].size // cols
        return rows, cols, [t[k].reshape(rows, cols) for t in (wts, mom, var)]

    recv_l, outs, after = {}, {k: [] for k in BIG}, [dx]
    for l in reversed(range(DEPTH)):
        recv_l[l] = _exchange_wait(f"grads_wait{l}", sending[l], after)
        for i, k in enumerate(BIG):
            rows, cols, wmv = as_rows(k)
            per_layer = rows // DEPTH
            outs[k] = _adamw_layer("adamw_" + k, l, recv_l[l][i].reshape(NDEV * per_layer, cols), *wmv,
                                   min(per_layer, 256), outs[k])
        after = [outs[k][0] for k in BIG]
    for k in BIG:
        grads[k], delta[k], new_m[k], new_v[k] = (o.reshape(wts[k].shape) for o in outs[k])

    def update(k, g):
        rows, cols, wmv = as_rows(k)
        res = _adamw("adamw_" + k, g.reshape(rows, cols), *wmv, min(rows, 1024 if cols <= 128 else 256))
        grads[k], delta[k], new_m[k], new_v[k] = (o.reshape(wts[k].shape) for o in res)

    recv_last = _exchange_wait("grads_wait_last", last, after)
    recv_small = jnp.concatenate([recv_last[0]] + [recv_l[l][len(BIG)] for l in range(DEPTH - 1)], axis=1)
    rows = DEPTH * PACK_BR
    (g_small,) = _ew("grads_small", lambda *p: (_sum_in_order(p),), rows, PACK_BR,
                     [_ri(recv_small.reshape(NDEV * rows, PACK_COLS), PACK_COLS, 0, d * DEPTH) for d in range(NDEV)], [],
                     [(PACK_COLS, F32)])
    for k, g in _unpack(g_small, wts).items():
        update(k, g)

    return (loss, dx[None], *[grads[k] for k in WEIGHTS], *[delta[k] for k in WEIGHTS],
            *[new_m[k] for k in WEIGHTS], *[new_v[k] for k in WEIGHTS])
```

```python
import functools
import math

import jax
import jax.numpy as jnp
from jax import lax
from jax.experimental import pallas as pl
from jax.experimental.pallas import tpu as pltpu

F32 = jnp.float32
BF16 = jnp.bfloat16

NDEV = 8
DEPTH = 4
SEQ = 2048
DM = 1024
NCOL = 8192
SW = 512
NGRP = 32
GCH = 16
NST = 64
NS = NGRP * NST
HD = 128
AW = 512
DILATIONS = (1, 4, 16)
ABLK = 128
ATTN_PAIR = 4
RMS_EPS = 1e-6
LR, B1, B2, ADAM_EPS, WD, STEP = 0.001, 0.9, 0.999, 1e-08, 0.01, 10

CB_U, CB_ZS, CB_Q, CB_K, CB_V, CB_ZA = 0, 1, 2, 5, 8, 11
CB_GS, CB_GA = 6, 7

VMEM_LIMIT = 56 * 2 ** 20
EW_ROWS = 512


def _row_order(j):
    return jnp.where(j < CB_Q, 1, 0)


def _params(*sem):
    return pltpu.CompilerParams(dimension_semantics=sem, vmem_limit_bytes=VMEM_LIMIT)


def _ew(name, fn, rows, br, row_ins, bc_ins, row_outs, red_outs=()):
    n_in = len(row_ins) + len(bc_ins)
    n_ro = len(row_outs)
    steps = rows // br
    assert steps * br == rows

    def body(*refs):
        vals = fn(*[r[...] for r in refs[:n_in]])
        outs = refs[n_in:]
        for r, v in zip(outs[:n_ro], vals[:n_ro]):
            r[...] = v.astype(r.dtype)
        if red_outs:
            @pl.when(pl.program_id(0) == 0)
            def _():
                for r in outs[n_ro:]:
                    r[...] = jnp.zeros(r.shape, r.dtype)
            for r, v in zip(outs[n_ro:], vals[n_ro:]):
                r[...] += v

    in_specs = []
    for (_, w, cb, rb) in row_ins:
        in_specs.append(pl.BlockSpec((br, w), functools.partial(lambda i, cb, rb: (rb + i, cb), cb=cb, rb=rb)))
    for a in bc_ins:
        in_specs.append(pl.BlockSpec(a.shape, functools.partial(lambda i, nd: (0,) * nd, nd=a.ndim)))
    out_specs = [pl.BlockSpec((br, w), lambda i: (i, 0)) for (w, _) in row_outs]
    out_specs += [pl.BlockSpec((1, w), lambda i: (0, 0)) for w in red_outs]
    out_shape = [jax.ShapeDtypeStruct((rows, w), dt) for (w, dt) in row_outs]
    out_shape += [jax.ShapeDtypeStruct((1, w), F32) for w in red_outs]
    return pl.pallas_call(
        body, name=name, grid=(steps,), in_specs=in_specs, out_specs=out_specs, out_shape=out_shape,
        compiler_params=_params("arbitrary"),
    )(*[a for (a, _, _, _) in row_ins], *bc_ins)


def _ri(a, w=None, cb=0, rb=0):
    return (a, a.shape[1] if w is None else w, cb, rb)


_DIMS = {"nn": ((1,), (0,)), "nt": ((1,), (1,)), "tn": ((0,), (0,))}


def _after(after):
    return ([pl.BlockSpec(memory_space=pl.ANY)], [after]) if after is not None else ([], [])


def _mm(name, a, b, mode, M, N, K, bm, bn, bk, out_dtype, a_spec=None, b_spec=None, o_spec=None, out_shape=None,
        after=None):
    nk = K // bk
    assert M % bm == 0 and N % bn == 0 and nk * bk == K
    after_specs, after_args = _after(after)

    own_acc = nk > 1 and out_dtype != F32

    def body(a_ref, b_ref, *rest):
        o_ref, scratch = rest[len(after_args)], rest[len(after_args) + 1:]
        part = lax.dot_general(a_ref[...].astype(BF16), b_ref[...].astype(BF16), (_DIMS[mode], ((), ())),
                               preferred_element_type=F32)
        if nk == 1:
            o_ref[...] = part.astype(o_ref.dtype)
            return
        k = pl.program_id(2)
        acc_ref = scratch[0] if own_acc else o_ref

        @pl.when(k == 0)
        def _():
            acc_ref[...] = part

        @pl.when(k > 0)
        def _():
            acc_ref[...] += part

        if own_acc:
            @pl.when(k == nk - 1)
            def _():
                o_ref[...] = acc_ref[...].astype(o_ref.dtype)

    if a_spec is None:
        a_spec = (pl.BlockSpec((bk, bm), lambda i, j, k: (k, i)) if mode == "tn"
                  else pl.BlockSpec((bm, bk), lambda i, j, k: (i, k)))
    if b_spec is None:
        b_spec = (pl.BlockSpec((bn, bk), lambda i, j, k: (j, k)) if mode == "nt"
                  else pl.BlockSpec((bk, bn), lambda i, j, k: (k, j)))
    if o_spec is None:
        o_spec = pl.BlockSpec((bm, bn), lambda i, j, k: (i, j))
    if out_shape is None:
        out_shape = (M, N)
    return pl.pallas_call(
        body, name=name, grid=(M // bm, N // bn, nk), in_specs=[a_spec, b_spec] + after_specs, out_specs=o_spec,
        out_shape=jax.ShapeDtypeStruct(out_shape, out_dtype),
        scratch_shapes=[pltpu.VMEM((bm, bn), F32)] if own_acc else [],
        compiler_params=_params("parallel", "parallel", "arbitrary"),
    )(a, b, *after_args)


SCAN_LANES = 512
SCAN_CHUNKS = 8


def _to_chunked(a):
    return a.reshape(SCAN_CHUNKS, SEQ // SCAN_CHUNKS, -1).transpose(1, 0, 2).reshape(SEQ, -1)


def _from_chunked(a):
    return a.reshape(SEQ // SCAN_CHUNKS, SCAN_CHUNKS, -1).transpose(1, 0, 2).reshape(SEQ, -1)


def _scan_block(dr_ref, di_ref, sr_ref, si_ref, lam_r, lam_i, reverse):
    T = SEQ // SCAN_CHUNKS
    bl = lam_r.shape[1]
    assert T == 2 ** 8
    ar = jnp.broadcast_to(lam_r, (SCAN_CHUNKS, bl))
    ai = jnp.broadcast_to(lam_i, (SCAN_CHUNKS, bl))
    zero = jnp.zeros((SCAN_CHUNKS, bl), F32)

    def tile(j):
        return pl.ds(pl.multiple_of(j * SCAN_CHUNKS, SCAN_CHUNKS), SCAN_CHUNKS)

    def step(jj, carry):
        sr, si = carry
        j = T - 1 - jj if reverse else jj
        nr = ar * sr - ai * si + dr_ref[tile(j), :]
        ni = ar * si + ai * sr + di_ref[tile(j), :]
        sr_ref[tile(j), :] = nr
        si_ref[tile(j), :] = ni
        return nr, ni

    er, ei = lax.fori_loop(0, T, step, (zero, zero), unroll=4)

    pr, pi = ar[0:1], ai[0:1]
    for _ in range(8):
        pr, pi = pr * pr - pi * pi, 2.0 * pr * pi
    rows = lax.broadcasted_iota(jnp.int32, (SCAN_CHUNKS, bl), 0)
    cr, ci = zero, zero
    xr = jnp.zeros((1, bl), F32)
    xi = jnp.zeros((1, bl), F32)
    order = range(SCAN_CHUNKS - 2, -1, -1) if reverse else range(1, SCAN_CHUNKS)
    for c in order:
        src = c + 1 if reverse else c - 1
        nxr = pr * xr - pi * xi + er[src:src + 1]
        nxi = pr * xi + pi * xr + ei[src:src + 1]
        xr, xi = nxr, nxi
        cr = jnp.where(rows == c, xr, cr)
        ci = jnp.where(rows == c, xi, ci)

    def fix(jj, pw):
        pwr, pwi = pw
        j = T - 1 - jj if reverse else jj
        sr_ref[tile(j), :] = sr_ref[tile(j), :] + (pwr * cr - pwi * ci)
        si_ref[tile(j), :] = si_ref[tile(j), :] + (pwr * ci + pwi * cr)
        return pwr * ar - pwi * ai, pwr * ai + pwi * ar

    lax.fori_loop(0, T, fix, (ar, ai), unroll=4)


def _s5_forward(name, proj, wdt, cmt, lam_r, lam_i):
    bl = SCAN_LANES
    nblk = NS // bl
    cw = bl * GCH // NST
    assert cw == 128 and CB_U == 0

    def nt(a, b_):
        return lax.dot_general(a, b_, (((1,), (1,)), ((), ())), preferred_element_type=F32)

    def body(u_ref, wr_ref, wi_ref, cr_ref, ci_ref, ar_ref, ai_ref, s_ref, y_ref):
        sr_ref, si_ref = s_ref.at[0], s_ref.at[1]
        u = u_ref[...].astype(BF16)
        sr_ref[...] = nt(u, wr_ref[...])
        si_ref[...] = nt(u, wi_ref[...])
        _scan_block(sr_ref, si_ref, sr_ref, si_ref, ar_ref[...], ai_ref[...], False)
        y_ref[...] = nt(sr_ref[...].astype(BF16), cr_ref[...]) + nt(si_ref[...].astype(BF16), ci_ref[...])

    return pl.pallas_call(
        body, name=name, grid=(nblk,),
        in_specs=[pl.BlockSpec((SEQ, cw), lambda i: (0, i)),
                  pl.BlockSpec((bl, cw), lambda i: (i, i)), pl.BlockSpec((bl, cw), lambda i: (nblk + i, i)),
                  pl.BlockSpec((cw, bl), lambda i: (i, i)), pl.BlockSpec((cw, bl), lambda i: (SW // cw + i, i)),
                  pl.BlockSpec((1, bl), lambda i: (0, i)), pl.BlockSpec((1, bl), lambda i: (0, i))],
        out_specs=[pl.BlockSpec((2, SEQ, bl), lambda i: (0, 0, i)), pl.BlockSpec((SEQ, cw), lambda i: (0, i))],
        out_shape=[jax.ShapeDtypeStruct((2, SEQ, NS), F32), jax.ShapeDtypeStruct((SEQ, SW), F32)],
        compiler_params=_params("arbitrary"),
    )(proj, wdt, wdt, cmt, cmt, lam_r, lam_i)


S5_BWD_LANES = 512


def _s5_backward(name, dy0, proj, s, wdt, cmt, lam_r, lam_i):
    bl = S5_BWD_LANES
    nblk = NS // bl
    cw = 128
    per = cw // (bl * GCH // NST)
    assert per >= 1 and CB_U == 0

    def tn(a, b_):
        return lax.dot_general(a, b_, (((0,), (0,)), ((), ())), preferred_element_type=F32)

    def prev(s_ref):
        last = pltpu.roll(s_ref[SEQ - SCAN_CHUNKS:SEQ, :], 1, 0)
        first = jnp.where(lax.broadcasted_iota(jnp.int32, (SCAN_CHUNKS, bl), 0) > 0, last, 0.0)
        return jnp.concatenate([first, s_ref[0:SEQ - SCAN_CHUNKS, :]], axis=0)

    def body(dy_ref, u_ref, s_ref, cr_ref, ci_ref, wr_ref, wi_ref, lr_ref, li_ref,
             dlr_ref, dli_ref, dwdt_ref, dcmt_ref, du_ref, ar_ref, ai_ref):
        dy = dy_ref[...]
        ar_ref[...] = jnp.dot(dy, cr_ref[...], preferred_element_type=F32)
        ai_ref[...] = jnp.dot(dy, ci_ref[...], preferred_element_type=F32)
        _scan_block(ar_ref, ai_ref, ar_ref, ai_ref, lr_ref[...], -li_ref[...], True)
        a_r, a_i = ar_ref[...], ai_ref[...]
        sr_ref, si_ref = s_ref.at[0], s_ref.at[1]
        spr, spi = prev(sr_ref), prev(si_ref)
        dlr_ref[...] = jnp.sum(a_r * spr + a_i * spi, axis=0, keepdims=True)
        dli_ref[...] = jnp.sum(a_i * spr - a_r * spi, axis=0, keepdims=True)
        a_rb, a_ib = a_r.astype(BF16), a_i.astype(BF16)
        u = u_ref[...].astype(BF16)
        dwdt_ref[0] = tn(a_rb, u)
        dwdt_ref[1] = tn(a_ib, u)
        dcmt_ref[0] = tn(dy, sr_ref[...].astype(BF16))
        dcmt_ref[1] = tn(dy, si_ref[...].astype(BF16))
        part = (jnp.dot(a_rb, wr_ref[...], preferred_element_type=F32)
                + jnp.dot(a_ib, wi_ref[...], preferred_element_type=F32))

        @pl.when(pl.program_id(0) % per == 0)
        def _():
            du_ref[...] = part

        @pl.when(pl.program_id(0) % per > 0)
        def _():
            du_ref[...] += part

    lam_spec = pl.BlockSpec((1, bl), lambda i: (0, i))
    return pl.pallas_call(
        body, name=name, grid=(nblk,),
        in_specs=[pl.BlockSpec((SEQ, cw), lambda i: (0, i // per)), pl.BlockSpec((SEQ, cw), lambda i: (0, i // per)),
                  pl.BlockSpec((2, SEQ, bl), lambda i: (0, 0, i)),
                  pl.BlockSpec((cw, bl), lambda i: (i // per, i)),
                  pl.BlockSpec((cw, bl), lambda i: (SW // cw + i // per, i)),
                  pl.BlockSpec((bl, cw), lambda i: (i, i // per)), pl.BlockSpec((bl, cw), lambda i: (nblk + i, i // per)),
                  lam_spec, lam_spec],
        out_specs=[lam_spec, lam_spec, pl.BlockSpec((2, bl, cw), lambda i: (0, i, i // per)),
                   pl.BlockSpec((2, cw, bl), lambda i: (0, i // per, i)),
                   pl.BlockSpec((SEQ, cw), lambda i: (0, i // per))],
        out_shape=[jax.ShapeDtypeStruct((1, NS), F32), jax.ShapeDtypeStruct((1, NS), F32),
                   jax.ShapeDtypeStruct((2, NS, SW), F32), jax.ShapeDtypeStruct((2, SW, NS), F32),
                   jax.ShapeDtypeStruct((SEQ, SW), F32)],
        scratch_shapes=[pltpu.VMEM((SEQ, bl), F32)] * 2,
        compiler_params=_params("arbitrary"),
    )(dy0, proj, s, cmt, cmt, wdt, wdt, lam_r, lam_i)


def _scores(qb, kb, prev):
    s = lax.dot_general(qb, kb, (((1,), (1,)), ((), ())), preferred_element_type=F32) * (HD ** -0.5)
    row = lax.broadcasted_iota(jnp.int32, (ABLK, ABLK), 0)
    col = lax.broadcasted_iota(jnp.int32, (ABLK, ABLK), 1)
    return jnp.where((col >= row) if prev else (col <= row), s, -1e30)


def _block_rows(dil, r, b):
    if dil == 1:
        return pl.ds(pl.multiple_of(b * ABLK, ABLK), ABLK)
    return pl.ds(r + dil * ABLK * b, ABLK, stride=dil)


def _group_blocks(dil):
    nb = SEQ // dil // ABLK
    shift = nb.bit_length() - 1
    return nb, (lambda idx: (idx >> shift, idx & (nb - 1)))


def _qkv_specs(j_of):
    return [pl.BlockSpec((SEQ, HD), functools.partial(lambda j, c: (0, c + j_of(j)), c=(cb + g) * 4))
            for g in range(3) for cb in (CB_Q, CB_K, CB_V)]


def _attention_fwd(name, proj):
    def body(*refs):
        qkv, z_ref = refs[:9], refs[9]
        y_ref, ya_ref, l_ref = refs[10:13]
        accs, maxs, dens = refs[13:16], refs[16:19], refs[19:22]
        for g, dil in enumerate(DILATIONS):
            q_ref, k_ref, v_ref = qkv[3 * g:3 * g + 3]
            nb, where = _group_blocks(dil)

            def step(t, c, g=g, dil=dil, nb=nb, where=where, q_ref=q_ref, k_ref=k_ref, v_ref=v_ref):
                two = range(ATTN_PAIR)
                rb = [where(t + i * (SEQ // ABLK // ATTN_PAIR)) for i in two]
                rows = [_block_rows(dil, r, b) for r, b in rb]
                qb = [q_ref[rows[i], :].astype(BF16) for i in two]
                s_c = [_scores(qb[i], k_ref[rows[i], :].astype(BF16), False) for i in two]
                if nb > 1:
                    prev = [_block_rows(dil, r, jnp.maximum(b - 1, 0)) for r, b in rb]
                    s_p = [jnp.where(rb[i][1] > 0, _scores(qb[i], k_ref[prev[i], :].astype(BF16), True), -1e30)
                           for i in two]
                m = [jnp.max(s_c[i], axis=-1, keepdims=True) for i in two]
                if nb > 1:
                    m = [jnp.maximum(m[i], jnp.max(s_p[i], axis=-1, keepdims=True)) for i in two]
                p_c = [jnp.exp(s_c[i] - m[i]) for i in two]
                den = [jnp.sum(p_c[i], axis=-1, keepdims=True) for i in two]
                acc = [jnp.dot(p_c[i].astype(BF16), v_ref[rows[i], :].astype(BF16), preferred_element_type=F32)
                       for i in two]
                if nb > 1:
                    p_p = [jnp.exp(s_p[i] - m[i]) for i in two]
                    den = [den[i] + jnp.sum(p_p[i], axis=-1, keepdims=True) for i in two]
                    acc = [acc[i] + jnp.dot(p_p[i].astype(BF16), v_ref[prev[i], :].astype(BF16),
                                            preferred_element_type=F32) for i in two]
                for i in two:
                    accs[g][rows[i], :] = acc[i]
                    maxs[g][rows[i], :] = jnp.broadcast_to(m[i], (ABLK, HD))
                    dens[g][rows[i], :] = jnp.broadcast_to(den[i], (ABLK, HD))
                return c

            lax.fori_loop(0, SEQ // ABLK // ATTN_PAIR, step, 0)
        top = jnp.maximum(jnp.maximum(maxs[0][...], maxs[1][...]), maxs[2][...])
        den = jnp.zeros((SEQ, HD), F32)
        y = jnp.zeros((SEQ, HD), F32)
        for g in range(3):
            wgt = jnp.exp(maxs[g][...] - top)
            den = den + wgt * dens[g][...]
            y = y + wgt * accs[g][...]
        y = y / den
        y_ref[...] = y
        ya_ref[...] = (y * _silu(z_ref[...])).astype(ya_ref.dtype)
        l_ref[...] = top + jnp.log(den)

    ospec = pl.BlockSpec((SEQ, HD), lambda j: (0, j))
    return pl.pallas_call(
        body, name=name, grid=(AW // HD,),
        in_specs=_qkv_specs(lambda j: j) + [pl.BlockSpec((SEQ, HD), lambda j: (0, CB_ZA * 4 + j))],
        out_specs=[ospec, ospec, ospec],
        out_shape=[jax.ShapeDtypeStruct((SEQ, AW), F32), jax.ShapeDtypeStruct((SEQ, AW), BF16),
                   jax.ShapeDtypeStruct((SEQ, AW), F32)],
        scratch_shapes=[pltpu.VMEM((SEQ, HD), F32)] * 9,
        compiler_params=_params("parallel"),
    )(*([proj] * 10))


def _attention_bwd(name, proj, dya, y, lse):
    def tn(a, b_):
        return lax.dot_general(a, b_, (((0,), (0,)), ((), ())), preferred_element_type=F32)

    def nt(a, b_):
        return lax.dot_general(a, b_, (((1,), (1,)), ((), ())), preferred_element_type=F32)

    def body(*refs):
        qkv, z_ref, dya_ref, y_ref, l_ref = refs[:9], refs[9], refs[10], refs[11], refs[12]
        outs, dza_ref = refs[13:22], refs[22]
        dy_s, dsum_s, dq_s, dk_own, dv_own, dk_prev, dv_prev = refs[23:]
        _, vjp = jax.vjp(lambda y_, z_: y_ * _silu(z_), y_ref[...], z_ref[...])
        dy, dz = vjp(dya_ref[...])
        dza_ref[...] = dz.astype(dza_ref.dtype)
        dy_s[...] = dy
        dsum_s[...] = jnp.broadcast_to(jnp.sum(dy * y_ref[...], axis=-1, keepdims=True), (SEQ, HD))
        for g, dil in enumerate(DILATIONS):
            q_ref, k_ref, v_ref = qkv[3 * g:3 * g + 3]
            nb, where = _group_blocks(dil)
            if nb > 1:
                dk_prev[...] = jnp.zeros(dk_prev.shape, F32)
                dv_prev[...] = jnp.zeros(dv_prev.shape, F32)

            def step(t, c, dil=dil, nb=nb, where=where, q_ref=q_ref, k_ref=k_ref, v_ref=v_ref):
                rb = [where(t + i * (SEQ // ABLK // ATTN_PAIR)) for i in range(ATTN_PAIR)]
                sides = []
                for r, b in rb:
                    rows = _block_rows(dil, r, b)
                    own = dict(b=b, qrows=rows, krows=rows, prev=False, dk=dk_own, dv=dv_own,
                               q=q_ref[rows, :].astype(BF16), dy=dy_s[rows, :].astype(BF16))
                    sides.append(own)
                    if nb > 1:
                        sides.append(dict(own, krows=_block_rows(dil, r, jnp.maximum(b - 1, 0)), prev=True,
                                          dk=dk_prev, dv=dv_prev))
                for s_ in sides:
                    s_["k"] = k_ref[s_["krows"], :].astype(BF16)
                    s_["v"] = v_ref[s_["krows"], :].astype(BF16)
                for s_ in sides:
                    sc = _scores(s_["q"], s_["k"], s_["prev"])
                    s_["s"] = jnp.where(s_["b"] > 0, sc, -1e30) if s_["prev"] else sc
                    s_["dp"] = nt(s_["dy"], s_["v"])
                for s_ in sides:
                    p = jnp.exp(s_["s"] - l_ref[s_["qrows"], :])
                    s_["p"] = p.astype(BF16)
                    s_["ds"] = (p * (s_["dp"] - dsum_s[s_["qrows"], :]) * (HD ** -0.5)).astype(BF16)
                for s_ in sides:
                    s_["dk"][s_["krows"], :] = tn(s_["ds"], s_["q"])
                    s_["dv"][s_["krows"], :] = tn(s_["p"], s_["dy"])
                    s_["dq"] = jnp.dot(s_["ds"], s_["k"], preferred_element_type=F32)
                per = len(sides) // ATTN_PAIR
                for i in range(ATTN_PAIR):
                    dq = sides[i * per]["dq"]
                    if per > 1:
                        dq = dq + sides[i * per + 1]["dq"]
                    dq_s[sides[i * per]["qrows"], :] = dq
                return c

            lax.fori_loop(0, SEQ // ABLK // ATTN_PAIR, step, 0)
            dq_ref, dk_ref, dv_ref = outs[3 * g:3 * g + 3]
            dq_ref[...] = dq_s[...].astype(dq_ref.dtype)
            if nb > 1:
                dk_ref[...] = (dk_own[...] + dk_prev[...]).astype(dk_ref.dtype)
                dv_ref[...] = (dv_own[...] + dv_prev[...]).astype(dv_ref.dtype)
            else:
                dk_ref[...] = dk_own[...].astype(dk_ref.dtype)
                dv_ref[...] = dv_own[...].astype(dv_ref.dtype)

    ospec = pl.BlockSpec((SEQ, HD), lambda j: (0, j))
    outs = pl.pallas_call(
        body, name=name, grid=(AW // HD,),
        in_specs=_qkv_specs(lambda j: j) + [pl.BlockSpec((SEQ, HD), lambda j: (0, CB_ZA * 4 + j))] + [ospec] * 3,
        out_specs=[ospec] * 10, out_shape=[jax.ShapeDtypeStruct((SEQ, AW), BF16)] * 10,
        scratch_shapes=[pltpu.VMEM((SEQ, HD), F32)] * 7,
        compiler_params=_params("parallel"),
    )(*([proj] * 10), dya, y, lse)
    return outs[:9], outs[9]


def _rms(x, g):
    return x * lax.rsqrt(jnp.mean(x * x, axis=-1, keepdims=True) + RMS_EPS) * g


def _sig(x):
    return 1.0 / (1.0 + jnp.exp(-x))


def _silu(x):
    return x * _sig(x)


def _gelu(x):
    return 0.5 * x * (1.0 + jnp.tanh(math.sqrt(2.0 / math.pi) * (x + 0.044715 * (x * x * x))))


def _y1_fn(y0p, u, dskip):
    return _gelu(y0p + dskip * u)


def _ys_fn(y1, t, z, bglu):
    return y1 * _sig(t + bglu) * _silu(z)


def _merge_fn(ms, ma, gs, ga):
    return _sig(gs) * ms.astype(F32) + _sig(ga) * ma.astype(F32)


def _colsum(v):
    return jnp.sum(v, axis=0, keepdims=True)


def _lam_fn(lre, lim, ldt):
    a = jnp.minimum(lre, -1e-4)
    dt = jnp.exp(ldt)
    mag = jnp.exp(a * dt)
    ar = mag * jnp.cos(lim * dt)
    ai = mag * jnp.sin(lim * dt)
    den = a * a + lim * lim
    cr = ((ar - 1.0) * a + ai * lim) / den
    ci = (ai * a - (ar - 1.0) * lim) / den
    return ar, ai, cr, ci


def _bbar_fn(cr, ci, bre, bim):
    return cr * bre - ci * bim, cr * bim + ci * bre


def _same_group(rows, a, cols, b):
    r = lax.broadcasted_iota(jnp.int32, (rows, cols), 0) >> (a.bit_length() - 1)
    c = lax.broadcasted_iota(jnp.int32, (rows, cols), 1) >> (b.bit_length() - 1)
    return r == c


def _expand(name, blocks, signs, a, b, dtype, after=None):
    rows, cols = NGRP * a, NGRP * b
    n = len(blocks)
    assert a & (a - 1) == 0 and b & (b - 1) == 0
    after_specs, after_args = _after(after)

    def body(*refs):
        o_ref = refs[-1]
        tile = (lax.broadcasted_iota(jnp.int32, (b, cols), 1) & (b - 1)
                == lax.broadcasted_iota(jnp.int32, (b, cols), 0)).astype(F32)
        keep = _same_group(rows, a, cols, b)
        for i, (ref, sign) in enumerate(zip(refs[:n], signs)):
            spread = jnp.dot(ref[...], tile, preferred_element_type=F32, precision=lax.Precision.HIGHEST)
            o_ref[i * rows:(i + 1) * rows, :] = jnp.where(keep, sign * spread, 0.0).astype(o_ref.dtype)

    return pl.pallas_call(body, name=name, grid=(1,),
                          in_specs=[pl.BlockSpec((rows, b), lambda i: (0, 0))] * n + after_specs,
                          out_specs=pl.BlockSpec((n * rows, cols), lambda i: (0, 0)),
                          out_shape=jax.ShapeDtypeStruct((n * rows, cols), dtype),
                          compiler_params=_params("arbitrary"))(*blocks, *after_args)


def _extract(name, m, a, b, ats, after=None):
    rows, cols = NGRP * a, NGRP * b
    n = len(ats)
    assert a & (a - 1) == 0 and b & (b - 1) == 0
    after_specs, after_args = _after(after)

    def body(*refs):
        tile = (lax.broadcasted_iota(jnp.int32, (cols, b), 0) & (b - 1)
                == lax.broadcasted_iota(jnp.int32, (cols, b), 1)).astype(F32)
        keep = _same_group(rows, a, cols, b)
        for m_ref, o_ref in zip(refs[:n], refs[-n:]):
            kept = jnp.where(keep, m_ref[...], 0.0)
            o_ref[...] = jnp.dot(kept, tile, preferred_element_type=F32, precision=lax.Precision.HIGHEST)

    return pl.pallas_call(
        body, name=name, grid=(1,),
        in_specs=[pl.BlockSpec((rows, cols), functools.partial(lambda i, at: at, at=at)) for at in ats] + after_specs,
        out_specs=[pl.BlockSpec((rows, b), lambda i: (0, 0))] * n,
        out_shape=[jax.ShapeDtypeStruct((rows, b), F32)] * n,
        compiler_params=_params("arbitrary"))(*([m] * n), *after_args)


def _s5_prepare(l, sp, after):
    tag = f"l{l}_"
    ar, ai, cr, ci = _ew(tag + "lam", _lam_fn, NGRP, NGRP,
                         [_ri(sp["lambda_re"]), _ri(sp["lambda_im"]),
                          _ri(sp["log_dt"].reshape(NGRP, 1) + after[0, 0])], [], [(NST, F32)] * 4)
    bre = sp["b_re"].reshape(NS, GCH)
    bim = sp["b_im"].reshape(NS, GCH)
    bbr, bbi = _ew(tag + "bbar", _bbar_fn, NS, NS, [_ri(cr.reshape(NS, 1)), _ri(ci.reshape(NS, 1)), _ri(bre), _ri(bim)],
                   [], [(GCH, F32)] * 2)
    wdt = _expand(tag + "wdt", [bbr, bbi], [1.0, 1.0], NST, GCH, BF16)
    cmt = _expand(tag + "cmt", [sp["c_re"].reshape(SW, NST), sp["c_im"].reshape(SW, NST)], [1.0, -1.0], GCH, NST, BF16,
                  after=after)
    return dict(ar=ar, ai=ai, cr=cr, ci=ci, wdt=wdt, cmt=cmt)


def _layer_head(l, x, sp):
    g1 = sp["pre_norm_g"].reshape(1, DM)
    (h,) = _ew(f"l{l}_rms1", lambda x_, g: (_rms(x_, g),), SEQ, EW_ROWS, [_ri(x)], [g1], [(DM, BF16)])
    return jnp.stack([h, _to_chunked(h)])


def _layer_fwd(l, x, hv, w, sp, prep, after):
    tag = f"l{l}_"
    win = w["w_in"]
    proj = _mm(tag + "proj", hv, win, "nn", SEQ, NCOL, DM, SEQ, 512, 1024, F32,
               a_spec=pl.BlockSpec((None, SEQ, 1024), lambda i, j, k: (_row_order(j), 0, 0)),
               b_spec=pl.BlockSpec((None, 1024, 512), lambda i, j, k: (j // 2, 0, j % 2)), after=after)

    ar, ai, cr, ci, wdt, cmt = (prep[k] for k in ("ar", "ai", "cr", "ci", "wdt", "cmt"))
    s, y0p = _s5_forward(tag + "s5", proj, wdt, cmt, ar.reshape(1, NS), ai.reshape(1, NS))
    dskip = sp["d_skip"].reshape(1, SW)
    (y1,) = _ew(tag + "y1", lambda a, u, d: (_y1_fn(a, u, d),), SEQ, EW_ROWS,
                [_ri(y0p), _ri(proj, SW, CB_U)], [dskip], [(SW, F32)])
    t = _mm(tag + "glu", y1, w["w_glu"], "nn", SEQ, SW, SW, 512, 512, 512, F32)
    bglu = sp["b_glu"].reshape(1, SW)
    (ys_c,) = _ew(tag + "ys", lambda y1_, t_, z, b_: (_ys_fn(y1_, t_, z, b_),), SEQ, EW_ROWS,
                  [_ri(y1), _ri(t), _ri(proj, SW, CB_ZS)], [bglu], [(SW, BF16)])
    ys = _from_chunked(ys_c)

    ypre, ya, lse = _attention_fwd(tag + "attn", proj)

    ms = _mm(tag + "branch_s", ys, w["w_branch_s"], "nn", SEQ, DM, SW, 512, 1024, 512, BF16)
    ma = _mm(tag + "branch_a", ya, w["w_branch_a"], "nn", SEQ, DM, AW, 512, 1024, 512, BF16)
    (merged,) = _ew(tag + "merge", lambda a, b_, c, d: (_merge_fn(a, b_, c, d),), SEQ, EW_ROWS,
                    [_ri(ms), _ri(ma), _ri(proj, DM, CB_GS), _ri(proj, DM, CB_GA)], [], [(DM, BF16)])
    out = _mm(tag + "out", merged, w["w_out"], "nn", SEQ, DM, DM, 512, 1024, 1024, F32)
    g2 = sp["post_norm_g"].reshape(1, DM)
    (x_new,) = _ew(tag + "post", lambda x_, o, g: (x_ + _rms(o, g),), SEQ, EW_ROWS, [_ri(x), _ri(out)], [g2], [(DM, F32)])
    res = dict(x=x, hv=hv, proj=proj, ar=ar, ai=ai, cr=cr, ci=ci, wdt=wdt, cmt=cmt, s=s, y0p=y0p,
               y1=y1, t=t, ys=ys, ya=ya, ypre=ypre, lse=lse, ms=ms, ma=ma, merged=merged, out=out)
    return x_new, res


def _layer_bwd(l, dxn, r, w, sp, big_done, after=None):
    tag = f"l{l}b_"
    proj = r["proj"]
    g1 = sp["pre_norm_g"].reshape(1, DM)
    g2 = sp["post_norm_g"].reshape(1, DM)
    dskip = sp["d_skip"].reshape(1, SW)
    bglu = sp["b_glu"].reshape(1, SW)

    def post_b(d, o, g):
        _, vjp = jax.vjp(_rms, o, g)
        do, dg = vjp(d)
        return do, dg

    d_out, dg2 = _ew(tag + "post", post_b, SEQ, EW_ROWS, [_ri(dxn), _ri(r["out"])], [g2], [(DM, BF16)], [DM])
    dw_out = _mm(tag + "dw_out", r["merged"], d_out, "tn", DM, DM, SEQ, 1024, 256, SEQ, BF16, after=after)
    dmerged = _mm(tag + "dmerged", d_out, w["w_out"], "nt", SEQ, DM, DM, 512, 1024, 1024, F32, after=after)

    def merge_b(d, ms, ma, gs, ga):
        _, vjp = jax.vjp(_merge_fn, ms, ma, gs, ga)
        return vjp(d)

    dms, dma, dgs, dga = _ew(tag + "merge", merge_b, SEQ, EW_ROWS,
                             [_ri(dmerged), _ri(r["ms"]), _ri(r["ma"]), _ri(proj, DM, CB_GS), _ri(proj, DM, CB_GA)],
                             [], [(DM, BF16)] * 4)
    dw_bs = _mm(tag + "dw_bs", r["ys"], dms, "tn", SW, DM, SEQ, 512, 256, SEQ, BF16)
    dw_ba = _mm(tag + "dw_ba", r["ya"], dma, "tn", AW, DM, SEQ, 512, 256, SEQ, BF16)
    dys = _mm(tag + "dys", dms, w["w_branch_s"], "nt", SEQ, SW, DM, 512, 512, 1024, F32)
    dya = _mm(tag + "dya", dma, w["w_branch_a"], "nt", SEQ, AW, DM, 512, 512, 1024, F32)

    dqkv, dza = _attention_bwd(tag + "attn", proj, dya, r["ypre"], r["lse"])

    def ys_b(d, y1, t, z, b_):
        _, vjp = jax.vjp(_ys_fn, y1, t, z, b_)
        dy1, dt, dz, _ = vjp(d)
        return dy1, dt, dz, _colsum(dt)

    dy1a, dt, dzs, dbglu = _ew(tag + "ys", ys_b, SEQ, EW_ROWS,
                               [_ri(_to_chunked(dys)), _ri(r["y1"]), _ri(r["t"]), _ri(proj, SW, CB_ZS)],
                               [bglu], [(SW, F32), (SW, BF16), (SW, BF16)], [SW])
    dw_glu = _mm(tag + "dw_glu", r["y1"], dt, "tn", SW, SW, SEQ, 512, 256, SEQ, BF16)
    dy1b = _mm(tag + "dy1b", dt, w["w_glu"], "nt", SEQ, SW, SW, 512, 512, 512, F32)

    def y1_b(da, db, y0p, u, d_):
        _, vjp = jax.vjp(_y1_fn, y0p, u, d_)
        dy0, du, dd = vjp(da + db)
        return dy0, du, dd

    dy0, du_skip, ddskip = _ew(tag + "y1", y1_b, SEQ, EW_ROWS,
                               [_ri(dy1a), _ri(dy1b), _ri(r["y0p"]), _ri(proj, SW, CB_U)], [dskip],
                               [(SW, BF16), (SW, F32)], [SW])
    dlr, dli, dwdt, dcmt, du_s = _s5_backward(tag + "s5", dy0, proj, r["s"], r["wdt"], r["cmt"],
                                              r["ar"].reshape(1, NS), r["ai"].reshape(1, NS))
    (du,) = _ew(tag + "du", lambda a, c: (a + c,), SEQ, EW_ROWS, [_ri(du_s), _ri(du_skip)], [], [(SW, BF16)])

    dq, dk, dv = ([dqkv[3 * g + i] for g in range(3)] for i in range(3))
    dproj = jnp.concatenate([du, dzs, *dq, *dk, *dv, dza, dgs, dga], axis=1)
    dw_in = _mm(tag + "dw_in", r["hv"], dproj, "tn", DM, NCOL, SEQ, 1024, 512, SEQ, BF16,
                a_spec=pl.BlockSpec((None, SEQ, 1024), lambda i, j, k: (_row_order(j), 0, 0)),
                o_spec=pl.BlockSpec((None, 1024, 512), lambda i, j, k: (j // 2, 0, j % 2)), out_shape=(NDEV, DM, DM))
    tok = big_done(l, dict(w_in=dw_in, w_glu=dw_glu.reshape(NDEV, SW // NDEV, SW),
                           w_branch_s=dw_bs.reshape(SW, NDEV, DM // NDEV).transpose(1, 0, 2),
                           w_branch_a=dw_ba.reshape(AW, NDEV, DM // NDEV).transpose(1, 0, 2),
                           w_out=dw_out.reshape(NDEV, DM // NDEV, DM)))
    if tok is not None:
        g1 = g1 + tok[0, 0]

    dwdt = dwdt.reshape(2 * NS, SW)
    dcmt = dcmt.reshape(2 * SW, NS)
    dbbr, dbbi = _extract(tag + "dbb", dwdt, NST, GCH, ((0, 0), (1, 0)), after=tok)
    bre = sp["b_re"].reshape(NS, GCH)
    bim = sp["b_im"].reshape(NS, GCH)

    def bbar_b(cr, ci, br_, bi_, dr, di):
        _, vjp = jax.vjp(_bbar_fn, cr, ci, br_, bi_)
        return vjp((dr, di))

    dcr, dci, dbre, dbim = _ew(tag + "bbar", bbar_b, NS, NS,
                               [_ri(r["cr"].reshape(NS, 1)), _ri(r["ci"].reshape(NS, 1)), _ri(bre), _ri(bim),
                                _ri(dbbr), _ri(dbbi)], [], [(1, F32), (1, F32), (GCH, F32), (GCH, F32)])

    def lam_b(lre, lim, ldt, dar, dai, dcr_, dci_):
        _, vjp = jax.vjp(_lam_fn, lre, lim, ldt)
        return vjp((dar, dai, dcr_, dci_))

    dlre, dlim, dldt = _ew(tag + "lam", lam_b, NGRP, NGRP,
                           [_ri(sp["lambda_re"]), _ri(sp["lambda_im"]), _ri(sp["log_dt"].reshape(NGRP, 1)),
                            _ri(dlr.reshape(NGRP, NST)), _ri(dli.reshape(NGRP, NST)),
                            _ri(dcr.reshape(NGRP, NST)), _ri(dci.reshape(NGRP, NST))], [],
                           [(NST, F32), (NST, F32), (1, F32)])
    dc_re, dc_im = _extract(tag + "dc", dcmt, GCH, NST, ((0, 0), (1, 0)), after=tok)
    dc_re, dc_im = dc_re.reshape(NGRP, GCH, NST), -dc_im.reshape(NGRP, GCH, NST)

    dh_time = _mm(tag + "dh_time", dproj, w["w_in"], "nt", SEQ, DM, NCOL - DM, SEQ, 1024, 1024, F32,
                  a_spec=pl.BlockSpec((SEQ, 1024), lambda i, j, k: (0, 1 + k)),
                  b_spec=pl.BlockSpec((None, 1024, 1024), lambda i, j, k: (1 + k, 0, 0)), after=tok)
    dh_chunked = _mm(tag + "dh_chunked", dproj, w["w_in"], "nt", SEQ, DM, DM, 512, 1024, 1024, F32,
                     a_spec=pl.BlockSpec((512, 1024), lambda i, j, k: (i, 0)),
                     b_spec=pl.BlockSpec((None, 1024, 1024), lambda i, j, k: (0, 0, 0)), after=tok)
    dh = [dh_time, _from_chunked(dh_chunked)]

    def pre_b(d, dh0, dh1, x_, g):
        _, vjp = jax.vjp(_rms, x_, g)
        dx_, dg = vjp(dh0 + dh1)
        return d + dx_, dg

    dx, dg1 = _ew(tag + "pre", pre_b, SEQ, EW_ROWS, [_ri(dxn)] + [_ri(t_) for t_ in dh] + [_ri(r["x"])], [g1],
                  [(DM, F32)], [DM])

    small = dict(pre_norm_g=dg1.reshape(DM), lambda_re=dlre, lambda_im=dlim, log_dt=dldt.reshape(NGRP),
                 b_re=dbre.reshape(NGRP, NST, GCH), b_im=dbim.reshape(NGRP, NST, GCH), c_re=dc_re, c_im=dc_im,
                 d_skip=ddskip.reshape(SW), b_glu=dbglu.reshape(SW), post_norm_g=dg2.reshape(DM))
    return dx, small


_HBM = pl.BlockSpec(memory_space=pltpu.HBM)
_SEM = pl.BlockSpec(memory_space=pltpu.SEMAPHORE)
_EFFECT = pltpu.SideEffectType.DATAFLOW_SIDE_EFFECTING


def _remote_copies(srcs, dsts, send_sems, recv_sems, gather):
    x, y, c = lax.axis_index("x"), lax.axis_index("y"), lax.axis_index("c")
    me = 4 * x + 2 * y + c
    copies = []
    for i in range(len(srcs)):
        for k in range(1, NDEV):
            peer = (x ^ (k >> 2), y ^ ((k >> 1) & 1), c ^ (k & 1))
            src = srcs[i] if gather[i] else srcs[i].at[me ^ k]
            copies.append(pltpu.make_async_remote_copy(
                src_ref=src, dst_ref=dsts[i].at[me], send_sem=send_sems[i], recv_sem=recv_sems[i],
                device_id=peer, device_id_type=pl.DeviceIdType.MESH))
    return copies


def _all_seven(dst, send_sem, recv_sem):
    seven = dst.at[pl.ds(0, NDEV - 1)]
    me = (lax.axis_index("x"), lax.axis_index("y"), lax.axis_index("c"))
    return pltpu.make_async_remote_copy(src_ref=seven, dst_ref=seven, send_sem=send_sem, recv_sem=recv_sem,
                                        device_id=me, device_id_type=pl.DeviceIdType.MESH)


def _own_slabs(name, arrs, gather, after):
    n = len(arrs)
    me = (4 * lax.axis_index("x") + 2 * lax.axis_index("y") + lax.axis_index("c")).astype(jnp.int32).reshape(1)

    def body(me_ref, *refs):
        for src, dst in zip(refs[:n], refs[n + 1:]):
            dst[...] = src[...]

    def zeros(k):
        return (0,) * k

    in_specs, out_specs, out_shape = [], [], []
    for a, g in zip(arrs, gather):
        slab = a.shape if g else a.shape[1:]
        nd = len(slab)
        if g:
            in_specs.append(pl.BlockSpec(slab, functools.partial(lambda i, me_ref, nd: zeros(nd), nd=nd)))
        else:
            in_specs.append(pl.BlockSpec((None,) + slab, functools.partial(lambda i, me_ref, nd: (me_ref[0],) + zeros(nd), nd=nd)))
        out_specs.append(pl.BlockSpec((None,) + slab, functools.partial(lambda i, me_ref, nd: (me_ref[0],) + zeros(nd), nd=nd)))
        out_shape.append(jax.ShapeDtypeStruct((NDEV,) + slab, a.dtype))
    in_specs.append(pl.BlockSpec(memory_space=pl.ANY))
    return pl.pallas_call(
        body, name=name, out_shape=out_shape,
        grid_spec=pltpu.PrefetchScalarGridSpec(num_scalar_prefetch=1, grid=(1,), in_specs=in_specs, out_specs=out_specs),
        compiler_params=_params("arbitrary"),
    )(me, *arrs, after)


def _exchange_start(name, arrs, gather, after):
    n = len(arrs)
    lands = _own_slabs(name + "_own", arrs, gather, after)

    def body(*refs):
        srcs, dsts = refs[:n], refs[n:2 * n]
        send_sems, recv_sems = refs[2 * n:3 * n], refs[3 * n:4 * n]
        token = refs[-1]
        for cp in _remote_copies(srcs, dsts, send_sems, recv_sems, gather):
            cp.start()
        token[...] = jnp.zeros(token.shape, token.dtype)

    thru = [pltpu.HBM(a.shape, a.dtype) for a in list(arrs) + list(lands)]
    outs = pl.pallas_call(
        body, name=name,
        out_shape=(*[pltpu.SemaphoreType.DMA(())] * (2 * n), *thru, jax.ShapeDtypeStruct((8, 128), F32)),
        in_specs=[_HBM] * (2 * n),
        out_specs=(*[_SEM] * (2 * n), *[_HBM] * (2 * n), pl.BlockSpec(memory_space=pltpu.VMEM)),
        input_output_aliases={i: 2 * n + i for i in range(2 * n)},
        compiler_params=pltpu.CompilerParams(has_side_effects=_EFFECT),
    )(*[pltpu.with_memory_space_constraint(a, pltpu.HBM) for a in list(arrs) + list(lands)])
    return dict(send=outs[:n], recv=outs[n:2 * n], srcs=outs[2 * n:3 * n], lands=outs[3 * n:4 * n], token=outs[-1],
                gather=gather)


def _exchange_wait(name, started, after):
    n = len(started["srcs"])
    after = list(after)

    def body(*refs):
        dsts = refs[n:2 * n]
        send_sems, recv_sems = refs[2 * n:3 * n], refs[3 * n:4 * n]
        for i in range(n):
            cp = _all_seven(dsts[i], send_sems[i], recv_sems[i])
            cp.wait_send()
            cp.wait_recv()

    bufs = list(started["srcs"]) + list(started["lands"])
    outs = pl.pallas_call(
        body, name=name, out_shape=tuple(pltpu.HBM(a.shape, a.dtype) for a in bufs),
        in_specs=[_HBM] * (2 * n) + [_SEM] * (2 * n) + [pl.BlockSpec(memory_space=pl.ANY)] * len(after),
        out_specs=(_HBM,) * (2 * n), input_output_aliases={i: i for i in range(2 * n)},
        compiler_params=pltpu.CompilerParams(has_side_effects=_EFFECT),
    )(*bufs, *started["send"], *started["recv"], *after)
    return outs[n:]


def _sum_in_order(parts):
    g = parts[0].astype(F32)
    for p in parts[1:]:
        g = g + p.astype(F32)
    return g


def _adam_update(g, w_, m_, v_):
    m2 = B1 * m_ + (1.0 - B1) * g
    v2 = B2 * v_ + (1.0 - B2) * (g * g)
    m_hat = m2 / (1.0 - B1 ** STEP)
    v_hat = v2 / (1.0 - B2 ** STEP)
    delta = -LR * (m_hat / (jnp.sqrt(v_hat) + ADAM_EPS) + WD * w_)
    return g, delta, m2, v2


def _adamw(name, g, w, m, v, br):
    rows, cols = w.shape
    return _ew(name, _adam_update, rows, br, [_ri(g), _ri(w), _ri(m), _ri(v)], [], [(cols, F32)] * 4)


def _adamw_layer(name, l, parts, w, m, v, br, outs):
    rows, cols = w.shape
    nb = rows // DEPTH // br
    assert nb * br * DEPTH == rows

    def body(*refs):
        vals = _adam_update(_sum_in_order([r[...] for r in refs[:NDEV]]), *[r[...] for r in refs[NDEV:NDEV + 3]])
        for r, val in zip(refs[-4:], vals):
            r[...] = val

    mine = pl.BlockSpec((br, cols), lambda i: (l * nb + i, 0))
    in_specs = [pl.BlockSpec((br, cols), functools.partial(lambda i, d: (d * nb + i, 0), d=d)) for d in range(NDEV)]
    in_specs += [mine] * 3 + [pl.BlockSpec(memory_space=pl.ANY)] * len(outs)
    return pl.pallas_call(
        body, name=f"{name}{l}", grid=(nb,), in_specs=in_specs, out_specs=[mine] * 4,
        out_shape=[jax.ShapeDtypeStruct((rows, cols), F32)] * 4,
        input_output_aliases={NDEV + 3 + q: q for q in range(len(outs))},
        compiler_params=_params("arbitrary"),
    )(*([parts] * NDEV), w, m, v, *outs)


SMALL = ("pre_norm_g", "lambda_re", "lambda_im", "log_dt", "b_re", "b_im", "c_re", "c_im", "d_skip", "b_glu",
         "post_norm_g")
BIG = ("w_in", "w_glu", "w_branch_s", "w_branch_a", "w_out")
WEIGHTS = ("pre_norm_g", "w_in", "lambda_re", "lambda_im", "log_dt", "b_re", "b_im", "c_re", "c_im", "d_skip",
           "w_glu", "b_glu", "w_branch_s", "w_branch_a", "w_out", "post_norm_g")
PACK_COLS = 1024
PACK_BR = 136


def _pack_layer(d):
    pieces = [d[k].astype(F32).reshape(-1) for k in SMALL]
    used = sum(p.shape[0] for p in pieces)
    assert used <= PACK_BR * PACK_COLS
    return jnp.concatenate(pieces + [jnp.zeros((PACK_BR * PACK_COLS - used,), F32)]).reshape(PACK_BR, PACK_COLS)


def _unpack(p, like):
    flat = p.reshape(DEPTH, PACK_BR * PACK_COLS)
    out, off = {}, 0
    for k in SMALL:
        n = like[k].size // DEPTH
        out[k] = flat[:, off:off + n].reshape(like[k].shape)
        off += n
    return out


def _local_step(x, target, small, started, weights_of, big_done, small_done, total_loss):
    preps = [_s5_prepare(l, {k: small[k][l] for k in SMALL}, started) for l in range(DEPTH)]
    res, ws = [], []
    for l in range(DEPTH):
        sp = {k: small[k][l] for k in SMALL}
        hv = _layer_head(l, x, sp)
        w_l, tok = weights_of(l, [hv] + ([p[k] for p in preps for k in ("wdt", "cmt")] if l == 0 else []))
        x, r = _layer_fwd(l, x, hv, w_l, sp, preps[l], tok)
        res.append(r)
        ws.append(w_l)

    def loss_fn(y, t):
        e = y - t
        return e * (1.0 / DM), jnp.sum(_colsum(0.5 * e * e * (1.0 / DM)), axis=1, keepdims=True)

    dx, loss = _ew("loss", loss_fn, SEQ, EW_ROWS, [_ri(x), _ri(target)], [], [(DM, F32)], [1])
    total = total_loss(loss.reshape(()))
    for l in reversed(range(DEPTH)):
        dx, sm = _layer_bwd(l, dx, res[l], ws[l], {k: small[k][l] for k in SMALL}, big_done,
                            after=total.reshape(1, 1) if l == DEPTH - 1 else None)
        small_done(l, sm)
    return total, dx


def _full_weights(gathered):
    g = gathered
    return dict(
        w_in=g["w_in"],
        w_glu=g["w_glu"].reshape(SW, SW),
        w_branch_s=g["w_branch_s"].transpose(1, 0, 2).reshape(SW, DM),
        w_branch_a=g["w_branch_a"].transpose(1, 0, 2).reshape(AW, DM),
        w_out=g["w_out"].reshape(DM, DM),
    )


def kernel(x, pre_norm_g, w_in, lambda_re, lambda_im, log_dt, b_re, b_im, c_re, c_im, d_skip, w_glu, b_glu, w_branch_s, w_branch_a, w_out, post_norm_g, loss_target, m_pre_norm_g, m_w_in, m_lambda_re, m_lambda_im, m_log_dt, m_b_re, m_b_im, m_c_re, m_c_im, m_d_skip, m_w_glu, m_b_glu, m_w_branch_s, m_w_branch_a, m_w_out, m_post_norm_g, v_pre_norm_g, v_w_in, v_lambda_re, v_lambda_im, v_log_dt, v_b_re, v_b_im, v_c_re, v_c_im, v_d_skip, v_w_glu, v_b_glu, v_w_branch_s, v_w_branch_a, v_w_out, v_post_norm_g):
    wts = dict(pre_norm_g=pre_norm_g, w_in=w_in, lambda_re=lambda_re, lambda_im=lambda_im, log_dt=log_dt, b_re=b_re,
               b_im=b_im, c_re=c_re, c_im=c_im, d_skip=d_skip, w_glu=w_glu, b_glu=b_glu, w_branch_s=w_branch_s,
               w_branch_a=w_branch_a, w_out=w_out, post_norm_g=post_norm_g)
    mom = dict(pre_norm_g=m_pre_norm_g, w_in=m_w_in, lambda_re=m_lambda_re, lambda_im=m_lambda_im, log_dt=m_log_dt,
               b_re=m_b_re, b_im=m_b_im, c_re=m_c_re, c_im=m_c_im, d_skip=m_d_skip, w_glu=m_w_glu, b_glu=m_b_glu,
               w_branch_s=m_w_branch_s, w_branch_a=m_w_branch_a, w_out=m_w_out, post_norm_g=m_post_norm_g)
    var = dict(pre_norm_g=v_pre_norm_g, w_in=v_w_in, lambda_re=v_lambda_re, lambda_im=v_lambda_im, log_dt=v_log_dt,
               b_re=v_b_re, b_im=v_b_im, c_re=v_c_re, c_im=v_c_im, d_skip=v_d_skip, w_glu=v_w_glu, b_glu=v_b_glu,
               w_branch_s=v_w_branch_s, w_branch_a=v_w_branch_a, w_out=v_w_out, post_norm_g=v_post_norm_g)

    def gather_start(l, after):
        return _exchange_start(f"gather_start{l}", [wts[k][l].astype(BF16) for k in BIG], [True] * len(BIG), after)

    gathering = {0: gather_start(0, x)}
    sending, packed = {}, {}

    def weights_of(l, after):
        gathered = _exchange_wait(f"gather_wait{l}", gathering[l], after)
        tok = None
        if l + 1 < DEPTH:
            gathering[l + 1] = gather_start(l + 1, gathered[0])
            tok = gathering[l + 1]["token"]
        return _full_weights(dict(zip(BIG, gathered))), tok

    def big_done(l, big):
        arrs, kinds = [big[k] for k in BIG], [False] * len(BIG)
        if l + 1 < DEPTH:
            arrs, kinds = arrs + [packed[l + 1]], kinds + [True]
        sending[l] = _exchange_start(f"grads_start{l}", arrs, kinds, big["w_in"])
        return sending[l]["token"]

    def small_done(l, sm):
        packed[l] = _pack_layer(sm)

    loss, dx = _local_step(x[0], loss_target[0], wts, gathering[0]["token"], weights_of, big_done, small_done,
                           lambda part: lax.psum(part, ("x", "y", "c")))
    last = _exchange_start("grads_start_last", [packed[0]], [True], dx)

    grads, delta, new_m, new_v = {}, {}, {}, {}

    def as_rows(k):
        cols = wts[k].shape[-1]
        rows = wts[k].size // cols
        return rows, cols, [t[k].reshape(rows, cols) for t in (wts, mom, var)]

    recv_l, outs, after = {}, {k: [] for k in BIG}, [dx]
    for l in reversed(range(DEPTH)):
        recv_l[l] = _exchange_wait(f"grads_wait{l}", sending[l], after)
        for i, k in enumerate(BIG):
            rows, cols, wmv = as_rows(k)
            per_layer = rows // DEPTH
            outs[k] = _adamw_layer("adamw_" + k, l, recv_l[l][i].reshape(NDEV * per_layer, cols), *wmv,
                                   min(per_layer, 256), outs[k])
        after = [outs[k][0] for k in BIG]
    for k in BIG:
        grads[k], delta[k], new_m[k], new_v[k] = (o.reshape(wts[k].shape) for o in outs[k])

    def update(k, g):
        rows, cols, wmv = as_rows(k)
        res = _adamw("adamw_" + k, g.reshape(rows, cols), *wmv, min(rows, 1024 if cols <= 128 else 256))
        grads[k], delta[k], new_m[k], new_v[k] = (o.reshape(wts[k].shape) for o in res)

    recv_last = _exchange_wait("grads_wait_last", last, after)
    recv_small = jnp.concatenate([recv_last[0]] + [recv_l[l][len(BIG)] for l in range(DEPTH - 1)], axis=1)
    rows = DEPTH * PACK_BR
    (g_small,) = _ew("grads_small", lambda *p: (_sum_in_order(p),), rows, PACK_BR,
                     [_ri(recv_small.reshape(NDEV * rows, PACK_COLS), PACK_COLS, 0, d * DEPTH) for d in range(NDEV)], [],
                     [(PACK_COLS, F32)])
    for k, g in _unpack(g_small, wts).items():
        update(k, g)

    return (loss, dx[None], *[grads[k] for k in WEIGHTS], *[delta[k] for k in WEIGHTS],
            *[new_m[k] for k in WEIGHTS], *[new_v[k] for k in WEIGHTS])
```

```python
import functools
import math

import jax
import jax.numpy as jnp
from jax import lax
from jax.experimental import pallas as pl
from jax.experimental.pallas import tpu as pltpu

F32 = jnp.float32
BF16 = jnp.bfloat16

NDEV = 8
DEPTH = 4
SEQ = 2048
DM = 1024
NCOL = 8192
SW = 512
NGRP = 32
GCH = 16
NST = 64
NS = NGRP * NST
HD = 128
AW = 512
DILATIONS = (1, 4, 16)
ABLK = 128
ATTN_PAIR = 4
ATTN_PAIR_FWD = 8
RMS_EPS = 1e-6
LR, B1, B2, ADAM_EPS, WD, STEP = 0.001, 0.9, 0.999, 1e-08, 0.01, 10

CB_U, CB_ZS, CB_Q, CB_K, CB_V, CB_ZA = 0, 1, 2, 5, 8, 11
CB_GS, CB_GA = 6, 7

VMEM_LIMIT = 56 * 2 ** 20
EW_ROWS = 512


def _row_order(j):
    return jnp.where(j < CB_Q, 1, 0)


def _params(*sem):
    return pltpu.CompilerParams(dimension_semantics=sem, vmem_limit_bytes=VMEM_LIMIT)


def _ew(name, fn, rows, br, row_ins, bc_ins, row_outs, red_outs=()):
    n_in = len(row_ins) + len(bc_ins)
    n_ro = len(row_outs)
    steps = rows // br
    assert steps * br == rows

    def body(*refs):
        vals = fn(*[r[...] for r in refs[:n_in]])
        outs = refs[n_in:]
        for r, v in zip(outs[:n_ro], vals[:n_ro]):
            r[...] = v.astype(r.dtype)
        if red_outs:
            @pl.when(pl.program_id(0) == 0)
            def _():
                for r in outs[n_ro:]:
                    r[...] = jnp.zeros(r.shape, r.dtype)
            for r, v in zip(outs[n_ro:], vals[n_ro:]):
                r[...] += v

    in_specs = []
    for (_, w, cb, rb) in row_ins:
        in_specs.append(pl.BlockSpec((br, w), functools.partial(lambda i, cb, rb: (rb + i, cb), cb=cb, rb=rb)))
    for a in bc_ins:
        in_specs.append(pl.BlockSpec(a.shape, functools.partial(lambda i, nd: (0,) * nd, nd=a.ndim)))
    out_specs = [pl.BlockSpec((br, w), lambda i: (i, 0)) for (w, _) in row_outs]
    out_specs += [pl.BlockSpec((1, w), lambda i: (0, 0)) for w in red_outs]
    out_shape = [jax.ShapeDtypeStruct((rows, w), dt) for (w, dt) in row_outs]
    out_shape += [jax.ShapeDtypeStruct((1, w), F32) for w in red_outs]
    return pl.pallas_call(
        body, name=name, grid=(steps,), in_specs=in_specs, out_specs=out_specs, out_shape=out_shape,
        compiler_params=_params("arbitrary"),
    )(*[a for (a, _, _, _) in row_ins], *bc_ins)


def _ri(a, w=None, cb=0, rb=0):
    return (a, a.shape[1] if w is None else w, cb, rb)


_DIMS = {"nn": ((1,), (0,)), "nt": ((1,), (1,)), "tn": ((0,), (0,))}


def _after(after):
    return ([pl.BlockSpec(memory_space=pl.ANY)], [after]) if after is not None else ([], [])


def _mm(name, a, b, mode, M, N, K, bm, bn, bk, out_dtype, a_spec=None, b_spec=None, o_spec=None, out_shape=None,
        after=None):
    nk = K // bk
    assert M % bm == 0 and N % bn == 0 and nk * bk == K
    after_specs, after_args = _after(after)

    own_acc = nk > 1 and out_dtype != F32

    def body(a_ref, b_ref, *rest):
        o_ref, scratch = rest[len(after_args)], rest[len(after_args) + 1:]
        part = lax.dot_general(a_ref[...].astype(BF16), b_ref[...].astype(BF16), (_DIMS[mode], ((), ())),
                               preferred_element_type=F32)
        if nk == 1:
            o_ref[...] = part.astype(o_ref.dtype)
            return
        k = pl.program_id(2)
        acc_ref = scratch[0] if own_acc else o_ref

        @pl.when(k == 0)
        def _():
            acc_ref[...] = part

        @pl.when(k > 0)
        def _():
            acc_ref[...] += part

        if own_acc:
            @pl.when(k == nk - 1)
            def _():
                o_ref[...] = acc_ref[...].astype(o_ref.dtype)

    if a_spec is None:
        a_spec = (pl.BlockSpec((bk, bm), lambda i, j, k: (k, i)) if mode == "tn"
                  else pl.BlockSpec((bm, bk), lambda i, j, k: (i, k)))
    if b_spec is None:
        b_spec = (pl.BlockSpec((bn, bk), lambda i, j, k: (j, k)) if mode == "nt"
                  else pl.BlockSpec((bk, bn), lambda i, j, k: (k, j)))
    if o_spec is None:
        o_spec = pl.BlockSpec((bm, bn), lambda i, j, k: (i, j))
    if out_shape is None:
        out_shape = (M, N)
    return pl.pallas_call(
        body, name=name, grid=(M // bm, N // bn, nk), in_specs=[a_spec, b_spec] + after_specs, out_specs=o_spec,
        out_shape=jax.ShapeDtypeStruct(out_shape, out_dtype),
        scratch_shapes=[pltpu.VMEM((bm, bn), F32)] if own_acc else [],
        compiler_params=_params("parallel", "parallel", "arbitrary"),
    )(a, b, *after_args)


SCAN_LANES = 512
SCAN_CHUNKS = 8


def _to_chunked(a):
    return a.reshape(SCAN_CHUNKS, SEQ // SCAN_CHUNKS, -1).transpose(1, 0, 2).reshape(SEQ, -1)


def _from_chunked(a):
    return a.reshape(SEQ // SCAN_CHUNKS, SCAN_CHUNKS, -1).transpose(1, 0, 2).reshape(SEQ, -1)


def _scan_block(dr_ref, di_ref, sr_ref, si_ref, lam_r, lam_i, reverse):
    T = SEQ // SCAN_CHUNKS
    bl = lam_r.shape[1]
    assert T == 2 ** 8
    ar = jnp.broadcast_to(lam_r, (SCAN_CHUNKS, bl))
    ai = jnp.broadcast_to(lam_i, (SCAN_CHUNKS, bl))
    zero = jnp.zeros((SCAN_CHUNKS, bl), F32)

    def tile(j):
        return pl.ds(pl.multiple_of(j * SCAN_CHUNKS, SCAN_CHUNKS), SCAN_CHUNKS)

    def step(jj, carry):
        sr, si = carry
        j = T - 1 - jj if reverse else jj
        nr = ar * sr - ai * si + dr_ref[tile(j), :]
        ni = ar * si + ai * sr + di_ref[tile(j), :]
        sr_ref[tile(j), :] = nr
        si_ref[tile(j), :] = ni
        return nr, ni

    er, ei = lax.fori_loop(0, T, step, (zero, zero), unroll=8)

    pr, pi = ar[0:1], ai[0:1]
    for _ in range(8):
        pr, pi = pr * pr - pi * pi, 2.0 * pr * pi
    rows = lax.broadcasted_iota(jnp.int32, (SCAN_CHUNKS, bl), 0)
    cr, ci = zero, zero
    xr = jnp.zeros((1, bl), F32)
    xi = jnp.zeros((1, bl), F32)
    order = range(SCAN_CHUNKS - 2, -1, -1) if reverse else range(1, SCAN_CHUNKS)
    for c in order:
        src = c + 1 if reverse else c - 1
        nxr = pr * xr - pi * xi + er[src:src + 1]
        nxi = pr * xi + pi * xr + ei[src:src + 1]
        xr, xi = nxr, nxi
        cr = jnp.where(rows == c, xr, cr)
        ci = jnp.where(rows == c, xi, ci)

    def fix(jj, pw):
        pwr, pwi = pw
        j = T - 1 - jj if reverse else jj
        sr_ref[tile(j), :] = sr_ref[tile(j), :] + (pwr * cr - pwi * ci)
        si_ref[tile(j), :] = si_ref[tile(j), :] + (pwr * ci + pwi * cr)
        return pwr * ar - pwi * ai, pwr * ai + pwi * ar

    lax.fori_loop(0, T, fix, (ar, ai), unroll=8)


def _s5_forward(name, proj, wdt, cmt, lam_r, lam_i):
    bl = SCAN_LANES
    nblk = NS // bl
    cw = bl * GCH // NST
    assert cw == 128 and CB_U == 0

    def nt(a, b_):
        return lax.dot_general(a, b_, (((1,), (1,)), ((), ())), preferred_element_type=F32)

    def body(u_ref, wr_ref, wi_ref, cr_ref, ci_ref, ar_ref, ai_ref, s_ref, y_ref):
        sr_ref, si_ref = s_ref.at[0], s_ref.at[1]
        u = u_ref[...].astype(BF16)
        sr_ref[...] = nt(u, wr_ref[...])
        si_ref[...] = nt(u, wi_ref[...])
        _scan_block(sr_ref, si_ref, sr_ref, si_ref, ar_ref[...], ai_ref[...], False)
        y_ref[...] = nt(sr_ref[...].astype(BF16), cr_ref[...]) + nt(si_ref[...].astype(BF16), ci_ref[...])

    return pl.pallas_call(
        body, name=name, grid=(nblk,),
        in_specs=[pl.BlockSpec((SEQ, cw), lambda i: (0, i)),
                  pl.BlockSpec((bl, cw), lambda i: (i, i)), pl.BlockSpec((bl, cw), lambda i: (nblk + i, i)),
                  pl.BlockSpec((cw, bl), lambda i: (i, i)), pl.BlockSpec((cw, bl), lambda i: (SW // cw + i, i)),
                  pl.BlockSpec((1, bl), lambda i: (0, i)), pl.BlockSpec((1, bl), lambda i: (0, i))],
        out_specs=[pl.BlockSpec((2, SEQ, bl), lambda i: (0, 0, i)), pl.BlockSpec((SEQ, cw), lambda i: (0, i))],
        out_shape=[jax.ShapeDtypeStruct((2, SEQ, NS), F32), jax.ShapeDtypeStruct((SEQ, SW), F32)],
        compiler_params=_params("arbitrary"),
    )(proj, wdt, wdt, cmt, cmt, lam_r, lam_i)


S5_BWD_LANES = 512


def _s5_backward(name, dy0, proj, s, wdt, cmt, lam_r, lam_i):
    bl = S5_BWD_LANES
    nblk = NS // bl
    cw = 128
    per = cw // (bl * GCH // NST)
    assert per >= 1 and CB_U == 0

    def tn(a, b_):
        return lax.dot_general(a, b_, (((0,), (0,)), ((), ())), preferred_element_type=F32)

    def prev(s_ref):
        last = pltpu.roll(s_ref[SEQ - SCAN_CHUNKS:SEQ, :], 1, 0)
        first = jnp.where(lax.broadcasted_iota(jnp.int32, (SCAN_CHUNKS, bl), 0) > 0, last, 0.0)
        return jnp.concatenate([first, s_ref[0:SEQ - SCAN_CHUNKS, :]], axis=0)

    def body(dy_ref, u_ref, s_ref, cr_ref, ci_ref, wr_ref, wi_ref, lr_ref, li_ref,
             dlr_ref, dli_ref, dwdt_ref, dcmt_ref, du_ref, ar_ref, ai_ref):
        dy = dy_ref[...]
        ar_ref[...] = jnp.dot(dy, cr_ref[...], preferred_element_type=F32)
        ai_ref[...] = jnp.dot(dy, ci_ref[...], preferred_element_type=F32)
        _scan_block(ar_ref, ai_ref, ar_ref, ai_ref, lr_ref[...], -li_ref[...], True)
        a_r, a_i = ar_ref[...], ai_ref[...]
        sr_ref, si_ref = s_ref.at[0], s_ref.at[1]
        spr, spi = prev(sr_ref), prev(si_ref)
        dlr_ref[...] = jnp.sum(a_r * spr + a_i * spi, axis=0, keepdims=True)
        dli_ref[...] = jnp.sum(a_i * spr - a_r * spi, axis=0, keepdims=True)
        a_rb, a_ib = a_r.astype(BF16), a_i.astype(BF16)
        u = u_ref[...].astype(BF16)
        dwdt_ref[0] = tn(a_rb, u)
        dwdt_ref[1] = tn(a_ib, u)
        dcmt_ref[0] = tn(dy, sr_ref[...].astype(BF16))
        dcmt_ref[1] = tn(dy, si_ref[...].astype(BF16))
        part = (jnp.dot(a_rb, wr_ref[...], preferred_element_type=F32)
                + jnp.dot(a_ib, wi_ref[...], preferred_element_type=F32))

        @pl.when(pl.program_id(0) % per == 0)
        def _():
            du_ref[...] = part

        @pl.when(pl.program_id(0) % per > 0)
        def _():
            du_ref[...] += part

    lam_spec = pl.BlockSpec((1, bl), lambda i: (0, i))
    return pl.pallas_call(
        body, name=name, grid=(nblk,),
        in_specs=[pl.BlockSpec((SEQ, cw), lambda i: (0, i // per)), pl.BlockSpec((SEQ, cw), lambda i: (0, i // per)),
                  pl.BlockSpec((2, SEQ, bl), lambda i: (0, 0, i)),
                  pl.BlockSpec((cw, bl), lambda i: (i // per, i)),
                  pl.BlockSpec((cw, bl), lambda i: (SW // cw + i // per, i)),
                  pl.BlockSpec((bl, cw), lambda i: (i, i // per)), pl.BlockSpec((bl, cw), lambda i: (nblk + i, i // per)),
                  lam_spec, lam_spec],
        out_specs=[lam_spec, lam_spec, pl.BlockSpec((2, bl, cw), lambda i: (0, i, i // per)),
                   pl.BlockSpec((2, cw, bl), lambda i: (0, i // per, i)),
                   pl.BlockSpec((SEQ, cw), lambda i: (0, i // per))],
        out_shape=[jax.ShapeDtypeStruct((1, NS), F32), jax.ShapeDtypeStruct((1, NS), F32),
                   jax.ShapeDtypeStruct((2, NS, SW), F32), jax.ShapeDtypeStruct((2, SW, NS), F32),
                   jax.ShapeDtypeStruct((SEQ, SW), F32)],
        scratch_shapes=[pltpu.VMEM((SEQ, bl), F32)] * 2,
        compiler_params=_params("arbitrary"),
    )(dy0, proj, s, cmt, cmt, wdt, wdt, lam_r, lam_i)


def _scores(qb, kb, prev):
    s = lax.dot_general(qb, kb, (((1,), (1,)), ((), ())), preferred_element_type=F32) * (HD ** -0.5)
    row = lax.broadcasted_iota(jnp.int32, (ABLK, ABLK), 0)
    col = lax.broadcasted_iota(jnp.int32, (ABLK, ABLK), 1)
    return jnp.where((col >= row) if prev else (col <= row), s, -1e30)


def _block_rows(dil, r, b):
    if dil == 1:
        return pl.ds(pl.multiple_of(b * ABLK, ABLK), ABLK)
    return pl.ds(r + dil * ABLK * b, ABLK, stride=dil)


def _group_blocks(dil):
    nb = SEQ // dil // ABLK
    shift = nb.bit_length() - 1
    return nb, (lambda idx: (idx >> shift, idx & (nb - 1)))


def _qkv_specs(j_of):
    return [pl.BlockSpec((SEQ, HD), functools.partial(lambda j, c: (0, c + j_of(j)), c=(cb + g) * 4))
            for g in range(3) for cb in (CB_Q, CB_K, CB_V)]


def _attention_fwd(name, proj):
    def body(*refs):
        qkv, z_ref = refs[:9], refs[9]
        y_ref, ya_ref, l_ref = refs[10:13]
        accs, maxs, dens = refs[13:16], refs[16:19], refs[19:22]
        for g, dil in enumerate(DILATIONS):
            q_ref, k_ref, v_ref = qkv[3 * g:3 * g + 3]
            nb, where = _group_blocks(dil)

            def step(t, c, g=g, dil=dil, nb=nb, where=where, q_ref=q_ref, k_ref=k_ref, v_ref=v_ref):
                two = range(ATTN_PAIR_FWD)
                rb = [where(t + i * (SEQ // ABLK // ATTN_PAIR_FWD)) for i in two]
                rows = [_block_rows(dil, r, b) for r, b in rb]
                qb = [q_ref[rows[i], :].astype(BF16) for i in two]
                s_c = [_scores(qb[i], k_ref[rows[i], :].astype(BF16), False) for i in two]
                if nb > 1:
                    prev = [_block_rows(dil, r, jnp.maximum(b - 1, 0)) for r, b in rb]
                    s_p = [jnp.where(rb[i][1] > 0, _scores(qb[i], k_ref[prev[i], :].astype(BF16), True), -1e30)
                           for i in two]
                m = [jnp.max(s_c[i], axis=-1, keepdims=True) for i in two]
                if nb > 1:
                    m = [jnp.maximum(m[i], jnp.max(s_p[i], axis=-1, keepdims=True)) for i in two]
                p_c = [jnp.exp(s_c[i] - m[i]) for i in two]
                den = [jnp.sum(p_c[i], axis=-1, keepdims=True) for i in two]
                acc = [jnp.dot(p_c[i].astype(BF16), v_ref[rows[i], :].astype(BF16), preferred_element_type=F32)
                       for i in two]
                if nb > 1:
                    p_p = [jnp.exp(s_p[i] - m[i]) for i in two]
                    den = [den[i] + jnp.sum(p_p[i], axis=-1, keepdims=True) for i in two]
                    acc = [acc[i] + jnp.dot(p_p[i].astype(BF16), v_ref[prev[i], :].astype(BF16),
                                            preferred_element_type=F32) for i in two]
                for i in two:
                    accs[g][rows[i], :] = acc[i]
                    maxs[g][rows[i], :] = jnp.broadcast_to(m[i], (ABLK, HD))
                    dens[g][rows[i], :] = jnp.broadcast_to(den[i], (ABLK, HD))
                return c

            lax.fori_loop(0, SEQ // ABLK // ATTN_PAIR_FWD, step, 0)
        top = jnp.maximum(jnp.maximum(maxs[0][...], maxs[1][...]), maxs[2][...])
        den = jnp.zeros((SEQ, HD), F32)
        y = jnp.zeros((SEQ, HD), F32)
        for g in range(3):
            wgt = jnp.exp(maxs[g][...] - top)
            den = den + wgt * dens[g][...]
            y = y + wgt * accs[g][...]
        y = y / den
        y_ref[...] = y
        ya_ref[...] = (y * _silu(z_ref[...])).astype(ya_ref.dtype)
        l_ref[...] = top + jnp.log(den)

    ospec = pl.BlockSpec((SEQ, HD), lambda j: (0, j))
    return pl.pallas_call(
        body, name=name, grid=(AW // HD,),
        in_specs=_qkv_specs(lambda j: j) + [pl.BlockSpec((SEQ, HD), lambda j: (0, CB_ZA * 4 + j))],
        out_specs=[ospec, ospec, ospec],
        out_shape=[jax.ShapeDtypeStruct((SEQ, AW), F32), jax.ShapeDtypeStruct((SEQ, AW), BF16),
                   jax.ShapeDtypeStruct((SEQ, AW), F32)],
        scratch_shapes=[pltpu.VMEM((SEQ, HD), F32)] * 9,
        compiler_params=_params("parallel"),
    )(*([proj] * 10))


def _attention_bwd(name, proj, dya, y, lse):
    def tn(a, b_):
        return lax.dot_general(a, b_, (((0,), (0,)), ((), ())), preferred_element_type=F32)

    def nt(a, b_):
        return lax.dot_general(a, b_, (((1,), (1,)), ((), ())), preferred_element_type=F32)

    def body(*refs):
        qkv, z_ref, dya_ref, y_ref, l_ref = refs[:9], refs[9], refs[10], refs[11], refs[12]
        outs, dza_ref = refs[13:22], refs[22]
        dy_s, dsum_s, dq_s, dk_own, dv_own, dk_prev, dv_prev = refs[23:]
        _, vjp = jax.vjp(lambda y_, z_: y_ * _silu(z_), y_ref[...], z_ref[...])
        dy, dz = vjp(dya_ref[...])
        dza_ref[...] = dz.astype(dza_ref.dtype)
        dy_s[...] = dy
        dsum_s[...] = jnp.broadcast_to(jnp.sum(dy * y_ref[...], axis=-1, keepdims=True), (SEQ, HD))
        for g, dil in enumerate(DILATIONS):
            q_ref, k_ref, v_ref = qkv[3 * g:3 * g + 3]
            nb, where = _group_blocks(dil)
            if nb > 1:
                dk_prev[...] = jnp.zeros(dk_prev.shape, F32)
                dv_prev[...] = jnp.zeros(dv_prev.shape, F32)

            def step(t, c, dil=dil, nb=nb, where=where, q_ref=q_ref, k_ref=k_ref, v_ref=v_ref):
                rb = [where(t + i * (SEQ // ABLK // ATTN_PAIR)) for i in range(ATTN_PAIR)]
                sides = []
                for r, b in rb:
                    rows = _block_rows(dil, r, b)
                    own = dict(b=b, qrows=rows, krows=rows, prev=False, dk=dk_own, dv=dv_own,
                               q=q_ref[rows, :].astype(BF16), dy=dy_s[rows, :].astype(BF16))
                    sides.append(own)
                    if nb > 1:
                        sides.append(dict(own, krows=_block_rows(dil, r, jnp.maximum(b - 1, 0)), prev=True,
                                          dk=dk_prev, dv=dv_prev))
                for s_ in sides:
                    s_["k"] = k_ref[s_["krows"], :].astype(BF16)
                    s_["v"] = v_ref[s_["krows"], :].astype(BF16)
                for s_ in sides:
                    sc = _scores(s_["q"], s_["k"], s_["prev"])
                    s_["s"] = jnp.where(s_["b"] > 0, sc, -1e30) if s_["prev"] else sc
                    s_["dp"] = nt(s_["dy"], s_["v"])
                for s_ in sides:
                    p = jnp.exp(s_["s"] - l_ref[s_["qrows"], :])
                    s_["p"] = p.astype(BF16)
                    s_["ds"] = (p * (s_["dp"] - dsum_s[s_["qrows"], :]) * (HD ** -0.5)).astype(BF16)
                for s_ in sides:
                    s_["dk"][s_["krows"], :] = tn(s_["ds"], s_["q"])
                    s_["dv"][s_["krows"], :] = tn(s_["p"], s_["dy"])
                    s_["dq"] = jnp.dot(s_["ds"], s_["k"], preferred_element_type=F32)
                per = len(sides) // ATTN_PAIR
                for i in range(ATTN_PAIR):
                    dq = sides[i * per]["dq"]
                    if per > 1:
                        dq = dq + sides[i * per + 1]["dq"]
                    dq_s[sides[i * per]["qrows"], :] = dq
                return c

            lax.fori_loop(0, SEQ // ABLK // ATTN_PAIR, step, 0)
            dq_ref, dk_ref, dv_ref = outs[3 * g:3 * g + 3]
            dq_ref[...] = dq_s[...].astype(dq_ref.dtype)
            if nb > 1:
                dk_ref[...] = (dk_own[...] + dk_prev[...]).astype(dk_ref.dtype)
                dv_ref[...] = (dv_own[...] + dv_prev[...]).astype(dv_ref.dtype)
            else:
                dk_ref[...] = dk_own[...].astype(dk_ref.dtype)
                dv_ref[...] = dv_own[...].astype(dv_ref.dtype)

    ospec = pl.BlockSpec((SEQ, HD), lambda j: (0, j))
    outs = pl.pallas_call(
        body, name=name, grid=(AW // HD,),
        in_specs=_qkv_specs(lambda j: j) + [pl.BlockSpec((SEQ, HD), lambda j: (0, CB_ZA * 4 + j))] + [ospec] * 3,
        out_specs=[ospec] * 10, out_shape=[jax.ShapeDtypeStruct((SEQ, AW), BF16)] * 10,
        scratch_shapes=[pltpu.VMEM((SEQ, HD), F32)] * 7,
        compiler_params=_params("parallel"),
    )(*([proj] * 10), dya, y, lse)
    return outs[:9], outs[9]


def _rms(x, g):
    return x * lax.rsqrt(jnp.mean(x * x, axis=-1, keepdims=True) + RMS_EPS) * g


def _sig(x):
    return 1.0 / (1.0 + jnp.exp(-x))


def _silu(x):
    return x * _sig(x)


def _gelu(x):
    return 0.5 * x * (1.0 + jnp.tanh(math.sqrt(2.0 / math.pi) * (x + 0.044715 * (x * x * x))))


def _y1_fn(y0p, u, dskip):
    return _gelu(y0p + dskip * u)


def _ys_fn(y1, t, z, bglu):
    return y1 * _sig(t + bglu) * _silu(z)


def _merge_fn(ms, ma, gs, ga):
    return _sig(gs) * ms.astype(F32) + _sig(ga) * ma.astype(F32)


def _colsum(v):
    return jnp.sum(v, axis=0, keepdims=True)


def _lam_fn(lre, lim, ldt):
    a = jnp.minimum(lre, -1e-4)
    dt = jnp.exp(ldt)
    mag = jnp.exp(a * dt)
    ar = mag * jnp.cos(lim * dt)
    ai = mag * jnp.sin(lim * dt)
    den = a * a + lim * lim
    cr = ((ar - 1.0) * a + ai * lim) / den
    ci = (ai * a - (ar - 1.0) * lim) / den
    return ar, ai, cr, ci


def _bbar_fn(cr, ci, bre, bim):
    return cr * bre - ci * bim, cr * bim + ci * bre


def _same_group(rows, a, cols, b):
    r = lax.broadcasted_iota(jnp.int32, (rows, cols), 0) >> (a.bit_length() - 1)
    c = lax.broadcasted_iota(jnp.int32, (rows, cols), 1) >> (b.bit_length() - 1)
    return r == c


def _expand(name, blocks, signs, a, b, dtype, after=None):
    rows, cols = NGRP * a, NGRP * b
    n = len(blocks)
    assert a & (a - 1) == 0 and b & (b - 1) == 0
    after_specs, after_args = _after(after)

    def body(*refs):
        o_ref = refs[-1]
        tile = (lax.broadcasted_iota(jnp.int32, (b, cols), 1) & (b - 1)
                == lax.broadcasted_iota(jnp.int32, (b, cols), 0)).astype(F32)
        keep = _same_group(rows, a, cols, b)
        for i, (ref, sign) in enumerate(zip(refs[:n], signs)):
            spread = jnp.dot(ref[...], tile, preferred_element_type=F32, precision=lax.Precision.HIGHEST)
            o_ref[i * rows:(i + 1) * rows, :] = jnp.where(keep, sign * spread, 0.0).astype(o_ref.dtype)

    return pl.pallas_call(body, name=name, grid=(1,),
                          in_specs=[pl.BlockSpec((rows, b), lambda i: (0, 0))] * n + after_specs,
                          out_specs=pl.BlockSpec((n * rows, cols), lambda i: (0, 0)),
                          out_shape=jax.ShapeDtypeStruct((n * rows, cols), dtype),
                          compiler_params=_params("arbitrary"))(*blocks, *after_args)


def _extract(name, m, a, b, ats, after=None):
    rows, cols = NGRP * a, NGRP * b
    n = len(ats)
    assert a & (a - 1) == 0 and b & (b - 1) == 0
    after_specs, after_args = _after(after)

    def body(*refs):
        tile = (lax.broadcasted_iota(jnp.int32, (cols, b), 0) & (b - 1)
                == lax.broadcasted_iota(jnp.int32, (cols, b), 1)).astype(F32)
        keep = _same_group(rows, a, cols, b)
        for m_ref, o_ref in zip(refs[:n], refs[-n:]):
            kept = jnp.where(keep, m_ref[...], 0.0)
            o_ref[...] = jnp.dot(kept, tile, preferred_element_type=F32, precision=lax.Precision.HIGHEST)

    return pl.pallas_call(
        body, name=name, grid=(1,),
        in_specs=[pl.BlockSpec((rows, cols), functools.partial(lambda i, at: at, at=at)) for at in ats] + after_specs,
        out_specs=[pl.BlockSpec((rows, b), lambda i: (0, 0))] * n,
        out_shape=[jax.ShapeDtypeStruct((rows, b), F32)] * n,
        compiler_params=_params("arbitrary"))(*([m] * n), *after_args)


def _s5_prepare(l, sp, after):
    tag = f"l{l}_"
    ar, ai, cr, ci = _ew(tag + "lam", _lam_fn, NGRP, NGRP,
                         [_ri(sp["lambda_re"]), _ri(sp["lambda_im"]),
                          _ri(sp["log_dt"].reshape(NGRP, 1) + after[0, 0])], [], [(NST, F32)] * 4)
    bre = sp["b_re"].reshape(NS, GCH)
    bim = sp["b_im"].reshape(NS, GCH)
    bbr, bbi = _ew(tag + "bbar", _bbar_fn, NS, NS, [_ri(cr.reshape(NS, 1)), _ri(ci.reshape(NS, 1)), _ri(bre), _ri(bim)],
                   [], [(GCH, F32)] * 2)
    wdt = _expand(tag + "wdt", [bbr, bbi], [1.0, 1.0], NST, GCH, BF16)
    cmt = _expand(tag + "cmt", [sp["c_re"].reshape(SW, NST), sp["c_im"].reshape(SW, NST)], [1.0, -1.0], GCH, NST, BF16,
                  after=after)
    return dict(ar=ar, ai=ai, cr=cr, ci=ci, wdt=wdt, cmt=cmt)


def _layer_head(l, x, sp):
    g1 = sp["pre_norm_g"].reshape(1, DM)
    (h,) = _ew(f"l{l}_rms1", lambda x_, g: (_rms(x_, g),), SEQ, EW_ROWS, [_ri(x)], [g1], [(DM, BF16)])
    return jnp.stack([h, _to_chunked(h)])


def _layer_fwd(l, x, hv, w, sp, prep, after):
    tag = f"l{l}_"
    win = w["w_in"]
    proj = _mm(tag + "proj", hv, win, "nn", SEQ, NCOL, DM, SEQ, 512, 1024, F32,
               a_spec=pl.BlockSpec((None, SEQ, 1024), lambda i, j, k: (_row_order(j), 0, 0)),
               b_spec=pl.BlockSpec((None, 1024, 512), lambda i, j, k: (j // 2, 0, j % 2)), after=after)

    ar, ai, cr, ci, wdt, cmt = (prep[k] for k in ("ar", "ai", "cr", "ci", "wdt", "cmt"))
    s, y0p = _s5_forward(tag + "s5", proj, wdt, cmt, ar.reshape(1, NS), ai.reshape(1, NS))
    dskip = sp["d_skip"].reshape(1, SW)
    (y1,) = _ew(tag + "y1", lambda a, u, d: (_y1_fn(a, u, d),), SEQ, EW_ROWS,
                [_ri(y0p), _ri(proj, SW, CB_U)], [dskip], [(SW, F32)])
    t = _mm(tag + "glu", y1, w["w_glu"], "nn", SEQ, SW, SW, 1024, 512, 512, F32)
    bglu = sp["b_glu"].reshape(1, SW)
    (ys_c,) = _ew(tag + "ys", lambda y1_, t_, z, b_: (_ys_fn(y1_, t_, z, b_),), SEQ, EW_ROWS,
                  [_ri(y1), _ri(t), _ri(proj, SW, CB_ZS)], [bglu], [(SW, BF16)])
    ys = _from_chunked(ys_c)

    ypre, ya, lse = _attention_fwd(tag + "attn", proj)

    ms = _mm(tag + "branch_s", ys, w["w_branch_s"], "nn", SEQ, DM, SW, 1024, 1024, 512, BF16)
    ma = _mm(tag + "branch_a", ya, w["w_branch_a"], "nn", SEQ, DM, AW, 1024, 1024, 512, BF16)
    (merged,) = _ew(tag + "merge", lambda a, b_, c, d: (_merge_fn(a, b_, c, d),), SEQ, EW_ROWS,
                    [_ri(ms), _ri(ma), _ri(proj, DM, CB_GS), _ri(proj, DM, CB_GA)], [], [(DM, BF16)])
    out = _mm(tag + "out", merged, w["w_out"], "nn", SEQ, DM, DM, 1024, 1024, 1024, F32)
    g2 = sp["post_norm_g"].reshape(1, DM)
    (x_new,) = _ew(tag + "post", lambda x_, o, g: (x_ + _rms(o, g),), SEQ, EW_ROWS, [_ri(x), _ri(out)], [g2], [(DM, F32)])
    res = dict(x=x, hv=hv, proj=proj, ar=ar, ai=ai, cr=cr, ci=ci, wdt=wdt, cmt=cmt, s=s, y0p=y0p,
               y1=y1, t=t, ys=ys, ya=ya, ypre=ypre, lse=lse, ms=ms, ma=ma, merged=merged, out=out)
    return x_new, res


def _layer_bwd(l, dxn, r, w, sp, big_done, after=None):
    tag = f"l{l}b_"
    proj = r["proj"]
    g1 = sp["pre_norm_g"].reshape(1, DM)
    g2 = sp["post_norm_g"].reshape(1, DM)
    dskip = sp["d_skip"].reshape(1, SW)
    bglu = sp["b_glu"].reshape(1, SW)

    def post_b(d, o, g):
        _, vjp = jax.vjp(_rms, o, g)
        do, dg = vjp(d)
        return do, dg

    d_out, dg2 = _ew(tag + "post", post_b, SEQ, EW_ROWS, [_ri(dxn), _ri(r["out"])], [g2], [(DM, BF16)], [DM])
    dw_out = _mm(tag + "dw_out", r["merged"], d_out, "tn", DM, DM, SEQ, 1024, 1024, SEQ, BF16, after=after)
    dmerged = _mm(tag + "dmerged", d_out, w["w_out"], "nt", SEQ, DM, DM, 1024, 1024, 1024, F32, after=after)

    def merge_b(d, ms, ma, gs, ga):
        _, vjp = jax.vjp(_merge_fn, ms, ma, gs, ga)
        return vjp(d)

    dms, dma, dgs, dga = _ew(tag + "merge", merge_b, SEQ, EW_ROWS,
                             [_ri(dmerged), _ri(r["ms"]), _ri(r["ma"]), _ri(proj, DM, CB_GS), _ri(proj, DM, CB_GA)],
                             [], [(DM, BF16)] * 4)
    dw_bs = _mm(tag + "dw_bs", r["ys"], dms, "tn", SW, DM, SEQ, 512, 1024, SEQ, BF16)
    dw_ba = _mm(tag + "dw_ba", r["ya"], dma, "tn", AW, DM, SEQ, 512, 1024, SEQ, BF16)
    dys = _mm(tag + "dys", dms, w["w_branch_s"], "nt", SEQ, SW, DM, 1024, 512, 1024, F32)
    dya = _mm(tag + "dya", dma, w["w_branch_a"], "nt", SEQ, AW, DM, 1024, 512, 1024, F32)

    dqkv, dza = _attention_bwd(tag + "attn", proj, dya, r["ypre"], r["lse"])

    def ys_b(d, y1, t, z, b_):
        _, vjp = jax.vjp(_ys_fn, y1, t, z, b_)
        dy1, dt, dz, _ = vjp(d)
        return dy1, dt, dz, _colsum(dt)

    dy1a, dt, dzs, dbglu = _ew(tag + "ys", ys_b, SEQ, EW_ROWS,
                               [_ri(_to_chunked(dys)), _ri(r["y1"]), _ri(r["t"]), _ri(proj, SW, CB_ZS)],
                               [bglu], [(SW, F32), (SW, BF16), (SW, BF16)], [SW])
    dw_glu = _mm(tag + "dw_glu", r["y1"], dt, "tn", SW, SW, SEQ, 512, 512, SEQ, BF16)
    dy1b = _mm(tag + "dy1b", dt, w["w_glu"], "nt", SEQ, SW, SW, 1024, 512, 512, F32)

    def y1_b(da, db, y0p, u, d_):
        _, vjp = jax.vjp(_y1_fn, y0p, u, d_)
        dy0, du, dd = vjp(da + db)
        return dy0, du, dd

    dy0, du_skip, ddskip = _ew(tag + "y1", y1_b, SEQ, EW_ROWS,
                               [_ri(dy1a), _ri(dy1b), _ri(r["y0p"]), _ri(proj, SW, CB_U)], [dskip],
                               [(SW, BF16), (SW, F32)], [SW])
    dlr, dli, dwdt, dcmt, du_s = _s5_backward(tag + "s5", dy0, proj, r["s"], r["wdt"], r["cmt"],
                                              r["ar"].reshape(1, NS), r["ai"].reshape(1, NS))
    (du,) = _ew(tag + "du", lambda a, c: (a + c,), SEQ, EW_ROWS, [_ri(du_s), _ri(du_skip)], [], [(SW, BF16)])

    dq, dk, dv = ([dqkv[3 * g + i] for g in range(3)] for i in range(3))
    dproj = jnp.concatenate([du, dzs, *dq, *dk, *dv, dza, dgs, dga], axis=1)
    dw_in = _mm(tag + "dw_in", r["hv"], dproj, "tn", DM, NCOL, SEQ, 1024, 512, SEQ, BF16,
                a_spec=pl.BlockSpec((None, SEQ, 1024), lambda i, j, k: (_row_order(j), 0, 0)),
                o_spec=pl.BlockSpec((None, 1024, 512), lambda i, j, k: (j // 2, 0, j % 2)), out_shape=(NDEV, DM, DM))
    tok = big_done(l, dict(w_in=dw_in, w_glu=dw_glu.reshape(NDEV, SW // NDEV, SW),
                           w_branch_s=dw_bs.reshape(SW, NDEV, DM // NDEV).transpose(1, 0, 2),
                           w_branch_a=dw_ba.reshape(AW, NDEV, DM // NDEV).transpose(1, 0, 2),
                           w_out=dw_out.reshape(NDEV, DM // NDEV, DM)))
    if tok is not None:
        g1 = g1 + tok[0, 0]

    dwdt = dwdt.reshape(2 * NS, SW)
    dcmt = dcmt.reshape(2 * SW, NS)
    dbbr, dbbi = _extract(tag + "dbb", dwdt, NST, GCH, ((0, 0), (1, 0)), after=tok)
    bre = sp["b_re"].reshape(NS, GCH)
    bim = sp["b_im"].reshape(NS, GCH)

    def bbar_b(cr, ci, br_, bi_, dr, di):
        _, vjp = jax.vjp(_bbar_fn, cr, ci, br_, bi_)
        return vjp((dr, di))

    dcr, dci, dbre, dbim = _ew(tag + "bbar", bbar_b, NS, NS,
                               [_ri(r["cr"].reshape(NS, 1)), _ri(r["ci"].reshape(NS, 1)), _ri(bre), _ri(bim),
                                _ri(dbbr), _ri(dbbi)], [], [(1, F32), (1, F32), (GCH, F32), (GCH, F32)])

    def lam_b(lre, lim, ldt, dar, dai, dcr_, dci_):
        _, vjp = jax.vjp(_lam_fn, lre, lim, ldt)
        return vjp((dar, dai, dcr_, dci_))

    dlre, dlim, dldt = _ew(tag + "lam", lam_b, NGRP, NGRP,
                           [_ri(sp["lambda_re"]), _ri(sp["lambda_im"]), _ri(sp["log_dt"].reshape(NGRP, 1)),
                            _ri(dlr.reshape(NGRP, NST)), _ri(dli.reshape(NGRP, NST)),
                            _ri(dcr.reshape(NGRP, NST)), _ri(dci.reshape(NGRP, NST))], [],
                           [(NST, F32), (NST, F32), (1, F32)])
    dc_re, dc_im = _extract(tag + "dc", dcmt, GCH, NST, ((0, 0), (1, 0)), after=tok)
    dc_re, dc_im = dc_re.reshape(NGRP, GCH, NST), -dc_im.reshape(NGRP, GCH, NST)

    dh_time = _mm(tag + "dh_time", dproj, w["w_in"], "nt", SEQ, DM, NCOL - DM, SEQ, 1024, 1024, F32,
                  a_spec=pl.BlockSpec((SEQ, 1024), lambda i, j, k: (0, 1 + k)),
                  b_spec=pl.BlockSpec((None, 1024, 1024), lambda i, j, k: (1 + k, 0, 0)), after=tok)
    dh_chunked = _mm(tag + "dh_chunked", dproj, w["w_in"], "nt", SEQ, DM, DM, SEQ, 1024, 1024, F32,
                     a_spec=pl.BlockSpec((SEQ, 1024), lambda i, j, k: (0, 0)),
                     b_spec=pl.BlockSpec((None, 1024, 1024), lambda i, j, k: (0, 0, 0)), after=tok)
    dh = [dh_time, _from_chunked(dh_chunked)]

    def pre_b(d, dh0, dh1, x_, g):
        _, vjp = jax.vjp(_rms, x_, g)
        dx_, dg = vjp(dh0 + dh1)
        return d + dx_, dg

    dx, dg1 = _ew(tag + "pre", pre_b, SEQ, EW_ROWS, [_ri(dxn)] + [_ri(t_) for t_ in dh] + [_ri(r["x"])], [g1],
                  [(DM, F32)], [DM])

    small = dict(pre_norm_g=dg1.reshape(DM), lambda_re=dlre, lambda_im=dlim, log_dt=dldt.reshape(NGRP),
                 b_re=dbre.reshape(NGRP, NST, GCH), b_im=dbim.reshape(NGRP, NST, GCH), c_re=dc_re, c_im=dc_im,
                 d_skip=ddskip.reshape(SW), b_glu=dbglu.reshape(SW), post_norm_g=dg2.reshape(DM))
    return dx, small


_HBM = pl.BlockSpec(memory_space=pltpu.HBM)
_SEM = pl.BlockSpec(memory_space=pltpu.SEMAPHORE)
_EFFECT = pltpu.SideEffectType.DATAFLOW_SIDE_EFFECTING


def _remote_copies(srcs, dsts, send_sems, recv_sems, gather):
    x, y, c = lax.axis_index("x"), lax.axis_index("y"), lax.axis_index("c")
    me = 4 * x + 2 * y + c
    copies = []
    for i in range(len(srcs)):
        for k in range(1, NDEV):
            peer = (x ^ (k >> 2), y ^ ((k >> 1) & 1), c ^ (k & 1))
            src = srcs[i] if gather[i] else srcs[i].at[me ^ k]
            copies.append(pltpu.make_async_remote_copy(
                src_ref=src, dst_ref=dsts[i].at[me], send_sem=send_sems[i], recv_sem=recv_sems[i],
                device_id=peer, device_id_type=pl.DeviceIdType.MESH))
    return copies


def _all_seven(dst, send_sem, recv_sem):
    seven = dst.at[pl.ds(0, NDEV - 1)]
    me = (lax.axis_index("x"), lax.axis_index("y"), lax.axis_index("c"))
    return pltpu.make_async_remote_copy(src_ref=seven, dst_ref=seven, send_sem=send_sem, recv_sem=recv_sem,
                                        device_id=me, device_id_type=pl.DeviceIdType.MESH)


def _own_slabs(name, arrs, gather, after):
    n = len(arrs)
    me = (4 * lax.axis_index("x") + 2 * lax.axis_index("y") + lax.axis_index("c")).astype(jnp.int32).reshape(1)

    def body(me_ref, *refs):
        for src, dst in zip(refs[:n], refs[n + 1:]):
            dst[...] = src[...]

    def zeros(k):
        return (0,) * k

    in_specs, out_specs, out_shape = [], [], []
    for a, g in zip(arrs, gather):
        slab = a.shape if g else a.shape[1:]
        nd = len(slab)
        if g:
            in_specs.append(pl.BlockSpec(slab, functools.partial(lambda i, me_ref, nd: zeros(nd), nd=nd)))
        else:
            in_specs.append(pl.BlockSpec((None,) + slab, functools.partial(lambda i, me_ref, nd: (me_ref[0],) + zeros(nd), nd=nd)))
        out_specs.append(pl.BlockSpec((None,) + slab, functools.partial(lambda i, me_ref, nd: (me_ref[0],) + zeros(nd), nd=nd)))
        out_shape.append(jax.ShapeDtypeStruct((NDEV,) + slab, a.dtype))
    in_specs.append(pl.BlockSpec(memory_space=pl.ANY))
    return pl.pallas_call(
        body, name=name, out_shape=out_shape,
        grid_spec=pltpu.PrefetchScalarGridSpec(num_scalar_prefetch=1, grid=(1,), in_specs=in_specs, out_specs=out_specs),
        compiler_params=_params("arbitrary"),
    )(me, *arrs, after)


def _exchange_start(name, arrs, gather, after):
    n = len(arrs)
    lands = _own_slabs(name + "_own", arrs, gather, after)

    def body(*refs):
        srcs, dsts = refs[:n], refs[n:2 * n]
        send_sems, recv_sems = refs[2 * n:3 * n], refs[3 * n:4 * n]
        token = refs[-1]
        for cp in _remote_copies(srcs, dsts, send_sems, recv_sems, gather):
            cp.start()
        token[...] = jnp.zeros(token.shape, token.dtype)

    thru = [pltpu.HBM(a.shape, a.dtype) for a in list(arrs) + list(lands)]
    outs = pl.pallas_call(
        body, name=name,
        out_shape=(*[pltpu.SemaphoreType.DMA(())] * (2 * n), *thru, jax.ShapeDtypeStruct((8, 128), F32)),
        in_specs=[_HBM] * (2 * n),
        out_specs=(*[_SEM] * (2 * n), *[_HBM] * (2 * n), pl.BlockSpec(memory_space=pltpu.VMEM)),
        input_output_aliases={i: 2 * n + i for i in range(2 * n)},
        compiler_params=pltpu.CompilerParams(has_side_effects=_EFFECT),
    )(*[pltpu.with_memory_space_constraint(a, pltpu.HBM) for a in list(arrs) + list(lands)])
    return dict(send=outs[:n], recv=outs[n:2 * n], srcs=outs[2 * n:3 * n], lands=outs[3 * n:4 * n], token=outs[-1],
                gather=gather)


def _exchange_wait(name, started, after):
    n = len(started["srcs"])
    after = list(after)

    def body(*refs):
        dsts = refs[n:2 * n]
        send_sems, recv_sems = refs[2 * n:3 * n], refs[3 * n:4 * n]
        for i in range(n):
            cp = _all_seven(dsts[i], send_sems[i], recv_sems[i])
            cp.wait_send()
            cp.wait_recv()

    bufs = list(started["srcs"]) + list(started["lands"])
    outs = pl.pallas_call(
        body, name=name, out_shape=tuple(pltpu.HBM(a.shape, a.dtype) for a in bufs),
        in_specs=[_HBM] * (2 * n) + [_SEM] * (2 * n) + [pl.BlockSpec(memory_space=pl.ANY)] * len(after),
        out_specs=(_HBM,) * (2 * n), input_output_aliases={i: i for i in range(2 * n)},
        compiler_params=pltpu.CompilerParams(has_side_effects=_EFFECT),
    )(*bufs, *started["send"], *started["recv"], *after)
    return outs[n:]


def _sum_in_order(parts):
    g = parts[0].astype(F32)
    for p in parts[1:]:
        g = g + p.astype(F32)
    return g


def _adam_update(g, w_, m_, v_):
    m2 = B1 * m_ + (1.0 - B1) * g
    v2 = B2 * v_ + (1.0 - B2) * (g * g)
    m_hat = m2 / (1.0 - B1 ** STEP)
    v_hat = v2 / (1.0 - B2 ** STEP)
    delta = -LR * (m_hat / (jnp.sqrt(v_hat) + ADAM_EPS) + WD * w_)
    return g, delta, m2, v2


def _adamw(name, g, w, m, v, br):
    rows, cols = w.shape
    return _ew(name, _adam_update, rows, br, [_ri(g), _ri(w), _ri(m), _ri(v)], [], [(cols, F32)] * 4)


def _adamw_layer(name, l, parts, w, m, v, br, outs):
    rows, cols = w.shape
    nb = rows // DEPTH // br
    assert nb * br * DEPTH == rows

    def body(*refs):
        vals = _adam_update(_sum_in_order([r[...] for r in refs[:NDEV]]), *[r[...] for r in refs[NDEV:NDEV + 3]])
        for r, val in zip(refs[-4:], vals):
            r[...] = val

    mine = pl.BlockSpec((br, cols), lambda i: (l * nb + i, 0))
    in_specs = [pl.BlockSpec((br, cols), functools.partial(lambda i, d: (d * nb + i, 0), d=d)) for d in range(NDEV)]
    in_specs += [mine] * 3 + [pl.BlockSpec(memory_space=pl.ANY)] * len(outs)
    return pl.pallas_call(
        body, name=f"{name}{l}", grid=(nb,), in_specs=in_specs, out_specs=[mine] * 4,
        out_shape=[jax.ShapeDtypeStruct((rows, cols), F32)] * 4,
        input_output_aliases={NDEV + 3 + q: q for q in range(len(outs))},
        compiler_params=_params("arbitrary"),
    )(*([parts] * NDEV), w, m, v, *outs)


SMALL = ("pre_norm_g", "lambda_re", "lambda_im", "log_dt", "b_re", "b_im", "c_re", "c_im", "d_skip", "b_glu",
         "post_norm_g")
BIG = ("w_in", "w_glu", "w_branch_s", "w_branch_a", "w_out")
WEIGHTS = ("pre_norm_g", "w_in", "lambda_re", "lambda_im", "log_dt", "b_re", "b_im", "c_re", "c_im", "d_skip",
           "w_glu", "b_glu", "w_branch_s", "w_branch_a", "w_out", "post_norm_g")
PACK_COLS = 1024
PACK_BR = 136


def _pack_layer(d):
    pieces = [d[k].astype(F32).reshape(-1) for k in SMALL]
    used = sum(p.shape[0] for p in pieces)
    assert used <= PACK_BR * PACK_COLS
    return jnp.concatenate(pieces + [jnp.zeros((PACK_BR * PACK_COLS - used,), F32)]).reshape(PACK_BR, PACK_COLS)


def _unpack(p, like):
    flat = p.reshape(DEPTH, PACK_BR * PACK_COLS)
    out, off = {}, 0
    for k in SMALL:
        n = like[k].size // DEPTH
        out[k] = flat[:, off:off + n].reshape(like[k].shape)
        off += n
    return out


def _local_step(x, target, small, started, weights_of, big_done, small_done, total_loss):
    preps = [_s5_prepare(l, {k: small[k][l] for k in SMALL}, started) for l in range(DEPTH)]
    res, ws = [], []
    for l in range(DEPTH):
        sp = {k: small[k][l] for k in SMALL}
        hv = _layer_head(l, x, sp)
        w_l, tok = weights_of(l, [hv] + ([p[k] for p in preps for k in ("wdt", "cmt")] if l == 0 else []))
        x, r = _layer_fwd(l, x, hv, w_l, sp, preps[l], tok)
        res.append(r)
        ws.append(w_l)

    def loss_fn(y, t):
        e = y - t
        return e * (1.0 / DM), jnp.sum(_colsum(0.5 * e * e * (1.0 / DM)), axis=1, keepdims=True)

    dx, loss = _ew("loss", loss_fn, SEQ, EW_ROWS, [_ri(x), _ri(target)], [], [(DM, F32)], [1])
    total = total_loss(loss.reshape(()))
    for l in reversed(range(DEPTH)):
        dx, sm = _layer_bwd(l, dx, res[l], ws[l], {k: small[k][l] for k in SMALL}, big_done,
                            after=total.reshape(1, 1) if l == DEPTH - 1 else None)
        small_done(l, sm)
    return total, dx


def _full_weights(gathered):
    g = gathered
    return dict(
        w_in=g["w_in"],
        w_glu=g["w_glu"].reshape(SW, SW),
        w_branch_s=g["w_branch_s"].transpose(1, 0, 2).reshape(SW, DM),
        w_branch_a=g["w_branch_a"].transpose(1, 0, 2).reshape(AW, DM),
        w_out=g["w_out"].reshape(DM, DM),
    )


def kernel(x, pre_norm_g, w_in, lambda_re, lambda_im, log_dt, b_re, b_im, c_re, c_im, d_skip, w_glu, b_glu, w_branch_s, w_branch_a, w_out, post_norm_g, loss_target, m_pre_norm_g, m_w_in, m_lambda_re, m_lambda_im, m_log_dt, m_b_re, m_b_im, m_c_re, m_c_im, m_d_skip, m_w_glu, m_b_glu, m_w_branch_s, m_w_branch_a, m_w_out, m_post_norm_g, v_pre_norm_g, v_w_in, v_lambda_re, v_lambda_im, v_log_dt, v_b_re, v_b_im, v_c_re, v_c_im, v_d_skip, v_w_glu, v_b_glu, v_w_branch_s, v_w_branch_a, v_w_out, v_post_norm_g):
    wts = dict(pre_norm_g=pre_norm_g, w_in=w_in, lambda_re=lambda_re, lambda_im=lambda_im, log_dt=log_dt, b_re=b_re,
               b_im=b_im, c_re=c_re, c_im=c_im, d_skip=d_skip, w_glu=w_glu, b_glu=b_glu, w_branch_s=w_branch_s,
               w_branch_a=w_branch_a, w_out=w_out, post_norm_g=post_norm_g)
    mom = dict(pre_norm_g=m_pre_norm_g, w_in=m_w_in, lambda_re=m_lambda_re, lambda_im=m_lambda_im, log_dt=m_log_dt,
               b_re=m_b_re, b_im=m_b_im, c_re=m_c_re, c_im=m_c_im, d_skip=m_d_skip, w_glu=m_w_glu, b_glu=m_b_glu,
               w_branch_s=m_w_branch_s, w_branch_a=m_w_branch_a, w_out=m_w_out, post_norm_g=m_post_norm_g)
    var = dict(pre_norm_g=v_pre_norm_g, w_in=v_w_in, lambda_re=v_lambda_re, lambda_im=v_lambda_im, log_dt=v_log_dt,
               b_re=v_b_re, b_im=v_b_im, c_re=v_c_re, c_im=v_c_im, d_skip=v_d_skip, w_glu=v_w_glu, b_glu=v_b_glu,
               w_branch_s=v_w_branch_s, w_branch_a=v_w_branch_a, w_out=v_w_out, post_norm_g=v_post_norm_g)

    def gather_start(l, after):
        return _exchange_start(f"gather_start{l}", [wts[k][l].astype(BF16) for k in BIG], [True] * len(BIG), after)

    gathering = {0: gather_start(0, x)}
    sending, packed = {}, {}

    def weights_of(l, after):
        gathered = _exchange_wait(f"gather_wait{l}", gathering[l], after)
        tok = None
        if l + 1 < DEPTH:
            gathering[l + 1] = gather_start(l + 1, gathered[0])
            tok = gathering[l + 1]["token"]
        return _full_weights(dict(zip(BIG, gathered))), tok

    def big_done(l, big):
        arrs, kinds = [big[k] for k in BIG], [False] * len(BIG)
        if l + 1 < DEPTH:
            arrs, kinds = arrs + [packed[l + 1]], kinds + [True]
        sending[l] = _exchange_start(f"grads_start{l}", arrs, kinds, big["w_in"])
        return sending[l]["token"]

    def small_done(l, sm):
        packed[l] = _pack_layer(sm)

    loss, dx = _local_step(x[0], loss_target[0], wts, gathering[0]["token"], weights_of, big_done, small_done,
                           lambda part: lax.psum(part, ("x", "y", "c")))
    last = _exchange_start("grads_start_last", [packed[0]], [True], dx)

    grads, delta, new_m, new_v = {}, {}, {}, {}

    def as_rows(k):
        cols = wts[k].shape[-1]
        rows = wts[k].size // cols
        return rows, cols, [t[k].reshape(rows, cols) for t in (wts, mom, var)]

    recv_l, outs, after = {}, {k: [] for k in BIG}, [dx]
    for l in reversed(range(DEPTH)):
        recv_l[l] = _exchange_wait(f"grads_wait{l}", sending[l], after)
        for i, k in enumerate(BIG):
            rows, cols, wmv = as_rows(k)
            per_layer = rows // DEPTH
            outs[k] = _adamw_layer("adamw_" + k, l, recv_l[l][i].reshape(NDEV * per_layer, cols), *wmv,
                                   min(per_layer, 256), outs[k])
        after = [outs[k][0] for k in BIG]
    for k in BIG:
        grads[k], delta[k], new_m[k], new_v[k] = (o.reshape(wts[k].shape) for o in outs[k])

    def update(k, g):
        rows, cols, wmv = as_rows(k)
        res = _adamw("adamw_" + k, g.reshape(rows, cols), *wmv, min(rows, 1024 if cols <= 128 else 256))
        grads[k], delta[k], new_m[k], new_v[k] = (o.reshape(wts[k].shape) for o in res)

    recv_last = _exchange_wait("grads_wait_last", last, after)
    recv_small = jnp.concatenate([recv_last[0]] + [recv_l[l][len(BIG)] for l in range(DEPTH - 1)], axis=1)
    rows = DEPTH * PACK_BR
    (g_small,) = _ew("grads_small", lambda *p: (_sum_in_order(p),), rows, PACK_BR,
                     [_ri(recv_small.reshape(NDEV * rows, PACK_COLS), PACK_COLS, 0, d * DEPTH) for d in range(NDEV)], [],
                     [(PACK_COLS, F32)])
    for k, g in _unpack(g_small, wts).items():
        update(k, g)

    return (loss, dx[None], *[grads[k] for k in WEIGHTS], *[delta[k] for k in WEIGHTS],
            *[new_m[k] for k in WEIGHTS], *[new_v[k] for k in WEIGHTS])
```

```python
import functools
import math

import jax
import jax.numpy as jnp
from jax import lax
from jax.experimental import pallas as pl
from jax.experimental.pallas import tpu as pltpu

F32 = jnp.float32
BF16 = jnp.bfloat16

NDEV = 8
DEPTH = 4
SEQ = 2048
DM = 1024
NCOL = 8192
SW = 512
NGRP = 32
GCH = 16
NST = 64
NS = NGRP * NST
HD = 128
AW = 512
DILATIONS = (1, 4, 16)
ABLK = 128
ATTN_PAIR = 8
ATTN_PAIR_FWD = 8
RMS_EPS = 1e-6
LR, B1, B2, ADAM_EPS, WD, STEP = 0.001, 0.9, 0.999, 1e-08, 0.01, 10

CB_U, CB_ZS, CB_Q, CB_K, CB_V, CB_ZA = 0, 1, 2, 5, 8, 11
CB_GS, CB_GA = 6, 7

VMEM_LIMIT = 56 * 2 ** 20
EW_ROWS = 512


def _row_order(j):
    return jnp.where(j < CB_Q, 1, 0)


def _params(*sem):
    return pltpu.CompilerParams(dimension_semantics=sem, vmem_limit_bytes=VMEM_LIMIT)


def _ew(name, fn, rows, br, row_ins, bc_ins, row_outs, red_outs=()):
    n_in = len(row_ins) + len(bc_ins)
    n_ro = len(row_outs)
    steps = rows // br
    assert steps * br == rows

    def body(*refs):
        vals = fn(*[r[...] for r in refs[:n_in]])
        outs = refs[n_in:]
        for r, v in zip(outs[:n_ro], vals[:n_ro]):
            r[...] = v.astype(r.dtype)
        if red_outs:
            @pl.when(pl.program_id(0) == 0)
            def _():
                for r in outs[n_ro:]:
                    r[...] = jnp.zeros(r.shape, r.dtype)
            for r, v in zip(outs[n_ro:], vals[n_ro:]):
                r[...] += v

    in_specs = []
    for (_, w, cb, rb) in row_ins:
        in_specs.append(pl.BlockSpec((br, w), functools.partial(lambda i, cb, rb: (rb + i, cb), cb=cb, rb=rb)))
    for a in bc_ins:
        in_specs.append(pl.BlockSpec(a.shape, functools.partial(lambda i, nd: (0,) * nd, nd=a.ndim)))
    out_specs = [pl.BlockSpec((br, w), lambda i: (i, 0)) for (w, _) in row_outs]
    out_specs += [pl.BlockSpec((1, w), lambda i: (0, 0)) for w in red_outs]
    out_shape = [jax.ShapeDtypeStruct((rows, w), dt) for (w, dt) in row_outs]
    out_shape += [jax.ShapeDtypeStruct((1, w), F32) for w in red_outs]
    return pl.pallas_call(
        body, name=name, grid=(steps,), in_specs=in_specs, out_specs=out_specs, out_shape=out_shape,
        compiler_params=_params("arbitrary"),
    )(*[a for (a, _, _, _) in row_ins], *bc_ins)


def _ri(a, w=None, cb=0, rb=0):
    return (a, a.shape[1] if w is None else w, cb, rb)


_DIMS = {"nn": ((1,), (0,)), "nt": ((1,), (1,)), "tn": ((0,), (0,))}


def _after(after):
    return ([pl.BlockSpec(memory_space=pl.ANY)], [after]) if after is not None else ([], [])


def _mm(name, a, b, mode, M, N, K, bm, bn, bk, out_dtype, a_spec=None, b_spec=None, o_spec=None, out_shape=None,
        after=None):
    nk = K // bk
    assert M % bm == 0 and N % bn == 0 and nk * bk == K
    after_specs, after_args = _after(after)

    own_acc = nk > 1 and out_dtype != F32

    def body(a_ref, b_ref, *rest):
        o_ref, scratch = rest[len(after_args)], rest[len(after_args) + 1:]
        part = lax.dot_general(a_ref[...].astype(BF16), b_ref[...].astype(BF16), (_DIMS[mode], ((), ())),
                               preferred_element_type=F32)
        if nk == 1:
            o_ref[...] = part.astype(o_ref.dtype)
            return
        k = pl.program_id(2)
        acc_ref = scratch[0] if own_acc else o_ref

        @pl.when(k == 0)
        def _():
            acc_ref[...] = part

        @pl.when(k > 0)
        def _():
            acc_ref[...] += part

        if own_acc:
            @pl.when(k == nk - 1)
            def _():
                o_ref[...] = acc_ref[...].astype(o_ref.dtype)

    if a_spec is None:
        a_spec = (pl.BlockSpec((bk, bm), lambda i, j, k: (k, i)) if mode == "tn"
                  else pl.BlockSpec((bm, bk), lambda i, j, k: (i, k)))
    if b_spec is None:
        b_spec = (pl.BlockSpec((bn, bk), lambda i, j, k: (j, k)) if mode == "nt"
                  else pl.BlockSpec((bk, bn), lambda i, j, k: (k, j)))
    if o_spec is None:
        o_spec = pl.BlockSpec((bm, bn), lambda i, j, k: (i, j))
    if out_shape is None:
        out_shape = (M, N)
    return pl.pallas_call(
        body, name=name, grid=(M // bm, N // bn, nk), in_specs=[a_spec, b_spec] + after_specs, out_specs=o_spec,
        out_shape=jax.ShapeDtypeStruct(out_shape, out_dtype),
        scratch_shapes=[pltpu.VMEM((bm, bn), F32)] if own_acc else [],
        compiler_params=_params("parallel", "parallel", "arbitrary"),
    )(a, b, *after_args)


SCAN_LANES = 512
SCAN_CHUNKS = 8


def _to_chunked(a):
    return a.reshape(SCAN_CHUNKS, SEQ // SCAN_CHUNKS, -1).transpose(1, 0, 2).reshape(SEQ, -1)


def _from_chunked(a):
    return a.reshape(SEQ // SCAN_CHUNKS, SCAN_CHUNKS, -1).transpose(1, 0, 2).reshape(SEQ, -1)


def _scan_block(dr_ref, di_ref, sr_ref, si_ref, lam_r, lam_i, reverse):
    T = SEQ // SCAN_CHUNKS
    bl = lam_r.shape[1]
    assert T == 2 ** 8
    ar = jnp.broadcast_to(lam_r, (SCAN_CHUNKS, bl))
    ai = jnp.broadcast_to(lam_i, (SCAN_CHUNKS, bl))
    zero = jnp.zeros((SCAN_CHUNKS, bl), F32)

    def tile(j):
        return pl.ds(pl.multiple_of(j * SCAN_CHUNKS, SCAN_CHUNKS), SCAN_CHUNKS)

    def step(jj, carry):
        sr, si = carry
        j = T - 1 - jj if reverse else jj
        nr = ar * sr - ai * si + dr_ref[tile(j), :]
        ni = ar * si + ai * sr + di_ref[tile(j), :]
        sr_ref[tile(j), :] = nr
        si_ref[tile(j), :] = ni
        return nr, ni

    er, ei = lax.fori_loop(0, T, step, (zero, zero), unroll=8)

    pr, pi = ar[0:1], ai[0:1]
    for _ in range(8):
        pr, pi = pr * pr - pi * pi, 2.0 * pr * pi
    rows = lax.broadcasted_iota(jnp.int32, (SCAN_CHUNKS, bl), 0)
    cr, ci = zero, zero
    xr = jnp.zeros((1, bl), F32)
    xi = jnp.zeros((1, bl), F32)
    order = range(SCAN_CHUNKS - 2, -1, -1) if reverse else range(1, SCAN_CHUNKS)
    for c in order:
        src = c + 1 if reverse else c - 1
        nxr = pr * xr - pi * xi + er[src:src + 1]
        nxi = pr * xi + pi * xr + ei[src:src + 1]
        xr, xi = nxr, nxi
        cr = jnp.where(rows == c, xr, cr)
        ci = jnp.where(rows == c, xi, ci)

    def fix(jj, pw):
        pwr, pwi = pw
        j = T - 1 - jj if reverse else jj
        sr_ref[tile(j), :] = sr_ref[tile(j), :] + (pwr * cr - pwi * ci)
        si_ref[tile(j), :] = si_ref[tile(j), :] + (pwr * ci + pwi * cr)
        return pwr * ar - pwi * ai, pwr * ai + pwi * ar

    lax.fori_loop(0, T, fix, (ar, ai), unroll=8)


def _s5_forward(name, proj, wdt, cmt, lam_r, lam_i):
    bl = SCAN_LANES
    nblk = NS // bl
    cw = bl * GCH // NST
    assert cw == 128 and CB_U == 0

    def nt(a, b_):
        return lax.dot_general(a, b_, (((1,), (1,)), ((), ())), preferred_element_type=F32)

    def body(u_ref, wr_ref, wi_ref, cr_ref, ci_ref, ar_ref, ai_ref, s_ref, y_ref):
        sr_ref, si_ref = s_ref.at[0], s_ref.at[1]
        u = u_ref[...].astype(BF16)
        sr_ref[...] = nt(u, wr_ref[...])
        si_ref[...] = nt(u, wi_ref[...])
        _scan_block(sr_ref, si_ref, sr_ref, si_ref, ar_ref[...], ai_ref[...], False)
        y_ref[...] = nt(sr_ref[...].astype(BF16), cr_ref[...]) + nt(si_ref[...].astype(BF16), ci_ref[...])

    return pl.pallas_call(
        body, name=name, grid=(nblk,),
        in_specs=[pl.BlockSpec((SEQ, cw), lambda i: (0, i)),
                  pl.BlockSpec((bl, cw), lambda i: (i, i)), pl.BlockSpec((bl, cw), lambda i: (nblk + i, i)),
                  pl.BlockSpec((cw, bl), lambda i: (i, i)), pl.BlockSpec((cw, bl), lambda i: (SW // cw + i, i)),
                  pl.BlockSpec((1, bl), lambda i: (0, i)), pl.BlockSpec((1, bl), lambda i: (0, i))],
        out_specs=[pl.BlockSpec((2, SEQ, bl), lambda i: (0, 0, i)), pl.BlockSpec((SEQ, cw), lambda i: (0, i))],
        out_shape=[jax.ShapeDtypeStruct((2, SEQ, NS), F32), jax.ShapeDtypeStruct((SEQ, SW), F32)],
        compiler_params=_params("arbitrary"),
    )(proj, wdt, wdt, cmt, cmt, lam_r, lam_i)


S5_BWD_LANES = 512


def _s5_backward(name, dy0, proj, s, wdt, cmt, lam_r, lam_i):
    bl = S5_BWD_LANES
    nblk = NS // bl
    cw = 128
    per = cw // (bl * GCH // NST)
    assert per >= 1 and CB_U == 0

    def tn(a, b_):
        return lax.dot_general(a, b_, (((0,), (0,)), ((), ())), preferred_element_type=F32)

    def prev(s_ref):
        last = pltpu.roll(s_ref[SEQ - SCAN_CHUNKS:SEQ, :], 1, 0)
        first = jnp.where(lax.broadcasted_iota(jnp.int32, (SCAN_CHUNKS, bl), 0) > 0, last, 0.0)
        return jnp.concatenate([first, s_ref[0:SEQ - SCAN_CHUNKS, :]], axis=0)

    def body(dy_ref, u_ref, s_ref, cr_ref, ci_ref, wr_ref, wi_ref, lr_ref, li_ref,
             dlr_ref, dli_ref, dwdt_ref, dcmt_ref, du_ref, ar_ref, ai_ref):
        dy = dy_ref[...]
        ar_ref[...] = jnp.dot(dy, cr_ref[...], preferred_element_type=F32)
        ai_ref[...] = jnp.dot(dy, ci_ref[...], preferred_element_type=F32)
        _scan_block(ar_ref, ai_ref, ar_ref, ai_ref, lr_ref[...], -li_ref[...], True)
        a_r, a_i = ar_ref[...], ai_ref[...]
        sr_ref, si_ref = s_ref.at[0], s_ref.at[1]
        spr, spi = prev(sr_ref), prev(si_ref)
        dlr_ref[...] = jnp.sum(a_r * spr + a_i * spi, axis=0, keepdims=True)
        dli_ref[...] = jnp.sum(a_i * spr - a_r * spi, axis=0, keepdims=True)
        a_rb, a_ib = a_r.astype(BF16), a_i.astype(BF16)
        u = u_ref[...].astype(BF16)
        dwdt_ref[0] = tn(a_rb, u)
        dwdt_ref[1] = tn(a_ib, u)
        dcmt_ref[0] = tn(dy, sr_ref[...].astype(BF16))
        dcmt_ref[1] = tn(dy, si_ref[...].astype(BF16))
        part = (jnp.dot(a_rb, wr_ref[...], preferred_element_type=F32)
                + jnp.dot(a_ib, wi_ref[...], preferred_element_type=F32))

        @pl.when(pl.program_id(0) % per == 0)
        def _():
            du_ref[...] = part

        @pl.when(pl.program_id(0) % per > 0)
        def _():
            du_ref[...] += part

    lam_spec = pl.BlockSpec((1, bl), lambda i: (0, i))
    return pl.pallas_call(
        body, name=name, grid=(nblk,),
        in_specs=[pl.BlockSpec((SEQ, cw), lambda i: (0, i // per)), pl.BlockSpec((SEQ, cw), lambda i: (0, i // per)),
                  pl.BlockSpec((2, SEQ, bl), lambda i: (0, 0, i)),
                  pl.BlockSpec((cw, bl), lambda i: (i // per, i)),
                  pl.BlockSpec((cw, bl), lambda i: (SW // cw + i // per, i)),
                  pl.BlockSpec((bl, cw), lambda i: (i, i // per)), pl.BlockSpec((bl, cw), lambda i: (nblk + i, i // per)),
                  lam_spec, lam_spec],
        out_specs=[lam_spec, lam_spec, pl.BlockSpec((2, bl, cw), lambda i: (0, i, i // per)),
                   pl.BlockSpec((2, cw, bl), lambda i: (0, i // per, i)),
                   pl.BlockSpec((SEQ, cw), lambda i: (0, i // per))],
        out_shape=[jax.ShapeDtypeStruct((1, NS), F32), jax.ShapeDtypeStruct((1, NS), F32),
                   jax.ShapeDtypeStruct((2, NS, SW), F32), jax.ShapeDtypeStruct((2, SW, NS), F32),
                   jax.ShapeDtypeStruct((SEQ, SW), F32)],
        scratch_shapes=[pltpu.VMEM((SEQ, bl), F32)] * 2,
        compiler_params=_params("arbitrary"),
    )(dy0, proj, s, cmt, cmt, wdt, wdt, lam_r, lam_i)


def _scores(qb, kb, prev):
    s = lax.dot_general(qb, kb, (((1,), (1,)), ((), ())), preferred_element_type=F32) * (HD ** -0.5)
    row = lax.broadcasted_iota(jnp.int32, (ABLK, ABLK), 0)
    col = lax.broadcasted_iota(jnp.int32, (ABLK, ABLK), 1)
    return jnp.where((col >= row) if prev else (col <= row), s, -1e30)


def _block_rows(dil, r, b):
    if dil == 1:
        return pl.ds(pl.multiple_of(b * ABLK, ABLK), ABLK)
    return pl.ds(r + dil * ABLK * b, ABLK, stride=dil)


def _group_blocks(dil):
    nb = SEQ // dil // ABLK
    shift = nb.bit_length() - 1
    return nb, (lambda idx: (idx >> shift, idx & (nb - 1)))


def _qkv_specs(j_of):
    return [pl.BlockSpec((SEQ, HD), functools.partial(lambda j, c: (0, c + j_of(j)), c=(cb + g) * 4))
            for g in range(3) for cb in (CB_Q, CB_K, CB_V)]


def _attention_fwd(name, proj):
    def body(*refs):
        qkv, z_ref = refs[:9], refs[9]
        y_ref, ya_ref, l_ref = refs[10:13]
        accs, maxs, dens = refs[13:16], refs[16:19], refs[19:22]
        for g, dil in enumerate(DILATIONS):
            q_ref, k_ref, v_ref = qkv[3 * g:3 * g + 3]
            nb, where = _group_blocks(dil)

            def step(t, c, g=g, dil=dil, nb=nb, where=where, q_ref=q_ref, k_ref=k_ref, v_ref=v_ref):
                two = range(ATTN_PAIR_FWD)
                rb = [where(t + i * (SEQ // ABLK // ATTN_PAIR_FWD)) for i in two]
                rows = [_block_rows(dil, r, b) for r, b in rb]
                qb = [q_ref[rows[i], :].astype(BF16) for i in two]
                s_c = [_scores(qb[i], k_ref[rows[i], :].astype(BF16), False) for i in two]
                if nb > 1:
                    prev = [_block_rows(dil, r, jnp.maximum(b - 1, 0)) for r, b in rb]
                    s_p = [jnp.where(rb[i][1] > 0, _scores(qb[i], k_ref[prev[i], :].astype(BF16), True), -1e30)
                           for i in two]
                m = [jnp.max(s_c[i], axis=-1, keepdims=True) for i in two]
                if nb > 1:
                    m = [jnp.maximum(m[i], jnp.max(s_p[i], axis=-1, keepdims=True)) for i in two]
                p_c = [jnp.exp(s_c[i] - m[i]) for i in two]
                den = [jnp.sum(p_c[i], axis=-1, keepdims=True) for i in two]
                acc = [jnp.dot(p_c[i].astype(BF16), v_ref[rows[i], :].astype(BF16), preferred_element_type=F32)
                       for i in two]
                if nb > 1:
                    p_p = [jnp.exp(s_p[i] - m[i]) for i in two]
                    den = [den[i] + jnp.sum(p_p[i], axis=-1, keepdims=True) for i in two]
                    acc = [acc[i] + jnp.dot(p_p[i].astype(BF16), v_ref[prev[i], :].astype(BF16),
                                            preferred_element_type=F32) for i in two]
                for i in two:
                    accs[g][rows[i], :] = acc[i]
                    maxs[g][rows[i], :] = jnp.broadcast_to(m[i], (ABLK, HD))
                    dens[g][rows[i], :] = jnp.broadcast_to(den[i], (ABLK, HD))
                return c

            lax.fori_loop(0, SEQ // ABLK // ATTN_PAIR_FWD, step, 0)
        top = jnp.maximum(jnp.maximum(maxs[0][...], maxs[1][...]), maxs[2][...])
        den = jnp.zeros((SEQ, HD), F32)
        y = jnp.zeros((SEQ, HD), F32)
        for g in range(3):
            wgt = jnp.exp(maxs[g][...] - top)
            den = den + wgt * dens[g][...]
            y = y + wgt * accs[g][...]
        y = y / den
        y_ref[...] = y
        ya_ref[...] = (y * _silu(z_ref[...])).astype(ya_ref.dtype)
        l_ref[...] = top + jnp.log(den)

    ospec = pl.BlockSpec((SEQ, HD), lambda j: (0, j))
    return pl.pallas_call(
        body, name=name, grid=(AW // HD,),
        in_specs=_qkv_specs(lambda j: j) + [pl.BlockSpec((SEQ, HD), lambda j: (0, CB_ZA * 4 + j))],
        out_specs=[ospec, ospec, ospec],
        out_shape=[jax.ShapeDtypeStruct((SEQ, AW), F32), jax.ShapeDtypeStruct((SEQ, AW), BF16),
                   jax.ShapeDtypeStruct((SEQ, AW), F32)],
        scratch_shapes=[pltpu.VMEM((SEQ, HD), F32)] * 9,
        compiler_params=_params("parallel"),
    )(*([proj] * 10))


def _attention_bwd(name, proj, dya, y, lse):
    def tn(a, b_):
        return lax.dot_general(a, b_, (((0,), (0,)), ((), ())), preferred_element_type=F32)

    def nt(a, b_):
        return lax.dot_general(a, b_, (((1,), (1,)), ((), ())), preferred_element_type=F32)

    def body(*refs):
        qkv, z_ref, dya_ref, y_ref, l_ref = refs[:9], refs[9], refs[10], refs[11], refs[12]
        outs, dza_ref = refs[13:22], refs[22]
        dy_s, dsum_s, dq_s, dk_own, dv_own, dk_prev, dv_prev = refs[23:]
        _, vjp = jax.vjp(lambda y_, z_: y_ * _silu(z_), y_ref[...], z_ref[...])
        dy, dz = vjp(dya_ref[...])
        dza_ref[...] = dz.astype(dza_ref.dtype)
        dy_s[...] = dy
        dsum_s[...] = jnp.broadcast_to(jnp.sum(dy * y_ref[...], axis=-1, keepdims=True), (SEQ, HD))
        for g, dil in enumerate(DILATIONS):
            q_ref, k_ref, v_ref = qkv[3 * g:3 * g + 3]
            nb, where = _group_blocks(dil)
            if nb > 1:
                dk_prev[...] = jnp.zeros(dk_prev.shape, F32)
                dv_prev[...] = jnp.zeros(dv_prev.shape, F32)

            def step(t, c, dil=dil, nb=nb, where=where, q_ref=q_ref, k_ref=k_ref, v_ref=v_ref):
                rb = [where(t + i * (SEQ // ABLK // ATTN_PAIR)) for i in range(ATTN_PAIR)]
                sides = []
                for r, b in rb:
                    rows = _block_rows(dil, r, b)
                    own = dict(b=b, qrows=rows, krows=rows, prev=False, dk=dk_own, dv=dv_own,
                               q=q_ref[rows, :].astype(BF16), dy=dy_s[rows, :].astype(BF16))
                    sides.append(own)
                    if nb > 1:
                        sides.append(dict(own, krows=_block_rows(dil, r, jnp.maximum(b - 1, 0)), prev=True,
                                          dk=dk_prev, dv=dv_prev))
                for s_ in sides:
                    s_["k"] = k_ref[s_["krows"], :].astype(BF16)
                    s_["v"] = v_ref[s_["krows"], :].astype(BF16)
                for s_ in sides:
                    sc = _scores(s_["q"], s_["k"], s_["prev"])
                    s_["s"] = jnp.where(s_["b"] > 0, sc, -1e30) if s_["prev"] else sc
                    s_["dp"] = nt(s_["dy"], s_["v"])
                for s_ in sides:
                    p = jnp.exp(s_["s"] - l_ref[s_["qrows"], :])
                    s_["p"] = p.astype(BF16)
                    s_["ds"] = (p * (s_["dp"] - dsum_s[s_["qrows"], :]) * (HD ** -0.5)).astype(BF16)
                for s_ in sides:
                    s_["dk"][s_["krows"], :] = tn(s_["ds"], s_["q"])
                    s_["dv"][s_["krows"], :] = tn(s_["p"], s_["dy"])
                    s_["dq"] = jnp.dot(s_["ds"], s_["k"], preferred_element_type=F32)
                per = len(sides) // ATTN_PAIR
                for i in range(ATTN_PAIR):
                    dq = sides[i * per]["dq"]
                    if per > 1:
                        dq = dq + sides[i * per + 1]["dq"]
                    dq_s[sides[i * per]["qrows"], :] = dq
                return c

            lax.fori_loop(0, SEQ // ABLK // ATTN_PAIR, step, 0)
            dq_ref, dk_ref, dv_ref = outs[3 * g:3 * g + 3]
            dq_ref[...] = dq_s[...].astype(dq_ref.dtype)
            if nb > 1:
                dk_ref[...] = (dk_own[...] + dk_prev[...]).astype(dk_ref.dtype)
                dv_ref[...] = (dv_own[...] + dv_prev[...]).astype(dv_ref.dtype)
            else:
                dk_ref[...] = dk_own[...].astype(dk_ref.dtype)
                dv_ref[...] = dv_own[...].astype(dv_ref.dtype)

    ospec = pl.BlockSpec((SEQ, HD), lambda j: (0, j))
    outs = pl.pallas_call(
        body, name=name, grid=(AW // HD,),
        in_specs=_qkv_specs(lambda j: j) + [pl.BlockSpec((SEQ, HD), lambda j: (0, CB_ZA * 4 + j))] + [ospec] * 3,
        out_specs=[ospec] * 10, out_shape=[jax.ShapeDtypeStruct((SEQ, AW), BF16)] * 10,
        scratch_shapes=[pltpu.VMEM((SEQ, HD), F32)] * 7,
        compiler_params=_params("parallel"),
    )(*([proj] * 10), dya, y, lse)
    return outs[:9], outs[9]


def _rms(x, g):
    return x * lax.rsqrt(jnp.mean(x * x, axis=-1, keepdims=True) + RMS_EPS) * g


def _sig(x):
    return 1.0 / (1.0 + jnp.exp(-x))


def _silu(x):
    return x * _sig(x)


def _gelu(x):
    return 0.5 * x * (1.0 + jnp.tanh(math.sqrt(2.0 / math.pi) * (x + 0.044715 * (x * x * x))))


def _y1_fn(y0p, u, dskip):
    return _gelu(y0p + dskip * u)


def _ys_fn(y1, t, z, bglu):
    return y1 * _sig(t + bglu) * _silu(z)


def _merge_fn(ms, ma, gs, ga):
    return _sig(gs) * ms.astype(F32) + _sig(ga) * ma.astype(F32)


def _colsum(v):
    return jnp.sum(v, axis=0, keepdims=True)


def _lam_fn(lre, lim, ldt):
    a = jnp.minimum(lre, -1e-4)
    dt = jnp.exp(ldt)
    mag = jnp.exp(a * dt)
    ar = mag * jnp.cos(lim * dt)
    ai = mag * jnp.sin(lim * dt)
    den = a * a + lim * lim
    cr = ((ar - 1.0) * a + ai * lim) / den
    ci = (ai * a - (ar - 1.0) * lim) / den
    return ar, ai, cr, ci


def _bbar_fn(cr, ci, bre, bim):
    return cr * bre - ci * bim, cr * bim + ci * bre


def _same_group(rows, a, cols, b):
    r = lax.broadcasted_iota(jnp.int32, (rows, cols), 0) >> (a.bit_length() - 1)
    c = lax.broadcasted_iota(jnp.int32, (rows, cols), 1) >> (b.bit_length() - 1)
    return r == c


def _expand(name, blocks, signs, a, b, dtype, after=None):
    rows, cols = NGRP * a, NGRP * b
    n = len(blocks)
    assert a & (a - 1) == 0 and b & (b - 1) == 0
    after_specs, after_args = _after(after)

    def body(*refs):
        o_ref = refs[-1]
        tile = (lax.broadcasted_iota(jnp.int32, (b, cols), 1) & (b - 1)
                == lax.broadcasted_iota(jnp.int32, (b, cols), 0)).astype(F32)
        keep = _same_group(rows, a, cols, b)
        for i, (ref, sign) in enumerate(zip(refs[:n], signs)):
            spread = jnp.dot(ref[...], tile, preferred_element_type=F32, precision=lax.Precision.HIGHEST)
            o_ref[i * rows:(i + 1) * rows, :] = jnp.where(keep, sign * spread, 0.0).astype(o_ref.dtype)

    return pl.pallas_call(body, name=name, grid=(1,),
                          in_specs=[pl.BlockSpec((rows, b), lambda i: (0, 0))] * n + after_specs,
                          out_specs=pl.BlockSpec((n * rows, cols), lambda i: (0, 0)),
                          out_shape=jax.ShapeDtypeStruct((n * rows, cols), dtype),
                          compiler_params=_params("arbitrary"))(*blocks, *after_args)


def _extract(name, m, a, b, ats, after=None):
    rows, cols = NGRP * a, NGRP * b
    n = len(ats)
    assert a & (a - 1) == 0 and b & (b - 1) == 0
    after_specs, after_args = _after(after)

    def body(*refs):
        tile = (lax.broadcasted_iota(jnp.int32, (cols, b), 0) & (b - 1)
                == lax.broadcasted_iota(jnp.int32, (cols, b), 1)).astype(F32)
        keep = _same_group(rows, a, cols, b)
        for m_ref, o_ref in zip(refs[:n], refs[-n:]):
            kept = jnp.where(keep, m_ref[...], 0.0)
            o_ref[...] = jnp.dot(kept, tile, preferred_element_type=F32, precision=lax.Precision.HIGHEST)

    return pl.pallas_call(
        body, name=name, grid=(1,),
        in_specs=[pl.BlockSpec((rows, cols), functools.partial(lambda i, at: at, at=at)) for at in ats] + after_specs,
        out_specs=[pl.BlockSpec((rows, b), lambda i: (0, 0))] * n,
        out_shape=[jax.ShapeDtypeStruct((rows, b), F32)] * n,
        compiler_params=_params("arbitrary"))(*([m] * n), *after_args)


def _s5_prepare(l, sp, after):
    tag = f"l{l}_"
    ar, ai, cr, ci = _ew(tag + "lam", _lam_fn, NGRP, NGRP,
                         [_ri(sp["lambda_re"]), _ri(sp["lambda_im"]),
                          _ri(sp["log_dt"].reshape(NGRP, 1) + after[0, 0])], [], [(NST, F32)] * 4)
    bre = sp["b_re"].reshape(NS, GCH)
    bim = sp["b_im"].reshape(NS, GCH)
    bbr, bbi = _ew(tag + "bbar", _bbar_fn, NS, NS, [_ri(cr.reshape(NS, 1)), _ri(ci.reshape(NS, 1)), _ri(bre), _ri(bim)],
                   [], [(GCH, F32)] * 2)
    wdt = _expand(tag + "wdt", [bbr, bbi], [1.0, 1.0], NST, GCH, BF16)
    cmt = _expand(tag + "cmt", [sp["c_re"].reshape(SW, NST), sp["c_im"].reshape(SW, NST)], [1.0, -1.0], GCH, NST, BF16,
                  after=after)
    return dict(ar=ar, ai=ai, cr=cr, ci=ci, wdt=wdt, cmt=cmt)


def _layer_head(l, x, sp):
    g1 = sp["pre_norm_g"].reshape(1, DM)
    (h,) = _ew(f"l{l}_rms1", lambda x_, g: (_rms(x_, g),), SEQ, EW_ROWS, [_ri(x)], [g1], [(DM, BF16)])
    return jnp.stack([h, _to_chunked(h)])


def _layer_fwd(l, x, hv, w, sp, prep, after):
    tag = f"l{l}_"
    win = w["w_in"]
    proj = _mm(tag + "proj", hv, win, "nn", SEQ, NCOL, DM, SEQ, 512, 1024, F32,
               a_spec=pl.BlockSpec((None, SEQ, 1024), lambda i, j, k: (_row_order(j), 0, 0)),
               b_spec=pl.BlockSpec((None, 1024, 512), lambda i, j, k: (j // 2, 0, j % 2)), after=after)

    ar, ai, cr, ci, wdt, cmt = (prep[k] for k in ("ar", "ai", "cr", "ci", "wdt", "cmt"))
    s, y0p = _s5_forward(tag + "s5", proj, wdt, cmt, ar.reshape(1, NS), ai.reshape(1, NS))
    dskip = sp["d_skip"].reshape(1, SW)
    (y1,) = _ew(tag + "y1", lambda a, u, d: (_y1_fn(a, u, d),), SEQ, EW_ROWS,
                [_ri(y0p), _ri(proj, SW, CB_U)], [dskip], [(SW, F32)])
    t = _mm(tag + "glu", y1, w["w_glu"], "nn", SEQ, SW, SW, 1024, 512, 512, F32)
    bglu = sp["b_glu"].reshape(1, SW)
    (ys_c,) = _ew(tag + "ys", lambda y1_, t_, z, b_: (_ys_fn(y1_, t_, z, b_),), SEQ, EW_ROWS,
                  [_ri(y1), _ri(t), _ri(proj, SW, CB_ZS)], [bglu], [(SW, BF16)])
    ys = _from_chunked(ys_c)

    ypre, ya, lse = _attention_fwd(tag + "attn", proj)

    ms = _mm(tag + "branch_s", ys, w["w_branch_s"], "nn", SEQ, DM, SW, 1024, 1024, 512, BF16)
    ma = _mm(tag + "branch_a", ya, w["w_branch_a"], "nn", SEQ, DM, AW, 1024, 1024, 512, BF16)
    (merged,) = _ew(tag + "merge", lambda a, b_, c, d: (_merge_fn(a, b_, c, d),), SEQ, EW_ROWS,
                    [_ri(ms), _ri(ma), _ri(proj, DM, CB_GS), _ri(proj, DM, CB_GA)], [], [(DM, BF16)])
    out = _mm(tag + "out", merged, w["w_out"], "nn", SEQ, DM, DM, 1024, 1024, 1024, F32)
    g2 = sp["post_norm_g"].reshape(1, DM)
    (x_new,) = _ew(tag + "post", lambda x_, o, g: (x_ + _rms(o, g),), SEQ, EW_ROWS, [_ri(x), _ri(out)], [g2], [(DM, F32)])
    res = dict(x=x, hv=hv, proj=proj, ar=ar, ai=ai, cr=cr, ci=ci, wdt=wdt, cmt=cmt, s=s, y0p=y0p,
               y1=y1, t=t, ys=ys, ya=ya, ypre=ypre, lse=lse, ms=ms, ma=ma, merged=merged, out=out)
    return x_new, res


def _layer_bwd(l, dxn, r, w, sp, big_done, after=None):
    tag = f"l{l}b_"
    proj = r["proj"]
    g1 = sp["pre_norm_g"].reshape(1, DM)
    g2 = sp["post_norm_g"].reshape(1, DM)
    dskip = sp["d_skip"].reshape(1, SW)
    bglu = sp["b_glu"].reshape(1, SW)

    def post_b(d, o, g):
        _, vjp = jax.vjp(_rms, o, g)
        do, dg = vjp(d)
        return do, dg

    d_out, dg2 = _ew(tag + "post", post_b, SEQ, EW_ROWS, [_ri(dxn), _ri(r["out"])], [g2], [(DM, BF16)], [DM])
    dw_out = _mm(tag + "dw_out", r["merged"], d_out, "tn", DM, DM, SEQ, 1024, 1024, SEQ, BF16, after=after)
    dmerged = _mm(tag + "dmerged", d_out, w["w_out"], "nt", SEQ, DM, DM, 1024, 1024, 1024, F32, after=after)

    def merge_b(d, ms, ma, gs, ga):
        _, vjp = jax.vjp(_merge_fn, ms, ma, gs, ga)
        return vjp(d)

    dms, dma, dgs, dga = _ew(tag + "merge", merge_b, SEQ, EW_ROWS,
                             [_ri(dmerged), _ri(r["ms"]), _ri(r["ma"]), _ri(proj, DM, CB_GS), _ri(proj, DM, CB_GA)],
                             [], [(DM, BF16)] * 4)
    dw_bs = _mm(tag + "dw_bs", r["ys"], dms, "tn", SW, DM, SEQ, 512, 1024, SEQ, BF16)
    dw_ba = _mm(tag + "dw_ba", r["ya"], dma, "tn", AW, DM, SEQ, 512, 1024, SEQ, BF16)
    dys = _mm(tag + "dys", dms, w["w_branch_s"], "nt", SEQ, SW, DM, 1024, 512, 1024, F32)
    dya = _mm(tag + "dya", dma, w["w_branch_a"], "nt", SEQ, AW, DM, 1024, 512, 1024, F32)

    dqkv, dza = _attention_bwd(tag + "attn", proj, dya, r["ypre"], r["lse"])

    def ys_b(d, y1, t, z, b_):
        _, vjp = jax.vjp(_ys_fn, y1, t, z, b_)
        dy1, dt, dz, _ = vjp(d)
        return dy1, dt, dz, _colsum(dt)

    dy1a, dt, dzs, dbglu = _ew(tag + "ys", ys_b, SEQ, EW_ROWS,
                               [_ri(_to_chunked(dys)), _ri(r["y1"]), _ri(r["t"]), _ri(proj, SW, CB_ZS)],
                               [bglu], [(SW, F32), (SW, BF16), (SW, BF16)], [SW])
    dw_glu = _mm(tag + "dw_glu", r["y1"], dt, "tn", SW, SW, SEQ, 512, 512, SEQ, BF16)
    dy1b = _mm(tag + "dy1b", dt, w["w_glu"], "nt", SEQ, SW, SW, 1024, 512, 512, F32)

    def y1_b(da, db, y0p, u, d_):
        _, vjp = jax.vjp(_y1_fn, y0p, u, d_)
        dy0, du, dd = vjp(da + db)
        return dy0, du, dd

    dy0, du_skip, ddskip = _ew(tag + "y1", y1_b, SEQ, EW_ROWS,
                               [_ri(dy1a), _ri(dy1b), _ri(r["y0p"]), _ri(proj, SW, CB_U)], [dskip],
                               [(SW, BF16), (SW, F32)], [SW])
    dlr, dli, dwdt, dcmt, du_s = _s5_backward(tag + "s5", dy0, proj, r["s"], r["wdt"], r["cmt"],
                                              r["ar"].reshape(1, NS), r["ai"].reshape(1, NS))
    (du,) = _ew(tag + "du", lambda a, c: (a + c,), SEQ, EW_ROWS, [_ri(du_s), _ri(du_skip)], [], [(SW, BF16)])

    dq, dk, dv = ([dqkv[3 * g + i] for g in range(3)] for i in range(3))
    dproj = jnp.concatenate([du, dzs, *dq, *dk, *dv, dza, dgs, dga], axis=1)
    dw_in = _mm(tag + "dw_in", r["hv"], dproj, "tn", DM, NCOL, SEQ, 1024, 512, SEQ, BF16,
                a_spec=pl.BlockSpec((None, SEQ, 1024), lambda i, j, k: (_row_order(j), 0, 0)),
                o_spec=pl.BlockSpec((None, 1024, 512), lambda i, j, k: (j // 2, 0, j % 2)), out_shape=(NDEV, DM, DM))
    tok = big_done(l, dict(w_in=dw_in, w_glu=dw_glu.reshape(NDEV, SW // NDEV, SW),
                           w_branch_s=dw_bs.reshape(SW, NDEV, DM // NDEV).transpose(1, 0, 2),
                           w_branch_a=dw_ba.reshape(AW, NDEV, DM // NDEV).transpose(1, 0, 2),
                           w_out=dw_out.reshape(NDEV, DM // NDEV, DM)))
    if tok is not None:
        g1 = g1 + tok[0, 0]

    dwdt = dwdt.reshape(2 * NS, SW)
    dcmt = dcmt.reshape(2 * SW, NS)
    dbbr, dbbi = _extract(tag + "dbb", dwdt, NST, GCH, ((0, 0), (1, 0)), after=tok)
    bre = sp["b_re"].reshape(NS, GCH)
    bim = sp["b_im"].reshape(NS, GCH)

    def bbar_b(cr, ci, br_, bi_, dr, di):
        _, vjp = jax.vjp(_bbar_fn, cr, ci, br_, bi_)
        return vjp((dr, di))

    dcr, dci, dbre, dbim = _ew(tag + "bbar", bbar_b, NS, NS,
                               [_ri(r["cr"].reshape(NS, 1)), _ri(r["ci"].reshape(NS, 1)), _ri(bre), _ri(bim),
                                _ri(dbbr), _ri(dbbi)], [], [(1, F32), (1, F32), (GCH, F32), (GCH, F32)])

    def lam_b(lre, lim, ldt, dar, dai, dcr_, dci_):
        _, vjp = jax.vjp(_lam_fn, lre, lim, ldt)
        return vjp((dar, dai, dcr_, dci_))

    dlre, dlim, dldt = _ew(tag + "lam", lam_b, NGRP, NGRP,
                           [_ri(sp["lambda_re"]), _ri(sp["lambda_im"]), _ri(sp["log_dt"].reshape(NGRP, 1)),
                            _ri(dlr.reshape(NGRP, NST)), _ri(dli.reshape(NGRP, NST)),
                            _ri(dcr.reshape(NGRP, NST)), _ri(dci.reshape(NGRP, NST))], [],
                           [(NST, F32), (NST, F32), (1, F32)])
    dc_re, dc_im = _extract(tag + "dc", dcmt, GCH, NST, ((0, 0), (1, 0)), after=tok)
    dc_re, dc_im = dc_re.reshape(NGRP, GCH, NST), -dc_im.reshape(NGRP, GCH, NST)

    dh_time = _mm(tag + "dh_time", dproj, w["w_in"], "nt", SEQ, DM, NCOL - DM, SEQ, 1024, 1024, F32,
                  a_spec=pl.BlockSpec((SEQ, 1024), lambda i, j, k: (0, 1 + k)),
                  b_spec=pl.BlockSpec((None, 1024, 1024), lambda i, j, k: (1 + k, 0, 0)), after=tok)
    dh_chunked = _mm(tag + "dh_chunked", dproj, w["w_in"], "nt", SEQ, DM, DM, SEQ, 1024, 1024, F32,
                     a_spec=pl.BlockSpec((SEQ, 1024), lambda i, j, k: (0, 0)),
                     b_spec=pl.BlockSpec((None, 1024, 1024), lambda i, j, k: (0, 0, 0)), after=tok)
    dh = [dh_time, _from_chunked(dh_chunked)]

    def pre_b(d, dh0, dh1, x_, g):
        _, vjp = jax.vjp(_rms, x_, g)
        dx_, dg = vjp(dh0 + dh1)
        return d + dx_, dg

    dx, dg1 = _ew(tag + "pre", pre_b, SEQ, EW_ROWS, [_ri(dxn)] + [_ri(t_) for t_ in dh] + [_ri(r["x"])], [g1],
                  [(DM, F32)], [DM])

    small = dict(pre_norm_g=dg1.reshape(DM), lambda_re=dlre, lambda_im=dlim, log_dt=dldt.reshape(NGRP),
                 b_re=dbre.reshape(NGRP, NST, GCH), b_im=dbim.reshape(NGRP, NST, GCH), c_re=dc_re, c_im=dc_im,
                 d_skip=ddskip.reshape(SW), b_glu=dbglu.reshape(SW), post_norm_g=dg2.reshape(DM))
    return dx, small


_HBM = pl.BlockSpec(memory_space=pltpu.HBM)
_SEM = pl.BlockSpec(memory_space=pltpu.SEMAPHORE)
_EFFECT = pltpu.SideEffectType.DATAFLOW_SIDE_EFFECTING


def _remote_copies(srcs, dsts, send_sems, recv_sems, gather):
    x, y, c = lax.axis_index("x"), lax.axis_index("y"), lax.axis_index("c")
    me = 4 * x + 2 * y + c
    copies = []
    for i in range(len(srcs)):
        for k in range(1, NDEV):
            peer = (x ^ (k >> 2), y ^ ((k >> 1) & 1), c ^ (k & 1))
            src = srcs[i] if gather[i] else srcs[i].at[me ^ k]
            copies.append(pltpu.make_async_remote_copy(
                src_ref=src, dst_ref=dsts[i].at[me], send_sem=send_sems[i], recv_sem=recv_sems[i],
                device_id=peer, device_id_type=pl.DeviceIdType.MESH))
    return copies


def _all_seven(dst, send_sem, recv_sem):
    seven = dst.at[pl.ds(0, NDEV - 1)]
    me = (lax.axis_index("x"), lax.axis_index("y"), lax.axis_index("c"))
    return pltpu.make_async_remote_copy(src_ref=seven, dst_ref=seven, send_sem=send_sem, recv_sem=recv_sem,
                                        device_id=me, device_id_type=pl.DeviceIdType.MESH)


def _own_slabs(name, arrs, gather, after):
    n = len(arrs)
    me = (4 * lax.axis_index("x") + 2 * lax.axis_index("y") + lax.axis_index("c")).astype(jnp.int32).reshape(1)

    def body(me_ref, *refs):
        for src, dst in zip(refs[:n], refs[n + 1:]):
            dst[...] = src[...]

    def zeros(k):
        return (0,) * k

    in_specs, out_specs, out_shape = [], [], []
    for a, g in zip(arrs, gather):
        slab = a.shape if g else a.shape[1:]
        nd = len(slab)
        if g:
            in_specs.append(pl.BlockSpec(slab, functools.partial(lambda i, me_ref, nd: zeros(nd), nd=nd)))
        else:
            in_specs.append(pl.BlockSpec((None,) + slab, functools.partial(lambda i, me_ref, nd: (me_ref[0],) + zeros(nd), nd=nd)))
        out_specs.append(pl.BlockSpec((None,) + slab, functools.partial(lambda i, me_ref, nd: (me_ref[0],) + zeros(nd), nd=nd)))
        out_shape.append(jax.ShapeDtypeStruct((NDEV,) + slab, a.dtype))
    in_specs.append(pl.BlockSpec(memory_space=pl.ANY))
    return pl.pallas_call(
        body, name=name, out_shape=out_shape,
        grid_spec=pltpu.PrefetchScalarGridSpec(num_scalar_prefetch=1, grid=(1,), in_specs=in_specs, out_specs=out_specs),
        compiler_params=_params("arbitrary"),
    )(me, *arrs, after)


def _exchange_start(name, arrs, gather, after):
    n = len(arrs)
    lands = _own_slabs(name + "_own", arrs, gather, after)

    def body(*refs):
        srcs, dsts = refs[:n], refs[n:2 * n]
        send_sems, recv_sems = refs[2 * n:3 * n], refs[3 * n:4 * n]
        token = refs[-1]
        for cp in _remote_copies(srcs, dsts, send_sems, recv_sems, gather):
            cp.start()
        token[...] = jnp.zeros(token.shape, token.dtype)

    thru = [pltpu.HBM(a.shape, a.dtype) for a in list(arrs) + list(lands)]
    outs = pl.pallas_call(
        body, name=name,
        out_shape=(*[pltpu.SemaphoreType.DMA(())] * (2 * n), *thru, jax.ShapeDtypeStruct((8, 128), F32)),
        in_specs=[_HBM] * (2 * n),
        out_specs=(*[_SEM] * (2 * n), *[_HBM] * (2 * n), pl.BlockSpec(memory_space=pltpu.VMEM)),
        input_output_aliases={i: 2 * n + i for i in range(2 * n)},
        compiler_params=pltpu.CompilerParams(has_side_effects=_EFFECT),
    )(*[pltpu.with_memory_space_constraint(a, pltpu.HBM) for a in list(arrs) + list(lands)])
    return dict(send=outs[:n], recv=outs[n:2 * n], srcs=outs[2 * n:3 * n], lands=outs[3 * n:4 * n], token=outs[-1],
                gather=gather)


def _exchange_wait(name, started, after):
    n = len(started["srcs"])
    after = list(after)

    def body(*refs):
        dsts = refs[n:2 * n]
        send_sems, recv_sems = refs[2 * n:3 * n], refs[3 * n:4 * n]
        for i in range(n):
            cp = _all_seven(dsts[i], send_sems[i], recv_sems[i])
            cp.wait_send()
            cp.wait_recv()

    bufs = list(started["srcs"]) + list(started["lands"])
    outs = pl.pallas_call(
        body, name=name, out_shape=tuple(pltpu.HBM(a.shape, a.dtype) for a in bufs),
        in_specs=[_HBM] * (2 * n) + [_SEM] * (2 * n) + [pl.BlockSpec(memory_space=pl.ANY)] * len(after),
        out_specs=(_HBM,) * (2 * n), input_output_aliases={i: i for i in range(2 * n)},
        compiler_params=pltpu.CompilerParams(has_side_effects=_EFFECT),
    )(*bufs, *started["send"], *started["recv"], *after)
    return outs[n:]


def _sum_in_order(parts):
    g = parts[0].astype(F32)
    for p in parts[1:]:
        g = g + p.astype(F32)
    return g


def _adam_update(g, w_, m_, v_):
    m2 = B1 * m_ + (1.0 - B1) * g
    v2 = B2 * v_ + (1.0 - B2) * (g * g)
    m_hat = m2 / (1.0 - B1 ** STEP)
    v_hat = v2 / (1.0 - B2 ** STEP)
    delta = -LR * (m_hat / (jnp.sqrt(v_hat) + ADAM_EPS) + WD * w_)
    return g, delta, m2, v2


def _adamw(name, g, w, m, v, br):
    rows, cols = w.shape
    return _ew(name, _adam_update, rows, br, [_ri(g), _ri(w), _ri(m), _ri(v)], [], [(cols, F32)] * 4)


def _adamw_layer(name, l, parts, w, m, v, br, outs):
    rows, cols = w.shape
    nb = rows // DEPTH // br
    assert nb * br * DEPTH == rows

    def body(*refs):
        vals = _adam_update(_sum_in_order([r[...] for r in refs[:NDEV]]), *[r[...] for r in refs[NDEV:NDEV + 3]])
        for r, val in zip(refs[-4:], vals):
            r[...] = val

    mine = pl.BlockSpec((br, cols), lambda i: (l * nb + i, 0))
    in_specs = [pl.BlockSpec((br, cols), functools.partial(lambda i, d: (d * nb + i, 0), d=d)) for d in range(NDEV)]
    in_specs += [mine] * 3 + [pl.BlockSpec(memory_space=pl.ANY)] * len(outs)
    return pl.pallas_call(
        body, name=f"{name}{l}", grid=(nb,), in_specs=in_specs, out_specs=[mine] * 4,
        out_shape=[jax.ShapeDtypeStruct((rows, cols), F32)] * 4,
        input_output_aliases={NDEV + 3 + q: q for q in range(len(outs))},
        compiler_params=_params("arbitrary"),
    )(*([parts] * NDEV), w, m, v, *outs)


SMALL = ("pre_norm_g", "lambda_re", "lambda_im", "log_dt", "b_re", "b_im", "c_re", "c_im", "d_skip", "b_glu",
         "post_norm_g")
BIG = ("w_in", "w_glu", "w_branch_s", "w_branch_a", "w_out")
WEIGHTS = ("pre_norm_g", "w_in", "lambda_re", "lambda_im", "log_dt", "b_re", "b_im", "c_re", "c_im", "d_skip",
           "w_glu", "b_glu", "w_branch_s", "w_branch_a", "w_out", "post_norm_g")
PACK_COLS = 1024
PACK_BR = 136


def _pack_layer(d):
    pieces = [d[k].astype(F32).reshape(-1) for k in SMALL]
    used = sum(p.shape[0] for p in pieces)
    assert used <= PACK_BR * PACK_COLS
    return jnp.concatenate(pieces + [jnp.zeros((PACK_BR * PACK_COLS - used,), F32)]).reshape(PACK_BR, PACK_COLS)


def _unpack(p, like):
    flat = p.reshape(DEPTH, PACK_BR * PACK_COLS)
    out, off = {}, 0
    for k in SMALL:
        n = like[k].size // DEPTH
        out[k] = flat[:, off:off + n].reshape(like[k].shape)
        off += n
    return out


def _local_step(x, target, small, started, weights_of, big_done, small_done, total_loss):
    preps = [_s5_prepare(l, {k: small[k][l] for k in SMALL}, started) for l in range(DEPTH)]
    res, ws = [], []
    for l in range(DEPTH):
        sp = {k: small[k][l] for k in SMALL}
        hv = _layer_head(l, x, sp)
        w_l, tok = weights_of(l, [hv] + ([p[k] for p in preps for k in ("wdt", "cmt")] if l == 0 else []))
        x, r = _layer_fwd(l, x, hv, w_l, sp, preps[l], tok)
        res.append(r)
        ws.append(w_l)

    def loss_fn(y, t):
        e = y - t
        return e * (1.0 / DM), jnp.sum(_colsum(0.5 * e * e * (1.0 / DM)), axis=1, keepdims=True)

    dx, loss = _ew("loss", loss_fn, SEQ, EW_ROWS, [_ri(x), _ri(target)], [], [(DM, F32)], [1])
    total = total_loss(loss.reshape(()))
    for l in reversed(range(DEPTH)):
        dx, sm = _layer_bwd(l, dx, res[l], ws[l], {k: small[k][l] for k in SMALL}, big_done,
                            after=total.reshape(1, 1) if l == DEPTH - 1 else None)
        small_done(l, sm)
    return total, dx


def _full_weights(gathered):
    g = gathered
    return dict(
        w_in=g["w_in"],
        w_glu=g["w_glu"].reshape(SW, SW),
        w_branch_s=g["w_branch_s"].transpose(1, 0, 2).reshape(SW, DM),
        w_branch_a=g["w_branch_a"].transpose(1, 0, 2).reshape(AW, DM),
        w_out=g["w_out"].reshape(DM, DM),
    )


def kernel(x, pre_norm_g, w_in, lambda_re, lambda_im, log_dt, b_re, b_im, c_re, c_im, d_skip, w_glu, b_glu, w_branch_s, w_branch_a, w_out, post_norm_g, loss_target, m_pre_norm_g, m_w_in, m_lambda_re, m_lambda_im, m_log_dt, m_b_re, m_b_im, m_c_re, m_c_im, m_d_skip, m_w_glu, m_b_glu, m_w_branch_s, m_w_branch_a, m_w_out, m_post_norm_g, v_pre_norm_g, v_w_in, v_lambda_re, v_lambda_im, v_log_dt, v_b_re, v_b_im, v_c_re, v_c_im, v_d_skip, v_w_glu, v_b_glu, v_w_branch_s, v_w_branch_a, v_w_out, v_post_norm_g):
    wts = dict(pre_norm_g=pre_norm_g, w_in=w_in, lambda_re=lambda_re, lambda_im=lambda_im, log_dt=log_dt, b_re=b_re,
               b_im=b_im, c_re=c_re, c_im=c_im, d_skip=d_skip, w_glu=w_glu, b_glu=b_glu, w_branch_s=w_branch_s,
               w_branch_a=w_branch_a, w_out=w_out, post_norm_g=post_norm_g)
    mom = dict(pre_norm_g=m_pre_norm_g, w_in=m_w_in, lambda_re=m_lambda_re, lambda_im=m_lambda_im, log_dt=m_log_dt,
               b_re=m_b_re, b_im=m_b_im, c_re=m_c_re, c_im=m_c_im, d_skip=m_d_skip, w_glu=m_w_glu, b_glu=m_b_glu,
               w_branch_s=m_w_branch_s, w_branch_a=m_w_branch_a, w_out=m_w_out, post_norm_g=m_post_norm_g)
    var = dict(pre_norm_g=v_pre_norm_g, w_in=v_w_in, lambda_re=v_lambda_re, lambda_im=v_lambda_im, log_dt=v_log_dt,
               b_re=v_b_re, b_im=v_b_im, c_re=v_c_re, c_im=v_c_im, d_skip=v_d_skip, w_glu=v_w_glu, b_glu=v_b_glu,
               w_branch_s=v_w_branch_s, w_branch_a=v_w_branch_a, w_out=v_w_out, post_norm_g=v_post_norm_g)

    def gather_start(l, after):
        return _exchange_start(f"gather_start{l}", [wts[k][l].astype(BF16) for k in BIG], [True] * len(BIG), after)

    gathering = {0: gather_start(0, x)}
    sending, packed = {}, {}

    def weights_of(l, after):
        gathered = _exchange_wait(f"gather_wait{l}", gathering[l], after)
        tok = None
        if l + 1 < DEPTH:
            gathering[l + 1] = gather_start(l + 1, gathered[0])
            tok = gathering[l + 1]["token"]
        return _full_weights(dict(zip(BIG, gathered))), tok

    def big_done(l, big):
        arrs, kinds = [big[k] for k in BIG], [False] * len(BIG)
        if l + 1 < DEPTH:
            arrs, kinds = arrs + [packed[l + 1]], kinds + [True]
        sending[l] = _exchange_start(f"grads_start{l}", arrs, kinds, big["w_in"])
        return sending[l]["token"]

    def small_done(l, sm):
        packed[l] = _pack_layer(sm)

    loss, dx = _local_step(x[0], loss_target[0], wts, gathering[0]["token"], weights_of, big_done, small_done,
                           lambda part: lax.psum(part, ("x", "y", "c")))
    last = _exchange_start("grads_start_last", [packed[0]], [True], dx)

    grads, delta, new_m, new_v = {}, {}, {}, {}

    def as_rows(k):
        cols = wts[k].shape[-1]
        rows = wts[k].size // cols
        return rows, cols, [t[k].reshape(rows, cols) for t in (wts, mom, var)]

    recv_l, outs, after = {}, {k: [] for k in BIG}, [dx]
    for l in reversed(range(DEPTH)):
        recv_l[l] = _exchange_wait(f"grads_wait{l}", sending[l], after)
        for i, k in enumerate(BIG):
            rows, cols, wmv = as_rows(k)
            per_layer = rows // DEPTH
            outs[k] = _adamw_layer("adamw_" + k, l, recv_l[l][i].reshape(NDEV * per_layer, cols), *wmv,
                                   min(per_layer, 256), outs[k])
        after = [outs[k][0] for k in BIG]
    for k in BIG:
        grads[k], delta[k], new_m[k], new_v[k] = (o.reshape(wts[k].shape) for o in outs[k])

    def update(k, g):
        rows, cols, wmv = as_rows(k)
        res = _adamw("adamw_" + k, g.reshape(rows, cols), *wmv, min(rows, 1024 if cols <= 128 else 256))
        grads[k], delta[k], new_m[k], new_v[k] = (o.reshape(wts[k].shape) for o in res)

    recv_last = _exchange_wait("grads_wait_last", last, after)
    recv_small = jnp.concatenate([recv_last[0]] + [recv_l[l][len(BIG)] for l in range(DEPTH - 1)], axis=1)
    rows = DEPTH * PACK_BR
    (g_small,) = _ew("grads_small", lambda *p: (_sum_in_order(p),), rows, PACK_BR,
                     [_ri(recv_small.reshape(NDEV * rows, PACK_COLS), PACK_COLS, 0, d * DEPTH) for d in range(NDEV)], [],
                     [(PACK_COLS, F32)])
    for k, g in _unpack(g_small, wts).items():
        update(k, g)

    return (loss, dx[None], *[grads[k] for k in WEIGHTS], *[delta[k] for k in WEIGHTS],
            *[new_m[k] for k in WEIGHTS], *[new_v[k] for k in WEIGHTS])
```

```python
import functools
import math

import jax
import jax.numpy as jnp
from jax import lax
from jax.experimental import pallas as pl
from jax.experimental.pallas import tpu as pltpu

F32 = jnp.float32
BF16 = jnp.bfloat16

NDEV = 8
DEPTH = 4
SEQ = 2048
DM = 1024
NCOL = 8192
SW = 512
NGRP = 32
GCH = 16
NST = 64
NS = NGRP * NST
HD = 128
AW = 512
DILATIONS = (1, 4, 16)
ABLK = 128
ATTN_PAIR = 8
ATTN_PAIR_FWD = 8
RMS_EPS = 1e-6
LR, B1, B2, ADAM_EPS, WD, STEP = 0.001, 0.9, 0.999, 1e-08, 0.01, 10

CB_U, CB_ZS, CB_Q, CB_K, CB_V, CB_ZA = 0, 1, 2, 5, 8, 11
CB_GS, CB_GA = 6, 7

VMEM_LIMIT = 56 * 2 ** 20
EW_ROWS = 512


def _row_order(j):
    return jnp.where(j < CB_Q, 1, 0)


def _params(*sem):
    return pltpu.CompilerParams(dimension_semantics=sem, vmem_limit_bytes=VMEM_LIMIT)


def _ew(name, fn, rows, br, row_ins, bc_ins, row_outs, red_outs=()):
    n_in = len(row_ins) + len(bc_ins)
    n_ro = len(row_outs)
    steps = rows // br
    assert steps * br == rows

    def body(*refs):
        vals = fn(*[r[...] for r in refs[:n_in]])
        outs = refs[n_in:]
        for r, v in zip(outs[:n_ro], vals[:n_ro]):
            r[...] = v.astype(r.dtype)
        if red_outs:
            @pl.when(pl.program_id(0) == 0)
            def _():
                for r in outs[n_ro:]:
                    r[...] = jnp.zeros(r.shape, r.dtype)
            for r, v in zip(outs[n_ro:], vals[n_ro:]):
                r[...] += v

    in_specs = []
    for (_, w, cb, rb) in row_ins:
        in_specs.append(pl.BlockSpec((br, w), functools.partial(lambda i, cb, rb: (rb + i, cb), cb=cb, rb=rb)))
    for a in bc_ins:
        in_specs.append(pl.BlockSpec(a.shape, functools.partial(lambda i, nd: (0,) * nd, nd=a.ndim)))
    out_specs = [pl.BlockSpec((br, w), lambda i: (i, 0)) for (w, _) in row_outs]
    out_specs += [pl.BlockSpec((1, w), lambda i: (0, 0)) for w in red_outs]
    out_shape = [jax.ShapeDtypeStruct((rows, w), dt) for (w, dt) in row_outs]
    out_shape += [jax.ShapeDtypeStruct((1, w), F32) for w in red_outs]
    return pl.pallas_call(
        body, name=name, grid=(steps,), in_specs=in_specs, out_specs=out_specs, out_shape=out_shape,
        compiler_params=_params("arbitrary"),
    )(*[a for (a, _, _, _) in row_ins], *bc_ins)


def _ri(a, w=None, cb=0, rb=0):
    return (a, a.shape[1] if w is None else w, cb, rb)


_DIMS = {"nn": ((1,), (0,)), "nt": ((1,), (1,)), "tn": ((0,), (0,))}


def _after(after):
    return ([pl.BlockSpec(memory_space=pl.ANY)], [after]) if after is not None else ([], [])


def _mm(name, a, b, mode, M, N, K, bm, bn, bk, out_dtype, a_spec=None, b_spec=None, o_spec=None, out_shape=None,
        after=None):
    nk = K // bk
    assert M % bm == 0 and N % bn == 0 and nk * bk == K
    after_specs, after_args = _after(after)

    own_acc = nk > 1 and out_dtype != F32

    def body(a_ref, b_ref, *rest):
        o_ref, scratch = rest[len(after_args)], rest[len(after_args) + 1:]
        part = lax.dot_general(a_ref[...].astype(BF16), b_ref[...].astype(BF16), (_DIMS[mode], ((), ())),
                               preferred_element_type=F32)
        if nk == 1:
            o_ref[...] = part.astype(o_ref.dtype)
            return
        k = pl.program_id(2)
        acc_ref = scratch[0] if own_acc else o_ref

        @pl.when(k == 0)
        def _():
            acc_ref[...] = part

        @pl.when(k > 0)
        def _():
            acc_ref[...] += part

        if own_acc:
            @pl.when(k == nk - 1)
            def _():
                o_ref[...] = acc_ref[...].astype(o_ref.dtype)

    if a_spec is None:
        a_spec = (pl.BlockSpec((bk, bm), lambda i, j, k: (k, i)) if mode == "tn"
                  else pl.BlockSpec((bm, bk), lambda i, j, k: (i, k)))
    if b_spec is None:
        b_spec = (pl.BlockSpec((bn, bk), lambda i, j, k: (j, k)) if mode == "nt"
                  else pl.BlockSpec((bk, bn), lambda i, j, k: (k, j)))
    if o_spec is None:
        o_spec = pl.BlockSpec((bm, bn), lambda i, j, k: (i, j))
    if out_shape is None:
        out_shape = (M, N)
    return pl.pallas_call(
        body, name=name, grid=(M // bm, N // bn, nk), in_specs=[a_spec, b_spec] + after_specs, out_specs=o_spec,
        out_shape=jax.ShapeDtypeStruct(out_shape, out_dtype),
        scratch_shapes=[pltpu.VMEM((bm, bn), F32)] if own_acc else [],
        compiler_params=_params("parallel", "parallel", "arbitrary"),
    )(a, b, *after_args)


SCAN_LANES = 512
SCAN_CHUNKS = 8


def _to_chunked(a):
    return a.reshape(SCAN_CHUNKS, SEQ // SCAN_CHUNKS, -1).transpose(1, 0, 2).reshape(SEQ, -1)


def _from_chunked(a):
    return a.reshape(SEQ // SCAN_CHUNKS, SCAN_CHUNKS, -1).transpose(1, 0, 2).reshape(SEQ, -1)


def _scan_block(dr_ref, di_ref, sr_ref, si_ref, lam_r, lam_i, reverse):
    T = SEQ // SCAN_CHUNKS
    bl = lam_r.shape[1]
    assert T == 2 ** 8
    ar = jnp.broadcast_to(lam_r, (SCAN_CHUNKS, bl))
    ai = jnp.broadcast_to(lam_i, (SCAN_CHUNKS, bl))
    zero = jnp.zeros((SCAN_CHUNKS, bl), F32)

    def tile(j):
        return pl.ds(pl.multiple_of(j * SCAN_CHUNKS, SCAN_CHUNKS), SCAN_CHUNKS)

    def step(jj, carry):
        sr, si = carry
        j = T - 1 - jj if reverse else jj
        nr = ar * sr - ai * si + dr_ref[tile(j), :]
        ni = ar * si + ai * sr + di_ref[tile(j), :]
        sr_ref[tile(j), :] = nr
        si_ref[tile(j), :] = ni
        return nr, ni

    er, ei = lax.fori_loop(0, T, step, (zero, zero), unroll=8)

    pr, pi = ar[0:1], ai[0:1]
    for _ in range(8):
        pr, pi = pr * pr - pi * pi, 2.0 * pr * pi
    rows = lax.broadcasted_iota(jnp.int32, (SCAN_CHUNKS, bl), 0)
    cr, ci = zero, zero
    xr = jnp.zeros((1, bl), F32)
    xi = jnp.zeros((1, bl), F32)
    order = range(SCAN_CHUNKS - 2, -1, -1) if reverse else range(1, SCAN_CHUNKS)
    for c in order:
        src = c + 1 if reverse else c - 1
        nxr = pr * xr - pi * xi + er[src:src + 1]
        nxi = pr * xi + pi * xr + ei[src:src + 1]
        xr, xi = nxr, nxi
        cr = jnp.where(rows == c, xr, cr)
        ci = jnp.where(rows == c, xi, ci)

    def fix(jj, pw):
        pwr, pwi = pw
        j = T - 1 - jj if reverse else jj
        sr_ref[tile(j), :] = sr_ref[tile(j), :] + (pwr * cr - pwi * ci)
        si_ref[tile(j), :] = si_ref[tile(j), :] + (pwr * ci + pwi * cr)
        return pwr * ar - pwi * ai, pwr * ai + pwi * ar

    lax.fori_loop(0, T, fix, (ar, ai), unroll=8)


def _s5_forward(name, proj, wdt, cmt, lam_r, lam_i):
    bl = SCAN_LANES
    nblk = NS // bl
    cw = bl * GCH // NST
    assert cw == 128 and CB_U == 0

    def nt(a, b_):
        return lax.dot_general(a, b_, (((1,), (1,)), ((), ())), preferred_element_type=F32)

    def body(u_ref, wr_ref, wi_ref, cr_ref, ci_ref, ar_ref, ai_ref, s_ref, y_ref):
        sr_ref, si_ref = s_ref.at[0], s_ref.at[1]
        u = u_ref[...].astype(BF16)
        sr_ref[...] = nt(u, wr_ref[...])
        si_ref[...] = nt(u, wi_ref[...])
        _scan_block(sr_ref, si_ref, sr_ref, si_ref, ar_ref[...], ai_ref[...], False)
        y_ref[...] = nt(sr_ref[...].astype(BF16), cr_ref[...]) + nt(si_ref[...].astype(BF16), ci_ref[...])

    return pl.pallas_call(
        body, name=name, grid=(nblk,),
        in_specs=[pl.BlockSpec((SEQ, cw), lambda i: (0, i)),
                  pl.BlockSpec((bl, cw), lambda i: (i, i)), pl.BlockSpec((bl, cw), lambda i: (nblk + i, i)),
                  pl.BlockSpec((cw, bl), lambda i: (i, i)), pl.BlockSpec((cw, bl), lambda i: (SW // cw + i, i)),
                  pl.BlockSpec((1, bl), lambda i: (0, i)), pl.BlockSpec((1, bl), lambda i: (0, i))],
        out_specs=[pl.BlockSpec((2, SEQ, bl), lambda i: (0, 0, i)), pl.BlockSpec((SEQ, cw), lambda i: (0, i))],
        out_shape=[jax.ShapeDtypeStruct((2, SEQ, NS), F32), jax.ShapeDtypeStruct((SEQ, SW), F32)],
        compiler_params=_params("arbitrary"),
    )(proj, wdt, wdt, cmt, cmt, lam_r, lam_i)


S5_BWD_LANES = 512


def _s5_backward(name, dy0, proj, s, wdt, cmt, lam_r, lam_i):
    bl = S5_BWD_LANES
    nblk = NS // bl
    cw = 128
    per = cw // (bl * GCH // NST)
    assert per >= 1 and CB_U == 0

    def tn(a, b_):
        return lax.dot_general(a, b_, (((0,), (0,)), ((), ())), preferred_element_type=F32)

    def prev(s_ref):
        last = pltpu.roll(s_ref[SEQ - SCAN_CHUNKS:SEQ, :], 1, 0)
        first = jnp.where(lax.broadcasted_iota(jnp.int32, (SCAN_CHUNKS, bl), 0) > 0, last, 0.0)
        return jnp.concatenate([first, s_ref[0:SEQ - SCAN_CHUNKS, :]], axis=0)

    def body(dy_ref, u_ref, s_ref, cr_ref, ci_ref, wr_ref, wi_ref, lr_ref, li_ref,
             dlr_ref, dli_ref, dwdt_ref, dcmt_ref, du_ref, ar_ref, ai_ref):
        dy = dy_ref[...]
        ar_ref[...] = jnp.dot(dy, cr_ref[...], preferred_element_type=F32)
        ai_ref[...] = jnp.dot(dy, ci_ref[...], preferred_element_type=F32)
        _scan_block(ar_ref, ai_ref, ar_ref, ai_ref, lr_ref[...], -li_ref[...], True)
        a_r, a_i = ar_ref[...], ai_ref[...]
        sr_ref, si_ref = s_ref.at[0], s_ref.at[1]
        spr, spi = prev(sr_ref), prev(si_ref)
        dlr_ref[...] = jnp.sum(a_r * spr + a_i * spi, axis=0, keepdims=True)
        dli_ref[...] = jnp.sum(a_i * spr - a_r * spi, axis=0, keepdims=True)
        a_rb, a_ib = a_r.astype(BF16), a_i.astype(BF16)
        u = u_ref[...].astype(BF16)
        dwdt_ref[0] = tn(a_rb, u)
        dwdt_ref[1] = tn(a_ib, u)
        dcmt_ref[0] = tn(dy, sr_ref[...].astype(BF16))
        dcmt_ref[1] = tn(dy, si_ref[...].astype(BF16))
        part = (jnp.dot(a_rb, wr_ref[...], preferred_element_type=F32)
                + jnp.dot(a_ib, wi_ref[...], preferred_element_type=F32))

        @pl.when(pl.program_id(0) % per == 0)
        def _():
            du_ref[...] = part

        @pl.when(pl.program_id(0) % per > 0)
        def _():
            du_ref[...] += part

    lam_spec = pl.BlockSpec((1, bl), lambda i: (0, i))
    return pl.pallas_call(
        body, name=name, grid=(nblk,),
        in_specs=[pl.BlockSpec((SEQ, cw), lambda i: (0, i // per)), pl.BlockSpec((SEQ, cw), lambda i: (0, i // per)),
                  pl.BlockSpec((2, SEQ, bl), lambda i: (0, 0, i)),
                  pl.BlockSpec((cw, bl), lambda i: (i // per, i)),
                  pl.BlockSpec((cw, bl), lambda i: (SW // cw + i // per, i)),
                  pl.BlockSpec((bl, cw), lambda i: (i, i // per)), pl.BlockSpec((bl, cw), lambda i: (nblk + i, i // per)),
                  lam_spec, lam_spec],
        out_specs=[lam_spec, lam_spec, pl.BlockSpec((2, bl, cw), lambda i: (0, i, i // per)),
                   pl.BlockSpec((2, cw, bl), lambda i: (0, i // per, i)),
                   pl.BlockSpec((SEQ, cw), lambda i: (0, i // per))],
        out_shape=[jax.ShapeDtypeStruct((1, NS), F32), jax.ShapeDtypeStruct((1, NS), F32),
                   jax.ShapeDtypeStruct((2, NS, SW), F32), jax.ShapeDtypeStruct((2, SW, NS), F32),
                   jax.ShapeDtypeStruct((SEQ, SW), F32)],
        scratch_shapes=[pltpu.VMEM((SEQ, bl), F32)] * 2,
        compiler_params=_params("arbitrary"),
    )(dy0, proj, s, cmt, cmt, wdt, wdt, lam_r, lam_i)


def _scores(qb, kb, prev):
    s = lax.dot_general(qb, kb, (((1,), (1,)), ((), ())), preferred_element_type=F32) * (HD ** -0.5)
    row = lax.broadcasted_iota(jnp.int32, (ABLK, ABLK), 0)
    col = lax.broadcasted_iota(jnp.int32, (ABLK, ABLK), 1)
    return jnp.where((col >= row) if prev else (col <= row), s, -1e30)


def _block_rows(dil, r, b):
    if dil == 1:
        return pl.ds(pl.multiple_of(b * ABLK, ABLK), ABLK)
    return pl.ds(r + dil * ABLK * b, ABLK, stride=dil)


def _group_blocks(dil):
    nb = SEQ // dil // ABLK
    shift = nb.bit_length() - 1
    return nb, (lambda idx: (idx >> shift, idx & (nb - 1)))


def _qkv_specs(j_of):
    return [pl.BlockSpec((SEQ, HD), functools.partial(lambda j, c: (0, c + j_of(j)), c=(cb + g) * 4))
            for g in range(3) for cb in (CB_Q, CB_K, CB_V)]


def _attention_fwd(name, proj):
    def body(*refs):
        qkv, z_ref = refs[:9], refs[9]
        y_ref, ya_ref, l_ref = refs[10:13]
        accs, maxs, dens = refs[13:16], refs[16:19], refs[19:22]
        for g, dil in enumerate(DILATIONS):
            q_ref, k_ref, v_ref = qkv[3 * g:3 * g + 3]
            nb, where = _group_blocks(dil)

            def step(t, c, g=g, dil=dil, nb=nb, where=where, q_ref=q_ref, k_ref=k_ref, v_ref=v_ref):
                two = range(ATTN_PAIR_FWD)
                rb = [where(t + i * (SEQ // ABLK // ATTN_PAIR_FWD)) for i in two]
                rows = [_block_rows(dil, r, b) for r, b in rb]
                qb = [q_ref[rows[i], :].astype(BF16) for i in two]
                s_c = [_scores(qb[i], k_ref[rows[i], :].astype(BF16), False) for i in two]
                if nb > 1:
                    prev = [_block_rows(dil, r, jnp.maximum(b - 1, 0)) for r, b in rb]
                    s_p = [jnp.where(rb[i][1] > 0, _scores(qb[i], k_ref[prev[i], :].astype(BF16), True), -1e30)
                           for i in two]
                m = [jnp.max(s_c[i], axis=-1, keepdims=True) for i in two]
                if nb > 1:
                    m = [jnp.maximum(m[i], jnp.max(s_p[i], axis=-1, keepdims=True)) for i in two]
                p_c = [jnp.exp(s_c[i] - m[i]) for i in two]
                den = [jnp.sum(p_c[i], axis=-1, keepdims=True) for i in two]
                acc = [jnp.dot(p_c[i].astype(BF16), v_ref[rows[i], :].astype(BF16), preferred_element_type=F32)
                       for i in two]
                if nb > 1:
                    p_p = [jnp.exp(s_p[i] - m[i]) for i in two]
                    den = [den[i] + jnp.sum(p_p[i], axis=-1, keepdims=True) for i in two]
                    acc = [acc[i] + jnp.dot(p_p[i].astype(BF16), v_ref[prev[i], :].astype(BF16),
                                            preferred_element_type=F32) for i in two]
                for i in two:
                    accs[g][rows[i], :] = acc[i]
                    maxs[g][rows[i], :] = jnp.broadcast_to(m[i], (ABLK, HD))
                    dens[g][rows[i], :] = jnp.broadcast_to(den[i], (ABLK, HD))
                return c

            lax.fori_loop(0, SEQ // ABLK // ATTN_PAIR_FWD, step, 0)
        top = jnp.maximum(jnp.maximum(maxs[0][...], maxs[1][...]), maxs[2][...])
        den = jnp.zeros((SEQ, HD), F32)
        y = jnp.zeros((SEQ, HD), F32)
        for g in range(3):
            wgt = jnp.exp(maxs[g][...] - top)
            den = den + wgt * dens[g][...]
            y = y + wgt * accs[g][...]
        y = y / den
        y_ref[...] = y
        ya_ref[...] = (y * _silu(z_ref[...])).astype(ya_ref.dtype)
        l_ref[...] = top + jnp.log(den)

    ospec = pl.BlockSpec((SEQ, HD), lambda j: (0, j))
    return pl.pallas_call(
        body, name=name, grid=(AW // HD,),
        in_specs=_qkv_specs(lambda j: j) + [pl.BlockSpec((SEQ, HD), lambda j: (0, CB_ZA * 4 + j))],
        out_specs=[ospec, ospec, ospec],
        out_shape=[jax.ShapeDtypeStruct((SEQ, AW), F32), jax.ShapeDtypeStruct((SEQ, AW), BF16),
                   jax.ShapeDtypeStruct((SEQ, AW), F32)],
        scratch_shapes=[pltpu.VMEM((SEQ, HD), F32)] * 9,
        compiler_params=_params("parallel"),
    )(*([proj] * 10))


def _attention_bwd(name, proj, dya, y, lse):
    def tn(a, b_):
        return lax.dot_general(a, b_, (((0,), (0,)), ((), ())), preferred_element_type=F32)

    def nt(a, b_):
        return lax.dot_general(a, b_, (((1,), (1,)), ((), ())), preferred_element_type=F32)

    def body(*refs):
        qkv, z_ref, dya_ref, y_ref, l_ref = refs[:9], refs[9], refs[10], refs[11], refs[12]
        outs, dza_ref = refs[13:22], refs[22]
        dy_s, dsum_s, dq_s, dk_own, dv_own, dk_prev, dv_prev = refs[23:]
        _, vjp = jax.vjp(lambda y_, z_: y_ * _silu(z_), y_ref[...], z_ref[...])
        dy, dz = vjp(dya_ref[...])
        dza_ref[...] = dz.astype(dza_ref.dtype)
        dy_s[...] = dy
        dsum_s[...] = jnp.broadcast_to(jnp.sum(dy * y_ref[...], axis=-1, keepdims=True), (SEQ, HD))
        for g, dil in enumerate(DILATIONS):
            q_ref, k_ref, v_ref = qkv[3 * g:3 * g + 3]
            nb, where = _group_blocks(dil)
            if nb > 1:
                dk_prev[...] = jnp.zeros(dk_prev.shape, F32)
                dv_prev[...] = jnp.zeros(dv_prev.shape, F32)

            def step(t, c, dil=dil, nb=nb, where=where, q_ref=q_ref, k_ref=k_ref, v_ref=v_ref):
                rb = [where(t + i * (SEQ // ABLK // ATTN_PAIR)) for i in range(ATTN_PAIR)]
                sides = []
                for r, b in rb:
                    rows = _block_rows(dil, r, b)
                    own = dict(b=b, qrows=rows, krows=rows, prev=False, dk=dk_own, dv=dv_own,
                               q=q_ref[rows, :].astype(BF16), dy=dy_s[rows, :].astype(BF16))
                    sides.append(own)
                    if nb > 1:
                        sides.append(dict(own, krows=_block_rows(dil, r, jnp.maximum(b - 1, 0)), prev=True,
                                          dk=dk_prev, dv=dv_prev))
                for s_ in sides:
                    s_["k"] = k_ref[s_["krows"], :].astype(BF16)
                    s_["v"] = v_ref[s_["krows"], :].astype(BF16)
                for s_ in sides:
                    sc = _scores(s_["q"], s_["k"], s_["prev"])
                    s_["s"] = jnp.where(s_["b"] > 0, sc, -1e30) if s_["prev"] else sc
                    s_["dp"] = nt(s_["dy"], s_["v"])
                for s_ in sides:
                    p = jnp.exp(s_["s"] - l_ref[s_["qrows"], :])
                    s_["p"] = p.astype(BF16)
                    s_["ds"] = (p * (s_["dp"] - dsum_s[s_["qrows"], :]) * (HD ** -0.5)).astype(BF16)
                for s_ in sides:
                    s_["dk"][s_["krows"], :] = tn(s_["ds"], s_["q"])
                    s_["dv"][s_["krows"], :] = tn(s_["p"], s_["dy"])
                    s_["dq"] = jnp.dot(s_["ds"], s_["k"], preferred_element_type=F32)
                per = len(sides) // ATTN_PAIR
                for i in range(ATTN_PAIR):
                    dq = sides[i * per]["dq"]
                    if per > 1:
                        dq = dq + sides[i * per + 1]["dq"]
                    dq_s[sides[i * per]["qrows"], :] = dq
                return c

            lax.fori_loop(0, SEQ // ABLK // ATTN_PAIR, step, 0)
            dq_ref, dk_ref, dv_ref = outs[3 * g:3 * g + 3]
            dq_ref[...] = dq_s[...].astype(dq_ref.dtype)
            if nb > 1:
                dk_ref[...] = (dk_own[...] + dk_prev[...]).astype(dk_ref.dtype)
                dv_ref[...] = (dv_own[...] + dv_prev[...]).astype(dv_ref.dtype)
            else:
                dk_ref[...] = dk_own[...].astype(dk_ref.dtype)
                dv_ref[...] = dv_own[...].astype(dv_ref.dtype)

    ospec = pl.BlockSpec((SEQ, HD), lambda j: (0, j))
    outs = pl.pallas_call(
        body, name=name, grid=(AW // HD,),
        in_specs=_qkv_specs(lambda j: j) + [pl.BlockSpec((SEQ, HD), lambda j: (0, CB_ZA * 4 + j))] + [ospec] * 3,
        out_specs=[ospec] * 10, out_shape=[jax.ShapeDtypeStruct((SEQ, AW), BF16)] * 10,
        scratch_shapes=[pltpu.VMEM((SEQ, HD), F32)] * 7,
        compiler_params=_params("parallel"),
    )(*([proj] * 10), dya, y, lse)
    return outs[:9], outs[9]


def _rms(x, g):
    return x * lax.rsqrt(jnp.mean(x * x, axis=-1, keepdims=True) + RMS_EPS) * g


def _sig(x):
    return 1.0 / (1.0 + jnp.exp(-x))


def _silu(x):
    return x * _sig(x)


def _gelu(x):
    return 0.5 * x * (1.0 + jnp.tanh(math.sqrt(2.0 / math.pi) * (x + 0.044715 * (x * x * x))))


def _y1_fn(y0p, u, dskip):
    return _gelu(y0p + dskip * u)


def _ys_fn(y1, t, z, bglu):
    return y1 * _sig(t + bglu) * _silu(z)


def _merge_fn(ms, ma, gs, ga):
    return _sig(gs) * ms.astype(F32) + _sig(ga) * ma.astype(F32)


def _colsum(v):
    return jnp.sum(v, axis=0, keepdims=True)


def _lam_fn(lre, lim, ldt):
    a = jnp.minimum(lre, -1e-4)
    dt = jnp.exp(ldt)
    mag = jnp.exp(a * dt)
    ar = mag * jnp.cos(lim * dt)
    ai = mag * jnp.sin(lim * dt)
    den = a * a + lim * lim
    cr = ((ar - 1.0) * a + ai * lim) / den
    ci = (ai * a - (ar - 1.0) * lim) / den
    return ar, ai, cr, ci


def _bbar_fn(cr, ci, bre, bim):
    return cr * bre - ci * bim, cr * bim + ci * bre


def _same_group(rows, a, cols, b):
    r = lax.broadcasted_iota(jnp.int32, (rows, cols), 0) >> (a.bit_length() - 1)
    c = lax.broadcasted_iota(jnp.int32, (rows, cols), 1) >> (b.bit_length() - 1)
    return r == c


def _expand(name, blocks, signs, a, b, dtype, after=None):
    rows, cols = NGRP * a, NGRP * b
    n = len(blocks)
    assert a & (a - 1) == 0 and b & (b - 1) == 0
    after_specs, after_args = _after(after)

    def body(*refs):
        o_ref = refs[-1]
        tile = (lax.broadcasted_iota(jnp.int32, (b, cols), 1) & (b - 1)
                == lax.broadcasted_iota(jnp.int32, (b, cols), 0)).astype(F32)
        keep = _same_group(rows, a, cols, b)
        for i, (ref, sign) in enumerate(zip(refs[:n], signs)):
            spread = jnp.dot(ref[...], tile, preferred_element_type=F32, precision=lax.Precision.HIGHEST)
            o_ref[i * rows:(i + 1) * rows, :] = jnp.where(keep, sign * spread, 0.0).astype(o_ref.dtype)

    return pl.pallas_call(body, name=name, grid=(1,),
                          in_specs=[pl.BlockSpec((rows, b), lambda i: (0, 0))] * n + after_specs,
                          out_specs=pl.BlockSpec((n * rows, cols), lambda i: (0, 0)),
                          out_shape=jax.ShapeDtypeStruct((n * rows, cols), dtype),
                          compiler_params=_params("arbitrary"))(*blocks, *after_args)


def _extract(name, m, a, b, ats, after=None):
    rows, cols = NGRP * a, NGRP * b
    n = len(ats)
    assert a & (a - 1) == 0 and b & (b - 1) == 0
    after_specs, after_args = _after(after)

    def body(*refs):
        tile = (lax.broadcasted_iota(jnp.int32, (cols, b), 0) & (b - 1)
                == lax.broadcasted_iota(jnp.int32, (cols, b), 1)).astype(F32)
        keep = _same_group(rows, a, cols, b)
        for m_ref, o_ref in zip(refs[:n], refs[-n:]):
            kept = jnp.where(keep, m_ref[...], 0.0)
            o_ref[...] = jnp.dot(kept, tile, preferred_element_type=F32, precision=lax.Precision.HIGHEST)

    return pl.pallas_call(
        body, name=name, grid=(1,),
        in_specs=[pl.BlockSpec((rows, cols), functools.partial(lambda i, at: at, at=at)) for at in ats] + after_specs,
        out_specs=[pl.BlockSpec((rows, b), lambda i: (0, 0))] * n,
        out_shape=[jax.ShapeDtypeStruct((rows, b), F32)] * n,
        compiler_params=_params("arbitrary"))(*([m] * n), *after_args)


def _s5_prepare(l, sp, after):
    tag = f"l{l}_"
    ar, ai, cr, ci = _ew(tag + "lam", _lam_fn, NGRP, NGRP,
                         [_ri(sp["lambda_re"]), _ri(sp["lambda_im"]),
                          _ri(sp["log_dt"].reshape(NGRP, 1) + after[0, 0])], [], [(NST, F32)] * 4)
    bre = sp["b_re"].reshape(NS, GCH)
    bim = sp["b_im"].reshape(NS, GCH)
    bbr, bbi = _ew(tag + "bbar", _bbar_fn, NS, NS, [_ri(cr.reshape(NS, 1)), _ri(ci.reshape(NS, 1)), _ri(bre), _ri(bim)],
                   [], [(GCH, F32)] * 2)
    wdt = _expand(tag + "wdt", [bbr, bbi], [1.0, 1.0], NST, GCH, BF16)
    cmt = _expand(tag + "cmt", [sp["c_re"].reshape(SW, NST), sp["c_im"].reshape(SW, NST)], [1.0, -1.0], GCH, NST, BF16,
                  after=after)
    return dict(ar=ar, ai=ai, cr=cr, ci=ci, wdt=wdt, cmt=cmt)


def _layer_head(l, x, sp):
    g1 = sp["pre_norm_g"].reshape(1, DM)
    (h,) = _ew(f"l{l}_rms1", lambda x_, g: (_rms(x_, g),), SEQ, EW_ROWS, [_ri(x)], [g1], [(DM, BF16)])
    return jnp.stack([h, _to_chunked(h)])


def _layer_fwd(l, x, hv, w, sp, prep, after):
    tag = f"l{l}_"
    win = w["w_in"]
    proj = _mm(tag + "proj", hv, win, "nn", SEQ, NCOL, DM, SEQ, 512, 1024, F32,
               a_spec=pl.BlockSpec((None, SEQ, 1024), lambda i, j, k: (_row_order(j), 0, 0)),
               b_spec=pl.BlockSpec((None, 1024, 512), lambda i, j, k: (j // 2, 0, j % 2)), after=after)

    ar, ai, cr, ci, wdt, cmt = (prep[k] for k in ("ar", "ai", "cr", "ci", "wdt", "cmt"))
    s, y0p = _s5_forward(tag + "s5", proj, wdt, cmt, ar.reshape(1, NS), ai.reshape(1, NS))
    dskip = sp["d_skip"].reshape(1, SW)
    (y1,) = _ew(tag + "y1", lambda a, u, d: (_y1_fn(a, u, d),), SEQ, EW_ROWS,
                [_ri(y0p), _ri(proj, SW, CB_U)], [dskip], [(SW, F32)])
    t = _mm(tag + "glu", y1, w["w_glu"], "nn", SEQ, SW, SW, 1024, 512, 512, F32)
    bglu = sp["b_glu"].reshape(1, SW)
    (ys_c,) = _ew(tag + "ys", lambda y1_, t_, z, b_: (_ys_fn(y1_, t_, z, b_),), SEQ, EW_ROWS,
                  [_ri(y1), _ri(t), _ri(proj, SW, CB_ZS)], [bglu], [(SW, BF16)])
    ys = _from_chunked(ys_c)

    ypre, ya, lse = _attention_fwd(tag + "attn", proj)

    ms = _mm(tag + "branch_s", ys, w["w_branch_s"], "nn", SEQ, DM, SW, 1024, 1024, 512, BF16)
    ma = _mm(tag + "branch_a", ya, w["w_branch_a"], "nn", SEQ, DM, AW, 1024, 1024, 512, BF16)
    (merged,) = _ew(tag + "merge", lambda a, b_, c, d: (_merge_fn(a, b_, c, d),), SEQ, EW_ROWS,
                    [_ri(ms), _ri(ma), _ri(proj, DM, CB_GS), _ri(proj, DM, CB_GA)], [], [(DM, BF16)])
    out = _mm(tag + "out", merged, w["w_out"], "nn", SEQ, DM, DM, 1024, 1024, 1024, F32)
    g2 = sp["post_norm_g"].reshape(1, DM)
    (x_new,) = _ew(tag + "post", lambda x_, o, g: (x_ + _rms(o, g),), SEQ, EW_ROWS, [_ri(x), _ri(out)], [g2], [(DM, F32)])
    res = dict(x=x, hv=hv, proj=proj, ar=ar, ai=ai, cr=cr, ci=ci, wdt=wdt, cmt=cmt, s=s, y0p=y0p,
               y1=y1, t=t, ys=ys, ya=ya, ypre=ypre, lse=lse, ms=ms, ma=ma, merged=merged, out=out)
    return x_new, res


def _layer_bwd(l, dxn, r, w, sp, big_done, after=None):
    tag = f"l{l}b_"
    proj = r["proj"]
    g1 = sp["pre_norm_g"].reshape(1, DM)
    g2 = sp["post_norm_g"].reshape(1, DM)
    dskip = sp["d_skip"].reshape(1, SW)
    bglu = sp["b_glu"].reshape(1, SW)

    def post_b(d, o, g):
        _, vjp = jax.vjp(_rms, o, g)
        do, dg = vjp(d)
        return do, dg

    d_out, dg2 = _ew(tag + "post", post_b, SEQ, EW_ROWS, [_ri(dxn), _ri(r["out"])], [g2], [(DM, BF16)], [DM])
    dw_out = _mm(tag + "dw_out", r["merged"], d_out, "tn", DM, DM, SEQ, 1024, 1024, SEQ, BF16, after=after)
    dmerged = _mm(tag + "dmerged", d_out, w["w_out"], "nt", SEQ, DM, DM, 1024, 1024, 1024, F32, after=after)

    def merge_b(d, ms, ma, gs, ga):
        _, vjp = jax.vjp(_merge_fn, ms, ma, gs, ga)
        return vjp(d)

    dms, dma, dgs, dga = _ew(tag + "merge", merge_b, SEQ, EW_ROWS,
                             [_ri(dmerged), _ri(r["ms"]), _ri(r["ma"]), _ri(proj, DM, CB_GS), _ri(proj, DM, CB_GA)],
                             [], [(DM, BF16)] * 4)
    dw_bs = _mm(tag + "dw_bs", r["ys"], dms, "tn", SW, DM, SEQ, 512, 1024, SEQ, BF16)
    dw_ba = _mm(tag + "dw_ba", r["ya"], dma, "tn", AW, DM, SEQ, 512, 1024, SEQ, BF16)
    dys = _mm(tag + "dys", dms, w["w_branch_s"], "nt", SEQ, SW, DM, 1024, 512, 1024, F32)
    dya = _mm(tag + "dya", dma, w["w_branch_a"], "nt", SEQ, AW, DM, 1024, 512, 1024, F32)

    dqkv, dza = _attention_bwd(tag + "attn", proj, dya, r["ypre"], r["lse"])

    def ys_b(d, y1, t, z, b_):
        _, vjp = jax.vjp(_ys_fn, y1, t, z, b_)
        dy1, dt, dz, _ = vjp(d)
        return dy1, dt, dz, _colsum(dt)

    dy1a, dt, dzs, dbglu = _ew(tag + "ys", ys_b, SEQ, EW_ROWS,
                               [_ri(_to_chunked(dys)), _ri(r["y1"]), _ri(r["t"]), _ri(proj, SW, CB_ZS)],
                               [bglu], [(SW, F32), (SW, BF16), (SW, BF16)], [SW])
    dw_glu = _mm(tag + "dw_glu", r["y1"], dt, "tn", SW, SW, SEQ, 512, 512, SEQ, BF16)
    dy1b = _mm(tag + "dy1b", dt, w["w_glu"], "nt", SEQ, SW, SW, 1024, 512, 512, F32)

    def y1_b(da, db, y0p, u, d_):
        _, vjp = jax.vjp(_y1_fn, y0p, u, d_)
        dy0, du, dd = vjp(da + db)
        return dy0, du, dd

    dy0, du_skip, ddskip = _ew(tag + "y1", y1_b, SEQ, EW_ROWS,
                               [_ri(dy1a), _ri(dy1b), _ri(r["y0p"]), _ri(proj, SW, CB_U)], [dskip],
                               [(SW, BF16), (SW, F32)], [SW])
    dlr, dli, dwdt, dcmt, du_s = _s5_backward(tag + "s5", dy0, proj, r["s"], r["wdt"], r["cmt"],
                                              r["ar"].reshape(1, NS), r["ai"].reshape(1, NS))
    (du,) = _ew(tag + "du", lambda a, c: (a + c,), SEQ, EW_ROWS, [_ri(du_s), _ri(du_skip)], [], [(SW, BF16)])

    dq, dk, dv = ([dqkv[3 * g + i] for g in range(3)] for i in range(3))
    dproj = jnp.concatenate([du, dzs, *dq, *dk, *dv, dza, dgs, dga], axis=1)
    dw_in = _mm(tag + "dw_in", r["hv"], dproj, "tn", DM, NCOL, SEQ, 1024, 512, SEQ, BF16,
                a_spec=pl.BlockSpec((None, SEQ, 1024), lambda i, j, k: (_row_order(j), 0, 0)),
                o_spec=pl.BlockSpec((None, 1024, 512), lambda i, j, k: (j // 2, 0, j % 2)), out_shape=(NDEV, DM, DM))
    tok = big_done(l, dict(w_in=dw_in, w_glu=dw_glu.reshape(NDEV, SW // NDEV, SW),
                           w_branch_s=dw_bs.reshape(SW, NDEV, DM // NDEV).transpose(1, 0, 2),
                           w_branch_a=dw_ba.reshape(AW, NDEV, DM // NDEV).transpose(1, 0, 2),
                           w_out=dw_out.reshape(NDEV, DM // NDEV, DM)))
    if tok is not None:
        g1 = g1 + tok[0, 0]

    dwdt = dwdt.reshape(2 * NS, SW)
    dcmt = dcmt.reshape(2 * SW, NS)
    dbbr, dbbi = _extract(tag + "dbb", dwdt, NST, GCH, ((0, 0), (1, 0)), after=tok)
    bre = sp["b_re"].reshape(NS, GCH)
    bim = sp["b_im"].reshape(NS, GCH)

    def bbar_b(cr, ci, br_, bi_, dr, di):
        _, vjp = jax.vjp(_bbar_fn, cr, ci, br_, bi_)
        return vjp((dr, di))

    dcr, dci, dbre, dbim = _ew(tag + "bbar", bbar_b, NS, NS,
                               [_ri(r["cr"].reshape(NS, 1)), _ri(r["ci"].reshape(NS, 1)), _ri(bre), _ri(bim),
                                _ri(dbbr), _ri(dbbi)], [], [(1, F32), (1, F32), (GCH, F32), (GCH, F32)])

    def lam_b(lre, lim, ldt, dar, dai, dcr_, dci_):
        _, vjp = jax.vjp(_lam_fn, lre, lim, ldt)
        return vjp((dar, dai, dcr_, dci_))

    dlre, dlim, dldt = _ew(tag + "lam", lam_b, NGRP, NGRP,
                           [_ri(sp["lambda_re"]), _ri(sp["lambda_im"]), _ri(sp["log_dt"].reshape(NGRP, 1)),
                            _ri(dlr.reshape(NGRP, NST)), _ri(dli.reshape(NGRP, NST)),
                            _ri(dcr.reshape(NGRP, NST)), _ri(dci.reshape(NGRP, NST))], [],
                           [(NST, F32), (NST, F32), (1, F32)])
    dc_re, dc_im = _extract(tag + "dc", dcmt, GCH, NST, ((0, 0), (1, 0)), after=tok)
    dc_re, dc_im = dc_re.reshape(NGRP, GCH, NST), -dc_im.reshape(NGRP, GCH, NST)

    dh_time = _mm(tag + "dh_time", dproj, w["w_in"], "nt", SEQ, DM, NCOL - DM, SEQ, 1024, 1024, F32,
                  a_spec=pl.BlockSpec((SEQ, 1024), lambda i, j, k: (0, 1 + k)),
                  b_spec=pl.BlockSpec((None, 1024, 1024), lambda i, j, k: (1 + k, 0, 0)), after=tok)
    dh_chunked = _mm(tag + "dh_chunked", dproj, w["w_in"], "nt", SEQ, DM, DM, 512, 1024, 1024, F32,
                     a_spec=pl.BlockSpec((512, 1024), lambda i, j, k: (i, 0)),
                     b_spec=pl.BlockSpec((None, 1024, 1024), lambda i, j, k: (0, 0, 0)), after=tok)
    dh = [dh_time, _from_chunked(dh_chunked)]

    def pre_b(d, dh0, dh1, x_, g):
        _, vjp = jax.vjp(_rms, x_, g)
        dx_, dg = vjp(dh0 + dh1)
        return d + dx_, dg

    dx, dg1 = _ew(tag + "pre", pre_b, SEQ, EW_ROWS, [_ri(dxn)] + [_ri(t_) for t_ in dh] + [_ri(r["x"])], [g1],
                  [(DM, F32)], [DM])

    small = dict(pre_norm_g=dg1.reshape(DM), lambda_re=dlre, lambda_im=dlim, log_dt=dldt.reshape(NGRP),
                 b_re=dbre.reshape(NGRP, NST, GCH), b_im=dbim.reshape(NGRP, NST, GCH), c_re=dc_re, c_im=dc_im,
                 d_skip=ddskip.reshape(SW), b_glu=dbglu.reshape(SW), post_norm_g=dg2.reshape(DM))
    return dx, small


_HBM = pl.BlockSpec(memory_space=pltpu.HBM)
_SEM = pl.BlockSpec(memory_space=pltpu.SEMAPHORE)
_EFFECT = pltpu.SideEffectType.DATAFLOW_SIDE_EFFECTING


def _remote_copies(srcs, dsts, send_sems, recv_sems, gather):
    x, y, c = lax.axis_index("x"), lax.axis_index("y"), lax.axis_index("c")
    me = 4 * x + 2 * y + c
    copies = []
    for i in range(len(srcs)):
        for k in range(1, NDEV):
            peer = (x ^ (k >> 2), y ^ ((k >> 1) & 1), c ^ (k & 1))
            src = srcs[i] if gather[i] else srcs[i].at[me ^ k]
            copies.append(pltpu.make_async_remote_copy(
                src_ref=src, dst_ref=dsts[i].at[me], send_sem=send_sems[i], recv_sem=recv_sems[i],
                device_id=peer, device_id_type=pl.DeviceIdType.MESH))
    return copies


def _all_seven(dst, send_sem, recv_sem):
    seven = dst.at[pl.ds(0, NDEV - 1)]
    me = (lax.axis_index("x"), lax.axis_index("y"), lax.axis_index("c"))
    return pltpu.make_async_remote_copy(src_ref=seven, dst_ref=seven, send_sem=send_sem, recv_sem=recv_sem,
                                        device_id=me, device_id_type=pl.DeviceIdType.MESH)


def _own_slabs(name, arrs, gather, after):
    n = len(arrs)
    me = (4 * lax.axis_index("x") + 2 * lax.axis_index("y") + lax.axis_index("c")).astype(jnp.int32).reshape(1)

    def body(me_ref, *refs):
        for src, dst in zip(refs[:n], refs[n + 1:]):
            dst[...] = src[...]

    def zeros(k):
        return (0,) * k

    in_specs, out_specs, out_shape = [], [], []
    for a, g in zip(arrs, gather):
        slab = a.shape if g else a.shape[1:]
        nd = len(slab)
        if g:
            in_specs.append(pl.BlockSpec(slab, functools.partial(lambda i, me_ref, nd: zeros(nd), nd=nd)))
        else:
            in_specs.append(pl.BlockSpec((None,) + slab, functools.partial(lambda i, me_ref, nd: (me_ref[0],) + zeros(nd), nd=nd)))
        out_specs.append(pl.BlockSpec((None,) + slab, functools.partial(lambda i, me_ref, nd: (me_ref[0],) + zeros(nd), nd=nd)))
        out_shape.append(jax.ShapeDtypeStruct((NDEV,) + slab, a.dtype))
    in_specs.append(pl.BlockSpec(memory_space=pl.ANY))
    return pl.pallas_call(
        body, name=name, out_shape=out_shape,
        grid_spec=pltpu.PrefetchScalarGridSpec(num_scalar_prefetch=1, grid=(1,), in_specs=in_specs, out_specs=out_specs),
        compiler_params=_params("arbitrary"),
    )(me, *arrs, after)


def _exchange_start(name, arrs, gather, after):
    n = len(arrs)
    lands = _own_slabs(name + "_own", arrs, gather, after)

    def body(*refs):
        srcs, dsts = refs[:n], refs[n:2 * n]
        send_sems, recv_sems = refs[2 * n:3 * n], refs[3 * n:4 * n]
        token = refs[-1]
        for cp in _remote_copies(srcs, dsts, send_sems, recv_sems, gather):
            cp.start()
        token[...] = jnp.zeros(token.shape, token.dtype)

    thru = [pltpu.HBM(a.shape, a.dtype) for a in list(arrs) + list(lands)]
    outs = pl.pallas_call(
        body, name=name,
        out_shape=(*[pltpu.SemaphoreType.DMA(())] * (2 * n), *thru, jax.ShapeDtypeStruct((8, 128), F32)),
        in_specs=[_HBM] * (2 * n),
        out_specs=(*[_SEM] * (2 * n), *[_HBM] * (2 * n), pl.BlockSpec(memory_space=pltpu.VMEM)),
        input_output_aliases={i: 2 * n + i for i in range(2 * n)},
        compiler_params=pltpu.CompilerParams(has_side_effects=_EFFECT),
    )(*[pltpu.with_memory_space_constraint(a, pltpu.HBM) for a in list(arrs) + list(lands)])
    return dict(send=outs[:n], recv=outs[n:2 * n], srcs=outs[2 * n:3 * n], lands=outs[3 * n:4 * n], token=outs[-1],
                gather=gather)


def _exchange_wait(name, started, after):
    n = len(started["srcs"])
    after = list(after)

    def body(*refs):
        dsts = refs[n:2 * n]
        send_sems, recv_sems = refs[2 * n:3 * n], refs[3 * n:4 * n]
        for i in range(n):
            cp = _all_seven(dsts[i], send_sems[i], recv_sems[i])
            cp.wait_send()
            cp.wait_recv()

    bufs = list(started["srcs"]) + list(started["lands"])
    outs = pl.pallas_call(
        body, name=name, out_shape=tuple(pltpu.HBM(a.shape, a.dtype) for a in bufs),
        in_specs=[_HBM] * (2 * n) + [_SEM] * (2 * n) + [pl.BlockSpec(memory_space=pl.ANY)] * len(after),
        out_specs=(_HBM,) * (2 * n), input_output_aliases={i: i for i in range(2 * n)},
        compiler_params=pltpu.CompilerParams(has_side_effects=_EFFECT),
    )(*bufs, *started["send"], *started["recv"], *after)
    return outs[n:]


def _sum_in_order(parts):
    g = parts[0].astype(F32)
    for p in parts[1:]:
        g = g + p.astype(F32)
    return g


def _adam_update(g, w_, m_, v_):
    m2 = B1 * m_ + (1.0 - B1) * g
    v2 = B2 * v_ + (1.0 - B2) * (g * g)
    m_hat = m2 / (1.0 - B1 ** STEP)
    v_hat = v2 / (1.0 - B2 ** STEP)
    delta = -LR * (m_hat / (jnp.sqrt(v_hat) + ADAM_EPS) + WD * w_)
    return g, delta, m2, v2


def _adamw(name, g, w, m, v, br):
    rows, cols = w.shape
    return _ew(name, _adam_update, rows, br, [_ri(g), _ri(w), _ri(m), _ri(v)], [], [(cols, F32)] * 4)


def _adamw_layer(name, l, parts, w, m, v, br, outs):
    rows, cols = w.shape
    nb = rows // DEPTH // br
    assert nb * br * DEPTH == rows

    def body(*refs):
        vals = _adam_update(_sum_in_order([r[...] for r in refs[:NDEV]]), *[r[...] for r in refs[NDEV:NDEV + 3]])
        for r, val in zip(refs[-4:], vals):
            r[...] = val

    mine = pl.BlockSpec((br, cols), lambda i: (l * nb + i, 0))
    in_specs = [pl.BlockSpec((br, cols), functools.partial(lambda i, d: (d * nb + i, 0), d=d)) for d in range(NDEV)]
    in_specs += [mine] * 3 + [pl.BlockSpec(memory_space=pl.ANY)] * len(outs)
    return pl.pallas_call(
        body, name=f"{name}{l}", grid=(nb,), in_specs=in_specs, out_specs=[mine] * 4,
        out_shape=[jax.ShapeDtypeStruct((rows, cols), F32)] * 4,
        input_output_aliases={NDEV + 3 + q: q for q in range(len(outs))},
        compiler_params=_params("arbitrary"),
    )(*([parts] * NDEV), w, m, v, *outs)


SMALL = ("pre_norm_g", "lambda_re", "lambda_im", "log_dt", "b_re", "b_im", "c_re", "c_im", "d_skip", "b_glu",
         "post_norm_g")
BIG = ("w_in", "w_glu", "w_branch_s", "w_branch_a", "w_out")
WEIGHTS = ("pre_norm_g", "w_in", "lambda_re", "lambda_im", "log_dt", "b_re", "b_im", "c_re", "c_im", "d_skip",
           "w_glu", "b_glu", "w_branch_s", "w_branch_a", "w_out", "post_norm_g")
PACK_COLS = 1024
PACK_BR = 136


def _pack_layer(d):
    pieces = [d[k].astype(F32).reshape(-1) for k in SMALL]
    used = sum(p.shape[0] for p in pieces)
    assert used <= PACK_BR * PACK_COLS
    return jnp.concatenate(pieces + [jnp.zeros((PACK_BR * PACK_COLS - used,), F32)]).reshape(PACK_BR, PACK_COLS)


def _unpack(p, like):
    flat = p.reshape(DEPTH, PACK_BR * PACK_COLS)
    out, off = {}, 0
    for k in SMALL:
        n = like[k].size // DEPTH
        out[k] = flat[:, off:off + n].reshape(like[k].shape)
        off += n
    return out


def _local_step(x, target, small, started, weights_of, big_done, small_done, total_loss):
    preps = [_s5_prepare(l, {k: small[k][l] for k in SMALL}, started) for l in range(DEPTH)]
    res, ws = [], []
    for l in range(DEPTH):
        sp = {k: small[k][l] for k in SMALL}
        hv = _layer_head(l, x, sp)
        w_l, tok = weights_of(l, [hv] + ([p[k] for p in preps for k in ("wdt", "cmt")] if l == 0 else []))
        x, r = _layer_fwd(l, x, hv, w_l, sp, preps[l], tok)
        res.append(r)
        ws.append(w_l)

    def loss_fn(y, t):
        e = y - t
        return e * (1.0 / DM), jnp.sum(_colsum(0.5 * e * e * (1.0 / DM)), axis=1, keepdims=True)

    dx, loss = _ew("loss", loss_fn, SEQ, EW_ROWS, [_ri(x), _ri(target)], [], [(DM, F32)], [1])
    total = total_loss(loss.reshape(()))
    for l in reversed(range(DEPTH)):
        dx, sm = _layer_bwd(l, dx, res[l], ws[l], {k: small[k][l] for k in SMALL}, big_done,
                            after=total.reshape(1, 1) if l == DEPTH - 1 else None)
        small_done(l, sm)
    return total, dx


def _full_weights(gathered):
    g = gathered
    return dict(
        w_in=g["w_in"],
        w_glu=g["w_glu"].reshape(SW, SW),
        w_branch_s=g["w_branch_s"].transpose(1, 0, 2).reshape(SW, DM),
        w_branch_a=g["w_branch_a"].transpose(1, 0, 2).reshape(AW, DM),
        w_out=g["w_out"].reshape(DM, DM),
    )


def kernel(x, pre_norm_g, w_in, lambda_re, lambda_im, log_dt, b_re, b_im, c_re, c_im, d_skip, w_glu, b_glu, w_branch_s, w_branch_a, w_out, post_norm_g, loss_target, m_pre_norm_g, m_w_in, m_lambda_re, m_lambda_im, m_log_dt, m_b_re, m_b_im, m_c_re, m_c_im, m_d_skip, m_w_glu, m_b_glu, m_w_branch_s, m_w_branch_a, m_w_out, m_post_norm_g, v_pre_norm_g, v_w_in, v_lambda_re, v_lambda_im, v_log_dt, v_b_re, v_b_im, v_c_re, v_c_im, v_d_skip, v_w_glu, v_b_glu, v_w_branch_s, v_w_branch_a, v_w_out, v_post_norm_g):
    wts = dict(pre_norm_g=pre_norm_g, w_in=w_in, lambda_re=lambda_re, lambda_im=lambda_im, log_dt=log_dt, b_re=b_re,
               b_im=b_im, c_re=c_re, c_im=c_im, d_skip=d_skip, w_glu=w_glu, b_glu=b_glu, w_branch_s=w_branch_s,
               w_branch_a=w_branch_a, w_out=w_out, post_norm_g=post_norm_g)
    mom = dict(pre_norm_g=m_pre_norm_g, w_in=m_w_in, lambda_re=m_lambda_re, lambda_im=m_lambda_im, log_dt=m_log_dt,
               b_re=m_b_re, b_im=m_b_im, c_re=m_c_re, c_im=m_c_im, d_skip=m_d_skip, w_glu=m_w_glu, b_glu=m_b_glu,
               w_branch_s=m_w_branch_s, w_branch_a=m_w_branch_a, w_out=m_w_out, post_norm_g=m_post_norm_g)
    var = dict(pre_norm_g=v_pre_norm_g, w_in=v_w_in, lambda_re=v_lambda_re, lambda_im=v_lambda_im, log_dt=v_log_dt,
               b_re=v_b_re, b_im=v_b_im, c_re=v_c_re, c_im=v_c_im, d_skip=v_d_skip, w_glu=v_w_glu, b_glu=v_b_glu,
               w_branch_s=v_w_branch_s, w_branch_a=v_w_branch_a, w_out=v_w_out, post_norm_g=v_post_norm_g)

    def gather_start(l, after):
        return _exchange_start(f"gather_start{l}", [wts[k][l].astype(BF16) for k in BIG], [True] * len(BIG), after)

    gathering = {0: gather_start(0, x)}
    sending, packed = {}, {}

    def weights_of(l, after):
        gathered = _exchange_wait(f"gather_wait{l}", gathering[l], after)
        tok = None
        if l + 1 < DEPTH:
            gathering[l + 1] = gather_start(l + 1, gathered[0])
            tok = gathering[l + 1]["token"]
        return _full_weights(dict(zip(BIG, gathered))), tok

    def big_done(l, big):
        arrs, kinds = [big[k] for k in BIG], [False] * len(BIG)
        if l + 1 < DEPTH:
            arrs, kinds = arrs + [packed[l + 1]], kinds + [True]
        sending[l] = _exchange_start(f"grads_start{l}", arrs, kinds, big["w_in"])
        return sending[l]["token"]

    def small_done(l, sm):
        packed[l] = _pack_layer(sm)

    loss, dx = _local_step(x[0], loss_target[0], wts, gathering[0]["token"], weights_of, big_done, small_done,
                           lambda part: lax.psum(part, ("x", "y", "c")))
    last = _exchange_start("grads_start_last", [packed[0]], [True], dx)

    grads, delta, new_m, new_v = {}, {}, {}, {}

    def as_rows(k):
        cols = wts[k].shape[-1]
        rows = wts[k].size // cols
        return rows, cols, [t[k].reshape(rows, cols) for t in (wts, mom, var)]

    recv_l, outs, after = {}, {k: [] for k in BIG}, [dx]
    for l in reversed(range(DEPTH)):
        recv_l[l] = _exchange_wait(f"grads_wait{l}", sending[l], after)
        for i, k in enumerate(BIG):
            rows, cols, wmv = as_rows(k)
            per_layer = rows // DEPTH
            outs[k] = _adamw_layer("adamw_" + k, l, recv_l[l][i].reshape(NDEV * per_layer, cols), *wmv,
                                   min(per_layer, 256), outs[k])
        after = [outs[k][0] for k in BIG]
    for k in BIG:
        grads[k], delta[k], new_m[k], new_v[k] = (o.reshape(wts[k].shape) for o in outs[k])

    def update(k, g):
        rows, cols, wmv = as_rows(k)
        res = _adamw("adamw_" + k, g.reshape(rows, cols), *wmv, min(rows, 1024 if cols <= 128 else 256))
        grads[k], delta[k], new_m[k], new_v[k] = (o.reshape(wts[k].shape) for o in res)

    recv_last = _exchange_wait("grads_wait_last", last, after)
    recv_small = jnp.concatenate([recv_last[0]] + [recv_l[l][len(BIG)] for l in range(DEPTH - 1)], axis=1)
    rows = DEPTH * PACK_BR
    (g_small,) = _ew("grads_small", lambda *p: (_sum_in_order(p),), rows, PACK_BR,
                     [_ri(recv_small.reshape(NDEV * rows, PACK_COLS), PACK_COLS, 0, d * DEPTH) for d in range(NDEV)], [],
                     [(PACK_COLS, F32)])
    for k, g in _unpack(g_small, wts).items():
        update(k, g)

    return (loss, dx[None], *[grads[k] for k in WEIGHTS], *[delta[k] for k in WEIGHTS],
            *[new_m[k] for k in WEIGHTS], *[new_v[k] for k in WEIGHTS])
```
